```python
import jax, jax.numpy as jnp
from jax import lax
import numpy as np

D_MODEL = 1024
BATCH = 2
SEQ = 8192
DEPTH = 1

GRID_W = 64
HEAD_DIM = 64
NA_HEADS = 8
NA_KH_MAX = 8
NA_KW = 16
SWA_HEADS = 8
SWA_KV_HEADS = 2
SWA_WINDOW = 128
SWA_BLOCK = 128
ROPE_THETA = 10000.0
NA_WIDTH = NA_HEADS * HEAD_DIM
SWA_WIDTH = SWA_HEADS * HEAD_DIM
SWA_KV_WIDTH = SWA_KV_HEADS * HEAD_DIM
MIX_WIDTH = NA_WIDTH + SWA_WIDTH
QKV_WIDTH = 3 * NA_WIDTH + SWA_WIDTH + 2 * SWA_KV_WIDTH
N_GROUPS = 4
EXPERTS_PER_GROUP = 8
N_EXPERTS = N_GROUPS * EXPERTS_PER_GROUP
TOP_K_IN_GROUP = 2
EXPERT_FF = 256
N_MOD = 6
EPS = 1e-6
NEG_INF = -1e30

kernel_name = "hymba_na_swa_hmoe_encoder"


def rmsnorm(x, g):
    xf = x.astype(jnp.float32)
    y = xf * lax.rsqrt(jnp.mean(xf * xf, axis=-1, keepdims=True) + EPS)
    return (y * g.astype(jnp.float32)).astype(x.dtype)


def rope(x, pos):
    d = x.shape[-1]
    half = d // 2
    inv = ROPE_THETA ** (-jnp.arange(half, dtype=jnp.float32) * 2.0 / d)
    ang = pos.astype(jnp.float32)[:, None] * inv[None, :]
    cos = jnp.cos(ang)[None, :, None, :]
    sin = jnp.sin(ang)[None, :, None, :]
    xf = x.astype(jnp.float32)
    x1, x2 = xf[..., :half], xf[..., half:]
    return jnp.concatenate([x1 * cos - x2 * sin, x2 * cos + x1 * sin], axis=-1).astype(x.dtype)


def neighborhood_attention(q, k, v, rpb):
    b, s, h, d = q.shape
    rows = s // GRID_W
    kh = min(NA_KH_MAX, rows)
    qg = q.reshape(b, rows, GRID_W, h, d)
    kg = k.reshape(b, rows, GRID_W, h, d)
    vg = v.reshape(b, rows, GRID_W, h, d)
    col = jnp.arange(GRID_W)
    col_start = jnp.clip(col - NA_KW // 2, 0, GRID_W - NA_KW)
    col_idx = col_start[:, None] + jnp.arange(NA_KW)[None, :]
    dc = col_idx - col[:, None] + (NA_KW - 1)
    scale = d ** -0.5

    def row_block(r):
        r0 = jnp.clip(r - kh // 2, 0, rows - kh)
        k_rows = lax.dynamic_slice_in_dim(kg, r0, kh, axis=1)
        v_rows = lax.dynamic_slice_in_dim(vg, r0, kh, axis=1)
        k_win = k_rows[:, :, col_idx]
        v_win = v_rows[:, :, col_idx]
        q_row = lax.dynamic_index_in_dim(qg, r, axis=1, keepdims=False)
        dr = r0 + jnp.arange(kh) - r + (NA_KH_MAX - 1)
        bias = rpb[:, dr[:, None, None], dc[None, :, :]]
        bias = jnp.transpose(bias, (0, 2, 1, 3)).astype(jnp.float32)
        logits = jnp.einsum('bqhd,biqjhd->bhqij', q_row, k_win).astype(jnp.float32) * scale
        logits = logits + bias[None]
        p = jax.nn.softmax(logits.reshape(b, h, GRID_W, kh * NA_KW), axis=-1)
        p = p.reshape(b, h, GRID_W, kh, NA_KW).astype(v.dtype)
        return jnp.einsum('bhqij,biqjhd->bqhd', p, v_win)

    out = lax.map(row_block, jnp.arange(rows))
    return jnp.transpose(out, (1, 0, 2, 3, 4)).reshape(b, s, h * d)


def sliding_window_gqa(q, k, v, sinks):
    b, s, hq, d = q.shape
    hkv = k.shape[2]
    g = hq // hkv
    nb = s // SWA_BLOCK
    qb = q.reshape(b, nb, SWA_BLOCK, hkv, g, d)

    def band(t):
        tp = jnp.pad(t, ((0, 0), (SWA_BLOCK, SWA_BLOCK), (0, 0), (0, 0)))
        tb = tp.reshape(b, nb + 2, SWA_BLOCK, hkv, d)
        return jnp.concatenate([tb[:, :-2], tb[:, 1:-1], tb[:, 2:]], axis=2)

    kb, vb = band(k), band(v)
    q_off = jnp.arange(SWA_BLOCK)
    k_off = jnp.arange(3 * SWA_BLOCK) - SWA_BLOCK
    in_window = jnp.abs(k_off[None, :] - q_off[:, None]) <= SWA_WINDOW
    k_abs = jnp.arange(nb)[:, None] * SWA_BLOCK + k_off[None, :]
    k_valid = (k_abs >= 0) & (k_abs < s)
    mask = in_window[None] & k_valid[:, None, :]
    logits = jnp.einsum('bnqhgd,bnkhd->bnhgqk', qb, kb).astype(jnp.float32) * (d ** -0.5)
    logits = jnp.where(mask[None, :, None, None], logits, NEG_INF)
    sink = jnp.broadcast_to(sinks.astype(jnp.float32).reshape(hkv, g)[None, None, :, :, None, None],
                            logits.shape[:-1] + (1,))
    p = jax.nn.softmax(jnp.concatenate([logits, sink], axis=-1), axis=-1)[..., :-1]
    out = jnp.einsum('bnhgqk,bnkhd->bnqhgd', p.astype(v.dtype), vb)
    return out.reshape(b, s, hq * d)


def hierarchical_moe(h, w_group_router, w_expert_router, w_gate, w_up, w_down):
    b, s, d = h.shape
    n = b * s
    t = h.reshape(n, d)
    g_logits = (t @ w_group_router).astype(jnp.float32)
    g_probs = jax.nn.softmax(g_logits, axis=-1)
    g_top = jnp.argmax(g_logits, axis=-1)
    g_weight = jnp.take_along_axis(g_probs, g_top[:, None], axis=-1)
    e_logits = jnp.einsum('nd,gde->nge', t, w_expert_router).astype(jnp.float32)
    e_in = jnp.take_along_axis(e_logits, g_top[:, None, None], axis=1)[:, 0]
    top_vals, top_idx = lax.top_k(e_in, TOP_K_IN_GROUP)
    top_w = jax.nn.softmax(top_vals, axis=-1) * g_weight
    expert_id = g_top[:, None] * EXPERTS_PER_GROUP + top_idx
    gates = jnp.sum(jax.nn.one_hot(expert_id, N_EXPERTS, dtype=jnp.float32) * top_w[..., None], axis=1)
    y = jnp.zeros((n, d), jnp.float32)
    for e in range(N_EXPERTS):
        he = jax.nn.silu(t @ w_gate[e]) * (t @ w_up[e])
        y = y + gates[:, e:e + 1] * (he @ w_down[e]).astype(jnp.float32)
    return y.astype(h.dtype).reshape(b, s, d)


def setup_inputs(seed: int = 0) -> dict:
    key = jax.random.key(seed)
    ks = jax.random.split(key, 20)
    f32 = jnp.float32
    nrm = lambda k, shape, sc: jax.random.normal(k, shape, f32) * sc
    return {
        "x": nrm(ks[0], (BATCH, SEQ, D_MODEL), 1.0),
        "c": nrm(ks[1], (BATCH, D_MODEL), 1.0),
        "w_ada": nrm(ks[2], (DEPTH, D_MODEL, N_MOD * D_MODEL), 0.5 * D_MODEL ** -0.5),
        "b_ada": nrm(ks[3], (DEPTH, N_MOD * D_MODEL), 0.01),
        "g_pre_mix": 1.0 + nrm(ks[4], (DEPTH, D_MODEL), 0.02),
        "w_in": nrm(ks[5], (DEPTH, D_MODEL, QKV_WIDTH), D_MODEL ** -0.5),
        "na_rpb": nrm(ks[6], (DEPTH, NA_HEADS, 2 * NA_KH_MAX - 1, 2 * NA_KW - 1), 0.1),
        "swa_sinks": nrm(ks[7], (DEPTH, SWA_HEADS), 1.0),
        "beta_na": 1.0 + nrm(ks[8], (DEPTH, NA_WIDTH), 0.02),
        "beta_swa": 1.0 + nrm(ks[9], (DEPTH, SWA_WIDTH), 0.02),
        "w_out": nrm(ks[10], (DEPTH, MIX_WIDTH, D_MODEL), MIX_WIDTH ** -0.5),
        "g_post_mix": 1.0 + nrm(ks[11], (DEPTH, D_MODEL), 0.02),
        "g_pre_ffn": 1.0 + nrm(ks[12], (DEPTH, D_MODEL), 0.02),
        "w_group_router": nrm(ks[13], (DEPTH, D_MODEL, N_GROUPS), D_MODEL ** -0.5),
        "w_expert_router": nrm(ks[14], (DEPTH, N_GROUPS, D_MODEL, EXPERTS_PER_GROUP), D_MODEL ** -0.5),
        "w_gate": nrm(ks[15], (DEPTH, N_EXPERTS, D_MODEL, EXPERT_FF), D_MODEL ** -0.5),
        "w_up": nrm(ks[16], (DEPTH, N_EXPERTS, D_MODEL, EXPERT_FF), D_MODEL ** -0.5),
        "w_down": nrm(ks[17], (DEPTH, N_EXPERTS, EXPERT_FF, D_MODEL), EXPERT_FF ** -0.5),
        "g_post_ffn": 1.0 + nrm(ks[18], (DEPTH, D_MODEL), 0.02),
    }


def reference(x, c, w_ada, b_ada, g_pre_mix, w_in, na_rpb, swa_sinks, beta_na, beta_swa,
              w_out, g_post_mix, g_pre_ffn, w_group_router, w_expert_router, w_gate, w_up,
              w_down, g_post_ffn):
    b, s, d = x.shape
    pos = jnp.arange(s)
    o_nq, o_nk, o_nv = 0, NA_WIDTH, 2 * NA_WIDTH
    o_sq = 3 * NA_WIDTH
    o_sk = o_sq + SWA_WIDTH
    o_sv = o_sk + SWA_KV_WIDTH
    for l in range(DEPTH):
        mod = (jax.nn.silu(c) @ w_ada[l] + b_ada[l]).reshape(b, N_MOD, d)
        shift_a, scale_a, gate_a = mod[:, 0, None], mod[:, 1, None], mod[:, 2, None]
        shift_f, scale_f, gate_f = mod[:, 3, None], mod[:, 4, None], mod[:, 5, None]

        h = rmsnorm(x, g_pre_mix[l]) * (1.0 + scale_a) + shift_a
        qkv = h @ w_in[l]
        na_q = qkv[..., o_nq:o_nk].reshape(b, s, NA_HEADS, HEAD_DIM)
        na_k = qkv[..., o_nk:o_nv].reshape(b, s, NA_HEADS, HEAD_DIM)
        na_v = qkv[..., o_nv:o_sq].reshape(b, s, NA_HEADS, HEAD_DIM)
        sw_q = rope(qkv[..., o_sq:o_sk].reshape(b, s, SWA_HEADS, HEAD_DIM), pos)
        sw_k = rope(qkv[..., o_sk:o_sv].reshape(b, s, SWA_KV_HEADS, HEAD_DIM), pos)
        sw_v = qkv[..., o_sv:].reshape(b, s, SWA_KV_HEADS, HEAD_DIM)
        na_out = rmsnorm(neighborhood_attention(na_q, na_k, na_v, na_rpb[l]), beta_na[l])
        sw_out = rmsnorm(sliding_window_gqa(sw_q, sw_k, sw_v, swa_sinks[l]), beta_swa[l])
        mix = jnp.concatenate([na_out, sw_out], axis=-1) @ w_out[l]
        x = x + gate_a * rmsnorm(mix, g_post_mix[l])

        h = rmsnorm(x, g_pre_ffn[l]) * (1.0 + scale_f) + shift_f
        y = hierarchical_moe(h, w_group_router[l], w_expert_router[l], w_gate[l], w_up[l], w_down[l])
        x = x + gate_f * rmsnorm(y, g_post_ffn[l])
    return x
```

```python
import functools

import jax
import jax.numpy as jnp
from jax import lax
from jax.experimental import pallas as pl
from jax.experimental.pallas import tpu as pltpu

D_MODEL = 1024
GRID_W = 64
HEAD_DIM = 64
NA_HEADS = 8
NA_KH = 8
NA_KW = 16
SWA_HEADS = 8
SWA_KV_HEADS = 2
SWA_WINDOW = 128
SWA_BLOCK = 128
ROPE_THETA = 10000.0
NA_WIDTH = NA_HEADS * HEAD_DIM
SWA_WIDTH = SWA_HEADS * HEAD_DIM
N_GROUPS = 4
EXPERTS_PER_GROUP = 8
N_EXPERTS = N_GROUPS * EXPERTS_PER_GROUP
EXPERT_FF = 256
N_MOD = 6
EPS = 1e-6
NEG_INF = -1e30

LANES = 128
PAIRS = NA_HEADS // 2
COL_NQ, COL_NK, COL_NV, COL_SQ, COL_SK, COL_SV = 0, 4, 8, 12, 16, 18
QKV_TILES = 20
ROUTER_BASE = N_GROUPS
VMEM_LIMIT = 56 * 1024 * 1024

F32 = jnp.float32
BF16 = jnp.bfloat16


def _rms(v):
    return v * lax.rsqrt(jnp.mean(v * v, axis=-1, keepdims=True) + EPS)


def _params(*sem):
    return pltpu.CompilerParams(dimension_semantics=sem, vmem_limit_bytes=VMEM_LIMIT)


def _adaln_kernel(c_ref, w_ref, b_ref, o_ref):
    c = c_ref[...]
    a = c * jax.nn.sigmoid(c)
    o_ref[...] = jnp.dot(a, w_ref[...], precision=lax.Precision.HIGHEST,
                         preferred_element_type=F32) + b_ref[...]


def _adaln(c, w_ada, b_ada):
    batch, d = c.shape
    n = w_ada.shape[1]
    tn = 1024
    b = 8
    c = jnp.pad(c, ((0, b - batch), (0, 0)))
    return pl.pallas_call(
        _adaln_kernel,
        out_shape=jax.ShapeDtypeStruct((b, n), F32),
        grid=(n // tn,),
        in_specs=[pl.BlockSpec((b, d), lambda j: (0, 0)),
                  pl.BlockSpec((d, tn), lambda j: (0, j)),
                  pl.BlockSpec((1, tn), lambda j: (0, j))],
        out_specs=pl.BlockSpec((b, tn), lambda j: (0, j)),
        compiler_params=_params("arbitrary"),
        name="adaln",
    )(c, w_ada, b_ada.reshape(1, n))[:batch]


def _rope(v, cos, sin_signed, first_half):
    rot = jnp.where(first_half, pltpu.roll(v, LANES - HEAD_DIM // 2, 1), pltpu.roll(v, HEAD_DIM // 2, 1))
    return v * cos + rot * sin_signed


def _qkv_kernel(x_ref, g_ref, sc_ref, sh_ref, w_ref, cos_ref, sin_ref, o_ref):
    x = x_ref[0]
    h = (_rms(x) * g_ref[...]) * (1.0 + sc_ref[0]) + sh_ref[0]
    h = h.astype(BF16)
    scale = HEAD_DIM ** -0.5
    cos = cos_ref[...]
    sin = sin_ref[...]
    lane = lax.broadcasted_iota(jnp.int32, cos.shape, 1)
    first_half = (lane % HEAD_DIM) < HEAD_DIM // 2
    upper = lane >= HEAD_DIM

    def proj(col, width):
        return jnp.dot(h, w_ref[:, col * LANES:(col + width) * LANES], preferred_element_type=F32)

    def tile(v, j):
        return v[:, j * LANES:(j + 1) * LANES]

    nq, nk, nv, sq = proj(COL_NQ, 4), proj(COL_NK, 4), proj(COL_NV, 4), proj(COL_SQ, 4)
    for j in range(PAIRS):
        o_ref[0, COL_NQ + j] = (tile(nq, j) * scale).astype(BF16)
        o_ref[0, COL_NK + j] = tile(nk, j).astype(BF16)
        o_ref[0, COL_NV + j] = tile(nv, j).astype(BF16)
        o_ref[0, COL_SQ + j] = (_rope(tile(sq, j), cos, sin, first_half) * scale).astype(BF16)
    skv = proj(COL_SK, 2)
    k = _rope(tile(skv, 0), cos, sin, first_half)
    v = tile(skv, 1)
    for t, col in ((k, COL_SK), (v, COL_SV)):
        swapped = pltpu.roll(t, HEAD_DIM, 1)
        o_ref[0, col] = jnp.where(upper, swapped, t).astype(BF16)
        o_ref[0, col + 1] = jnp.where(upper, t, swapped).astype(BF16)


def _qkv(x, g, scale_a, shift_a, w_in, cos, sin):
    b, s, d = x.shape
    tm = 512
    n_in = w_in.shape[1]
    return pl.pallas_call(
        _qkv_kernel,
        out_shape=jax.ShapeDtypeStruct((b, QKV_TILES, s, LANES), BF16),
        grid=(b, s // tm),
        in_specs=[pl.BlockSpec((1, tm, d), lambda bi, i: (bi, i, 0)),
                  pl.BlockSpec((1, d), lambda bi, i: (0, 0)),
                  pl.BlockSpec((1, 1, d), lambda bi, i: (bi, 0, 0)),
                  pl.BlockSpec((1, 1, d), lambda bi, i: (bi, 0, 0)),
                  pl.BlockSpec((d, n_in), lambda bi, i: (0, 0)),
                  pl.BlockSpec((tm, LANES), lambda bi, i: (i, 0)),
                  pl.BlockSpec((tm, LANES), lambda bi, i: (i, 0))],
        out_specs=pl.BlockSpec((1, QKV_TILES, tm, LANES), lambda bi, i: (bi, 0, i, 0)),
        compiler_params=_params("arbitrary", "arbitrary"),
        name="qkv",
    )(x, g.reshape(1, d), scale_a, shift_a, w_in, cos, sin)


NA_ROWS_PER_STEP = 8


def _na_kernel(q_ref, k_ref, v_ref, bias_ref, o_ref, *, rows):
    i = pl.program_id(2)
    lane = lax.broadcasted_iota(jnp.int32, (GRID_W, LANES), 1)
    upper = lane >= HEAD_DIM

    def row(j, carry):
        r = i * NA_ROWS_PER_STEP + j
        r0 = jnp.clip(r - NA_KH // 2, 0, rows - NA_KH)
        var = r - r0
        q = q_ref[0, 0, pl.ds(pl.multiple_of(j * GRID_W, GRID_W), GRID_W), :]
        start = pl.multiple_of(r0 * GRID_W, GRID_W)
        ks = k_ref[0, 0, pl.ds(start, NA_KH * GRID_W), :]
        vs = v_ref[0, 0, pl.ds(start, NA_KH * GRID_W), :]
        outs = []
        for hh in range(2):
            qm = jnp.where(upper if hh else ~upper, q, jnp.zeros_like(q))
            s = lax.dot_general(qm, ks, (((1,), (1,)), ((), ())), preferred_element_type=F32)
            s = s + bias_ref[hh, var]
            m = jnp.max(s, axis=-1, keepdims=True)
            e = jnp.exp(s - m)
            l = jnp.sum(e, axis=-1, keepdims=True)
            o = jnp.dot(e.astype(BF16), vs, preferred_element_type=F32)
            outs.append(o / l)
        out = jnp.where(upper, outs[1], outs[0])
        o_ref[0, 0, pl.ds(pl.multiple_of(j * GRID_W, GRID_W), GRID_W), :] = out.astype(BF16)
        return carry

    lax.fori_loop(0, NA_ROWS_PER_STEP, row, 0)


def _na_bias(rpb):
    qc = jnp.arange(GRID_W)[:, None]
    kc = jnp.arange(GRID_W)[None, :]
    c0 = jnp.clip(qc - NA_KW // 2, 0, GRID_W - NA_KW)
    inwin = (kc >= c0) & (kc < c0 + NA_KW)
    dc = jnp.clip(kc - qc + NA_KW - 1, 0, 2 * NA_KW - 2)
    o = jnp.arange(NA_KH)[:, None]
    ki = jnp.arange(NA_KH)[None, :]
    dr = ki - o + NA_KH - 1
    bias = rpb[:, dr[:, None, :, None], dc[None, :, None, :]]
    bias = jnp.where(inwin[None, None, :, None, :], bias.astype(F32), NEG_INF)
    return bias.reshape(NA_HEADS, NA_KH, GRID_W, NA_KH * GRID_W)


def _na(qkv, bias):
    b, _, s, _ = qkv.shape
    rows = s // GRID_W
    tq = NA_ROWS_PER_STEP * GRID_W
    return pl.pallas_call(
        functools.partial(_na_kernel, rows=rows),
        out_shape=jax.ShapeDtypeStruct((b, PAIRS, s, LANES), BF16),
        grid=(b, PAIRS, rows // NA_ROWS_PER_STEP),
        in_specs=[pl.BlockSpec((1, 1, tq, LANES), lambda bi, p, i: (bi, COL_NQ + p, i, 0)),
                  pl.BlockSpec((1, 1, s, LANES), lambda bi, p, i: (bi, COL_NK + p, 0, 0)),
                  pl.BlockSpec((1, 1, s, LANES), lambda bi, p, i: (bi, COL_NV + p, 0, 0)),
                  pl.BlockSpec((2, NA_KH, GRID_W, NA_KH * GRID_W), lambda bi, p, i: (p, 0, 0, 0))],
        out_specs=pl.BlockSpec((1, 1, tq, LANES), lambda bi, p, i: (bi, p, i, 0)),
        compiler_params=_params("arbitrary", "arbitrary", "arbitrary"),
        name="na",
    )(qkv, qkv, qkv, bias)


SWA_KEYS = 3 * SWA_BLOCK


def _swa_kernel(sink_ref, q_ref, k_ref, v_ref, o_ref, *, seq):
    p = pl.program_id(1)
    n = pl.program_id(2)
    start = pl.multiple_of(jnp.clip((n - 1) * SWA_BLOCK, 0, seq - SWA_KEYS), SWA_BLOCK)
    q = q_ref[0, 0]
    ks = k_ref[0, 0, pl.ds(start, SWA_KEYS), :]
    vs = v_ref[0, 0, pl.ds(start, SWA_KEYS), :]
    lane = lax.broadcasted_iota(jnp.int32, (SWA_BLOCK, LANES), 1)
    upper = lane >= HEAD_DIM
    q_abs = n * SWA_BLOCK + lax.broadcasted_iota(jnp.int32, (SWA_BLOCK, SWA_KEYS), 0)
    k_abs = start + lax.broadcasted_iota(jnp.int32, (SWA_BLOCK, SWA_KEYS), 1)
    in_window = jnp.abs(k_abs - q_abs) <= SWA_WINDOW
    outs = []
    for hh in range(2):
        sink = sink_ref[2 * p + hh]
        qm = jnp.where(upper if hh else ~upper, q, jnp.zeros_like(q))
        s = lax.dot_general(qm, ks, (((1,), (1,)), ((), ())), preferred_element_type=F32)
        s = jnp.where(in_window, s, NEG_INF)
        m = jnp.maximum(jnp.max(s, axis=-1, keepdims=True), sink)
        e = jnp.exp(s - m)
        l = jnp.sum(e, axis=-1, keepdims=True) + jnp.exp(sink - m)
        o = jnp.dot(e.astype(BF16), vs, preferred_element_type=F32)
        outs.append(o / l)
    o_ref[0, 0] = jnp.where(upper, outs[1], outs[0]).astype(BF16)


def _swa(qkv, sinks):
    b, _, s, _ = qkv.shape
    pairs_per_kv = PAIRS // SWA_KV_HEADS
    return pl.pallas_call(
        functools.partial(_swa_kernel, seq=s),
        out_shape=jax.ShapeDtypeStruct((b, PAIRS, s, LANES), BF16),
        grid=(b, PAIRS, s // SWA_BLOCK),
        in_specs=[pl.BlockSpec(memory_space=pltpu.SMEM),
                  pl.BlockSpec((1, 1, SWA_BLOCK, LANES), lambda bi, p, n: (bi, COL_SQ + p, n, 0)),
                  pl.BlockSpec((1, 1, s, LANES), lambda bi, p, n: (bi, COL_SK + p // pairs_per_kv, 0, 0)),
                  pl.BlockSpec((1, 1, s, LANES), lambda bi, p, n: (bi, COL_SV + p // pairs_per_kv, 0, 0))],
        out_specs=pl.BlockSpec((1, 1, SWA_BLOCK, LANES), lambda bi, p, n: (bi, p, n, 0)),
        compiler_params=_params("arbitrary", "arbitrary", "arbitrary"),
        name="swa",
    )(sinks, qkv, qkv, qkv)


def _split3(v):
    hi = v.astype(BF16)
    r1 = v - hi.astype(F32)
    mid = r1.astype(BF16)
    lo = (r1 - mid.astype(F32)).astype(BF16)
    return hi, mid, lo


def _route(logits):
    lane = lax.broadcasted_iota(jnp.int32, logits.shape, 1)
    big = jnp.int32(LANES)
    gmask = lane < N_GROUPS
    gl = jnp.where(gmask, logits, NEG_INF)
    gmax = jnp.max(gl, axis=-1, keepdims=True)
    g_top = jnp.min(jnp.where(gmask & (gl == gmax), lane, big), axis=-1, keepdims=True)
    g_weight = 1.0 / jnp.sum(jnp.where(gmask, jnp.exp(gl - gmax), 0.0), axis=-1, keepdims=True)
    lo = ROUTER_BASE + g_top * EXPERTS_PER_GROUP
    emask = (lane >= lo) & (lane < lo + EXPERTS_PER_GROUP)
    el = jnp.where(emask, logits, NEG_INF)
    m1 = jnp.max(el, axis=-1, keepdims=True)
    i1 = jnp.min(jnp.where(emask & (el == m1), lane, big), axis=-1, keepdims=True)
    emask2 = emask & (lane != i1)
    el2 = jnp.where(emask2, logits, NEG_INF)
    m2 = jnp.max(el2, axis=-1, keepdims=True)
    i2 = jnp.min(jnp.where(emask2 & (el2 == m2), lane, big), axis=-1, keepdims=True)
    e2 = jnp.exp(m2 - m1)
    w1 = g_weight / (1.0 + e2)
    w2 = g_weight * e2 / (1.0 + e2)
    return jnp.where(lane == i1, w1, jnp.where(lane == i2, w2, 0.0))


def _mix_kernel(na_ref, sw_ref, x_ref, bna_ref, bsw_ref, wo_ref, gpm_ref, ga_ref, gpf_ref, scf_ref, shf_ref,
                wr_ref, x1_ref, h2_ref, gates_ref):
    def heads(ref):
        return jnp.concatenate([ref[0, j] for j in range(PAIRS)], axis=-1).astype(F32)

    na = (_rms(heads(na_ref)) * bna_ref[...]).astype(BF16)
    sw = (_rms(heads(sw_ref)) * bsw_ref[...]).astype(BF16)
    mix = (jnp.dot(na, wo_ref[:NA_WIDTH, :], preferred_element_type=F32)
           + jnp.dot(sw, wo_ref[NA_WIDTH:, :], preferred_element_type=F32))
    x1 = x_ref[0] + ga_ref[0] * (_rms(mix) * gpm_ref[...])
    x1_ref[0] = x1
    h2 = (_rms(x1) * gpf_ref[...]) * (1.0 + scf_ref[0]) + shf_ref[0]
    h2_ref[0] = h2.astype(BF16)
    h_hi, h_mid, h_lo = _split3(h2)
    w_hi, w_mid, w_lo = wr_ref[0], wr_ref[1], wr_ref[2]
    dot = functools.partial(jnp.dot, preferred_element_type=F32)
    logits = (dot(h_lo, w_hi) + dot(h_mid, w_mid) + dot(h_hi, w_lo)
              + dot(h_mid, w_hi) + dot(h_hi, w_mid) + dot(h_hi, w_hi))
    gates_ref[0] = _route(logits)


def _mix(na, sw, x, beta_na, beta_swa, w_out, g_post_mix, gate_a, g_pre_ffn, scale_f, shift_f, w_router3):
    b, s, d = x.shape
    tm = 512
    row = lambda bi, i: (bi, i, 0)
    const2 = lambda bi, i: (0, 0)
    per_b = lambda bi, i: (bi, 0, 0)
    return pl.pallas_call(
        _mix_kernel,
        out_shape=(jax.ShapeDtypeStruct((b, s, d), F32),
                   jax.ShapeDtypeStruct((b, s, d), BF16),
                   jax.ShapeDtypeStruct((b, s, LANES), F32)),
        grid=(b, s // tm),
        in_specs=[pl.BlockSpec((1, PAIRS, tm, LANES), lambda bi, i: (bi, 0, i, 0)),
                  pl.BlockSpec((1, PAIRS, tm, LANES), lambda bi, i: (bi, 0, i, 0)),
                  pl.BlockSpec((1, tm, d), row),
                  pl.BlockSpec((1, NA_WIDTH), const2),
                  pl.BlockSpec((1, SWA_WIDTH), const2),
                  pl.BlockSpec((NA_WIDTH + SWA_WIDTH, d), const2),
                  pl.BlockSpec((1, d), const2),
                  pl.BlockSpec((1, 1, d), per_b),
                  pl.BlockSpec((1, d), const2),
                  pl.BlockSpec((1, 1, d), per_b),
                  pl.BlockSpec((1, 1, d), per_b),
                  pl.BlockSpec((3, d, LANES), lambda bi, i: (0, 0, 0))],
        out_specs=(pl.BlockSpec((1, tm, d), row),
                   pl.BlockSpec((1, tm, d), row),
                   pl.BlockSpec((1, tm, LANES), row)),
        compiler_params=_params("arbitrary", "arbitrary"),
        name="mix",
    )(na, sw, x, beta_na.reshape(1, -1), beta_swa.reshape(1, -1), w_out, g_post_mix.reshape(1, d), gate_a,
      g_pre_ffn.reshape(1, d), scale_f, shift_f, w_router3)


def _moe_kernel(h_ref, gates_ref, wg_ref, wu_ref, wd_ref, x1_ref, gf_ref, gpost_ref, o_ref, acc_ref):
    e = pl.program_id(2)

    @pl.when(e == 0)
    def _():
        acc_ref[...] = jnp.zeros_like(acc_ref)

    h = h_ref[0]
    gate = jnp.dot(h, wg_ref[0].astype(BF16), preferred_element_type=F32)
    up = jnp.dot(h, wu_ref[0].astype(BF16), preferred_element_type=F32)
    he = (gate * jax.nn.sigmoid(gate) * up).astype(BF16)
    ye = jnp.dot(he, wd_ref[0].astype(BF16), preferred_element_type=F32)
    g = gates_ref[0]
    lane = lax.broadcasted_iota(jnp.int32, g.shape, 1)
    ge = jnp.sum(jnp.where(lane == ROUTER_BASE + e, g, 0.0), axis=-1, keepdims=True)
    acc_ref[...] += ge * ye

    @pl.when(e == N_EXPERTS - 1)
    def _():
        o_ref[0] = x1_ref[0] + gf_ref[0] * (_rms(acc_ref[...]) * gpost_ref[...])


def _moe(h2, gates, w_gate, w_up, w_down, x1, gate_f, g_post_ffn):
    b, s, d = x1.shape
    tm = 1024
    row = lambda bi, i, e: (bi, i, 0)
    return pl.pallas_call(
        _moe_kernel,
        out_shape=jax.ShapeDtypeStruct((b, s, d), F32),
        grid=(b, s // tm, N_EXPERTS),
        in_specs=[pl.BlockSpec((1, tm, d), row),
                  pl.BlockSpec((1, tm, LANES), row),
                  pl.BlockSpec((1, d, EXPERT_FF), lambda bi, i, e: (e, 0, 0)),
                  pl.BlockSpec((1, d, EXPERT_FF), lambda bi, i, e: (e, 0, 0)),
                  pl.BlockSpec((1, EXPERT_FF, d), lambda bi, i, e: (e, 0, 0)),
                  pl.BlockSpec((1, tm, d), row),
                  pl.BlockSpec((1, 1, d), lambda bi, i, e: (bi, 0, 0)),
                  pl.BlockSpec((1, d), lambda bi, i, e: (0, 0))],
        out_specs=pl.BlockSpec((1, tm, d), row),
        scratch_shapes=[pltpu.VMEM((tm, d), F32)],
        compiler_params=_params("arbitrary", "arbitrary", "arbitrary"),
        name="moe",
    )(h2, gates, w_gate, w_up, w_down, x1, gate_f, g_post_ffn.reshape(1, d))


def _rope_tables(s):
    half = HEAD_DIM // 2
    inv = ROPE_THETA ** (-jnp.arange(half, dtype=F32) * 2.0 / HEAD_DIM)
    ang = jnp.arange(s).astype(F32)[:, None] * inv[None, :]
    cos = jnp.cos(ang)
    sin = jnp.sin(ang)
    cos = jnp.concatenate([cos, cos, cos, cos], axis=-1)
    sin_signed = jnp.concatenate([-sin, sin, -sin, sin], axis=-1)
    return cos, sin_signed


def _router_weights(w_group_router, w_expert_router):
    d = w_group_router.shape[0]
    we = jnp.transpose(w_expert_router, (1, 0, 2)).reshape(d, N_EXPERTS)
    w = jnp.concatenate([w_group_router, we, jnp.zeros((d, LANES - N_GROUPS - N_EXPERTS), F32)], axis=-1)
    return jnp.stack(_split3(w))


def kernel(x, c, w_ada, b_ada, g_pre_mix, w_in, na_rpb, swa_sinks, beta_na, beta_swa, w_out, g_post_mix, g_pre_ffn,
           w_group_router, w_expert_router, w_gate, w_up, w_down, g_post_ffn):
    b, s, d = x.shape
    depth = w_ada.shape[0]
    cos, sin_signed = _rope_tables(s)
    for l in range(depth):
        mod = _adaln(c, w_ada[l], b_ada[l]).reshape(b, N_MOD, 1, d)
        shift_a, scale_a, gate_a, shift_f, scale_f, gate_f = (mod[:, k] for k in range(N_MOD))
        qkv = _qkv(x, g_pre_mix[l], scale_a, shift_a, w_in[l].astype(BF16), cos, sin_signed)
        na = _na(qkv, _na_bias(na_rpb[l]))
        sw = _swa(qkv, swa_sinks[l])
        x1, h2, gates = _mix(na, sw, x, beta_na[l], beta_swa[l], w_out[l].astype(BF16), g_post_mix[l], gate_a,
                             g_pre_ffn[l], scale_f, shift_f,
                             _router_weights(w_group_router[l], w_expert_router[l]))
        x = _moe(h2, gates, w_gate[l], w_up[l], w_down[l], x1, gate_f, g_post_ffn[l])
    return x
```

```python
import functools

import jax
import jax.numpy as jnp
from jax import lax
from jax.experimental import pallas as pl
from jax.experimental.pallas import tpu as pltpu

D_MODEL = 1024
GRID_W = 64
HEAD_DIM = 64
NA_HEADS = 8
NA_KH = 8
NA_KW = 16
SWA_HEADS = 8
SWA_KV_HEADS = 2
SWA_WINDOW = 128
SWA_BLOCK = 128
ROPE_THETA = 10000.0
NA_WIDTH = NA_HEADS * HEAD_DIM
SWA_WIDTH = SWA_HEADS * HEAD_DIM
N_GROUPS = 4
EXPERTS_PER_GROUP = 8
N_EXPERTS = N_GROUPS * EXPERTS_PER_GROUP
EXPERT_FF = 256
N_MOD = 6
EPS = 1e-6
NEG_INF = -1e30

LANES = 128
PAIRS = NA_HEADS // 2
COL_NQ, COL_NK, COL_NV, COL_SQ, COL_SK, COL_SV = 0, 4, 8, 12, 16, 18
QKV_TILES = 20
ROUTER_BASE = N_GROUPS
VMEM_LIMIT = 56 * 1024 * 1024

F32 = jnp.float32
BF16 = jnp.bfloat16


def _rms(v):
    return v * lax.rsqrt(jnp.mean(v * v, axis=-1, keepdims=True) + EPS)


def _params(*sem):
    return pltpu.CompilerParams(dimension_semantics=sem, vmem_limit_bytes=VMEM_LIMIT)


def _adaln_kernel(c_ref, w_ref, b_ref, o_ref):
    c = c_ref[...]
    a = c * jax.nn.sigmoid(c)
    o_ref[...] = jnp.dot(a, w_ref[...], precision=lax.Precision.HIGHEST,
                         preferred_element_type=F32) + b_ref[...]


def _adaln(c, w_ada, b_ada):
    batch, d = c.shape
    n = w_ada.shape[1]
    tn = 1024
    b = 8
    c = jnp.pad(c, ((0, b - batch), (0, 0)))
    return pl.pallas_call(
        _adaln_kernel,
        out_shape=jax.ShapeDtypeStruct((b, n), F32),
        grid=(n // tn,),
        in_specs=[pl.BlockSpec((b, d), lambda j: (0, 0)),
                  pl.BlockSpec((d, tn), lambda j: (0, j)),
                  pl.BlockSpec((1, tn), lambda j: (0, j))],
        out_specs=pl.BlockSpec((b, tn), lambda j: (0, j)),
        compiler_params=_params("arbitrary"),
        name="adaln",
    )(c, w_ada, b_ada.reshape(1, n))[:batch]


def _rope(v, cos, sin_signed, first_half):
    rot = jnp.where(first_half, pltpu.roll(v, LANES - HEAD_DIM // 2, 1), pltpu.roll(v, HEAD_DIM // 2, 1))
    return v * cos + rot * sin_signed


def _qkv_kernel(x_ref, g_ref, sc_ref, sh_ref, w_ref, cos_ref, sin_ref, o_ref):
    x = x_ref[0]
    h = (_rms(x) * g_ref[...]) * (1.0 + sc_ref[0]) + sh_ref[0]
    h = h.astype(BF16)
    scale = HEAD_DIM ** -0.5
    cos = cos_ref[...]
    sin = sin_ref[...]
    lane = lax.broadcasted_iota(jnp.int32, cos.shape, 1)
    first_half = (lane % HEAD_DIM) < HEAD_DIM // 2
    upper = lane >= HEAD_DIM

    def proj(col, width):
        return jnp.dot(h, w_ref[:, col * LANES:(col + width) * LANES], preferred_element_type=F32)

    def tile(v, j):
        return v[:, j * LANES:(j + 1) * LANES]

    nq, nk, nv, sq = proj(COL_NQ, 4), proj(COL_NK, 4), proj(COL_NV, 4), proj(COL_SQ, 4)
    for j in range(PAIRS):
        o_ref[0, COL_NQ + j] = (tile(nq, j) * scale).astype(BF16)
        o_ref[0, COL_NK + j] = tile(nk, j).astype(BF16)
        o_ref[0, COL_NV + j] = tile(nv, j).astype(BF16)
        o_ref[0, COL_SQ + j] = (_rope(tile(sq, j), cos, sin, first_half) * scale).astype(BF16)
    skv = proj(COL_SK, 2)
    k = _rope(tile(skv, 0), cos, sin, first_half)
    v = tile(skv, 1)
    for t, col in ((k, COL_SK), (v, COL_SV)):
        swapped = pltpu.roll(t, HEAD_DIM, 1)
        o_ref[0, col] = jnp.where(upper, swapped, t).astype(BF16)
        o_ref[0, col + 1] = jnp.where(upper, t, swapped).astype(BF16)


def _qkv(x, g, scale_a, shift_a, w_in, cos, sin):
    b, s, d = x.shape
    tm = 512
    n_in = w_in.shape[1]
    return pl.pallas_call(
        _qkv_kernel,
        out_shape=jax.ShapeDtypeStruct((b, QKV_TILES, s, LANES), BF16),
        grid=(b, s // tm),
        in_specs=[pl.BlockSpec((1, tm, d), lambda bi, i: (bi, i, 0)),
                  pl.BlockSpec((1, d), lambda bi, i: (0, 0)),
                  pl.BlockSpec((1, 1, d), lambda bi, i: (bi, 0, 0)),
                  pl.BlockSpec((1, 1, d), lambda bi, i: (bi, 0, 0)),
                  pl.BlockSpec((d, n_in), lambda bi, i: (0, 0)),
                  pl.BlockSpec((tm, LANES), lambda bi, i: (i, 0)),
                  pl.BlockSpec((tm, LANES), lambda bi, i: (i, 0))],
        out_specs=pl.BlockSpec((1, QKV_TILES, tm, LANES), lambda bi, i: (bi, 0, i, 0)),
        compiler_params=_params("arbitrary", "arbitrary"),
        name="qkv",
    )(x, g.reshape(1, d), scale_a, shift_a, w_in, cos, sin)


NA_ROWS_PER_STEP = 8


def _na_kernel(q_ref, k_ref, v_ref, bias_ref, o_ref, *, rows):
    i = pl.program_id(2)
    lane = lax.broadcasted_iota(jnp.int32, (GRID_W, LANES), 1)
    upper = lane >= HEAD_DIM

    def row(j, carry):
        r = i * NA_ROWS_PER_STEP + j
        r0 = jnp.clip(r - NA_KH // 2, 0, rows - NA_KH)
        var = r - r0
        q = q_ref[0, 0, pl.ds(pl.multiple_of(j * GRID_W, GRID_W), GRID_W), :]
        start = pl.multiple_of(r0 * GRID_W, GRID_W)
        ks = k_ref[0, 0, pl.ds(start, NA_KH * GRID_W), :]
        vs = v_ref[0, 0, pl.ds(start, NA_KH * GRID_W), :]
        outs = []
        for hh in range(2):
            qm = jnp.where(upper if hh else ~upper, q, jnp.zeros_like(q))
            s = lax.dot_general(qm, ks, (((1,), (1,)), ((), ())), preferred_element_type=F32)
            s = s + jnp.concatenate([bias_ref[hh, NA_KH - 1 - var + 2 * t] for t in range(NA_KH // 2)], axis=-1)
            m = jnp.max(s, axis=-1, keepdims=True)
            e = jnp.exp(s - m)
            l = jnp.sum(e, axis=-1, keepdims=True)
            o = jnp.dot(e.astype(BF16), vs, preferred_element_type=F32)
            outs.append(o / l)
        out = jnp.where(upper, outs[1], outs[0])
        o_ref[0, 0, pl.ds(pl.multiple_of(j * GRID_W, GRID_W), GRID_W), :] = out.astype(BF16)
        return carry

    lax.fori_loop(0, NA_ROWS_PER_STEP, row, 0, unroll=True)


NA_BIAS_ROWS = 2 * NA_KH - 2
NA_RPB_COLS = 2 * NA_KW - 1


def _na_bias_kernel(rpb_ref, o_ref):
    h = pl.program_id(0)
    q = lax.broadcasted_iota(jnp.int32, (GRID_W, LANES), 0)
    lane = lax.broadcasted_iota(jnp.int32, (GRID_W, LANES), 1)
    kc = lane % GRID_W
    upper = lane >= GRID_W
    c0 = jnp.clip(q - NA_KW // 2, 0, GRID_W - NA_KW)
    in_window = (kc >= c0) & (kc < c0 + NA_KW)
    dc = kc - q + NA_KW - 1
    base = h * (NA_BIAS_ROWS + 1) * NA_RPB_COLS
    for d in range(NA_BIAS_ROWS):
        acc = jnp.full((GRID_W, LANES), NEG_INF, F32)
        for dd in range(NA_RPB_COLS):
            lo = rpb_ref[base + d * NA_RPB_COLS + dd]
            hi = rpb_ref[base + (d + 1) * NA_RPB_COLS + dd]
            acc = jnp.where(dc == dd, jnp.where(upper, hi, lo), acc)
        o_ref[0, d] = jnp.where(in_window, acc, NEG_INF)


def _na_bias(rpb):
    return pl.pallas_call(
        _na_bias_kernel,
        out_shape=jax.ShapeDtypeStruct((NA_HEADS, NA_BIAS_ROWS, GRID_W, LANES), F32),
        grid=(NA_HEADS,),
        in_specs=[pl.BlockSpec(memory_space=pltpu.SMEM)],
        out_specs=pl.BlockSpec((1, NA_BIAS_ROWS, GRID_W, LANES), lambda h: (h, 0, 0, 0)),
        compiler_params=_params("arbitrary"),
        name="na_bias",
    )(rpb.astype(F32).reshape(-1))


def _na(qkv, bias):
    b, _, s, _ = qkv.shape
    rows = s // GRID_W
    tq = NA_ROWS_PER_STEP * GRID_W
    return pl.pallas_call(
        functools.partial(_na_kernel, rows=rows),
        out_shape=jax.ShapeDtypeStruct((b, PAIRS, s, LANES), BF16),
        grid=(b, PAIRS, rows // NA_ROWS_PER_STEP),
        in_specs=[pl.BlockSpec((1, 1, tq, LANES), lambda bi, p, i: (bi, COL_NQ + p, i, 0)),
                  pl.BlockSpec((1, 1, s, LANES), lambda bi, p, i: (bi, COL_NK + p, 0, 0)),
                  pl.BlockSpec((1, 1, s, LANES), lambda bi, p, i: (bi, COL_NV + p, 0, 0)),
                  pl.BlockSpec((2, NA_BIAS_ROWS, GRID_W, LANES), lambda bi, p, i: (p, 0, 0, 0))],
        out_specs=pl.BlockSpec((1, 1, tq, LANES), lambda bi, p, i: (bi, p, i, 0)),
        compiler_params=_params("arbitrary", "arbitrary", "arbitrary"),
        name="na",
    )(qkv, qkv, qkv, bias)


SWA_KEYS = 3 * SWA_BLOCK
SWA_BLOCKS_PER_STEP = 4


def _swa_kernel(sink_ref, q_ref, k_ref, v_ref, o_ref, *, seq):
    p = pl.program_id(1)
    lane = lax.broadcasted_iota(jnp.int32, (SWA_BLOCK, LANES), 1)
    upper = lane >= HEAD_DIM
    q_off = lax.broadcasted_iota(jnp.int32, (SWA_BLOCK, SWA_KEYS), 0)
    k_off = lax.broadcasted_iota(jnp.int32, (SWA_BLOCK, SWA_KEYS), 1)
    for j in range(SWA_BLOCKS_PER_STEP):
        n = pl.program_id(2) * SWA_BLOCKS_PER_STEP + j
        start = pl.multiple_of(jnp.clip((n - 1) * SWA_BLOCK, 0, seq - SWA_KEYS), SWA_BLOCK)
        q = q_ref[0, 0, j * SWA_BLOCK:(j + 1) * SWA_BLOCK, :]
        ks = k_ref[0, 0, pl.ds(start, SWA_KEYS), :]
        vs = v_ref[0, 0, pl.ds(start, SWA_KEYS), :]
        in_window = jnp.abs((start + k_off) - (n * SWA_BLOCK + q_off)) <= SWA_WINDOW
        outs = []
        for hh in range(2):
            sink = sink_ref[2 * p + hh]
            qm = jnp.where(upper if hh else ~upper, q, jnp.zeros_like(q))
            s = lax.dot_general(qm, ks, (((1,), (1,)), ((), ())), preferred_element_type=F32)
            s = jnp.where(in_window, s, NEG_INF)
            m = jnp.maximum(jnp.max(s, axis=-1, keepdims=True), sink)
            e = jnp.exp(s - m)
            l = jnp.sum(e, axis=-1, keepdims=True) + jnp.exp(sink - m)
            o = jnp.dot(e.astype(BF16), vs, preferred_element_type=F32)
            outs.append(o / l)
        o_ref[0, 0, j * SWA_BLOCK:(j + 1) * SWA_BLOCK, :] = jnp.where(upper, outs[1], outs[0]).astype(BF16)


def _swa(qkv, sinks):
    b, _, s, _ = qkv.shape
    pairs_per_kv = PAIRS // SWA_KV_HEADS
    tq = SWA_BLOCKS_PER_STEP * SWA_BLOCK
    return pl.pallas_call(
        functools.partial(_swa_kernel, seq=s),
        out_shape=jax.ShapeDtypeStruct((b, PAIRS, s, LANES), BF16),
        grid=(b, PAIRS, s // tq),
        in_specs=[pl.BlockSpec(memory_space=pltpu.SMEM),
                  pl.BlockSpec((1, 1, tq, LANES), lambda bi, p, n: (bi, COL_SQ + p, n, 0)),
                  pl.BlockSpec((1, 1, s, LANES), lambda bi, p, n: (bi, COL_SK + p // pairs_per_kv, 0, 0)),
                  pl.BlockSpec((1, 1, s, LANES), lambda bi, p, n: (bi, COL_SV + p // pairs_per_kv, 0, 0))],
        out_specs=pl.BlockSpec((1, 1, tq, LANES), lambda bi, p, n: (bi, p, n, 0)),
        compiler_params=_params("arbitrary", "arbitrary", "arbitrary"),
        name="swa",
    )(sinks, qkv, qkv, qkv)


def _split3(v):
    hi = v.astype(BF16)
    r1 = v - hi.astype(F32)
    mid = r1.astype(BF16)
    lo = (r1 - mid.astype(F32)).astype(BF16)
    return hi, mid, lo


def _route(logits):
    lane = lax.broadcasted_iota(jnp.int32, logits.shape, 1)
    big = jnp.int32(LANES)
    gmask = lane < N_GROUPS
    gl = jnp.where(gmask, logits, NEG_INF)
    gmax = jnp.max(gl, axis=-1, keepdims=True)
    g_top = jnp.min(jnp.where(gmask & (gl == gmax), lane, big), axis=-1, keepdims=True)
    g_weight = 1.0 / jnp.sum(jnp.where(gmask, jnp.exp(gl - gmax), 0.0), axis=-1, keepdims=True)
    lo = ROUTER_BASE + g_top * EXPERTS_PER_GROUP
    emask = (lane >= lo) & (lane < lo + EXPERTS_PER_GROUP)
    el = jnp.where(emask, logits, NEG_INF)
    m1 = jnp.max(el, axis=-1, keepdims=True)
    i1 = jnp.min(jnp.where(emask & (el == m1), lane, big), axis=-1, keepdims=True)
    emask2 = emask & (lane != i1)
    el2 = jnp.where(emask2, logits, NEG_INF)
    m2 = jnp.max(el2, axis=-1, keepdims=True)
    i2 = jnp.min(jnp.where(emask2 & (el2 == m2), lane, big), axis=-1, keepdims=True)
    e2 = jnp.exp(m2 - m1)
    w1 = g_weight / (1.0 + e2)
    w2 = g_weight * e2 / (1.0 + e2)
    return jnp.where(lane == i1, w1, jnp.where(lane == i2, w2, 0.0))


def _mix_kernel(na_ref, sw_ref, x_ref, bna_ref, bsw_ref, wo_ref, gpm_ref, ga_ref, gpf_ref, scf_ref, shf_ref,
                wr_ref, x1_ref, h2_ref, gates_ref):
    def heads(ref):
        return jnp.concatenate([ref[0, j] for j in range(PAIRS)], axis=-1).astype(F32)

    na = (_rms(heads(na_ref)) * bna_ref[...]).astype(BF16)
    sw = (_rms(heads(sw_ref)) * bsw_ref[...]).astype(BF16)
    mix = (jnp.dot(na, wo_ref[:NA_WIDTH, :], preferred_element_type=F32)
           + jnp.dot(sw, wo_ref[NA_WIDTH:, :], preferred_element_type=F32))
    x1 = x_ref[0] + ga_ref[0] * (_rms(mix) * gpm_ref[...])
    x1_ref[0] = x1
    h2 = (_rms(x1) * gpf_ref[...]) * (1.0 + scf_ref[0]) + shf_ref[0]
    h2_ref[0] = h2.astype(BF16)
    h_hi, h_mid, h_lo = _split3(h2)
    w_hi, w_mid, w_lo = wr_ref[0], wr_ref[1], wr_ref[2]
    dot = functools.partial(jnp.dot, preferred_element_type=F32)
    logits = (dot(h_lo, w_hi) + dot(h_mid, w_mid) + dot(h_hi, w_lo)
              + dot(h_mid, w_hi) + dot(h_hi, w_mid) + dot(h_hi, w_hi))
    gates_ref[0] = _route(logits)


def _mix(na, sw, x, beta_na, beta_swa, w_out, g_post_mix, gate_a, g_pre_ffn, scale_f, shift_f, w_router3):
    b, s, d = x.shape
    tm = 512
    row = lambda bi, i: (bi, i, 0)
    const2 = lambda bi, i: (0, 0)
    per_b = lambda bi, i: (bi, 0, 0)
    return pl.pallas_call(
        _mix_kernel,
        out_shape=(jax.ShapeDtypeStruct((b, s, d), F32),
                   jax.ShapeDtypeStruct((b, s, d), BF16),
                   jax.ShapeDtypeStruct((b, s, LANES), F32)),
        grid=(b, s // tm),
        in_specs=[pl.BlockSpec((1, PAIRS, tm, LANES), lambda bi, i: (bi, 0, i, 0)),
                  pl.BlockSpec((1, PAIRS, tm, LANES), lambda bi, i: (bi, 0, i, 0)),
                  pl.BlockSpec((1, tm, d), row),
                  pl.BlockSpec((1, NA_WIDTH), const2),
                  pl.BlockSpec((1, SWA_WIDTH), const2),
                  pl.BlockSpec((NA_WIDTH + SWA_WIDTH, d), const2),
                  pl.BlockSpec((1, d), const2),
                  pl.BlockSpec((1, 1, d), per_b),
                  pl.BlockSpec((1, d), const2),
                  pl.BlockSpec((1, 1, d), per_b),
                  pl.BlockSpec((1, 1, d), per_b),
                  pl.BlockSpec((3, d, LANES), lambda bi, i: (0, 0, 0))],
        out_specs=(pl.BlockSpec((1, tm, d), row),
                   pl.BlockSpec((1, tm, d), row),
                   pl.BlockSpec((1, tm, LANES), row)),
        compiler_params=_params("arbitrary", "arbitrary"),
        name="mix",
    )(na, sw, x, beta_na.reshape(1, -1), beta_swa.reshape(1, -1), w_out, g_post_mix.reshape(1, d), gate_a,
      g_pre_ffn.reshape(1, d), scale_f, shift_f, w_router3)


def _moe_kernel(h_ref, gates_ref, wg_ref, wu_ref, wd_ref, x1_ref, gf_ref, gpost_ref, o_ref, acc_ref):
    e = pl.program_id(2)

    @pl.when(e == 0)
    def _():
        acc_ref[...] = jnp.zeros_like(acc_ref)

    h = h_ref[0]
    gate = jnp.dot(h, wg_ref[0].astype(BF16), preferred_element_type=F32)
    up = jnp.dot(h, wu_ref[0].astype(BF16), preferred_element_type=F32)
    he = (gate * jax.nn.sigmoid(gate) * up).astype(BF16)
    ye = jnp.dot(he, wd_ref[0].astype(BF16), preferred_element_type=F32)
    g = gates_ref[0]
    lane = lax.broadcasted_iota(jnp.int32, g.shape, 1)
    ge = jnp.sum(jnp.where(lane == ROUTER_BASE + e, g, 0.0), axis=-1, keepdims=True)
    acc_ref[...] += ge * ye

    @pl.when(e == N_EXPERTS - 1)
    def _():
        o_ref[0] = x1_ref[0] + gf_ref[0] * (_rms(acc_ref[...]) * gpost_ref[...])


def _moe(h2, gates, w_gate, w_up, w_down, x1, gate_f, g_post_ffn):
    b, s, d = x1.shape
    tm = 1024
    row = lambda bi, i, e: (bi, i, 0)
    return pl.pallas_call(
        _moe_kernel,
        out_shape=jax.ShapeDtypeStruct((b, s, d), F32),
        grid=(b, s // tm, N_EXPERTS),
        in_specs=[pl.BlockSpec((1, tm, d), row),
                  pl.BlockSpec((1, tm, LANES), row),
                  pl.BlockSpec((1, d, EXPERT_FF), lambda bi, i, e: (e, 0, 0)),
                  pl.BlockSpec((1, d, EXPERT_FF), lambda bi, i, e: (e, 0, 0)),
                  pl.BlockSpec((1, EXPERT_FF, d), lambda bi, i, e: (e, 0, 0)),
                  pl.BlockSpec((1, tm, d), row),
                  pl.BlockSpec((1, 1, d), lambda bi, i, e: (bi, 0, 0)),
                  pl.BlockSpec((1, d), lambda bi, i, e: (0, 0))],
        out_specs=pl.BlockSpec((1, tm, d), row),
        scratch_shapes=[pltpu.VMEM((tm, d), F32)],
        compiler_params=_params("arbitrary", "arbitrary", "arbitrary"),
        name="moe",
    )(h2, gates, w_gate, w_up, w_down, x1, gate_f, g_post_ffn.reshape(1, d))


def _rope_tables(s):
    half = HEAD_DIM // 2
    inv = ROPE_THETA ** (-jnp.arange(half, dtype=F32) * 2.0 / HEAD_DIM)
    ang = jnp.arange(s).astype(F32)[:, None] * inv[None, :]
    cos = jnp.cos(ang)
    sin = jnp.sin(ang)
    cos = jnp.concatenate([cos, cos, cos, cos], axis=-1)
    sin_signed = jnp.concatenate([-sin, sin, -sin, sin], axis=-1)
    return cos, sin_signed


def _router_weights(w_group_router, w_expert_router):
    d = w_group_router.shape[0]
    we = jnp.transpose(w_expert_router, (1, 0, 2)).reshape(d, N_EXPERTS)
    w = jnp.concatenate([w_group_router, we, jnp.zeros((d, LANES - N_GROUPS - N_EXPERTS), F32)], axis=-1)
    return jnp.stack(_split3(w))


def kernel(x, c, w_ada, b_ada, g_pre_mix, w_in, na_rpb, swa_sinks, beta_na, beta_swa, w_out, g_post_mix, g_pre_ffn,
           w_group_router, w_expert_router, w_gate, w_up, w_down, g_post_ffn):
    b, s, d = x.shape
    depth = w_ada.shape[0]
    cos, sin_signed = _rope_tables(s)
    for l in range(depth):
        mod = _adaln(c, w_ada[l], b_ada[l]).reshape(b, N_MOD, 1, d)
        shift_a, scale_a, gate_a, shift_f, scale_f, gate_f = (mod[:, k] for k in range(N_MOD))
        qkv = _qkv(x, g_pre_mix[l], scale_a, shift_a, w_in[l].astype(BF16), cos, sin_signed)
        na = _na(qkv, _na_bias(na_rpb[l]))
        sw = _swa(qkv, swa_sinks[l])
        x1, h2, gates = _mix(na, sw, x, beta_na[l], beta_swa[l], w_out[l].astype(BF16), g_post_mix[l], gate_a,
                             g_pre_ffn[l], scale_f, shift_f,
                             _router_weights(w_group_router[l], w_expert_router[l]))
        x = _moe(h2, gates, w_gate[l], w_up[l], w_down[l], x1, gate_f, g_post_ffn[l])
    return x
```

```python
import functools

import jax
import jax.numpy as jnp
from jax import lax
from jax.experimental import pallas as pl
from jax.experimental.pallas import tpu as pltpu

D_MODEL = 1024
GRID_W = 64
HEAD_DIM = 64
NA_HEADS = 8
NA_KH = 8
NA_KW = 16
SWA_HEADS = 8
SWA_KV_HEADS = 2
SWA_WINDOW = 128
SWA_BLOCK = 128
ROPE_THETA = 10000.0
NA_WIDTH = NA_HEADS * HEAD_DIM
SWA_WIDTH = SWA_HEADS * HEAD_DIM
N_GROUPS = 4
EXPERTS_PER_GROUP = 8
N_EXPERTS = N_GROUPS * EXPERTS_PER_GROUP
EXPERT_FF = 256
N_MOD = 6
EPS = 1e-6
NEG_INF = -1e30

LANES = 128
SUBLANES = 8
R_E1, R_E2, R_W1, R_W2 = range(4)
PAIRS = NA_HEADS // 2
COL_NQ, COL_NK, COL_NV, COL_SQ, COL_SK, COL_SV = 0, 4, 8, 12, 16, 18
QKV_TILES = 20
ROUTER_BASE = N_GROUPS
VMEM_LIMIT = 56 * 1024 * 1024

F32 = jnp.float32
BF16 = jnp.bfloat16


def _rms(v):
    return v * lax.rsqrt(jnp.mean(v * v, axis=-1, keepdims=True) + EPS)


def _params(*sem):
    return pltpu.CompilerParams(dimension_semantics=sem, vmem_limit_bytes=VMEM_LIMIT)


def _adaln_kernel(c_ref, w_ref, b_ref, o_ref):
    c = c_ref[...]
    a = c * jax.nn.sigmoid(c)
    o_ref[...] = jnp.dot(a, w_ref[...], precision=lax.Precision.HIGHEST,
                         preferred_element_type=F32) + b_ref[...]


def _adaln(c, w_ada, b_ada):
    batch, d = c.shape
    n = w_ada.shape[1]
    tn = 1024
    b = 8
    c = jnp.pad(c, ((0, b - batch), (0, 0)))
    return pl.pallas_call(
        _adaln_kernel,
        out_shape=jax.ShapeDtypeStruct((b, n), F32),
        grid=(n // tn,),
        in_specs=[pl.BlockSpec((b, d), lambda j: (0, 0)),
                  pl.BlockSpec((d, tn), lambda j: (0, j)),
                  pl.BlockSpec((1, tn), lambda j: (0, j))],
        out_specs=pl.BlockSpec((b, tn), lambda j: (0, j)),
        compiler_params=_params("arbitrary"),
        name="adaln",
    )(c, w_ada, b_ada.reshape(1, n))[:batch]


def _rope(v, cos, sin_signed, first_half):
    rot = jnp.where(first_half, pltpu.roll(v, LANES - HEAD_DIM // 2, 1), pltpu.roll(v, HEAD_DIM // 2, 1))
    return v * cos + rot * sin_signed


def _qkv_kernel(x_ref, g_ref, sc_ref, sh_ref, w_ref, cos_ref, sin_ref, o_ref):
    x = x_ref[0]
    h = (_rms(x) * g_ref[...]) * (1.0 + sc_ref[0]) + sh_ref[0]
    h = h.astype(BF16)
    scale = HEAD_DIM ** -0.5
    cos = cos_ref[...]
    sin = sin_ref[...]
    lane = lax.broadcasted_iota(jnp.int32, cos.shape, 1)
    first_half = (lane % HEAD_DIM) < HEAD_DIM // 2
    upper = lane >= HEAD_DIM

    def proj(col, width):
        return jnp.dot(h, w_ref[:, col * LANES:(col + width) * LANES], preferred_element_type=F32)

    def tile(v, j):
        return v[:, j * LANES:(j + 1) * LANES]

    nq, nk, nv, sq = proj(COL_NQ, 4), proj(COL_NK, 4), proj(COL_NV, 4), proj(COL_SQ, 4)
    for j in range(PAIRS):
        o_ref[0, COL_NQ + j] = (tile(nq, j) * scale).astype(BF16)
        o_ref[0, COL_NK + j] = tile(nk, j).astype(BF16)
        o_ref[0, COL_NV + j] = tile(nv, j).astype(BF16)
        o_ref[0, COL_SQ + j] = (_rope(tile(sq, j), cos, sin, first_half) * scale).astype(BF16)
    skv = proj(COL_SK, 2)
    k = _rope(tile(skv, 0), cos, sin, first_half)
    v = tile(skv, 1)
    for t, col in ((k, COL_SK), (v, COL_SV)):
        swapped = pltpu.roll(t, HEAD_DIM, 1)
        o_ref[0, col] = jnp.where(upper, swapped, t).astype(BF16)
        o_ref[0, col + 1] = jnp.where(upper, t, swapped).astype(BF16)


def _qkv(x, g, scale_a, shift_a, w_in, cos, sin):
    b, s, d = x.shape
    tm = 512
    n_in = w_in.shape[1]
    return pl.pallas_call(
        _qkv_kernel,
        out_shape=jax.ShapeDtypeStruct((b, QKV_TILES, s, LANES), BF16),
        grid=(b, s // tm),
        in_specs=[pl.BlockSpec((1, tm, d), lambda bi, i: (bi, i, 0)),
                  pl.BlockSpec((1, d), lambda bi, i: (0, 0)),
                  pl.BlockSpec((1, 1, d), lambda bi, i: (bi, 0, 0)),
                  pl.BlockSpec((1, 1, d), lambda bi, i: (bi, 0, 0)),
                  pl.BlockSpec((d, n_in), lambda bi, i: (0, 0)),
                  pl.BlockSpec((tm, LANES), lambda bi, i: (i, 0)),
                  pl.BlockSpec((tm, LANES), lambda bi, i: (i, 0))],
        out_specs=pl.BlockSpec((1, QKV_TILES, tm, LANES), lambda bi, i: (bi, 0, i, 0)),
        compiler_params=_params("arbitrary", "arbitrary"),
        name="qkv",
    )(x, g.reshape(1, d), scale_a, shift_a, w_in, cos, sin)


NA_ROWS_PER_STEP = 8


def _na_kernel(q_ref, k_ref, v_ref, bias_ref, o_ref, *, rows):
    i = pl.program_id(2)
    lane = lax.broadcasted_iota(jnp.int32, (GRID_W, LANES), 1)
    upper = lane >= HEAD_DIM

    def row(j, carry):
        r = i * NA_ROWS_PER_STEP + j
        r0 = jnp.clip(r - NA_KH // 2, 0, rows - NA_KH)
        var = r - r0
        q = q_ref[0, 0, pl.ds(pl.multiple_of(j * GRID_W, GRID_W), GRID_W), :]
        start = pl.multiple_of(r0 * GRID_W, GRID_W)
        ks = k_ref[0, 0, pl.ds(start, NA_KH * GRID_W), :]
        vs = v_ref[0, 0, pl.ds(start, NA_KH * GRID_W), :]
        outs = []
        for hh in range(2):
            qm = jnp.where(upper if hh else ~upper, q, jnp.zeros_like(q))
            s = lax.dot_general(qm, ks, (((1,), (1,)), ((), ())), preferred_element_type=F32)
            s = s + jnp.concatenate([bias_ref[hh, NA_KH - 1 - var + 2 * t] for t in range(NA_KH // 2)], axis=-1)
            m = jnp.max(s, axis=-1, keepdims=True)
            e = jnp.exp(s - m)
            l = jnp.sum(e, axis=-1, keepdims=True)
            o = jnp.dot(e.astype(BF16), vs, preferred_element_type=F32)
            outs.append(o / l)
        out = jnp.where(upper, outs[1], outs[0])
        o_ref[0, 0, pl.ds(pl.multiple_of(j * GRID_W, GRID_W), GRID_W), :] = out.astype(BF16)
        return carry

    lax.fori_loop(0, NA_ROWS_PER_STEP, row, 0, unroll=True)


NA_BIAS_ROWS = 2 * NA_KH - 2
NA_RPB_COLS = 2 * NA_KW - 1


def _na_bias_kernel(rpb_ref, o_ref):
    h = pl.program_id(0)
    q = lax.broadcasted_iota(jnp.int32, (GRID_W, LANES), 0)
    lane = lax.broadcasted_iota(jnp.int32, (GRID_W, LANES), 1)
    kc = lane % GRID_W
    upper = lane >= GRID_W
    c0 = jnp.clip(q - NA_KW // 2, 0, GRID_W - NA_KW)
    in_window = (kc >= c0) & (kc < c0 + NA_KW)
    dc = kc - q + NA_KW - 1
    base = h * (NA_BIAS_ROWS + 1) * NA_RPB_COLS
    for d in range(NA_BIAS_ROWS):
        acc = jnp.full((GRID_W, LANES), NEG_INF, F32)
        for dd in range(NA_RPB_COLS):
            lo = rpb_ref[base + d * NA_RPB_COLS + dd]
            hi = rpb_ref[base + (d + 1) * NA_RPB_COLS + dd]
            acc = jnp.where(dc == dd, jnp.where(upper, hi, lo), acc)
        o_ref[0, d] = jnp.where(in_window, acc, NEG_INF)


def _na_bias(rpb):
    return pl.pallas_call(
        _na_bias_kernel,
        out_shape=jax.ShapeDtypeStruct((NA_HEADS, NA_BIAS_ROWS, GRID_W, LANES), F32),
        grid=(NA_HEADS,),
        in_specs=[pl.BlockSpec(memory_space=pltpu.SMEM)],
        out_specs=pl.BlockSpec((1, NA_BIAS_ROWS, GRID_W, LANES), lambda h: (h, 0, 0, 0)),
        compiler_params=_params("arbitrary"),
        name="na_bias",
    )(rpb.astype(F32).reshape(-1))


def _na(qkv, bias):
    b, _, s, _ = qkv.shape
    rows = s // GRID_W
    tq = NA_ROWS_PER_STEP * GRID_W
    return pl.pallas_call(
        functools.partial(_na_kernel, rows=rows),
        out_shape=jax.ShapeDtypeStruct((b, PAIRS, s, LANES), BF16),
        grid=(b, PAIRS, rows // NA_ROWS_PER_STEP),
        in_specs=[pl.BlockSpec((1, 1, tq, LANES), lambda bi, p, i: (bi, COL_NQ + p, i, 0)),
                  pl.BlockSpec((1, 1, s, LANES), lambda bi, p, i: (bi, COL_NK + p, 0, 0)),
                  pl.BlockSpec((1, 1, s, LANES), lambda bi, p, i: (bi, COL_NV + p, 0, 0)),
                  pl.BlockSpec((2, NA_BIAS_ROWS, GRID_W, LANES), lambda bi, p, i: (p, 0, 0, 0))],
        out_specs=pl.BlockSpec((1, 1, tq, LANES), lambda bi, p, i: (bi, p, i, 0)),
        compiler_params=_params("arbitrary", "arbitrary", "arbitrary"),
        name="na",
    )(qkv, qkv, qkv, bias)


SWA_KEYS = 3 * SWA_BLOCK
SWA_BLOCKS_PER_STEP = 4


def _swa_kernel(sink_ref, q_ref, k_ref, v_ref, o_ref, *, seq):
    p = pl.program_id(1)
    lane = lax.broadcasted_iota(jnp.int32, (SWA_BLOCK, LANES), 1)
    upper = lane >= HEAD_DIM
    q_off = lax.broadcasted_iota(jnp.int32, (SWA_BLOCK, SWA_KEYS), 0)
    k_off = lax.broadcasted_iota(jnp.int32, (SWA_BLOCK, SWA_KEYS), 1)
    for j in range(SWA_BLOCKS_PER_STEP):
        n = pl.program_id(2) * SWA_BLOCKS_PER_STEP + j
        start = pl.multiple_of(jnp.clip((n - 1) * SWA_BLOCK, 0, seq - SWA_KEYS), SWA_BLOCK)
        q = q_ref[0, 0, j * SWA_BLOCK:(j + 1) * SWA_BLOCK, :]
        ks = k_ref[0, 0, pl.ds(start, SWA_KEYS), :]
        vs = v_ref[0, 0, pl.ds(start, SWA_KEYS), :]
        in_window = jnp.abs((start + k_off) - (n * SWA_BLOCK + q_off)) <= SWA_WINDOW
        outs = []
        for hh in range(2):
            sink = sink_ref[2 * p + hh]
            qm = jnp.where(upper if hh else ~upper, q, jnp.zeros_like(q))
            s = lax.dot_general(qm, ks, (((1,), (1,)), ((), ())), preferred_element_type=F32)
            s = jnp.where(in_window, s, NEG_INF)
            m = jnp.maximum(jnp.max(s, axis=-1, keepdims=True), sink)
            e = jnp.exp(s - m)
            l = jnp.sum(e, axis=-1, keepdims=True) + jnp.exp(sink - m)
            o = jnp.dot(e.astype(BF16), vs, preferred_element_type=F32)
            outs.append(o / l)
        o_ref[0, 0, j * SWA_BLOCK:(j + 1) * SWA_BLOCK, :] = jnp.where(upper, outs[1], outs[0]).astype(BF16)


def _swa(qkv, sinks):
    b, _, s, _ = qkv.shape
    pairs_per_kv = PAIRS // SWA_KV_HEADS
    tq = SWA_BLOCKS_PER_STEP * SWA_BLOCK
    return pl.pallas_call(
        functools.partial(_swa_kernel, seq=s),
        out_shape=jax.ShapeDtypeStruct((b, PAIRS, s, LANES), BF16),
        grid=(b, PAIRS, s // tq),
        in_specs=[pl.BlockSpec(memory_space=pltpu.SMEM),
                  pl.BlockSpec((1, 1, tq, LANES), lambda bi, p, n: (bi, COL_SQ + p, n, 0)),
                  pl.BlockSpec((1, 1, s, LANES), lambda bi, p, n: (bi, COL_SK + p // pairs_per_kv, 0, 0)),
                  pl.BlockSpec((1, 1, s, LANES), lambda bi, p, n: (bi, COL_SV + p // pairs_per_kv, 0, 0))],
        out_specs=pl.BlockSpec((1, 1, tq, LANES), lambda bi, p, n: (bi, p, n, 0)),
        compiler_params=_params("arbitrary", "arbitrary", "arbitrary"),
        name="swa",
    )(sinks, qkv, qkv, qkv)


def _split3(v):
    hi = v.astype(BF16)
    r1 = v - hi.astype(F32)
    mid = r1.astype(BF16)
    lo = (r1 - mid.astype(F32)).astype(BF16)
    return hi, mid, lo


def _route(logits):
    lane = lax.broadcasted_iota(jnp.int32, logits.shape, 1)
    big = jnp.int32(LANES)
    gmask = lane < N_GROUPS
    gl = jnp.where(gmask, logits, NEG_INF)
    gmax = jnp.max(gl, axis=-1, keepdims=True)
    g_top = jnp.min(jnp.where(gmask & (gl == gmax), lane, big), axis=-1, keepdims=True)
    g_weight = 1.0 / jnp.sum(jnp.where(gmask, jnp.exp(gl - gmax), 0.0), axis=-1, keepdims=True)
    lo = ROUTER_BASE + g_top * EXPERTS_PER_GROUP
    emask = (lane >= lo) & (lane < lo + EXPERTS_PER_GROUP)
    el = jnp.where(emask, logits, NEG_INF)
    m1 = jnp.max(el, axis=-1, keepdims=True)
    i1 = jnp.min(jnp.where(emask & (el == m1), lane, big), axis=-1, keepdims=True)
    emask2 = emask & (lane != i1)
    el2 = jnp.where(emask2, logits, NEG_INF)
    m2 = jnp.max(el2, axis=-1, keepdims=True)
    i2 = jnp.min(jnp.where(emask2 & (el2 == m2), lane, big), axis=-1, keepdims=True)
    e2 = jnp.exp(m2 - m1)
    w1 = g_weight / (1.0 + e2)
    w2 = g_weight * e2 / (1.0 + e2)
    return i1 - ROUTER_BASE, i2 - ROUTER_BASE, w1, w2


def _to_token_tiles(ref, v, base=0):
    t = v.shape[0]
    for s in range(SUBLANES):
        ref[pl.ds(base + s, t, stride=SUBLANES), :] = v[:, s * LANES:(s + 1) * LANES]


def _from_token_tiles(ref, t, base=0):
    return jnp.concatenate([ref[pl.ds(base + s, t, stride=SUBLANES), :] for s in range(SUBLANES)], axis=-1)


def _mix_kernel(na_ref, sw_ref, x_ref, bna_ref, bsw_ref, wo_ref, gpm_ref, ga_ref, gpf_ref, scf_ref, shf_ref,
                wr_ref, x1_ref, h2_ref, r_ref, rt_ref, cntc_ref, cntr_ref):
    def heads(ref):
        return jnp.concatenate([ref[0, j] for j in range(PAIRS)], axis=-1).astype(F32)

    na = (_rms(heads(na_ref)) * bna_ref[...]).astype(BF16)
    sw = (_rms(heads(sw_ref)) * bsw_ref[...]).astype(BF16)
    mix = (jnp.dot(na, wo_ref[:NA_WIDTH, :], preferred_element_type=F32)
           + jnp.dot(sw, wo_ref[NA_WIDTH:, :], preferred_element_type=F32))
    x1 = x_ref[0] + ga_ref[0] * (_rms(mix) * gpm_ref[...])
    x1_ref[0] = x1
    h2 = (_rms(x1) * gpf_ref[...]) * (1.0 + scf_ref[0]) + shf_ref[0]
    _to_token_tiles(h2_ref, h2)
    h_hi, h_mid, h_lo = _split3(h2)
    w_hi, w_mid, w_lo = wr_ref[0], wr_ref[1], wr_ref[2]
    dot = functools.partial(jnp.dot, preferred_element_type=F32)
    logits = (dot(h_lo, w_hi) + dot(h_mid, w_mid) + dot(h_hi, w_lo)
              + dot(h_mid, w_hi) + dot(h_hi, w_mid) + dot(h_hi, w_hi))
    e1, e2, w1, w2 = _route(logits)
    lane = lax.broadcasted_iota(jnp.int32, logits.shape, 1)
    r = jnp.where(lane == R_E1, e1.astype(F32),
                  jnp.where(lane == R_E2, e2.astype(F32),
                            jnp.where(lane == R_W1, w1, jnp.where(lane == R_W2, w2, 0.0))))
    r_ref[...] = r
    rt = jnp.transpose(r)[:SUBLANES]
    rt_ref[...] = rt
    first_step = (pl.program_id(0) == 0) & (pl.program_id(1) == 0)

    @pl.when(first_step)
    def _():
        cntc_ref[...] = jnp.zeros_like(cntc_ref)
        cntr_ref[...] = jnp.zeros_like(cntr_ref)

    on_lane = ((lane == e1) | (lane == e2)).astype(F32)
    cntr_ref[...] += jnp.broadcast_to(jnp.sum(on_lane, axis=0, keepdims=True), cntr_ref.shape)
    sub = lax.broadcasted_iota(jnp.int32, (N_EXPERTS, rt.shape[1]), 0).astype(F32)
    on_sub = ((sub == rt[R_E1:R_E1 + 1]) | (sub == rt[R_E2:R_E2 + 1])).astype(F32)
    cntc_ref[...] += jnp.broadcast_to(jnp.sum(on_sub, axis=1, keepdims=True), cntc_ref.shape)


def _mix(na, sw, x, beta_na, beta_swa, w_out, g_post_mix, gate_a, g_pre_ffn, scale_f, shift_f, w_router3):
    b, s, d = x.shape
    tm = 512
    nt = s // tm
    row = lambda bi, i: (bi, i, 0)
    const2 = lambda bi, i: (0, 0)
    per_b = lambda bi, i: (bi, 0, 0)
    return pl.pallas_call(
        _mix_kernel,
        out_shape=(jax.ShapeDtypeStruct((b, s, d), F32),
                   jax.ShapeDtypeStruct((b * s * SUBLANES, LANES), F32),
                   jax.ShapeDtypeStruct((b * s, LANES), F32),
                   jax.ShapeDtypeStruct((SUBLANES, b * s), F32),
                   jax.ShapeDtypeStruct((N_EXPERTS, LANES), F32),
                   jax.ShapeDtypeStruct((SUBLANES, LANES), F32)),
        grid=(b, s // tm),
        in_specs=[pl.BlockSpec((1, PAIRS, tm, LANES), lambda bi, i: (bi, 0, i, 0)),
                  pl.BlockSpec((1, PAIRS, tm, LANES), lambda bi, i: (bi, 0, i, 0)),
                  pl.BlockSpec((1, tm, d), row),
                  pl.BlockSpec((1, NA_WIDTH), const2),
                  pl.BlockSpec((1, SWA_WIDTH), const2),
                  pl.BlockSpec((NA_WIDTH + SWA_WIDTH, d), const2),
                  pl.BlockSpec((1, d), const2),
                  pl.BlockSpec((1, 1, d), per_b),
                  pl.BlockSpec((1, d), const2),
                  pl.BlockSpec((1, 1, d), per_b),
                  pl.BlockSpec((1, 1, d), per_b),
                  pl.BlockSpec((3, d, LANES), lambda bi, i: (0, 0, 0))],
        out_specs=(pl.BlockSpec((1, tm, d), row),
                   pl.BlockSpec((tm * SUBLANES, LANES), lambda bi, i: (bi * nt + i, 0)),
                   pl.BlockSpec((tm, LANES), lambda bi, i: (bi * nt + i, 0)),
                   pl.BlockSpec((SUBLANES, tm), lambda bi, i: (0, bi * nt + i)),
                   pl.BlockSpec((N_EXPERTS, LANES), const2),
                   pl.BlockSpec((SUBLANES, LANES), const2)),
        compiler_params=_params("arbitrary", "arbitrary"),
        name="mix",
    )(na, sw, x, beta_na.reshape(1, -1), beta_swa.reshape(1, -1), w_out, g_post_mix.reshape(1, d), gate_a,
      g_pre_ffn.reshape(1, d), scale_f, shift_f, w_router3)


MOE_TILE = 256
PLAN_T = 512
I_TILE, I_EXPERT, I_LO, I_HI, I_FIRST = range(5)


def _plan_kernel(rt_ref, cntc_ref, cntr_ref, pos_ref, items_ref, start_ref, carry_ref, *, n_tiles, n_items):
    i = pl.program_id(0)
    sub = lax.broadcasted_iota(jnp.int32, (N_EXPERTS, LANES), 0)
    lane = lax.broadcasted_iota(jnp.int32, (N_EXPERTS, LANES), 1)

    @pl.when(i == 0)
    def _():
        c_col = cntc_ref[:, 0:1]
        c_row = cntr_ref[0:1, :]
        s_col = jnp.sum(jnp.where(lane < sub, c_row, 0.0), axis=1, keepdims=True)
        s_row = jnp.sum(jnp.where(sub < lane, c_col, 0.0), axis=0, keepdims=True)
        start_ref[...] = jnp.broadcast_to(s_col, start_ref.shape)
        carry_ref[...] = jnp.zeros_like(carry_ref)

        def tiles_of(s, c):
            first = jnp.floor(s * (1.0 / MOE_TILE))
            last = jnp.floor((s + c - 1.0) * (1.0 / MOE_TILE))
            return first, jnp.where(c > 0.0, last - first + 1.0, 0.0)

        f_col, n_col = tiles_of(s_col, c_col)
        _, n_row = tiles_of(s_row, c_row)
        i_col = jnp.sum(jnp.where(lane < sub, n_row, 0.0), axis=1, keepdims=True)
        total = jnp.sum(n_col, axis=0, keepdims=True)
        k = lax.broadcasted_iota(jnp.int32, (N_EXPERTS, n_items), 1).astype(F32)
        subk = lax.broadcasted_iota(jnp.int32, (N_EXPERTS, n_items), 0).astype(F32)
        ek = jnp.sum(jnp.where(i_col + n_col <= k, 1.0, 0.0), axis=0, keepdims=True)
        k0 = k[0:1]
        valid = k0 < total
        sel = subk == ek

        def pick(v):
            return jnp.sum(jnp.where(sel, v, 0.0), axis=0, keepdims=True)

        i_k, f_k, s_k, c_k = pick(i_col), pick(f_col), pick(s_col), pick(c_col)
        tile = f_k + (k0 - i_k)
        row0 = tile * MOE_TILE
        lo = jnp.maximum(s_k, row0) - row0
        hi = jnp.minimum(s_k + c_k, row0 + MOE_TILE) - row0
        rows = [jnp.where(valid, tile, n_tiles - 1.0), jnp.where(valid, ek, N_EXPERTS - 1.0),
                jnp.where(valid, lo, 0.0), jnp.where(valid, hi, 0.0),
                jnp.where(valid & (lo == 0.0), 1.0, 0.0)]
        rows += [jnp.zeros_like(k0)] * (SUBLANES - len(rows))
        items_ref[...] = jnp.concatenate(rows, axis=0).astype(jnp.int32)

    t = rt_ref.shape[1]
    e1 = rt_ref[R_E1:R_E1 + 1, :]
    e2 = rt_ref[R_E2:R_E2 + 1, :]
    sub_t = lax.broadcasted_iota(jnp.int32, (N_EXPERTS, t), 0).astype(F32)
    oh1 = sub_t == e1
    oh2 = sub_t == e2
    oh = (oh1 | oh2).astype(F32)
    before = (lax.broadcasted_iota(jnp.int32, (t, t), 0) < lax.broadcasted_iota(jnp.int32, (t, t), 1)).astype(BF16)
    rank = jnp.dot(oh.astype(BF16), before, preferred_element_type=F32)
    base = start_ref[:, 0:1] + carry_ref[:, 0:1] + rank
    pos1 = jnp.sum(jnp.where(oh1, base, 0.0), axis=0, keepdims=True)
    pos2 = jnp.sum(jnp.where(oh2, base, 0.0), axis=0, keepdims=True)
    carry_ref[...] += jnp.broadcast_to(jnp.sum(oh, axis=1, keepdims=True), carry_ref.shape)
    pos = jnp.concatenate([pos1, pos2] + [jnp.zeros_like(pos1)] * (SUBLANES - 2), axis=0)
    pos_ref[...] = pos.astype(jnp.int32)


def _plan(rt, cntc, cntr):
    n = rt.shape[1]
    n_tiles = 2 * n // MOE_TILE
    n_items = 2 * LANES
    assert n_tiles + N_EXPERTS <= n_items
    return pl.pallas_call(
        functools.partial(_plan_kernel, n_tiles=n_tiles, n_items=n_items),
        out_shape=(jax.ShapeDtypeStruct((SUBLANES, n), jnp.int32),
                   jax.ShapeDtypeStruct((SUBLANES, n_items), jnp.int32)),
        grid=(n // PLAN_T,),
        in_specs=[pl.BlockSpec((SUBLANES, PLAN_T), lambda i: (0, i)),
                  pl.BlockSpec((N_EXPERTS, LANES), lambda i: (0, 0)),
                  pl.BlockSpec((SUBLANES, LANES), lambda i: (0, 0))],
        out_specs=(pl.BlockSpec((SUBLANES, PLAN_T), lambda i: (0, i)),
                   pl.BlockSpec((SUBLANES, n_items), lambda i: (0, 0))),
        scratch_shapes=[pltpu.VMEM((N_EXPERTS, LANES), F32), pltpu.VMEM((N_EXPERTS, LANES), F32)],
        compiler_params=_params("arbitrary"),
        name="plan",
    )(rt, cntc, cntr)


DISPATCH_T = 256


def _token_rows(ref, index):
    return ref.at[pl.ds(pl.multiple_of(index * SUBLANES, SUBLANES), SUBLANES), :]


def _dispatch_kernel(pos_ref, h_hbm, xs_hbm, sem):
    i = pl.program_id(0)
    slot = i % 2

    def body(r, carry):
        src = _token_rows(h_hbm, i * DISPATCH_T + r)
        for k in range(2):
            pltpu.make_async_copy(src, _token_rows(xs_hbm, pos_ref[k, r]), sem.at[slot]).start()
        return carry

    lax.fori_loop(0, DISPATCH_T, body, 0, unroll=8)

    def wait_step(sl):
        rows = 2 * DISPATCH_T * SUBLANES
        pltpu.make_async_copy(h_hbm.at[pl.ds(0, rows), :], xs_hbm.at[pl.ds(0, rows), :], sem.at[sl]).wait()

    @pl.when(i > 0)
    def _():
        wait_step(1 - slot)

    @pl.when(i == pl.num_programs(0) - 1)
    def _():
        wait_step(slot)


def _dispatch(pos, h2t):
    n = pos.shape[1]
    return pl.pallas_call(
        _dispatch_kernel,
        out_shape=jax.ShapeDtypeStruct((2 * n * SUBLANES, LANES), F32),
        grid=(n // DISPATCH_T,),
        in_specs=[pl.BlockSpec((SUBLANES, DISPATCH_T), lambda i: (0, i), memory_space=pltpu.SMEM),
                  pl.BlockSpec(memory_space=pl.ANY)],
        out_specs=pl.BlockSpec(memory_space=pl.ANY),
        scratch_shapes=[pltpu.SemaphoreType.DMA((2,))],
        compiler_params=_params("arbitrary"),
        name="dispatch",
    )(pos, h2t)


def _expert_kernel(tile_ref, exp_ref, lo_ref, hi_ref, first_ref, xs_ref, wg_ref, wu_ref, wd_ref, o_ref):
    k = pl.program_id(0)
    lo = lo_ref[k]
    hi = hi_ref[k]

    @pl.when(first_ref[k] == 1)
    def _():
        o_ref[...] = jnp.zeros_like(o_ref)

    @pl.when(hi > lo)
    def _():
        x = _from_token_tiles(xs_ref, MOE_TILE).astype(BF16)
        gate = jnp.dot(x, wg_ref[0].astype(BF16), preferred_element_type=F32)
        up = jnp.dot(x, wu_ref[0].astype(BF16), preferred_element_type=F32)
        he = (gate * jax.nn.sigmoid(gate) * up).astype(BF16)
        ye = jnp.dot(he, wd_ref[0].astype(BF16), preferred_element_type=F32)
        row = lax.broadcasted_iota(jnp.int32, (MOE_TILE, LANES), 0)
        mine = (row >= lo) & (row < hi)
        for s in range(SUBLANES):
            rows = pl.ds(s, MOE_TILE, stride=SUBLANES)
            o_ref[rows, :] = jnp.where(mine, ye[:, s * LANES:(s + 1) * LANES], o_ref[rows, :])


def _experts(items, xs, w_gate, w_up, w_down):
    n_rows = xs.shape[0] // SUBLANES
    n_items = n_rows // MOE_TILE + N_EXPERTS
    d = w_gate.shape[1]
    tile_map = lambda k, tile, exp, lo, hi, first: (tile[k], 0)
    exp_map = lambda k, tile, exp, lo, hi, first: (exp[k], 0, 0)
    grid_spec = pltpu.PrefetchScalarGridSpec(
        num_scalar_prefetch=5,
        grid=(n_items,),
        in_specs=[pl.BlockSpec((MOE_TILE * SUBLANES, LANES), tile_map),
                  pl.BlockSpec((1, d, EXPERT_FF), exp_map),
                  pl.BlockSpec((1, d, EXPERT_FF), exp_map),
                  pl.BlockSpec((1, EXPERT_FF, d), exp_map)],
        out_specs=pl.BlockSpec((MOE_TILE * SUBLANES, LANES), tile_map))
    return pl.pallas_call(
        _expert_kernel,
        out_shape=jax.ShapeDtypeStruct(xs.shape, F32),
        grid_spec=grid_spec,
        compiler_params=_params("arbitrary"),
        name="experts",
    )(*(items[j, :n_items] for j in (I_TILE, I_EXPERT, I_LO, I_HI, I_FIRST)), xs, w_gate, w_up, w_down)


COMBINE_T = 256


def _combine_kernel(pos_ref, posn_ref, ys_hbm, r_ref, x1_ref, gf_ref, gpost_ref, o_ref, buf, sem):
    i = pl.program_id(0)
    n = pl.num_programs(0)
    slot = i % 2
    slot_rows = COMBINE_T * SUBLANES

    def issue(p_ref, sl):
        def body(r, carry):
            for k in range(2):
                dst = buf.at[sl, pl.ds(pl.multiple_of(k * slot_rows + r * SUBLANES, SUBLANES), SUBLANES), :]
                pltpu.make_async_copy(_token_rows(ys_hbm, p_ref[k, r]), dst, sem.at[sl]).start()
            return carry

        lax.fori_loop(0, COMBINE_T, body, 0, unroll=8)

    @pl.when(i == 0)
    def _():
        issue(pos_ref, 0)

    @pl.when(i + 1 < n)
    def _():
        issue(posn_ref, 1 - slot)

    pltpu.make_async_copy(ys_hbm.at[pl.ds(0, 2 * slot_rows), :], buf.at[slot], sem.at[slot]).wait()
    ya = _from_token_tiles(buf.at[slot], COMBINE_T)
    yb = _from_token_tiles(buf.at[slot], COMBINE_T, base=slot_rows)
    r = r_ref[...]
    y = r[:, R_W1:R_W1 + 1] * ya + r[:, R_W2:R_W2 + 1] * yb
    o_ref[0] = x1_ref[0] + gf_ref[0] * (_rms(y) * gpost_ref[...])


def _combine(pos, ys, r, x1, gate_f, g_post_ffn):
    b, s, d = x1.shape
    nt = s // COMBINE_T
    n_steps = b * nt
    row = lambda i: (i // nt, i % nt, 0)
    pos_spec = lambda f: pl.BlockSpec((SUBLANES, COMBINE_T), lambda i: (0, f(i)), memory_space=pltpu.SMEM)
    return pl.pallas_call(
        _combine_kernel,
        out_shape=jax.ShapeDtypeStruct((b, s, d), F32),
        grid=(n_steps,),
        in_specs=[pos_spec(lambda i: i),
                  pos_spec(lambda i: jnp.minimum(i + 1, n_steps - 1)),
                  pl.BlockSpec(memory_space=pl.ANY),
                  pl.BlockSpec((COMBINE_T, LANES), lambda i: (i, 0)),
                  pl.BlockSpec((1, COMBINE_T, d), row),
                  pl.BlockSpec((1, 1, d), lambda i: (i // nt, 0, 0)),
                  pl.BlockSpec((1, d), lambda i: (0, 0))],
        out_specs=pl.BlockSpec((1, COMBINE_T, d), row),
        scratch_shapes=[pltpu.VMEM((2, 2 * COMBINE_T * SUBLANES, LANES), F32), pltpu.SemaphoreType.DMA((2,))],
        compiler_params=_params("arbitrary"),
        name="combine",
    )(pos, pos, ys, r, x1, gate_f, g_post_ffn.reshape(1, d))


def _rope_tables(s):
    half = HEAD_DIM // 2
    inv = ROPE_THETA ** (-jnp.arange(half, dtype=F32) * 2.0 / HEAD_DIM)
    ang = jnp.arange(s).astype(F32)[:, None] * inv[None, :]
    cos = jnp.cos(ang)
    sin = jnp.sin(ang)
    cos = jnp.concatenate([cos, cos, cos, cos], axis=-1)
    sin_signed = jnp.concatenate([-sin, sin, -sin, sin], axis=-1)
    return cos, sin_signed


def _router_weights(w_group_router, w_expert_router):
    d = w_group_router.shape[0]
    we = jnp.transpose(w_expert_router, (1, 0, 2)).reshape(d, N_EXPERTS)
    w = jnp.concatenate([w_group_router, we, jnp.zeros((d, LANES - N_GROUPS - N_EXPERTS), F32)], axis=-1)
    return jnp.stack(_split3(w))


def kernel(x, c, w_ada, b_ada, g_pre_mix, w_in, na_rpb, swa_sinks, beta_na, beta_swa, w_out, g_post_mix, g_pre_ffn,
           w_group_router, w_expert_router, w_gate, w_up, w_down, g_post_ffn):
    b, s, d = x.shape
    depth = w_ada.shape[0]
    cos, sin_signed = _rope_tables(s)
    for l in range(depth):
        mod = _adaln(c, w_ada[l], b_ada[l]).reshape(b, N_MOD, 1, d)
        shift_a, scale_a, gate_a, shift_f, scale_f, gate_f = (mod[:, k] for k in range(N_MOD))
        qkv = _qkv(x, g_pre_mix[l], scale_a, shift_a, w_in[l].astype(BF16), cos, sin_signed)
        na = _na(qkv, _na_bias(na_rpb[l]))
        sw = _swa(qkv, swa_sinks[l])
        x1, h2t, r, rt, cntc, cntr = _mix(na, sw, x, beta_na[l], beta_swa[l], w_out[l].astype(BF16), g_post_mix[l],
                                          gate_a, g_pre_ffn[l], scale_f, shift_f,
                                          _router_weights(w_group_router[l], w_expert_router[l]))
        pos, items = _plan(rt, cntc, cntr)
        xs = _dispatch(pos, h2t)
        ys = _experts(items, xs, w_gate[l], w_up[l], w_down[l])
        x = _combine(pos, ys, r, x1, gate_f, g_post_ffn[l])
    return x
```

```python
import functools

import jax
import jax.numpy as jnp
from jax import lax
from jax.experimental import pallas as pl
from jax.experimental.pallas import tpu as pltpu

D_MODEL = 1024
GRID_W = 64
HEAD_DIM = 64
NA_HEADS = 8
NA_KH = 8
NA_KW = 16
SWA_HEADS = 8
SWA_KV_HEADS = 2
SWA_WINDOW = 128
SWA_BLOCK = 128
ROPE_THETA = 10000.0
NA_WIDTH = NA_HEADS * HEAD_DIM
SWA_WIDTH = SWA_HEADS * HEAD_DIM
N_GROUPS = 4
EXPERTS_PER_GROUP = 8
N_EXPERTS = N_GROUPS * EXPERTS_PER_GROUP
EXPERT_FF = 256
N_MOD = 6
EPS = 1e-6
NEG_INF = -1e30

LANES = 128
SUBLANES = 8
R_E1, R_E2, R_W1, R_W2 = range(4)
PAIRS = NA_HEADS // 2
COL_NQ, COL_NK, COL_NV, COL_SQ, COL_SK, COL_SV = 0, 4, 8, 12, 16, 18
QKV_TILES = 20
ROUTER_BASE = N_GROUPS
VMEM_LIMIT = 56 * 1024 * 1024

F32 = jnp.float32
BF16 = jnp.bfloat16


def _rms(v):
    return v * lax.rsqrt(jnp.mean(v * v, axis=-1, keepdims=True) + EPS)


def _params(*sem):
    return pltpu.CompilerParams(dimension_semantics=sem, vmem_limit_bytes=VMEM_LIMIT)


def _adaln_kernel(c_ref, w_ref, b_ref, o_ref):
    c = c_ref[...]
    a = c * jax.nn.sigmoid(c)
    o_ref[...] = jnp.dot(a, w_ref[...], precision=lax.Precision.HIGHEST,
                         preferred_element_type=F32) + b_ref[...]


def _adaln(c, w_ada, b_ada):
    batch, d = c.shape
    n = w_ada.shape[1]
    tn = 1024
    b = 8
    c = jnp.pad(c, ((0, b - batch), (0, 0)))
    return pl.pallas_call(
        _adaln_kernel,
        out_shape=jax.ShapeDtypeStruct((b, n), F32),
        grid=(n // tn,),
        in_specs=[pl.BlockSpec((b, d), lambda j: (0, 0)),
                  pl.BlockSpec((d, tn), lambda j: (0, j)),
                  pl.BlockSpec((1, tn), lambda j: (0, j))],
        out_specs=pl.BlockSpec((b, tn), lambda j: (0, j)),
        compiler_params=_params("arbitrary"),
        name="adaln",
    )(c, w_ada, b_ada.reshape(1, n))[:batch]


def _rope(v, cos, sin_signed, first_half):
    rot = jnp.where(first_half, pltpu.roll(v, LANES - HEAD_DIM // 2, 1), pltpu.roll(v, HEAD_DIM // 2, 1))
    return v * cos + rot * sin_signed


def _qkv_kernel(x_ref, g_ref, sc_ref, sh_ref, w_ref, cos_ref, sin_ref, o_ref):
    x = x_ref[0]
    h = (_rms(x) * g_ref[...]) * (1.0 + sc_ref[0]) + sh_ref[0]
    h = h.astype(BF16)
    scale = HEAD_DIM ** -0.5
    cos = cos_ref[...]
    sin = sin_ref[...]
    lane = lax.broadcasted_iota(jnp.int32, cos.shape, 1)
    first_half = (lane % HEAD_DIM) < HEAD_DIM // 2
    upper = lane >= HEAD_DIM

    def proj(col, width):
        return jnp.dot(h, w_ref[:, col * LANES:(col + width) * LANES], preferred_element_type=F32)

    def tile(v, j):
        return v[:, j * LANES:(j + 1) * LANES]

    nq, nk, nv, sq = proj(COL_NQ, 4), proj(COL_NK, 4), proj(COL_NV, 4), proj(COL_SQ, 4)
    for j in range(PAIRS):
        o_ref[0, COL_NQ + j] = (tile(nq, j) * scale).astype(BF16)
        o_ref[0, COL_NK + j] = tile(nk, j).astype(BF16)
        o_ref[0, COL_NV + j] = tile(nv, j).astype(BF16)
        o_ref[0, COL_SQ + j] = (_rope(tile(sq, j), cos, sin, first_half) * scale).astype(BF16)
    skv = proj(COL_SK, 2)
    k = _rope(tile(skv, 0), cos, sin, first_half)
    v = tile(skv, 1)
    for t, col in ((k, COL_SK), (v, COL_SV)):
        swapped = pltpu.roll(t, HEAD_DIM, 1)
        o_ref[0, col] = jnp.where(upper, swapped, t).astype(BF16)
        o_ref[0, col + 1] = jnp.where(upper, t, swapped).astype(BF16)


def _qkv(x, g, scale_a, shift_a, w_in, cos, sin):
    b, s, d = x.shape
    tm = 512
    n_in = w_in.shape[1]
    return pl.pallas_call(
        _qkv_kernel,
        out_shape=jax.ShapeDtypeStruct((b, QKV_TILES, s, LANES), BF16),
        grid=(b, s // tm),
        in_specs=[pl.BlockSpec((1, tm, d), lambda bi, i: (bi, i, 0)),
                  pl.BlockSpec((1, d), lambda bi, i: (0, 0)),
                  pl.BlockSpec((1, 1, d), lambda bi, i: (bi, 0, 0)),
                  pl.BlockSpec((1, 1, d), lambda bi, i: (bi, 0, 0)),
                  pl.BlockSpec((d, n_in), lambda bi, i: (0, 0)),
                  pl.BlockSpec((tm, LANES), lambda bi, i: (i, 0)),
                  pl.BlockSpec((tm, LANES), lambda bi, i: (i, 0))],
        out_specs=pl.BlockSpec((1, QKV_TILES, tm, LANES), lambda bi, i: (bi, 0, i, 0)),
        compiler_params=_params("arbitrary", "arbitrary"),
        name="qkv",
    )(x, g.reshape(1, d), scale_a, shift_a, w_in, cos, sin)


NA_QROWS = 4
NA_KROWS = NA_QROWS + NA_KH
NA_BLOCKS_PER_STEP = 4
NA_Q = NA_QROWS * GRID_W
NA_K = NA_KROWS * GRID_W
NA_RPB_ROWS = 2 * NA_KH - 1
NA_RPB_COLS = 2 * NA_KW - 1
NA_TYPES = 3


def _na_first_key_row(block, rows, clip):
    return clip(block * NA_QROWS - NA_KH // 2, 0, rows - NA_KROWS)


def _na_kernel(q_ref, k_ref, v_ref, bias_ref, o_ref, *, rows):
    lane = lax.broadcasted_iota(jnp.int32, (NA_Q, LANES), 1)
    upper = lane >= HEAD_DIM
    for c in range(NA_BLOCKS_PER_STEP):
        block = pl.program_id(2) * NA_BLOCKS_PER_STEP + c
        a = _na_first_key_row(block, rows, jnp.clip)
        ty = jnp.where(block == 0, 0, jnp.where(block == rows // NA_QROWS - 1, 2, 1))
        start = pl.multiple_of(a * GRID_W, GRID_W)
        q = q_ref[0, 0, c * NA_Q:(c + 1) * NA_Q, :]
        ks = k_ref[0, 0, pl.ds(start, NA_K), :]
        vs = v_ref[0, 0, pl.ds(start, NA_K), :]
        outs = []
        for hh in range(2):
            qm = jnp.where(upper if hh else ~upper, q, jnp.zeros_like(q))
            s = lax.dot_general(qm, ks, (((1,), (1,)), ((), ())), preferred_element_type=F32)
            s = s + bias_ref[hh, ty]
            m = jnp.max(s, axis=-1, keepdims=True)
            e = jnp.exp(s - m)
            l = jnp.sum(e, axis=-1, keepdims=True)
            o = jnp.dot(e.astype(BF16), vs, preferred_element_type=F32)
            outs.append(o / l)
        o_ref[0, 0, c * NA_Q:(c + 1) * NA_Q, :] = jnp.where(upper, outs[1], outs[0]).astype(BF16)


def _na_bias_kernel(rpb_ref, o_ref, *, rows):
    h = pl.program_id(0)
    q = lax.broadcasted_iota(jnp.int32, (GRID_W, LANES), 0)
    lane = lax.broadcasted_iota(jnp.int32, (GRID_W, LANES), 1)
    kc = lane % GRID_W
    upper = lane >= GRID_W
    c0 = jnp.clip(q - NA_KW // 2, 0, GRID_W - NA_KW)
    in_cols = (kc >= c0) & (kc < c0 + NA_KW)
    dc = kc - q + NA_KW - 1
    neg = jnp.full((GRID_W, LANES), NEG_INF, F32)
    base = h * NA_RPB_ROWS * NA_RPB_COLS
    by_row_offset = []
    for d in range(NA_RPB_ROWS):
        acc = neg
        for dd in range(NA_RPB_COLS):
            acc = jnp.where(dc == dd, rpb_ref[base + d * NA_RPB_COLS + dd], acc)
        by_row_offset.append(jnp.where(in_cols, acc, NEG_INF))
    clamp = lambda v, lo, hi: min(max(v, lo), hi)
    for ty, block in enumerate((0, 1, rows // NA_QROWS - 1)):
        r = block * NA_QROWS
        a = _na_first_key_row(block, rows, clamp)
        for j in range(NA_QROWS):
            r0 = min(max(r + j - NA_KH // 2, 0), rows - NA_KH)
            halves = []
            for i in range(NA_KROWS):
                in_rows = r0 <= a + i < r0 + NA_KH
                halves.append(by_row_offset[a + i - (r + j) + NA_KH - 1] if in_rows else neg)
            for t in range(NA_KROWS // 2):
                tile = jnp.where(upper, halves[2 * t + 1], halves[2 * t])
                o_ref[0, ty, j * GRID_W:(j + 1) * GRID_W, t * LANES:(t + 1) * LANES] = tile


def _na_bias(rpb, rows):
    return pl.pallas_call(
        functools.partial(_na_bias_kernel, rows=rows),
        out_shape=jax.ShapeDtypeStruct((NA_HEADS, NA_TYPES, NA_Q, NA_K), F32),
        grid=(NA_HEADS,),
        in_specs=[pl.BlockSpec(memory_space=pltpu.SMEM)],
        out_specs=pl.BlockSpec((1, NA_TYPES, NA_Q, NA_K), lambda h: (h, 0, 0, 0)),
        compiler_params=_params("arbitrary"),
        name="na_bias",
    )(rpb.astype(F32).reshape(-1))


def _na(qkv, rpb):
    b, _, s, _ = qkv.shape
    rows = s // GRID_W
    assert rows % (NA_QROWS * NA_BLOCKS_PER_STEP) == 0 and rows // NA_QROWS >= NA_TYPES
    tq = NA_BLOCKS_PER_STEP * NA_Q
    return pl.pallas_call(
        functools.partial(_na_kernel, rows=rows),
        out_shape=jax.ShapeDtypeStruct((b, PAIRS, s, LANES), BF16),
        grid=(b, PAIRS, s // tq),
        in_specs=[pl.BlockSpec((1, 1, tq, LANES), lambda bi, p, i: (bi, COL_NQ + p, i, 0)),
                  pl.BlockSpec((1, 1, s, LANES), lambda bi, p, i: (bi, COL_NK + p, 0, 0)),
                  pl.BlockSpec((1, 1, s, LANES), lambda bi, p, i: (bi, COL_NV + p, 0, 0)),
                  pl.BlockSpec((2, NA_TYPES, NA_Q, NA_K), lambda bi, p, i: (p, 0, 0, 0))],
        out_specs=pl.BlockSpec((1, 1, tq, LANES), lambda bi, p, i: (bi, p, i, 0)),
        compiler_params=_params("arbitrary", "arbitrary", "arbitrary"),
        name="na",
    )(qkv, qkv, qkv, _na_bias(rpb, rows))


SWA_KEYS = 3 * SWA_BLOCK
SWA_BLOCKS_PER_STEP = 4
SWA_GROUP = SWA_HEADS // SWA_KV_HEADS
assert COL_SQ % (PAIRS // SWA_KV_HEADS) == 0


def _swa_kernel(sink_ref, q_ref, k_ref, v_ref, o_ref, *, seq):
    kv = pl.program_id(1)
    rows = SWA_GROUP * SWA_BLOCK
    lane = lax.broadcasted_iota(jnp.int32, (SWA_BLOCK, LANES), 1)
    upper = lane >= HEAD_DIM
    row = lax.broadcasted_iota(jnp.int32, (rows, SWA_KEYS), 0)
    q_off = row % SWA_BLOCK
    k_off = lax.broadcasted_iota(jnp.int32, (rows, SWA_KEYS), 1)
    head = lax.broadcasted_iota(jnp.int32, (rows, 1), 0) // SWA_BLOCK
    sink = jnp.zeros((rows, 1), F32)
    for g in range(SWA_GROUP):
        sink = jnp.where(head == g, sink_ref[kv * SWA_GROUP + g], sink)
    for j in range(SWA_BLOCKS_PER_STEP):
        n = pl.program_id(2) * SWA_BLOCKS_PER_STEP + j
        start = pl.multiple_of(jnp.clip((n - 1) * SWA_BLOCK, 0, seq - SWA_KEYS), SWA_BLOCK)
        ks = k_ref[0, 0, pl.ds(start, SWA_KEYS), :]
        vs = v_ref[0, 0, pl.ds(start, SWA_KEYS), :]
        qs = []
        for g in range(SWA_GROUP):
            q = q_ref[0, g // 2, j * SWA_BLOCK:(j + 1) * SWA_BLOCK, :]
            qs.append(jnp.where(upper if g % 2 else ~upper, q, jnp.zeros_like(q)))
        s = lax.dot_general(jnp.concatenate(qs, axis=0), ks, (((1,), (1,)), ((), ())), preferred_element_type=F32)
        in_window = jnp.abs((start + k_off) - (n * SWA_BLOCK + q_off)) <= SWA_WINDOW
        s = jnp.where(in_window, s, NEG_INF)
        m = jnp.maximum(jnp.max(s, axis=-1, keepdims=True), sink)
        e = jnp.exp(s - m)
        l = jnp.sum(e, axis=-1, keepdims=True) + jnp.exp(sink - m)
        o = jnp.dot(e.astype(BF16), vs, preferred_element_type=F32) / l
        for pair in range(SWA_GROUP // 2):
            even = o[(2 * pair) * SWA_BLOCK:(2 * pair + 1) * SWA_BLOCK]
            odd = o[(2 * pair + 1) * SWA_BLOCK:(2 * pair + 2) * SWA_BLOCK]
            o_ref[0, pair, j * SWA_BLOCK:(j + 1) * SWA_BLOCK, :] = jnp.where(upper, odd, even).astype(BF16)


def _swa(qkv, sinks):
    b, _, s, _ = qkv.shape
    pairs_per_kv = PAIRS // SWA_KV_HEADS
    tq = SWA_BLOCKS_PER_STEP * SWA_BLOCK
    return pl.pallas_call(
        functools.partial(_swa_kernel, seq=s),
        out_shape=jax.ShapeDtypeStruct((b, PAIRS, s, LANES), BF16),
        grid=(b, SWA_KV_HEADS, s // tq),
        in_specs=[pl.BlockSpec(memory_space=pltpu.SMEM),
                  pl.BlockSpec((1, pairs_per_kv, tq, LANES),
                               lambda bi, kv, n: (bi, COL_SQ // pairs_per_kv + kv, n, 0)),
                  pl.BlockSpec((1, 1, s, LANES), lambda bi, kv, n: (bi, COL_SK + kv, 0, 0)),
                  pl.BlockSpec((1, 1, s, LANES), lambda bi, kv, n: (bi, COL_SV + kv, 0, 0))],
        out_specs=pl.BlockSpec((1, pairs_per_kv, tq, LANES), lambda bi, kv, n: (bi, kv, n, 0)),
        compiler_params=_params("arbitrary", "arbitrary", "arbitrary"),
        name="swa",
    )(sinks, qkv, qkv, qkv)


def _split3(v):
    hi = v.astype(BF16)
    r1 = v - hi.astype(F32)
    mid = r1.astype(BF16)
    lo = (r1 - mid.astype(F32)).astype(BF16)
    return hi, mid, lo


def _route(logits):
    lane = lax.broadcasted_iota(jnp.int32, logits.shape, 1)
    big = jnp.int32(LANES)
    gmask = lane < N_GROUPS
    gl = jnp.where(gmask, logits, NEG_INF)
    gmax = jnp.max(gl, axis=-1, keepdims=True)
    g_top = jnp.min(jnp.where(gmask & (gl == gmax), lane, big), axis=-1, keepdims=True)
    g_weight = 1.0 / jnp.sum(jnp.where(gmask, jnp.exp(gl - gmax), 0.0), axis=-1, keepdims=True)
    lo = ROUTER_BASE + g_top * EXPERTS_PER_GROUP
    emask = (lane >= lo) & (lane < lo + EXPERTS_PER_GROUP)
    el = jnp.where(emask, logits, NEG_INF)
    m1 = jnp.max(el, axis=-1, keepdims=True)
    i1 = jnp.min(jnp.where(emask & (el == m1), lane, big), axis=-1, keepdims=True)
    emask2 = emask & (lane != i1)
    el2 = jnp.where(emask2, logits, NEG_INF)
    m2 = jnp.max(el2, axis=-1, keepdims=True)
    i2 = jnp.min(jnp.where(emask2 & (el2 == m2), lane, big), axis=-1, keepdims=True)
    e2 = jnp.exp(m2 - m1)
    w1 = g_weight / (1.0 + e2)
    w2 = g_weight * e2 / (1.0 + e2)
    return i1 - ROUTER_BASE, i2 - ROUTER_BASE, w1, w2


def _to_token_tiles(ref, v, base=0):
    t = v.shape[0]
    for s in range(SUBLANES):
        ref[pl.ds(base + s, t, stride=SUBLANES), :] = v[:, s * LANES:(s + 1) * LANES]


def _from_token_tiles(ref, t, base=0):
    return jnp.concatenate([ref[pl.ds(base + s, t, stride=SUBLANES), :] for s in range(SUBLANES)], axis=-1)


def _mix_kernel(na_ref, sw_ref, x_ref, bna_ref, bsw_ref, wo_ref, gpm_ref, ga_ref, gpf_ref, scf_ref, shf_ref,
                wr_ref, x1_ref, h2_ref, r_ref, rt_ref, cntc_ref, cntr_ref):
    def heads(ref):
        return jnp.concatenate([ref[0, j] for j in range(PAIRS)], axis=-1).astype(F32)

    na = (_rms(heads(na_ref)) * bna_ref[...]).astype(BF16)
    sw = (_rms(heads(sw_ref)) * bsw_ref[...]).astype(BF16)
    mix = (jnp.dot(na, wo_ref[:NA_WIDTH, :], preferred_element_type=F32)
           + jnp.dot(sw, wo_ref[NA_WIDTH:, :], preferred_element_type=F32))
    x1 = x_ref[0] + ga_ref[0] * (_rms(mix) * gpm_ref[...])
    x1_ref[0] = x1
    h2 = (_rms(x1) * gpf_ref[...]) * (1.0 + scf_ref[0]) + shf_ref[0]
    _to_token_tiles(h2_ref, h2)
    h_hi, h_mid, h_lo = _split3(h2)
    w_hi, w_mid, w_lo = wr_ref[0], wr_ref[1], wr_ref[2]
    dot = functools.partial(jnp.dot, preferred_element_type=F32)
    logits = (dot(h_lo, w_hi) + dot(h_mid, w_mid) + dot(h_hi, w_lo)
              + dot(h_mid, w_hi) + dot(h_hi, w_mid) + dot(h_hi, w_hi))
    e1, e2, w1, w2 = _route(logits)
    lane = lax.broadcasted_iota(jnp.int32, logits.shape, 1)
    r = jnp.where(lane == R_E1, e1.astype(F32),
                  jnp.where(lane == R_E2, e2.astype(F32),
                            jnp.where(lane == R_W1, w1, jnp.where(lane == R_W2, w2, 0.0))))
    r_ref[...] = r
    rt = jnp.transpose(r)[:SUBLANES]
    rt_ref[...] = rt
    first_step = (pl.program_id(0) == 0) & (pl.program_id(1) == 0)

    @pl.when(first_step)
    def _():
        cntc_ref[...] = jnp.zeros_like(cntc_ref)
        cntr_ref[...] = jnp.zeros_like(cntr_ref)

    on_lane = ((lane == e1) | (lane == e2)).astype(F32)
    cntr_ref[...] += jnp.broadcast_to(jnp.sum(on_lane, axis=0, keepdims=True), cntr_ref.shape)
    sub = lax.broadcasted_iota(jnp.int32, (N_EXPERTS, rt.shape[1]), 0).astype(F32)
    on_sub = ((sub == rt[R_E1:R_E1 + 1]) | (sub == rt[R_E2:R_E2 + 1])).astype(F32)
    cntc_ref[...] += jnp.broadcast_to(jnp.sum(on_sub, axis=1, keepdims=True), cntc_ref.shape)


def _mix(na, sw, x, beta_na, beta_swa, w_out, g_post_mix, gate_a, g_pre_ffn, scale_f, shift_f, w_router3):
    b, s, d = x.shape
    tm = 512
    nt = s // tm
    row = lambda bi, i: (bi, i, 0)
    const2 = lambda bi, i: (0, 0)
    per_b = lambda bi, i: (bi, 0, 0)
    return pl.pallas_call(
        _mix_kernel,
        out_shape=(jax.ShapeDtypeStruct((b, s, d), F32),
                   jax.ShapeDtypeStruct((b * s * SUBLANES, LANES), F32),
                   jax.ShapeDtypeStruct((b * s, LANES), F32),
                   jax.ShapeDtypeStruct((SUBLANES, b * s), F32),
                   jax.ShapeDtypeStruct((N_EXPERTS, LANES), F32),
                   jax.ShapeDtypeStruct((SUBLANES, LANES), F32)),
        grid=(b, s // tm),
        in_specs=[pl.BlockSpec((1, PAIRS, tm, LANES), lambda bi, i: (bi, 0, i, 0)),
                  pl.BlockSpec((1, PAIRS, tm, LANES), lambda bi, i: (bi, 0, i, 0)),
                  pl.BlockSpec((1, tm, d), row),
                  pl.BlockSpec((1, NA_WIDTH), const2),
                  pl.BlockSpec((1, SWA_WIDTH), const2),
                  pl.BlockSpec((NA_WIDTH + SWA_WIDTH, d), const2),
                  pl.BlockSpec((1, d), const2),
                  pl.BlockSpec((1, 1, d), per_b),
                  pl.BlockSpec((1, d), const2),
                  pl.BlockSpec((1, 1, d), per_b),
                  pl.BlockSpec((1, 1, d), per_b),
                  pl.BlockSpec((3, d, LANES), lambda bi, i: (0, 0, 0))],
        out_specs=(pl.BlockSpec((1, tm, d), row),
                   pl.BlockSpec((tm * SUBLANES, LANES), lambda bi, i: (bi * nt + i, 0)),
                   pl.BlockSpec((tm, LANES), lambda bi, i: (bi * nt + i, 0)),
                   pl.BlockSpec((SUBLANES, tm), lambda bi, i: (0, bi * nt + i)),
                   pl.BlockSpec((N_EXPERTS, LANES), const2),
                   pl.BlockSpec((SUBLANES, LANES), const2)),
        compiler_params=_params("arbitrary", "arbitrary"),
        name="mix",
    )(na, sw, x, beta_na.reshape(1, -1), beta_swa.reshape(1, -1), w_out, g_post_mix.reshape(1, d), gate_a,
      g_pre_ffn.reshape(1, d), scale_f, shift_f, w_router3)


MOE_TILE = 256
PLAN_T = 512
I_TILE, I_EXPERT, I_LO, I_HI, I_FIRST = range(5)


def _plan_kernel(rt_ref, cntc_ref, cntr_ref, pos_ref, items_ref, start_ref, carry_ref, *, n_tiles, n_items):
    i = pl.program_id(0)
    sub = lax.broadcasted_iota(jnp.int32, (N_EXPERTS, LANES), 0)
    lane = lax.broadcasted_iota(jnp.int32, (N_EXPERTS, LANES), 1)

    @pl.when(i == 0)
    def _():
        c_col = cntc_ref[:, 0:1]
        c_row = cntr_ref[0:1, :]
        s_col = jnp.sum(jnp.where(lane < sub, c_row, 0.0), axis=1, keepdims=True)
        s_row = jnp.sum(jnp.where(sub < lane, c_col, 0.0), axis=0, keepdims=True)
        start_ref[...] = jnp.broadcast_to(s_col, start_ref.shape)
        carry_ref[...] = jnp.zeros_like(carry_ref)

        def tiles_of(s, c):
            first = jnp.floor(s * (1.0 / MOE_TILE))
            last = jnp.floor((s + c - 1.0) * (1.0 / MOE_TILE))
            return first, jnp.where(c > 0.0, last - first + 1.0, 0.0)

        f_col, n_col = tiles_of(s_col, c_col)
        _, n_row = tiles_of(s_row, c_row)
        i_col = jnp.sum(jnp.where(lane < sub, n_row, 0.0), axis=1, keepdims=True)
        total = jnp.sum(n_col, axis=0, keepdims=True)
        k = lax.broadcasted_iota(jnp.int32, (N_EXPERTS, n_items), 1).astype(F32)
        subk = lax.broadcasted_iota(jnp.int32, (N_EXPERTS, n_items), 0).astype(F32)
        ek = jnp.sum(jnp.where(i_col + n_col <= k, 1.0, 0.0), axis=0, keepdims=True)
        k0 = k[0:1]
        valid = k0 < total
        sel = subk == ek

        def pick(v):
            return jnp.sum(jnp.where(sel, v, 0.0), axis=0, keepdims=True)

        i_k, f_k, s_k, c_k = pick(i_col), pick(f_col), pick(s_col), pick(c_col)
        tile = f_k + (k0 - i_k)
        row0 = tile * MOE_TILE
        lo = jnp.maximum(s_k, row0) - row0
        hi = jnp.minimum(s_k + c_k, row0 + MOE_TILE) - row0
        rows = [jnp.where(valid, tile, n_tiles - 1.0), jnp.where(valid, ek, N_EXPERTS - 1.0),
                jnp.where(valid, lo, 0.0), jnp.where(valid, hi, 0.0),
                jnp.where(valid & (lo == 0.0), 1.0, 0.0)]
        rows += [jnp.zeros_like(k0)] * (SUBLANES - len(rows))
        items_ref[...] = jnp.concatenate(rows, axis=0).astype(jnp.int32)

    t = rt_ref.shape[1]
    e1 = rt_ref[R_E1:R_E1 + 1, :]
    e2 = rt_ref[R_E2:R_E2 + 1, :]
    sub_t = lax.broadcasted_iota(jnp.int32, (N_EXPERTS, t), 0).astype(F32)
    oh1 = sub_t == e1
    oh2 = sub_t == e2
    oh = (oh1 | oh2).astype(F32)
    before = (lax.broadcasted_iota(jnp.int32, (t, t), 0) < lax.broadcasted_iota(jnp.int32, (t, t), 1)).astype(BF16)
    rank = jnp.dot(oh.astype(BF16), before, preferred_element_type=F32)
    base = start_ref[:, 0:1] + carry_ref[:, 0:1] + rank
    pos1 = jnp.sum(jnp.where(oh1, base, 0.0), axis=0, keepdims=True)
    pos2 = jnp.sum(jnp.where(oh2, base, 0.0), axis=0, keepdims=True)
    carry_ref[...] += jnp.broadcast_to(jnp.sum(oh, axis=1, keepdims=True), carry_ref.shape)
    pos = jnp.concatenate([pos1, pos2] + [jnp.zeros_like(pos1)] * (SUBLANES - 2), axis=0)
    pos_ref[...] = pos.astype(jnp.int32)


def _plan(rt, cntc, cntr):
    n = rt.shape[1]
    n_tiles = 2 * n // MOE_TILE
    n_items = 2 * LANES
    assert n_tiles + N_EXPERTS <= n_items
    return pl.pallas_call(
        functools.partial(_plan_kernel, n_tiles=n_tiles, n_items=n_items),
        out_shape=(jax.ShapeDtypeStruct((SUBLANES, n), jnp.int32),
                   jax.ShapeDtypeStruct((SUBLANES, n_items), jnp.int32)),
        grid=(n // PLAN_T,),
        in_specs=[pl.BlockSpec((SUBLANES, PLAN_T), lambda i: (0, i)),
                  pl.BlockSpec((N_EXPERTS, LANES), lambda i: (0, 0)),
                  pl.BlockSpec((SUBLANES, LANES), lambda i: (0, 0))],
        out_specs=(pl.BlockSpec((SUBLANES, PLAN_T), lambda i: (0, i)),
                   pl.BlockSpec((SUBLANES, n_items), lambda i: (0, 0))),
        scratch_shapes=[pltpu.VMEM((N_EXPERTS, LANES), F32), pltpu.VMEM((N_EXPERTS, LANES), F32)],
        compiler_params=_params("arbitrary"),
        name="plan",
    )(rt, cntc, cntr)


DISPATCH_T = 1024


def _token_rows(ref, index):
    return ref.at[pl.ds(pl.multiple_of(index * SUBLANES, SUBLANES), SUBLANES), :]


def _dispatch_kernel(pos_ref, h_ref, xs_hbm, sem):
    def body(r, carry):
        src = _token_rows(h_ref, r)
        for k in range(2):
            pltpu.make_async_copy(src, _token_rows(xs_hbm, pos_ref[k, r]), sem.at[0]).start()
        return carry

    lax.fori_loop(0, DISPATCH_T, body, 0, unroll=8)
    for k in range(2):
        pltpu.make_async_copy(h_ref, xs_hbm.at[pl.ds(0, DISPATCH_T * SUBLANES), :], sem.at[0]).wait()


def _dispatch(pos, h2t):
    n = pos.shape[1]
    return pl.pallas_call(
        _dispatch_kernel,
        out_shape=jax.ShapeDtypeStruct((2 * n * SUBLANES, LANES), F32),
        grid=(n // DISPATCH_T,),
        in_specs=[pl.BlockSpec((SUBLANES, DISPATCH_T), lambda i: (0, i), memory_space=pltpu.SMEM),
                  pl.BlockSpec((DISPATCH_T * SUBLANES, LANES), lambda i: (i, 0))],
        out_specs=pl.BlockSpec(memory_space=pl.ANY),
        scratch_shapes=[pltpu.SemaphoreType.DMA((1,))],
        compiler_params=_params("arbitrary"),
        name="dispatch",
    )(pos, h2t)


def _expert_kernel(tile_ref, exp_ref, lo_ref, hi_ref, first_ref, xs_ref, wg_ref, wu_ref, wd_ref, o_ref):
    k = pl.program_id(0)
    lo = lo_ref[k]
    hi = hi_ref[k]

    @pl.when(first_ref[k] == 1)
    def _():
        o_ref[...] = jnp.zeros_like(o_ref)

    @pl.when(hi > lo)
    def _():
        x = _from_token_tiles(xs_ref, MOE_TILE).astype(BF16)
        gate = jnp.dot(x, wg_ref[0].astype(BF16), preferred_element_type=F32)
        up = jnp.dot(x, wu_ref[0].astype(BF16), preferred_element_type=F32)
        he = (gate * jax.nn.sigmoid(gate) * up).astype(BF16)
        ye = jnp.dot(he, wd_ref[0].astype(BF16), preferred_element_type=F32)
        row = lax.broadcasted_iota(jnp.int32, (MOE_TILE, LANES), 0)
        mine = (row >= lo) & (row < hi)
        for s in range(SUBLANES):
            rows = pl.ds(s, MOE_TILE, stride=SUBLANES)
            o_ref[rows, :] = jnp.where(mine, ye[:, s * LANES:(s + 1) * LANES], o_ref[rows, :])


def _experts(items, xs, w_gate, w_up, w_down):
    n_rows = xs.shape[0] // SUBLANES
    n_items = n_rows // MOE_TILE + N_EXPERTS
    d = w_gate.shape[1]
    tile_map = lambda k, tile, exp, lo, hi, first: (tile[k], 0)
    exp_map = lambda k, tile, exp, lo, hi, first: (exp[k], 0, 0)
    grid_spec = pltpu.PrefetchScalarGridSpec(
        num_scalar_prefetch=5,
        grid=(n_items,),
        in_specs=[pl.BlockSpec((MOE_TILE * SUBLANES, LANES), tile_map),
                  pl.BlockSpec((1, d, EXPERT_FF), exp_map),
                  pl.BlockSpec((1, d, EXPERT_FF), exp_map),
                  pl.BlockSpec((1, EXPERT_FF, d), exp_map)],
        out_specs=pl.BlockSpec((MOE_TILE * SUBLANES, LANES), tile_map))
    return pl.pallas_call(
        _expert_kernel,
        out_shape=jax.ShapeDtypeStruct(xs.shape, F32),
        grid_spec=grid_spec,
        compiler_params=_params("arbitrary"),
        name="experts",
    )(*(items[j, :n_items] for j in (I_TILE, I_EXPERT, I_LO, I_HI, I_FIRST)), xs, w_gate, w_up, w_down)


COMBINE_T = 256


def _combine_kernel(pos_ref, posn_ref, ys_hbm, r_ref, x1_ref, gf_ref, gpost_ref, o_ref, buf, sem):
    i = pl.program_id(0)
    n = pl.num_programs(0)
    slot = i % 2
    slot_rows = COMBINE_T * SUBLANES

    def issue(p_ref, sl):
        def body(r, carry):
            for k in range(2):
                dst = buf.at[sl, pl.ds(pl.multiple_of(k * slot_rows + r * SUBLANES, SUBLANES), SUBLANES), :]
                pltpu.make_async_copy(_token_rows(ys_hbm, p_ref[k, r]), dst, sem.at[sl]).start()
            return carry

        lax.fori_loop(0, COMBINE_T, body, 0, unroll=8)

    @pl.when(i == 0)
    def _():
        issue(pos_ref, 0)

    @pl.when(i + 1 < n)
    def _():
        issue(posn_ref, 1 - slot)

    pltpu.make_async_copy(ys_hbm.at[pl.ds(0, 2 * slot_rows), :], buf.at[slot], sem.at[slot]).wait()
    ya = _from_token_tiles(buf.at[slot], COMBINE_T)
    yb = _from_token_tiles(buf.at[slot], COMBINE_T, base=slot_rows)
    r = r_ref[...]
    y = r[:, R_W1:R_W1 + 1] * ya + r[:, R_W2:R_W2 + 1] * yb
    o_ref[0] = x1_ref[0] + gf_ref[0] * (_rms(y) * gpost_ref[...])


def _combine(pos, ys, r, x1, gate_f, g_post_ffn):
    b, s, d = x1.shape
    nt = s // COMBINE_T
    n_steps = b * nt
    row = lambda i: (i // nt, i % nt, 0)
    pos_spec = lambda f: pl.BlockSpec((SUBLANES, COMBINE_T), lambda i: (0, f(i)), memory_space=pltpu.SMEM)
    return pl.pallas_call(
        _combine_kernel,
        out_shape=jax.ShapeDtypeStruct((b, s, d), F32),
        grid=(n_steps,),
        in_specs=[pos_spec(lambda i: i),
                  pos_spec(lambda i: jnp.minimum(i + 1, n_steps - 1)),
                  pl.BlockSpec(memory_space=pl.ANY),
                  pl.BlockSpec((COMBINE_T, LANES), lambda i: (i, 0)),
                  pl.BlockSpec((1, COMBINE_T, d), row),
                  pl.BlockSpec((1, 1, d), lambda i: (i // nt, 0, 0)),
                  pl.BlockSpec((1, d), lambda i: (0, 0))],
        out_specs=pl.BlockSpec((1, COMBINE_T, d), row),
        scratch_shapes=[pltpu.VMEM((2, 2 * COMBINE_T * SUBLANES, LANES), F32), pltpu.SemaphoreType.DMA((2,))],
        compiler_params=_params("arbitrary"),
        name="combine",
    )(pos, pos, ys, r, x1, gate_f, g_post_ffn.reshape(1, d))


def _rope_tables(s):
    half = HEAD_DIM // 2
    inv = ROPE_THETA ** (-jnp.arange(half, dtype=F32) * 2.0 / HEAD_DIM)
    ang = jnp.arange(s).astype(F32)[:, None] * inv[None, :]
    cos = jnp.cos(ang)
    sin = jnp.sin(ang)
    cos = jnp.concatenate([cos, cos, cos, cos], axis=-1)
    sin_signed = jnp.concatenate([-sin, sin, -sin, sin], axis=-1)
    return cos, sin_signed


def _router_weights(w_group_router, w_expert_router):
    d = w_group_router.shape[0]
    we = jnp.transpose(w_expert_router, (1, 0, 2)).reshape(d, N_EXPERTS)
    w = jnp.concatenate([w_group_router, we, jnp.zeros((d, LANES - N_GROUPS - N_EXPERTS), F32)], axis=-1)
    return jnp.stack(_split3(w))


def kernel(x, c, w_ada, b_ada, g_pre_mix, w_in, na_rpb, swa_sinks, beta_na, beta_swa, w_out, g_post_mix, g_pre_ffn,
           w_group_router, w_expert_router, w_gate, w_up, w_down, g_post_ffn):
    b, s, d = x.shape
    depth = w_ada.shape[0]
    cos, sin_signed = _rope_tables(s)
    for l in range(depth):
        mod = _adaln(c, w_ada[l], b_ada[l]).reshape(b, N_MOD, 1, d)
        shift_a, scale_a, gate_a, shift_f, scale_f, gate_f = (mod[:, k] for k in range(N_MOD))
        qkv = _qkv(x, g_pre_mix[l], scale_a, shift_a, w_in[l].astype(BF16), cos, sin_signed)
        na = _na(qkv, na_rpb[l])
        sw = _swa(qkv, swa_sinks[l])
        x1, h2t, r, rt, cntc, cntr = _mix(na, sw, x, beta_na[l], beta_swa[l], w_out[l].astype(BF16), g_post_mix[l],
                                          gate_a, g_pre_ffn[l], scale_f, shift_f,
                                          _router_weights(w_group_router[l], w_expert_router[l]))
        pos, items = _plan(rt, cntc, cntr)
        xs = _dispatch(pos, h2t)
        ys = _experts(items, xs, w_gate[l], w_up[l], w_down[l])
        x = _combine(pos, ys, r, x1, gate_f, g_post_ffn[l])
    return x
```

```python
import functools

import jax
import jax.numpy as jnp
from jax import lax
from jax.experimental import pallas as pl
from jax.experimental.pallas import tpu as pltpu

D_MODEL = 1024
GRID_W = 64
HEAD_DIM = 64
NA_HEADS = 8
NA_KH = 8
NA_KW = 16
SWA_HEADS = 8
SWA_KV_HEADS = 2
SWA_WINDOW = 128
SWA_BLOCK = 128
ROPE_THETA = 10000.0
NA_WIDTH = NA_HEADS * HEAD_DIM
SWA_WIDTH = SWA_HEADS * HEAD_DIM
N_GROUPS = 4
EXPERTS_PER_GROUP = 8
N_EXPERTS = N_GROUPS * EXPERTS_PER_GROUP
EXPERT_FF = 256
N_MOD = 6
EPS = 1e-6
NEG_INF = -1e30

LANES = 128
SUBLANES = 8
R_E1, R_E2, R_W1, R_W2 = range(4)
PAIRS = NA_HEADS // 2
COL_NQ, COL_NK, COL_NV, COL_SQ, COL_SK, COL_SV = 0, 4, 8, 12, 16, 18
QKV_TILES = 20
ROUTER_BASE = N_GROUPS
VMEM_LIMIT = 56 * 1024 * 1024

F32 = jnp.float32
BF16 = jnp.bfloat16


def _rms(v):
    return v * lax.rsqrt(jnp.mean(v * v, axis=-1, keepdims=True) + EPS)


def _params(*sem):
    return pltpu.CompilerParams(dimension_semantics=sem, vmem_limit_bytes=VMEM_LIMIT)


def _adaln_kernel(c_ref, w_ref, b_ref, o_ref):
    c = c_ref[...]
    a = c * jax.nn.sigmoid(c)
    o_ref[...] = jnp.dot(a, w_ref[...], precision=lax.Precision.HIGHEST,
                         preferred_element_type=F32) + b_ref[...]


def _adaln(c, w_ada, b_ada):
    batch, d = c.shape
    n = w_ada.shape[1]
    tn = 1024
    b = 8
    c = jnp.pad(c, ((0, b - batch), (0, 0)))
    return pl.pallas_call(
        _adaln_kernel,
        out_shape=jax.ShapeDtypeStruct((b, n), F32),
        grid=(n // tn,),
        in_specs=[pl.BlockSpec((b, d), lambda j: (0, 0)),
                  pl.BlockSpec((d, tn), lambda j: (0, j)),
                  pl.BlockSpec((1, tn), lambda j: (0, j))],
        out_specs=pl.BlockSpec((b, tn), lambda j: (0, j)),
        compiler_params=_params("arbitrary"),
        name="adaln",
    )(c, w_ada, b_ada.reshape(1, n))[:batch]


def _rope(v, cos, sin_signed, first_half):
    rot = jnp.where(first_half, pltpu.roll(v, LANES - HEAD_DIM // 2, 1), pltpu.roll(v, HEAD_DIM // 2, 1))
    return v * cos + rot * sin_signed


def _qkv_kernel(x_ref, g_ref, sc_ref, sh_ref, w_ref, cos_ref, sin_ref, o_ref):
    x = x_ref[0]
    h = (_rms(x) * g_ref[...]) * (1.0 + sc_ref[0]) + sh_ref[0]
    h = h.astype(BF16)
    scale = HEAD_DIM ** -0.5
    cos = cos_ref[...]
    sin = sin_ref[...]
    lane = lax.broadcasted_iota(jnp.int32, cos.shape, 1)
    first_half = (lane % HEAD_DIM) < HEAD_DIM // 2
    upper = lane >= HEAD_DIM

    def proj(col, width):
        return jnp.dot(h, w_ref[:, col * LANES:(col + width) * LANES], preferred_element_type=F32)

    def tile(v, j):
        return v[:, j * LANES:(j + 1) * LANES]

    nq, nk, nv, sq = proj(COL_NQ, 4), proj(COL_NK, 4), proj(COL_NV, 4), proj(COL_SQ, 4)
    for j in range(PAIRS):
        o_ref[0, COL_NQ + j] = (tile(nq, j) * scale).astype(BF16)
        o_ref[0, COL_NK + j] = tile(nk, j).astype(BF16)
        o_ref[0, COL_NV + j] = tile(nv, j).astype(BF16)
        o_ref[0, COL_SQ + j] = (_rope(tile(sq, j), cos, sin, first_half) * scale).astype(BF16)
    skv = proj(COL_SK, 2)
    k = _rope(tile(skv, 0), cos, sin, first_half)
    v = tile(skv, 1)
    for t, col in ((k, COL_SK), (v, COL_SV)):
        swapped = pltpu.roll(t, HEAD_DIM, 1)
        o_ref[0, col] = jnp.where(upper, swapped, t).astype(BF16)
        o_ref[0, col + 1] = jnp.where(upper, t, swapped).astype(BF16)


def _qkv(x, g, scale_a, shift_a, w_in, cos, sin):
    b, s, d = x.shape
    tm = 512
    n_in = w_in.shape[1]
    return pl.pallas_call(
        _qkv_kernel,
        out_shape=jax.ShapeDtypeStruct((b, QKV_TILES, s, LANES), BF16),
        grid=(b, s // tm),
        in_specs=[pl.BlockSpec((1, tm, d), lambda bi, i: (bi, i, 0)),
                  pl.BlockSpec((1, d), lambda bi, i: (0, 0)),
                  pl.BlockSpec((1, 1, d), lambda bi, i: (bi, 0, 0)),
                  pl.BlockSpec((1, 1, d), lambda bi, i: (bi, 0, 0)),
                  pl.BlockSpec((d, n_in), lambda bi, i: (0, 0)),
                  pl.BlockSpec((tm, LANES), lambda bi, i: (i, 0)),
                  pl.BlockSpec((tm, LANES), lambda bi, i: (i, 0))],
        out_specs=pl.BlockSpec((1, QKV_TILES, tm, LANES), lambda bi, i: (bi, 0, i, 0)),
        compiler_params=_params("arbitrary", "arbitrary"),
        name="qkv",
    )(x, g.reshape(1, d), scale_a, shift_a, w_in, cos, sin)


NA_QROWS = 4
NA_KROWS = NA_QROWS + NA_KH
NA_BLOCKS_PER_STEP = 4
NA_Q = NA_QROWS * GRID_W
NA_K = NA_KROWS * GRID_W
NA_RPB_ROWS = 2 * NA_KH - 1
NA_RPB_COLS = 2 * NA_KW - 1
NA_TYPES = 3


def _na_first_key_row(block, rows, clip):
    return clip(block * NA_QROWS - NA_KH // 2, 0, rows - NA_KROWS)


def _na_kernel(q_ref, k_ref, v_ref, bias_ref, o_ref, *, rows):
    lane = lax.broadcasted_iota(jnp.int32, (NA_Q, LANES), 1)
    upper = lane >= HEAD_DIM
    for c in range(NA_BLOCKS_PER_STEP):
        block = pl.program_id(2) * NA_BLOCKS_PER_STEP + c
        a = _na_first_key_row(block, rows, jnp.clip)
        ty = jnp.where(block == 0, 0, jnp.where(block == rows // NA_QROWS - 1, 2, 1))
        start = pl.multiple_of(a * GRID_W, GRID_W)
        q = q_ref[0, 0, c * NA_Q:(c + 1) * NA_Q, :]
        ks = k_ref[0, 0, pl.ds(start, NA_K), :]
        vs = v_ref[0, 0, pl.ds(start, NA_K), :]
        outs = []
        for hh in range(2):
            qm = jnp.where(upper if hh else ~upper, q, jnp.zeros_like(q))
            s = lax.dot_general(qm, ks, (((1,), (1,)), ((), ())), preferred_element_type=F32)
            s = s + bias_ref[hh, ty]
            m = jnp.max(s, axis=-1, keepdims=True)
            e = jnp.exp(s - m)
            l = jnp.sum(e, axis=-1, keepdims=True)
            o = jnp.dot(e.astype(BF16), vs, preferred_element_type=F32)
            outs.append(o / l)
        o_ref[0, 0, c * NA_Q:(c + 1) * NA_Q, :] = jnp.where(upper, outs[1], outs[0]).astype(BF16)


def _na_bias_kernel(rpb_ref, o_ref, *, rows):
    h = pl.program_id(0)
    q = lax.broadcasted_iota(jnp.int32, (GRID_W, LANES), 0)
    lane = lax.broadcasted_iota(jnp.int32, (GRID_W, LANES), 1)
    kc = lane % GRID_W
    upper = lane >= GRID_W
    c0 = jnp.clip(q - NA_KW // 2, 0, GRID_W - NA_KW)
    in_cols = (kc >= c0) & (kc < c0 + NA_KW)
    dc = kc - q + NA_KW - 1
    neg = jnp.full((GRID_W, LANES), NEG_INF, F32)
    base = h * NA_RPB_ROWS * NA_RPB_COLS
    by_row_offset = []
    for d in range(NA_RPB_ROWS):
        acc = neg
        for dd in range(NA_RPB_COLS):
            acc = jnp.where(dc == dd, rpb_ref[base + d * NA_RPB_COLS + dd], acc)
        by_row_offset.append(jnp.where(in_cols, acc, NEG_INF))
    clamp = lambda v, lo, hi: min(max(v, lo), hi)
    for ty, block in enumerate((0, 1, rows // NA_QROWS - 1)):
        r = block * NA_QROWS
        a = _na_first_key_row(block, rows, clamp)
        for j in range(NA_QROWS):
            r0 = min(max(r + j - NA_KH // 2, 0), rows - NA_KH)
            halves = []
            for i in range(NA_KROWS):
                in_rows = r0 <= a + i < r0 + NA_KH
                halves.append(by_row_offset[a + i - (r + j) + NA_KH - 1] if in_rows else neg)
            for t in range(NA_KROWS // 2):
                tile = jnp.where(upper, halves[2 * t + 1], halves[2 * t])
                o_ref[0, ty, j * GRID_W:(j + 1) * GRID_W, t * LANES:(t + 1) * LANES] = tile


def _na_bias(rpb, rows):
    return pl.pallas_call(
        functools.partial(_na_bias_kernel, rows=rows),
        out_shape=jax.ShapeDtypeStruct((NA_HEADS, NA_TYPES, NA_Q, NA_K), F32),
        grid=(NA_HEADS,),
        in_specs=[pl.BlockSpec(memory_space=pltpu.SMEM)],
        out_specs=pl.BlockSpec((1, NA_TYPES, NA_Q, NA_K), lambda h: (h, 0, 0, 0)),
        compiler_params=_params("arbitrary"),
        name="na_bias",
    )(rpb.astype(F32).reshape(-1))


def _na(qkv, rpb):
    b, _, s, _ = qkv.shape
    rows = s // GRID_W
    assert rows % (NA_QROWS * NA_BLOCKS_PER_STEP) == 0 and rows // NA_QROWS >= NA_TYPES
    tq = NA_BLOCKS_PER_STEP * NA_Q
    return pl.pallas_call(
        functools.partial(_na_kernel, rows=rows),
        out_shape=jax.ShapeDtypeStruct((b, PAIRS, s, LANES), BF16),
        grid=(b, PAIRS, s // tq),
        in_specs=[pl.BlockSpec((1, 1, tq, LANES), lambda bi, p, i: (bi, COL_NQ + p, i, 0)),
                  pl.BlockSpec((1, 1, s, LANES), lambda bi, p, i: (bi, COL_NK + p, 0, 0)),
                  pl.BlockSpec((1, 1, s, LANES), lambda bi, p, i: (bi, COL_NV + p, 0, 0)),
                  pl.BlockSpec((2, NA_TYPES, NA_Q, NA_K), lambda bi, p, i: (p, 0, 0, 0))],
        out_specs=pl.BlockSpec((1, 1, tq, LANES), lambda bi, p, i: (bi, p, i, 0)),
        compiler_params=_params("arbitrary", "arbitrary", "arbitrary"),
        name="na",
    )(qkv, qkv, qkv, _na_bias(rpb, rows))


SWA_KEYS = 3 * SWA_BLOCK
SWA_BLOCKS_PER_STEP = 4
SWA_GROUP = SWA_HEADS // SWA_KV_HEADS
assert COL_SQ % (PAIRS // SWA_KV_HEADS) == 0


def _swa_kernel(sink_ref, q_ref, k_ref, v_ref, o_ref, *, seq):
    kv = pl.program_id(1)
    rows = SWA_GROUP * SWA_BLOCK
    lane = lax.broadcasted_iota(jnp.int32, (SWA_BLOCK, LANES), 1)
    upper = lane >= HEAD_DIM
    row = lax.broadcasted_iota(jnp.int32, (rows, SWA_KEYS), 0)
    q_off = row % SWA_BLOCK
    k_off = lax.broadcasted_iota(jnp.int32, (rows, SWA_KEYS), 1)
    head = lax.broadcasted_iota(jnp.int32, (rows, 1), 0) // SWA_BLOCK
    sink = jnp.zeros((rows, 1), F32)
    for g in range(SWA_GROUP):
        sink = jnp.where(head == g, sink_ref[kv * SWA_GROUP + g], sink)
    for j in range(SWA_BLOCKS_PER_STEP):
        n = pl.program_id(2) * SWA_BLOCKS_PER_STEP + j
        start = pl.multiple_of(jnp.clip((n - 1) * SWA_BLOCK, 0, seq - SWA_KEYS), SWA_BLOCK)
        ks = k_ref[0, 0, pl.ds(start, SWA_KEYS), :]
        vs = v_ref[0, 0, pl.ds(start, SWA_KEYS), :]
        qs = []
        for g in range(SWA_GROUP):
            q = q_ref[0, g // 2, j * SWA_BLOCK:(j + 1) * SWA_BLOCK, :]
            qs.append(jnp.where(upper if g % 2 else ~upper, q, jnp.zeros_like(q)))
        s = lax.dot_general(jnp.concatenate(qs, axis=0), ks, (((1,), (1,)), ((), ())), preferred_element_type=F32)
        in_window = jnp.abs((start + k_off) - (n * SWA_BLOCK + q_off)) <= SWA_WINDOW
        s = jnp.where(in_window, s, NEG_INF)
        m = jnp.maximum(jnp.max(s, axis=-1, keepdims=True), sink)
        e = jnp.exp(s - m)
        l = jnp.sum(e, axis=-1, keepdims=True) + jnp.exp(sink - m)
        o = jnp.dot(e.astype(BF16), vs, preferred_element_type=F32) / l
        for pair in range(SWA_GROUP // 2):
            even = o[(2 * pair) * SWA_BLOCK:(2 * pair + 1) * SWA_BLOCK]
            odd = o[(2 * pair + 1) * SWA_BLOCK:(2 * pair + 2) * SWA_BLOCK]
            o_ref[0, pair, j * SWA_BLOCK:(j + 1) * SWA_BLOCK, :] = jnp.where(upper, odd, even).astype(BF16)


def _swa(qkv, sinks):
    b, _, s, _ = qkv.shape
    pairs_per_kv = PAIRS // SWA_KV_HEADS
    tq = SWA_BLOCKS_PER_STEP * SWA_BLOCK
    return pl.pallas_call(
        functools.partial(_swa_kernel, seq=s),
        out_shape=jax.ShapeDtypeStruct((b, PAIRS, s, LANES), BF16),
        grid=(b, SWA_KV_HEADS, s // tq),
        in_specs=[pl.BlockSpec(memory_space=pltpu.SMEM),
                  pl.BlockSpec((1, pairs_per_kv, tq, LANES),
                               lambda bi, kv, n: (bi, COL_SQ // pairs_per_kv + kv, n, 0)),
                  pl.BlockSpec((1, 1, s, LANES), lambda bi, kv, n: (bi, COL_SK + kv, 0, 0)),
                  pl.BlockSpec((1, 1, s, LANES), lambda bi, kv, n: (bi, COL_SV + kv, 0, 0))],
        out_specs=pl.BlockSpec((1, pairs_per_kv, tq, LANES), lambda bi, kv, n: (bi, kv, n, 0)),
        compiler_params=_params("arbitrary", "arbitrary", "arbitrary"),
        name="swa",
    )(sinks, qkv, qkv, qkv)


ROUTER_COLS = N_GROUPS + N_EXPERTS


def _split2(v):
    hi = v.astype(BF16)
    lo = (v - hi.astype(F32)).astype(BF16)
    return hi, lo


def _route(logits):
    lane = lax.broadcasted_iota(jnp.int32, logits.shape, 1)
    big = jnp.int32(LANES)
    gmask = lane < N_GROUPS
    gl = jnp.where(gmask, logits, NEG_INF)
    gmax = jnp.max(gl, axis=-1, keepdims=True)
    g_top = jnp.min(jnp.where(gmask & (gl == gmax), lane, big), axis=-1, keepdims=True)
    g_weight = 1.0 / jnp.sum(jnp.where(gmask, jnp.exp(gl - gmax), 0.0), axis=-1, keepdims=True)
    lo = ROUTER_BASE + g_top * EXPERTS_PER_GROUP
    emask = (lane >= lo) & (lane < lo + EXPERTS_PER_GROUP)
    el = jnp.where(emask, logits, NEG_INF)
    m1 = jnp.max(el, axis=-1, keepdims=True)
    i1 = jnp.min(jnp.where(emask & (el == m1), lane, big), axis=-1, keepdims=True)
    emask2 = emask & (lane != i1)
    el2 = jnp.where(emask2, logits, NEG_INF)
    m2 = jnp.max(el2, axis=-1, keepdims=True)
    i2 = jnp.min(jnp.where(emask2 & (el2 == m2), lane, big), axis=-1, keepdims=True)
    e2 = jnp.exp(m2 - m1)
    w1 = g_weight / (1.0 + e2)
    w2 = g_weight * e2 / (1.0 + e2)
    return i1 - ROUTER_BASE, i2 - ROUTER_BASE, w1, w2


def _mix_kernel(na_ref, sw_ref, x_ref, bna_ref, bsw_ref, wo_ref, gpm_ref, ga_ref, gpf_ref, scf_ref, shf_ref,
                wr_ref, x1_ref, h2_ref, r_ref, rt_ref, cntc_ref, cntr_ref):
    def heads(ref):
        return jnp.concatenate([ref[0, j] for j in range(PAIRS)], axis=-1).astype(F32)

    na = (_rms(heads(na_ref)) * bna_ref[...]).astype(BF16)
    sw = (_rms(heads(sw_ref)) * bsw_ref[...]).astype(BF16)
    mix = (jnp.dot(na, wo_ref[:NA_WIDTH, :], preferred_element_type=F32)
           + jnp.dot(sw, wo_ref[NA_WIDTH:, :], preferred_element_type=F32))
    x1 = x_ref[0] + ga_ref[0] * (_rms(mix) * gpm_ref[...])
    x1_ref[0] = x1
    h2 = (_rms(x1) * gpf_ref[...]) * (1.0 + scf_ref[0]) + shf_ref[0]
    h2_ref[...] = h2
    h_hi, h_lo = _split2(h2)
    both = (jnp.dot(h_hi, wr_ref[...], preferred_element_type=F32)
            + jnp.dot(h_lo, wr_ref[...], preferred_element_type=F32))
    logits = both + pltpu.roll(both, LANES - ROUTER_COLS, 1)
    e1, e2, w1, w2 = _route(logits)
    lane = lax.broadcasted_iota(jnp.int32, logits.shape, 1)
    r = jnp.where(lane == R_E1, e1.astype(F32),
                  jnp.where(lane == R_E2, e2.astype(F32),
                            jnp.where(lane == R_W1, w1, jnp.where(lane == R_W2, w2, 0.0))))
    r_ref[...] = r
    rt = jnp.transpose(r)[:SUBLANES]
    rt_ref[...] = rt
    first_step = (pl.program_id(0) == 0) & (pl.program_id(1) == 0)

    @pl.when(first_step)
    def _():
        cntc_ref[...] = jnp.zeros_like(cntc_ref)
        cntr_ref[...] = jnp.zeros_like(cntr_ref)

    on_lane = ((lane == e1) | (lane == e2)).astype(F32)
    cntr_ref[...] += jnp.broadcast_to(jnp.sum(on_lane, axis=0, keepdims=True), cntr_ref.shape)
    sub = lax.broadcasted_iota(jnp.int32, (N_EXPERTS, rt.shape[1]), 0).astype(F32)
    on_sub = ((sub == rt[R_E1:R_E1 + 1]) | (sub == rt[R_E2:R_E2 + 1])).astype(F32)
    cntc_ref[...] += jnp.broadcast_to(jnp.sum(on_sub, axis=1, keepdims=True), cntc_ref.shape)


def _mix(na, sw, x, beta_na, beta_swa, w_out, g_post_mix, gate_a, g_pre_ffn, scale_f, shift_f, w_router3):
    b, s, d = x.shape
    tm = 512
    nt = s // tm
    row = lambda bi, i: (bi, i, 0)
    const2 = lambda bi, i: (0, 0)
    per_b = lambda bi, i: (bi, 0, 0)
    return pl.pallas_call(
        _mix_kernel,
        out_shape=(jax.ShapeDtypeStruct((b, s, d), F32),
                   jax.ShapeDtypeStruct((b * s, d), F32),
                   jax.ShapeDtypeStruct((b * s, LANES), F32),
                   jax.ShapeDtypeStruct((SUBLANES, b * s), F32),
                   jax.ShapeDtypeStruct((N_EXPERTS, LANES), F32),
                   jax.ShapeDtypeStruct((SUBLANES, LANES), F32)),
        grid=(b, s // tm),
        in_specs=[pl.BlockSpec((1, PAIRS, tm, LANES), lambda bi, i: (bi, 0, i, 0)),
                  pl.BlockSpec((1, PAIRS, tm, LANES), lambda bi, i: (bi, 0, i, 0)),
                  pl.BlockSpec((1, tm, d), row),
                  pl.BlockSpec((1, NA_WIDTH), const2),
                  pl.BlockSpec((1, SWA_WIDTH), const2),
                  pl.BlockSpec((NA_WIDTH + SWA_WIDTH, d), const2),
                  pl.BlockSpec((1, d), const2),
                  pl.BlockSpec((1, 1, d), per_b),
                  pl.BlockSpec((1, d), const2),
                  pl.BlockSpec((1, 1, d), per_b),
                  pl.BlockSpec((1, 1, d), per_b),
                  pl.BlockSpec((d, LANES), const2)],
        out_specs=(pl.BlockSpec((1, tm, d), row),
                   pl.BlockSpec((tm, d), lambda bi, i: (bi * nt + i, 0)),
                   pl.BlockSpec((tm, LANES), lambda bi, i: (bi * nt + i, 0)),
                   pl.BlockSpec((SUBLANES, tm), lambda bi, i: (0, bi * nt + i)),
                   pl.BlockSpec((N_EXPERTS, LANES), const2),
                   pl.BlockSpec((SUBLANES, LANES), const2)),
        compiler_params=_params("arbitrary", "arbitrary"),
        name="mix",
    )(na, sw, x, beta_na.reshape(1, -1), beta_swa.reshape(1, -1), w_out, g_post_mix.reshape(1, d), gate_a,
      g_pre_ffn.reshape(1, d), scale_f, shift_f, w_router3)


MOE_TILE = 256
PLAN_T = 512
I_TILE, I_EXPERT, I_LO, I_HI, I_FIRST = range(5)


def _plan_kernel(rt_ref, cntc_ref, cntr_ref, pos_ref, items_ref, start_ref, carry_ref, *, n_tiles, n_items):
    i = pl.program_id(0)
    sub = lax.broadcasted_iota(jnp.int32, (N_EXPERTS, LANES), 0)
    lane = lax.broadcasted_iota(jnp.int32, (N_EXPERTS, LANES), 1)

    @pl.when(i == 0)
    def _():
        c_col = cntc_ref[:, 0:1]
        c_row = cntr_ref[0:1, :]
        s_col = jnp.sum(jnp.where(lane < sub, c_row, 0.0), axis=1, keepdims=True)
        s_row = jnp.sum(jnp.where(sub < lane, c_col, 0.0), axis=0, keepdims=True)
        start_ref[...] = jnp.broadcast_to(s_col, start_ref.shape)
        carry_ref[...] = jnp.zeros_like(carry_ref)

        def tiles_of(s, c):
            first = jnp.floor(s * (1.0 / MOE_TILE))
            last = jnp.floor((s + c - 1.0) * (1.0 / MOE_TILE))
            return first, jnp.where(c > 0.0, last - first + 1.0, 0.0)

        f_col, n_col = tiles_of(s_col, c_col)
        _, n_row = tiles_of(s_row, c_row)
        i_col = jnp.sum(jnp.where(lane < sub, n_row, 0.0), axis=1, keepdims=True)
        total = jnp.sum(n_col, axis=0, keepdims=True)
        k = lax.broadcasted_iota(jnp.int32, (N_EXPERTS, n_items), 1).astype(F32)
        subk = lax.broadcasted_iota(jnp.int32, (N_EXPERTS, n_items), 0).astype(F32)
        ek = jnp.sum(jnp.where(i_col + n_col <= k, 1.0, 0.0), axis=0, keepdims=True)
        k0 = k[0:1]
        valid = k0 < total
        sel = subk == ek

        def pick(v):
            return jnp.sum(jnp.where(sel, v, 0.0), axis=0, keepdims=True)

        i_k, f_k, s_k, c_k = pick(i_col), pick(f_col), pick(s_col), pick(c_col)
        tile = f_k + (k0 - i_k)
        row0 = tile * MOE_TILE
        lo = jnp.maximum(s_k, row0) - row0
        hi = jnp.minimum(s_k + c_k, row0 + MOE_TILE) - row0
        rows = [jnp.where(valid, tile, n_tiles - 1.0), jnp.where(valid, ek, N_EXPERTS - 1.0),
                jnp.where(valid, lo, 0.0), jnp.where(valid, hi, 0.0),
                jnp.where(valid & (lo == 0.0), 1.0, 0.0)]
        rows += [jnp.zeros_like(k0)] * (SUBLANES - len(rows))
        items_ref[...] = jnp.concatenate(rows, axis=0).astype(jnp.int32)

    t = rt_ref.shape[1]
    e1 = rt_ref[R_E1:R_E1 + 1, :]
    e2 = rt_ref[R_E2:R_E2 + 1, :]
    sub_t = lax.broadcasted_iota(jnp.int32, (N_EXPERTS, t), 0).astype(F32)
    oh1 = sub_t == e1
    oh2 = sub_t == e2
    oh = (oh1 | oh2).astype(F32)
    before = (lax.broadcasted_iota(jnp.int32, (t, t), 0) < lax.broadcasted_iota(jnp.int32, (t, t), 1)).astype(BF16)
    rank = jnp.dot(oh.astype(BF16), before, preferred_element_type=F32)
    base = start_ref[:, 0:1] + carry_ref[:, 0:1] + rank
    pos1 = jnp.sum(jnp.where(oh1, base, 0.0), axis=0, keepdims=True)
    pos2 = jnp.sum(jnp.where(oh2, base, 0.0), axis=0, keepdims=True)
    carry_ref[...] += jnp.broadcast_to(jnp.sum(oh, axis=1, keepdims=True), carry_ref.shape)
    pos = jnp.concatenate([pos1, pos2] + [jnp.zeros_like(pos1)] * (SUBLANES - 2), axis=0)
    pos_ref[...] = pos.astype(jnp.int32)


def _plan(rt, cntc, cntr):
    n = rt.shape[1]
    n_tiles = 2 * n // MOE_TILE
    n_items = 2 * LANES
    assert n_tiles + N_EXPERTS <= n_items
    return pl.pallas_call(
        functools.partial(_plan_kernel, n_tiles=n_tiles, n_items=n_items),
        out_shape=(jax.ShapeDtypeStruct((SUBLANES, n), jnp.int32),
                   jax.ShapeDtypeStruct((SUBLANES, n_items), jnp.int32)),
        grid=(n // PLAN_T,),
        in_specs=[pl.BlockSpec((SUBLANES, PLAN_T), lambda i: (0, i)),
                  pl.BlockSpec((N_EXPERTS, LANES), lambda i: (0, 0)),
                  pl.BlockSpec((SUBLANES, LANES), lambda i: (0, 0))],
        out_specs=(pl.BlockSpec((SUBLANES, PLAN_T), lambda i: (0, i)),
                   pl.BlockSpec((SUBLANES, n_items), lambda i: (0, 0))),
        scratch_shapes=[pltpu.VMEM((N_EXPERTS, LANES), F32), pltpu.VMEM((N_EXPERTS, LANES), F32)],
        compiler_params=_params("arbitrary"),
        name="plan",
    )(rt, cntc, cntr)


DISPATCH_T = 1024


def _row(ref, index):
    return ref.at[pl.ds(index, 1), :]


def _dispatch_kernel(pos_ref, h_ref, xs_hbm, sem):
    def body(r, carry):
        for k in range(2):
            pltpu.make_async_copy(_row(h_ref, r), _row(xs_hbm, pos_ref[k, r]), sem.at[0]).start(priority=k)
        return carry

    lax.fori_loop(0, DISPATCH_T, body, 0, unroll=8)
    for k in range(2):
        pltpu.make_async_copy(h_ref, xs_hbm.at[pl.ds(0, DISPATCH_T), :], sem.at[0]).wait()


def _dispatch(pos, h2):
    n, d = h2.shape
    return pl.pallas_call(
        _dispatch_kernel,
        out_shape=jax.ShapeDtypeStruct((2 * n, d), F32),
        grid=(n // DISPATCH_T,),
        in_specs=[pl.BlockSpec((SUBLANES, DISPATCH_T), lambda i: (0, i), memory_space=pltpu.SMEM),
                  pl.BlockSpec((DISPATCH_T, d), lambda i: (i, 0))],
        out_specs=pl.BlockSpec(memory_space=pl.ANY),
        scratch_shapes=[pltpu.SemaphoreType.DMA((1,))],
        compiler_params=_params("arbitrary"),
        name="dispatch",
    )(pos, h2)


def _expert_kernel(tile_ref, exp_ref, lo_ref, hi_ref, first_ref, xs_ref, wg_ref, wu_ref, wd_ref, o_ref):
    k = pl.program_id(0)
    lo = lo_ref[k]
    hi = hi_ref[k]

    @pl.when(first_ref[k] == 1)
    def _():
        o_ref[...] = jnp.zeros_like(o_ref)

    @pl.when(hi > lo)
    def _():
        x = xs_ref[...].astype(BF16)
        gate = jnp.dot(x, wg_ref[0].astype(BF16), preferred_element_type=F32)
        up = jnp.dot(x, wu_ref[0].astype(BF16), preferred_element_type=F32)
        he = (gate * jax.nn.sigmoid(gate) * up).astype(BF16)
        ye = jnp.dot(he, wd_ref[0].astype(BF16), preferred_element_type=F32)
        row = lax.broadcasted_iota(jnp.int32, (MOE_TILE, 1), 0)
        mine = (row >= lo) & (row < hi)
        o_ref[...] = jnp.where(mine, ye, o_ref[...])


def _experts(items, xs, w_gate, w_up, w_down):
    n_rows, d = xs.shape
    n_items = n_rows // MOE_TILE + N_EXPERTS
    tile_map = lambda k, tile, exp, lo, hi, first: (tile[k], 0)
    exp_map = lambda k, tile, exp, lo, hi, first: (exp[k], 0, 0)
    grid_spec = pltpu.PrefetchScalarGridSpec(
        num_scalar_prefetch=5,
        grid=(n_items,),
        in_specs=[pl.BlockSpec((MOE_TILE, d), tile_map),
                  pl.BlockSpec((1, d, EXPERT_FF), exp_map),
                  pl.BlockSpec((1, d, EXPERT_FF), exp_map),
                  pl.BlockSpec((1, EXPERT_FF, d), exp_map)],
        out_specs=pl.BlockSpec((MOE_TILE, d), tile_map))
    return pl.pallas_call(
        _expert_kernel,
        out_shape=jax.ShapeDtypeStruct(xs.shape, F32),
        grid_spec=grid_spec,
        compiler_params=_params("arbitrary"),
        name="experts",
    )(*(items[j, :n_items] for j in (I_TILE, I_EXPERT, I_LO, I_HI, I_FIRST)), xs, w_gate, w_up, w_down)


COMBINE_T = 256


def _combine_kernel(pos_ref, posn_ref, ys_hbm, r_ref, x1_ref, gf_ref, gpost_ref, o_ref, buf, sem):
    i = pl.program_id(0)
    n = pl.num_programs(0)
    slot = i % 2
    def issue(p_ref, sl):
        def body(r, carry):
            for k in range(2):
                dst = buf.at[sl, pl.ds(k * COMBINE_T + r, 1), :]
                pltpu.make_async_copy(_row(ys_hbm, p_ref[k, r]), dst, sem.at[sl]).start(priority=k)
            return carry

        lax.fori_loop(0, COMBINE_T, body, 0, unroll=8)

    @pl.when(i == 0)
    def _():
        issue(pos_ref, 0)

    @pl.when(i + 1 < n)
    def _():
        issue(posn_ref, 1 - slot)

    pltpu.make_async_copy(ys_hbm.at[pl.ds(0, 2 * COMBINE_T), :], buf.at[slot], sem.at[slot]).wait()
    ya = buf[slot, :COMBINE_T, :]
    yb = buf[slot, COMBINE_T:, :]
    r = r_ref[...]
    y = r[:, R_W1:R_W1 + 1] * ya + r[:, R_W2:R_W2 + 1] * yb
    o_ref[0] = x1_ref[0] + gf_ref[0] * (_rms(y) * gpost_ref[...])


def _combine(pos, ys, r, x1, gate_f, g_post_ffn):
    b, s, d = x1.shape
    nt = s // COMBINE_T
    n_steps = b * nt
    row = lambda i: (i // nt, i % nt, 0)
    pos_spec = lambda f: pl.BlockSpec((SUBLANES, COMBINE_T), lambda i: (0, f(i)), memory_space=pltpu.SMEM)
    return pl.pallas_call(
        _combine_kernel,
        out_shape=jax.ShapeDtypeStruct((b, s, d), F32),
        grid=(n_steps,),
        in_specs=[pos_spec(lambda i: i),
                  pos_spec(lambda i: jnp.minimum(i + 1, n_steps - 1)),
                  pl.BlockSpec(memory_space=pl.ANY),
                  pl.BlockSpec((COMBINE_T, LANES), lambda i: (i, 0)),
                  pl.BlockSpec((1, COMBINE_T, d), row),
                  pl.BlockSpec((1, 1, d), lambda i: (i // nt, 0, 0)),
                  pl.BlockSpec((1, d), lambda i: (0, 0))],
        out_specs=pl.BlockSpec((1, COMBINE_T, d), row),
        scratch_shapes=[pltpu.VMEM((2, 2 * COMBINE_T, d), F32), pltpu.SemaphoreType.DMA((2,))],
        compiler_params=_params("arbitrary"),
        name="combine",
    )(pos, pos, ys, r, x1, gate_f, g_post_ffn.reshape(1, d))


def _rope_tables(s):
    half = HEAD_DIM // 2
    inv = ROPE_THETA ** (-jnp.arange(half, dtype=F32) * 2.0 / HEAD_DIM)
    ang = jnp.arange(s).astype(F32)[:, None] * inv[None, :]
    cos = jnp.cos(ang)
    sin = jnp.sin(ang)
    cos = jnp.concatenate([cos, cos, cos, cos], axis=-1)
    sin_signed = jnp.concatenate([-sin, sin, -sin, sin], axis=-1)
    return cos, sin_signed


def _router_weights(w_group_router, w_expert_router):
    d = w_group_router.shape[0]
    we = jnp.transpose(w_expert_router, (1, 0, 2)).reshape(d, N_EXPERTS)
    hi, lo = _split2(jnp.concatenate([w_group_router, we], axis=-1))
    return jnp.concatenate([hi, lo, jnp.zeros((d, LANES - 2 * ROUTER_COLS), BF16)], axis=-1)


def kernel(x, c, w_ada, b_ada, g_pre_mix, w_in, na_rpb, swa_sinks, beta_na, beta_swa, w_out, g_post_mix, g_pre_ffn,
           w_group_router, w_expert_router, w_gate, w_up, w_down, g_post_ffn):
    b, s, d = x.shape
    depth = w_ada.shape[0]
    cos, sin_signed = _rope_tables(s)
    for l in range(depth):
        mod = _adaln(c, w_ada[l], b_ada[l]).reshape(b, N_MOD, 1, d)
        shift_a, scale_a, gate_a, shift_f, scale_f, gate_f = (mod[:, k] for k in range(N_MOD))
        qkv = _qkv(x, g_pre_mix[l], scale_a, shift_a, w_in[l].astype(BF16), cos, sin_signed)
        na = _na(qkv, na_rpb[l])
        sw = _swa(qkv, swa_sinks[l])
        x1, h2, r, rt, cntc, cntr = _mix(na, sw, x, beta_na[l], beta_swa[l], w_out[l].astype(BF16), g_post_mix[l],
                                          gate_a, g_pre_ffn[l], scale_f, shift_f,
                                          _router_weights(w_group_router[l], w_expert_router[l]))
        pos, items = _plan(rt, cntc, cntr)
        xs = _dispatch(pos, h2)
        ys = _experts(items, xs, w_gate[l], w_up[l], w_down[l])
        x = _combine(pos, ys, r, x1, gate_f, g_post_ffn[l])
    return x
```

```python
import functools

import jax
import jax.numpy as jnp
from jax import lax
from jax.experimental import pallas as pl
from jax.experimental.pallas import tpu as pltpu

D_MODEL = 1024
GRID_W = 64
HEAD_DIM = 64
NA_HEADS = 8
NA_KH = 8
NA_KW = 16
SWA_HEADS = 8
SWA_KV_HEADS = 2
SWA_WINDOW = 128
SWA_BLOCK = 128
ROPE_THETA = 10000.0
NA_WIDTH = NA_HEADS * HEAD_DIM
SWA_WIDTH = SWA_HEADS * HEAD_DIM
N_GROUPS = 4
EXPERTS_PER_GROUP = 8
N_EXPERTS = N_GROUPS * EXPERTS_PER_GROUP
EXPERT_FF = 256
N_MOD = 6
EPS = 1e-6
NEG_INF = -1e30

LANES = 128
SUBLANES = 8
R_E1, R_E2, R_W1, R_W2 = range(4)
PAIRS = NA_HEADS // 2
COL_NQ, COL_NK, COL_NV, COL_SQ, COL_SK, COL_SV = 0, 4, 8, 12, 16, 18
QKV_TILES = 20
ROUTER_BASE = N_GROUPS
VMEM_LIMIT = 56 * 1024 * 1024

F32 = jnp.float32
BF16 = jnp.bfloat16


def _rms(v):
    return v * lax.rsqrt(jnp.mean(v * v, axis=-1, keepdims=True) + EPS)


def _params(*sem):
    return pltpu.CompilerParams(dimension_semantics=sem, vmem_limit_bytes=VMEM_LIMIT)


def _adaln_kernel(c_ref, w_ref, b_ref, o_ref):
    c = c_ref[...]
    a = c * jax.nn.sigmoid(c)
    o_ref[...] = jnp.dot(a, w_ref[...], precision=lax.Precision.HIGHEST,
                         preferred_element_type=F32) + b_ref[...]


def _adaln(c, w_ada, b_ada):
    batch, d = c.shape
    n = w_ada.shape[1]
    tn = 1024
    b = 8
    c = jnp.pad(c, ((0, b - batch), (0, 0)))
    return pl.pallas_call(
        _adaln_kernel,
        out_shape=jax.ShapeDtypeStruct((b, n), F32),
        grid=(n // tn,),
        in_specs=[pl.BlockSpec((b, d), lambda j: (0, 0)),
                  pl.BlockSpec((d, tn), lambda j: (0, j)),
                  pl.BlockSpec((1, tn), lambda j: (0, j))],
        out_specs=pl.BlockSpec((b, tn), lambda j: (0, j)),
        compiler_params=_params("arbitrary"),
        name="adaln",
    )(c, w_ada, b_ada.reshape(1, n))[:batch]


def _rope(v, cos, sin_signed, first_half):
    rot = jnp.where(first_half, pltpu.roll(v, LANES - HEAD_DIM // 2, 1), pltpu.roll(v, HEAD_DIM // 2, 1))
    return v * cos + rot * sin_signed


def _qkv_kernel(x_ref, g_ref, sc_ref, sh_ref, w_ref, cos_ref, sin_ref, o_ref):
    x = x_ref[0]
    h = (_rms(x) * g_ref[...]) * (1.0 + sc_ref[0]) + sh_ref[0]
    h = h.astype(BF16)
    scale = HEAD_DIM ** -0.5
    cos = cos_ref[...]
    sin = sin_ref[...]
    lane = lax.broadcasted_iota(jnp.int32, cos.shape, 1)
    first_half = (lane % HEAD_DIM) < HEAD_DIM // 2
    upper = lane >= HEAD_DIM

    def proj(col, width):
        return jnp.dot(h, w_ref[:, col * LANES:(col + width) * LANES], preferred_element_type=F32)

    def tile(v, j):
        return v[:, j * LANES:(j + 1) * LANES]

    nq, nk, nv, sq = proj(COL_NQ, 4), proj(COL_NK, 4), proj(COL_NV, 4), proj(COL_SQ, 4)
    for j in range(PAIRS):
        o_ref[0, COL_NQ + j] = (tile(nq, j) * scale).astype(BF16)
        o_ref[0, COL_NK + j] = tile(nk, j).astype(BF16)
        o_ref[0, COL_NV + j] = tile(nv, j).astype(BF16)
        o_ref[0, COL_SQ + j] = (_rope(tile(sq, j), cos, sin, first_half) * scale).astype(BF16)
    skv = proj(COL_SK, 2)
    k = _rope(tile(skv, 0), cos, sin, first_half)
    v = tile(skv, 1)
    for t, col in ((k, COL_SK), (v, COL_SV)):
        swapped = pltpu.roll(t, HEAD_DIM, 1)
        o_ref[0, col] = jnp.where(upper, swapped, t).astype(BF16)
        o_ref[0, col + 1] = jnp.where(upper, t, swapped).astype(BF16)


def _qkv(x, g, scale_a, shift_a, w_in, cos, sin):
    b, s, d = x.shape
    tm = 512
    n_in = w_in.shape[1]
    return pl.pallas_call(
        _qkv_kernel,
        out_shape=jax.ShapeDtypeStruct((b, QKV_TILES, s, LANES), BF16),
        grid=(b, s // tm),
        in_specs=[pl.BlockSpec((1, tm, d), lambda bi, i: (bi, i, 0)),
                  pl.BlockSpec((1, d), lambda bi, i: (0, 0)),
                  pl.BlockSpec((1, 1, d), lambda bi, i: (bi, 0, 0)),
                  pl.BlockSpec((1, 1, d), lambda bi, i: (bi, 0, 0)),
                  pl.BlockSpec((d, n_in), lambda bi, i: (0, 0)),
                  pl.BlockSpec((tm, LANES), lambda bi, i: (i, 0)),
                  pl.BlockSpec((tm, LANES), lambda bi, i: (i, 0))],
        out_specs=pl.BlockSpec((1, QKV_TILES, tm, LANES), lambda bi, i: (bi, 0, i, 0)),
        compiler_params=_params("arbitrary", "arbitrary"),
        name="qkv",
    )(x, g.reshape(1, d), scale_a, shift_a, w_in, cos, sin)


NA_QROWS = 4
NA_KROWS = NA_QROWS + NA_KH
NA_BLOCKS_PER_STEP = 4
NA_Q = NA_QROWS * GRID_W
NA_K = NA_KROWS * GRID_W
NA_RPB_ROWS = 2 * NA_KH - 1
NA_RPB_COLS = 2 * NA_KW - 1
NA_TYPES = 3


def _na_first_key_row(block, rows, clip):
    return clip(block * NA_QROWS - NA_KH // 2, 0, rows - NA_KROWS)


def _na_kernel(q_ref, k_ref, v_ref, bias_ref, o_ref, *, rows):
    lane = lax.broadcasted_iota(jnp.int32, (NA_Q, LANES), 1)
    upper = lane >= HEAD_DIM
    for c in range(NA_BLOCKS_PER_STEP):
        block = pl.program_id(2) * NA_BLOCKS_PER_STEP + c
        a = _na_first_key_row(block, rows, jnp.clip)
        ty = jnp.where(block == 0, 0, jnp.where(block == rows // NA_QROWS - 1, 2, 1))
        start = pl.multiple_of(a * GRID_W, GRID_W)
        q = q_ref[0, 0, c * NA_Q:(c + 1) * NA_Q, :]
        ks = k_ref[0, 0, pl.ds(start, NA_K), :]
        vs = v_ref[0, 0, pl.ds(start, NA_K), :]
        outs = []
        for hh in range(2):
            qm = jnp.where(upper if hh else ~upper, q, jnp.zeros_like(q))
            s = lax.dot_general(qm, ks, (((1,), (1,)), ((), ())), preferred_element_type=F32)
            s = s + bias_ref[hh, ty]
            m = jnp.max(s, axis=-1, keepdims=True)
            e = jnp.exp(s - m)
            l = jnp.sum(e, axis=-1, keepdims=True)
            o = jnp.dot(e.astype(BF16), vs, preferred_element_type=F32)
            outs.append(o / l)
        o_ref[0, 0, c * NA_Q:(c + 1) * NA_Q, :] = jnp.where(upper, outs[1], outs[0]).astype(BF16)


def _na_bias_kernel(rpb_ref, o_ref, *, rows):
    h = pl.program_id(0)
    q = lax.broadcasted_iota(jnp.int32, (GRID_W, LANES), 0)
    lane = lax.broadcasted_iota(jnp.int32, (GRID_W, LANES), 1)
    kc = lane % GRID_W
    upper = lane >= GRID_W
    c0 = jnp.clip(q - NA_KW // 2, 0, GRID_W - NA_KW)
    in_cols = (kc >= c0) & (kc < c0 + NA_KW)
    dc = kc - q + NA_KW - 1
    neg = jnp.full((GRID_W, LANES), NEG_INF, F32)
    base = h * NA_RPB_ROWS * NA_RPB_COLS
    by_row_offset = []
    for d in range(NA_RPB_ROWS):
        acc = neg
        for dd in range(NA_RPB_COLS):
            acc = jnp.where(dc == dd, rpb_ref[base + d * NA_RPB_COLS + dd], acc)
        by_row_offset.append(jnp.where(in_cols, acc, NEG_INF))
    clamp = lambda v, lo, hi: min(max(v, lo), hi)
    for ty, block in enumerate((0, 1, rows // NA_QROWS - 1)):
        r = block * NA_QROWS
        a = _na_first_key_row(block, rows, clamp)
        for j in range(NA_QROWS):
            r0 = min(max(r + j - NA_KH // 2, 0), rows - NA_KH)
            halves = []
            for i in range(NA_KROWS):
                in_rows = r0 <= a + i < r0 + NA_KH
                halves.append(by_row_offset[a + i - (r + j) + NA_KH - 1] if in_rows else neg)
            for t in range(NA_KROWS // 2):
                tile = jnp.where(upper, halves[2 * t + 1], halves[2 * t])
                o_ref[0, ty, j * GRID_W:(j + 1) * GRID_W, t * LANES:(t + 1) * LANES] = tile


def _na_bias(rpb, rows):
    return pl.pallas_call(
        functools.partial(_na_bias_kernel, rows=rows),
        out_shape=jax.ShapeDtypeStruct((NA_HEADS, NA_TYPES, NA_Q, NA_K), F32),
        grid=(NA_HEADS,),
        in_specs=[pl.BlockSpec(memory_space=pltpu.SMEM)],
        out_specs=pl.BlockSpec((1, NA_TYPES, NA_Q, NA_K), lambda h: (h, 0, 0, 0)),
        compiler_params=_params("arbitrary"),
        name="na_bias",
    )(rpb.astype(F32).reshape(-1))


def _na(qkv, rpb):
    b, _, s, _ = qkv.shape
    rows = s // GRID_W
    assert rows % (NA_QROWS * NA_BLOCKS_PER_STEP) == 0 and rows // NA_QROWS >= NA_TYPES
    tq = NA_BLOCKS_PER_STEP * NA_Q
    return pl.pallas_call(
        functools.partial(_na_kernel, rows=rows),
        out_shape=jax.ShapeDtypeStruct((b, PAIRS, s, LANES), BF16),
        grid=(b, PAIRS, s // tq),
        in_specs=[pl.BlockSpec((1, 1, tq, LANES), lambda bi, p, i: (bi, COL_NQ + p, i, 0)),
                  pl.BlockSpec((1, 1, s, LANES), lambda bi, p, i: (bi, COL_NK + p, 0, 0)),
                  pl.BlockSpec((1, 1, s, LANES), lambda bi, p, i: (bi, COL_NV + p, 0, 0)),
                  pl.BlockSpec((2, NA_TYPES, NA_Q, NA_K), lambda bi, p, i: (p, 0, 0, 0))],
        out_specs=pl.BlockSpec((1, 1, tq, LANES), lambda bi, p, i: (bi, p, i, 0)),
        compiler_params=_params("arbitrary", "arbitrary", "arbitrary"),
        name="na",
    )(qkv, qkv, qkv, _na_bias(rpb, rows))


SWA_KEYS = 3 * SWA_BLOCK
SWA_BLOCKS_PER_STEP = 4
SWA_GROUP = SWA_HEADS // SWA_KV_HEADS
assert COL_SQ % (PAIRS // SWA_KV_HEADS) == 0


def _swa_kernel(sink_ref, q_ref, k_ref, v_ref, o_ref, *, seq):
    kv = pl.program_id(1)
    rows = SWA_GROUP * SWA_BLOCK
    lane = lax.broadcasted_iota(jnp.int32, (SWA_BLOCK, LANES), 1)
    upper = lane >= HEAD_DIM
    row = lax.broadcasted_iota(jnp.int32, (rows, SWA_KEYS), 0)
    q_off = row % SWA_BLOCK
    k_off = lax.broadcasted_iota(jnp.int32, (rows, SWA_KEYS), 1)
    head = lax.broadcasted_iota(jnp.int32, (rows, 1), 0) // SWA_BLOCK
    sink = jnp.zeros((rows, 1), F32)
    for g in range(SWA_GROUP):
        sink = jnp.where(head == g, sink_ref[kv * SWA_GROUP + g], sink)
    for j in range(SWA_BLOCKS_PER_STEP):
        n = pl.program_id(2) * SWA_BLOCKS_PER_STEP + j
        start = pl.multiple_of(jnp.clip((n - 1) * SWA_BLOCK, 0, seq - SWA_KEYS), SWA_BLOCK)
        ks = k_ref[0, 0, pl.ds(start, SWA_KEYS), :]
        vs = v_ref[0, 0, pl.ds(start, SWA_KEYS), :]
        qs = []
        for g in range(SWA_GROUP):
            q = q_ref[0, g // 2, j * SWA_BLOCK:(j + 1) * SWA_BLOCK, :]
            qs.append(jnp.where(upper if g % 2 else ~upper, q, jnp.zeros_like(q)))
        s = lax.dot_general(jnp.concatenate(qs, axis=0), ks, (((1,), (1,)), ((), ())), preferred_element_type=F32)
        in_window = jnp.abs((start + k_off) - (n * SWA_BLOCK + q_off)) <= SWA_WINDOW
        s = jnp.where(in_window, s, NEG_INF)
        m = jnp.maximum(jnp.max(s, axis=-1, keepdims=True), sink)
        e = jnp.exp(s - m)
        l = jnp.sum(e, axis=-1, keepdims=True) + jnp.exp(sink - m)
        o = jnp.dot(e.astype(BF16), vs, preferred_element_type=F32) / l
        for pair in range(SWA_GROUP // 2):
            even = o[(2 * pair) * SWA_BLOCK:(2 * pair + 1) * SWA_BLOCK]
            odd = o[(2 * pair + 1) * SWA_BLOCK:(2 * pair + 2) * SWA_BLOCK]
            o_ref[0, pair, j * SWA_BLOCK:(j + 1) * SWA_BLOCK, :] = jnp.where(upper, odd, even).astype(BF16)


def _swa(qkv, sinks):
    b, _, s, _ = qkv.shape
    pairs_per_kv = PAIRS // SWA_KV_HEADS
    tq = SWA_BLOCKS_PER_STEP * SWA_BLOCK
    return pl.pallas_call(
        functools.partial(_swa_kernel, seq=s),
        out_shape=jax.ShapeDtypeStruct((b, PAIRS, s, LANES), BF16),
        grid=(b, SWA_KV_HEADS, s // tq),
        in_specs=[pl.BlockSpec(memory_space=pltpu.SMEM),
                  pl.BlockSpec((1, pairs_per_kv, tq, LANES),
                               lambda bi, kv, n: (bi, COL_SQ // pairs_per_kv + kv, n, 0)),
                  pl.BlockSpec((1, 1, s, LANES), lambda bi, kv, n: (bi, COL_SK + kv, 0, 0)),
                  pl.BlockSpec((1, 1, s, LANES), lambda bi, kv, n: (bi, COL_SV + kv, 0, 0))],
        out_specs=pl.BlockSpec((1, pairs_per_kv, tq, LANES), lambda bi, kv, n: (bi, kv, n, 0)),
        compiler_params=_params("arbitrary", "arbitrary", "arbitrary"),
        name="swa",
    )(sinks, qkv, qkv, qkv)


ROUTER_COLS = N_GROUPS + N_EXPERTS


def _split2(v):
    hi = v.astype(BF16)
    lo = (v - hi.astype(F32)).astype(BF16)
    return hi, lo


def _route(logits):
    lane = lax.broadcasted_iota(jnp.int32, logits.shape, 1)
    big = jnp.int32(LANES)
    gmask = lane < N_GROUPS
    gl = jnp.where(gmask, logits, NEG_INF)
    gmax = jnp.max(gl, axis=-1, keepdims=True)
    g_top = jnp.min(jnp.where(gmask & (gl == gmax), lane, big), axis=-1, keepdims=True)
    g_weight = 1.0 / jnp.sum(jnp.where(gmask, jnp.exp(gl - gmax), 0.0), axis=-1, keepdims=True)
    lo = ROUTER_BASE + g_top * EXPERTS_PER_GROUP
    emask = (lane >= lo) & (lane < lo + EXPERTS_PER_GROUP)
    el = jnp.where(emask, logits, NEG_INF)
    m1 = jnp.max(el, axis=-1, keepdims=True)
    i1 = jnp.min(jnp.where(emask & (el == m1), lane, big), axis=-1, keepdims=True)
    emask2 = emask & (lane != i1)
    el2 = jnp.where(emask2, logits, NEG_INF)
    m2 = jnp.max(el2, axis=-1, keepdims=True)
    i2 = jnp.min(jnp.where(emask2 & (el2 == m2), lane, big), axis=-1, keepdims=True)
    e2 = jnp.exp(m2 - m1)
    w1 = g_weight / (1.0 + e2)
    w2 = g_weight * e2 / (1.0 + e2)
    return i1 - ROUTER_BASE, i2 - ROUTER_BASE, w1, w2


def _to_token_tiles(ref, v, base=0):
    t = v.shape[0]
    for s in range(SUBLANES):
        ref[pl.ds(base + s, t, stride=SUBLANES), :] = v[:, s * LANES:(s + 1) * LANES]


def _from_token_tiles(ref, t, base=0):
    return jnp.concatenate([ref[pl.ds(base + s, t, stride=SUBLANES), :] for s in range(SUBLANES)], axis=-1)


def _mix_kernel(na_ref, sw_ref, x_ref, bna_ref, bsw_ref, wo_ref, gpm_ref, ga_ref, gpf_ref, scf_ref, shf_ref,
                wr_ref, x1_ref, h2_ref, r_ref, rt_ref, cntc_ref, cntr_ref):
    def heads(ref):
        return jnp.concatenate([ref[0, j] for j in range(PAIRS)], axis=-1).astype(F32)

    na = (_rms(heads(na_ref)) * bna_ref[...]).astype(BF16)
    sw = (_rms(heads(sw_ref)) * bsw_ref[...]).astype(BF16)
    mix = (jnp.dot(na, wo_ref[:NA_WIDTH, :], preferred_element_type=F32)
           + jnp.dot(sw, wo_ref[NA_WIDTH:, :], preferred_element_type=F32))
    x1 = x_ref[0] + ga_ref[0] * (_rms(mix) * gpm_ref[...])
    x1_ref[0] = x1
    h2 = (_rms(x1) * gpf_ref[...]) * (1.0 + scf_ref[0]) + shf_ref[0]
    _to_token_tiles(h2_ref, h2)
    h_hi, h_lo = _split2(h2)
    both = (jnp.dot(h_hi, wr_ref[...], preferred_element_type=F32)
            + jnp.dot(h_lo, wr_ref[...], preferred_element_type=F32))
    logits = both + pltpu.roll(both, LANES - ROUTER_COLS, 1)
    e1, e2, w1, w2 = _route(logits)
    lane = lax.broadcasted_iota(jnp.int32, logits.shape, 1)
    r = jnp.where(lane == R_E1, e1.astype(F32),
                  jnp.where(lane == R_E2, e2.astype(F32),
                            jnp.where(lane == R_W1, w1, jnp.where(lane == R_W2, w2, 0.0))))
    r_ref[...] = r
    rt = jnp.transpose(r)[:SUBLANES]
    rt_ref[...] = rt
    first_step = (pl.program_id(0) == 0) & (pl.program_id(1) == 0)

    @pl.when(first_step)
    def _():
        cntc_ref[...] = jnp.zeros_like(cntc_ref)
        cntr_ref[...] = jnp.zeros_like(cntr_ref)

    on_lane = ((lane == e1) | (lane == e2)).astype(F32)
    cntr_ref[...] += jnp.broadcast_to(jnp.sum(on_lane, axis=0, keepdims=True), cntr_ref.shape)
    sub = lax.broadcasted_iota(jnp.int32, (N_EXPERTS, rt.shape[1]), 0).astype(F32)
    on_sub = ((sub == rt[R_E1:R_E1 + 1]) | (sub == rt[R_E2:R_E2 + 1])).astype(F32)
    cntc_ref[...] += jnp.broadcast_to(jnp.sum(on_sub, axis=1, keepdims=True), cntc_ref.shape)


def _mix(na, sw, x, beta_na, beta_swa, w_out, g_post_mix, gate_a, g_pre_ffn, scale_f, shift_f, w_router3):
    b, s, d = x.shape
    tm = 512
    nt = s // tm
    row = lambda bi, i: (bi, i, 0)
    const2 = lambda bi, i: (0, 0)
    per_b = lambda bi, i: (bi, 0, 0)
    return pl.pallas_call(
        _mix_kernel,
        out_shape=(jax.ShapeDtypeStruct((b, s, d), F32),
                   jax.ShapeDtypeStruct((b * s * SUBLANES, LANES), F32),
                   jax.ShapeDtypeStruct((b * s, LANES), F32),
                   jax.ShapeDtypeStruct((SUBLANES, b * s), F32),
                   jax.ShapeDtypeStruct((N_EXPERTS, LANES), F32),
                   jax.ShapeDtypeStruct((SUBLANES, LANES), F32)),
        grid=(b, s // tm),
        in_specs=[pl.BlockSpec((1, PAIRS, tm, LANES), lambda bi, i: (bi, 0, i, 0)),
                  pl.BlockSpec((1, PAIRS, tm, LANES), lambda bi, i: (bi, 0, i, 0)),
                  pl.BlockSpec((1, tm, d), row),
                  pl.BlockSpec((1, NA_WIDTH), const2),
                  pl.BlockSpec((1, SWA_WIDTH), const2),
                  pl.BlockSpec((NA_WIDTH + SWA_WIDTH, d), const2),
                  pl.BlockSpec((1, d), const2),
                  pl.BlockSpec((1, 1, d), per_b),
                  pl.BlockSpec((1, d), const2),
                  pl.BlockSpec((1, 1, d), per_b),
                  pl.BlockSpec((1, 1, d), per_b),
                  pl.BlockSpec((d, LANES), const2)],
        out_specs=(pl.BlockSpec((1, tm, d), row),
                   pl.BlockSpec((tm * SUBLANES, LANES), lambda bi, i: (bi * nt + i, 0)),
                   pl.BlockSpec((tm, LANES), lambda bi, i: (bi * nt + i, 0)),
                   pl.BlockSpec((SUBLANES, tm), lambda bi, i: (0, bi * nt + i)),
                   pl.BlockSpec((N_EXPERTS, LANES), const2),
                   pl.BlockSpec((SUBLANES, LANES), const2)),
        compiler_params=_params("arbitrary", "arbitrary"),
        name="mix",
    )(na, sw, x, beta_na.reshape(1, -1), beta_swa.reshape(1, -1), w_out, g_post_mix.reshape(1, d), gate_a,
      g_pre_ffn.reshape(1, d), scale_f, shift_f, w_router3)


MOE_TILE = 256
PLAN_T = 512
I_TILE, I_EXPERT, I_LO, I_HI, I_FIRST = range(5)


def _plan_kernel(rt_ref, cntc_ref, cntr_ref, pos_ref, items_ref, start_ref, carry_ref, *, n_tiles, n_items):
    i = pl.program_id(0)
    sub = lax.broadcasted_iota(jnp.int32, (N_EXPERTS, LANES), 0)
    lane = lax.broadcasted_iota(jnp.int32, (N_EXPERTS, LANES), 1)

    @pl.when(i == 0)
    def _():
        c_col = cntc_ref[:, 0:1]
        c_row = cntr_ref[0:1, :]
        s_col = jnp.sum(jnp.where(lane < sub, c_row, 0.0), axis=1, keepdims=True)
        s_row = jnp.sum(jnp.where(sub < lane, c_col, 0.0), axis=0, keepdims=True)
        start_ref[...] = jnp.broadcast_to(s_col, start_ref.shape)
        carry_ref[...] = jnp.zeros_like(carry_ref)

        def tiles_of(s, c):
            first = jnp.floor(s * (1.0 / MOE_TILE))
            last = jnp.floor((s + c - 1.0) * (1.0 / MOE_TILE))
            return first, jnp.where(c > 0.0, last - first + 1.0, 0.0)

        f_col, n_col = tiles_of(s_col, c_col)
        _, n_row = tiles_of(s_row, c_row)
        i_col = jnp.sum(jnp.where(lane < sub, n_row, 0.0), axis=1, keepdims=True)
        total = jnp.sum(n_col, axis=0, keepdims=True)
        k = lax.broadcasted_iota(jnp.int32, (N_EXPERTS, n_items), 1).astype(F32)
        subk = lax.broadcasted_iota(jnp.int32, (N_EXPERTS, n_items), 0).astype(F32)
        ek = jnp.sum(jnp.where(i_col + n_col <= k, 1.0, 0.0), axis=0, keepdims=True)
        k0 = k[0:1]
        valid = k0 < total
        sel = subk == ek

        def pick(v):
            return jnp.sum(jnp.where(sel, v, 0.0), axis=0, keepdims=True)

        i_k, f_k, s_k, c_k = pick(i_col), pick(f_col), pick(s_col), pick(c_col)
        tile = f_k + (k0 - i_k)
        row0 = tile * MOE_TILE
        lo = jnp.maximum(s_k, row0) - row0
        hi = jnp.minimum(s_k + c_k, row0 + MOE_TILE) - row0
        rows = [jnp.where(valid, tile, n_tiles - 1.0), jnp.where(valid, ek, N_EXPERTS - 1.0),
                jnp.where(valid, lo, 0.0), jnp.where(valid, hi, 0.0),
                jnp.where(valid & (lo == 0.0), 1.0, 0.0)]
        rows += [jnp.zeros_like(k0)] * (SUBLANES - len(rows))
        items_ref[...] = jnp.concatenate(rows, axis=0).astype(jnp.int32)

    t = rt_ref.shape[1]
    e1 = rt_ref[R_E1:R_E1 + 1, :]
    e2 = rt_ref[R_E2:R_E2 + 1, :]
    sub_t = lax.broadcasted_iota(jnp.int32, (N_EXPERTS, t), 0).astype(F32)
    oh1 = sub_t == e1
    oh2 = sub_t == e2
    oh = (oh1 | oh2).astype(F32)
    before = (lax.broadcasted_iota(jnp.int32, (t, t), 0) < lax.broadcasted_iota(jnp.int32, (t, t), 1)).astype(BF16)
    rank = jnp.dot(oh.astype(BF16), before, preferred_element_type=F32)
    base = start_ref[:, 0:1] + carry_ref[:, 0:1] + rank
    pos1 = jnp.sum(jnp.where(oh1, base, 0.0), axis=0, keepdims=True)
    pos2 = jnp.sum(jnp.where(oh2, base, 0.0), axis=0, keepdims=True)
    carry_ref[...] += jnp.broadcast_to(jnp.sum(oh, axis=1, keepdims=True), carry_ref.shape)
    pos = jnp.concatenate([pos1, pos2] + [jnp.zeros_like(pos1)] * (SUBLANES - 2), axis=0)
    pos_ref[...] = pos.astype(jnp.int32)


def _plan(rt, cntc, cntr):
    n = rt.shape[1]
    n_tiles = 2 * n // MOE_TILE
    n_items = 2 * LANES
    assert n_tiles + N_EXPERTS <= n_items
    return pl.pallas_call(
        functools.partial(_plan_kernel, n_tiles=n_tiles, n_items=n_items),
        out_shape=(jax.ShapeDtypeStruct((SUBLANES, n), jnp.int32),
                   jax.ShapeDtypeStruct((SUBLANES, n_items), jnp.int32)),
        grid=(n // PLAN_T,),
        in_specs=[pl.BlockSpec((SUBLANES, PLAN_T), lambda i: (0, i)),
                  pl.BlockSpec((N_EXPERTS, LANES), lambda i: (0, 0)),
                  pl.BlockSpec((SUBLANES, LANES), lambda i: (0, 0))],
        out_specs=(pl.BlockSpec((SUBLANES, PLAN_T), lambda i: (0, i)),
                   pl.BlockSpec((SUBLANES, n_items), lambda i: (0, 0))),
        scratch_shapes=[pltpu.VMEM((N_EXPERTS, LANES), F32), pltpu.VMEM((N_EXPERTS, LANES), F32)],
        compiler_params=_params("arbitrary"),
        name="plan",
    )(rt, cntc, cntr)


DISPATCH_T = 1024


def _token_rows(ref, index):
    return ref.at[pl.ds(pl.multiple_of(index * SUBLANES, SUBLANES), SUBLANES), :]


def _dispatch_kernel(pos_ref, h_ref, xs_hbm, sem):
    def body(r, carry):
        src = _token_rows(h_ref, r)
        for k in range(2):
            pltpu.make_async_copy(src, _token_rows(xs_hbm, pos_ref[k, r]), sem.at[0]).start(priority=k)
        return carry

    lax.fori_loop(0, DISPATCH_T, body, 0, unroll=8)
    for k in range(2):
        pltpu.make_async_copy(h_ref, xs_hbm.at[pl.ds(0, DISPATCH_T * SUBLANES), :], sem.at[0]).wait()


def _dispatch(pos, h2):
    n = pos.shape[1]
    return pl.pallas_call(
        _dispatch_kernel,
        out_shape=jax.ShapeDtypeStruct((2 * n * SUBLANES, LANES), F32),
        grid=(n // DISPATCH_T,),
        in_specs=[pl.BlockSpec((SUBLANES, DISPATCH_T), lambda i: (0, i), memory_space=pltpu.SMEM),
                  pl.BlockSpec((DISPATCH_T * SUBLANES, LANES), lambda i: (i, 0))],
        out_specs=pl.BlockSpec(memory_space=pl.ANY),
        scratch_shapes=[pltpu.SemaphoreType.DMA((1,))],
        compiler_params=_params("arbitrary"),
        name="dispatch",
    )(pos, h2)


def _expert_kernel(tile_ref, exp_ref, lo_ref, hi_ref, first_ref, xs_ref, wg_ref, wu_ref, wd_ref, o_ref):
    k = pl.program_id(0)
    lo = lo_ref[k]
    hi = hi_ref[k]

    @pl.when(first_ref[k] == 1)
    def _():
        o_ref[...] = jnp.zeros_like(o_ref)

    @pl.when(hi > lo)
    def _():
        x = _from_token_tiles(xs_ref, MOE_TILE).astype(BF16)
        gate = jnp.dot(x, wg_ref[0].astype(BF16), preferred_element_type=F32)
        up = jnp.dot(x, wu_ref[0].astype(BF16), preferred_element_type=F32)
        he = (gate * jax.nn.sigmoid(gate) * up).astype(BF16)
        ye = jnp.dot(he, wd_ref[0].astype(BF16), preferred_element_type=F32)
        row = lax.broadcasted_iota(jnp.int32, (MOE_TILE, LANES), 0)
        mine = (row >= lo) & (row < hi)
        for s in range(SUBLANES):
            rows = pl.ds(s, MOE_TILE, stride=SUBLANES)
            o_ref[rows, :] = jnp.where(mine, ye[:, s * LANES:(s + 1) * LANES], o_ref[rows, :])


def _experts(items, xs, w_gate, w_up, w_down):
    n_rows = xs.shape[0] // SUBLANES
    n_items = n_rows // MOE_TILE + N_EXPERTS
    d = w_gate.shape[1]
    tile_map = lambda k, tile, exp, lo, hi, first: (tile[k], 0)
    exp_map = lambda k, tile, exp, lo, hi, first: (exp[k], 0, 0)
    grid_spec = pltpu.PrefetchScalarGridSpec(
        num_scalar_prefetch=5,
        grid=(n_items,),
        in_specs=[pl.BlockSpec((MOE_TILE * SUBLANES, LANES), tile_map),
                  pl.BlockSpec((1, d, EXPERT_FF), exp_map),
                  pl.BlockSpec((1, d, EXPERT_FF), exp_map),
                  pl.BlockSpec((1, EXPERT_FF, d), exp_map)],
        out_specs=pl.BlockSpec((MOE_TILE * SUBLANES, LANES), tile_map))
    return pl.pallas_call(
        _expert_kernel,
        out_shape=jax.ShapeDtypeStruct(xs.shape, F32),
        grid_spec=grid_spec,
        compiler_params=_params("arbitrary"),
        name="experts",
    )(*(items[j, :n_items] for j in (I_TILE, I_EXPERT, I_LO, I_HI, I_FIRST)), xs, w_gate, w_up, w_down)


COMBINE_T = 256


def _combine_kernel(pos_ref, posn_ref, ys_hbm, r_ref, x1_ref, gf_ref, gpost_ref, o_ref, buf, sem):
    i = pl.program_id(0)
    n = pl.num_programs(0)
    slot = i % 2
    slot_rows = COMBINE_T * SUBLANES

    def issue(p_ref, sl):
        def body(r, carry):
            for k in range(2):
                dst = buf.at[sl, pl.ds(pl.multiple_of(k * slot_rows + r * SUBLANES, SUBLANES), SUBLANES), :]
                pltpu.make_async_copy(_token_rows(ys_hbm, p_ref[k, r]), dst, sem.at[sl]).start(priority=k)
            return carry

        lax.fori_loop(0, COMBINE_T, body, 0, unroll=8)

    @pl.when(i == 0)
    def _():
        issue(pos_ref, 0)

    @pl.when(i + 1 < n)
    def _():
        issue(posn_ref, 1 - slot)

    pltpu.make_async_copy(ys_hbm.at[pl.ds(0, 2 * slot_rows), :], buf.at[slot], sem.at[slot]).wait()
    ya = _from_token_tiles(buf.at[slot], COMBINE_T)
    yb = _from_token_tiles(buf.at[slot], COMBINE_T, base=slot_rows)
    r = r_ref[...]
    y = r[:, R_W1:R_W1 + 1] * ya + r[:, R_W2:R_W2 + 1] * yb
    o_ref[0] = x1_ref[0] + gf_ref[0] * (_rms(y) * gpost_ref[...])


def _combine(pos, ys, r, x1, gate_f, g_post_ffn):
    b, s, d = x1.shape
    nt = s // COMBINE_T
    n_steps = b * nt
    row = lambda i: (i // nt, i % nt, 0)
    pos_spec = lambda f: pl.BlockSpec((SUBLANES, COMBINE_T), lambda i: (0, f(i)), memory_space=pltpu.SMEM)
    return pl.pallas_call(
        _combine_kernel,
        out_shape=jax.ShapeDtypeStruct((b, s, d), F32),
        grid=(n_steps,),
        in_specs=[pos_spec(lambda i: i),
                  pos_spec(lambda i: jnp.minimum(i + 1, n_steps - 1)),
                  pl.BlockSpec(memory_space=pl.ANY),
                  pl.BlockSpec((COMBINE_T, LANES), lambda i: (i, 0)),
                  pl.BlockSpec((1, COMBINE_T, d), row),
                  pl.BlockSpec((1, 1, d), lambda i: (i // nt, 0, 0)),
                  pl.BlockSpec((1, d), lambda i: (0, 0))],
        out_specs=pl.BlockSpec((1, COMBINE_T, d), row),
        scratch_shapes=[pltpu.VMEM((2, 2 * COMBINE_T * SUBLANES, LANES), F32), pltpu.SemaphoreType.DMA((2,))],
        compiler_params=_params("arbitrary"),
        name="combine",
    )(pos, pos, ys, r, x1, gate_f, g_post_ffn.reshape(1, d))


def _rope_tables(s):
    half = HEAD_DIM // 2
    inv = ROPE_THETA ** (-jnp.arange(half, dtype=F32) * 2.0 / HEAD_DIM)
    ang = jnp.arange(s).astype(F32)[:, None] * inv[None, :]
    cos = jnp.cos(ang)
    sin = jnp.sin(ang)
    cos = jnp.concatenate([cos, cos, cos, cos], axis=-1)
    sin_signed = jnp.concatenate([-sin, sin, -sin, sin], axis=-1)
    return cos, sin_signed


def _router_weights(w_group_router, w_expert_router):
    d = w_group_router.shape[0]
    we = jnp.transpose(w_expert_router, (1, 0, 2)).reshape(d, N_EXPERTS)
    hi, lo = _split2(jnp.concatenate([w_group_router, we], axis=-1))
    return jnp.concatenate([hi, lo, jnp.zeros((d, LANES - 2 * ROUTER_COLS), BF16)], axis=-1)


def kernel(x, c, w_ada, b_ada, g_pre_mix, w_in, na_rpb, swa_sinks, beta_na, beta_swa, w_out, g_post_mix, g_pre_ffn,
           w_group_router, w_expert_router, w_gate, w_up, w_down, g_post_ffn):
    b, s, d = x.shape
    depth = w_ada.shape[0]
    cos, sin_signed = _rope_tables(s)
    for l in range(depth):
        mod = _adaln(c, w_ada[l], b_ada[l]).reshape(b, N_MOD, 1, d)
        shift_a, scale_a, gate_a, shift_f, scale_f, gate_f = (mod[:, k] for k in range(N_MOD))
        qkv = _qkv(x, g_pre_mix[l], scale_a, shift_a, w_in[l].astype(BF16), cos, sin_signed)
        na = _na(qkv, na_rpb[l])
        sw = _swa(qkv, swa_sinks[l])
        x1, h2, r, rt, cntc, cntr = _mix(na, sw, x, beta_na[l], beta_swa[l], w_out[l].astype(BF16), g_post_mix[l],
                                          gate_a, g_pre_ffn[l], scale_f, shift_f,
                                          _router_weights(w_group_router[l], w_expert_router[l]))
        pos, items = _plan(rt, cntc, cntr)
        xs = _dispatch(pos, h2)
        ys = _experts(items, xs, w_gate[l], w_up[l], w_down[l])
        x = _combine(pos, ys, r, x1, gate_f, g_post_ffn[l])
    return x
```

```python
import functools

import jax
import jax.numpy as jnp
from jax import lax
from jax.experimental import pallas as pl
from jax.experimental.pallas import tpu as pltpu

D_MODEL = 1024
GRID_W = 64
HEAD_DIM = 64
NA_HEADS = 8
NA_KH = 8
NA_KW = 16
SWA_HEADS = 8
SWA_KV_HEADS = 2
SWA_WINDOW = 128
SWA_BLOCK = 128
ROPE_THETA = 10000.0
NA_WIDTH = NA_HEADS * HEAD_DIM
SWA_WIDTH = SWA_HEADS * HEAD_DIM
N_GROUPS = 4
EXPERTS_PER_GROUP = 8
N_EXPERTS = N_GROUPS * EXPERTS_PER_GROUP
EXPERT_FF = 256
N_MOD = 6
EPS = 1e-6
NEG_INF = -1e30

LANES = 128
SUBLANES = 8
R_E1, R_E2, R_W1, R_W2 = range(4)
PAIRS = NA_HEADS // 2
COL_NQ, COL_NK, COL_NV, COL_SQ, COL_SK, COL_SV = 0, 4, 8, 12, 16, 18
QKV_TILES = 20
ROUTER_BASE = N_GROUPS
VMEM_LIMIT = 56 * 1024 * 1024

F32 = jnp.float32
BF16 = jnp.bfloat16


def _rms(v):
    return v * lax.rsqrt(jnp.mean(v * v, axis=-1, keepdims=True) + EPS)


def _params(*sem):
    return pltpu.CompilerParams(dimension_semantics=sem, vmem_limit_bytes=VMEM_LIMIT)


def _adaln_kernel(c_ref, w_ref, b_ref, o_ref):
    c = c_ref[...]
    a = c * jax.nn.sigmoid(c)
    o_ref[...] = jnp.dot(a, w_ref[...], precision=lax.Precision.HIGHEST,
                         preferred_element_type=F32) + b_ref[...]


def _adaln(c, w_ada, b_ada):
    batch, d = c.shape
    n = w_ada.shape[1]
    tn = 1024
    b = 8
    c = jnp.pad(c, ((0, b - batch), (0, 0)))
    return pl.pallas_call(
        _adaln_kernel,
        out_shape=jax.ShapeDtypeStruct((b, n), F32),
        grid=(n // tn,),
        in_specs=[pl.BlockSpec((b, d), lambda j: (0, 0)),
                  pl.BlockSpec((d, tn), lambda j: (0, j)),
                  pl.BlockSpec((1, tn), lambda j: (0, j))],
        out_specs=pl.BlockSpec((b, tn), lambda j: (0, j)),
        compiler_params=_params("arbitrary"),
        name="adaln",
    )(c, w_ada, b_ada.reshape(1, n))[:batch]


def _rope(v, cos, sin_signed, first_half):
    rot = jnp.where(first_half, pltpu.roll(v, LANES - HEAD_DIM // 2, 1), pltpu.roll(v, HEAD_DIM // 2, 1))
    return v * cos + rot * sin_signed


QKV_CHUNK = 256


def _qkv_kernel(x_ref, g_ref, sc_ref, sh_ref, w_ref, cos_ref, sin_ref, o_ref):
    scale = HEAD_DIM ** -0.5
    lane = lax.broadcasted_iota(jnp.int32, (QKV_CHUNK, LANES), 1)
    first_half = (lane % HEAD_DIM) < HEAD_DIM // 2
    upper = lane >= HEAD_DIM

    def tile(v, j):
        return v[:, j * LANES:(j + 1) * LANES]

    for c in range(x_ref.shape[1] // QKV_CHUNK):
        rows = slice(c * QKV_CHUNK, (c + 1) * QKV_CHUNK)
        h = (_rms(x_ref[0, rows, :]) * g_ref[...]) * (1.0 + sc_ref[0]) + sh_ref[0]
        h = h.astype(BF16)
        cos = cos_ref[rows, :]
        sin = sin_ref[rows, :]

        def proj(col, width):
            return jnp.dot(h, w_ref[:, col * LANES:(col + width) * LANES], preferred_element_type=F32)

        nq, nk, nv, sq = proj(COL_NQ, 4), proj(COL_NK, 4), proj(COL_NV, 4), proj(COL_SQ, 4)
        for j in range(PAIRS):
            o_ref[0, COL_NQ + j, rows, :] = (tile(nq, j) * scale).astype(BF16)
            o_ref[0, COL_NK + j, rows, :] = tile(nk, j).astype(BF16)
            o_ref[0, COL_NV + j, rows, :] = tile(nv, j).astype(BF16)
            o_ref[0, COL_SQ + j, rows, :] = (_rope(tile(sq, j), cos, sin, first_half) * scale).astype(BF16)
        skv = proj(COL_SK, 2)
        k = _rope(tile(skv, 0), cos, sin, first_half)
        v = tile(skv, 1)
        for t, col in ((k, COL_SK), (v, COL_SV)):
            swapped = pltpu.roll(t, HEAD_DIM, 1)
            o_ref[0, col, rows, :] = jnp.where(upper, swapped, t).astype(BF16)
            o_ref[0, col + 1, rows, :] = jnp.where(upper, t, swapped).astype(BF16)


def _qkv(x, g, scale_a, shift_a, w_in, cos, sin):
    b, s, d = x.shape
    tm = 512
    n_in = w_in.shape[1]
    return pl.pallas_call(
        _qkv_kernel,
        out_shape=jax.ShapeDtypeStruct((b, QKV_TILES, s, LANES), BF16),
        grid=(b, s // tm),
        in_specs=[pl.BlockSpec((1, tm, d), lambda bi, i: (bi, i, 0)),
                  pl.BlockSpec((1, d), lambda bi, i: (0, 0)),
                  pl.BlockSpec((1, 1, d), lambda bi, i: (bi, 0, 0)),
                  pl.BlockSpec((1, 1, d), lambda bi, i: (bi, 0, 0)),
                  pl.BlockSpec((d, n_in), lambda bi, i: (0, 0)),
                  pl.BlockSpec((tm, LANES), lambda bi, i: (i, 0)),
                  pl.BlockSpec((tm, LANES), lambda bi, i: (i, 0))],
        out_specs=pl.BlockSpec((1, QKV_TILES, tm, LANES), lambda bi, i: (bi, 0, i, 0)),
        compiler_params=_params("arbitrary", "arbitrary"),
        name="qkv",
    )(x, g.reshape(1, d), scale_a, shift_a, w_in, cos, sin)


NA_QROWS = 4
NA_KROWS = NA_QROWS + NA_KH
NA_BLOCKS_PER_STEP = 4
NA_Q = NA_QROWS * GRID_W
NA_K = NA_KROWS * GRID_W
NA_RPB_ROWS = 2 * NA_KH - 1
NA_RPB_COLS = 2 * NA_KW - 1
NA_TYPES = 3


def _na_first_key_row(block, rows, clip):
    return clip(block * NA_QROWS - NA_KH // 2, 0, rows - NA_KROWS)


def _na_kernel(q_ref, k_ref, v_ref, bias_ref, o_ref, *, rows):
    lane = lax.broadcasted_iota(jnp.int32, (NA_Q, LANES), 1)
    upper = lane >= HEAD_DIM
    for c in range(NA_BLOCKS_PER_STEP):
        block = pl.program_id(2) * NA_BLOCKS_PER_STEP + c
        a = _na_first_key_row(block, rows, jnp.clip)
        ty = jnp.where(block == 0, 0, jnp.where(block == rows // NA_QROWS - 1, 2, 1))
        start = pl.multiple_of(a * GRID_W, GRID_W)
        q = q_ref[0, 0, c * NA_Q:(c + 1) * NA_Q, :]
        ks = k_ref[0, 0, pl.ds(start, NA_K), :]
        vs = v_ref[0, 0, pl.ds(start, NA_K), :]
        outs = []
        for hh in range(2):
            qm = jnp.where(upper if hh else ~upper, q, jnp.zeros_like(q))
            s = lax.dot_general(qm, ks, (((1,), (1,)), ((), ())), preferred_element_type=F32)
            s = s + bias_ref[hh, ty]
            m = jnp.max(s, axis=-1, keepdims=True)
            e = jnp.exp(s - m)
            l = jnp.sum(e, axis=-1, keepdims=True)
            o = jnp.dot(e.astype(BF16), vs, preferred_element_type=F32)
            outs.append(o / l)
        o_ref[0, 0, c * NA_Q:(c + 1) * NA_Q, :] = jnp.where(upper, outs[1], outs[0]).astype(BF16)


def _na_bias_kernel(rpb_ref, o_ref, *, rows):
    h = pl.program_id(0)
    q = lax.broadcasted_iota(jnp.int32, (GRID_W, LANES), 0)
    lane = lax.broadcasted_iota(jnp.int32, (GRID_W, LANES), 1)
    kc = lane % GRID_W
    upper = lane >= GRID_W
    c0 = jnp.clip(q - NA_KW // 2, 0, GRID_W - NA_KW)
    in_cols = (kc >= c0) & (kc < c0 + NA_KW)
    dc = kc - q + NA_KW - 1
    neg = jnp.full((GRID_W, LANES), NEG_INF, F32)
    base = h * NA_RPB_ROWS * NA_RPB_COLS
    by_row_offset = []
    for d in range(NA_RPB_ROWS):
        acc = neg
        for dd in range(NA_RPB_COLS):
            acc = jnp.where(dc == dd, rpb_ref[base + d * NA_RPB_COLS + dd], acc)
        by_row_offset.append(jnp.where(in_cols, acc, NEG_INF))
    clamp = lambda v, lo, hi: min(max(v, lo), hi)
    for ty, block in enumerate((0, 1, rows // NA_QROWS - 1)):
        r = block * NA_QROWS
        a = _na_first_key_row(block, rows, clamp)
        for j in range(NA_QROWS):
            r0 = min(max(r + j - NA_KH // 2, 0), rows - NA_KH)
            halves = []
            for i in range(NA_KROWS):
                in_rows = r0 <= a + i < r0 + NA_KH
                halves.append(by_row_offset[a + i - (r + j) + NA_KH - 1] if in_rows else neg)
            for t in range(NA_KROWS // 2):
                tile = jnp.where(upper, halves[2 * t + 1], halves[2 * t])
                o_ref[0, ty, j * GRID_W:(j + 1) * GRID_W, t * LANES:(t + 1) * LANES] = tile


def _na_bias(rpb, rows):
    return pl.pallas_call(
        functools.partial(_na_bias_kernel, rows=rows),
        out_shape=jax.ShapeDtypeStruct((NA_HEADS, NA_TYPES, NA_Q, NA_K), F32),
        grid=(NA_HEADS,),
        in_specs=[pl.BlockSpec(memory_space=pltpu.SMEM)],
        out_specs=pl.BlockSpec((1, NA_TYPES, NA_Q, NA_K), lambda h: (h, 0, 0, 0)),
        compiler_params=_params("arbitrary"),
        name="na_bias",
    )(rpb.astype(F32).reshape(-1))


def _na(qkv, rpb):
    b, _, s, _ = qkv.shape
    rows = s // GRID_W
    assert rows % (NA_QROWS * NA_BLOCKS_PER_STEP) == 0 and rows // NA_QROWS >= NA_TYPES
    tq = NA_BLOCKS_PER_STEP * NA_Q
    return pl.pallas_call(
        functools.partial(_na_kernel, rows=rows),
        out_shape=jax.ShapeDtypeStruct((b, PAIRS, s, LANES), BF16),
        grid=(b, PAIRS, s // tq),
        in_specs=[pl.BlockSpec((1, 1, tq, LANES), lambda bi, p, i: (bi, COL_NQ + p, i, 0)),
                  pl.BlockSpec((1, 1, s, LANES), lambda bi, p, i: (bi, COL_NK + p, 0, 0)),
                  pl.BlockSpec((1, 1, s, LANES), lambda bi, p, i: (bi, COL_NV + p, 0, 0)),
                  pl.BlockSpec((2, NA_TYPES, NA_Q, NA_K), lambda bi, p, i: (p, 0, 0, 0))],
        out_specs=pl.BlockSpec((1, 1, tq, LANES), lambda bi, p, i: (bi, p, i, 0)),
        compiler_params=_params("arbitrary", "arbitrary", "arbitrary"),
        name="na",
    )(qkv, qkv, qkv, _na_bias(rpb, rows))


SWA_KEYS = 3 * SWA_BLOCK
SWA_BLOCKS_PER_STEP = 4
SWA_GROUP = SWA_HEADS // SWA_KV_HEADS
assert COL_SQ % (PAIRS // SWA_KV_HEADS) == 0


def _swa_kernel(sink_ref, q_ref, k_ref, v_ref, o_ref, *, seq):
    kv = pl.program_id(1)
    rows = SWA_GROUP * SWA_BLOCK
    lane = lax.broadcasted_iota(jnp.int32, (SWA_BLOCK, LANES), 1)
    upper = lane >= HEAD_DIM
    row = lax.broadcasted_iota(jnp.int32, (rows, SWA_KEYS), 0)
    q_off = row % SWA_BLOCK
    k_off = lax.broadcasted_iota(jnp.int32, (rows, SWA_KEYS), 1)
    head = lax.broadcasted_iota(jnp.int32, (rows, 1), 0) // SWA_BLOCK
    sink = jnp.zeros((rows, 1), F32)
    for g in range(SWA_GROUP):
        sink = jnp.where(head == g, sink_ref[kv * SWA_GROUP + g], sink)
    for j in range(SWA_BLOCKS_PER_STEP):
        n = pl.program_id(2) * SWA_BLOCKS_PER_STEP + j
        start = pl.multiple_of(jnp.clip((n - 1) * SWA_BLOCK, 0, seq - SWA_KEYS), SWA_BLOCK)
        ks = k_ref[0, 0, pl.ds(start, SWA_KEYS), :]
        vs = v_ref[0, 0, pl.ds(start, SWA_KEYS), :]
        qs = []
        for g in range(SWA_GROUP):
            q = q_ref[0, g // 2, j * SWA_BLOCK:(j + 1) * SWA_BLOCK, :]
            qs.append(jnp.where(upper if g % 2 else ~upper, q, jnp.zeros_like(q)))
        s = lax.dot_general(jnp.concatenate(qs, axis=0), ks, (((1,), (1,)), ((), ())), preferred_element_type=F32)
        in_window = jnp.abs((start + k_off) - (n * SWA_BLOCK + q_off)) <= SWA_WINDOW
        s = jnp.where(in_window, s, NEG_INF)
        m = jnp.maximum(jnp.max(s, axis=-1, keepdims=True), sink)
        e = jnp.exp(s - m)
        l = jnp.sum(e, axis=-1, keepdims=True) + jnp.exp(sink - m)
        o = jnp.dot(e.astype(BF16), vs, preferred_element_type=F32) / l
        for pair in range(SWA_GROUP // 2):
            even = o[(2 * pair) * SWA_BLOCK:(2 * pair + 1) * SWA_BLOCK]
            odd = o[(2 * pair + 1) * SWA_BLOCK:(2 * pair + 2) * SWA_BLOCK]
            o_ref[0, pair, j * SWA_BLOCK:(j + 1) * SWA_BLOCK, :] = jnp.where(upper, odd, even).astype(BF16)


def _swa(qkv, sinks):
    b, _, s, _ = qkv.shape
    pairs_per_kv = PAIRS // SWA_KV_HEADS
    tq = SWA_BLOCKS_PER_STEP * SWA_BLOCK
    return pl.pallas_call(
        functools.partial(_swa_kernel, seq=s),
        out_shape=jax.ShapeDtypeStruct((b, PAIRS, s, LANES), BF16),
        grid=(b, SWA_KV_HEADS, s // tq),
        in_specs=[pl.BlockSpec(memory_space=pltpu.SMEM),
                  pl.BlockSpec((1, pairs_per_kv, tq, LANES),
                               lambda bi, kv, n: (bi, COL_SQ // pairs_per_kv + kv, n, 0)),
                  pl.BlockSpec((1, 1, s, LANES), lambda bi, kv, n: (bi, COL_SK + kv, 0, 0)),
                  pl.BlockSpec((1, 1, s, LANES), lambda bi, kv, n: (bi, COL_SV + kv, 0, 0))],
        out_specs=pl.BlockSpec((1, pairs_per_kv, tq, LANES), lambda bi, kv, n: (bi, kv, n, 0)),
        compiler_params=_params("arbitrary", "arbitrary", "arbitrary"),
        name="swa",
    )(sinks, qkv, qkv, qkv)


ROUTER_COLS = N_GROUPS + N_EXPERTS


def _split2(v):
    hi = v.astype(BF16)
    lo = (v - hi.astype(F32)).astype(BF16)
    return hi, lo


def _route(logits):
    lane = lax.broadcasted_iota(jnp.int32, logits.shape, 1)
    big = jnp.int32(LANES)
    gmask = lane < N_GROUPS
    gl = jnp.where(gmask, logits, NEG_INF)
    gmax = jnp.max(gl, axis=-1, keepdims=True)
    g_top = jnp.min(jnp.where(gmask & (gl == gmax), lane, big), axis=-1, keepdims=True)
    g_weight = 1.0 / jnp.sum(jnp.where(gmask, jnp.exp(gl - gmax), 0.0), axis=-1, keepdims=True)
    lo = ROUTER_BASE + g_top * EXPERTS_PER_GROUP
    emask = (lane >= lo) & (lane < lo + EXPERTS_PER_GROUP)
    el = jnp.where(emask, logits, NEG_INF)
    m1 = jnp.max(el, axis=-1, keepdims=True)
    i1 = jnp.min(jnp.where(emask & (el == m1), lane, big), axis=-1, keepdims=True)
    emask2 = emask & (lane != i1)
    el2 = jnp.where(emask2, logits, NEG_INF)
    m2 = jnp.max(el2, axis=-1, keepdims=True)
    i2 = jnp.min(jnp.where(emask2 & (el2 == m2), lane, big), axis=-1, keepdims=True)
    e2 = jnp.exp(m2 - m1)
    w1 = g_weight / (1.0 + e2)
    w2 = g_weight * e2 / (1.0 + e2)
    return i1 - ROUTER_BASE, i2 - ROUTER_BASE, w1, w2


def _to_token_tiles(ref, v, base=0):
    t = v.shape[0]
    for s in range(SUBLANES):
        ref[pl.ds(base + s, t, stride=SUBLANES), :] = v[:, s * LANES:(s + 1) * LANES]


def _from_token_tiles(ref, t, base=0):
    return jnp.concatenate([ref[pl.ds(base + s, t, stride=SUBLANES), :] for s in range(SUBLANES)], axis=-1)


MIX_CHUNK = 256


def _mix_kernel(na_ref, sw_ref, x_ref, bna_ref, bsw_ref, wo_ref, gpm_ref, ga_ref, gpf_ref, scf_ref, shf_ref,
                wr_ref, x1_ref, h2_ref, r_ref, rt_ref, cntc_ref, cntr_ref):
    first_step = (pl.program_id(0) == 0) & (pl.program_id(1) == 0)

    @pl.when(first_step)
    def _():
        cntc_ref[...] = jnp.zeros_like(cntc_ref)
        cntr_ref[...] = jnp.zeros_like(cntr_ref)

    t = MIX_CHUNK
    for c in range(x_ref.shape[1] // t):
        rows = slice(c * t, (c + 1) * t)

        def heads(ref):
            return jnp.concatenate([ref[0, j, rows, :] for j in range(PAIRS)], axis=-1).astype(F32)

        na = (_rms(heads(na_ref)) * bna_ref[...]).astype(BF16)
        sw = (_rms(heads(sw_ref)) * bsw_ref[...]).astype(BF16)
        mix = (jnp.dot(na, wo_ref[:NA_WIDTH, :], preferred_element_type=F32)
               + jnp.dot(sw, wo_ref[NA_WIDTH:, :], preferred_element_type=F32))
        x1 = x_ref[0, rows, :] + ga_ref[0] * (_rms(mix) * gpm_ref[...])
        x1_ref[0, rows, :] = x1
        h2 = (_rms(x1) * gpf_ref[...]) * (1.0 + scf_ref[0]) + shf_ref[0]
        _to_token_tiles(h2_ref, h2, base=c * t * SUBLANES)
        h_hi, h_lo = _split2(h2)
        both = (jnp.dot(h_hi, wr_ref[...], preferred_element_type=F32)
                + jnp.dot(h_lo, wr_ref[...], preferred_element_type=F32))
        logits = both + pltpu.roll(both, LANES - ROUTER_COLS, 1)
        e1, e2, w1, w2 = _route(logits)
        lane = lax.broadcasted_iota(jnp.int32, logits.shape, 1)
        r = jnp.where(lane == R_E1, e1.astype(F32),
                      jnp.where(lane == R_E2, e2.astype(F32),
                                jnp.where(lane == R_W1, w1, jnp.where(lane == R_W2, w2, 0.0))))
        r_ref[rows, :] = r
        rt = jnp.transpose(r)[:SUBLANES]
        rt_ref[:, rows] = rt
        on_lane = ((lane == e1) | (lane == e2)).astype(F32)
        cntr_ref[...] += jnp.broadcast_to(jnp.sum(on_lane, axis=0, keepdims=True), cntr_ref.shape)
        sub = lax.broadcasted_iota(jnp.int32, (N_EXPERTS, t), 0).astype(F32)
        on_sub = ((sub == rt[R_E1:R_E1 + 1]) | (sub == rt[R_E2:R_E2 + 1])).astype(F32)
        cntc_ref[...] += jnp.broadcast_to(jnp.sum(on_sub, axis=1, keepdims=True), cntc_ref.shape)


def _mix(na, sw, x, beta_na, beta_swa, w_out, g_post_mix, gate_a, g_pre_ffn, scale_f, shift_f, w_router3):
    b, s, d = x.shape
    tm = 512
    nt = s // tm
    row = lambda bi, i: (bi, i, 0)
    const2 = lambda bi, i: (0, 0)
    per_b = lambda bi, i: (bi, 0, 0)
    return pl.pallas_call(
        _mix_kernel,
        out_shape=(jax.ShapeDtypeStruct((b, s, d), F32),
                   jax.ShapeDtypeStruct((b * s * SUBLANES, LANES), F32),
                   jax.ShapeDtypeStruct((b * s, LANES), F32),
                   jax.ShapeDtypeStruct((SUBLANES, b * s), F32),
                   jax.ShapeDtypeStruct((N_EXPERTS, LANES), F32),
                   jax.ShapeDtypeStruct((SUBLANES, LANES), F32)),
        grid=(b, s // tm),
        in_specs=[pl.BlockSpec((1, PAIRS, tm, LANES), lambda bi, i: (bi, 0, i, 0)),
                  pl.BlockSpec((1, PAIRS, tm, LANES), lambda bi, i: (bi, 0, i, 0)),
                  pl.BlockSpec((1, tm, d), row),
                  pl.BlockSpec((1, NA_WIDTH), const2),
                  pl.BlockSpec((1, SWA_WIDTH), const2),
                  pl.BlockSpec((NA_WIDTH + SWA_WIDTH, d), const2),
                  pl.BlockSpec((1, d), const2),
                  pl.BlockSpec((1, 1, d), per_b),
                  pl.BlockSpec((1, d), const2),
                  pl.BlockSpec((1, 1, d), per_b),
                  pl.BlockSpec((1, 1, d), per_b),
                  pl.BlockSpec((d, LANES), const2)],
        out_specs=(pl.BlockSpec((1, tm, d), row),
                   pl.BlockSpec((tm * SUBLANES, LANES), lambda bi, i: (bi * nt + i, 0)),
                   pl.BlockSpec((tm, LANES), lambda bi, i: (bi * nt + i, 0)),
                   pl.BlockSpec((SUBLANES, tm), lambda bi, i: (0, bi * nt + i)),
                   pl.BlockSpec((N_EXPERTS, LANES), const2),
                   pl.BlockSpec((SUBLANES, LANES), const2)),
        compiler_params=_params("arbitrary", "arbitrary"),
        name="mix",
    )(na, sw, x, beta_na.reshape(1, -1), beta_swa.reshape(1, -1), w_out, g_post_mix.reshape(1, d), gate_a,
      g_pre_ffn.reshape(1, d), scale_f, shift_f, w_router3)


MOE_TILE = 256
PLAN_T = 512
I_TILE, I_EXPERT, I_LO, I_HI, I_FIRST = range(5)


def _plan_kernel(rt_ref, cntc_ref, cntr_ref, pos_ref, items_ref, start_ref, carry_ref, *, n_tiles, n_items):
    i = pl.program_id(0)
    sub = lax.broadcasted_iota(jnp.int32, (N_EXPERTS, LANES), 0)
    lane = lax.broadcasted_iota(jnp.int32, (N_EXPERTS, LANES), 1)

    @pl.when(i == 0)
    def _():
        c_col = cntc_ref[:, 0:1]
        c_row = cntr_ref[0:1, :]
        s_col = jnp.sum(jnp.where(lane < sub, c_row, 0.0), axis=1, keepdims=True)
        s_row = jnp.sum(jnp.where(sub < lane, c_col, 0.0), axis=0, keepdims=True)
        start_ref[...] = jnp.broadcast_to(s_col, start_ref.shape)
        carry_ref[...] = jnp.zeros_like(carry_ref)

        def tiles_of(s, c):
            first = jnp.floor(s * (1.0 / MOE_TILE))
            last = jnp.floor((s + c - 1.0) * (1.0 / MOE_TILE))
            return first, jnp.where(c > 0.0, last - first + 1.0, 0.0)

        f_col, n_col = tiles_of(s_col, c_col)
        _, n_row = tiles_of(s_row, c_row)
        i_col = jnp.sum(jnp.where(lane < sub, n_row, 0.0), axis=1, keepdims=True)
        total = jnp.sum(n_col, axis=0, keepdims=True)
        k = lax.broadcasted_iota(jnp.int32, (N_EXPERTS, n_items), 1).astype(F32)
        subk = lax.broadcasted_iota(jnp.int32, (N_EXPERTS, n_items), 0).astype(F32)
        ek = jnp.sum(jnp.where(i_col + n_col <= k, 1.0, 0.0), axis=0, keepdims=True)
        k0 = k[0:1]
        valid = k0 < total
        sel = subk == ek

        def pick(v):
            return jnp.sum(jnp.where(sel, v, 0.0), axis=0, keepdims=True)

        i_k, f_k, s_k, c_k = pick(i_col), pick(f_col), pick(s_col), pick(c_col)
        tile = f_k + (k0 - i_k)
        row0 = tile * MOE_TILE
        lo = jnp.maximum(s_k, row0) - row0
        hi = jnp.minimum(s_k + c_k, row0 + MOE_TILE) - row0
        rows = [jnp.where(valid, tile, n_tiles - 1.0), jnp.where(valid, ek, N_EXPERTS - 1.0),
                jnp.where(valid, lo, 0.0), jnp.where(valid, hi, 0.0),
                jnp.where(valid & (lo == 0.0), 1.0, 0.0)]
        rows += [jnp.zeros_like(k0)] * (SUBLANES - len(rows))
        items_ref[...] = jnp.concatenate(rows, axis=0).astype(jnp.int32)

    t = rt_ref.shape[1]
    e1 = rt_ref[R_E1:R_E1 + 1, :]
    e2 = rt_ref[R_E2:R_E2 + 1, :]
    sub_t = lax.broadcasted_iota(jnp.int32, (N_EXPERTS, t), 0).astype(F32)
    oh1 = sub_t == e1
    oh2 = sub_t == e2
    oh = (oh1 | oh2).astype(F32)
    before = (lax.broadcasted_iota(jnp.int32, (t, t), 0) < lax.broadcasted_iota(jnp.int32, (t, t), 1)).astype(BF16)
    rank = jnp.dot(oh.astype(BF16), before, preferred_element_type=F32)
    base = start_ref[:, 0:1] + carry_ref[:, 0:1] + rank
    pos1 = jnp.sum(jnp.where(oh1, base, 0.0), axis=0, keepdims=True)
    pos2 = jnp.sum(jnp.where(oh2, base, 0.0), axis=0, keepdims=True)
    carry_ref[...] += jnp.broadcast_to(jnp.sum(oh, axis=1, keepdims=True), carry_ref.shape)
    pos = jnp.concatenate([pos1, pos2] + [jnp.zeros_like(pos1)] * (SUBLANES - 2), axis=0)
    pos_ref[...] = pos.astype(jnp.int32)


def _plan(rt, cntc, cntr):
    n = rt.shape[1]
    n_tiles = 2 * n // MOE_TILE
    n_items = 2 * LANES
    assert n_tiles + N_EXPERTS <= n_items
    return pl.pallas_call(
        functools.partial(_plan_kernel, n_tiles=n_tiles, n_items=n_items),
        out_shape=(jax.ShapeDtypeStruct((SUBLANES, n), jnp.int32),
                   jax.ShapeDtypeStruct((SUBLANES, n_items), jnp.int32)),
        grid=(n // PLAN_T,),
        in_specs=[pl.BlockSpec((SUBLANES, PLAN_T), lambda i: (0, i)),
                  pl.BlockSpec((N_EXPERTS, LANES), lambda i: (0, 0)),
                  pl.BlockSpec((SUBLANES, LANES), lambda i: (0, 0))],
        out_specs=(pl.BlockSpec((SUBLANES, PLAN_T), lambda i: (0, i)),
                   pl.BlockSpec((SUBLANES, n_items), lambda i: (0, 0))),
        scratch_shapes=[pltpu.VMEM((N_EXPERTS, LANES), F32), pltpu.VMEM((N_EXPERTS, LANES), F32)],
        compiler_params=_params("arbitrary"),
        name="plan",
    )(rt, cntc, cntr)


DISPATCH_T = 1024


def _token_rows(ref, index):
    return ref.at[pl.ds(pl.multiple_of(index * SUBLANES, SUBLANES), SUBLANES), :]


def _dispatch_kernel(pos_ref, h_ref, xs_hbm, sem):
    def body(r, carry):
        src = _token_rows(h_ref, r)
        for k in range(2):
            pltpu.make_async_copy(src, _token_rows(xs_hbm, pos_ref[k, r]), sem.at[0]).start(priority=k)
        return carry

    lax.fori_loop(0, DISPATCH_T, body, 0, unroll=8)
    for k in range(2):
        pltpu.make_async_copy(h_ref, xs_hbm.at[pl.ds(0, DISPATCH_T * SUBLANES), :], sem.at[0]).wait()


def _dispatch(pos, h2):
    n = pos.shape[1]
    return pl.pallas_call(
        _dispatch_kernel,
        out_shape=jax.ShapeDtypeStruct((2 * n * SUBLANES, LANES), F32),
        grid=(n // DISPATCH_T,),
        in_specs=[pl.BlockSpec((SUBLANES, DISPATCH_T), lambda i: (0, i), memory_space=pltpu.SMEM),
                  pl.BlockSpec((DISPATCH_T * SUBLANES, LANES), lambda i: (i, 0))],
        out_specs=pl.BlockSpec(memory_space=pl.ANY),
        scratch_shapes=[pltpu.SemaphoreType.DMA((1,))],
        compiler_params=_params("arbitrary"),
        name="dispatch",
    )(pos, h2)


def _expert_kernel(tile_ref, exp_ref, lo_ref, hi_ref, first_ref, xs_ref, wg_ref, wu_ref, wd_ref, o_ref):
    k = pl.program_id(0)
    lo = lo_ref[k]
    hi = hi_ref[k]

    @pl.when(first_ref[k] == 1)
    def _():
        o_ref[...] = jnp.zeros_like(o_ref)

    @pl.when(hi > lo)
    def _():
        x = _from_token_tiles(xs_ref, MOE_TILE).astype(BF16)
        gate = jnp.dot(x, wg_ref[0].astype(BF16), preferred_element_type=F32)
        up = jnp.dot(x, wu_ref[0].astype(BF16), preferred_element_type=F32)
        he = (gate * jax.nn.sigmoid(gate) * up).astype(BF16)
        ye = jnp.dot(he, wd_ref[0].astype(BF16), preferred_element_type=F32)
        row = lax.broadcasted_iota(jnp.int32, (MOE_TILE, LANES), 0)
        mine = (row >= lo) & (row < hi)
        for s in range(SUBLANES):
            rows = pl.ds(s, MOE_TILE, stride=SUBLANES)
            o_ref[rows, :] = jnp.where(mine, ye[:, s * LANES:(s + 1) * LANES], o_ref[rows, :])


def _experts(items, xs, w_gate, w_up, w_down):
    n_rows = xs.shape[0] // SUBLANES
    n_items = n_rows // MOE_TILE + N_EXPERTS
    d = w_gate.shape[1]
    tile_map = lambda k, tile, exp, lo, hi, first: (tile[k], 0)
    exp_map = lambda k, tile, exp, lo, hi, first: (exp[k], 0, 0)
    grid_spec = pltpu.PrefetchScalarGridSpec(
        num_scalar_prefetch=5,
        grid=(n_items,),
        in_specs=[pl.BlockSpec((MOE_TILE * SUBLANES, LANES), tile_map),
                  pl.BlockSpec((1, d, EXPERT_FF), exp_map),
                  pl.BlockSpec((1, d, EXPERT_FF), exp_map),
                  pl.BlockSpec((1, EXPERT_FF, d), exp_map)],
        out_specs=pl.BlockSpec((MOE_TILE * SUBLANES, LANES), tile_map))
    return pl.pallas_call(
        _expert_kernel,
        out_shape=jax.ShapeDtypeStruct(xs.shape, F32),
        grid_spec=grid_spec,
        compiler_params=_params("arbitrary"),
        name="experts",
    )(*(items[j, :n_items] for j in (I_TILE, I_EXPERT, I_LO, I_HI, I_FIRST)), xs, w_gate, w_up, w_down)


COMBINE_T = 256
COMBINE_CHUNK = 32


def _combine_kernel(pos_ref, posn_ref, ys_hbm, r_ref, x1_ref, gf_ref, gpost_ref, o_ref, buf, sem):
    i = pl.program_id(0)
    n = pl.num_programs(0)
    slot = i % 2
    slot_rows = COMBINE_T * SUBLANES

    def start_row(p_ref, sl, r):
        for k in range(2):
            dst = buf.at[sl, pl.ds(pl.multiple_of(k * slot_rows + r * SUBLANES, SUBLANES), SUBLANES), :]
            pltpu.make_async_copy(_token_rows(ys_hbm, p_ref[k, r]), dst, sem.at[sl]).start(priority=k)

    def wait_slot(sl):
        pltpu.make_async_copy(ys_hbm.at[pl.ds(0, 2 * slot_rows), :], buf.at[sl], sem.at[sl]).wait()

    @pl.when(i == 0)
    def _():
        def body(r, carry):
            start_row(pos_ref, 0, r)
            return carry

        lax.fori_loop(0, COMBINE_T, body, 0, unroll=8)

    wait_slot(slot)
    cur = buf.at[slot]

    def body(it, carry):
        base = pl.multiple_of(it * COMBINE_CHUNK, COMBINE_CHUNK)
        rows = pl.ds(base, COMBINE_CHUNK)
        ya = _from_token_tiles(cur, COMBINE_CHUNK, base=base * SUBLANES)
        yb = _from_token_tiles(cur, COMBINE_CHUNK, base=slot_rows + base * SUBLANES)
        r = r_ref[rows, :]
        x1 = x1_ref[0, rows, :]
        for rr in range(COMBINE_CHUNK):
            start_row(posn_ref, 1 - slot, base + rr)
        y = r[:, R_W1:R_W1 + 1] * ya + r[:, R_W2:R_W2 + 1] * yb
        o_ref[0, rows, :] = x1 + gf_ref[0] * (_rms(y) * gpost_ref[...])
        return carry

    lax.fori_loop(0, COMBINE_T // COMBINE_CHUNK, body, 0)

    @pl.when(i == n - 1)
    def _():
        wait_slot(1 - slot)


def _combine(pos, ys, r, x1, gate_f, g_post_ffn):
    b, s, d = x1.shape
    nt = s // COMBINE_T
    n_steps = b * nt
    row = lambda i: (i // nt, i % nt, 0)
    pos_spec = lambda f: pl.BlockSpec((SUBLANES, COMBINE_T), lambda i: (0, f(i)), memory_space=pltpu.SMEM)
    return pl.pallas_call(
        _combine_kernel,
        out_shape=jax.ShapeDtypeStruct((b, s, d), F32),
        grid=(n_steps,),
        in_specs=[pos_spec(lambda i: i),
                  pos_spec(lambda i: jnp.minimum(i + 1, n_steps - 1)),
                  pl.BlockSpec(memory_space=pl.ANY),
                  pl.BlockSpec((COMBINE_T, LANES), lambda i: (i, 0)),
                  pl.BlockSpec((1, COMBINE_T, d), row),
                  pl.BlockSpec((1, 1, d), lambda i: (i // nt, 0, 0)),
                  pl.BlockSpec((1, d), lambda i: (0, 0))],
        out_specs=pl.BlockSpec((1, COMBINE_T, d), row),
        scratch_shapes=[pltpu.VMEM((2, 2 * COMBINE_T * SUBLANES, LANES), F32), pltpu.SemaphoreType.DMA((2,))],
        compiler_params=_params("arbitrary"),
        name="combine",
    )(pos, pos, ys, r, x1, gate_f, g_post_ffn.reshape(1, d))


def _rope_tables(s):
    half = HEAD_DIM // 2
    inv = ROPE_THETA ** (-jnp.arange(half, dtype=F32) * 2.0 / HEAD_DIM)
    ang = jnp.arange(s).astype(F32)[:, None] * inv[None, :]
    cos = jnp.cos(ang)
    sin = jnp.sin(ang)
    cos = jnp.concatenate([cos, cos, cos, cos], axis=-1)
    sin_signed = jnp.concatenate([-sin, sin, -sin, sin], axis=-1)
    return cos, sin_signed


def _router_weights(w_group_router, w_expert_router):
    d = w_group_router.shape[0]
    we = jnp.transpose(w_expert_router, (1, 0, 2)).reshape(d, N_EXPERTS)
    hi, lo = _split2(jnp.concatenate([w_group_router, we], axis=-1))
    return jnp.concatenate([hi, lo, jnp.zeros((d, LANES - 2 * ROUTER_COLS), BF16)], axis=-1)


def kernel(x, c, w_ada, b_ada, g_pre_mix, w_in, na_rpb, swa_sinks, beta_na, beta_swa, w_out, g_post_mix, g_pre_ffn,
           w_group_router, w_expert_router, w_gate, w_up, w_down, g_post_ffn):
    b, s, d = x.shape
    depth = w_ada.shape[0]
    cos, sin_signed = _rope_tables(s)
    for l in range(depth):
        mod = _adaln(c, w_ada[l], b_ada[l]).reshape(b, N_MOD, 1, d)
        shift_a, scale_a, gate_a, shift_f, scale_f, gate_f = (mod[:, k] for k in range(N_MOD))
        qkv = _qkv(x, g_pre_mix[l], scale_a, shift_a, w_in[l].astype(BF16), cos, sin_signed)
        na = _na(qkv, na_rpb[l])
        sw = _swa(qkv, swa_sinks[l])
        x1, h2, r, rt, cntc, cntr = _mix(na, sw, x, beta_na[l], beta_swa[l], w_out[l].astype(BF16), g_post_mix[l],
                                          gate_a, g_pre_ffn[l], scale_f, shift_f,
                                          _router_weights(w_group_router[l], w_expert_router[l]))
        pos, items = _plan(rt, cntc, cntr)
        xs = _dispatch(pos, h2)
        ys = _experts(items, xs, w_gate[l], w_up[l], w_down[l])
        x = _combine(pos, ys, r, x1, gate_f, g_post_ffn[l])
    return x
```

```python
import functools

import jax
import jax.numpy as jnp
from jax import lax
from jax.experimental import pallas as pl
from jax.experimental.pallas import tpu as pltpu

D_MODEL = 1024
GRID_W = 64
HEAD_DIM = 64
NA_HEADS = 8
NA_KH = 8
NA_KW = 16
SWA_HEADS = 8
SWA_KV_HEADS = 2
SWA_WINDOW = 128
SWA_BLOCK = 128
ROPE_THETA = 10000.0
NA_WIDTH = NA_HEADS * HEAD_DIM
SWA_WIDTH = SWA_HEADS * HEAD_DIM
N_GROUPS = 4
EXPERTS_PER_GROUP = 8
N_EXPERTS = N_GROUPS * EXPERTS_PER_GROUP
EXPERT_FF = 256
N_MOD = 6
EPS = 1e-6
NEG_INF = -1e30

LANES = 128
SUBLANES = 8
R_E1, R_E2, R_W1, R_W2 = range(4)
PAIRS = NA_HEADS // 2
COL_NQ, COL_NK, COL_NV, COL_SQ, COL_SK, COL_SV = 0, 4, 8, 12, 16, 18
QKV_TILES = 20
ROUTER_BASE = N_GROUPS
VMEM_LIMIT = 56 * 1024 * 1024

F32 = jnp.float32
BF16 = jnp.bfloat16
LOG2E = 1.4426950408889634


def _rms(v):
    return v * lax.rsqrt(jnp.mean(v * v, axis=-1, keepdims=True) + EPS)


def _params(*sem):
    return pltpu.CompilerParams(dimension_semantics=sem, vmem_limit_bytes=VMEM_LIMIT)


def _adaln_kernel(c_ref, w_ref, b_ref, o_ref):
    c = c_ref[...]
    a = c * jax.nn.sigmoid(c)
    o_ref[...] = jnp.dot(a, w_ref[...], precision=lax.Precision.HIGHEST,
                         preferred_element_type=F32) + b_ref[...]


def _adaln(c, w_ada, b_ada):
    batch, d = c.shape
    n = w_ada.shape[1]
    tn = 1024
    b = 8
    c = jnp.pad(c, ((0, b - batch), (0, 0)))
    return pl.pallas_call(
        _adaln_kernel,
        out_shape=jax.ShapeDtypeStruct((b, n), F32),
        grid=(n // tn,),
        in_specs=[pl.BlockSpec((b, d), lambda j: (0, 0)),
                  pl.BlockSpec((d, tn), lambda j: (0, j)),
                  pl.BlockSpec((1, tn), lambda j: (0, j))],
        out_specs=pl.BlockSpec((b, tn), lambda j: (0, j)),
        compiler_params=_params("arbitrary"),
        name="adaln",
    )(c, w_ada, b_ada.reshape(1, n))[:batch]


def _rope(v, cos, sin_signed, first_half):
    rot = jnp.where(first_half, pltpu.roll(v, LANES - HEAD_DIM // 2, 1), pltpu.roll(v, HEAD_DIM // 2, 1))
    return v * cos + rot * sin_signed


QKV_CHUNK = 256


def _qkv_kernel(x_ref, g_ref, sc_ref, sh_ref, w_ref, cos_ref, sin_ref, o_ref):
    scale = HEAD_DIM ** -0.5 * LOG2E
    lane = lax.broadcasted_iota(jnp.int32, (QKV_CHUNK, LANES), 1)
    first_half = (lane % HEAD_DIM) < HEAD_DIM // 2
    upper = lane >= HEAD_DIM

    def tile(v, j):
        return v[:, j * LANES:(j + 1) * LANES]

    for c in range(x_ref.shape[1] // QKV_CHUNK):
        rows = slice(c * QKV_CHUNK, (c + 1) * QKV_CHUNK)
        h = (_rms(x_ref[0, rows, :]) * g_ref[...]) * (1.0 + sc_ref[0]) + sh_ref[0]
        h = h.astype(BF16)
        cos = cos_ref[rows, :]
        sin = sin_ref[rows, :]

        def proj(col, width):
            return jnp.dot(h, w_ref[:, col * LANES:(col + width) * LANES], preferred_element_type=F32)

        nq, nk, nv, sq = proj(COL_NQ, 4), proj(COL_NK, 4), proj(COL_NV, 4), proj(COL_SQ, 4)
        for j in range(PAIRS):
            o_ref[0, COL_NQ + j, rows, :] = (tile(nq, j) * scale).astype(BF16)
            o_ref[0, COL_NK + j, rows, :] = tile(nk, j).astype(BF16)
            o_ref[0, COL_NV + j, rows, :] = tile(nv, j).astype(BF16)
            o_ref[0, COL_SQ + j, rows, :] = (_rope(tile(sq, j), cos, sin, first_half) * scale).astype(BF16)
        skv = proj(COL_SK, 2)
        k = _rope(tile(skv, 0), cos, sin, first_half)
        v = tile(skv, 1)
        for t, col in ((k, COL_SK), (v, COL_SV)):
            swapped = pltpu.roll(t, HEAD_DIM, 1)
            o_ref[0, col, rows, :] = jnp.where(upper, swapped, t).astype(BF16)
            o_ref[0, col + 1, rows, :] = jnp.where(upper, t, swapped).astype(BF16)


def _qkv(x, g, scale_a, shift_a, w_in, cos, sin):
    b, s, d = x.shape
    tm = 1024
    n_in = w_in.shape[1]
    return pl.pallas_call(
        _qkv_kernel,
        out_shape=jax.ShapeDtypeStruct((b, QKV_TILES, s, LANES), BF16),
        grid=(b, s // tm),
        in_specs=[pl.BlockSpec((1, tm, d), lambda bi, i: (bi, i, 0)),
                  pl.BlockSpec((1, d), lambda bi, i: (0, 0)),
                  pl.BlockSpec((1, 1, d), lambda bi, i: (bi, 0, 0)),
                  pl.BlockSpec((1, 1, d), lambda bi, i: (bi, 0, 0)),
                  pl.BlockSpec((d, n_in), lambda bi, i: (0, 0)),
                  pl.BlockSpec((tm, LANES), lambda bi, i: (i, 0)),
                  pl.BlockSpec((tm, LANES), lambda bi, i: (i, 0))],
        out_specs=pl.BlockSpec((1, QKV_TILES, tm, LANES), lambda bi, i: (bi, 0, i, 0)),
        compiler_params=_params("arbitrary", "arbitrary"),
        name="qkv",
    )(x, g.reshape(1, d), scale_a, shift_a, w_in, cos, sin)


NA_QROWS = 4
NA_KROWS = NA_QROWS + NA_KH
NA_BLOCKS_PER_STEP = 8
NA_Q = NA_QROWS * GRID_W
NA_K = NA_KROWS * GRID_W
NA_RPB_ROWS = 2 * NA_KH - 1
NA_RPB_COLS = 2 * NA_KW - 1


def _clamp(v, lo, hi):
    return min(max(v, lo), hi)


def _na_first_key_row(block, rows, clip):
    return clip(block * NA_QROWS - NA_KH // 2, 0, rows - NA_KROWS)


def _na_block_types(rows):
    def geometry(block):
        r = block * NA_QROWS
        a = _na_first_key_row(block, rows, _clamp)
        return (a - r,) + tuple(_clamp(r + j - NA_KH // 2, 0, rows - NA_KH) - r for j in range(NA_QROWS))

    n_blocks = rows // NA_QROWS
    interior = geometry(n_blocks // 2)
    lead = next(b for b in range(n_blocks) if geometry(b) == interior)
    trail = next(b for b in range(n_blocks) if geometry(n_blocks - 1 - b) == interior)
    assert all(geometry(b) == interior for b in range(lead, n_blocks - trail))
    return lead, trail


def _na_kernel(q_ref, k_ref, v_ref, bias_ref, o_ref, *, rows):
    lane = lax.broadcasted_iota(jnp.int32, (NA_Q, LANES), 1)
    upper = lane >= HEAD_DIM
    lead, trail = _na_block_types(rows)
    first_trailing = rows // NA_QROWS - trail
    for c in range(NA_BLOCKS_PER_STEP):
        block = pl.program_id(2) * NA_BLOCKS_PER_STEP + c
        a = _na_first_key_row(block, rows, jnp.clip)
        ty = jnp.where(block < lead, block, jnp.where(block >= first_trailing, block - first_trailing + lead + 1, lead))
        start = pl.multiple_of(a * GRID_W, GRID_W)
        q = q_ref[0, 0, c * NA_Q:(c + 1) * NA_Q, :]
        ks = k_ref[0, 0, pl.ds(start, NA_K), :]
        vs = v_ref[0, 0, pl.ds(start, NA_K), :]
        outs = []
        for hh in range(2):
            qm = jnp.where(upper if hh else ~upper, q, jnp.zeros_like(q))
            s = lax.dot_general(qm, ks, (((1,), (1,)), ((), ())), preferred_element_type=F32)
            s = s + bias_ref[hh, ty]
            m = jnp.max(s, axis=-1, keepdims=True)
            e = jnp.exp2(s - m)
            l = jnp.sum(e, axis=-1, keepdims=True)
            o = jnp.dot(e.astype(BF16), vs, preferred_element_type=F32)
            outs.append(o / l)
        o_ref[0, 0, c * NA_Q:(c + 1) * NA_Q, :] = jnp.where(upper, outs[1], outs[0]).astype(BF16)


def _na_bias_kernel(rpb_ref, o_ref, *, rows):
    h = pl.program_id(0)
    q = lax.broadcasted_iota(jnp.int32, (GRID_W, LANES), 0)
    lane = lax.broadcasted_iota(jnp.int32, (GRID_W, LANES), 1)
    kc = lane % GRID_W
    upper = lane >= GRID_W
    c0 = jnp.clip(q - NA_KW // 2, 0, GRID_W - NA_KW)
    in_cols = (kc >= c0) & (kc < c0 + NA_KW)
    dc = kc - q + NA_KW - 1
    neg = jnp.full((GRID_W, LANES), NEG_INF, F32)
    base = h * NA_RPB_ROWS * NA_RPB_COLS
    by_row_offset = []
    for d in range(NA_RPB_ROWS):
        acc = neg
        for dd in range(NA_RPB_COLS):
            acc = jnp.where(dc == dd, rpb_ref[base + d * NA_RPB_COLS + dd], acc)
        by_row_offset.append(jnp.where(in_cols, acc * LOG2E, NEG_INF))
    lead, trail = _na_block_types(rows)
    n_blocks = rows // NA_QROWS
    type_blocks = list(range(lead + 1)) + list(range(n_blocks - trail, n_blocks))
    for ty, block in enumerate(type_blocks):
        r = block * NA_QROWS
        a = _na_first_key_row(block, rows, _clamp)
        for j in range(NA_QROWS):
            r0 = _clamp(r + j - NA_KH // 2, 0, rows - NA_KH)
            halves = []
            for i in range(NA_KROWS):
                in_rows = r0 <= a + i < r0 + NA_KH
                halves.append(by_row_offset[a + i - (r + j) + NA_KH - 1] if in_rows else neg)
            for t in range(NA_KROWS // 2):
                tile = jnp.where(upper, halves[2 * t + 1], halves[2 * t])
                o_ref[0, ty, j * GRID_W:(j + 1) * GRID_W, t * LANES:(t + 1) * LANES] = tile


def _na_bias(rpb, rows):
    n_types = sum(_na_block_types(rows)) + 1
    return pl.pallas_call(
        functools.partial(_na_bias_kernel, rows=rows),
        out_shape=jax.ShapeDtypeStruct((NA_HEADS, n_types, NA_Q, NA_K), F32),
        grid=(NA_HEADS,),
        in_specs=[pl.BlockSpec(memory_space=pltpu.SMEM)],
        out_specs=pl.BlockSpec((1, n_types, NA_Q, NA_K), lambda h: (h, 0, 0, 0)),
        compiler_params=_params("arbitrary"),
        name="na_bias",
    )(rpb.astype(F32).reshape(-1))


def _na(qkv, rpb):
    b, _, s, _ = qkv.shape
    rows = s // GRID_W
    assert rows % (NA_QROWS * NA_BLOCKS_PER_STEP) == 0
    tq = NA_BLOCKS_PER_STEP * NA_Q
    bias = _na_bias(rpb, rows)
    return pl.pallas_call(
        functools.partial(_na_kernel, rows=rows),
        out_shape=jax.ShapeDtypeStruct((b, PAIRS, s, LANES), BF16),
        grid=(b, PAIRS, s // tq),
        in_specs=[pl.BlockSpec((1, 1, tq, LANES), lambda bi, p, i: (bi, COL_NQ + p, i, 0)),
                  pl.BlockSpec((1, 1, s, LANES), lambda bi, p, i: (bi, COL_NK + p, 0, 0)),
                  pl.BlockSpec((1, 1, s, LANES), lambda bi, p, i: (bi, COL_NV + p, 0, 0)),
                  pl.BlockSpec((2,) + bias.shape[1:], lambda bi, p, i: (p, 0, 0, 0))],
        out_specs=pl.BlockSpec((1, 1, tq, LANES), lambda bi, p, i: (bi, p, i, 0)),
        compiler_params=_params("arbitrary", "arbitrary", "arbitrary"),
        name="na",
    )(qkv, qkv, qkv, bias)


SWA_KEYS = 3 * SWA_BLOCK
SWA_BLOCKS_PER_STEP = 8
SWA_GROUP = SWA_HEADS // SWA_KV_HEADS
assert COL_SQ % (PAIRS // SWA_KV_HEADS) == 0


def _swa_kernel(sink_ref, q_ref, k_ref, v_ref, o_ref, *, seq):
    kv = pl.program_id(1)
    rows = SWA_GROUP * SWA_BLOCK
    lane = lax.broadcasted_iota(jnp.int32, (SWA_BLOCK, LANES), 1)
    upper = lane >= HEAD_DIM
    row = lax.broadcasted_iota(jnp.int32, (rows, SWA_KEYS), 0)
    q_off = row % SWA_BLOCK
    k_off = lax.broadcasted_iota(jnp.int32, (rows, SWA_KEYS), 1)
    head = lax.broadcasted_iota(jnp.int32, (rows, 1), 0) // SWA_BLOCK
    sink = jnp.zeros((rows, 1), F32)
    for g in range(SWA_GROUP):
        sink = jnp.where(head == g, sink_ref[kv * SWA_GROUP + g], sink)
    sink = sink * LOG2E
    for j in range(SWA_BLOCKS_PER_STEP):
        n = pl.program_id(2) * SWA_BLOCKS_PER_STEP + j
        start = pl.multiple_of(jnp.clip((n - 1) * SWA_BLOCK, 0, seq - SWA_KEYS), SWA_BLOCK)
        ks = k_ref[0, 0, pl.ds(start, SWA_KEYS), :]
        vs = v_ref[0, 0, pl.ds(start, SWA_KEYS), :]
        qs = []
        for g in range(SWA_GROUP):
            q = q_ref[0, g // 2, j * SWA_BLOCK:(j + 1) * SWA_BLOCK, :]
            qs.append(jnp.where(upper if g % 2 else ~upper, q, jnp.zeros_like(q)))
        s = lax.dot_general(jnp.concatenate(qs, axis=0), ks, (((1,), (1,)), ((), ())), preferred_element_type=F32)
        in_window = jnp.abs((start + k_off) - (n * SWA_BLOCK + q_off)) <= SWA_WINDOW
        s = jnp.where(in_window, s, NEG_INF)
        m = jnp.maximum(jnp.max(s, axis=-1, keepdims=True), sink)
        e = jnp.exp2(s - m)
        l = jnp.sum(e, axis=-1, keepdims=True) + jnp.exp2(sink - m)
        o = jnp.dot(e.astype(BF16), vs, preferred_element_type=F32) / l
        for pair in range(SWA_GROUP // 2):
            even = o[(2 * pair) * SWA_BLOCK:(2 * pair + 1) * SWA_BLOCK]
            odd = o[(2 * pair + 1) * SWA_BLOCK:(2 * pair + 2) * SWA_BLOCK]
            o_ref[0, pair, j * SWA_BLOCK:(j + 1) * SWA_BLOCK, :] = jnp.where(upper, odd, even).astype(BF16)


def _swa(qkv, sinks):
    b, _, s, _ = qkv.shape
    pairs_per_kv = PAIRS // SWA_KV_HEADS
    tq = SWA_BLOCKS_PER_STEP * SWA_BLOCK
    return pl.pallas_call(
        functools.partial(_swa_kernel, seq=s),
        out_shape=jax.ShapeDtypeStruct((b, PAIRS, s, LANES), BF16),
        grid=(b, SWA_KV_HEADS, s // tq),
        in_specs=[pl.BlockSpec(memory_space=pltpu.SMEM),
                  pl.BlockSpec((1, pairs_per_kv, tq, LANES),
                               lambda bi, kv, n: (bi, COL_SQ // pairs_per_kv + kv, n, 0)),
                  pl.BlockSpec((1, 1, s, LANES), lambda bi, kv, n: (bi, COL_SK + kv, 0, 0)),
                  pl.BlockSpec((1, 1, s, LANES), lambda bi, kv, n: (bi, COL_SV + kv, 0, 0))],
        out_specs=pl.BlockSpec((1, pairs_per_kv, tq, LANES), lambda bi, kv, n: (bi, kv, n, 0)),
        compiler_params=_params("arbitrary", "arbitrary", "arbitrary"),
        name="swa",
    )(sinks, qkv, qkv, qkv)


ROUTER_COLS = N_GROUPS + N_EXPERTS


def _split2(v):
    hi = v.astype(BF16)
    lo = (v - hi.astype(F32)).astype(BF16)
    return hi, lo


def _route(logits):
    lane = lax.broadcasted_iota(jnp.int32, logits.shape, 1)
    big = jnp.int32(LANES)
    gmask = lane < N_GROUPS
    gl = jnp.where(gmask, logits, NEG_INF)
    gmax = jnp.max(gl, axis=-1, keepdims=True)
    g_top = jnp.min(jnp.where(gmask & (gl == gmax), lane, big), axis=-1, keepdims=True)
    g_weight = 1.0 / jnp.sum(jnp.where(gmask, jnp.exp(gl - gmax), 0.0), axis=-1, keepdims=True)
    lo = ROUTER_BASE + g_top * EXPERTS_PER_GROUP
    emask = (lane >= lo) & (lane < lo + EXPERTS_PER_GROUP)
    el = jnp.where(emask, logits, NEG_INF)
    m1 = jnp.max(el, axis=-1, keepdims=True)
    i1 = jnp.min(jnp.where(emask & (el == m1), lane, big), axis=-1, keepdims=True)
    emask2 = emask & (lane != i1)
    el2 = jnp.where(emask2, logits, NEG_INF)
    m2 = jnp.max(el2, axis=-1, keepdims=True)
    i2 = jnp.min(jnp.where(emask2 & (el2 == m2), lane, big), axis=-1, keepdims=True)
    e2 = jnp.exp(m2 - m1)
    w1 = g_weight / (1.0 + e2)
    w2 = g_weight * e2 / (1.0 + e2)
    return i1 - ROUTER_BASE, i2 - ROUTER_BASE, w1, w2


def _to_token_tiles(ref, v, base=0):
    t = v.shape[0]
    for s in range(SUBLANES):
        ref[pl.ds(base + s, t, stride=SUBLANES), :] = v[:, s * LANES:(s + 1) * LANES]


def _from_token_tiles(ref, t, base=0):
    return jnp.concatenate([ref[pl.ds(base + s, t, stride=SUBLANES), :] for s in range(SUBLANES)], axis=-1)


MIX_CHUNK = 256


def _mix_kernel(na_ref, sw_ref, x_ref, bna_ref, bsw_ref, wo_ref, gpm_ref, ga_ref, gpf_ref, scf_ref, shf_ref,
                wr_ref, x1_ref, h2_ref, r_ref, rt_ref, cntc_ref, cntr_ref):
    first_step = (pl.program_id(0) == 0) & (pl.program_id(1) == 0)

    @pl.when(first_step)
    def _():
        cntc_ref[...] = jnp.zeros_like(cntc_ref)
        cntr_ref[...] = jnp.zeros_like(cntr_ref)

    t = MIX_CHUNK
    for c in range(x_ref.shape[1] // t):
        rows = slice(c * t, (c + 1) * t)

        def heads(ref):
            return jnp.concatenate([ref[0, j, rows, :] for j in range(PAIRS)], axis=-1).astype(F32)

        na = (_rms(heads(na_ref)) * bna_ref[...]).astype(BF16)
        sw = (_rms(heads(sw_ref)) * bsw_ref[...]).astype(BF16)
        mix = (jnp.dot(na, wo_ref[:NA_WIDTH, :], preferred_element_type=F32)
               + jnp.dot(sw, wo_ref[NA_WIDTH:, :], preferred_element_type=F32))
        x1 = x_ref[0, rows, :] + ga_ref[0] * (_rms(mix) * gpm_ref[...])
        x1_ref[0, rows, :] = x1
        h2 = (_rms(x1) * gpf_ref[...]) * (1.0 + scf_ref[0]) + shf_ref[0]
        _to_token_tiles(h2_ref, h2, base=c * t * SUBLANES)
        h_hi, h_lo = _split2(h2)
        both = (jnp.dot(h_hi, wr_ref[...], preferred_element_type=F32)
                + jnp.dot(h_lo, wr_ref[...], preferred_element_type=F32))
        logits = both + pltpu.roll(both, LANES - ROUTER_COLS, 1)
        e1, e2, w1, w2 = _route(logits)
        lane = lax.broadcasted_iota(jnp.int32, logits.shape, 1)
        r = jnp.where(lane == R_E1, e1.astype(F32),
                      jnp.where(lane == R_E2, e2.astype(F32),
                                jnp.where(lane == R_W1, w1, jnp.where(lane == R_W2, w2, 0.0))))
        r_ref[rows, :] = r
        rt = jnp.transpose(r)[:SUBLANES]
        rt_ref[:, rows] = rt
        on_lane = ((lane == e1) | (lane == e2)).astype(F32)
        cntr_ref[...] += jnp.broadcast_to(jnp.sum(on_lane, axis=0, keepdims=True), cntr_ref.shape)
        sub = lax.broadcasted_iota(jnp.int32, (N_EXPERTS, t), 0).astype(F32)
        on_sub = ((sub == rt[R_E1:R_E1 + 1]) | (sub == rt[R_E2:R_E2 + 1])).astype(F32)
        cntc_ref[...] += jnp.broadcast_to(jnp.sum(on_sub, axis=1, keepdims=True), cntc_ref.shape)


def _mix(na, sw, x, beta_na, beta_swa, w_out, g_post_mix, gate_a, g_pre_ffn, scale_f, shift_f, w_router3):
    b, s, d = x.shape
    tm = 1024
    nt = s // tm
    row = lambda bi, i: (bi, i, 0)
    const2 = lambda bi, i: (0, 0)
    per_b = lambda bi, i: (bi, 0, 0)
    return pl.pallas_call(
        _mix_kernel,
        out_shape=(jax.ShapeDtypeStruct((b, s, d), F32),
                   jax.ShapeDtypeStruct((b * s * SUBLANES, LANES), F32),
                   jax.ShapeDtypeStruct((b * s, LANES), F32),
                   jax.ShapeDtypeStruct((SUBLANES, b * s), F32),
                   jax.ShapeDtypeStruct((N_EXPERTS, LANES), F32),
                   jax.ShapeDtypeStruct((SUBLANES, LANES), F32)),
        grid=(b, s // tm),
        in_specs=[pl.BlockSpec((1, PAIRS, tm, LANES), lambda bi, i: (bi, 0, i, 0)),
                  pl.BlockSpec((1, PAIRS, tm, LANES), lambda bi, i: (bi, 0, i, 0)),
                  pl.BlockSpec((1, tm, d), row),
                  pl.BlockSpec((1, NA_WIDTH), const2),
                  pl.BlockSpec((1, SWA_WIDTH), const2),
                  pl.BlockSpec((NA_WIDTH + SWA_WIDTH, d), const2),
                  pl.BlockSpec((1, d), const2),
                  pl.BlockSpec((1, 1, d), per_b),
                  pl.BlockSpec((1, d), const2),
                  pl.BlockSpec((1, 1, d), per_b),
                  pl.BlockSpec((1, 1, d), per_b),
                  pl.BlockSpec((d, LANES), const2)],
        out_specs=(pl.BlockSpec((1, tm, d), row),
                   pl.BlockSpec((tm * SUBLANES, LANES), lambda bi, i: (bi * nt + i, 0)),
                   pl.BlockSpec((tm, LANES), lambda bi, i: (bi * nt + i, 0)),
                   pl.BlockSpec((SUBLANES, tm), lambda bi, i: (0, bi * nt + i)),
                   pl.BlockSpec((N_EXPERTS, LANES), const2),
                   pl.BlockSpec((SUBLANES, LANES), const2)),
        compiler_params=_params("arbitrary", "arbitrary"),
        name="mix",
    )(na, sw, x, beta_na.reshape(1, -1), beta_swa.reshape(1, -1), w_out, g_post_mix.reshape(1, d), gate_a,
      g_pre_ffn.reshape(1, d), scale_f, shift_f, w_router3)


MOE_TILE = 256
PLAN_T = 512
I_TILE, I_EXPERT, I_LO, I_HI, I_FIRST = range(5)


def _plan_kernel(rt_ref, cntc_ref, cntr_ref, pos_ref, items_ref, start_ref, carry_ref, *, n_tiles, n_items):
    i = pl.program_id(0)
    sub = lax.broadcasted_iota(jnp.int32, (N_EXPERTS, LANES), 0)
    lane = lax.broadcasted_iota(jnp.int32, (N_EXPERTS, LANES), 1)

    @pl.when(i == 0)
    def _():
        c_col = cntc_ref[:, 0:1]
        c_row = cntr_ref[0:1, :]
        s_col = jnp.sum(jnp.where(lane < sub, c_row, 0.0), axis=1, keepdims=True)
        s_row = jnp.sum(jnp.where(sub < lane, c_col, 0.0), axis=0, keepdims=True)
        start_ref[...] = jnp.broadcast_to(s_col, start_ref.shape)
        carry_ref[...] = jnp.zeros_like(carry_ref)

        def tiles_of(s, c):
            first = jnp.floor(s * (1.0 / MOE_TILE))
            last = jnp.floor((s + c - 1.0) * (1.0 / MOE_TILE))
            return first, jnp.where(c > 0.0, last - first + 1.0, 0.0)

        f_col, n_col = tiles_of(s_col, c_col)
        _, n_row = tiles_of(s_row, c_row)
        i_col = jnp.sum(jnp.where(lane < sub, n_row, 0.0), axis=1, keepdims=True)
        total = jnp.sum(n_col, axis=0, keepdims=True)
        k = lax.broadcasted_iota(jnp.int32, (N_EXPERTS, n_items), 1).astype(F32)
        subk = lax.broadcasted_iota(jnp.int32, (N_EXPERTS, n_items), 0).astype(F32)
        ek = jnp.sum(jnp.where(i_col + n_col <= k, 1.0, 0.0), axis=0, keepdims=True)
        k0 = k[0:1]
        valid = k0 < total
        sel = subk == ek

        def pick(v):
            return jnp.sum(jnp.where(sel, v, 0.0), axis=0, keepdims=True)

        i_k, f_k, s_k, c_k = pick(i_col), pick(f_col), pick(s_col), pick(c_col)
        tile = f_k + (k0 - i_k)
        row0 = tile * MOE_TILE
        lo = jnp.maximum(s_k, row0) - row0
        hi = jnp.minimum(s_k + c_k, row0 + MOE_TILE) - row0
        rows = [jnp.where(valid, tile, n_tiles - 1.0), jnp.where(valid, ek, N_EXPERTS - 1.0),
                jnp.where(valid, lo, 0.0), jnp.where(valid, hi, 0.0),
                jnp.where(valid & (lo == 0.0), 1.0, 0.0)]
        rows += [jnp.zeros_like(k0)] * (SUBLANES - len(rows))
        items_ref[...] = jnp.concatenate(rows, axis=0).astype(jnp.int32)

    t = rt_ref.shape[1]
    e1 = rt_ref[R_E1:R_E1 + 1, :]
    e2 = rt_ref[R_E2:R_E2 + 1, :]
    sub_t = lax.broadcasted_iota(jnp.int32, (N_EXPERTS, t), 0).astype(F32)
    oh1 = sub_t == e1
    oh2 = sub_t == e2
    oh = (oh1 | oh2).astype(F32)
    before = (lax.broadcasted_iota(jnp.int32, (t, t), 0) < lax.broadcasted_iota(jnp.int32, (t, t), 1)).astype(BF16)
    rank = jnp.dot(oh.astype(BF16), before, preferred_element_type=F32)
    base = start_ref[:, 0:1] + carry_ref[:, 0:1] + rank
    pos1 = jnp.sum(jnp.where(oh1, base, 0.0), axis=0, keepdims=True)
    pos2 = jnp.sum(jnp.where(oh2, base, 0.0), axis=0, keepdims=True)
    carry_ref[...] += jnp.broadcast_to(jnp.sum(oh, axis=1, keepdims=True), carry_ref.shape)
    pos = jnp.concatenate([pos1, pos2] + [jnp.zeros_like(pos1)] * (SUBLANES - 2), axis=0)
    pos_ref[...] = pos.astype(jnp.int32)


def _plan(rt, cntc, cntr):
    n = rt.shape[1]
    n_tiles = 2 * n // MOE_TILE
    n_items = 2 * LANES
    assert n_tiles + N_EXPERTS <= n_items
    return pl.pallas_call(
        functools.partial(_plan_kernel, n_tiles=n_tiles, n_items=n_items),
        out_shape=(jax.ShapeDtypeStruct((SUBLANES, n), jnp.int32),
                   jax.ShapeDtypeStruct((SUBLANES, n_items), jnp.int32)),
        grid=(n // PLAN_T,),
        in_specs=[pl.BlockSpec((SUBLANES, PLAN_T), lambda i: (0, i)),
                  pl.BlockSpec((N_EXPERTS, LANES), lambda i: (0, 0)),
                  pl.BlockSpec((SUBLANES, LANES), lambda i: (0, 0))],
        out_specs=(pl.BlockSpec((SUBLANES, PLAN_T), lambda i: (0, i)),
                   pl.BlockSpec((SUBLANES, n_items), lambda i: (0, 0))),
        scratch_shapes=[pltpu.VMEM((N_EXPERTS, LANES), F32), pltpu.VMEM((N_EXPERTS, LANES), F32)],
        compiler_params=_params("arbitrary"),
        name="plan",
    )(rt, cntc, cntr)


DISPATCH_T = 1024


def _token_rows(ref, index):
    return ref.at[pl.ds(pl.multiple_of(index * SUBLANES, SUBLANES), SUBLANES), :]


def _dispatch_kernel(pos_ref, h_ref, xs_hbm, sem):
    def body(r, carry):
        src = _token_rows(h_ref, r)
        for k in range(2):
            pltpu.make_async_copy(src, _token_rows(xs_hbm, pos_ref[k, r]), sem.at[0]).start(priority=k)
        return carry

    lax.fori_loop(0, DISPATCH_T, body, 0, unroll=8)
    for k in range(2):
        pltpu.make_async_copy(h_ref, xs_hbm.at[pl.ds(0, DISPATCH_T * SUBLANES), :], sem.at[0]).wait()


def _dispatch(pos, h2):
    n = pos.shape[1]
    return pl.pallas_call(
        _dispatch_kernel,
        out_shape=jax.ShapeDtypeStruct((2 * n * SUBLANES, LANES), F32),
        grid=(n // DISPATCH_T,),
        in_specs=[pl.BlockSpec((SUBLANES, DISPATCH_T), lambda i: (0, i), memory_space=pltpu.SMEM),
                  pl.BlockSpec((DISPATCH_T * SUBLANES, LANES), lambda i: (i, 0))],
        out_specs=pl.BlockSpec(memory_space=pl.ANY),
        scratch_shapes=[pltpu.SemaphoreType.DMA((1,))],
        compiler_params=_params("arbitrary"),
        name="dispatch",
    )(pos, h2)


def _expert_kernel(tile_ref, exp_ref, lo_ref, hi_ref, first_ref, xs_ref, wg_ref, wu_ref, wd_ref, o_ref):
    k = pl.program_id(0)
    lo = lo_ref[k]
    hi = hi_ref[k]

    @pl.when(first_ref[k] == 1)
    def _():
        o_ref[...] = jnp.zeros_like(o_ref)

    @pl.when(hi > lo)
    def _():
        x = _from_token_tiles(xs_ref, MOE_TILE).astype(BF16)
        gate = jnp.dot(x, wg_ref[0].astype(BF16), preferred_element_type=F32)
        up = jnp.dot(x, wu_ref[0].astype(BF16), preferred_element_type=F32)
        he = (gate * jax.nn.sigmoid(gate) * up).astype(BF16)
        ye = jnp.dot(he, wd_ref[0].astype(BF16), preferred_element_type=F32)
        row = lax.broadcasted_iota(jnp.int32, (MOE_TILE, LANES), 0)
        mine = (row >= lo) & (row < hi)
        for s in range(SUBLANES):
            rows = pl.ds(s, MOE_TILE, stride=SUBLANES)
            o_ref[rows, :] = jnp.where(mine, ye[:, s * LANES:(s + 1) * LANES], o_ref[rows, :])


def _experts(items, xs, w_gate, w_up, w_down):
    n_rows = xs.shape[0] // SUBLANES
    n_items = n_rows // MOE_TILE + N_EXPERTS
    d = w_gate.shape[1]
    tile_map = lambda k, tile, exp, lo, hi, first: (tile[k], 0)
    exp_map = lambda k, tile, exp, lo, hi, first: (exp[k], 0, 0)
    grid_spec = pltpu.PrefetchScalarGridSpec(
        num_scalar_prefetch=5,
        grid=(n_items,),
        in_specs=[pl.BlockSpec((MOE_TILE * SUBLANES, LANES), tile_map),
                  pl.BlockSpec((1, d, EXPERT_FF), exp_map),
                  pl.BlockSpec((1, d, EXPERT_FF), exp_map),
                  pl.BlockSpec((1, EXPERT_FF, d), exp_map)],
        out_specs=pl.BlockSpec((MOE_TILE * SUBLANES, LANES), tile_map))
    return pl.pallas_call(
        _expert_kernel,
        out_shape=jax.ShapeDtypeStruct(xs.shape, F32),
        grid_spec=grid_spec,
        compiler_params=_params("arbitrary"),
        name="experts",
    )(*(items[j, :n_items] for j in (I_TILE, I_EXPERT, I_LO, I_HI, I_FIRST)), xs, w_gate, w_up, w_down)


COMBINE_T = 256


def _combine_kernel(pos_ref, posn_ref, ys_hbm, r_ref, x1_ref, gf_ref, gpost_ref, o_ref, buf, sem):
    i = pl.program_id(0)
    n = pl.num_programs(0)
    slot = i % 2
    slot_rows = COMBINE_T * SUBLANES

    def start_row(p_ref, sl, r):
        for k in range(2):
            dst = buf.at[sl, pl.ds(pl.multiple_of(k * slot_rows + r * SUBLANES, SUBLANES), SUBLANES), :]
            pltpu.make_async_copy(_token_rows(ys_hbm, p_ref[k, r]), dst, sem.at[sl]).start(priority=k)

    def wait_slot(sl):
        pltpu.make_async_copy(ys_hbm.at[pl.ds(0, 2 * slot_rows), :], buf.at[sl], sem.at[sl]).wait()

    def issue(p_ref, sl):
        def body(r, carry):
            start_row(p_ref, sl, r)
            return carry

        lax.fori_loop(0, COMBINE_T, body, 0, unroll=8)

    @pl.when(i == 0)
    def _():
        issue(pos_ref, 0)

    @pl.when(i + 1 < n)
    def _():
        issue(posn_ref, 1 - slot)

    wait_slot(slot)
    ya = _from_token_tiles(buf.at[slot], COMBINE_T)
    yb = _from_token_tiles(buf.at[slot], COMBINE_T, base=slot_rows)
    r = r_ref[...]
    y = r[:, R_W1:R_W1 + 1] * ya + r[:, R_W2:R_W2 + 1] * yb
    o_ref[0] = x1_ref[0] + gf_ref[0] * (_rms(y) * gpost_ref[...])


def _combine(pos, ys, r, x1, gate_f, g_post_ffn):
    b, s, d = x1.shape
    nt = s // COMBINE_T
    n_steps = b * nt
    row = lambda i: (i // nt, i % nt, 0)
    pos_spec = lambda f: pl.BlockSpec((SUBLANES, COMBINE_T), lambda i: (0, f(i)), memory_space=pltpu.SMEM)
    return pl.pallas_call(
        _combine_kernel,
        out_shape=jax.ShapeDtypeStruct((b, s, d), F32),
        grid=(n_steps,),
        in_specs=[pos_spec(lambda i: i),
                  pos_spec(lambda i: jnp.minimum(i + 1, n_steps - 1)),
                  pl.BlockSpec(memory_space=pl.ANY),
                  pl.BlockSpec((COMBINE_T, LANES), lambda i: (i, 0)),
                  pl.BlockSpec((1, COMBINE_T, d), row),
                  pl.BlockSpec((1, 1, d), lambda i: (i // nt, 0, 0)),
                  pl.BlockSpec((1, d), lambda i: (0, 0))],
        out_specs=pl.BlockSpec((1, COMBINE_T, d), row),
        scratch_shapes=[pltpu.VMEM((2, 2 * COMBINE_T * SUBLANES, LANES), F32), pltpu.SemaphoreType.DMA((2,))],
        compiler_params=_params("arbitrary"),
        name="combine",
    )(pos, pos, ys, r, x1, gate_f, g_post_ffn.reshape(1, d))


def _rope_tables(s):
    half = HEAD_DIM // 2
    inv = ROPE_THETA ** (-jnp.arange(half, dtype=F32) * 2.0 / HEAD_DIM)
    ang = jnp.arange(s).astype(F32)[:, None] * inv[None, :]
    cos = jnp.cos(ang)
    sin = jnp.sin(ang)
    cos = jnp.concatenate([cos, cos, cos, cos], axis=-1)
    sin_signed = jnp.concatenate([-sin, sin, -sin, sin], axis=-1)
    return cos, sin_signed


def _router_weights(w_group_router, w_expert_router):
    d = w_group_router.shape[0]
    we = jnp.transpose(w_expert_router, (1, 0, 2)).reshape(d, N_EXPERTS)
    hi, lo = _split2(jnp.concatenate([w_group_router, we], axis=-1))
    return jnp.concatenate([hi, lo, jnp.zeros((d, LANES - 2 * ROUTER_COLS), BF16)], axis=-1)


def kernel(x, c, w_ada, b_ada, g_pre_mix, w_in, na_rpb, swa_sinks, beta_na, beta_swa, w_out, g_post_mix, g_pre_ffn,
           w_group_router, w_expert_router, w_gate, w_up, w_down, g_post_ffn):
    b, s, d = x.shape
    depth = w_ada.shape[0]
    cos, sin_signed = _rope_tables(s)
    for l in range(depth):
        mod = _adaln(c, w_ada[l], b_ada[l]).reshape(b, N_MOD, 1, d)
        shift_a, scale_a, gate_a, shift_f, scale_f, gate_f = (mod[:, k] for k in range(N_MOD))
        qkv = _qkv(x, g_pre_mix[l], scale_a, shift_a, w_in[l].astype(BF16), cos, sin_signed)
        na = _na(qkv, na_rpb[l])
        sw = _swa(qkv, swa_sinks[l])
        x1, h2, r, rt, cntc, cntr = _mix(na, sw, x, beta_na[l], beta_swa[l], w_out[l].astype(BF16), g_post_mix[l],
                                          gate_a, g_pre_ffn[l], scale_f, shift_f,
                                          _router_weights(w_group_router[l], w_expert_router[l]))
        pos, items = _plan(rt, cntc, cntr)
        xs = _dispatch(pos, h2)
        ys = _experts(items, xs, w_gate[l], w_up[l], w_down[l])
        x = _combine(pos, ys, r, x1, gate_f, g_post_ffn[l])
    return x
```

```python
import functools

import jax
import jax.numpy as jnp
import numpy as np
from jax import lax
from jax.experimental import pallas as pl
from jax.experimental.pallas import tpu as pltpu

D_MODEL = 1024
GRID_W = 64
HEAD_DIM = 64
NA_HEADS = 8
NA_KH = 8
NA_KW = 16
SWA_HEADS = 8
SWA_KV_HEADS = 2
SWA_WINDOW = 128
SWA_BLOCK = 128
ROPE_THETA = 10000.0
NA_WIDTH = NA_HEADS * HEAD_DIM
SWA_WIDTH = SWA_HEADS * HEAD_DIM
N_GROUPS = 4
EXPERTS_PER_GROUP = 8
N_EXPERTS = N_GROUPS * EXPERTS_PER_GROUP
EXPERT_FF = 256
N_MOD = 6
EPS = 1e-6
NEG_INF = -1e30

LANES = 128
SUBLANES = 8
R_E1, R_E2, R_W1, R_W2 = range(4)
PAIRS = NA_HEADS // 2
COL_NQ, COL_NK, COL_NV, COL_SQ, COL_SK, COL_SV = 0, 4, 8, 12, 16, 18
QKV_TILES = 20
ROUTER_BASE = N_GROUPS
VMEM_LIMIT = 56 * 1024 * 1024

F32 = jnp.float32
BF16 = jnp.bfloat16
LOG2E = 1.4426950408889634


def _rms(v):
    return v * lax.rsqrt(jnp.mean(v * v, axis=-1, keepdims=True) + EPS)


def _params(*sem):
    return pltpu.CompilerParams(dimension_semantics=sem, vmem_limit_bytes=VMEM_LIMIT)


def _adaln_kernel(c_ref, w_ref, b_ref, o_ref):
    c = c_ref[...]
    a = c * jax.nn.sigmoid(c)
    o_ref[...] = jnp.dot(a, w_ref[...], precision=lax.Precision.HIGHEST,
                         preferred_element_type=F32) + b_ref[...]


def _adaln(c, w_ada, b_ada):
    batch, d = c.shape
    n = w_ada.shape[1]
    tn = 1024
    b = 8
    c = jnp.pad(c, ((0, b - batch), (0, 0)))
    return pl.pallas_call(
        _adaln_kernel,
        out_shape=jax.ShapeDtypeStruct((b, n), F32),
        grid=(n // tn,),
        in_specs=[pl.BlockSpec((b, d), lambda j: (0, 0)),
                  pl.BlockSpec((d, tn), lambda j: (0, j)),
                  pl.BlockSpec((1, tn), lambda j: (0, j))],
        out_specs=pl.BlockSpec((b, tn), lambda j: (0, j)),
        compiler_params=_params("arbitrary"),
        name="adaln",
    )(c, w_ada, b_ada.reshape(1, n))[:batch]


def _rope(v, cos, sin_signed, first_half):
    rot = jnp.where(first_half, pltpu.roll(v, LANES - HEAD_DIM // 2, 1), pltpu.roll(v, HEAD_DIM // 2, 1))
    return v * cos + rot * sin_signed


QKV_CHUNK = 256


def _qkv_kernel(x_ref, g_ref, sc_ref, sh_ref, w_ref, cos_ref, sin_ref, o_ref):
    scale = HEAD_DIM ** -0.5 * LOG2E
    lane = lax.broadcasted_iota(jnp.int32, (QKV_CHUNK, LANES), 1)
    first_half = (lane % HEAD_DIM) < HEAD_DIM // 2
    upper = lane >= HEAD_DIM

    def tile(v, j):
        return v[:, j * LANES:(j + 1) * LANES]

    for c in range(x_ref.shape[1] // QKV_CHUNK):
        rows = slice(c * QKV_CHUNK, (c + 1) * QKV_CHUNK)
        h = (_rms(x_ref[0, rows, :]) * g_ref[...]) * (1.0 + sc_ref[0]) + sh_ref[0]
        h = h.astype(BF16)
        cos = cos_ref[rows, :]
        sin = sin_ref[rows, :]

        def proj(col, width):
            return jnp.dot(h, w_ref[:, col * LANES:(col + width) * LANES], preferred_element_type=F32)

        nq, nk, nv, sq = proj(COL_NQ, 4), proj(COL_NK, 4), proj(COL_NV, 4), proj(COL_SQ, 4)
        for j in range(PAIRS):
            o_ref[0, COL_NQ + j, rows, :] = (tile(nq, j) * scale).astype(BF16)
            o_ref[0, COL_NK + j, rows, :] = tile(nk, j).astype(BF16)
            o_ref[0, COL_NV + j, rows, :] = tile(nv, j).astype(BF16)
            o_ref[0, COL_SQ + j, rows, :] = (_rope(tile(sq, j), cos, sin, first_half) * scale).astype(BF16)
        skv = proj(COL_SK, 2)
        k = _rope(tile(skv, 0), cos, sin, first_half)
        v = tile(skv, 1)
        for t, col in ((k, COL_SK), (v, COL_SV)):
            swapped = pltpu.roll(t, HEAD_DIM, 1)
            o_ref[0, col, rows, :] = jnp.where(upper, swapped, t).astype(BF16)
            o_ref[0, col + 1, rows, :] = jnp.where(upper, t, swapped).astype(BF16)


def _qkv(x, g, scale_a, shift_a, w_in, cos, sin):
    b, s, d = x.shape
    tm = 1024
    n_in = w_in.shape[1]
    return pl.pallas_call(
        _qkv_kernel,
        out_shape=jax.ShapeDtypeStruct((b, QKV_TILES, s, LANES), BF16),
        grid=(b, s // tm),
        in_specs=[pl.BlockSpec((1, tm, d), lambda bi, i: (bi, i, 0)),
                  pl.BlockSpec((1, d), lambda bi, i: (0, 0)),
                  pl.BlockSpec((1, 1, d), lambda bi, i: (bi, 0, 0)),
                  pl.BlockSpec((1, 1, d), lambda bi, i: (bi, 0, 0)),
                  pl.BlockSpec((d, n_in), lambda bi, i: (0, 0)),
                  pl.BlockSpec((tm, LANES), lambda bi, i: (i, 0)),
                  pl.BlockSpec((tm, LANES), lambda bi, i: (i, 0))],
        out_specs=pl.BlockSpec((1, QKV_TILES, tm, LANES), lambda bi, i: (bi, 0, i, 0)),
        compiler_params=_params("arbitrary", "arbitrary"),
        name="qkv",
    )(x, g.reshape(1, d), scale_a, shift_a, w_in, cos, sin)


NA_QROWS = 4
NA_KROWS = NA_QROWS + NA_KH
NA_BLOCKS_PER_STEP = 8
NA_Q = NA_QROWS * GRID_W
NA_K = NA_KROWS * GRID_W
NA_RPB_ROWS = 2 * NA_KH - 1
NA_RPB_COLS = 2 * NA_KW - 1


def _clamp(v, lo, hi):
    return min(max(v, lo), hi)


def _na_first_key_row(block, rows, clip):
    return clip(block * NA_QROWS - NA_KH // 2, 0, rows - NA_KROWS)


def _na_block_types(rows):
    def geometry(block):
        r = block * NA_QROWS
        a = _na_first_key_row(block, rows, _clamp)
        return (a - r,) + tuple(_clamp(r + j - NA_KH // 2, 0, rows - NA_KH) - r for j in range(NA_QROWS))

    n_blocks = rows // NA_QROWS
    interior = geometry(n_blocks // 2)
    lead = next(b for b in range(n_blocks) if geometry(b) == interior)
    trail = next(b for b in range(n_blocks) if geometry(n_blocks - 1 - b) == interior)
    assert all(geometry(b) == interior for b in range(lead, n_blocks - trail))
    return lead, trail


def _na_kernel(q_ref, k_ref, v_ref, bias_ref, o_ref, *, rows):
    lane = lax.broadcasted_iota(jnp.int32, (NA_Q, LANES), 1)
    upper = lane >= HEAD_DIM
    lead, trail = _na_block_types(rows)
    first_trailing = rows // NA_QROWS - trail
    for c in range(NA_BLOCKS_PER_STEP):
        block = pl.program_id(2) * NA_BLOCKS_PER_STEP + c
        a = _na_first_key_row(block, rows, jnp.clip)
        ty = jnp.where(block < lead, block, jnp.where(block >= first_trailing, block - first_trailing + lead + 1, lead))
        start = pl.multiple_of(a * GRID_W, GRID_W)
        q = q_ref[0, 0, c * NA_Q:(c + 1) * NA_Q, :]
        ks = k_ref[0, 0, pl.ds(start, NA_K), :]
        vs = v_ref[0, 0, pl.ds(start, NA_K), :]
        outs = []
        for hh in range(2):
            qm = jnp.where(upper if hh else ~upper, q, jnp.zeros_like(q))
            s = lax.dot_general(qm, ks, (((1,), (1,)), ((), ())), preferred_element_type=F32)
            s = s + bias_ref[hh, ty]
            m = jnp.max(s, axis=-1, keepdims=True)
            e = jnp.exp2(s - m)
            l = jnp.sum(e, axis=-1, keepdims=True)
            o = jnp.dot(e.astype(BF16), vs, preferred_element_type=F32)
            outs.append(o / l)
        o_ref[0, 0, c * NA_Q:(c + 1) * NA_Q, :] = jnp.where(upper, outs[1], outs[0]).astype(BF16)


def _na_bias_kernel(rpb_ref, o_ref, *, rows):
    h = pl.program_id(0)
    q = lax.broadcasted_iota(jnp.int32, (GRID_W, LANES), 0)
    lane = lax.broadcasted_iota(jnp.int32, (GRID_W, LANES), 1)
    kc = lane % GRID_W
    upper = lane >= GRID_W
    c0 = jnp.clip(q - NA_KW // 2, 0, GRID_W - NA_KW)
    in_cols = (kc >= c0) & (kc < c0 + NA_KW)
    dc = kc - q + NA_KW - 1
    neg = jnp.full((GRID_W, LANES), NEG_INF, F32)
    base = h * NA_RPB_ROWS * NA_RPB_COLS
    by_row_offset = []
    for d in range(NA_RPB_ROWS):
        acc = neg
        for dd in range(NA_RPB_COLS):
            acc = jnp.where(dc == dd, rpb_ref[base + d * NA_RPB_COLS + dd], acc)
        by_row_offset.append(jnp.where(in_cols, acc * LOG2E, NEG_INF))
    lead, trail = _na_block_types(rows)
    n_blocks = rows // NA_QROWS
    type_blocks = list(range(lead + 1)) + list(range(n_blocks - trail, n_blocks))
    for ty, block in enumerate(type_blocks):
        r = block * NA_QROWS
        a = _na_first_key_row(block, rows, _clamp)
        for j in range(NA_QROWS):
            r0 = _clamp(r + j - NA_KH // 2, 0, rows - NA_KH)
            halves = []
            for i in range(NA_KROWS):
                in_rows = r0 <= a + i < r0 + NA_KH
                halves.append(by_row_offset[a + i - (r + j) + NA_KH - 1] if in_rows else neg)
            for t in range(NA_KROWS // 2):
                tile = jnp.where(upper, halves[2 * t + 1], halves[2 * t])
                o_ref[0, ty, j * GRID_W:(j + 1) * GRID_W, t * LANES:(t + 1) * LANES] = tile


def _na_bias(rpb, rows):
    n_types = sum(_na_block_types(rows)) + 1
    return pl.pallas_call(
        functools.partial(_na_bias_kernel, rows=rows),
        out_shape=jax.ShapeDtypeStruct((NA_HEADS, n_types, NA_Q, NA_K), F32),
        grid=(NA_HEADS,),
        in_specs=[pl.BlockSpec(memory_space=pltpu.SMEM)],
        out_specs=pl.BlockSpec((1, n_types, NA_Q, NA_K), lambda h: (h, 0, 0, 0)),
        compiler_params=_params("arbitrary"),
        name="na_bias",
    )(rpb.astype(F32).reshape(-1))


def _na(qkv, rpb):
    b, _, s, _ = qkv.shape
    rows = s // GRID_W
    assert rows % (NA_QROWS * NA_BLOCKS_PER_STEP) == 0
    tq = NA_BLOCKS_PER_STEP * NA_Q
    bias = _na_bias(rpb, rows)
    return pl.pallas_call(
        functools.partial(_na_kernel, rows=rows),
        out_shape=jax.ShapeDtypeStruct((b, PAIRS, s, LANES), BF16),
        grid=(b, PAIRS, s // tq),
        in_specs=[pl.BlockSpec((1, 1, tq, LANES), lambda bi, p, i: (bi, COL_NQ + p, i, 0)),
                  pl.BlockSpec((1, 1, s, LANES), lambda bi, p, i: (bi, COL_NK + p, 0, 0)),
                  pl.BlockSpec((1, 1, s, LANES), lambda bi, p, i: (bi, COL_NV + p, 0, 0)),
                  pl.BlockSpec((2,) + bias.shape[1:], lambda bi, p, i: (p, 0, 0, 0))],
        out_specs=pl.BlockSpec((1, 1, tq, LANES), lambda bi, p, i: (bi, p, i, 0)),
        compiler_params=_params("arbitrary", "arbitrary", "arbitrary"),
        name="na",
    )(qkv, qkv, qkv, bias)


SWA_KEYS = 3 * SWA_BLOCK
SWA_BLOCKS_PER_STEP = 8
SWA_GROUP = SWA_HEADS // SWA_KV_HEADS
assert COL_SQ % (PAIRS // SWA_KV_HEADS) == 0


def _swa_kernel(sink_ref, q_ref, k_ref, v_ref, o_ref, *, seq):
    kv = pl.program_id(1)
    rows = SWA_GROUP * SWA_BLOCK
    lane = lax.broadcasted_iota(jnp.int32, (SWA_BLOCK, LANES), 1)
    upper = lane >= HEAD_DIM
    row = lax.broadcasted_iota(jnp.int32, (rows, SWA_KEYS), 0)
    q_off = row % SWA_BLOCK
    k_off = lax.broadcasted_iota(jnp.int32, (rows, SWA_KEYS), 1)
    head = lax.broadcasted_iota(jnp.int32, (rows, 1), 0) // SWA_BLOCK
    sink = jnp.zeros((rows, 1), F32)
    for g in range(SWA_GROUP):
        sink = jnp.where(head == g, sink_ref[kv * SWA_GROUP + g], sink)
    sink = sink * LOG2E
    for j in range(SWA_BLOCKS_PER_STEP):
        n = pl.program_id(2) * SWA_BLOCKS_PER_STEP + j
        start = pl.multiple_of(jnp.clip((n - 1) * SWA_BLOCK, 0, seq - SWA_KEYS), SWA_BLOCK)
        ks = k_ref[0, 0, pl.ds(start, SWA_KEYS), :]
        vs = v_ref[0, 0, pl.ds(start, SWA_KEYS), :]
        qs = []
        for g in range(SWA_GROUP):
            q = q_ref[0, g // 2, j * SWA_BLOCK:(j + 1) * SWA_BLOCK, :]
            qs.append(jnp.where(upper if g % 2 else ~upper, q, jnp.zeros_like(q)))
        s = lax.dot_general(jnp.concatenate(qs, axis=0), ks, (((1,), (1,)), ((), ())), preferred_element_type=F32)
        in_window = jnp.abs((start + k_off) - (n * SWA_BLOCK + q_off)) <= SWA_WINDOW
        s = jnp.where(in_window, s, NEG_INF)
        m = jnp.maximum(jnp.max(s, axis=-1, keepdims=True), sink)
        e = jnp.exp2(s - m)
        l = jnp.sum(e, axis=-1, keepdims=True) + jnp.exp2(sink - m)
        o = jnp.dot(e.astype(BF16), vs, preferred_element_type=F32) / l
        for pair in range(SWA_GROUP // 2):
            even = o[(2 * pair) * SWA_BLOCK:(2 * pair + 1) * SWA_BLOCK]
            odd = o[(2 * pair + 1) * SWA_BLOCK:(2 * pair + 2) * SWA_BLOCK]
            o_ref[0, pair, j * SWA_BLOCK:(j + 1) * SWA_BLOCK, :] = jnp.where(upper, odd, even).astype(BF16)


def _swa(qkv, sinks):
    b, _, s, _ = qkv.shape
    pairs_per_kv = PAIRS // SWA_KV_HEADS
    tq = SWA_BLOCKS_PER_STEP * SWA_BLOCK
    return pl.pallas_call(
        functools.partial(_swa_kernel, seq=s),
        out_shape=jax.ShapeDtypeStruct((b, PAIRS, s, LANES), BF16),
        grid=(b, SWA_KV_HEADS, s // tq),
        in_specs=[pl.BlockSpec(memory_space=pltpu.SMEM),
                  pl.BlockSpec((1, pairs_per_kv, tq, LANES),
                               lambda bi, kv, n: (bi, COL_SQ // pairs_per_kv + kv, n, 0)),
                  pl.BlockSpec((1, 1, s, LANES), lambda bi, kv, n: (bi, COL_SK + kv, 0, 0)),
                  pl.BlockSpec((1, 1, s, LANES), lambda bi, kv, n: (bi, COL_SV + kv, 0, 0))],
        out_specs=pl.BlockSpec((1, pairs_per_kv, tq, LANES), lambda bi, kv, n: (bi, kv, n, 0)),
        compiler_params=_params("arbitrary", "arbitrary", "arbitrary"),
        name="swa",
    )(sinks, qkv, qkv, qkv)


ROUTER_COLS = N_GROUPS + N_EXPERTS


def _split2(v):
    hi = v.astype(BF16)
    lo = (v - hi.astype(F32)).astype(BF16)
    return hi, lo


def _route(logits):
    lane = lax.broadcasted_iota(jnp.int32, logits.shape, 1)
    big = jnp.int32(LANES)
    gmask = lane < N_GROUPS
    gl = jnp.where(gmask, logits, NEG_INF)
    gmax = jnp.max(gl, axis=-1, keepdims=True)
    g_top = jnp.min(jnp.where(gmask & (gl == gmax), lane, big), axis=-1, keepdims=True)
    g_weight = 1.0 / jnp.sum(jnp.where(gmask, jnp.exp(gl - gmax), 0.0), axis=-1, keepdims=True)
    lo = ROUTER_BASE + g_top * EXPERTS_PER_GROUP
    emask = (lane >= lo) & (lane < lo + EXPERTS_PER_GROUP)
    el = jnp.where(emask, logits, NEG_INF)
    m1 = jnp.max(el, axis=-1, keepdims=True)
    i1 = jnp.min(jnp.where(emask & (el == m1), lane, big), axis=-1, keepdims=True)
    emask2 = emask & (lane != i1)
    el2 = jnp.where(emask2, logits, NEG_INF)
    m2 = jnp.max(el2, axis=-1, keepdims=True)
    i2 = jnp.min(jnp.where(emask2 & (el2 == m2), lane, big), axis=-1, keepdims=True)
    e2 = jnp.exp(m2 - m1)
    w1 = g_weight / (1.0 + e2)
    w2 = g_weight * e2 / (1.0 + e2)
    return i1 - ROUTER_BASE, i2 - ROUTER_BASE, w1, w2


TOKEN_ROWS = D_MODEL // (2 * LANES)
WORD = jnp.uint32


def _to_token_tiles(ref, v, base=0):
    t, d = v.shape
    words = pltpu.pack_elementwise([v[:, :d // 2], v[:, d // 2:]], packed_dtype=BF16)
    for s in range(TOKEN_ROWS):
        ref[pl.ds(base + s, t, stride=TOKEN_ROWS), :] = words[:, s * LANES:(s + 1) * LANES]


def _token_words(ref, t, base=0):
    return jnp.concatenate([ref[pl.ds(base + s, t, stride=TOKEN_ROWS), :] for s in range(TOKEN_ROWS)], axis=-1)


def _from_token_tiles(ref, t, base=0):
    words = _token_words(ref, t, base)
    halves = [pltpu.unpack_elementwise(words, index=j, packed_dtype=BF16, unpacked_dtype=F32) for j in range(2)]
    return jnp.concatenate(halves, axis=-1)


MIX_CHUNK = 256


def _mix_kernel(na_ref, sw_ref, x_ref, bna_ref, bsw_ref, wo_ref, gpm_ref, ga_ref, gpf_ref, scf_ref, shf_ref,
                wr_ref, x1_ref, h2_ref, r_ref, rt_ref, cntc_ref, cntr_ref):
    first_step = (pl.program_id(0) == 0) & (pl.program_id(1) == 0)

    @pl.when(first_step)
    def _():
        cntc_ref[...] = jnp.zeros_like(cntc_ref)
        cntr_ref[...] = jnp.zeros_like(cntr_ref)

    t = MIX_CHUNK
    for c in range(x_ref.shape[1] // t):
        rows = slice(c * t, (c + 1) * t)

        def heads(ref):
            return jnp.concatenate([ref[0, j, rows, :] for j in range(PAIRS)], axis=-1).astype(F32)

        na = (_rms(heads(na_ref)) * bna_ref[...]).astype(BF16)
        sw = (_rms(heads(sw_ref)) * bsw_ref[...]).astype(BF16)
        mix = (jnp.dot(na, wo_ref[:NA_WIDTH, :], preferred_element_type=F32)
               + jnp.dot(sw, wo_ref[NA_WIDTH:, :], preferred_element_type=F32))
        x1 = x_ref[0, rows, :] + ga_ref[0] * (_rms(mix) * gpm_ref[...])
        x1_ref[0, rows, :] = x1
        h2 = (_rms(x1) * gpf_ref[...]) * (1.0 + scf_ref[0]) + shf_ref[0]
        _to_token_tiles(h2_ref, h2, base=c * t * TOKEN_ROWS)
        h_hi, h_lo = _split2(h2)
        both = (jnp.dot(h_hi, wr_ref[...], preferred_element_type=F32)
                + jnp.dot(h_lo, wr_ref[...], preferred_element_type=F32))
        logits = both + pltpu.roll(both, LANES - ROUTER_COLS, 1)
        e1, e2, w1, w2 = _route(logits)
        lane = lax.broadcasted_iota(jnp.int32, logits.shape, 1)
        r = jnp.where(lane == R_E1, e1.astype(F32),
                      jnp.where(lane == R_E2, e2.astype(F32),
                                jnp.where(lane == R_W1, w1, jnp.where(lane == R_W2, w2, 0.0))))
        r_ref[rows, :] = r
        rt = jnp.transpose(r)[:SUBLANES]
        rt_ref[:, rows] = rt
        on_lane = ((lane == e1) | (lane == e2)).astype(F32)
        cntr_ref[...] += jnp.broadcast_to(jnp.sum(on_lane, axis=0, keepdims=True), cntr_ref.shape)
        sub = lax.broadcasted_iota(jnp.int32, (N_EXPERTS, t), 0).astype(F32)
        on_sub = ((sub == rt[R_E1:R_E1 + 1]) | (sub == rt[R_E2:R_E2 + 1])).astype(F32)
        cntc_ref[...] += jnp.broadcast_to(jnp.sum(on_sub, axis=1, keepdims=True), cntc_ref.shape)


def _mix(na, sw, x, beta_na, beta_swa, w_out, g_post_mix, gate_a, g_pre_ffn, scale_f, shift_f, w_router3):
    b, s, d = x.shape
    tm = 1024
    nt = s // tm
    row = lambda bi, i: (bi, i, 0)
    const2 = lambda bi, i: (0, 0)
    per_b = lambda bi, i: (bi, 0, 0)
    return pl.pallas_call(
        _mix_kernel,
        out_shape=(jax.ShapeDtypeStruct((b, s, d), F32),
                   jax.ShapeDtypeStruct((b * s * TOKEN_ROWS, LANES), WORD),
                   jax.ShapeDtypeStruct((b * s, LANES), F32),
                   jax.ShapeDtypeStruct((SUBLANES, b * s), F32),
                   jax.ShapeDtypeStruct((N_EXPERTS, LANES), F32),
                   jax.ShapeDtypeStruct((SUBLANES, LANES), F32)),
        grid=(b, s // tm),
        in_specs=[pl.BlockSpec((1, PAIRS, tm, LANES), lambda bi, i: (bi, 0, i, 0)),
                  pl.BlockSpec((1, PAIRS, tm, LANES), lambda bi, i: (bi, 0, i, 0)),
                  pl.BlockSpec((1, tm, d), row),
                  pl.BlockSpec((1, NA_WIDTH), const2),
                  pl.BlockSpec((1, SWA_WIDTH), const2),
                  pl.BlockSpec((NA_WIDTH + SWA_WIDTH, d), const2),
                  pl.BlockSpec((1, d), const2),
                  pl.BlockSpec((1, 1, d), per_b),
                  pl.BlockSpec((1, d), const2),
                  pl.BlockSpec((1, 1, d), per_b),
                  pl.BlockSpec((1, 1, d), per_b),
                  pl.BlockSpec((d, LANES), const2)],
        out_specs=(pl.BlockSpec((1, tm, d), row),
                   pl.BlockSpec((tm * TOKEN_ROWS, LANES), lambda bi, i: (bi * nt + i, 0)),
                   pl.BlockSpec((tm, LANES), lambda bi, i: (bi * nt + i, 0)),
                   pl.BlockSpec((SUBLANES, tm), lambda bi, i: (0, bi * nt + i)),
                   pl.BlockSpec((N_EXPERTS, LANES), const2),
                   pl.BlockSpec((SUBLANES, LANES), const2)),
        compiler_params=_params("arbitrary", "arbitrary"),
        name="mix",
    )(na, sw, x, beta_na.reshape(1, -1), beta_swa.reshape(1, -1), w_out, g_post_mix.reshape(1, d), gate_a,
      g_pre_ffn.reshape(1, d), scale_f, shift_f, w_router3)


MOE_TILE = 256
PLAN_T = 512
I_TILE, I_EXPERT, I_LO, I_HI, I_FIRST = range(5)


def _plan_kernel(rt_ref, cntc_ref, cntr_ref, pos_ref, items_ref, start_ref, carry_ref, *, n_tiles, n_items):
    i = pl.program_id(0)
    sub = lax.broadcasted_iota(jnp.int32, (N_EXPERTS, LANES), 0)
    lane = lax.broadcasted_iota(jnp.int32, (N_EXPERTS, LANES), 1)

    @pl.when(i == 0)
    def _():
        c_col = cntc_ref[:, 0:1]
        c_row = cntr_ref[0:1, :]
        s_col = jnp.sum(jnp.where(lane < sub, c_row, 0.0), axis=1, keepdims=True)
        s_row = jnp.sum(jnp.where(sub < lane, c_col, 0.0), axis=0, keepdims=True)
        start_ref[...] = jnp.broadcast_to(s_col, start_ref.shape)
        carry_ref[...] = jnp.zeros_like(carry_ref)

        def tiles_of(s, c):
            first = jnp.floor(s * (1.0 / MOE_TILE))
            last = jnp.floor((s + c - 1.0) * (1.0 / MOE_TILE))
            return first, jnp.where(c > 0.0, last - first + 1.0, 0.0)

        f_col, n_col = tiles_of(s_col, c_col)
        _, n_row = tiles_of(s_row, c_row)
        i_col = jnp.sum(jnp.where(lane < sub, n_row, 0.0), axis=1, keepdims=True)
        total = jnp.sum(n_col, axis=0, keepdims=True)
        k = lax.broadcasted_iota(jnp.int32, (N_EXPERTS, n_items), 1).astype(F32)
        subk = lax.broadcasted_iota(jnp.int32, (N_EXPERTS, n_items), 0).astype(F32)
        ek = jnp.sum(jnp.where(i_col + n_col <= k, 1.0, 0.0), axis=0, keepdims=True)
        k0 = k[0:1]
        valid = k0 < total
        sel = subk == ek

        def pick(v):
            return jnp.sum(jnp.where(sel, v, 0.0), axis=0, keepdims=True)

        i_k, f_k, s_k, c_k = pick(i_col), pick(f_col), pick(s_col), pick(c_col)
        tile = f_k + (k0 - i_k)
        row0 = tile * MOE_TILE
        lo = jnp.maximum(s_k, row0) - row0
        hi = jnp.minimum(s_k + c_k, row0 + MOE_TILE) - row0
        rows = [jnp.where(valid, tile, n_tiles - 1.0), jnp.where(valid, ek, N_EXPERTS - 1.0),
                jnp.where(valid, lo, 0.0), jnp.where(valid, hi, 0.0),
                jnp.where(valid & (lo == 0.0), 1.0, 0.0)]
        rows += [jnp.zeros_like(k0)] * (SUBLANES - len(rows))
        items_ref[...] = jnp.concatenate(rows, axis=0).astype(jnp.int32)

    t = rt_ref.shape[1]
    e1 = rt_ref[R_E1:R_E1 + 1, :]
    e2 = rt_ref[R_E2:R_E2 + 1, :]
    sub_t = lax.broadcasted_iota(jnp.int32, (N_EXPERTS, t), 0).astype(F32)
    oh1 = sub_t == e1
    oh2 = sub_t == e2
    oh = (oh1 | oh2).astype(F32)
    before = (lax.broadcasted_iota(jnp.int32, (t, t), 0) < lax.broadcasted_iota(jnp.int32, (t, t), 1)).astype(BF16)
    rank = jnp.dot(oh.astype(BF16), before, preferred_element_type=F32)
    base = start_ref[:, 0:1] + carry_ref[:, 0:1] + rank
    pos1 = jnp.sum(jnp.where(oh1, base, 0.0), axis=0, keepdims=True)
    pos2 = jnp.sum(jnp.where(oh2, base, 0.0), axis=0, keepdims=True)
    carry_ref[...] += jnp.broadcast_to(jnp.sum(oh, axis=1, keepdims=True), carry_ref.shape)
    pos = jnp.concatenate([pos1, pos2] + [jnp.zeros_like(pos1)] * (SUBLANES - 2), axis=0)
    pos_ref[...] = pos.astype(jnp.int32)


def _plan(rt, cntc, cntr):
    n = rt.shape[1]
    n_tiles = 2 * n // MOE_TILE
    n_items = 2 * LANES
    assert n_tiles + N_EXPERTS <= n_items
    return pl.pallas_call(
        functools.partial(_plan_kernel, n_tiles=n_tiles, n_items=n_items),
        out_shape=(jax.ShapeDtypeStruct((SUBLANES, n), jnp.int32),
                   jax.ShapeDtypeStruct((SUBLANES, n_items), jnp.int32)),
        grid=(n // PLAN_T,),
        in_specs=[pl.BlockSpec((SUBLANES, PLAN_T), lambda i: (0, i)),
                  pl.BlockSpec((N_EXPERTS, LANES), lambda i: (0, 0)),
                  pl.BlockSpec((SUBLANES, LANES), lambda i: (0, 0))],
        out_specs=(pl.BlockSpec((SUBLANES, PLAN_T), lambda i: (0, i)),
                   pl.BlockSpec((SUBLANES, n_items), lambda i: (0, 0))),
        scratch_shapes=[pltpu.VMEM((N_EXPERTS, LANES), F32), pltpu.VMEM((N_EXPERTS, LANES), F32)],
        compiler_params=_params("arbitrary"),
        name="plan",
    )(rt, cntc, cntr)


DISPATCH_T = 1024


def _token_rows(ref, index):
    return ref.at[pl.ds(pl.multiple_of(index * TOKEN_ROWS, TOKEN_ROWS), TOKEN_ROWS), :]


def _dispatch_kernel(pos_ref, h_ref, xs_hbm, sem):
    def body(r, carry):
        src = _token_rows(h_ref, r)
        for k in range(2):
            pltpu.make_async_copy(src, _token_rows(xs_hbm, pos_ref[k, r]), sem.at[0]).start(priority=k)
        return carry

    lax.fori_loop(0, DISPATCH_T, body, 0, unroll=8)
    for k in range(2):
        pltpu.make_async_copy(h_ref, xs_hbm.at[pl.ds(0, DISPATCH_T * TOKEN_ROWS), :], sem.at[0]).wait()


def _dispatch(pos, h2):
    n = pos.shape[1]
    return pl.pallas_call(
        _dispatch_kernel,
        out_shape=jax.ShapeDtypeStruct((2 * n * TOKEN_ROWS, LANES), WORD),
        grid=(n // DISPATCH_T,),
        in_specs=[pl.BlockSpec((SUBLANES, DISPATCH_T), lambda i: (0, i), memory_space=pltpu.SMEM),
                  pl.BlockSpec((DISPATCH_T * TOKEN_ROWS, LANES), lambda i: (i, 0))],
        out_specs=pl.BlockSpec(memory_space=pl.ANY),
        scratch_shapes=[pltpu.SemaphoreType.DMA((1,))],
        compiler_params=_params("arbitrary"),
        name="dispatch",
    )(pos, h2)


def _expert_kernel(tile_ref, exp_ref, lo_ref, hi_ref, first_ref, xs_ref, wg_ref, wu_ref, wd_ref, o_ref):
    k = pl.program_id(0)
    lo = lo_ref[k]
    hi = hi_ref[k]

    @pl.when(first_ref[k] == 1)
    def _():
        o_ref[...] = jnp.zeros_like(o_ref)

    @pl.when(hi > lo)
    def _():
        x = _from_token_tiles(xs_ref, MOE_TILE).astype(BF16)
        gate = jnp.dot(x, wg_ref[0].astype(BF16), preferred_element_type=F32)
        up = jnp.dot(x, wu_ref[0].astype(BF16), preferred_element_type=F32)
        he = (gate * jax.nn.sigmoid(gate) * up).astype(BF16)
        ye = jnp.dot(he, wd_ref[0].astype(BF16), preferred_element_type=F32)
        row = lax.broadcasted_iota(jnp.int32, (MOE_TILE, 1), 0)
        mine = (row >= lo) & (row < hi)
        _to_token_tiles(o_ref, jnp.where(mine, ye, _from_token_tiles(o_ref, MOE_TILE)))


def _experts(items, xs, w_gate, w_up, w_down):
    n_rows = xs.shape[0] // TOKEN_ROWS
    n_items = n_rows // MOE_TILE + N_EXPERTS
    d = w_gate.shape[1]
    tile_map = lambda k, tile, exp, lo, hi, first: (tile[k], 0)
    exp_map = lambda k, tile, exp, lo, hi, first: (exp[k], 0, 0)
    grid_spec = pltpu.PrefetchScalarGridSpec(
        num_scalar_prefetch=5,
        grid=(n_items,),
        in_specs=[pl.BlockSpec((MOE_TILE * TOKEN_ROWS, LANES), tile_map),
                  pl.BlockSpec((1, d, EXPERT_FF), exp_map),
                  pl.BlockSpec((1, d, EXPERT_FF), exp_map),
                  pl.BlockSpec((1, EXPERT_FF, d), exp_map)],
        out_specs=pl.BlockSpec((MOE_TILE * TOKEN_ROWS, LANES), tile_map))
    return pl.pallas_call(
        _expert_kernel,
        out_shape=jax.ShapeDtypeStruct(xs.shape, WORD),
        grid_spec=grid_spec,
        compiler_params=_params("arbitrary"),
        name="experts",
    )(*(items[j, :n_items] for j in (I_TILE, I_EXPERT, I_LO, I_HI, I_FIRST)), xs, w_gate, w_up, w_down)


COMBINE_T = 256


def _combine_kernel(pos_ref, posn_ref, ys_hbm, r_ref, x1_ref, gf_ref, gpost_ref, o_ref, buf, sem):
    i = pl.program_id(0)
    n = pl.num_programs(0)
    slot = i % 2
    slot_rows = COMBINE_T * TOKEN_ROWS

    def start_row(p_ref, sl, r):
        for k in range(2):
            dst = buf.at[sl, pl.ds(pl.multiple_of(k * slot_rows + r * TOKEN_ROWS, TOKEN_ROWS), TOKEN_ROWS), :]
            pltpu.make_async_copy(_token_rows(ys_hbm, p_ref[k, r]), dst, sem.at[sl]).start(priority=k)

    def wait_slot(sl):
        pltpu.make_async_copy(ys_hbm.at[pl.ds(0, 2 * slot_rows), :], buf.at[sl], sem.at[sl]).wait()

    def issue(p_ref, sl):
        def body(r, carry):
            start_row(p_ref, sl, r)
            return carry

        lax.fori_loop(0, COMBINE_T, body, 0, unroll=8)

    @pl.when(i == 0)
    def _():
        issue(pos_ref, 0)

    @pl.when(i + 1 < n)
    def _():
        issue(posn_ref, 1 - slot)

    wait_slot(slot)
    ya = _from_token_tiles(buf.at[slot], COMBINE_T)
    yb = _from_token_tiles(buf.at[slot], COMBINE_T, base=slot_rows)
    r = r_ref[...]
    y = r[:, R_W1:R_W1 + 1] * ya + r[:, R_W2:R_W2 + 1] * yb
    o_ref[0] = x1_ref[0] + gf_ref[0] * (_rms(y) * gpost_ref[...])


def _combine(pos, ys, r, x1, gate_f, g_post_ffn):
    b, s, d = x1.shape
    nt = s // COMBINE_T
    n_steps = b * nt
    row = lambda i: (i // nt, i % nt, 0)
    pos_spec = lambda f: pl.BlockSpec((SUBLANES, COMBINE_T), lambda i: (0, f(i)), memory_space=pltpu.SMEM)
    return pl.pallas_call(
        _combine_kernel,
        out_shape=jax.ShapeDtypeStruct((b, s, d), F32),
        grid=(n_steps,),
        in_specs=[pos_spec(lambda i: i),
                  pos_spec(lambda i: jnp.minimum(i + 1, n_steps - 1)),
                  pl.BlockSpec(memory_space=pl.ANY),
                  pl.BlockSpec((COMBINE_T, LANES), lambda i: (i, 0)),
                  pl.BlockSpec((1, COMBINE_T, d), row),
                  pl.BlockSpec((1, 1, d), lambda i: (i // nt, 0, 0)),
                  pl.BlockSpec((1, d), lambda i: (0, 0))],
        out_specs=pl.BlockSpec((1, COMBINE_T, d), row),
        scratch_shapes=[pltpu.VMEM((2, 2 * COMBINE_T * TOKEN_ROWS, LANES), WORD), pltpu.SemaphoreType.DMA((2,))],
        compiler_params=_params("arbitrary"),
        name="combine",
    )(pos, pos, ys, r, x1, gate_f, g_post_ffn.reshape(1, d))


def _rope_tables(s):
    half = HEAD_DIM // 2
    inv = ROPE_THETA ** (-np.arange(half, dtype=np.float64) * 2.0 / HEAD_DIM)
    ang = np.arange(s, dtype=np.float64)[:, None] * inv[None, :]
    cos = np.cos(ang)
    sin = np.sin(ang)
    cos = np.concatenate([cos, cos, cos, cos], axis=-1)
    sin_signed = np.concatenate([-sin, sin, -sin, sin], axis=-1)
    return jnp.asarray(cos, F32), jnp.asarray(sin_signed, F32)


def _router_weights(w_group_router, w_expert_router):
    d = w_group_router.shape[0]
    we = jnp.transpose(w_expert_router, (1, 0, 2)).reshape(d, N_EXPERTS)
    hi, lo = _split2(jnp.concatenate([w_group_router, we], axis=-1))
    return jnp.concatenate([hi, lo, jnp.zeros((d, LANES - 2 * ROUTER_COLS), BF16)], axis=-1)


def kernel(x, c, w_ada, b_ada, g_pre_mix, w_in, na_rpb, swa_sinks, beta_na, beta_swa, w_out, g_post_mix, g_pre_ffn,
           w_group_router, w_expert_router, w_gate, w_up, w_down, g_post_ffn):
    b, s, d = x.shape
    depth = w_ada.shape[0]
    cos, sin_signed = _rope_tables(s)
    for l in range(depth):
        mod = _adaln(c, w_ada[l], b_ada[l]).reshape(b, N_MOD, 1, d)
        shift_a, scale_a, gate_a, shift_f, scale_f, gate_f = (mod[:, k] for k in range(N_MOD))
        qkv = _qkv(x, g_pre_mix[l], scale_a, shift_a, w_in[l].astype(BF16), cos, sin_signed)
        na = _na(qkv, na_rpb[l])
        sw = _swa(qkv, swa_sinks[l])
        x1, h2, r, rt, cntc, cntr = _mix(na, sw, x, beta_na[l], beta_swa[l], w_out[l].astype(BF16), g_post_mix[l],
                                          gate_a, g_pre_ffn[l], scale_f, shift_f,
                                          _router_weights(w_group_router[l], w_expert_router[l]))
        pos, items = _plan(rt, cntc, cntr)
        xs = _dispatch(pos, h2)
        ys = _experts(items, xs, w_gate[l], w_up[l], w_down[l])
        x = _combine(pos, ys, r, x1, gate_f, g_post_ffn[l])
    return x
```

```python
import functools

import jax
import jax.numpy as jnp
import numpy as np
from jax import lax
from jax.experimental import pallas as pl
from jax.experimental.pallas import tpu as pltpu

D_MODEL = 1024
GRID_W = 64
HEAD_DIM = 64
NA_HEADS = 8
NA_KH = 8
NA_KW = 16
SWA_HEADS = 8
SWA_KV_HEADS = 2
SWA_WINDOW = 128
SWA_BLOCK = 128
ROPE_THETA = 10000.0
NA_WIDTH = NA_HEADS * HEAD_DIM
SWA_WIDTH = SWA_HEADS * HEAD_DIM
N_GROUPS = 4
EXPERTS_PER_GROUP = 8
N_EXPERTS = N_GROUPS * EXPERTS_PER_GROUP
EXPERT_FF = 256
N_MOD = 6
EPS = 1e-6
NEG_INF = -1e30

LANES = 128
SUBLANES = 8
R_E1, R_E2, R_W1, R_W2 = range(4)
PAIRS = NA_HEADS // 2
COL_NQ, COL_NK, COL_NV, COL_SQ, COL_SK, COL_SV = 0, 4, 8, 12, 16, 18
QKV_TILES = 20
ROUTER_BASE = N_GROUPS
VMEM_LIMIT = 56 * 1024 * 1024

F32 = jnp.float32
BF16 = jnp.bfloat16
LOG2E = 1.4426950408889634


def _rms(v):
    return v * lax.rsqrt(jnp.mean(v * v, axis=-1, keepdims=True) + EPS)


def _params(*sem):
    return pltpu.CompilerParams(dimension_semantics=sem, vmem_limit_bytes=VMEM_LIMIT)


def _adaln_kernel(c_ref, w_ref, b_ref, o_ref):
    c = c_ref[...]
    a = c * jax.nn.sigmoid(c)
    o_ref[...] = jnp.dot(a, w_ref[...], precision=lax.Precision.HIGHEST,
                         preferred_element_type=F32) + b_ref[...]


def _adaln(c, w_ada, b_ada):
    batch, d = c.shape
    n = w_ada.shape[1]
    tn = 1024
    b = 8
    c = jnp.pad(c, ((0, b - batch), (0, 0)))
    return pl.pallas_call(
        _adaln_kernel,
        out_shape=jax.ShapeDtypeStruct((b, n), F32),
        grid=(n // tn,),
        in_specs=[pl.BlockSpec((b, d), lambda j: (0, 0)),
                  pl.BlockSpec((d, tn), lambda j: (0, j)),
                  pl.BlockSpec((1, tn), lambda j: (0, j))],
        out_specs=pl.BlockSpec((b, tn), lambda j: (0, j)),
        compiler_params=_params("arbitrary"),
        name="adaln",
    )(c, w_ada, b_ada.reshape(1, n))[:batch]


def _rope(v, cos, sin_signed, first_half):
    rot = jnp.where(first_half, pltpu.roll(v, LANES - HEAD_DIM // 2, 1), pltpu.roll(v, HEAD_DIM // 2, 1))
    return v * cos + rot * sin_signed


QKV_CHUNK = 256


def _qkv_kernel(x_ref, g_ref, sc_ref, sh_ref, w_ref, cos_ref, sin_ref, o_ref):
    scale = HEAD_DIM ** -0.5 * LOG2E
    lane = lax.broadcasted_iota(jnp.int32, (QKV_CHUNK, LANES), 1)
    first_half = (lane % HEAD_DIM) < HEAD_DIM // 2
    upper = lane >= HEAD_DIM

    def tile(v, j):
        return v[:, j * LANES:(j + 1) * LANES]

    for c in range(x_ref.shape[1] // QKV_CHUNK):
        rows = slice(c * QKV_CHUNK, (c + 1) * QKV_CHUNK)
        h = (_rms(x_ref[0, rows, :]) * g_ref[...]) * (1.0 + sc_ref[0]) + sh_ref[0]
        h = h.astype(BF16)
        cos = cos_ref[rows, :]
        sin = sin_ref[rows, :]

        def proj(col, width):
            return jnp.dot(h, w_ref[:, col * LANES:(col + width) * LANES], preferred_element_type=F32)

        nq, nk, nv, sq = proj(COL_NQ, 4), proj(COL_NK, 4), proj(COL_NV, 4), proj(COL_SQ, 4)
        for j in range(PAIRS):
            o_ref[0, COL_NQ + j, rows, :] = (tile(nq, j) * scale).astype(BF16)
            o_ref[0, COL_NK + j, rows, :] = tile(nk, j).astype(BF16)
            o_ref[0, COL_NV + j, rows, :] = tile(nv, j).astype(BF16)
            o_ref[0, COL_SQ + j, rows, :] = (_rope(tile(sq, j), cos, sin, first_half) * scale).astype(BF16)
        skv = proj(COL_SK, 2)
        k = _rope(tile(skv, 0), cos, sin, first_half)
        v = tile(skv, 1)
        for t, col in ((k, COL_SK), (v, COL_SV)):
            swapped = pltpu.roll(t, HEAD_DIM, 1)
            o_ref[0, col, rows, :] = jnp.where(upper, swapped, t).astype(BF16)
            o_ref[0, col + 1, rows, :] = jnp.where(upper, t, swapped).astype(BF16)


def _qkv(x, g, scale_a, shift_a, w_in, cos, sin):
    b, s, d = x.shape
    tm = 1024
    n_in = w_in.shape[1]
    return pl.pallas_call(
        _qkv_kernel,
        out_shape=jax.ShapeDtypeStruct((b, QKV_TILES, s, LANES), BF16),
        grid=(b, s // tm),
        in_specs=[pl.BlockSpec((1, tm, d), lambda bi, i: (bi, i, 0)),
                  pl.BlockSpec((1, d), lambda bi, i: (0, 0)),
                  pl.BlockSpec((1, 1, d), lambda bi, i: (bi, 0, 0)),
                  pl.BlockSpec((1, 1, d), lambda bi, i: (bi, 0, 0)),
                  pl.BlockSpec((d, n_in), lambda bi, i: (0, 0)),
                  pl.BlockSpec((tm, LANES), lambda bi, i: (i, 0)),
                  pl.BlockSpec((tm, LANES), lambda bi, i: (i, 0))],
        out_specs=pl.BlockSpec((1, QKV_TILES, tm, LANES), lambda bi, i: (bi, 0, i, 0)),
        compiler_params=_params("arbitrary", "arbitrary"),
        name="qkv",
    )(x, g.reshape(1, d), scale_a, shift_a, w_in, cos, sin)


NA_QROWS = 4
NA_KROWS = NA_QROWS + NA_KH
NA_BLOCKS_PER_STEP = 8
NA_Q = NA_QROWS * GRID_W
NA_K = NA_KROWS * GRID_W
NA_RPB_ROWS = 2 * NA_KH - 1
NA_RPB_COLS = 2 * NA_KW - 1


def _clamp(v, lo, hi):
    return min(max(v, lo), hi)


def _na_first_key_row(block, rows, clip):
    return clip(block * NA_QROWS - NA_KH // 2, 0, rows - NA_KROWS)


def _na_block_types(rows):
    def geometry(block):
        r = block * NA_QROWS
        a = _na_first_key_row(block, rows, _clamp)
        return (a - r,) + tuple(_clamp(r + j - NA_KH // 2, 0, rows - NA_KH) - r for j in range(NA_QROWS))

    n_blocks = rows // NA_QROWS
    interior = geometry(n_blocks // 2)
    lead = next(b for b in range(n_blocks) if geometry(b) == interior)
    trail = next(b for b in range(n_blocks) if geometry(n_blocks - 1 - b) == interior)
    assert all(geometry(b) == interior for b in range(lead, n_blocks - trail))
    return lead, trail


def _na_kernel(q_ref, k_ref, v_ref, bias_ref, o_ref, *, rows):
    lane = lax.broadcasted_iota(jnp.int32, (NA_Q, LANES), 1)
    upper = lane >= HEAD_DIM
    lead, trail = _na_block_types(rows)
    first_trailing = rows // NA_QROWS - trail
    for c in range(NA_BLOCKS_PER_STEP):
        block = pl.program_id(2) * NA_BLOCKS_PER_STEP + c
        a = _na_first_key_row(block, rows, jnp.clip)
        ty = jnp.where(block < lead, block, jnp.where(block >= first_trailing, block - first_trailing + lead + 1, lead))
        start = pl.multiple_of(a * GRID_W, GRID_W)
        q = q_ref[0, 0, c * NA_Q:(c + 1) * NA_Q, :]
        ks = k_ref[0, 0, pl.ds(start, NA_K), :]
        vs = v_ref[0, 0, pl.ds(start, NA_K), :]
        outs = []
        for hh in range(2):
            qm = jnp.where(upper if hh else ~upper, q, jnp.zeros_like(q))
            s = lax.dot_general(qm, ks, (((1,), (1,)), ((), ())), preferred_element_type=F32)
            s = s + bias_ref[hh, ty]
            m = jnp.max(s, axis=-1, keepdims=True)
            e = jnp.exp2(s - m)
            l = jnp.sum(e, axis=-1, keepdims=True)
            o = jnp.dot(e.astype(BF16), vs, preferred_element_type=F32)
            outs.append(o / l)
        o_ref[0, 0, c * NA_Q:(c + 1) * NA_Q, :] = jnp.where(upper, outs[1], outs[0]).astype(BF16)


def _na_bias_kernel(rpb_ref, o_ref, *, rows):
    h = pl.program_id(0)
    q = lax.broadcasted_iota(jnp.int32, (GRID_W, LANES), 0)
    lane = lax.broadcasted_iota(jnp.int32, (GRID_W, LANES), 1)
    kc = lane % GRID_W
    upper = lane >= GRID_W
    c0 = jnp.clip(q - NA_KW // 2, 0, GRID_W - NA_KW)
    in_cols = (kc >= c0) & (kc < c0 + NA_KW)
    dc = kc - q + NA_KW - 1
    neg = jnp.full((GRID_W, LANES), NEG_INF, F32)
    base = h * NA_RPB_ROWS * NA_RPB_COLS
    by_row_offset = []
    for d in range(NA_RPB_ROWS):
        acc = neg
        for dd in range(NA_RPB_COLS):
            acc = jnp.where(dc == dd, rpb_ref[base + d * NA_RPB_COLS + dd], acc)
        by_row_offset.append(jnp.where(in_cols, acc * LOG2E, NEG_INF))
    lead, trail = _na_block_types(rows)
    n_blocks = rows // NA_QROWS
    type_blocks = list(range(lead + 1)) + list(range(n_blocks - trail, n_blocks))
    for ty, block in enumerate(type_blocks):
        r = block * NA_QROWS
        a = _na_first_key_row(block, rows, _clamp)
        for j in range(NA_QROWS):
            r0 = _clamp(r + j - NA_KH // 2, 0, rows - NA_KH)
            halves = []
            for i in range(NA_KROWS):
                in_rows = r0 <= a + i < r0 + NA_KH
                halves.append(by_row_offset[a + i - (r + j) + NA_KH - 1] if in_rows else neg)
            for t in range(NA_KROWS // 2):
                tile = jnp.where(upper, halves[2 * t + 1], halves[2 * t])
                o_ref[0, ty, j * GRID_W:(j + 1) * GRID_W, t * LANES:(t + 1) * LANES] = tile


def _na_bias(rpb, rows):
    n_types = sum(_na_block_types(rows)) + 1
    return pl.pallas_call(
        functools.partial(_na_bias_kernel, rows=rows),
        out_shape=jax.ShapeDtypeStruct((NA_HEADS, n_types, NA_Q, NA_K), F32),
        grid=(NA_HEADS,),
        in_specs=[pl.BlockSpec(memory_space=pltpu.SMEM)],
        out_specs=pl.BlockSpec((1, n_types, NA_Q, NA_K), lambda h: (h, 0, 0, 0)),
        compiler_params=_params("arbitrary"),
        name="na_bias",
    )(rpb.astype(F32).reshape(-1))


def _na(qkv, rpb):
    b, _, s, _ = qkv.shape
    rows = s // GRID_W
    assert rows % (NA_QROWS * NA_BLOCKS_PER_STEP) == 0
    tq = NA_BLOCKS_PER_STEP * NA_Q
    bias = _na_bias(rpb, rows)
    return pl.pallas_call(
        functools.partial(_na_kernel, rows=rows),
        out_shape=jax.ShapeDtypeStruct((b, PAIRS, s, LANES), BF16),
        grid=(b, PAIRS, s // tq),
        in_specs=[pl.BlockSpec((1, 1, tq, LANES), lambda bi, p, i: (bi, COL_NQ + p, i, 0)),
                  pl.BlockSpec((1, 1, s, LANES), lambda bi, p, i: (bi, COL_NK + p, 0, 0)),
                  pl.BlockSpec((1, 1, s, LANES), lambda bi, p, i: (bi, COL_NV + p, 0, 0)),
                  pl.BlockSpec((2,) + bias.shape[1:], lambda bi, p, i: (p, 0, 0, 0))],
        out_specs=pl.BlockSpec((1, 1, tq, LANES), lambda bi, p, i: (bi, p, i, 0)),
        compiler_params=_params("arbitrary", "arbitrary", "arbitrary"),
        name="na",
    )(qkv, qkv, qkv, bias)


SWA_KEYS = 3 * SWA_BLOCK
SWA_BLOCKS_PER_STEP = 8
SWA_GROUP = SWA_HEADS // SWA_KV_HEADS
assert COL_SQ % (PAIRS // SWA_KV_HEADS) == 0


def _swa_kernel(sink_ref, q_ref, k_ref, v_ref, o_ref, *, seq):
    kv = pl.program_id(1)
    rows = SWA_GROUP * SWA_BLOCK
    lane = lax.broadcasted_iota(jnp.int32, (SWA_BLOCK, LANES), 1)
    upper = lane >= HEAD_DIM
    row = lax.broadcasted_iota(jnp.int32, (rows, SWA_KEYS), 0)
    q_off = row % SWA_BLOCK
    k_off = lax.broadcasted_iota(jnp.int32, (rows, SWA_KEYS), 1)
    head = lax.broadcasted_iota(jnp.int32, (rows, 1), 0) // SWA_BLOCK
    sink = jnp.zeros((rows, 1), F32)
    for g in range(SWA_GROUP):
        sink = jnp.where(head == g, sink_ref[kv * SWA_GROUP + g], sink)
    sink = sink * LOG2E
    for j in range(SWA_BLOCKS_PER_STEP):
        n = pl.program_id(2) * SWA_BLOCKS_PER_STEP + j
        start = pl.multiple_of(jnp.clip((n - 1) * SWA_BLOCK, 0, seq - SWA_KEYS), SWA_BLOCK)
        ks = k_ref[0, 0, pl.ds(start, SWA_KEYS), :]
        vs = v_ref[0, 0, pl.ds(start, SWA_KEYS), :]
        qs = []
        for g in range(SWA_GROUP):
            q = q_ref[0, g // 2, j * SWA_BLOCK:(j + 1) * SWA_BLOCK, :]
            qs.append(jnp.where(upper if g % 2 else ~upper, q, jnp.zeros_like(q)))
        s = lax.dot_general(jnp.concatenate(qs, axis=0), ks, (((1,), (1,)), ((), ())), preferred_element_type=F32)
        in_window = jnp.abs((start + k_off) - (n * SWA_BLOCK + q_off)) <= SWA_WINDOW
        s = jnp.where(in_window, s, NEG_INF)
        m = jnp.maximum(jnp.max(s, axis=-1, keepdims=True), sink)
        e = jnp.exp2(s - m)
        l = jnp.sum(e, axis=-1, keepdims=True) + jnp.exp2(sink - m)
        o = jnp.dot(e.astype(BF16), vs, preferred_element_type=F32) / l
        for pair in range(SWA_GROUP // 2):
            even = o[(2 * pair) * SWA_BLOCK:(2 * pair + 1) * SWA_BLOCK]
            odd = o[(2 * pair + 1) * SWA_BLOCK:(2 * pair + 2) * SWA_BLOCK]
            o_ref[0, pair, j * SWA_BLOCK:(j + 1) * SWA_BLOCK, :] = jnp.where(upper, odd, even).astype(BF16)


def _swa(qkv, sinks):
    b, _, s, _ = qkv.shape
    pairs_per_kv = PAIRS // SWA_KV_HEADS
    tq = SWA_BLOCKS_PER_STEP * SWA_BLOCK
    return pl.pallas_call(
        functools.partial(_swa_kernel, seq=s),
        out_shape=jax.ShapeDtypeStruct((b, PAIRS, s, LANES), BF16),
        grid=(b, SWA_KV_HEADS, s // tq),
        in_specs=[pl.BlockSpec(memory_space=pltpu.SMEM),
                  pl.BlockSpec((1, pairs_per_kv, tq, LANES),
                               lambda bi, kv, n: (bi, COL_SQ // pairs_per_kv + kv, n, 0)),
                  pl.BlockSpec((1, 1, s, LANES), lambda bi, kv, n: (bi, COL_SK + kv, 0, 0)),
                  pl.BlockSpec((1, 1, s, LANES), lambda bi, kv, n: (bi, COL_SV + kv, 0, 0))],
        out_specs=pl.BlockSpec((1, pairs_per_kv, tq, LANES), lambda bi, kv, n: (bi, kv, n, 0)),
        compiler_params=_params("arbitrary", "arbitrary", "arbitrary"),
        name="swa",
    )(sinks, qkv, qkv, qkv)


ROUTER_COLS = N_GROUPS + N_EXPERTS


def _split2(v):
    hi = v.astype(BF16)
    lo = (v - hi.astype(F32)).astype(BF16)
    return hi, lo


def _route(logits):
    lane = lax.broadcasted_iota(jnp.int32, logits.shape, 1)
    big = jnp.int32(LANES)
    gmask = lane < N_GROUPS
    gl = jnp.where(gmask, logits, NEG_INF)
    gmax = jnp.max(gl, axis=-1, keepdims=True)
    g_top = jnp.min(jnp.where(gmask & (gl == gmax), lane, big), axis=-1, keepdims=True)
    g_weight = 1.0 / jnp.sum(jnp.where(gmask, jnp.exp(gl - gmax), 0.0), axis=-1, keepdims=True)
    lo = ROUTER_BASE + g_top * EXPERTS_PER_GROUP
    emask = (lane >= lo) & (lane < lo + EXPERTS_PER_GROUP)
    el = jnp.where(emask, logits, NEG_INF)
    m1 = jnp.max(el, axis=-1, keepdims=True)
    i1 = jnp.min(jnp.where(emask & (el == m1), lane, big), axis=-1, keepdims=True)
    emask2 = emask & (lane != i1)
    el2 = jnp.where(emask2, logits, NEG_INF)
    m2 = jnp.max(el2, axis=-1, keepdims=True)
    i2 = jnp.min(jnp.where(emask2 & (el2 == m2), lane, big), axis=-1, keepdims=True)
    e2 = jnp.exp(m2 - m1)
    w1 = g_weight / (1.0 + e2)
    w2 = g_weight * e2 / (1.0 + e2)
    return i1 - ROUTER_BASE, i2 - ROUTER_BASE, w1, w2


TOKEN_ROWS = D_MODEL // (2 * LANES)
WORD = jnp.uint32


def _to_token_tiles(ref, v, base=0):
    t, d = v.shape
    words = pltpu.pack_elementwise([v[:, :d // 2], v[:, d // 2:]], packed_dtype=BF16)
    for s in range(TOKEN_ROWS):
        ref[pl.ds(base + s, t, stride=TOKEN_ROWS), :] = words[:, s * LANES:(s + 1) * LANES]


def _token_words(ref, t, base=0):
    return jnp.concatenate([ref[pl.ds(base + s, t, stride=TOKEN_ROWS), :] for s in range(TOKEN_ROWS)], axis=-1)


def _from_token_tiles(ref, t, base=0):
    words = _token_words(ref, t, base)
    halves = [pltpu.unpack_elementwise(words, index=j, packed_dtype=BF16, unpacked_dtype=F32) for j in range(2)]
    return jnp.concatenate(halves, axis=-1)


MIX_CHUNK = 256


def _mix_kernel(na_ref, sw_ref, x_ref, bna_ref, bsw_ref, wo_ref, gpm_ref, ga_ref, gpf_ref, scf_ref, shf_ref,
                wr_ref, x1_ref, h2_ref, r_ref, rt_ref, cntc_ref, cntr_ref):
    first_step = (pl.program_id(0) == 0) & (pl.program_id(1) == 0)

    @pl.when(first_step)
    def _():
        cntc_ref[...] = jnp.zeros_like(cntc_ref)
        cntr_ref[...] = jnp.zeros_like(cntr_ref)

    t = MIX_CHUNK
    for c in range(x_ref.shape[1] // t):
        rows = slice(c * t, (c + 1) * t)

        def heads(ref):
            return jnp.concatenate([ref[0, j, rows, :] for j in range(PAIRS)], axis=-1).astype(F32)

        na = (_rms(heads(na_ref)) * bna_ref[...]).astype(BF16)
        sw = (_rms(heads(sw_ref)) * bsw_ref[...]).astype(BF16)
        mix = (jnp.dot(na, wo_ref[:NA_WIDTH, :], preferred_element_type=F32)
               + jnp.dot(sw, wo_ref[NA_WIDTH:, :], preferred_element_type=F32))
        x1 = x_ref[0, rows, :] + ga_ref[0] * (_rms(mix) * gpm_ref[...])
        x1_ref[0, rows, :] = x1
        h2 = (_rms(x1) * gpf_ref[...]) * (1.0 + scf_ref[0]) + shf_ref[0]
        _to_token_tiles(h2_ref, h2, base=c * t * TOKEN_ROWS)
        h_hi, h_lo = _split2(h2)
        both = (jnp.dot(h_hi, wr_ref[...], preferred_element_type=F32)
                + jnp.dot(h_lo, wr_ref[...], preferred_element_type=F32))
        logits = both + pltpu.roll(both, LANES - ROUTER_COLS, 1)
        e1, e2, w1, w2 = _route(logits)
        lane = lax.broadcasted_iota(jnp.int32, logits.shape, 1)
        r = jnp.where(lane == R_E1, e1.astype(F32),
                      jnp.where(lane == R_E2, e2.astype(F32),
                                jnp.where(lane == R_W1, w1, jnp.where(lane == R_W2, w2, 0.0))))
        r_ref[rows, :] = r
        rt = jnp.transpose(r)[:SUBLANES]
        rt_ref[:, rows] = rt
        on_lane = ((lane == e1) | (lane == e2)).astype(F32)
        cntr_ref[...] += jnp.broadcast_to(jnp.sum(on_lane, axis=0, keepdims=True), cntr_ref.shape)
        sub = lax.broadcasted_iota(jnp.int32, (N_EXPERTS, t), 0).astype(F32)
        on_sub = ((sub == rt[R_E1:R_E1 + 1]) | (sub == rt[R_E2:R_E2 + 1])).astype(F32)
        cntc_ref[...] += jnp.broadcast_to(jnp.sum(on_sub, axis=1, keepdims=True), cntc_ref.shape)


def _mix(na, sw, x, beta_na, beta_swa, w_out, g_post_mix, gate_a, g_pre_ffn, scale_f, shift_f, w_router3):
    b, s, d = x.shape
    tm = 1024
    nt = s // tm
    row = lambda bi, i: (bi, i, 0)
    const2 = lambda bi, i: (0, 0)
    per_b = lambda bi, i: (bi, 0, 0)
    return pl.pallas_call(
        _mix_kernel,
        out_shape=(jax.ShapeDtypeStruct((b, s, d), F32),
                   jax.ShapeDtypeStruct((b * s * TOKEN_ROWS, LANES), WORD),
                   jax.ShapeDtypeStruct((b * s, LANES), F32),
                   jax.ShapeDtypeStruct((SUBLANES, b * s), F32),
                   jax.ShapeDtypeStruct((N_EXPERTS, LANES), F32),
                   jax.ShapeDtypeStruct((SUBLANES, LANES), F32)),
        grid=(b, s // tm),
        in_specs=[pl.BlockSpec((1, PAIRS, tm, LANES), lambda bi, i: (bi, 0, i, 0)),
                  pl.BlockSpec((1, PAIRS, tm, LANES), lambda bi, i: (bi, 0, i, 0)),
                  pl.BlockSpec((1, tm, d), row),
                  pl.BlockSpec((1, NA_WIDTH), const2),
                  pl.BlockSpec((1, SWA_WIDTH), const2),
                  pl.BlockSpec((NA_WIDTH + SWA_WIDTH, d), const2),
                  pl.BlockSpec((1, d), const2),
                  pl.BlockSpec((1, 1, d), per_b),
                  pl.BlockSpec((1, d), const2),
                  pl.BlockSpec((1, 1, d), per_b),
                  pl.BlockSpec((1, 1, d), per_b),
                  pl.BlockSpec((d, LANES), const2)],
        out_specs=(pl.BlockSpec((1, tm, d), row),
                   pl.BlockSpec((tm * TOKEN_ROWS, LANES), lambda bi, i: (bi * nt + i, 0)),
                   pl.BlockSpec((tm, LANES), lambda bi, i: (bi * nt + i, 0)),
                   pl.BlockSpec((SUBLANES, tm), lambda bi, i: (0, bi * nt + i)),
                   pl.BlockSpec((N_EXPERTS, LANES), const2),
                   pl.BlockSpec((SUBLANES, LANES), const2)),
        compiler_params=_params("arbitrary", "arbitrary"),
        name="mix",
    )(na, sw, x, beta_na.reshape(1, -1), beta_swa.reshape(1, -1), w_out, g_post_mix.reshape(1, d), gate_a,
      g_pre_ffn.reshape(1, d), scale_f, shift_f, w_router3)


MOE_TILE = 256
PLAN_T = 512
I_TILE, I_EXPERT, I_LO, I_HI, I_FIRST, I_NEW, I_NEXT, I_ORDER = range(8)


def _plan_kernel(rt_ref, cntc_ref, cntr_ref, pos_ref, items_ref, start_ref, carry_ref, *, n_tiles, n_items):
    i = pl.program_id(0)
    sub = lax.broadcasted_iota(jnp.int32, (N_EXPERTS, LANES), 0)
    lane = lax.broadcasted_iota(jnp.int32, (N_EXPERTS, LANES), 1)

    @pl.when(i == 0)
    def _():
        c_col = cntc_ref[:, 0:1]
        c_row = cntr_ref[0:1, :]
        s_col = jnp.sum(jnp.where(lane < sub, c_row, 0.0), axis=1, keepdims=True)
        s_row = jnp.sum(jnp.where(sub < lane, c_col, 0.0), axis=0, keepdims=True)
        start_ref[...] = jnp.broadcast_to(s_col, start_ref.shape)
        carry_ref[...] = jnp.zeros_like(carry_ref)

        def tiles_of(s, c):
            first = jnp.floor(s * (1.0 / MOE_TILE))
            last = jnp.floor((s + c - 1.0) * (1.0 / MOE_TILE))
            return first, jnp.where(c > 0.0, last - first + 1.0, 0.0)

        f_col, n_col = tiles_of(s_col, c_col)
        _, n_row = tiles_of(s_row, c_row)
        i_col = jnp.sum(jnp.where(lane < sub, n_row, 0.0), axis=1, keepdims=True)
        total = jnp.sum(n_col, axis=0, keepdims=True)
        k = lax.broadcasted_iota(jnp.int32, (N_EXPERTS, n_items), 1).astype(F32)
        subk = lax.broadcasted_iota(jnp.int32, (N_EXPERTS, n_items), 0).astype(F32)
        ek = jnp.sum(jnp.where(i_col + n_col <= k, 1.0, 0.0), axis=0, keepdims=True)
        k0 = k[0:1]
        valid = k0 < total
        sel = subk == ek

        def pick(v):
            return jnp.sum(jnp.where(sel, v, 0.0), axis=0, keepdims=True)

        i_k, f_k, s_k, c_k = pick(i_col), pick(f_col), pick(s_col), pick(c_col)
        tile = f_k + (k0 - i_k)
        row0 = tile * MOE_TILE
        lo = jnp.maximum(s_k, row0) - row0
        hi = jnp.minimum(s_k + c_k, row0 + MOE_TILE) - row0
        present = n_col > 0.0
        last_expert = jnp.sum(jnp.where(i_col + n_col <= total - 1.0, 1.0, 0.0), axis=0, keepdims=True)
        nxt = jnp.min(jnp.where(present & (subk > ek), subk, float(N_EXPERTS)), axis=0, keepdims=True)
        order = jnp.sum(jnp.where(present & (subk < ek), 1.0, 0.0), axis=0, keepdims=True)
        rows = [jnp.where(valid, tile, n_tiles - 1.0), jnp.where(valid, ek, last_expert),
                jnp.where(valid, lo, 0.0), jnp.where(valid, hi, 0.0),
                jnp.where(valid & (lo == 0.0), 1.0, 0.0),
                jnp.where(valid & (k0 == i_k), 1.0, 0.0), jnp.where(valid, nxt, float(N_EXPERTS)),
                jnp.where(valid, order, 0.0)]
        assert len(rows) == SUBLANES
        items_ref[...] = jnp.concatenate(rows, axis=0).astype(jnp.int32)

    t = rt_ref.shape[1]
    e1 = rt_ref[R_E1:R_E1 + 1, :]
    e2 = rt_ref[R_E2:R_E2 + 1, :]
    sub_t = lax.broadcasted_iota(jnp.int32, (N_EXPERTS, t), 0).astype(F32)
    oh1 = sub_t == e1
    oh2 = sub_t == e2
    oh = (oh1 | oh2).astype(F32)
    before = (lax.broadcasted_iota(jnp.int32, (t, t), 0) < lax.broadcasted_iota(jnp.int32, (t, t), 1)).astype(BF16)
    rank = jnp.dot(oh.astype(BF16), before, preferred_element_type=F32)
    base = start_ref[:, 0:1] + carry_ref[:, 0:1] + rank
    pos1 = jnp.sum(jnp.where(oh1, base, 0.0), axis=0, keepdims=True)
    pos2 = jnp.sum(jnp.where(oh2, base, 0.0), axis=0, keepdims=True)
    carry_ref[...] += jnp.broadcast_to(jnp.sum(oh, axis=1, keepdims=True), carry_ref.shape)
    pos = jnp.concatenate([pos1, pos2] + [jnp.zeros_like(pos1)] * (SUBLANES - 2), axis=0)
    pos_ref[...] = pos.astype(jnp.int32)


def _plan(rt, cntc, cntr):
    n = rt.shape[1]
    n_tiles = 2 * n // MOE_TILE
    n_items = 2 * LANES
    assert n_tiles + N_EXPERTS <= n_items
    return pl.pallas_call(
        functools.partial(_plan_kernel, n_tiles=n_tiles, n_items=n_items),
        out_shape=(jax.ShapeDtypeStruct((SUBLANES, n), jnp.int32),
                   jax.ShapeDtypeStruct((SUBLANES, n_items), jnp.int32)),
        grid=(n // PLAN_T,),
        in_specs=[pl.BlockSpec((SUBLANES, PLAN_T), lambda i: (0, i)),
                  pl.BlockSpec((N_EXPERTS, LANES), lambda i: (0, 0)),
                  pl.BlockSpec((SUBLANES, LANES), lambda i: (0, 0))],
        out_specs=(pl.BlockSpec((SUBLANES, PLAN_T), lambda i: (0, i)),
                   pl.BlockSpec((SUBLANES, n_items), lambda i: (0, 0))),
        scratch_shapes=[pltpu.VMEM((N_EXPERTS, LANES), F32), pltpu.VMEM((N_EXPERTS, LANES), F32)],
        compiler_params=_params("arbitrary"),
        name="plan",
    )(rt, cntc, cntr)


DISPATCH_T = 1024


def _token_rows(ref, index):
    return ref.at[pl.ds(pl.multiple_of(index * TOKEN_ROWS, TOKEN_ROWS), TOKEN_ROWS), :]


def _dispatch_kernel(pos_ref, h_ref, xs_hbm, sem):
    def body(r, carry):
        src = _token_rows(h_ref, r)
        for k in range(2):
            pltpu.make_async_copy(src, _token_rows(xs_hbm, pos_ref[k, r]), sem.at[0]).start(priority=k)
        return carry

    lax.fori_loop(0, DISPATCH_T, body, 0, unroll=8)
    for k in range(2):
        pltpu.make_async_copy(h_ref, xs_hbm.at[pl.ds(0, DISPATCH_T * TOKEN_ROWS), :], sem.at[0]).wait()


def _dispatch(pos, h2):
    n = pos.shape[1]
    return pl.pallas_call(
        _dispatch_kernel,
        out_shape=jax.ShapeDtypeStruct((2 * n * TOKEN_ROWS, LANES), WORD),
        grid=(n // DISPATCH_T,),
        in_specs=[pl.BlockSpec((SUBLANES, DISPATCH_T), lambda i: (0, i), memory_space=pltpu.SMEM),
                  pl.BlockSpec((DISPATCH_T * TOKEN_ROWS, LANES), lambda i: (i, 0))],
        out_specs=pl.BlockSpec(memory_space=pl.ANY),
        scratch_shapes=[pltpu.SemaphoreType.DMA((1,))],
        compiler_params=_params("arbitrary"),
        name="dispatch",
    )(pos, h2)


def _expert_kernel(tile_ref, exp_ref, lo_ref, hi_ref, first_ref, new_ref, next_ref, order_ref,
                   xs_ref, wg_hbm, wu_hbm, wd_hbm, o_ref, wg_buf, wu_buf, wd_buf, sem):
    k = pl.program_id(0)
    lo = lo_ref[k]
    hi = hi_ref[k]
    slot = order_ref[k] % 2

    def weight_copies(expert, sl):
        return [pltpu.make_async_copy(src.at[expert], dst.at[sl], sem.at[sl])
                for src, dst in ((wg_hbm, wg_buf), (wu_hbm, wu_buf), (wd_hbm, wd_buf))]

    @pl.when(k == 0)
    def _():
        for copy in weight_copies(exp_ref[0], 0):
            copy.start()

    @pl.when(new_ref[k] == 1)
    def _():
        for copy in weight_copies(exp_ref[k], slot):
            copy.wait()

        @pl.when(next_ref[k] < N_EXPERTS)
        def _():
            for copy in weight_copies(next_ref[k], 1 - slot):
                copy.start()

    @pl.when(first_ref[k] == 1)
    def _():
        o_ref[...] = jnp.zeros_like(o_ref)

    @pl.when(hi > lo)
    def _():
        x = _from_token_tiles(xs_ref, MOE_TILE).astype(BF16)
        gate = jnp.dot(x, wg_buf[slot].astype(BF16), preferred_element_type=F32)
        up = jnp.dot(x, wu_buf[slot].astype(BF16), preferred_element_type=F32)
        he = (gate * jax.nn.sigmoid(gate) * up).astype(BF16)
        ye = jnp.dot(he, wd_buf[slot].astype(BF16), preferred_element_type=F32)
        row = lax.broadcasted_iota(jnp.int32, (MOE_TILE, 1), 0)
        mine = (row >= lo) & (row < hi)
        _to_token_tiles(o_ref, jnp.where(mine, ye, _from_token_tiles(o_ref, MOE_TILE)))


def _experts(items, xs, w_gate, w_up, w_down):
    n_rows = xs.shape[0] // TOKEN_ROWS
    n_items = n_rows // MOE_TILE + N_EXPERTS
    d = w_gate.shape[1]
    tile_map = lambda k, tile, *_: (tile[k], 0)
    grid_spec = pltpu.PrefetchScalarGridSpec(
        num_scalar_prefetch=SUBLANES,
        grid=(n_items,),
        in_specs=[pl.BlockSpec((MOE_TILE * TOKEN_ROWS, LANES), tile_map),
                  pl.BlockSpec(memory_space=pl.ANY),
                  pl.BlockSpec(memory_space=pl.ANY),
                  pl.BlockSpec(memory_space=pl.ANY)],
        out_specs=pl.BlockSpec((MOE_TILE * TOKEN_ROWS, LANES), tile_map),
        scratch_shapes=[pltpu.VMEM((2, d, EXPERT_FF), F32), pltpu.VMEM((2, d, EXPERT_FF), F32),
                        pltpu.VMEM((2, EXPERT_FF, d), F32), pltpu.SemaphoreType.DMA((2,))])
    return pl.pallas_call(
        _expert_kernel,
        out_shape=jax.ShapeDtypeStruct(xs.shape, WORD),
        grid_spec=grid_spec,
        compiler_params=_params("arbitrary"),
        name="experts",
    )(*(items[j, :n_items] for j in range(SUBLANES)), xs, w_gate, w_up, w_down)


COMBINE_T = 256


def _combine_kernel(pos_ref, posn_ref, ys_hbm, r_ref, x1_ref, gf_ref, gpost_ref, o_ref, buf, sem):
    i = pl.program_id(0)
    n = pl.num_programs(0)
    slot = i % 2
    slot_rows = COMBINE_T * TOKEN_ROWS

    def start_row(p_ref, sl, r):
        for k in range(2):
            dst = buf.at[sl, pl.ds(pl.multiple_of(k * slot_rows + r * TOKEN_ROWS, TOKEN_ROWS), TOKEN_ROWS), :]
            pltpu.make_async_copy(_token_rows(ys_hbm, p_ref[k, r]), dst, sem.at[sl]).start(priority=k)

    def wait_slot(sl):
        pltpu.make_async_copy(ys_hbm.at[pl.ds(0, 2 * slot_rows), :], buf.at[sl], sem.at[sl]).wait()

    def issue(p_ref, sl):
        def body(r, carry):
            start_row(p_ref, sl, r)
            return carry

        lax.fori_loop(0, COMBINE_T, body, 0, unroll=8)

    @pl.when(i == 0)
    def _():
        issue(pos_ref, 0)

    @pl.when(i + 1 < n)
    def _():
        issue(posn_ref, 1 - slot)

    wait_slot(slot)
    ya = _from_token_tiles(buf.at[slot], COMBINE_T)
    yb = _from_token_tiles(buf.at[slot], COMBINE_T, base=slot_rows)
    r = r_ref[...]
    y = r[:, R_W1:R_W1 + 1] * ya + r[:, R_W2:R_W2 + 1] * yb
    o_ref[0] = x1_ref[0] + gf_ref[0] * (_rms(y) * gpost_ref[...])


def _combine(pos, ys, r, x1, gate_f, g_post_ffn):
    b, s, d = x1.shape
    nt = s // COMBINE_T
    n_steps = b * nt
    row = lambda i: (i // nt, i % nt, 0)
    pos_spec = lambda f: pl.BlockSpec((SUBLANES, COMBINE_T), lambda i: (0, f(i)), memory_space=pltpu.SMEM)
    return pl.pallas_call(
        _combine_kernel,
        out_shape=jax.ShapeDtypeStruct((b, s, d), F32),
        grid=(n_steps,),
        in_specs=[pos_spec(lambda i: i),
                  pos_spec(lambda i: jnp.minimum(i + 1, n_steps - 1)),
                  pl.BlockSpec(memory_space=pl.ANY),
                  pl.BlockSpec((COMBINE_T, LANES), lambda i: (i, 0)),
                  pl.BlockSpec((1, COMBINE_T, d), row),
                  pl.BlockSpec((1, 1, d), lambda i: (i // nt, 0, 0)),
                  pl.BlockSpec((1, d), lambda i: (0, 0))],
        out_specs=pl.BlockSpec((1, COMBINE_T, d), row),
        scratch_shapes=[pltpu.VMEM((2, 2 * COMBINE_T * TOKEN_ROWS, LANES), WORD), pltpu.SemaphoreType.DMA((2,))],
        compiler_params=_params("arbitrary"),
        name="combine",
    )(pos, pos, ys, r, x1, gate_f, g_post_ffn.reshape(1, d))


def _rope_tables(s):
    half = HEAD_DIM // 2
    inv = ROPE_THETA ** (-np.arange(half, dtype=np.float64) * 2.0 / HEAD_DIM)
    ang = np.arange(s, dtype=np.float64)[:, None] * inv[None, :]
    cos = np.cos(ang)
    sin = np.sin(ang)
    cos = np.concatenate([cos, cos, cos, cos], axis=-1)
    sin_signed = np.concatenate([-sin, sin, -sin, sin], axis=-1)
    return jnp.asarray(cos, F32), jnp.asarray(sin_signed, F32)


def _router_weights(w_group_router, w_expert_router):
    d = w_group_router.shape[0]
    we = jnp.transpose(w_expert_router, (1, 0, 2)).reshape(d, N_EXPERTS)
    hi, lo = _split2(jnp.concatenate([w_group_router, we], axis=-1))
    return jnp.concatenate([hi, lo, jnp.zeros((d, LANES - 2 * ROUTER_COLS), BF16)], axis=-1)


def kernel(x, c, w_ada, b_ada, g_pre_mix, w_in, na_rpb, swa_sinks, beta_na, beta_swa, w_out, g_post_mix, g_pre_ffn,
           w_group_router, w_expert_router, w_gate, w_up, w_down, g_post_ffn):
    b, s, d = x.shape
    depth = w_ada.shape[0]
    cos, sin_signed = _rope_tables(s)
    for l in range(depth):
        mod = _adaln(c, w_ada[l], b_ada[l]).reshape(b, N_MOD, 1, d)
        shift_a, scale_a, gate_a, shift_f, scale_f, gate_f = (mod[:, k] for k in range(N_MOD))
        qkv = _qkv(x, g_pre_mix[l], scale_a, shift_a, w_in[l].astype(BF16), cos, sin_signed)
        na = _na(qkv, na_rpb[l])
        sw = _swa(qkv, swa_sinks[l])
        x1, h2, r, rt, cntc, cntr = _mix(na, sw, x, beta_na[l], beta_swa[l], w_out[l].astype(BF16), g_post_mix[l],
                                          gate_a, g_pre_ffn[l], scale_f, shift_f,
                                          _router_weights(w_group_router[l], w_expert_router[l]))
        pos, items = _plan(rt, cntc, cntr)
        xs = _dispatch(pos, h2)
        ys = _experts(items, xs, w_gate[l], w_up[l], w_down[l])
        x = _combine(pos, ys, r, x1, gate_f, g_post_ffn[l])
    return x
```

```python
import functools

import jax
import jax.numpy as jnp
import numpy as np
from jax import lax
from jax.experimental import pallas as pl
from jax.experimental.pallas import tpu as pltpu

D_MODEL = 1024
GRID_W = 64
HEAD_DIM = 64
NA_HEADS = 8
NA_KH = 8
NA_KW = 16
SWA_HEADS = 8
SWA_KV_HEADS = 2
SWA_WINDOW = 128
SWA_BLOCK = 128
ROPE_THETA = 10000.0
NA_WIDTH = NA_HEADS * HEAD_DIM
SWA_WIDTH = SWA_HEADS * HEAD_DIM
N_GROUPS = 4
EXPERTS_PER_GROUP = 8
N_EXPERTS = N_GROUPS * EXPERTS_PER_GROUP
EXPERT_FF = 256
N_MOD = 6
EPS = 1e-6
NEG_INF = -1e30

LANES = 128
SUBLANES = 8
R_E1, R_E2, R_W1, R_W2 = range(4)
PAIRS = NA_HEADS // 2
COL_NQ, COL_NK, COL_NV, COL_SQ, COL_SK, COL_SV = 0, 4, 8, 12, 16, 18
QKV_TILES = 20
ROUTER_BASE = N_GROUPS
VMEM_LIMIT = 56 * 1024 * 1024

F32 = jnp.float32
BF16 = jnp.bfloat16
LOG2E = 1.4426950408889634


def _rms(v):
    return v * lax.rsqrt(jnp.mean(v * v, axis=-1, keepdims=True) + EPS)


def _params(*sem):
    return pltpu.CompilerParams(dimension_semantics=sem, vmem_limit_bytes=VMEM_LIMIT)


def _adaln_kernel(c_ref, w_ref, b_ref, o_ref):
    c = c_ref[...]
    a = c * jax.nn.sigmoid(c)
    o_ref[...] = jnp.dot(a, w_ref[...], precision=lax.Precision.HIGHEST,
                         preferred_element_type=F32) + b_ref[...]


def _adaln(c, w_ada, b_ada):
    batch, d = c.shape
    n = w_ada.shape[1]
    tn = 2048
    b = 8
    c = jnp.pad(c, ((0, b - batch), (0, 0)))
    return pl.pallas_call(
        _adaln_kernel,
        out_shape=jax.ShapeDtypeStruct((b, n), F32),
        grid=(n // tn,),
        in_specs=[pl.BlockSpec((b, d), lambda j: (0, 0)),
                  pl.BlockSpec((d, tn), lambda j: (0, j)),
                  pl.BlockSpec((1, tn), lambda j: (0, j))],
        out_specs=pl.BlockSpec((b, tn), lambda j: (0, j)),
        compiler_params=_params("arbitrary"),
        name="adaln",
    )(c, w_ada, b_ada.reshape(1, n))[:batch]


def _rope(v, cos, sin_signed, first_half):
    rot = jnp.where(first_half, pltpu.roll(v, LANES - HEAD_DIM // 2, 1), pltpu.roll(v, HEAD_DIM // 2, 1))
    return v * cos + rot * sin_signed


QKV_CHUNK = 256


def _qkv_kernel(x_ref, g_ref, sc_ref, sh_ref, w_ref, cos_ref, sin_ref, o_ref):
    scale = HEAD_DIM ** -0.5 * LOG2E
    lane = lax.broadcasted_iota(jnp.int32, (QKV_CHUNK, LANES), 1)
    first_half = (lane % HEAD_DIM) < HEAD_DIM // 2
    upper = lane >= HEAD_DIM

    def tile(v, j):
        return v[:, j * LANES:(j + 1) * LANES]

    for c in range(x_ref.shape[1] // QKV_CHUNK):
        rows = slice(c * QKV_CHUNK, (c + 1) * QKV_CHUNK)
        h = (_rms(x_ref[0, rows, :]) * g_ref[...]) * (1.0 + sc_ref[0]) + sh_ref[0]
        h = h.astype(BF16)
        cos = cos_ref[rows, :]
        sin = sin_ref[rows, :]

        def proj(col, width):
            return jnp.dot(h, w_ref[:, col * LANES:(col + width) * LANES], preferred_element_type=F32)

        nq, nk, nv, sq = proj(COL_NQ, 4), proj(COL_NK, 4), proj(COL_NV, 4), proj(COL_SQ, 4)
        for j in range(PAIRS):
            o_ref[0, COL_NQ + j, rows, :] = (tile(nq, j) * scale).astype(BF16)
            o_ref[0, COL_NK + j, rows, :] = tile(nk, j).astype(BF16)
            o_ref[0, COL_NV + j, rows, :] = tile(nv, j).astype(BF16)
            o_ref[0, COL_SQ + j, rows, :] = (_rope(tile(sq, j), cos, sin, first_half) * scale).astype(BF16)
        skv = proj(COL_SK, 2)
        k = _rope(tile(skv, 0), cos, sin, first_half)
        v = tile(skv, 1)
        for t, col in ((k, COL_SK), (v, COL_SV)):
            swapped = pltpu.roll(t, HEAD_DIM, 1)
            o_ref[0, col, rows, :] = jnp.where(upper, swapped, t).astype(BF16)
            o_ref[0, col + 1, rows, :] = jnp.where(upper, t, swapped).astype(BF16)


def _qkv(x, g, scale_a, shift_a, w_in, cos, sin):
    b, s, d = x.shape
    tm = 1024
    n_in = w_in.shape[1]
    return pl.pallas_call(
        _qkv_kernel,
        out_shape=jax.ShapeDtypeStruct((b, QKV_TILES, s, LANES), BF16),
        grid=(b, s // tm),
        in_specs=[pl.BlockSpec((1, tm, d), lambda bi, i: (bi, i, 0)),
                  pl.BlockSpec((1, d), lambda bi, i: (0, 0)),
                  pl.BlockSpec((1, 1, d), lambda bi, i: (bi, 0, 0)),
                  pl.BlockSpec((1, 1, d), lambda bi, i: (bi, 0, 0)),
                  pl.BlockSpec((d, n_in), lambda bi, i: (0, 0)),
                  pl.BlockSpec((tm, LANES), lambda bi, i: (i, 0)),
                  pl.BlockSpec((tm, LANES), lambda bi, i: (i, 0))],
        out_specs=pl.BlockSpec((1, QKV_TILES, tm, LANES), lambda bi, i: (bi, 0, i, 0)),
        compiler_params=_params("arbitrary", "arbitrary"),
        name="qkv",
    )(x, g.reshape(1, d), scale_a, shift_a, w_in, cos, sin)


NA_QROWS = 4
NA_KROWS = NA_QROWS + NA_KH
NA_BLOCKS_PER_STEP = 8
NA_UNROLL = 8
NA_Q = NA_QROWS * GRID_W
NA_K = NA_KROWS * GRID_W
NA_RPB_ROWS = 2 * NA_KH - 1
NA_RPB_COLS = 2 * NA_KW - 1


def _clamp(v, lo, hi):
    return min(max(v, lo), hi)


def _na_first_key_row(block, rows, clip):
    return clip(block * NA_QROWS - NA_KH // 2, 0, rows - NA_KROWS)


def _na_block_types(rows):
    def geometry(block):
        r = block * NA_QROWS
        a = _na_first_key_row(block, rows, _clamp)
        return (a - r,) + tuple(_clamp(r + j - NA_KH // 2, 0, rows - NA_KH) - r for j in range(NA_QROWS))

    n_blocks = rows // NA_QROWS
    interior = geometry(n_blocks // 2)
    lead = next(b for b in range(n_blocks) if geometry(b) == interior)
    trail = next(b for b in range(n_blocks) if geometry(n_blocks - 1 - b) == interior)
    assert all(geometry(b) == interior for b in range(lead, n_blocks - trail))
    return lead, trail


def _na_kernel(q_ref, k_ref, v_ref, bias_ref, o_ref, *, rows):
    lane = lax.broadcasted_iota(jnp.int32, (NA_Q, LANES), 1)
    upper = lane >= HEAD_DIM
    lead, trail = _na_block_types(rows)
    first_trailing = rows // NA_QROWS - trail

    def one_block(c):
        block = pl.program_id(2) * NA_BLOCKS_PER_STEP + c
        a = _na_first_key_row(block, rows, jnp.clip)
        ty = jnp.where(block < lead, block, jnp.where(block >= first_trailing, block - first_trailing + lead + 1, lead))
        start = pl.multiple_of(a * GRID_W, GRID_W)
        q_rows = pl.ds(pl.multiple_of(c * NA_Q, NA_Q), NA_Q)
        q = q_ref[0, 0, q_rows, :]
        ks = k_ref[0, 0, pl.ds(start, NA_K), :]
        vs = v_ref[0, 0, pl.ds(start, NA_K), :]
        key_upper = lax.broadcasted_iota(jnp.int32, vs.shape, 1) >= HEAD_DIM
        outs = []
        for hh in range(2):
            mine = upper if hh else ~upper
            qm = jnp.where(mine, q, jnp.zeros_like(q))
            s = lax.dot_general(qm, ks, (((1,), (1,)), ((), ())), preferred_element_type=F32)
            s = s.astype(BF16) + bias_ref[hh, ty]
            m = jnp.max(s, axis=-1, keepdims=True)
            e = jnp.exp2(s - m)
            v1 = jnp.where(key_upper if hh else ~key_upper, vs, jnp.ones_like(vs))
            o = jnp.dot(e, v1, preferred_element_type=F32)
            l = o[:, (1 - hh) * HEAD_DIM:(1 - hh) * HEAD_DIM + 1]
            outs.append(o / l)
        o_ref[0, 0, q_rows, :] = jnp.where(upper, outs[1], outs[0]).astype(BF16)

    def trip(it, carry):
        for u in range(NA_UNROLL):
            one_block(it * NA_UNROLL + u)
        return carry

    lax.fori_loop(0, NA_BLOCKS_PER_STEP // NA_UNROLL, trip, 0)


def _na_bias_kernel(rpb_ref, o_ref, *, rows):
    h = pl.program_id(0)
    q = lax.broadcasted_iota(jnp.int32, (GRID_W, LANES), 0)
    lane = lax.broadcasted_iota(jnp.int32, (GRID_W, LANES), 1)
    kc = lane % GRID_W
    upper = lane >= GRID_W
    c0 = jnp.clip(q - NA_KW // 2, 0, GRID_W - NA_KW)
    in_cols = (kc >= c0) & (kc < c0 + NA_KW)
    dc = kc - q + NA_KW - 1
    neg = jnp.full((GRID_W, LANES), NEG_INF, F32)
    base = h * NA_RPB_ROWS * NA_RPB_COLS
    by_row_offset = []
    for d in range(NA_RPB_ROWS):
        acc = neg
        for dd in range(NA_RPB_COLS):
            acc = jnp.where(dc == dd, rpb_ref[base + d * NA_RPB_COLS + dd], acc)
        by_row_offset.append(jnp.where(in_cols, acc * LOG2E, NEG_INF))
    lead, trail = _na_block_types(rows)
    n_blocks = rows // NA_QROWS
    type_blocks = list(range(lead + 1)) + list(range(n_blocks - trail, n_blocks))
    for ty, block in enumerate(type_blocks):
        r = block * NA_QROWS
        a = _na_first_key_row(block, rows, _clamp)
        for j in range(NA_QROWS):
            r0 = _clamp(r + j - NA_KH // 2, 0, rows - NA_KH)
            halves = []
            for i in range(NA_KROWS):
                in_rows = r0 <= a + i < r0 + NA_KH
                halves.append(by_row_offset[a + i - (r + j) + NA_KH - 1] if in_rows else neg)
            for t in range(NA_KROWS // 2):
                tile = jnp.where(upper, halves[2 * t + 1], halves[2 * t])
                o_ref[0, ty, j * GRID_W:(j + 1) * GRID_W, t * LANES:(t + 1) * LANES] = tile.astype(BF16)


def _na_bias(rpb, rows):
    n_types = sum(_na_block_types(rows)) + 1
    return pl.pallas_call(
        functools.partial(_na_bias_kernel, rows=rows),
        out_shape=jax.ShapeDtypeStruct((NA_HEADS, n_types, NA_Q, NA_K), BF16),
        grid=(NA_HEADS,),
        in_specs=[pl.BlockSpec(memory_space=pltpu.SMEM)],
        out_specs=pl.BlockSpec((1, n_types, NA_Q, NA_K), lambda h: (h, 0, 0, 0)),
        compiler_params=_params("arbitrary"),
        name="na_bias",
    )(rpb.astype(F32).reshape(-1))


def _na(qkv, rpb):
    b, _, s, _ = qkv.shape
    rows = s // GRID_W
    assert rows % (NA_QROWS * NA_BLOCKS_PER_STEP) == 0
    tq = NA_BLOCKS_PER_STEP * NA_Q
    bias = _na_bias(rpb, rows)
    return pl.pallas_call(
        functools.partial(_na_kernel, rows=rows),
        out_shape=jax.ShapeDtypeStruct((b, PAIRS, s, LANES), BF16),
        grid=(b, PAIRS, s // tq),
        in_specs=[pl.BlockSpec((1, 1, tq, LANES), lambda bi, p, i: (bi, COL_NQ + p, i, 0)),
                  pl.BlockSpec((1, 1, s, LANES), lambda bi, p, i: (bi, COL_NK + p, 0, 0)),
                  pl.BlockSpec((1, 1, s, LANES), lambda bi, p, i: (bi, COL_NV + p, 0, 0)),
                  pl.BlockSpec((2,) + bias.shape[1:], lambda bi, p, i: (p, 0, 0, 0))],
        out_specs=pl.BlockSpec((1, 1, tq, LANES), lambda bi, p, i: (bi, p, i, 0)),
        compiler_params=_params("arbitrary", "arbitrary", "arbitrary"),
        name="na",
    )(qkv, qkv, qkv, bias)


SWA_KEYS = 3 * SWA_BLOCK
SWA_BLOCKS_PER_STEP = 8
SWA_GROUP = SWA_HEADS // SWA_KV_HEADS
assert COL_SQ % (PAIRS // SWA_KV_HEADS) == 0


def _swa_kernel(sink_ref, q_ref, k_ref, v_ref, o_ref, *, seq):
    kv = pl.program_id(1)
    rows = SWA_GROUP * SWA_BLOCK
    lane = lax.broadcasted_iota(jnp.int32, (SWA_BLOCK, LANES), 1)
    upper = lane >= HEAD_DIM
    row = lax.broadcasted_iota(jnp.int32, (rows, SWA_KEYS), 0)
    q_off = row % SWA_BLOCK
    k_off = lax.broadcasted_iota(jnp.int32, (rows, SWA_KEYS), 1)
    head = lax.broadcasted_iota(jnp.int32, (rows, 1), 0) // SWA_BLOCK
    sink = jnp.zeros((rows, 1), F32)
    for g in range(SWA_GROUP):
        sink = jnp.where(head == g, sink_ref[kv * SWA_GROUP + g], sink)
    sink = sink * LOG2E
    for j in range(SWA_BLOCKS_PER_STEP):
        n = pl.program_id(2) * SWA_BLOCKS_PER_STEP + j
        start = pl.multiple_of(jnp.clip((n - 1) * SWA_BLOCK, 0, seq - SWA_KEYS), SWA_BLOCK)
        ks = k_ref[0, 0, pl.ds(start, SWA_KEYS), :]
        vs = v_ref[0, 0, pl.ds(start, SWA_KEYS), :]
        qs = []
        for g in range(SWA_GROUP):
            q = q_ref[0, g // 2, j * SWA_BLOCK:(j + 1) * SWA_BLOCK, :]
            qs.append(jnp.where(upper if g % 2 else ~upper, q, jnp.zeros_like(q)))
        s = lax.dot_general(jnp.concatenate(qs, axis=0), ks, (((1,), (1,)), ((), ())), preferred_element_type=F32)
        in_window = jnp.abs((start + k_off) - (n * SWA_BLOCK + q_off)) <= SWA_WINDOW
        s = jnp.where(in_window, s, NEG_INF)
        m = jnp.maximum(jnp.max(s, axis=-1, keepdims=True), sink)
        e = jnp.exp2(s - m)
        l = jnp.sum(e, axis=-1, keepdims=True) + jnp.exp2(sink - m)
        o = jnp.dot(e.astype(BF16), vs, preferred_element_type=F32) / l
        for pair in range(SWA_GROUP // 2):
            even = o[(2 * pair) * SWA_BLOCK:(2 * pair + 1) * SWA_BLOCK]
            odd = o[(2 * pair + 1) * SWA_BLOCK:(2 * pair + 2) * SWA_BLOCK]
            o_ref[0, pair, j * SWA_BLOCK:(j + 1) * SWA_BLOCK, :] = jnp.where(upper, odd, even).astype(BF16)


def _swa(qkv, sinks):
    b, _, s, _ = qkv.shape
    pairs_per_kv = PAIRS // SWA_KV_HEADS
    tq = SWA_BLOCKS_PER_STEP * SWA_BLOCK
    return pl.pallas_call(
        functools.partial(_swa_kernel, seq=s),
        out_shape=jax.ShapeDtypeStruct((b, PAIRS, s, LANES), BF16),
        grid=(b, SWA_KV_HEADS, s // tq),
        in_specs=[pl.BlockSpec(memory_space=pltpu.SMEM),
                  pl.BlockSpec((1, pairs_per_kv, tq, LANES),
                               lambda bi, kv, n: (bi, COL_SQ // pairs_per_kv + kv, n, 0)),
                  pl.BlockSpec((1, 1, s, LANES), lambda bi, kv, n: (bi, COL_SK + kv, 0, 0)),
                  pl.BlockSpec((1, 1, s, LANES), lambda bi, kv, n: (bi, COL_SV + kv, 0, 0))],
        out_specs=pl.BlockSpec((1, pairs_per_kv, tq, LANES), lambda bi, kv, n: (bi, kv, n, 0)),
        compiler_params=_params("arbitrary", "arbitrary", "arbitrary"),
        name="swa",
    )(sinks, qkv, qkv, qkv)


ROUTER_COLS = N_GROUPS + N_EXPERTS


def _split2(v):
    hi = v.astype(BF16)
    lo = (v - hi.astype(F32)).astype(BF16)
    return hi, lo


def _route(logits):
    lane = lax.broadcasted_iota(jnp.int32, logits.shape, 1)
    big = jnp.int32(LANES)
    gmask = lane < N_GROUPS
    gl = jnp.where(gmask, logits, NEG_INF)
    gmax = jnp.max(gl, axis=-1, keepdims=True)
    g_top = jnp.min(jnp.where(gmask & (gl == gmax), lane, big), axis=-1, keepdims=True)
    g_weight = 1.0 / jnp.sum(jnp.where(gmask, jnp.exp(gl - gmax), 0.0), axis=-1, keepdims=True)
    lo = ROUTER_BASE + g_top * EXPERTS_PER_GROUP
    emask = (lane >= lo) & (lane < lo + EXPERTS_PER_GROUP)
    el = jnp.where(emask, logits, NEG_INF)
    m1 = jnp.max(el, axis=-1, keepdims=True)
    i1 = jnp.min(jnp.where(emask & (el == m1), lane, big), axis=-1, keepdims=True)
    emask2 = emask & (lane != i1)
    el2 = jnp.where(emask2, logits, NEG_INF)
    m2 = jnp.max(el2, axis=-1, keepdims=True)
    i2 = jnp.min(jnp.where(emask2 & (el2 == m2), lane, big), axis=-1, keepdims=True)
    e2 = jnp.exp(m2 - m1)
    w1 = g_weight / (1.0 + e2)
    w2 = g_weight * e2 / (1.0 + e2)
    return i1 - ROUTER_BASE, i2 - ROUTER_BASE, w1, w2


TOKEN_ROWS = D_MODEL // (2 * LANES)
WORD = jnp.uint32


def _to_token_tiles(ref, v, base=0):
    t, d = v.shape
    words = pltpu.pack_elementwise([v[:, :d // 2], v[:, d // 2:]], packed_dtype=BF16)
    for s in range(TOKEN_ROWS):
        ref[pl.ds(base + s, t, stride=TOKEN_ROWS), :] = words[:, s * LANES:(s + 1) * LANES]


def _token_words(ref, t, base=0):
    return jnp.concatenate([ref[pl.ds(base + s, t, stride=TOKEN_ROWS), :] for s in range(TOKEN_ROWS)], axis=-1)


def _from_token_tiles(ref, t, base=0):
    words = _token_words(ref, t, base)
    halves = [pltpu.unpack_elementwise(words, index=j, packed_dtype=BF16, unpacked_dtype=F32) for j in range(2)]
    return jnp.concatenate(halves, axis=-1)


MIX_CHUNK = 256


def _mix_kernel(na_ref, sw_ref, x_ref, bna_ref, bsw_ref, wo_ref, gpm_ref, ga_ref, gpf_ref, scf_ref, shf_ref,
                wr_ref, x1_ref, h2_ref, r_ref, rt_ref, cntc_ref, cntr_ref):
    first_step = (pl.program_id(0) == 0) & (pl.program_id(1) == 0)

    @pl.when(first_step)
    def _():
        cntc_ref[...] = jnp.zeros_like(cntc_ref)
        cntr_ref[...] = jnp.zeros_like(cntr_ref)

    t = MIX_CHUNK
    for c in range(x_ref.shape[1] // t):
        rows = slice(c * t, (c + 1) * t)

        def heads(ref):
            return jnp.concatenate([ref[0, j, rows, :] for j in range(PAIRS)], axis=-1).astype(F32)

        na = (_rms(heads(na_ref)) * bna_ref[...]).astype(BF16)
        sw = (_rms(heads(sw_ref)) * bsw_ref[...]).astype(BF16)
        mix = (jnp.dot(na, wo_ref[:NA_WIDTH, :], preferred_element_type=F32)
               + jnp.dot(sw, wo_ref[NA_WIDTH:, :], preferred_element_type=F32))
        x1 = x_ref[0, rows, :] + ga_ref[0] * (_rms(mix) * gpm_ref[...])
        x1_ref[0, rows, :] = x1
        h2 = (_rms(x1) * gpf_ref[...]) * (1.0 + scf_ref[0]) + shf_ref[0]
        _to_token_tiles(h2_ref, h2, base=c * t * TOKEN_ROWS)
        h_hi, h_lo = _split2(h2)
        both = (jnp.dot(h_hi, wr_ref[...], preferred_element_type=F32)
                + jnp.dot(h_lo, wr_ref[...], preferred_element_type=F32))
        logits = both + pltpu.roll(both, LANES - ROUTER_COLS, 1)
        e1, e2, w1, w2 = _route(logits)
        lane = lax.broadcasted_iota(jnp.int32, logits.shape, 1)
        r = jnp.where(lane == R_E1, e1.astype(F32),
                      jnp.where(lane == R_E2, e2.astype(F32),
                                jnp.where(lane == R_W1, w1, jnp.where(lane == R_W2, w2, 0.0))))
        r_ref[rows, :] = r
        rt = jnp.transpose(r)[:SUBLANES]
        rt_ref[:, rows] = rt
        on_lane = ((lane == e1) | (lane == e2)).astype(F32)
        cntr_ref[...] += jnp.broadcast_to(jnp.sum(on_lane, axis=0, keepdims=True), cntr_ref.shape)
        sub = lax.broadcasted_iota(jnp.int32, (N_EXPERTS, t), 0).astype(F32)
        on_sub = ((sub == rt[R_E1:R_E1 + 1]) | (sub == rt[R_E2:R_E2 + 1])).astype(F32)
        cntc_ref[...] += jnp.broadcast_to(jnp.sum(on_sub, axis=1, keepdims=True), cntc_ref.shape)


def _mix(na, sw, x, beta_na, beta_swa, w_out, g_post_mix, gate_a, g_pre_ffn, scale_f, shift_f, w_router3):
    b, s, d = x.shape
    tm = 1024
    nt = s // tm
    row = lambda bi, i: (bi, i, 0)
    const2 = lambda bi, i: (0, 0)
    per_b = lambda bi, i: (bi, 0, 0)
    return pl.pallas_call(
        _mix_kernel,
        out_shape=(jax.ShapeDtypeStruct((b, s, d), F32),
                   jax.ShapeDtypeStruct((b * s * TOKEN_ROWS, LANES), WORD),
                   jax.ShapeDtypeStruct((b * s, LANES), F32),
                   jax.ShapeDtypeStruct((SUBLANES, b * s), F32),
                   jax.ShapeDtypeStruct((N_EXPERTS, LANES), F32),
                   jax.ShapeDtypeStruct((SUBLANES, LANES), F32)),
        grid=(b, s // tm),
        in_specs=[pl.BlockSpec((1, PAIRS, tm, LANES), lambda bi, i: (bi, 0, i, 0)),
                  pl.BlockSpec((1, PAIRS, tm, LANES), lambda bi, i: (bi, 0, i, 0)),
                  pl.BlockSpec((1, tm, d), row),
                  pl.BlockSpec((1, NA_WIDTH), const2),
                  pl.BlockSpec((1, SWA_WIDTH), const2),
                  pl.BlockSpec((NA_WIDTH + SWA_WIDTH, d), const2),
                  pl.BlockSpec((1, d), const2),
                  pl.BlockSpec((1, 1, d), per_b),
                  pl.BlockSpec((1, d), const2),
                  pl.BlockSpec((1, 1, d), per_b),
                  pl.BlockSpec((1, 1, d), per_b),
                  pl.BlockSpec((d, LANES), const2)],
        out_specs=(pl.BlockSpec((1, tm, d), row),
                   pl.BlockSpec((tm * TOKEN_ROWS, LANES), lambda bi, i: (bi * nt + i, 0)),
                   pl.BlockSpec((tm, LANES), lambda bi, i: (bi * nt + i, 0)),
                   pl.BlockSpec((SUBLANES, tm), lambda bi, i: (0, bi * nt + i)),
                   pl.BlockSpec((N_EXPERTS, LANES), const2),
                   pl.BlockSpec((SUBLANES, LANES), const2)),
        compiler_params=_params("arbitrary", "arbitrary"),
        name="mix",
    )(na, sw, x, beta_na.reshape(1, -1), beta_swa.reshape(1, -1), w_out, g_post_mix.reshape(1, d), gate_a,
      g_pre_ffn.reshape(1, d), scale_f, shift_f, w_router3)


MOE_TILE = 256
MOE_CHUNKS = 1
PLAN_T = 1024
I_TILE, I_EXPERT, I_LO, I_HI, I_FIRST, I_NEW, I_NEXT, I_ORDER = range(8)


def _plan_kernel(rt_ref, cntc_ref, cntr_ref, pos_ref, items_ref, start_ref, carry_ref, *, n_tiles, n_items):
    i = pl.program_id(0)
    sub = lax.broadcasted_iota(jnp.int32, (N_EXPERTS, LANES), 0)
    lane = lax.broadcasted_iota(jnp.int32, (N_EXPERTS, LANES), 1)

    @pl.when(i == 0)
    def _():
        c_col = cntc_ref[:, 0:1]
        c_row = cntr_ref[0:1, :]
        s_col = jnp.sum(jnp.where(lane < sub, c_row, 0.0), axis=1, keepdims=True)
        s_row = jnp.sum(jnp.where(sub < lane, c_col, 0.0), axis=0, keepdims=True)
        start_ref[...] = jnp.broadcast_to(s_col, start_ref.shape)
        carry_ref[...] = jnp.zeros_like(carry_ref)

        def tiles_of(s, c):
            first = jnp.floor(s * (1.0 / MOE_TILE))
            last = jnp.floor((s + c - 1.0) * (1.0 / MOE_TILE))
            return first, jnp.where(c > 0.0, last - first + 1.0, 0.0)

        f_col, n_col = tiles_of(s_col, c_col)
        _, n_row = tiles_of(s_row, c_row)
        i_col = jnp.sum(jnp.where(lane < sub, n_row, 0.0), axis=1, keepdims=True)
        total = jnp.sum(n_col, axis=0, keepdims=True)
        k = lax.broadcasted_iota(jnp.int32, (N_EXPERTS, n_items), 1).astype(F32)
        subk = lax.broadcasted_iota(jnp.int32, (N_EXPERTS, n_items), 0).astype(F32)
        ek = jnp.sum(jnp.where(i_col + n_col <= k, 1.0, 0.0), axis=0, keepdims=True)
        k0 = k[0:1]
        valid = k0 < total
        sel = subk == ek

        def pick(v):
            return jnp.sum(jnp.where(sel, v, 0.0), axis=0, keepdims=True)

        i_k, f_k, s_k, c_k = pick(i_col), pick(f_col), pick(s_col), pick(c_col)
        tile = f_k + (k0 - i_k)
        row0 = tile * MOE_TILE
        lo = jnp.maximum(s_k, row0) - row0
        hi = jnp.minimum(s_k + c_k, row0 + MOE_TILE) - row0
        present = n_col > 0.0
        last_expert = jnp.sum(jnp.where(i_col + n_col <= total - 1.0, 1.0, 0.0), axis=0, keepdims=True)
        nxt = jnp.min(jnp.where(present & (subk > ek), subk, float(N_EXPERTS)), axis=0, keepdims=True)
        order = jnp.sum(jnp.where(present & (subk < ek), 1.0, 0.0), axis=0, keepdims=True)
        rows = [jnp.where(valid, tile, n_tiles - 1.0), jnp.where(valid, ek, last_expert),
                jnp.where(valid, lo, 0.0), jnp.where(valid, hi, 0.0),
                jnp.where(valid & (lo == 0.0), 1.0, 0.0),
                jnp.where(valid & (k0 == i_k), 1.0, 0.0), jnp.where(valid, nxt, float(N_EXPERTS)),
                jnp.where(valid, order, 0.0)]
        assert len(rows) == SUBLANES
        items_ref[...] = jnp.concatenate(rows, axis=0).astype(jnp.int32)

    t = rt_ref.shape[1]
    e1 = rt_ref[R_E1:R_E1 + 1, :]
    e2 = rt_ref[R_E2:R_E2 + 1, :]
    sub_t = lax.broadcasted_iota(jnp.int32, (N_EXPERTS, t), 0).astype(F32)
    oh1 = sub_t == e1
    oh2 = sub_t == e2
    oh = (oh1 | oh2).astype(F32)
    before = (lax.broadcasted_iota(jnp.int32, (t, t), 0) < lax.broadcasted_iota(jnp.int32, (t, t), 1)).astype(BF16)
    rank = jnp.dot(oh.astype(BF16), before, preferred_element_type=F32)
    base = start_ref[:, 0:1] + carry_ref[:, 0:1] + rank
    pos1 = jnp.sum(jnp.where(oh1, base, 0.0), axis=0, keepdims=True)
    pos2 = jnp.sum(jnp.where(oh2, base, 0.0), axis=0, keepdims=True)
    carry_ref[...] += jnp.broadcast_to(jnp.sum(oh, axis=1, keepdims=True), carry_ref.shape)
    pos = jnp.concatenate([pos1, pos2] + [jnp.zeros_like(pos1)] * (SUBLANES - 2), axis=0)
    pos_ref[...] = pos.astype(jnp.int32)


def _plan(rt, cntc, cntr):
    n = rt.shape[1]
    n_tiles = 2 * n // MOE_TILE
    n_items = 2 * LANES
    assert n_tiles + N_EXPERTS <= n_items
    return pl.pallas_call(
        functools.partial(_plan_kernel, n_tiles=n_tiles, n_items=n_items),
        out_shape=(jax.ShapeDtypeStruct((SUBLANES, n), jnp.int32),
                   jax.ShapeDtypeStruct((SUBLANES, n_items), jnp.int32)),
        grid=(n // PLAN_T,),
        in_specs=[pl.BlockSpec((SUBLANES, PLAN_T), lambda i: (0, i)),
                  pl.BlockSpec((N_EXPERTS, LANES), lambda i: (0, 0)),
                  pl.BlockSpec((SUBLANES, LANES), lambda i: (0, 0))],
        out_specs=(pl.BlockSpec((SUBLANES, PLAN_T), lambda i: (0, i)),
                   pl.BlockSpec((SUBLANES, n_items), lambda i: (0, 0))),
        scratch_shapes=[pltpu.VMEM((N_EXPERTS, LANES), F32), pltpu.VMEM((N_EXPERTS, LANES), F32)],
        compiler_params=_params("arbitrary"),
        name="plan",
    )(rt, cntc, cntr)


DISPATCH_T = 1024


def _token_rows(ref, index):
    return ref.at[pl.ds(pl.multiple_of(index * TOKEN_ROWS, TOKEN_ROWS), TOKEN_ROWS), :]


def _dispatch_kernel(pos_ref, h_ref, xs_hbm, sem):
    def body(r, carry):
        src = _token_rows(h_ref, r)
        for k in range(2):
            pltpu.make_async_copy(src, _token_rows(xs_hbm, pos_ref[k, r]), sem.at[0]).start(priority=k)
        return carry

    lax.fori_loop(0, DISPATCH_T, body, 0, unroll=8)
    for k in range(2):
        pltpu.make_async_copy(h_ref, xs_hbm.at[pl.ds(0, DISPATCH_T * TOKEN_ROWS), :], sem.at[0]).wait()


def _dispatch(pos, h2):
    n = pos.shape[1]
    return pl.pallas_call(
        _dispatch_kernel,
        out_shape=jax.ShapeDtypeStruct((2 * n * TOKEN_ROWS, LANES), WORD),
        grid=(n // DISPATCH_T,),
        in_specs=[pl.BlockSpec((SUBLANES, DISPATCH_T), lambda i: (0, i), memory_space=pltpu.SMEM),
                  pl.BlockSpec((DISPATCH_T * TOKEN_ROWS, LANES), lambda i: (i, 0))],
        out_specs=pl.BlockSpec(memory_space=pl.ANY),
        scratch_shapes=[pltpu.SemaphoreType.DMA((1,))],
        compiler_params=_params("arbitrary"),
        name="dispatch",
    )(pos, h2)


def _expert_kernel(tile_ref, exp_ref, lo_ref, hi_ref, first_ref, new_ref, next_ref, order_ref,
                   xs_ref, wg_hbm, wu_hbm, wd_hbm, o_ref, wg_buf, wu_buf, wd_buf, sem):
    k = pl.program_id(0)
    lo = lo_ref[k]
    hi = hi_ref[k]
    slot = order_ref[k] % 2

    def weight_copies(expert, sl):
        return [pltpu.make_async_copy(src.at[expert], dst.at[sl], sem.at[sl])
                for src, dst in ((wg_hbm, wg_buf), (wu_hbm, wu_buf), (wd_hbm, wd_buf))]

    @pl.when(k == 0)
    def _():
        for copy in weight_copies(exp_ref[0], 0):
            copy.start()

    @pl.when(new_ref[k] == 1)
    def _():
        for copy in weight_copies(exp_ref[k], slot):
            copy.wait()

        @pl.when(next_ref[k] < N_EXPERTS)
        def _():
            for copy in weight_copies(next_ref[k], 1 - slot):
                copy.start()

    @pl.when(first_ref[k] == 1)
    def _():
        o_ref[...] = jnp.zeros_like(o_ref)

    @pl.when(hi > lo)
    def _():
        wg = wg_buf[slot].astype(BF16)
        wu = wu_buf[slot].astype(BF16)
        wd = wd_buf[slot].astype(BF16)
        t = MOE_TILE // MOE_CHUNKS
        for c in range(MOE_CHUNKS):
            base = c * t * TOKEN_ROWS
            x = _from_token_tiles(xs_ref, t, base=base).astype(BF16)
            gate = jnp.dot(x, wg, preferred_element_type=F32)
            up = jnp.dot(x, wu, preferred_element_type=F32)
            he = (gate * jax.nn.sigmoid(gate) * up).astype(BF16)
            ye = jnp.dot(he, wd, preferred_element_type=F32)
            row = c * t + lax.broadcasted_iota(jnp.int32, (t, 1), 0)
            mine = (row >= lo) & (row < hi)
            _to_token_tiles(o_ref, jnp.where(mine, ye, _from_token_tiles(o_ref, t, base=base)), base=base)


def _experts(items, xs, w_gate, w_up, w_down):
    n_rows = xs.shape[0] // TOKEN_ROWS
    n_items = n_rows // MOE_TILE + N_EXPERTS
    d = w_gate.shape[1]
    tile_map = lambda k, tile, *_: (tile[k], 0)
    grid_spec = pltpu.PrefetchScalarGridSpec(
        num_scalar_prefetch=SUBLANES,
        grid=(n_items,),
        in_specs=[pl.BlockSpec((MOE_TILE * TOKEN_ROWS, LANES), tile_map),
                  pl.BlockSpec(memory_space=pl.ANY),
                  pl.BlockSpec(memory_space=pl.ANY),
                  pl.BlockSpec(memory_space=pl.ANY)],
        out_specs=pl.BlockSpec((MOE_TILE * TOKEN_ROWS, LANES), tile_map),
        scratch_shapes=[pltpu.VMEM((2, d, EXPERT_FF), F32), pltpu.VMEM((2, d, EXPERT_FF), F32),
                        pltpu.VMEM((2, EXPERT_FF, d), F32), pltpu.SemaphoreType.DMA((2,))])
    return pl.pallas_call(
        _expert_kernel,
        out_shape=jax.ShapeDtypeStruct(xs.shape, WORD),
        grid_spec=grid_spec,
        compiler_params=_params("arbitrary"),
        name="experts",
    )(*(items[j, :n_items] for j in range(SUBLANES)), xs, w_gate, w_up, w_down)


COMBINE_T = 256


def _combine_kernel(pos_ref, posn_ref, ys_hbm, r_ref, x1_ref, gf_ref, gpost_ref, o_ref, buf, sem):
    i = pl.program_id(0)
    n = pl.num_programs(0)
    slot = i % 2
    slot_rows = COMBINE_T * TOKEN_ROWS

    def start_row(p_ref, sl, r):
        for k in range(2):
            dst = buf.at[sl, pl.ds(pl.multiple_of(k * slot_rows + r * TOKEN_ROWS, TOKEN_ROWS), TOKEN_ROWS), :]
            pltpu.make_async_copy(_token_rows(ys_hbm, p_ref[k, r]), dst, sem.at[sl]).start(priority=k)

    def wait_slot(sl):
        pltpu.make_async_copy(ys_hbm.at[pl.ds(0, 2 * slot_rows), :], buf.at[sl], sem.at[sl]).wait()

    def issue(p_ref, sl):
        def body(r, carry):
            start_row(p_ref, sl, r)
            return carry

        lax.fori_loop(0, COMBINE_T, body, 0, unroll=8)

    @pl.when(i == 0)
    def _():
        issue(pos_ref, 0)

    @pl.when(i + 1 < n)
    def _():
        issue(posn_ref, 1 - slot)

    wait_slot(slot)
    ya = _from_token_tiles(buf.at[slot], COMBINE_T)
    yb = _from_token_tiles(buf.at[slot], COMBINE_T, base=slot_rows)
    r = r_ref[...]
    y = r[:, R_W1:R_W1 + 1] * ya + r[:, R_W2:R_W2 + 1] * yb
    o_ref[0] = x1_ref[0] + gf_ref[0] * (_rms(y) * gpost_ref[...])


def _combine(pos, ys, r, x1, gate_f, g_post_ffn):
    b, s, d = x1.shape
    nt = s // COMBINE_T
    n_steps = b * nt
    row = lambda i: (i // nt, i % nt, 0)
    pos_spec = lambda f: pl.BlockSpec((SUBLANES, COMBINE_T), lambda i: (0, f(i)), memory_space=pltpu.SMEM)
    return pl.pallas_call(
        _combine_kernel,
        out_shape=jax.ShapeDtypeStruct((b, s, d), F32),
        grid=(n_steps,),
        in_specs=[pos_spec(lambda i: i),
                  pos_spec(lambda i: jnp.minimum(i + 1, n_steps - 1)),
                  pl.BlockSpec(memory_space=pl.ANY),
                  pl.BlockSpec((COMBINE_T, LANES), lambda i: (i, 0)),
                  pl.BlockSpec((1, COMBINE_T, d), row),
                  pl.BlockSpec((1, 1, d), lambda i: (i // nt, 0, 0)),
                  pl.BlockSpec((1, d), lambda i: (0, 0))],
        out_specs=pl.BlockSpec((1, COMBINE_T, d), row),
        scratch_shapes=[pltpu.VMEM((2, 2 * COMBINE_T * TOKEN_ROWS, LANES), WORD), pltpu.SemaphoreType.DMA((2,))],
        compiler_params=_params("arbitrary"),
        name="combine",
    )(pos, pos, ys, r, x1, gate_f, g_post_ffn.reshape(1, d))


def _rope_tables(s):
    half = HEAD_DIM // 2
    inv = ROPE_THETA ** (-np.arange(half, dtype=np.float64) * 2.0 / HEAD_DIM)
    ang = np.arange(s, dtype=np.float64)[:, None] * inv[None, :]
    cos = np.cos(ang)
    sin = np.sin(ang)
    cos = np.concatenate([cos, cos, cos, cos], axis=-1)
    sin_signed = np.concatenate([-sin, sin, -sin, sin], axis=-1)
    return jnp.asarray(cos, F32), jnp.asarray(sin_signed, F32)


def _router_weights(w_group_router, w_expert_router):
    d = w_group_router.shape[0]
    we = jnp.transpose(w_expert_router, (1, 0, 2)).reshape(d, N_EXPERTS)
    hi, lo = _split2(jnp.concatenate([w_group_router, we], axis=-1))
    return jnp.concatenate([hi, lo, jnp.zeros((d, LANES - 2 * ROUTER_COLS), BF16)], axis=-1)


def kernel(x, c, w_ada, b_ada, g_pre_mix, w_in, na_rpb, swa_sinks, beta_na, beta_swa, w_out, g_post_mix, g_pre_ffn,
           w_group_router, w_expert_router, w_gate, w_up, w_down, g_post_ffn):
    b, s, d = x.shape
    depth = w_ada.shape[0]
    cos, sin_signed = _rope_tables(s)
    for l in range(depth):
        mod = _adaln(c, w_ada[l], b_ada[l]).reshape(b, N_MOD, 1, d)
        shift_a, scale_a, gate_a, shift_f, scale_f, gate_f = (mod[:, k] for k in range(N_MOD))
        qkv = _qkv(x, g_pre_mix[l], scale_a, shift_a, w_in[l].astype(BF16), cos, sin_signed)
        na = _na(qkv, na_rpb[l])
        sw = _swa(qkv, swa_sinks[l])
        x1, h2, r, rt, cntc, cntr = _mix(na, sw, x, beta_na[l], beta_swa[l], w_out[l].astype(BF16), g_post_mix[l],
                                          gate_a, g_pre_ffn[l], scale_f, shift_f,
                                          _router_weights(w_group_router[l], w_expert_router[l]))
        pos, items = _plan(rt, cntc, cntr)
        xs = _dispatch(pos, h2)
        ys = _experts(items, xs, w_gate[l], w_up[l], w_down[l])
        x = _combine(pos, ys, r, x1, gate_f, g_post_ffn[l])
    return x
```

```python
import functools

import jax
import jax.numpy as jnp
import numpy as np
from jax import lax
from jax.experimental import pallas as pl
from jax.experimental.pallas import tpu as pltpu

D_MODEL = 1024
GRID_W = 64
HEAD_DIM = 64
NA_HEADS = 8
NA_KH = 8
NA_KW = 16
SWA_HEADS = 8
SWA_KV_HEADS = 2
SWA_WINDOW = 128
SWA_BLOCK = 128
ROPE_THETA = 10000.0
NA_WIDTH = NA_HEADS * HEAD_DIM
SWA_WIDTH = SWA_HEADS * HEAD_DIM
N_GROUPS = 4
EXPERTS_PER_GROUP = 8
N_EXPERTS = N_GROUPS * EXPERTS_PER_GROUP
EXPERT_FF = 256
N_MOD = 6
EPS = 1e-6
NEG_INF = -1e30

LANES = 128
SUBLANES = 8
R_E1, R_E2, R_W1, R_W2 = range(4)
PAIRS = NA_HEADS // 2
W_NQ, W_NK, W_NV, W_SQ, W_SKV = 0, 4, 8, 12, 16
COL_NQ, COL_SQ, COL_SK, COL_SV = 0, 4, 8, 10
QKV_TILES = 12
NA_GROUPS = GRID_W // NA_KW
NA_WIN = 2 * NA_KW
NA_WIN_START = tuple(min(max(NA_KW * g - NA_KW // 2, 0), GRID_W - NA_WIN) for g in range(NA_GROUPS))
ROUTER_BASE = N_GROUPS
VMEM_LIMIT = 56 * 1024 * 1024

F32 = jnp.float32
BF16 = jnp.bfloat16
LOG2E = 1.4426950408889634


def _rms(v):
    return v * lax.rsqrt(jnp.mean(v * v, axis=-1, keepdims=True) + EPS)


def _params(*sem):
    return pltpu.CompilerParams(dimension_semantics=sem, vmem_limit_bytes=VMEM_LIMIT)


def _adaln_kernel(c_ref, w_ref, b_ref, o_ref, *, batch):
    c = c_ref[...]
    a_t = jnp.transpose(c * jax.nn.sigmoid(c))
    w = w_ref[...]
    rows = [jnp.sum(a_t[:, bi:bi + 1] * w, axis=0, keepdims=True) for bi in range(batch)]
    rows.append(jnp.zeros((c.shape[0] - batch, w.shape[1]), F32))
    o_ref[...] = jnp.concatenate(rows, axis=0) + b_ref[...]


def _adaln(c, w_ada, b_ada):
    batch, d = c.shape
    n = w_ada.shape[1]
    tn = 1024
    b = SUBLANES
    assert batch < b
    c = jnp.pad(c, ((0, b - batch), (0, 0)))
    return pl.pallas_call(
        functools.partial(_adaln_kernel, batch=batch),
        out_shape=jax.ShapeDtypeStruct((b, n), F32),
        grid=(n // tn,),
        in_specs=[pl.BlockSpec((b, d), lambda j: (0, 0)),
                  pl.BlockSpec((d, tn), lambda j: (0, j)),
                  pl.BlockSpec((1, tn), lambda j: (0, j))],
        out_specs=pl.BlockSpec((b, tn), lambda j: (0, j)),
        compiler_params=_params("arbitrary"),
        name="adaln",
    )(c, w_ada, b_ada.reshape(1, n))[:batch]


def _rope(v, cos, sin_signed, first_half):
    rot = jnp.where(first_half, pltpu.roll(v, LANES - HEAD_DIM // 2, 1), pltpu.roll(v, HEAD_DIM // 2, 1))
    return v * cos + rot * sin_signed


QKV_CHUNK = 256


def _column_windows(v):
    grid_rows = v.shape[0] // GRID_W
    return [jnp.concatenate([v[r * GRID_W + w0:r * GRID_W + w0 + NA_WIN] for r in range(grid_rows)], axis=0)
            for w0 in NA_WIN_START]


def _qkv_kernel(x_ref, g_ref, sc_ref, sh_ref, w_ref, cos_ref, sin_ref, o_ref, kg_ref, vg_ref):
    scale = HEAD_DIM ** -0.5 * LOG2E
    lane = lax.broadcasted_iota(jnp.int32, (QKV_CHUNK, LANES), 1)
    first_half = (lane % HEAD_DIM) < HEAD_DIM // 2
    upper = lane >= HEAD_DIM

    def tile(v, j):
        return v[:, j * LANES:(j + 1) * LANES]

    for c in range(x_ref.shape[1] // QKV_CHUNK):
        rows = slice(c * QKV_CHUNK, (c + 1) * QKV_CHUNK)
        h = (_rms(x_ref[0, rows, :]) * g_ref[...]) * (1.0 + sc_ref[0]) + sh_ref[0]
        h = h.astype(BF16)
        cos = cos_ref[rows, :]
        sin = sin_ref[rows, :]

        def proj(col, width):
            return jnp.dot(h, w_ref[:, col * LANES:(col + width) * LANES], preferred_element_type=F32)

        nq, nk, nv, sq = proj(W_NQ, 4), proj(W_NK, 4), proj(W_NV, 4), proj(W_SQ, 4)
        win_rows = slice(c * QKV_CHUNK // 2, (c + 1) * QKV_CHUNK // 2)
        for j in range(PAIRS):
            o_ref[0, COL_NQ + j, rows, :] = (tile(nq, j) * scale).astype(BF16)
            o_ref[0, COL_SQ + j, rows, :] = (_rope(tile(sq, j), cos, sin, first_half) * scale).astype(BF16)
            for ref, val in ((kg_ref, tile(nk, j)), (vg_ref, tile(nv, j))):
                for g, win in enumerate(_column_windows(val)):
                    ref[0, j, g, win_rows, :] = win.astype(BF16)
        skv = proj(W_SKV, 2)
        k = _rope(tile(skv, 0), cos, sin, first_half)
        v = tile(skv, 1)
        for t, col in ((k, COL_SK), (v, COL_SV)):
            swapped = pltpu.roll(t, HEAD_DIM, 1)
            o_ref[0, col, rows, :] = jnp.where(upper, swapped, t).astype(BF16)
            o_ref[0, col + 1, rows, :] = jnp.where(upper, t, swapped).astype(BF16)


def _qkv(x, g, scale_a, shift_a, w_in, cos, sin):
    b, s, d = x.shape
    tm = 1024
    n_in = w_in.shape[1]
    windows = jax.ShapeDtypeStruct((b, PAIRS, NA_GROUPS, s // 2, LANES), BF16)
    windows_spec = pl.BlockSpec((1, PAIRS, NA_GROUPS, tm // 2, LANES), lambda bi, i: (bi, 0, 0, i, 0))
    return pl.pallas_call(
        _qkv_kernel,
        out_shape=(jax.ShapeDtypeStruct((b, QKV_TILES, s, LANES), BF16), windows, windows),
        grid=(b, s // tm),
        in_specs=[pl.BlockSpec((1, tm, d), lambda bi, i: (bi, i, 0)),
                  pl.BlockSpec((1, d), lambda bi, i: (0, 0)),
                  pl.BlockSpec((1, 1, d), lambda bi, i: (bi, 0, 0)),
                  pl.BlockSpec((1, 1, d), lambda bi, i: (bi, 0, 0)),
                  pl.BlockSpec((d, n_in), lambda bi, i: (0, 0)),
                  pl.BlockSpec((tm, LANES), lambda bi, i: (i, 0)),
                  pl.BlockSpec((tm, LANES), lambda bi, i: (i, 0))],
        out_specs=(pl.BlockSpec((1, QKV_TILES, tm, LANES), lambda bi, i: (bi, 0, i, 0)), windows_spec, windows_spec),
        compiler_params=_params("arbitrary", "arbitrary"),
        name="qkv",
    )(x, g.reshape(1, d), scale_a, shift_a, w_in, cos, sin)


NA_QROWS = 8
NA_KROWS = NA_QROWS + NA_KH
NA_BLOCKS_PER_STEP = 4
NA_Q = NA_QROWS * NA_KW
NA_K = NA_KROWS * NA_WIN
NA_RPB_ROWS = 2 * NA_KH - 1
NA_RPB_COLS = 2 * NA_KW - 1


def _clamp(v, lo, hi):
    return min(max(v, lo), hi)


def _na_first_key_row(block, rows, clip):
    return clip(block * NA_QROWS - NA_KH // 2, 0, rows - NA_KROWS)


def _na_block_types(rows):
    def geometry(block):
        r = block * NA_QROWS
        a = _na_first_key_row(block, rows, _clamp)
        return (a - r,) + tuple(_clamp(r + j - NA_KH // 2, 0, rows - NA_KH) - r for j in range(NA_QROWS))

    n_blocks = rows // NA_QROWS
    interior = geometry(n_blocks // 2)
    lead = next(b for b in range(n_blocks) if geometry(b) == interior)
    trail = next(b for b in range(n_blocks) if geometry(n_blocks - 1 - b) == interior)
    assert all(geometry(b) == interior for b in range(lead, n_blocks - trail))
    return lead, trail


def _na_kernel(q_ref, k_ref, v_ref, bias_ref, o_ref, *, rows):
    upper = lax.broadcasted_iota(jnp.int32, (NA_Q, LANES), 1) >= HEAD_DIM
    key_upper = lax.broadcasted_iota(jnp.int32, (NA_K, LANES), 1) >= HEAD_DIM
    lead, trail = _na_block_types(rows)
    first_trailing = rows // NA_QROWS - trail
    n_types = lead + trail + 1
    for c in range(NA_BLOCKS_PER_STEP):
        block = pl.program_id(2) * NA_BLOCKS_PER_STEP + c
        a = _na_first_key_row(block, rows, jnp.clip)
        ty = jnp.where(block < lead, block, jnp.where(block >= first_trailing, block - first_trailing + lead + 1, lead))
        start = pl.multiple_of(a * NA_WIN, NA_WIN)
        pieces = [[slice((c * NA_QROWS + rr) * GRID_W + g * NA_KW, (c * NA_QROWS + rr) * GRID_W + (g + 1) * NA_KW)
                   for rr in range(NA_QROWS)] for g in range(NA_GROUPS)]
        chains = [(g, hh) for g in range(NA_GROUPS) for hh in range(2)]
        scores = {}
        for g, hh in chains:
            q = jnp.concatenate([q_ref[0, 0, rws, :] for rws in pieces[g]], axis=0)
            qm = jnp.where(upper if hh else ~upper, q, jnp.zeros_like(q))
            ks = k_ref[0, 0, g, pl.ds(start, NA_K), :]
            s = lax.dot_general(qm, ks, (((1,), (1,)), ((), ())), preferred_element_type=F32)
            scores[g, hh] = s + bias_ref[hh, g * n_types + ty]
        probs = {}
        for g, hh in chains:
            s = scores[g, hh]
            probs[g, hh] = jnp.exp2(s - jnp.max(s, axis=-1, keepdims=True)).astype(BF16)
        outs = {}
        for g, hh in chains:
            vs = v_ref[0, 0, g, pl.ds(start, NA_K), :]
            v1 = jnp.where(key_upper if hh else ~key_upper, vs, jnp.ones_like(vs))
            o = jnp.dot(probs[g, hh], v1, preferred_element_type=F32)
            outs[g, hh] = o / o[:, (1 - hh) * HEAD_DIM:(1 - hh) * HEAD_DIM + 1]
        for g in range(NA_GROUPS):
            out = jnp.where(upper, outs[g, 1], outs[g, 0]).astype(BF16)
            for rr, rws in enumerate(pieces[g]):
                o_ref[0, 0, rws, :] = out[rr * NA_KW:(rr + 1) * NA_KW]


def _na_bias_kernel(rpb_ref, o_ref, *, rows):
    h = pl.program_id(0)
    cc = lax.broadcasted_iota(jnp.int32, (NA_KW, LANES), 0)
    lane = lax.broadcasted_iota(jnp.int32, (NA_KW, LANES), 1)
    w = lane % NA_WIN
    key_row_in_tile = lane // NA_WIN
    rows_per_tile = LANES // NA_WIN
    neg = jnp.full((NA_KW, LANES), NEG_INF, F32)
    base = h * NA_RPB_ROWS * NA_RPB_COLS
    lead, trail = _na_block_types(rows)
    n_blocks = rows // NA_QROWS
    type_blocks = list(range(lead + 1)) + list(range(n_blocks - trail, n_blocks))
    for g in range(NA_GROUPS):
        qc = g * NA_KW + cc
        kc = NA_WIN_START[g] + w
        c0 = jnp.clip(qc - NA_KW // 2, 0, GRID_W - NA_KW)
        in_cols = (kc >= c0) & (kc < c0 + NA_KW)
        dc = kc - qc + NA_KW - 1
        by_row_offset = []
        for d in range(NA_RPB_ROWS):
            acc = neg
            for dd in range(NA_RPB_COLS):
                acc = jnp.where(dc == dd, rpb_ref[base + d * NA_RPB_COLS + dd], acc)
            by_row_offset.append(jnp.where(in_cols, acc * LOG2E, NEG_INF))
        for ty, block in enumerate(type_blocks):
            r = block * NA_QROWS
            a = _na_first_key_row(block, rows, _clamp)
            for j in range(NA_QROWS):
                r0 = _clamp(r + j - NA_KH // 2, 0, rows - NA_KH)
                for t in range(NA_K // LANES):
                    tile = neg
                    for part in range(rows_per_tile):
                        i = t * rows_per_tile + part
                        if r0 <= a + i < r0 + NA_KH:
                            tile = jnp.where(key_row_in_tile == part, by_row_offset[a + i - (r + j) + NA_KH - 1], tile)
                    o_ref[0, g * len(type_blocks) + ty, j * NA_KW:(j + 1) * NA_KW, t * LANES:(t + 1) * LANES] = tile


def _na_bias(rpb, rows):
    n_tables = NA_GROUPS * (sum(_na_block_types(rows)) + 1)
    return pl.pallas_call(
        functools.partial(_na_bias_kernel, rows=rows),
        out_shape=jax.ShapeDtypeStruct((NA_HEADS, n_tables, NA_Q, NA_K), F32),
        grid=(NA_HEADS,),
        in_specs=[pl.BlockSpec(memory_space=pltpu.SMEM)],
        out_specs=pl.BlockSpec((1, n_tables, NA_Q, NA_K), lambda h: (h, 0, 0, 0)),
        compiler_params=_params("arbitrary"),
        name="na_bias",
    )(rpb.astype(F32).reshape(-1))


def _na(qkv, kg, vg, rpb):
    b, _, s, _ = qkv.shape
    rows = s // GRID_W
    assert rows % (NA_QROWS * NA_BLOCKS_PER_STEP) == 0
    tq = NA_BLOCKS_PER_STEP * NA_QROWS * GRID_W
    bias = _na_bias(rpb, rows)
    windows_spec = pl.BlockSpec((1, 1) + kg.shape[2:], lambda bi, p, i: (bi, p, 0, 0, 0))
    return pl.pallas_call(
        functools.partial(_na_kernel, rows=rows),
        out_shape=jax.ShapeDtypeStruct((b, PAIRS, s, LANES), BF16),
        grid=(b, PAIRS, s // tq),
        in_specs=[pl.BlockSpec((1, 1, tq, LANES), lambda bi, p, i: (bi, COL_NQ + p, i, 0)),
                  windows_spec,
                  windows_spec,
                  pl.BlockSpec((2,) + bias.shape[1:], lambda bi, p, i: (p, 0, 0, 0))],
        out_specs=pl.BlockSpec((1, 1, tq, LANES), lambda bi, p, i: (bi, p, i, 0)),
        compiler_params=_params("arbitrary", "arbitrary", "arbitrary"),
        name="na",
    )(qkv, kg, vg, bias)


SWA_KEYS = 3 * SWA_BLOCK
SWA_BLOCKS_PER_STEP = 8
SWA_INTERLEAVE = 2
SWA_GROUP = SWA_HEADS // SWA_KV_HEADS
assert COL_SQ % (PAIRS // SWA_KV_HEADS) == 0


def _swa_masks():
    v = np.arange(SWA_KEYS // SWA_BLOCK)[:, None, None]
    q = np.arange(SWA_BLOCK)[None, :, None]
    k = np.arange(SWA_KEYS)[None, None, :]
    return np.where(np.abs(k - v * SWA_BLOCK - q) <= SWA_WINDOW, 0.0, NEG_INF).astype(np.float32)


def _swa_kernel(sink_ref, mask_ref, q_ref, k_ref, v_ref, o_ref, *, seq):
    kv = pl.program_id(1)
    rows = SWA_GROUP * SWA_BLOCK
    lane = lax.broadcasted_iota(jnp.int32, (SWA_BLOCK, LANES), 1)
    upper = lane >= HEAD_DIM
    head = lax.broadcasted_iota(jnp.int32, (rows, 1), 0) // SWA_BLOCK
    sink = jnp.zeros((rows, 1), F32)
    for g in range(SWA_GROUP):
        sink = jnp.where(head == g, sink_ref[kv * SWA_GROUP + g], sink)
    sink = sink * LOG2E
    for j0 in range(0, SWA_BLOCKS_PER_STEP, SWA_INTERLEAVE):
        blocks = range(j0, j0 + SWA_INTERLEAVE)
        starts, scores, probs = {}, {}, {}
        for j in blocks:
            n = pl.program_id(2) * SWA_BLOCKS_PER_STEP + j
            start = pl.multiple_of(jnp.clip((n - 1) * SWA_BLOCK, 0, seq - SWA_KEYS), SWA_BLOCK)
            starts[j] = start
            ks = k_ref[0, 0, pl.ds(start, SWA_KEYS), :]
            qs = []
            for g in range(SWA_GROUP):
                q = q_ref[0, g // 2, j * SWA_BLOCK:(j + 1) * SWA_BLOCK, :]
                qs.append(jnp.where(upper if g % 2 else ~upper, q, jnp.zeros_like(q)))
            s = lax.dot_general(jnp.concatenate(qs, axis=0), ks, (((1,), (1,)), ((), ())), preferred_element_type=F32)
            mask = mask_ref[(n * SWA_BLOCK - start) // SWA_BLOCK]
            scores[j] = s + jnp.concatenate([mask] * SWA_GROUP, axis=0)
        for j in blocks:
            s = scores[j]
            m = jnp.maximum(jnp.max(s, axis=-1, keepdims=True), sink)
            e = jnp.exp2(s - m)
            probs[j] = (e.astype(BF16), jnp.sum(e, axis=-1, keepdims=True) + jnp.exp2(sink - m))
        for j in blocks:
            e, l = probs[j]
            vs = v_ref[0, 0, pl.ds(starts[j], SWA_KEYS), :]
            o = jnp.dot(e, vs, preferred_element_type=F32) / l
            for pair in range(SWA_GROUP // 2):
                even = o[(2 * pair) * SWA_BLOCK:(2 * pair + 1) * SWA_BLOCK]
                odd = o[(2 * pair + 1) * SWA_BLOCK:(2 * pair + 2) * SWA_BLOCK]
                o_ref[0, pair, j * SWA_BLOCK:(j + 1) * SWA_BLOCK, :] = jnp.where(upper, odd, even).astype(BF16)


def _swa(qkv, sinks):
    b, _, s, _ = qkv.shape
    pairs_per_kv = PAIRS // SWA_KV_HEADS
    tq = SWA_BLOCKS_PER_STEP * SWA_BLOCK
    masks = _swa_masks()
    return pl.pallas_call(
        functools.partial(_swa_kernel, seq=s),
        out_shape=jax.ShapeDtypeStruct((b, PAIRS, s, LANES), BF16),
        grid=(b, SWA_KV_HEADS, s // tq),
        in_specs=[pl.BlockSpec(memory_space=pltpu.SMEM),
                  pl.BlockSpec(masks.shape, lambda bi, kv, n: (0, 0, 0)),
                  pl.BlockSpec((1, pairs_per_kv, tq, LANES),
                               lambda bi, kv, n: (bi, COL_SQ // pairs_per_kv + kv, n, 0)),
                  pl.BlockSpec((1, 1, s, LANES), lambda bi, kv, n: (bi, COL_SK + kv, 0, 0)),
                  pl.BlockSpec((1, 1, s, LANES), lambda bi, kv, n: (bi, COL_SV + kv, 0, 0))],
        out_specs=pl.BlockSpec((1, pairs_per_kv, tq, LANES), lambda bi, kv, n: (bi, kv, n, 0)),
        compiler_params=_params("arbitrary", "arbitrary", "arbitrary"),
        name="swa",
    )(sinks, jnp.asarray(masks), qkv, qkv, qkv)


ROUTER_COLS = N_GROUPS + N_EXPERTS


def _split2(v):
    hi = v.astype(BF16)
    lo = (v - hi.astype(F32)).astype(BF16)
    return hi, lo


def _route(logits):
    lane = lax.broadcasted_iota(jnp.int32, logits.shape, 1)
    big = jnp.int32(LANES)
    gmask = lane < N_GROUPS
    gl = jnp.where(gmask, logits, NEG_INF)
    gmax = jnp.max(gl, axis=-1, keepdims=True)
    g_top = jnp.min(jnp.where(gmask & (gl == gmax), lane, big), axis=-1, keepdims=True)
    g_weight = 1.0 / jnp.sum(jnp.where(gmask, jnp.exp(gl - gmax), 0.0), axis=-1, keepdims=True)
    lo = ROUTER_BASE + g_top * EXPERTS_PER_GROUP
    emask = (lane >= lo) & (lane < lo + EXPERTS_PER_GROUP)
    el = jnp.where(emask, logits, NEG_INF)
    m1 = jnp.max(el, axis=-1, keepdims=True)
    i1 = jnp.min(jnp.where(emask & (el == m1), lane, big), axis=-1, keepdims=True)
    emask2 = emask & (lane != i1)
    el2 = jnp.where(emask2, logits, NEG_INF)
    m2 = jnp.max(el2, axis=-1, keepdims=True)
    i2 = jnp.min(jnp.where(emask2 & (el2 == m2), lane, big), axis=-1, keepdims=True)
    e2 = jnp.exp(m2 - m1)
    w1 = g_weight / (1.0 + e2)
    w2 = g_weight * e2 / (1.0 + e2)
    return i1 - ROUTER_BASE, i2 - ROUTER_BASE, w1, w2


TOKEN_ROWS = D_MODEL // (2 * LANES)
WORD = jnp.uint32


def _to_token_tiles(ref, v, base=0):
    t, d = v.shape
    words = pltpu.pack_elementwise([v[:, :d // 2], v[:, d // 2:]], packed_dtype=BF16)
    for s in range(TOKEN_ROWS):
        ref[pl.ds(base + s, t, stride=TOKEN_ROWS), :] = words[:, s * LANES:(s + 1) * LANES]


def _token_words(ref, t, base=0):
    return jnp.concatenate([ref[pl.ds(base + s, t, stride=TOKEN_ROWS), :] for s in range(TOKEN_ROWS)], axis=-1)


def _from_token_tiles(ref, t, base=0):
    words = _token_words(ref, t, base)
    halves = [pltpu.unpack_elementwise(words, index=j, packed_dtype=BF16, unpacked_dtype=F32) for j in range(2)]
    return jnp.concatenate(halves, axis=-1)


MIX_CHUNK = 256


def _mix_kernel(na_ref, sw_ref, x_ref, bna_ref, bsw_ref, wo_ref, gpm_ref, ga_ref, gpf_ref, scf_ref, shf_ref,
                wr_ref, x1_ref, h2_ref, r_ref, rt_ref, cntc_ref, cntr_ref):
    first_step = (pl.program_id(0) == 0) & (pl.program_id(1) == 0)

    @pl.when(first_step)
    def _():
        cntc_ref[...] = jnp.zeros_like(cntc_ref)
        cntr_ref[...] = jnp.zeros_like(cntr_ref)

    t = MIX_CHUNK
    chunks = range(x_ref.shape[1] // t)
    row_slices = [slice(c * t, (c + 1) * t) for c in chunks]
    mixes, h2s, all_logits = [], [], []
    for rows in row_slices:
        def heads(ref):
            return jnp.concatenate([ref[0, j, rows, :] for j in range(PAIRS)], axis=-1).astype(F32)

        na = (_rms(heads(na_ref)) * bna_ref[...]).astype(BF16)
        sw = (_rms(heads(sw_ref)) * bsw_ref[...]).astype(BF16)
        mixes.append(jnp.dot(na, wo_ref[:NA_WIDTH, :], preferred_element_type=F32)
                     + jnp.dot(sw, wo_ref[NA_WIDTH:, :], preferred_element_type=F32))
    gate_gain = ga_ref[0] * gpm_ref[...]
    ffn_gain = gpf_ref[...] * (1.0 + scf_ref[0])
    for c, rows in zip(chunks, row_slices):
        x1 = x_ref[0, rows, :] + _rms(mixes[c]) * gate_gain
        x1_ref[0, rows, :] = x1
        h2 = _rms(x1) * ffn_gain + shf_ref[0]
        _to_token_tiles(h2_ref, h2, base=c * t * TOKEN_ROWS)
        h2s.append(h2)
    for c in chunks:
        h_hi, h_lo = _split2(h2s[c])
        both = (jnp.dot(h_hi, wr_ref[...], preferred_element_type=F32)
                + jnp.dot(h_lo, wr_ref[...], preferred_element_type=F32))
        all_logits.append(both + pltpu.roll(both, LANES - ROUTER_COLS, 1))
    for c, rows in zip(chunks, row_slices):
        logits = all_logits[c]
        e1, e2, w1, w2 = _route(logits)
        lane = lax.broadcasted_iota(jnp.int32, logits.shape, 1)
        r = jnp.where(lane == R_E1, e1.astype(F32),
                      jnp.where(lane == R_E2, e2.astype(F32),
                                jnp.where(lane == R_W1, w1, jnp.where(lane == R_W2, w2, 0.0))))
        r_ref[rows, :] = r
        rt = jnp.transpose(r)[:SUBLANES]
        rt_ref[:, rows] = rt
        on_lane = ((lane == e1) | (lane == e2)).astype(F32)
        cntr_ref[...] += jnp.broadcast_to(jnp.sum(on_lane, axis=0, keepdims=True), cntr_ref.shape)
        sub = lax.broadcasted_iota(jnp.int32, (N_EXPERTS, t), 0).astype(F32)
        on_sub = ((sub == rt[R_E1:R_E1 + 1]) | (sub == rt[R_E2:R_E2 + 1])).astype(F32)
        cntc_ref[...] += jnp.broadcast_to(jnp.sum(on_sub, axis=1, keepdims=True), cntc_ref.shape)


def _mix(na, sw, x, beta_na, beta_swa, w_out, g_post_mix, gate_a, g_pre_ffn, scale_f, shift_f, w_router3):
    b, s, d = x.shape
    tm = 1024
    nt = s // tm
    row = lambda bi, i: (bi, i, 0)
    const2 = lambda bi, i: (0, 0)
    per_b = lambda bi, i: (bi, 0, 0)
    return pl.pallas_call(
        _mix_kernel,
        out_shape=(jax.ShapeDtypeStruct((b, s, d), F32),
                   jax.ShapeDtypeStruct((b * s * TOKEN_ROWS, LANES), WORD),
                   jax.ShapeDtypeStruct((b * s, LANES), F32),
                   jax.ShapeDtypeStruct((SUBLANES, b * s), F32),
                   jax.ShapeDtypeStruct((N_EXPERTS, LANES), F32),
                   jax.ShapeDtypeStruct((SUBLANES, LANES), F32)),
        grid=(b, s // tm),
        in_specs=[pl.BlockSpec((1, PAIRS, tm, LANES), lambda bi, i: (bi, 0, i, 0)),
                  pl.BlockSpec((1, PAIRS, tm, LANES), lambda bi, i: (bi, 0, i, 0)),
                  pl.BlockSpec((1, tm, d), row),
                  pl.BlockSpec((1, NA_WIDTH), const2),
                  pl.BlockSpec((1, SWA_WIDTH), const2),
                  pl.BlockSpec((NA_WIDTH + SWA_WIDTH, d), const2),
                  pl.BlockSpec((1, d), const2),
                  pl.BlockSpec((1, 1, d), per_b),
                  pl.BlockSpec((1, d), const2),
                  pl.BlockSpec((1, 1, d), per_b),
                  pl.BlockSpec((1, 1, d), per_b),
                  pl.BlockSpec((d, LANES), const2)],
        out_specs=(pl.BlockSpec((1, tm, d), row),
                   pl.BlockSpec((tm * TOKEN_ROWS, LANES), lambda bi, i: (bi * nt + i, 0)),
                   pl.BlockSpec((tm, LANES), lambda bi, i: (bi * nt + i, 0)),
                   pl.BlockSpec((SUBLANES, tm), lambda bi, i: (0, bi * nt + i)),
                   pl.BlockSpec((N_EXPERTS, LANES), const2),
                   pl.BlockSpec((SUBLANES, LANES), const2)),
        compiler_params=_params("arbitrary", "arbitrary"),
        name="mix",
    )(na, sw, x, beta_na.reshape(1, -1), beta_swa.reshape(1, -1), w_out, g_post_mix.reshape(1, d), gate_a,
      g_pre_ffn.reshape(1, d), scale_f, shift_f, w_router3)


MOE_TILE = 256
MOE_CHUNKS = 1
PLAN_T = 1024
I_TILE, I_EXPERT, I_LO, I_HI, I_FIRST, I_NEW, I_NEXT, I_ORDER = range(8)


def _plan_kernel(rt_ref, cntc_ref, cntr_ref, pos_ref, items_ref, start_ref, carry_ref, *, n_tiles, n_items):
    i = pl.program_id(0)
    sub = lax.broadcasted_iota(jnp.int32, (N_EXPERTS, LANES), 0)
    lane = lax.broadcasted_iota(jnp.int32, (N_EXPERTS, LANES), 1)

    @pl.when(i == 0)
    def _():
        c_col = cntc_ref[:, 0:1]
        c_row = cntr_ref[0:1, :]
        s_col = jnp.sum(jnp.where(lane < sub, c_row, 0.0), axis=1, keepdims=True)
        s_row = jnp.sum(jnp.where(sub < lane, c_col, 0.0), axis=0, keepdims=True)
        start_ref[...] = jnp.broadcast_to(s_col, start_ref.shape)
        carry_ref[...] = jnp.zeros_like(carry_ref)

        def tiles_of(s, c):
            first = jnp.floor(s * (1.0 / MOE_TILE))
            last = jnp.floor((s + c - 1.0) * (1.0 / MOE_TILE))
            return first, jnp.where(c > 0.0, last - first + 1.0, 0.0)

        f_col, n_col = tiles_of(s_col, c_col)
        _, n_row = tiles_of(s_row, c_row)
        i_col = jnp.sum(jnp.where(lane < sub, n_row, 0.0), axis=1, keepdims=True)
        total = jnp.sum(n_col, axis=0, keepdims=True)
        k = lax.broadcasted_iota(jnp.int32, (N_EXPERTS, n_items), 1).astype(F32)
        subk = lax.broadcasted_iota(jnp.int32, (N_EXPERTS, n_items), 0).astype(F32)
        ek = jnp.sum(jnp.where(i_col + n_col <= k, 1.0, 0.0), axis=0, keepdims=True)
        k0 = k[0:1]
        valid = k0 < total
        sel = subk == ek

        def pick(v):
            return jnp.sum(jnp.where(sel, v, 0.0), axis=0, keepdims=True)

        i_k, f_k, s_k, c_k = pick(i_col), pick(f_col), pick(s_col), pick(c_col)
        tile = f_k + (k0 - i_k)
        row0 = tile * MOE_TILE
        lo = jnp.maximum(s_k, row0) - row0
        hi = jnp.minimum(s_k + c_k, row0 + MOE_TILE) - row0
        present = n_col > 0.0
        last_expert = jnp.sum(jnp.where(i_col + n_col <= total - 1.0, 1.0, 0.0), axis=0, keepdims=True)
        nxt = jnp.min(jnp.where(present & (subk > ek), subk, float(N_EXPERTS)), axis=0, keepdims=True)
        order = jnp.sum(jnp.where(present & (subk < ek), 1.0, 0.0), axis=0, keepdims=True)
        rows = [jnp.where(valid, tile, n_tiles - 1.0), jnp.where(valid, ek, last_expert),
                jnp.where(valid, lo, 0.0), jnp.where(valid, hi, 0.0),
                jnp.where(valid & (lo == 0.0), 1.0, 0.0),
                jnp.where(valid & (k0 == i_k), 1.0, 0.0), jnp.where(valid, nxt, float(N_EXPERTS)),
                jnp.where(valid, order, 0.0)]
        assert len(rows) == SUBLANES
        items_ref[...] = jnp.concatenate(rows, axis=0).astype(jnp.int32)

    t = rt_ref.shape[1]
    e1 = rt_ref[R_E1:R_E1 + 1, :]
    e2 = rt_ref[R_E2:R_E2 + 1, :]
    sub_t = lax.broadcasted_iota(jnp.int32, (N_EXPERTS, t), 0).astype(F32)
    oh1 = sub_t == e1
    oh2 = sub_t == e2
    oh = (oh1 | oh2).astype(F32)
    before = (lax.broadcasted_iota(jnp.int32, (t, t), 0) < lax.broadcasted_iota(jnp.int32, (t, t), 1)).astype(BF16)
    rank = jnp.dot(oh.astype(BF16), before, preferred_element_type=F32)
    base = start_ref[:, 0:1] + carry_ref[:, 0:1] + rank
    pos1 = jnp.sum(jnp.where(oh1, base, 0.0), axis=0, keepdims=True)
    pos2 = jnp.sum(jnp.where(oh2, base, 0.0), axis=0, keepdims=True)
    carry_ref[...] += jnp.broadcast_to(jnp.sum(oh, axis=1, keepdims=True), carry_ref.shape)
    pos = jnp.concatenate([pos1, pos2] + [jnp.zeros_like(pos1)] * (SUBLANES - 2), axis=0)
    pos_ref[...] = pos.astype(jnp.int32)


def _plan(rt, cntc, cntr):
    n = rt.shape[1]
    n_tiles = 2 * n // MOE_TILE
    n_items = 2 * LANES
    assert n_tiles + N_EXPERTS <= n_items
    return pl.pallas_call(
        functools.partial(_plan_kernel, n_tiles=n_tiles, n_items=n_items),
        out_shape=(jax.ShapeDtypeStruct((SUBLANES, n), jnp.int32),
                   jax.ShapeDtypeStruct((SUBLANES, n_items), jnp.int32)),
        grid=(n // PLAN_T,),
        in_specs=[pl.BlockSpec((SUBLANES, PLAN_T), lambda i: (0, i)),
                  pl.BlockSpec((N_EXPERTS, LANES), lambda i: (0, 0)),
                  pl.BlockSpec((SUBLANES, LANES), lambda i: (0, 0))],
        out_specs=(pl.BlockSpec((SUBLANES, PLAN_T), lambda i: (0, i)),
                   pl.BlockSpec((SUBLANES, n_items), lambda i: (0, 0))),
        scratch_shapes=[pltpu.VMEM((N_EXPERTS, LANES), F32), pltpu.VMEM((N_EXPERTS, LANES), F32)],
        compiler_params=_params("arbitrary"),
        name="plan",
    )(rt, cntc, cntr)


DISPATCH_T = 1024


def _token_rows(ref, index):
    return ref.at[pl.ds(pl.multiple_of(index * TOKEN_ROWS, TOKEN_ROWS), TOKEN_ROWS), :]


def _dispatch_kernel(pos_ref, h_ref, xs_hbm, sem):
    def body(r, carry):
        src = _token_rows(h_ref, r)
        for k in range(2):
            pltpu.make_async_copy(src, _token_rows(xs_hbm, pos_ref[k, r]), sem.at[0]).start(priority=k)
        return carry

    lax.fori_loop(0, DISPATCH_T, body, 0, unroll=8)
    for k in range(2):
        pltpu.make_async_copy(h_ref, xs_hbm.at[pl.ds(0, DISPATCH_T * TOKEN_ROWS), :], sem.at[0]).wait()


def _dispatch(pos, h2):
    n = pos.shape[1]
    return pl.pallas_call(
        _dispatch_kernel,
        out_shape=jax.ShapeDtypeStruct((2 * n * TOKEN_ROWS, LANES), WORD),
        grid=(n // DISPATCH_T,),
        in_specs=[pl.BlockSpec((SUBLANES, DISPATCH_T), lambda i: (0, i), memory_space=pltpu.SMEM),
                  pl.BlockSpec((DISPATCH_T * TOKEN_ROWS, LANES), lambda i: (i, 0))],
        out_specs=pl.BlockSpec(memory_space=pl.ANY),
        scratch_shapes=[pltpu.SemaphoreType.DMA((1,))],
        compiler_params=_params("arbitrary"),
        name="dispatch",
    )(pos, h2)


def _expert_kernel(tile_ref, exp_ref, lo_ref, hi_ref, first_ref, new_ref, next_ref, order_ref,
                   xs_ref, wg_hbm, wu_hbm, wd_hbm, o_ref, wg_buf, wu_buf, wd_buf, sem):
    k = pl.program_id(0)
    lo = lo_ref[k]
    hi = hi_ref[k]
    slot = order_ref[k] % 2

    def weight_copies(expert, sl):
        return [pltpu.make_async_copy(src.at[expert], dst.at[sl], sem.at[sl])
                for src, dst in ((wg_hbm, wg_buf), (wu_hbm, wu_buf), (wd_hbm, wd_buf))]

    @pl.when(k == 0)
    def _():
        for copy in weight_copies(exp_ref[0], 0):
            copy.start()

    @pl.when(new_ref[k] == 1)
    def _():
        for copy in weight_copies(exp_ref[k], slot):
            copy.wait()

        @pl.when(next_ref[k] < N_EXPERTS)
        def _():
            for copy in weight_copies(next_ref[k], 1 - slot):
                copy.start()

    @pl.when(first_ref[k] == 1)
    def _():
        o_ref[...] = jnp.zeros_like(o_ref)

    @pl.when(hi > lo)
    def _():
        wg = wg_buf[slot].astype(BF16)
        wu = wu_buf[slot].astype(BF16)
        wd = wd_buf[slot].astype(BF16)
        t = MOE_TILE // MOE_CHUNKS
        for c in range(MOE_CHUNKS):
            base = c * t * TOKEN_ROWS
            x = _from_token_tiles(xs_ref, t, base=base).astype(BF16)
            gate = jnp.dot(x, wg, preferred_element_type=F32)
            up = jnp.dot(x, wu, preferred_element_type=F32)
            he = (gate * jax.nn.sigmoid(gate) * up).astype(BF16)
            ye = jnp.dot(he, wd, preferred_element_type=F32)
            row = c * t + lax.broadcasted_iota(jnp.int32, (t, 1), 0)
            mine = (row >= lo) & (row < hi)
            _to_token_tiles(o_ref, jnp.where(mine, ye, _from_token_tiles(o_ref, t, base=base)), base=base)


def _experts(items, xs, w_gate, w_up, w_down):
    n_rows = xs.shape[0] // TOKEN_ROWS
    n_items = n_rows // MOE_TILE + N_EXPERTS
    d = w_gate.shape[1]
    tile_map = lambda k, tile, *_: (tile[k], 0)
    grid_spec = pltpu.PrefetchScalarGridSpec(
        num_scalar_prefetch=SUBLANES,
        grid=(n_items,),
        in_specs=[pl.BlockSpec((MOE_TILE * TOKEN_ROWS, LANES), tile_map),
                  pl.BlockSpec(memory_space=pl.ANY),
                  pl.BlockSpec(memory_space=pl.ANY),
                  pl.BlockSpec(memory_space=pl.ANY)],
        out_specs=pl.BlockSpec((MOE_TILE * TOKEN_ROWS, LANES), tile_map),
        scratch_shapes=[pltpu.VMEM((2, d, EXPERT_FF), F32), pltpu.VMEM((2, d, EXPERT_FF), F32),
                        pltpu.VMEM((2, EXPERT_FF, d), F32), pltpu.SemaphoreType.DMA((2,))])
    return pl.pallas_call(
        _expert_kernel,
        out_shape=jax.ShapeDtypeStruct(xs.shape, WORD),
        grid_spec=grid_spec,
        compiler_params=_params("arbitrary"),
        name="experts",
    )(*(items[j, :n_items] for j in range(SUBLANES)), xs, w_gate, w_up, w_down)


COMBINE_T = 256


def _combine_kernel(pos_ref, posn_ref, ys_hbm, r_ref, x1_ref, gf_ref, gpost_ref, o_ref, buf, sem):
    i = pl.program_id(0)
    n = pl.num_programs(0)
    slot = i % 2
    slot_rows = COMBINE_T * TOKEN_ROWS

    def start_row(p_ref, sl, r):
        for k in range(2):
            dst = buf.at[sl, pl.ds(pl.multiple_of(k * slot_rows + r * TOKEN_ROWS, TOKEN_ROWS), TOKEN_ROWS), :]
            pltpu.make_async_copy(_token_rows(ys_hbm, p_ref[k, r]), dst, sem.at[sl]).start(priority=k)

    def wait_slot(sl):
        pltpu.make_async_copy(ys_hbm.at[pl.ds(0, 2 * slot_rows), :], buf.at[sl], sem.at[sl]).wait()

    def issue(p_ref, sl):
        def body(r, carry):
            start_row(p_ref, sl, r)
            return carry

        lax.fori_loop(0, COMBINE_T, body, 0, unroll=8)

    @pl.when(i == 0)
    def _():
        issue(pos_ref, 0)

    @pl.when(i + 1 < n)
    def _():
        issue(posn_ref, 1 - slot)

    wait_slot(slot)
    ya = _from_token_tiles(buf.at[slot], COMBINE_T)
    yb = _from_token_tiles(buf.at[slot], COMBINE_T, base=slot_rows)
    r = r_ref[...]
    y = r[:, R_W1:R_W1 + 1] * ya + r[:, R_W2:R_W2 + 1] * yb
    o_ref[0] = x1_ref[0] + gf_ref[0] * (_rms(y) * gpost_ref[...])


def _combine(pos, ys, r, x1, gate_f, g_post_ffn):
    b, s, d = x1.shape
    nt = s // COMBINE_T
    n_steps = b * nt
    row = lambda i: (i // nt, i % nt, 0)
    pos_spec = lambda f: pl.BlockSpec((SUBLANES, COMBINE_T), lambda i: (0, f(i)), memory_space=pltpu.SMEM)
    return pl.pallas_call(
        _combine_kernel,
        out_shape=jax.ShapeDtypeStruct((b, s, d), F32),
        grid=(n_steps,),
        in_specs=[pos_spec(lambda i: i),
                  pos_spec(lambda i: jnp.minimum(i + 1, n_steps - 1)),
                  pl.BlockSpec(memory_space=pl.ANY),
                  pl.BlockSpec((COMBINE_T, LANES), lambda i: (i, 0)),
                  pl.BlockSpec((1, COMBINE_T, d), row),
                  pl.BlockSpec((1, 1, d), lambda i: (i // nt, 0, 0)),
                  pl.BlockSpec((1, d), lambda i: (0, 0))],
        out_specs=pl.BlockSpec((1, COMBINE_T, d), row),
        scratch_shapes=[pltpu.VMEM((2, 2 * COMBINE_T * TOKEN_ROWS, LANES), WORD), pltpu.SemaphoreType.DMA((2,))],
        compiler_params=_params("arbitrary"),
        name="combine",
    )(pos, pos, ys, r, x1, gate_f, g_post_ffn.reshape(1, d))


def _rope_tables(s):
    half = HEAD_DIM // 2
    inv = ROPE_THETA ** (-np.arange(half, dtype=np.float64) * 2.0 / HEAD_DIM)
    ang = np.arange(s, dtype=np.float64)[:, None] * inv[None, :]
    cos = np.cos(ang)
    sin = np.sin(ang)
    cos = np.concatenate([cos, cos, cos, cos], axis=-1)
    sin_signed = np.concatenate([-sin, sin, -sin, sin], axis=-1)
    return jnp.asarray(cos, F32), jnp.asarray(sin_signed, F32)


def _router_weights(w_group_router, w_expert_router):
    d = w_group_router.shape[0]
    we = jnp.transpose(w_expert_router, (1, 0, 2)).reshape(d, N_EXPERTS)
    hi, lo = _split2(jnp.concatenate([w_group_router, we], axis=-1))
    return jnp.concatenate([hi, lo, jnp.zeros((d, LANES - 2 * ROUTER_COLS), BF16)], axis=-1)


def kernel(x, c, w_ada, b_ada, g_pre_mix, w_in, na_rpb, swa_sinks, beta_na, beta_swa, w_out, g_post_mix, g_pre_ffn,
           w_group_router, w_expert_router, w_gate, w_up, w_down, g_post_ffn):
    b, s, d = x.shape
    depth = w_ada.shape[0]
    cos, sin_signed = _rope_tables(s)
    for l in range(depth):
        mod = _adaln(c, w_ada[l], b_ada[l]).reshape(b, N_MOD, 1, d)
        shift_a, scale_a, gate_a, shift_f, scale_f, gate_f = (mod[:, k] for k in range(N_MOD))
        qkv, na_kg, na_vg = _qkv(x, g_pre_mix[l], scale_a, shift_a, w_in[l].astype(BF16), cos, sin_signed)
        na = _na(qkv, na_kg, na_vg, na_rpb[l])
        sw = _swa(qkv, swa_sinks[l])
        x1, h2, r, rt, cntc, cntr = _mix(na, sw, x, beta_na[l], beta_swa[l], w_out[l].astype(BF16), g_post_mix[l],
                                          gate_a, g_pre_ffn[l], scale_f, shift_f,
                                          _router_weights(w_group_router[l], w_expert_router[l]))
        pos, items = _plan(rt, cntc, cntr)
        xs = _dispatch(pos, h2)
        ys = _experts(items, xs, w_gate[l], w_up[l], w_down[l])
        x = _combine(pos, ys, r, x1, gate_f, g_post_ffn[l])
    return x
```

```python
import functools

import jax
import jax.numpy as jnp
import numpy as np
from jax import lax
from jax.experimental import pallas as pl
from jax.experimental.pallas import tpu as pltpu

D_MODEL = 1024
GRID_W = 64
HEAD_DIM = 64
NA_HEADS = 8
NA_KH = 8
NA_KW = 16
SWA_HEADS = 8
SWA_KV_HEADS = 2
SWA_WINDOW = 128
SWA_BLOCK = 128
ROPE_THETA = 10000.0
NA_WIDTH = NA_HEADS * HEAD_DIM
SWA_WIDTH = SWA_HEADS * HEAD_DIM
N_GROUPS = 4
EXPERTS_PER_GROUP = 8
N_EXPERTS = N_GROUPS * EXPERTS_PER_GROUP
EXPERT_FF = 256
N_MOD = 6
EPS = 1e-6
NEG_INF = -1e30

LANES = 128
SUBLANES = 8
R_E1, R_E2, R_W1, R_W2 = range(4)
PAIRS = NA_HEADS // 2
W_NQ, W_NK, W_NV, W_SQ, W_SKV = 0, 4, 8, 12, 16
COL_NQ, COL_SQ, COL_SK, COL_SV = 0, 4, 8, 10
QKV_TILES = 12
NA_GROUPS = GRID_W // NA_KW
NA_WIN = 2 * NA_KW
NA_WIN_START = tuple(min(max(NA_KW * g - NA_KW // 2, 0), GRID_W - NA_WIN) for g in range(NA_GROUPS))
ROUTER_BASE = N_GROUPS
VMEM_LIMIT = 56 * 1024 * 1024

F32 = jnp.float32
BF16 = jnp.bfloat16
LOG2E = 1.4426950408889634


def _rms(v):
    return v * lax.rsqrt(jnp.mean(v * v, axis=-1, keepdims=True) + EPS)


def _params(*sem):
    return pltpu.CompilerParams(dimension_semantics=sem, vmem_limit_bytes=VMEM_LIMIT)


def _adaln_kernel(c_ref, w_ref, b_ref, o_ref, *, batch):
    c = c_ref[...]
    a_t = jnp.transpose(c * jax.nn.sigmoid(c))
    w = w_ref[...]
    rows = [jnp.sum(a_t[:, bi:bi + 1] * w, axis=0, keepdims=True) for bi in range(batch)]
    rows.append(jnp.zeros((c.shape[0] - batch, w.shape[1]), F32))
    o_ref[...] = jnp.concatenate(rows, axis=0) + b_ref[...]


def _adaln(c, w_ada, b_ada):
    batch, d = c.shape
    n = w_ada.shape[1]
    tn = 1024
    b = SUBLANES
    assert batch < b
    c = jnp.pad(c, ((0, b - batch), (0, 0)))
    return pl.pallas_call(
        functools.partial(_adaln_kernel, batch=batch),
        out_shape=jax.ShapeDtypeStruct((b, n), F32),
        grid=(n // tn,),
        in_specs=[pl.BlockSpec((b, d), lambda j: (0, 0)),
                  pl.BlockSpec((d, tn), lambda j: (0, j)),
                  pl.BlockSpec((1, tn), lambda j: (0, j))],
        out_specs=pl.BlockSpec((b, tn), lambda j: (0, j)),
        compiler_params=_params("arbitrary"),
        name="adaln",
    )(c, w_ada, b_ada.reshape(1, n))[:batch]


def _rope(v, cos, sin_signed, first_half):
    rot = jnp.where(first_half, pltpu.roll(v, LANES - HEAD_DIM // 2, 1), pltpu.roll(v, HEAD_DIM // 2, 1))
    return v * cos + rot * sin_signed


QKV_CHUNK = 256


def _column_windows(v):
    grid_rows = v.shape[0] // GRID_W
    return [jnp.concatenate([v[r * GRID_W + w0:r * GRID_W + w0 + NA_WIN] for r in range(grid_rows)], axis=0)
            for w0 in NA_WIN_START]


def _qkv_kernel(x_ref, g_ref, sc_ref, sh_ref, w_ref, cos_ref, sin_ref, o_ref, kg_ref, vg_ref):
    scale = HEAD_DIM ** -0.5 * LOG2E
    lane = lax.broadcasted_iota(jnp.int32, (QKV_CHUNK, LANES), 1)
    first_half = (lane % HEAD_DIM) < HEAD_DIM // 2
    upper = lane >= HEAD_DIM

    def tile(v, j):
        return v[:, j * LANES:(j + 1) * LANES]

    for c in range(x_ref.shape[1] // QKV_CHUNK):
        rows = slice(c * QKV_CHUNK, (c + 1) * QKV_CHUNK)
        h = (_rms(x_ref[0, rows, :]) * g_ref[...]) * (1.0 + sc_ref[0]) + sh_ref[0]
        h = h.astype(BF16)
        cos = cos_ref[rows, :]
        sin = sin_ref[rows, :]

        def proj(col, width):
            return jnp.dot(h, w_ref[:, col * LANES:(col + width) * LANES], preferred_element_type=F32)

        nq, nk, nv, sq = proj(W_NQ, 4), proj(W_NK, 4), proj(W_NV, 4), proj(W_SQ, 4)
        win_rows = slice(c * QKV_CHUNK // 2, (c + 1) * QKV_CHUNK // 2)
        for j in range(PAIRS):
            o_ref[0, COL_NQ + j, rows, :] = (tile(nq, j) * scale).astype(BF16)
            o_ref[0, COL_SQ + j, rows, :] = (_rope(tile(sq, j), cos, sin, first_half) * scale).astype(BF16)
            for ref, val in ((kg_ref, tile(nk, j)), (vg_ref, tile(nv, j))):
                for g, win in enumerate(_column_windows(val)):
                    ref[0, j, g, win_rows, :] = win.astype(BF16)
        skv = proj(W_SKV, 2)
        k = _rope(tile(skv, 0), cos, sin, first_half)
        v = tile(skv, 1)
        for t, col in ((k, COL_SK), (v, COL_SV)):
            swapped = pltpu.roll(t, HEAD_DIM, 1)
            o_ref[0, col, rows, :] = jnp.where(upper, swapped, t).astype(BF16)
            o_ref[0, col + 1, rows, :] = jnp.where(upper, t, swapped).astype(BF16)


def _qkv(x, g, scale_a, shift_a, w_in, cos, sin):
    b, s, d = x.shape
    tm = 1024
    n_in = w_in.shape[1]
    windows = jax.ShapeDtypeStruct((b, PAIRS, NA_GROUPS, s // 2, LANES), BF16)
    windows_spec = pl.BlockSpec((1, PAIRS, NA_GROUPS, tm // 2, LANES), lambda bi, i: (bi, 0, 0, i, 0))
    return pl.pallas_call(
        _qkv_kernel,
        out_shape=(jax.ShapeDtypeStruct((b, QKV_TILES, s, LANES), BF16), windows, windows),
        grid=(b, s // tm),
        in_specs=[pl.BlockSpec((1, tm, d), lambda bi, i: (bi, i, 0)),
                  pl.BlockSpec((1, d), lambda bi, i: (0, 0)),
                  pl.BlockSpec((1, 1, d), lambda bi, i: (bi, 0, 0)),
                  pl.BlockSpec((1, 1, d), lambda bi, i: (bi, 0, 0)),
                  pl.BlockSpec((d, n_in), lambda bi, i: (0, 0)),
                  pl.BlockSpec((tm, LANES), lambda bi, i: (i, 0)),
                  pl.BlockSpec((tm, LANES), lambda bi, i: (i, 0))],
        out_specs=(pl.BlockSpec((1, QKV_TILES, tm, LANES), lambda bi, i: (bi, 0, i, 0)), windows_spec, windows_spec),
        compiler_params=_params("arbitrary", "arbitrary"),
        name="qkv",
    )(x, g.reshape(1, d), scale_a, shift_a, w_in, cos, sin)


NA_QROWS = 8
NA_KROWS = NA_QROWS + NA_KH
NA_BLOCKS_PER_STEP = 4
NA_INTERLEAVE = 4
NA_Q = NA_QROWS * NA_KW
NA_K = NA_KROWS * NA_WIN
NA_RPB_ROWS = 2 * NA_KH - 1
NA_RPB_COLS = 2 * NA_KW - 1


def _clamp(v, lo, hi):
    return min(max(v, lo), hi)


def _na_first_key_row(block, rows, clip):
    return clip(block * NA_QROWS - NA_KH // 2, 0, rows - NA_KROWS)


def _na_group_tables():
    def geometry(g):
        cols = [NA_KW * g + cc for cc in range(NA_KW)]
        return (NA_WIN_START[g] - NA_KW * g,) + tuple(_clamp(c - NA_KW // 2, 0, GRID_W - NA_KW) - c for c in cols)

    seen, table_of_group, representatives = {}, [], []
    for g in range(NA_GROUPS):
        key = geometry(g)
        if key not in seen:
            seen[key] = len(representatives)
            representatives.append(g)
        table_of_group.append(seen[key])
    return table_of_group, representatives


def _na_block_types(rows):
    def geometry(block):
        r = block * NA_QROWS
        a = _na_first_key_row(block, rows, _clamp)
        return (a - r,) + tuple(_clamp(r + j - NA_KH // 2, 0, rows - NA_KH) - r for j in range(NA_QROWS))

    n_blocks = rows // NA_QROWS
    interior = geometry(n_blocks // 2)
    lead = next(b for b in range(n_blocks) if geometry(b) == interior)
    trail = next(b for b in range(n_blocks) if geometry(n_blocks - 1 - b) == interior)
    assert all(geometry(b) == interior for b in range(lead, n_blocks - trail))
    return lead, trail


def _na_kernel(q_ref, k_ref, v_ref, bias_ref, o_ref, *, rows):
    upper = lax.broadcasted_iota(jnp.int32, (NA_Q, LANES), 1) >= HEAD_DIM
    key_upper = lax.broadcasted_iota(jnp.int32, (NA_K, LANES), 1) >= HEAD_DIM
    lead, trail = _na_block_types(rows)
    first_trailing = rows // NA_QROWS - trail
    n_types = lead + trail + 1
    table_of_group, _ = _na_group_tables()
    for c0 in range(0, NA_BLOCKS_PER_STEP, NA_INTERLEAVE):
        blocks = range(c0, c0 + NA_INTERLEAVE)
        starts, types, pieces = {}, {}, {}
        for c in blocks:
            block = pl.program_id(2) * NA_BLOCKS_PER_STEP + c
            a = _na_first_key_row(block, rows, jnp.clip)
            types[c] = jnp.where(block < lead, block,
                                 jnp.where(block >= first_trailing, block - first_trailing + lead + 1, lead))
            starts[c] = pl.multiple_of(a * NA_WIN, NA_WIN)
            for g in range(NA_GROUPS):
                pieces[c, g] = [slice((c * NA_QROWS + rr) * GRID_W + g * NA_KW,
                                      (c * NA_QROWS + rr) * GRID_W + (g + 1) * NA_KW) for rr in range(NA_QROWS)]
        chains = [(c, g, hh) for c in blocks for g in range(NA_GROUPS) for hh in range(2)]
        scores, probs, outs = {}, {}, {}
        for c, g, hh in chains:
            q = jnp.concatenate([q_ref[0, 0, rws, :] for rws in pieces[c, g]], axis=0)
            qm = jnp.where(upper if hh else ~upper, q, jnp.zeros_like(q))
            ks = k_ref[0, 0, g, pl.ds(starts[c], NA_K), :]
            s = lax.dot_general(qm, ks, (((1,), (1,)), ((), ())), preferred_element_type=F32)
            scores[c, g, hh] = s + bias_ref[hh, table_of_group[g] * n_types + types[c]]
        for chain in chains:
            s = scores[chain]
            probs[chain] = jnp.exp2(s - jnp.max(s, axis=-1, keepdims=True)).astype(BF16)
        for c, g, hh in chains:
            vs = v_ref[0, 0, g, pl.ds(starts[c], NA_K), :]
            v1 = jnp.where(key_upper if hh else ~key_upper, vs, jnp.ones_like(vs))
            o = jnp.dot(probs[c, g, hh], v1, preferred_element_type=F32)
            outs[c, g, hh] = o / o[:, (1 - hh) * HEAD_DIM:(1 - hh) * HEAD_DIM + 1]
        for c in blocks:
            for g in range(NA_GROUPS):
                out = jnp.where(upper, outs[c, g, 1], outs[c, g, 0]).astype(BF16)
                for rr, rws in enumerate(pieces[c, g]):
                    o_ref[0, 0, rws, :] = out[rr * NA_KW:(rr + 1) * NA_KW]


def _na_bias_kernel(rpb_ref, o_ref, *, rows):
    h = pl.program_id(0)
    cc = lax.broadcasted_iota(jnp.int32, (NA_KW, LANES), 0)
    lane = lax.broadcasted_iota(jnp.int32, (NA_KW, LANES), 1)
    w = lane % NA_WIN
    key_row_in_tile = lane // NA_WIN
    rows_per_tile = LANES // NA_WIN
    neg = jnp.full((NA_KW, LANES), NEG_INF, F32)
    base = h * NA_RPB_ROWS * NA_RPB_COLS
    lead, trail = _na_block_types(rows)
    n_blocks = rows // NA_QROWS
    type_blocks = list(range(lead + 1)) + list(range(n_blocks - trail, n_blocks))
    for table, g in enumerate(_na_group_tables()[1]):
        qc = g * NA_KW + cc
        kc = NA_WIN_START[g] + w
        c0 = jnp.clip(qc - NA_KW // 2, 0, GRID_W - NA_KW)
        in_cols = (kc >= c0) & (kc < c0 + NA_KW)
        dc = kc - qc + NA_KW - 1
        by_row_offset = []
        for d in range(NA_RPB_ROWS):
            acc = neg
            for dd in range(NA_RPB_COLS):
                acc = jnp.where(dc == dd, rpb_ref[base + d * NA_RPB_COLS + dd], acc)
            by_row_offset.append(jnp.where(in_cols, acc * LOG2E, NEG_INF))
        for ty, block in enumerate(type_blocks):
            r = block * NA_QROWS
            a = _na_first_key_row(block, rows, _clamp)
            for j in range(NA_QROWS):
                r0 = _clamp(r + j - NA_KH // 2, 0, rows - NA_KH)
                for t in range(NA_K // LANES):
                    tile = neg
                    for part in range(rows_per_tile):
                        i = t * rows_per_tile + part
                        if r0 <= a + i < r0 + NA_KH:
                            tile = jnp.where(key_row_in_tile == part, by_row_offset[a + i - (r + j) + NA_KH - 1], tile)
                    o_ref[0, table * len(type_blocks) + ty, j * NA_KW:(j + 1) * NA_KW, t * LANES:(t + 1) * LANES] = tile


def _na_bias(rpb, rows):
    n_tables = len(_na_group_tables()[1]) * (sum(_na_block_types(rows)) + 1)
    return pl.pallas_call(
        functools.partial(_na_bias_kernel, rows=rows),
        out_shape=jax.ShapeDtypeStruct((NA_HEADS, n_tables, NA_Q, NA_K), F32),
        grid=(NA_HEADS,),
        in_specs=[pl.BlockSpec(memory_space=pltpu.SMEM)],
        out_specs=pl.BlockSpec((1, n_tables, NA_Q, NA_K), lambda h: (h, 0, 0, 0)),
        compiler_params=_params("arbitrary"),
        name="na_bias",
    )(rpb.astype(F32).reshape(-1))


def _na(qkv, kg, vg, rpb):
    b, _, s, _ = qkv.shape
    rows = s // GRID_W
    assert rows % (NA_QROWS * NA_BLOCKS_PER_STEP) == 0
    tq = NA_BLOCKS_PER_STEP * NA_QROWS * GRID_W
    bias = _na_bias(rpb, rows)
    windows_spec = pl.BlockSpec((1, 1) + kg.shape[2:], lambda bi, p, i: (bi, p, 0, 0, 0))
    return pl.pallas_call(
        functools.partial(_na_kernel, rows=rows),
        out_shape=jax.ShapeDtypeStruct((b, PAIRS, s, LANES), BF16),
        grid=(b, PAIRS, s // tq),
        in_specs=[pl.BlockSpec((1, 1, tq, LANES), lambda bi, p, i: (bi, COL_NQ + p, i, 0)),
                  windows_spec,
                  windows_spec,
                  pl.BlockSpec((2,) + bias.shape[1:], lambda bi, p, i: (p, 0, 0, 0))],
        out_specs=pl.BlockSpec((1, 1, tq, LANES), lambda bi, p, i: (bi, p, i, 0)),
        compiler_params=_params("arbitrary", "arbitrary", "arbitrary"),
        name="na",
    )(qkv, kg, vg, bias)


SWA_KEYS = 3 * SWA_BLOCK
SWA_BLOCKS_PER_STEP = 8
SWA_INTERLEAVE = 2
SWA_GROUP = SWA_HEADS // SWA_KV_HEADS
assert COL_SQ % (PAIRS // SWA_KV_HEADS) == 0


def _swa_masks():
    v = np.arange(SWA_KEYS // SWA_BLOCK)[:, None, None]
    q = np.arange(SWA_BLOCK)[None, :, None]
    k = np.arange(SWA_KEYS)[None, None, :]
    return np.where(np.abs(k - v * SWA_BLOCK - q) <= SWA_WINDOW, 0.0, NEG_INF).astype(np.float32)


def _swa_kernel(sink_ref, mask_ref, q_ref, k_ref, v_ref, o_ref, *, seq):
    kv = pl.program_id(1)
    rows = SWA_GROUP * SWA_BLOCK
    lane = lax.broadcasted_iota(jnp.int32, (SWA_BLOCK, LANES), 1)
    upper = lane >= HEAD_DIM
    head = lax.broadcasted_iota(jnp.int32, (rows, 1), 0) // SWA_BLOCK
    sink = jnp.zeros((rows, 1), F32)
    for g in range(SWA_GROUP):
        sink = jnp.where(head == g, sink_ref[kv * SWA_GROUP + g], sink)
    sink = sink * LOG2E
    for j0 in range(0, SWA_BLOCKS_PER_STEP, SWA_INTERLEAVE):
        blocks = range(j0, j0 + SWA_INTERLEAVE)
        starts, scores, probs = {}, {}, {}
        for j in blocks:
            n = pl.program_id(2) * SWA_BLOCKS_PER_STEP + j
            start = pl.multiple_of(jnp.clip((n - 1) * SWA_BLOCK, 0, seq - SWA_KEYS), SWA_BLOCK)
            starts[j] = start
            ks = k_ref[0, 0, pl.ds(start, SWA_KEYS), :]
            qs = []
            for g in range(SWA_GROUP):
                q = q_ref[0, g // 2, j * SWA_BLOCK:(j + 1) * SWA_BLOCK, :]
                qs.append(jnp.where(upper if g % 2 else ~upper, q, jnp.zeros_like(q)))
            s = lax.dot_general(jnp.concatenate(qs, axis=0), ks, (((1,), (1,)), ((), ())), preferred_element_type=F32)
            mask = mask_ref[(n * SWA_BLOCK - start) // SWA_BLOCK]
            scores[j] = s + jnp.concatenate([mask] * SWA_GROUP, axis=0)
        for j in blocks:
            s = scores[j]
            m = jnp.maximum(jnp.max(s, axis=-1, keepdims=True), sink)
            e = jnp.exp2(s - m)
            probs[j] = (e.astype(BF16), jnp.sum(e, axis=-1, keepdims=True) + jnp.exp2(sink - m))
        for j in blocks:
            e, l = probs[j]
            vs = v_ref[0, 0, pl.ds(starts[j], SWA_KEYS), :]
            o = jnp.dot(e, vs, preferred_element_type=F32) / l
            for pair in range(SWA_GROUP // 2):
                even = o[(2 * pair) * SWA_BLOCK:(2 * pair + 1) * SWA_BLOCK]
                odd = o[(2 * pair + 1) * SWA_BLOCK:(2 * pair + 2) * SWA_BLOCK]
                o_ref[0, pair, j * SWA_BLOCK:(j + 1) * SWA_BLOCK, :] = jnp.where(upper, odd, even).astype(BF16)


def _swa(qkv, sinks):
    b, _, s, _ = qkv.shape
    pairs_per_kv = PAIRS // SWA_KV_HEADS
    tq = SWA_BLOCKS_PER_STEP * SWA_BLOCK
    masks = _swa_masks()
    return pl.pallas_call(
        functools.partial(_swa_kernel, seq=s),
        out_shape=jax.ShapeDtypeStruct((b, PAIRS, s, LANES), BF16),
        grid=(b, SWA_KV_HEADS, s // tq),
        in_specs=[pl.BlockSpec(memory_space=pltpu.SMEM),
                  pl.BlockSpec(masks.shape, lambda bi, kv, n: (0, 0, 0)),
                  pl.BlockSpec((1, pairs_per_kv, tq, LANES),
                               lambda bi, kv, n: (bi, COL_SQ // pairs_per_kv + kv, n, 0)),
                  pl.BlockSpec((1, 1, s, LANES), lambda bi, kv, n: (bi, COL_SK + kv, 0, 0)),
                  pl.BlockSpec((1, 1, s, LANES), lambda bi, kv, n: (bi, COL_SV + kv, 0, 0))],
        out_specs=pl.BlockSpec((1, pairs_per_kv, tq, LANES), lambda bi, kv, n: (bi, kv, n, 0)),
        compiler_params=_params("arbitrary", "arbitrary", "arbitrary"),
        name="swa",
    )(sinks, jnp.asarray(masks), qkv, qkv, qkv)


ROUTER_COLS = N_GROUPS + N_EXPERTS


def _split2(v):
    hi = v.astype(BF16)
    lo = (v - hi.astype(F32)).astype(BF16)
    return hi, lo


def _route(logits):
    lane = lax.broadcasted_iota(jnp.int32, logits.shape, 1)
    big = jnp.int32(LANES)
    gmask = lane < N_GROUPS
    gl = jnp.where(gmask, logits, NEG_INF)
    gmax = jnp.max(gl, axis=-1, keepdims=True)
    g_top = jnp.min(jnp.where(gmask & (gl == gmax), lane, big), axis=-1, keepdims=True)
    g_weight = 1.0 / jnp.sum(jnp.where(gmask, jnp.exp(gl - gmax), 0.0), axis=-1, keepdims=True)
    lo = ROUTER_BASE + g_top * EXPERTS_PER_GROUP
    emask = (lane >= lo) & (lane < lo + EXPERTS_PER_GROUP)
    el = jnp.where(emask, logits, NEG_INF)
    m1 = jnp.max(el, axis=-1, keepdims=True)
    i1 = jnp.min(jnp.where(emask & (el == m1), lane, big), axis=-1, keepdims=True)
    emask2 = emask & (lane != i1)
    el2 = jnp.where(emask2, logits, NEG_INF)
    m2 = jnp.max(el2, axis=-1, keepdims=True)
    i2 = jnp.min(jnp.where(emask2 & (el2 == m2), lane, big), axis=-1, keepdims=True)
    e2 = jnp.exp(m2 - m1)
    w1 = g_weight / (1.0 + e2)
    w2 = g_weight * e2 / (1.0 + e2)
    return i1 - ROUTER_BASE, i2 - ROUTER_BASE, w1, w2


TOKEN_ROWS = D_MODEL // (2 * LANES)
WORD = jnp.uint32


def _to_token_tiles(ref, v, base=0):
    t, d = v.shape
    words = pltpu.pack_elementwise([v[:, :d // 2], v[:, d // 2:]], packed_dtype=BF16)
    for s in range(TOKEN_ROWS):
        ref[pl.ds(base + s, t, stride=TOKEN_ROWS), :] = words[:, s * LANES:(s + 1) * LANES]


def _token_words(ref, t, base=0):
    return jnp.concatenate([ref[pl.ds(base + s, t, stride=TOKEN_ROWS), :] for s in range(TOKEN_ROWS)], axis=-1)


def _from_token_tiles(ref, t, base=0):
    words = _token_words(ref, t, base)
    halves = [pltpu.unpack_elementwise(words, index=j, packed_dtype=BF16, unpacked_dtype=F32) for j in range(2)]
    return jnp.concatenate(halves, axis=-1)


MIX_CHUNK = 256


def _mix_kernel(na_ref, sw_ref, x_ref, bna_ref, bsw_ref, wo_ref, gpm_ref, ga_ref, gpf_ref, scf_ref, shf_ref,
                wr_ref, x1_ref, h2_ref, r_ref, rt_ref, cntc_ref, cntr_ref):
    first_step = (pl.program_id(0) == 0) & (pl.program_id(1) == 0)

    @pl.when(first_step)
    def _():
        cntc_ref[...] = jnp.zeros_like(cntc_ref)
        cntr_ref[...] = jnp.zeros_like(cntr_ref)

    t = MIX_CHUNK
    chunks = range(x_ref.shape[1] // t)
    row_slices = [slice(c * t, (c + 1) * t) for c in chunks]
    mixes, h2s, all_logits = [], [], []
    for rows in row_slices:
        def heads(ref):
            return jnp.concatenate([ref[0, j, rows, :] for j in range(PAIRS)], axis=-1).astype(F32)

        na = (_rms(heads(na_ref)) * bna_ref[...]).astype(BF16)
        sw = (_rms(heads(sw_ref)) * bsw_ref[...]).astype(BF16)
        mixes.append(jnp.dot(na, wo_ref[:NA_WIDTH, :], preferred_element_type=F32)
                     + jnp.dot(sw, wo_ref[NA_WIDTH:, :], preferred_element_type=F32))
    gate_gain = ga_ref[0] * gpm_ref[...]
    ffn_gain = gpf_ref[...] * (1.0 + scf_ref[0])
    for c, rows in zip(chunks, row_slices):
        x1 = x_ref[0, rows, :] + _rms(mixes[c]) * gate_gain
        x1_ref[0, rows, :] = x1
        h2 = _rms(x1) * ffn_gain + shf_ref[0]
        _to_token_tiles(h2_ref, h2, base=c * t * TOKEN_ROWS)
        h2s.append(h2)
    for c in chunks:
        h_hi, h_lo = _split2(h2s[c])
        both = (jnp.dot(h_hi, wr_ref[...], preferred_element_type=F32)
                + jnp.dot(h_lo, wr_ref[...], preferred_element_type=F32))
        all_logits.append(both + pltpu.roll(both, LANES - ROUTER_COLS, 1))
    for c, rows in zip(chunks, row_slices):
        logits = all_logits[c]
        e1, e2, w1, w2 = _route(logits)
        lane = lax.broadcasted_iota(jnp.int32, logits.shape, 1)
        r = jnp.where(lane == R_E1, e1.astype(F32),
                      jnp.where(lane == R_E2, e2.astype(F32),
                                jnp.where(lane == R_W1, w1, jnp.where(lane == R_W2, w2, 0.0))))
        r_ref[rows, :] = r
        rt = jnp.transpose(r)[:SUBLANES]
        rt_ref[:, rows] = rt
        on_lane = ((lane == e1) | (lane == e2)).astype(F32)
        cntr_ref[...] += jnp.broadcast_to(jnp.sum(on_lane, axis=0, keepdims=True), cntr_ref.shape)
        sub = lax.broadcasted_iota(jnp.int32, (N_EXPERTS, t), 0).astype(F32)
        on_sub = ((sub == rt[R_E1:R_E1 + 1]) | (sub == rt[R_E2:R_E2 + 1])).astype(F32)
        cntc_ref[...] += jnp.broadcast_to(jnp.sum(on_sub, axis=1, keepdims=True), cntc_ref.shape)


def _mix(na, sw, x, beta_na, beta_swa, w_out, g_post_mix, gate_a, g_pre_ffn, scale_f, shift_f, w_router3):
    b, s, d = x.shape
    tm = 1024
    nt = s // tm
    row = lambda bi, i: (bi, i, 0)
    const2 = lambda bi, i: (0, 0)
    per_b = lambda bi, i: (bi, 0, 0)
    return pl.pallas_call(
        _mix_kernel,
        out_shape=(jax.ShapeDtypeStruct((b, s, d), F32),
                   jax.ShapeDtypeStruct((b * s * TOKEN_ROWS, LANES), WORD),
                   jax.ShapeDtypeStruct((b * s, LANES), F32),
                   jax.ShapeDtypeStruct((SUBLANES, b * s), F32),
                   jax.ShapeDtypeStruct((N_EXPERTS, LANES), F32),
                   jax.ShapeDtypeStruct((SUBLANES, LANES), F32)),
        grid=(b, s // tm),
        in_specs=[pl.BlockSpec((1, PAIRS, tm, LANES), lambda bi, i: (bi, 0, i, 0)),
                  pl.BlockSpec((1, PAIRS, tm, LANES), lambda bi, i: (bi, 0, i, 0)),
                  pl.BlockSpec((1, tm, d), row),
                  pl.BlockSpec((1, NA_WIDTH), const2),
                  pl.BlockSpec((1, SWA_WIDTH), const2),
                  pl.BlockSpec((NA_WIDTH + SWA_WIDTH, d), const2),
                  pl.BlockSpec((1, d), const2),
                  pl.BlockSpec((1, 1, d), per_b),
                  pl.BlockSpec((1, d), const2),
                  pl.BlockSpec((1, 1, d), per_b),
                  pl.BlockSpec((1, 1, d), per_b),
                  pl.BlockSpec((d, LANES), const2)],
        out_specs=(pl.BlockSpec((1, tm, d), row),
                   pl.BlockSpec((tm * TOKEN_ROWS, LANES), lambda bi, i: (bi * nt + i, 0)),
                   pl.BlockSpec((tm, LANES), lambda bi, i: (bi * nt + i, 0)),
                   pl.BlockSpec((SUBLANES, tm), lambda bi, i: (0, bi * nt + i)),
                   pl.BlockSpec((N_EXPERTS, LANES), const2),
                   pl.BlockSpec((SUBLANES, LANES), const2)),
        compiler_params=_params("arbitrary", "arbitrary"),
        name="mix",
    )(na, sw, x, beta_na.reshape(1, -1), beta_swa.reshape(1, -1), w_out, g_post_mix.reshape(1, d), gate_a,
      g_pre_ffn.reshape(1, d), scale_f, shift_f, w_router3)


MOE_TILE = 256
PLAN_T = 1024
I_TILE, I_EXPERT, I_LO, I_HI, I_FIRST, I_NEW, I_NEXT, I_ORDER = range(8)


def _plan_kernel(rt_ref, cntc_ref, cntr_ref, pos_ref, items_ref, start_ref, carry_ref, *, n_tiles, n_items):
    i = pl.program_id(0)
    sub = lax.broadcasted_iota(jnp.int32, (N_EXPERTS, LANES), 0)
    lane = lax.broadcasted_iota(jnp.int32, (N_EXPERTS, LANES), 1)

    @pl.when(i == 0)
    def _():
        c_col = cntc_ref[:, 0:1]
        c_row = cntr_ref[0:1, :]
        s_col = jnp.sum(jnp.where(lane < sub, c_row, 0.0), axis=1, keepdims=True)
        s_row = jnp.sum(jnp.where(sub < lane, c_col, 0.0), axis=0, keepdims=True)
        start_ref[...] = jnp.broadcast_to(s_col, start_ref.shape)
        carry_ref[...] = jnp.zeros_like(carry_ref)

        def tiles_of(s, c):
            first = jnp.floor(s * (1.0 / MOE_TILE))
            last = jnp.floor((s + c - 1.0) * (1.0 / MOE_TILE))
            return first, jnp.where(c > 0.0, last - first + 1.0, 0.0)

        f_col, n_col = tiles_of(s_col, c_col)
        _, n_row = tiles_of(s_row, c_row)
        i_col = jnp.sum(jnp.where(lane < sub, n_row, 0.0), axis=1, keepdims=True)
        total = jnp.sum(n_col, axis=0, keepdims=True)
        k = lax.broadcasted_iota(jnp.int32, (N_EXPERTS, n_items), 1).astype(F32)
        subk = lax.broadcasted_iota(jnp.int32, (N_EXPERTS, n_items), 0).astype(F32)
        ek = jnp.sum(jnp.where(i_col + n_col <= k, 1.0, 0.0), axis=0, keepdims=True)
        k0 = k[0:1]
        valid = k0 < total
        sel = subk == ek

        def pick(v):
            return jnp.sum(jnp.where(sel, v, 0.0), axis=0, keepdims=True)

        i_k, f_k, s_k, c_k = pick(i_col), pick(f_col), pick(s_col), pick(c_col)
        tile = f_k + (k0 - i_k)
        row0 = tile * MOE_TILE
        lo = jnp.maximum(s_k, row0) - row0
        hi = jnp.minimum(s_k + c_k, row0 + MOE_TILE) - row0
        present = n_col > 0.0
        last_expert = jnp.sum(jnp.where(i_col + n_col <= total - 1.0, 1.0, 0.0), axis=0, keepdims=True)
        nxt = jnp.min(jnp.where(present & (subk > ek), subk, float(N_EXPERTS)), axis=0, keepdims=True)
        order = jnp.sum(jnp.where(present & (subk < ek), 1.0, 0.0), axis=0, keepdims=True)
        rows = [jnp.where(valid, tile, n_tiles - 1.0), jnp.where(valid, ek, last_expert),
                jnp.where(valid, lo, 0.0), jnp.where(valid, hi, 0.0),
                jnp.where(valid & (lo == 0.0), 1.0, 0.0),
                jnp.where(valid & (k0 == i_k), 1.0, 0.0), jnp.where(valid, nxt, float(N_EXPERTS)),
                jnp.where(valid, order, 0.0)]
        assert len(rows) == SUBLANES
        items_ref[...] = jnp.concatenate(rows, axis=0).astype(jnp.int32)

    t = rt_ref.shape[1]
    e1 = rt_ref[R_E1:R_E1 + 1, :]
    e2 = rt_ref[R_E2:R_E2 + 1, :]
    sub_t = lax.broadcasted_iota(jnp.int32, (N_EXPERTS, t), 0).astype(F32)
    oh1 = sub_t == e1
    oh2 = sub_t == e2
    oh = (oh1 | oh2).astype(F32)
    before = (lax.broadcasted_iota(jnp.int32, (t, t), 0) < lax.broadcasted_iota(jnp.int32, (t, t), 1)).astype(BF16)
    rank = jnp.dot(oh.astype(BF16), before, preferred_element_type=F32)
    base = start_ref[:, 0:1] + carry_ref[:, 0:1] + rank
    pos1 = jnp.sum(jnp.where(oh1, base, 0.0), axis=0, keepdims=True)
    pos2 = jnp.sum(jnp.where(oh2, base, 0.0), axis=0, keepdims=True)
    carry_ref[...] += jnp.broadcast_to(jnp.sum(oh, axis=1, keepdims=True), carry_ref.shape)
    pos = jnp.concatenate([pos1, pos2] + [jnp.zeros_like(pos1)] * (SUBLANES - 2), axis=0)
    pos_ref[...] = pos.astype(jnp.int32)


def _plan(rt, cntc, cntr):
    n = rt.shape[1]
    n_tiles = 2 * n // MOE_TILE
    n_items = 2 * LANES
    assert n_tiles + N_EXPERTS <= n_items
    return pl.pallas_call(
        functools.partial(_plan_kernel, n_tiles=n_tiles, n_items=n_items),
        out_shape=(jax.ShapeDtypeStruct((SUBLANES, n), jnp.int32),
                   jax.ShapeDtypeStruct((SUBLANES, n_items), jnp.int32)),
        grid=(n // PLAN_T,),
        in_specs=[pl.BlockSpec((SUBLANES, PLAN_T), lambda i: (0, i)),
                  pl.BlockSpec((N_EXPERTS, LANES), lambda i: (0, 0)),
                  pl.BlockSpec((SUBLANES, LANES), lambda i: (0, 0))],
        out_specs=(pl.BlockSpec((SUBLANES, PLAN_T), lambda i: (0, i)),
                   pl.BlockSpec((SUBLANES, n_items), lambda i: (0, 0))),
        scratch_shapes=[pltpu.VMEM((N_EXPERTS, LANES), F32), pltpu.VMEM((N_EXPERTS, LANES), F32)],
        compiler_params=_params("arbitrary"),
        name="plan",
    )(rt, cntc, cntr)


DISPATCH_T = 1024


def _token_rows(ref, index):
    return ref.at[pl.ds(pl.multiple_of(index * TOKEN_ROWS, TOKEN_ROWS), TOKEN_ROWS), :]


def _dispatch_kernel(pos_ref, h_ref, xs_hbm, sem):
    def body(r, carry):
        src = _token_rows(h_ref, r)
        for k in range(2):
            pltpu.make_async_copy(src, _token_rows(xs_hbm, pos_ref[k, r]), sem.at[0]).start(priority=k)
        return carry

    lax.fori_loop(0, DISPATCH_T, body, 0, unroll=8)
    for k in range(2):
        pltpu.make_async_copy(h_ref, xs_hbm.at[pl.ds(0, DISPATCH_T * TOKEN_ROWS), :], sem.at[0]).wait()


def _dispatch(pos, h2):
    n = pos.shape[1]
    return pl.pallas_call(
        _dispatch_kernel,
        out_shape=jax.ShapeDtypeStruct((2 * n * TOKEN_ROWS, LANES), WORD),
        grid=(n // DISPATCH_T,),
        in_specs=[pl.BlockSpec((SUBLANES, DISPATCH_T), lambda i: (0, i), memory_space=pltpu.SMEM),
                  pl.BlockSpec((DISPATCH_T * TOKEN_ROWS, LANES), lambda i: (i, 0))],
        out_specs=pl.BlockSpec(memory_space=pl.ANY),
        scratch_shapes=[pltpu.SemaphoreType.DMA((1,))],
        compiler_params=_params("arbitrary"),
        name="dispatch",
    )(pos, h2)


def _expert_kernel(tile_ref, exp_ref, lo_ref, hi_ref, first_ref, new_ref, next_ref, order_ref,
                   xs_ref, wg_hbm, wu_hbm, wd_hbm, o_ref, wg_buf, wu_buf, wd_buf, sem):
    k = pl.program_id(0)
    lo = lo_ref[k]
    hi = hi_ref[k]
    slot = order_ref[k] % 2

    def weight_copies(expert, sl):
        return [pltpu.make_async_copy(src.at[expert], dst.at[sl], sem.at[sl])
                for src, dst in ((wg_hbm, wg_buf), (wu_hbm, wu_buf), (wd_hbm, wd_buf))]

    @pl.when(k == 0)
    def _():
        for copy in weight_copies(exp_ref[0], 0):
            copy.start()

    @pl.when(new_ref[k] == 1)
    def _():
        for copy in weight_copies(exp_ref[k], slot):
            copy.wait()

        @pl.when(next_ref[k] < N_EXPERTS)
        def _():
            for copy in weight_copies(next_ref[k], 1 - slot):
                copy.start()

    @pl.when(first_ref[k] == 1)
    def _():
        o_ref[...] = jnp.zeros_like(o_ref)

    @pl.when(hi > lo)
    def _():
        x = _from_token_tiles(xs_ref, MOE_TILE).astype(BF16)
        gate = jnp.dot(x, wg_buf[slot].astype(BF16), preferred_element_type=F32)
        up = jnp.dot(x, wu_buf[slot].astype(BF16), preferred_element_type=F32)
        he = (gate * jax.nn.sigmoid(gate) * up).astype(BF16)
        ye = jnp.dot(he, wd_buf[slot].astype(BF16), preferred_element_type=F32)
        row = lax.broadcasted_iota(jnp.int32, (MOE_TILE, 1), 0)
        mine = (row >= lo) & (row < hi)
        _to_token_tiles(o_ref, jnp.where(mine, ye, _from_token_tiles(o_ref, MOE_TILE)))


def _experts(items, xs, w_gate, w_up, w_down):
    n_rows = xs.shape[0] // TOKEN_ROWS
    n_items = n_rows // MOE_TILE + N_EXPERTS
    d = w_gate.shape[1]
    tile_map = lambda k, tile, *_: (tile[k], 0)
    grid_spec = pltpu.PrefetchScalarGridSpec(
        num_scalar_prefetch=SUBLANES,
        grid=(n_items,),
        in_specs=[pl.BlockSpec((MOE_TILE * TOKEN_ROWS, LANES), tile_map),
                  pl.BlockSpec(memory_space=pl.ANY),
                  pl.BlockSpec(memory_space=pl.ANY),
                  pl.BlockSpec(memory_space=pl.ANY)],
        out_specs=pl.BlockSpec((MOE_TILE * TOKEN_ROWS, LANES), tile_map),
        scratch_shapes=[pltpu.VMEM((2, d, EXPERT_FF), F32), pltpu.VMEM((2, d, EXPERT_FF), F32),
                        pltpu.VMEM((2, EXPERT_FF, d), F32), pltpu.SemaphoreType.DMA((2,))])
    return pl.pallas_call(
        _expert_kernel,
        out_shape=jax.ShapeDtypeStruct(xs.shape, WORD),
        grid_spec=grid_spec,
        compiler_params=_params("arbitrary"),
        name="experts",
    )(*(items[j, :n_items] for j in range(SUBLANES)), xs, w_gate, w_up, w_down)


COMBINE_T = 256


def _combine_kernel(pos_ref, posn_ref, ys_hbm, r_ref, x1_ref, gf_ref, gpost_ref, o_ref, buf, sem):
    i = pl.program_id(0)
    n = pl.num_programs(0)
    slot = i % 2
    slot_rows = COMBINE_T * TOKEN_ROWS

    def start_row(p_ref, sl, r):
        for k in range(2):
            dst = buf.at[sl, pl.ds(pl.multiple_of(k * slot_rows + r * TOKEN_ROWS, TOKEN_ROWS), TOKEN_ROWS), :]
            pltpu.make_async_copy(_token_rows(ys_hbm, p_ref[k, r]), dst, sem.at[sl]).start(priority=k)

    def wait_slot(sl):
        pltpu.make_async_copy(ys_hbm.at[pl.ds(0, 2 * slot_rows), :], buf.at[sl], sem.at[sl]).wait()

    def issue(p_ref, sl):
        def body(r, carry):
            start_row(p_ref, sl, r)
            return carry

        lax.fori_loop(0, COMBINE_T, body, 0, unroll=8)

    @pl.when(i == 0)
    def _():
        issue(pos_ref, 0)

    @pl.when(i + 1 < n)
    def _():
        issue(posn_ref, 1 - slot)

    wait_slot(slot)
    ya = _from_token_tiles(buf.at[slot], COMBINE_T)
    yb = _from_token_tiles(buf.at[slot], COMBINE_T, base=slot_rows)
    r = r_ref[...]
    y = r[:, R_W1:R_W1 + 1] * ya + r[:, R_W2:R_W2 + 1] * yb
    o_ref[0] = x1_ref[0] + gf_ref[0] * (_rms(y) * gpost_ref[...])


def _combine(pos, ys, r, x1, gate_f, g_post_ffn):
    b, s, d = x1.shape
    nt = s // COMBINE_T
    n_steps = b * nt
    row = lambda i: (i // nt, i % nt, 0)
    pos_spec = lambda f: pl.BlockSpec((SUBLANES, COMBINE_T), lambda i: (0, f(i)), memory_space=pltpu.SMEM)
    return pl.pallas_call(
        _combine_kernel,
        out_shape=jax.ShapeDtypeStruct((b, s, d), F32),
        grid=(n_steps,),
        in_specs=[pos_spec(lambda i: i),
                  pos_spec(lambda i: jnp.minimum(i + 1, n_steps - 1)),
                  pl.BlockSpec(memory_space=pl.ANY),
                  pl.BlockSpec((COMBINE_T, LANES), lambda i: (i, 0)),
                  pl.BlockSpec((1, COMBINE_T, d), row),
                  pl.BlockSpec((1, 1, d), lambda i: (i // nt, 0, 0)),
                  pl.BlockSpec((1, d), lambda i: (0, 0))],
        out_specs=pl.BlockSpec((1, COMBINE_T, d), row),
        scratch_shapes=[pltpu.VMEM((2, 2 * COMBINE_T * TOKEN_ROWS, LANES), WORD), pltpu.SemaphoreType.DMA((2,))],
        compiler_params=_params("arbitrary"),
        name="combine",
    )(pos, pos, ys, r, x1, gate_f, g_post_ffn.reshape(1, d))


def _rope_tables(s):
    half = HEAD_DIM // 2
    inv = ROPE_THETA ** (-np.arange(half, dtype=np.float64) * 2.0 / HEAD_DIM)
    ang = np.arange(s, dtype=np.float64)[:, None] * inv[None, :]
    cos = np.cos(ang)
    sin = np.sin(ang)
    cos = np.concatenate([cos, cos, cos, cos], axis=-1)
    sin_signed = np.concatenate([-sin, sin, -sin, sin], axis=-1)
    return jnp.asarray(cos, F32), jnp.asarray(sin_signed, F32)


def _router_weights(w_group_router, w_expert_router):
    d = w_group_router.shape[0]
    we = jnp.transpose(w_expert_router, (1, 0, 2)).reshape(d, N_EXPERTS)
    hi, lo = _split2(jnp.concatenate([w_group_router, we], axis=-1))
    return jnp.concatenate([hi, lo, jnp.zeros((d, LANES - 2 * ROUTER_COLS), BF16)], axis=-1)


def kernel(x, c, w_ada, b_ada, g_pre_mix, w_in, na_rpb, swa_sinks, beta_na, beta_swa, w_out, g_post_mix, g_pre_ffn,
           w_group_router, w_expert_router, w_gate, w_up, w_down, g_post_ffn):
    b, s, d = x.shape
    depth = w_ada.shape[0]
    cos, sin_signed = _rope_tables(s)
    for l in range(depth):
        mod = _adaln(c, w_ada[l], b_ada[l]).reshape(b, N_MOD, 1, d)
        shift_a, scale_a, gate_a, shift_f, scale_f, gate_f = (mod[:, k] for k in range(N_MOD))
        qkv, na_kg, na_vg = _qkv(x, g_pre_mix[l], scale_a, shift_a, w_in[l].astype(BF16), cos, sin_signed)
        na = _na(qkv, na_kg, na_vg, na_rpb[l])
        sw = _swa(qkv, swa_sinks[l])
        x1, h2, r, rt, cntc, cntr = _mix(na, sw, x, beta_na[l], beta_swa[l], w_out[l].astype(BF16), g_post_mix[l],
                                          gate_a, g_pre_ffn[l], scale_f, shift_f,
                                          _router_weights(w_group_router[l], w_expert_router[l]))
        pos, items = _plan(rt, cntc, cntr)
        xs = _dispatch(pos, h2)
        ys = _experts(items, xs, w_gate[l], w_up[l], w_down[l])
        x = _combine(pos, ys, r, x1, gate_f, g_post_ffn[l])
    return x
```

```python
import functools

import jax
import jax.numpy as jnp
import numpy as np
from jax import lax
from jax.experimental import pallas as pl
from jax.experimental.pallas import tpu as pltpu

D_MODEL = 1024
GRID_W = 64
HEAD_DIM = 64
NA_HEADS = 8
NA_KH = 8
NA_KW = 16
SWA_HEADS = 8
SWA_KV_HEADS = 2
SWA_WINDOW = 128
SWA_BLOCK = 128
ROPE_THETA = 10000.0
NA_WIDTH = NA_HEADS * HEAD_DIM
SWA_WIDTH = SWA_HEADS * HEAD_DIM
N_GROUPS = 4
EXPERTS_PER_GROUP = 8
N_EXPERTS = N_GROUPS * EXPERTS_PER_GROUP
EXPERT_FF = 256
N_MOD = 6
EPS = 1e-6
NEG_INF = -1e30

LANES = 128
SUBLANES = 8
R_E1, R_E2, R_W1, R_W2 = range(4)
PAIRS = NA_HEADS // 2
W_NQ, W_NK, W_NV, W_SQ, W_SKV = 0, 4, 8, 12, 16
COL_NQ, COL_SQ, COL_SK, COL_SV = 0, 4, 8, 10
QKV_TILES = 12
NA_GROUPS = GRID_W // NA_KW
NA_WIN = 2 * NA_KW
NA_WIN_START = tuple(min(max(NA_KW * g - NA_KW // 2, 0), GRID_W - NA_WIN) for g in range(NA_GROUPS))
ROUTER_BASE = N_GROUPS
VMEM_LIMIT = 56 * 1024 * 1024

F32 = jnp.float32
BF16 = jnp.bfloat16
LOG2E = 1.4426950408889634


def _rms(v):
    return v * lax.rsqrt(jnp.mean(v * v, axis=-1, keepdims=True) + EPS)


def _params(*sem):
    return pltpu.CompilerParams(dimension_semantics=sem, vmem_limit_bytes=VMEM_LIMIT)


def _adaln_kernel(c_ref, w_ref, b_ref, o_ref, *, batch):
    c = c_ref[...]
    a_t = jnp.transpose(c * jax.nn.sigmoid(c))
    w = w_ref[...]
    rows = [jnp.sum(a_t[:, bi:bi + 1] * w, axis=0, keepdims=True) for bi in range(batch)]
    rows.append(jnp.zeros((c.shape[0] - batch, w.shape[1]), F32))
    o_ref[...] = jnp.concatenate(rows, axis=0) + b_ref[...]


def _adaln(c, w_ada, b_ada):
    batch, d = c.shape
    n = w_ada.shape[1]
    tn = 1024
    b = SUBLANES
    assert batch < b
    c = jnp.pad(c, ((0, b - batch), (0, 0)))
    return pl.pallas_call(
        functools.partial(_adaln_kernel, batch=batch),
        out_shape=jax.ShapeDtypeStruct((b, n), F32),
        grid=(n // tn,),
        in_specs=[pl.BlockSpec((b, d), lambda j: (0, 0)),
                  pl.BlockSpec((d, tn), lambda j: (0, j)),
                  pl.BlockSpec((1, tn), lambda j: (0, j))],
        out_specs=pl.BlockSpec((b, tn), lambda j: (0, j)),
        compiler_params=_params("arbitrary"),
        name="adaln",
    )(c, w_ada, b_ada.reshape(1, n))[:batch]


def _rope(v, cos, sin_signed, first_half):
    rot = jnp.where(first_half, pltpu.roll(v, LANES - HEAD_DIM // 2, 1), pltpu.roll(v, HEAD_DIM // 2, 1))
    return v * cos + rot * sin_signed


QKV_CHUNK = 256


def _column_windows(v):
    grid_rows = v.shape[0] // GRID_W
    return [jnp.concatenate([v[r * GRID_W + w0:r * GRID_W + w0 + NA_WIN] for r in range(grid_rows)], axis=0)
            for w0 in NA_WIN_START]


def _qkv_kernel(x_ref, g_ref, sc_ref, sh_ref, w_ref, cos_ref, sin_ref, o_ref, kg_ref, vg_ref):
    scale = HEAD_DIM ** -0.5 * LOG2E
    lane = lax.broadcasted_iota(jnp.int32, (QKV_CHUNK, LANES), 1)
    first_half = (lane % HEAD_DIM) < HEAD_DIM // 2
    upper = lane >= HEAD_DIM

    def tile(v, j):
        return v[:, j * LANES:(j + 1) * LANES]

    for c in range(x_ref.shape[1] // QKV_CHUNK):
        rows = slice(c * QKV_CHUNK, (c + 1) * QKV_CHUNK)
        h = (_rms(x_ref[0, rows, :]) * g_ref[...]) * (1.0 + sc_ref[0]) + sh_ref[0]
        h = h.astype(BF16)
        cos = cos_ref[rows, :]
        sin = sin_ref[rows, :]

        def proj(col, width):
            return jnp.dot(h, w_ref[:, col * LANES:(col + width) * LANES], preferred_element_type=F32)

        nq, nk, nv, sq = proj(W_NQ, 4), proj(W_NK, 4), proj(W_NV, 4), proj(W_SQ, 4)
        win_rows = slice(c * QKV_CHUNK // 2, (c + 1) * QKV_CHUNK // 2)
        for j in range(PAIRS):
            o_ref[0, COL_NQ + j, rows, :] = (tile(nq, j) * scale).astype(BF16)
            o_ref[0, COL_SQ + j, rows, :] = (_rope(tile(sq, j), cos, sin, first_half) * scale).astype(BF16)
            for ref, val in ((kg_ref, tile(nk, j)), (vg_ref, tile(nv, j))):
                for g, win in enumerate(_column_windows(val)):
                    ref[0, j, g, win_rows, :] = win.astype(BF16)
        skv = proj(W_SKV, 2)
        k = _rope(tile(skv, 0), cos, sin, first_half)
        v = tile(skv, 1)
        for t, col in ((k, COL_SK), (v, COL_SV)):
            swapped = pltpu.roll(t, HEAD_DIM, 1)
            o_ref[0, col, rows, :] = jnp.where(upper, swapped, t).astype(BF16)
            o_ref[0, col + 1, rows, :] = jnp.where(upper, t, swapped).astype(BF16)


def _qkv(x, g, scale_a, shift_a, w_in, cos, sin):
    b, s, d = x.shape
    tm = 1024
    n_in = w_in.shape[1]
    windows = jax.ShapeDtypeStruct((b, PAIRS, NA_GROUPS, s // 2, LANES), BF16)
    windows_spec = pl.BlockSpec((1, PAIRS, NA_GROUPS, tm // 2, LANES), lambda bi, i: (bi, 0, 0, i, 0))
    return pl.pallas_call(
        _qkv_kernel,
        out_shape=(jax.ShapeDtypeStruct((b, QKV_TILES, s, LANES), BF16), windows, windows),
        grid=(b, s // tm),
        in_specs=[pl.BlockSpec((1, tm, d), lambda bi, i: (bi, i, 0)),
                  pl.BlockSpec((1, d), lambda bi, i: (0, 0)),
                  pl.BlockSpec((1, 1, d), lambda bi, i: (bi, 0, 0)),
                  pl.BlockSpec((1, 1, d), lambda bi, i: (bi, 0, 0)),
                  pl.BlockSpec((d, n_in), lambda bi, i: (0, 0)),
                  pl.BlockSpec((tm, LANES), lambda bi, i: (i, 0)),
                  pl.BlockSpec((tm, LANES), lambda bi, i: (i, 0))],
        out_specs=(pl.BlockSpec((1, QKV_TILES, tm, LANES), lambda bi, i: (bi, 0, i, 0)), windows_spec, windows_spec),
        compiler_params=_params("arbitrary", "arbitrary"),
        name="qkv",
    )(x, g.reshape(1, d), scale_a, shift_a, w_in, cos, sin)


NA_QROWS = 8
NA_KROWS = NA_QROWS + NA_KH
NA_BLOCKS_PER_STEP = 4
NA_INTERLEAVE = 4
NA_Q = NA_QROWS * NA_KW
NA_K = NA_KROWS * NA_WIN
NA_RPB_ROWS = 2 * NA_KH - 1
NA_RPB_COLS = 2 * NA_KW - 1


def _clamp(v, lo, hi):
    return min(max(v, lo), hi)


def _na_first_key_row(block, rows, clip):
    return clip(block * NA_QROWS - NA_KH // 2, 0, rows - NA_KROWS)


def _na_group_tables():
    def geometry(g):
        cols = [NA_KW * g + cc for cc in range(NA_KW)]
        return (NA_WIN_START[g] - NA_KW * g,) + tuple(_clamp(c - NA_KW // 2, 0, GRID_W - NA_KW) - c for c in cols)

    seen, table_of_group, representatives = {}, [], []
    for g in range(NA_GROUPS):
        key = geometry(g)
        if key not in seen:
            seen[key] = len(representatives)
            representatives.append(g)
        table_of_group.append(seen[key])
    return table_of_group, representatives


def _na_block_types(rows):
    def geometry(block):
        r = block * NA_QROWS
        a = _na_first_key_row(block, rows, _clamp)
        return (a - r,) + tuple(_clamp(r + j - NA_KH // 2, 0, rows - NA_KH) - r for j in range(NA_QROWS))

    n_blocks = rows // NA_QROWS
    interior = geometry(n_blocks // 2)
    lead = next(b for b in range(n_blocks) if geometry(b) == interior)
    trail = next(b for b in range(n_blocks) if geometry(n_blocks - 1 - b) == interior)
    assert all(geometry(b) == interior for b in range(lead, n_blocks - trail))
    return lead, trail


def _na_kernel(q_ref, k_ref, v_ref, bias_ref, o_ref, *, rows):
    upper = lax.broadcasted_iota(jnp.int32, (NA_Q, LANES), 1) >= HEAD_DIM
    key_upper = lax.broadcasted_iota(jnp.int32, (NA_K, LANES), 1) >= HEAD_DIM
    lead, trail = _na_block_types(rows)
    first_trailing = rows // NA_QROWS - trail
    n_types = lead + trail + 1
    table_of_group, _ = _na_group_tables()
    for c0 in range(0, NA_BLOCKS_PER_STEP, NA_INTERLEAVE):
        blocks = range(c0, c0 + NA_INTERLEAVE)
        starts, types, pieces = {}, {}, {}
        for c in blocks:
            block = pl.program_id(2) * NA_BLOCKS_PER_STEP + c
            a = _na_first_key_row(block, rows, jnp.clip)
            types[c] = jnp.where(block < lead, block,
                                 jnp.where(block >= first_trailing, block - first_trailing + lead + 1, lead))
            starts[c] = pl.multiple_of(a * NA_WIN, NA_WIN)
            for g in range(NA_GROUPS):
                pieces[c, g] = [slice((c * NA_QROWS + rr) * GRID_W + g * NA_KW,
                                      (c * NA_QROWS + rr) * GRID_W + (g + 1) * NA_KW) for rr in range(NA_QROWS)]
        chains = [(c, g, hh) for c in blocks for g in range(NA_GROUPS) for hh in range(2)]
        scores, probs, outs = {}, {}, {}
        for c, g, hh in chains:
            q = jnp.concatenate([q_ref[0, 0, rws, :] for rws in pieces[c, g]], axis=0)
            qm = jnp.where(upper if hh else ~upper, q, jnp.zeros_like(q))
            ks = k_ref[0, 0, g, pl.ds(starts[c], NA_K), :]
            s = lax.dot_general(qm, ks, (((1,), (1,)), ((), ())), preferred_element_type=F32)
            scores[c, g, hh] = s + bias_ref[hh, table_of_group[g] * n_types + types[c]]
        for chain in chains:
            s = scores[chain]
            probs[chain] = jnp.exp2(s - jnp.max(s, axis=-1, keepdims=True)).astype(BF16)
        for c, g, hh in chains:
            vs = v_ref[0, 0, g, pl.ds(starts[c], NA_K), :]
            v1 = jnp.where(key_upper if hh else ~key_upper, vs, jnp.ones_like(vs))
            o = jnp.dot(probs[c, g, hh], v1, preferred_element_type=F32)
            outs[c, g, hh] = o / o[:, (1 - hh) * HEAD_DIM:(1 - hh) * HEAD_DIM + 1]
        for c in blocks:
            for g in range(NA_GROUPS):
                out = jnp.where(upper, outs[c, g, 1], outs[c, g, 0]).astype(BF16)
                for rr, rws in enumerate(pieces[c, g]):
                    o_ref[0, 0, rws, :] = out[rr * NA_KW:(rr + 1) * NA_KW]


def _na_bias_kernel(rpb_ref, o_ref, *, rows):
    h = pl.program_id(0)
    cc = lax.broadcasted_iota(jnp.int32, (NA_KW, LANES), 0)
    lane = lax.broadcasted_iota(jnp.int32, (NA_KW, LANES), 1)
    w = lane % NA_WIN
    key_row_in_tile = lane // NA_WIN
    rows_per_tile = LANES // NA_WIN
    neg = jnp.full((NA_KW, LANES), NEG_INF, F32)
    base = h * NA_RPB_ROWS * NA_RPB_COLS
    lead, trail = _na_block_types(rows)
    n_blocks = rows // NA_QROWS
    type_blocks = list(range(lead + 1)) + list(range(n_blocks - trail, n_blocks))
    for table, g in enumerate(_na_group_tables()[1]):
        qc = g * NA_KW + cc
        kc = NA_WIN_START[g] + w
        c0 = jnp.clip(qc - NA_KW // 2, 0, GRID_W - NA_KW)
        in_cols = (kc >= c0) & (kc < c0 + NA_KW)
        dc = kc - qc + NA_KW - 1
        by_row_offset = []
        for d in range(NA_RPB_ROWS):
            acc = neg
            for dd in range(NA_RPB_COLS):
                acc = jnp.where(dc == dd, rpb_ref[base + d * NA_RPB_COLS + dd], acc)
            by_row_offset.append(jnp.where(in_cols, acc * LOG2E, NEG_INF))
        for ty, block in enumerate(type_blocks):
            r = block * NA_QROWS
            a = _na_first_key_row(block, rows, _clamp)
            for j in range(NA_QROWS):
                r0 = _clamp(r + j - NA_KH // 2, 0, rows - NA_KH)
                for t in range(NA_K // LANES):
                    tile = neg
                    for part in range(rows_per_tile):
                        i = t * rows_per_tile + part
                        if r0 <= a + i < r0 + NA_KH:
                            tile = jnp.where(key_row_in_tile == part, by_row_offset[a + i - (r + j) + NA_KH - 1], tile)
                    o_ref[0, table * len(type_blocks) + ty, j * NA_KW:(j + 1) * NA_KW, t * LANES:(t + 1) * LANES] = tile


def _na_bias(rpb, rows):
    n_tables = len(_na_group_tables()[1]) * (sum(_na_block_types(rows)) + 1)
    return pl.pallas_call(
        functools.partial(_na_bias_kernel, rows=rows),
        out_shape=jax.ShapeDtypeStruct((NA_HEADS, n_tables, NA_Q, NA_K), F32),
        grid=(NA_HEADS,),
        in_specs=[pl.BlockSpec(memory_space=pltpu.SMEM)],
        out_specs=pl.BlockSpec((1, n_tables, NA_Q, NA_K), lambda h: (h, 0, 0, 0)),
        compiler_params=_params("arbitrary"),
        name="na_bias",
    )(rpb.astype(F32).reshape(-1))


def _na(qkv, kg, vg, rpb):
    b, _, s, _ = qkv.shape
    rows = s // GRID_W
    assert rows % (NA_QROWS * NA_BLOCKS_PER_STEP) == 0
    tq = NA_BLOCKS_PER_STEP * NA_QROWS * GRID_W
    bias = _na_bias(rpb, rows)
    windows_spec = pl.BlockSpec((1, 1) + kg.shape[2:], lambda bi, p, i: (bi, p, 0, 0, 0))
    return pl.pallas_call(
        functools.partial(_na_kernel, rows=rows),
        out_shape=jax.ShapeDtypeStruct((b, PAIRS, s, LANES), BF16),
        grid=(b, PAIRS, s // tq),
        in_specs=[pl.BlockSpec((1, 1, tq, LANES), lambda bi, p, i: (bi, COL_NQ + p, i, 0)),
                  windows_spec,
                  windows_spec,
                  pl.BlockSpec((2,) + bias.shape[1:], lambda bi, p, i: (p, 0, 0, 0))],
        out_specs=pl.BlockSpec((1, 1, tq, LANES), lambda bi, p, i: (bi, p, i, 0)),
        compiler_params=_params("arbitrary", "arbitrary", "arbitrary"),
        name="na",
    )(qkv, kg, vg, bias)


SWA_KEYS = 3 * SWA_BLOCK
SWA_BLOCKS_PER_STEP = 8
SWA_INTERLEAVE = 2
SWA_GROUP = SWA_HEADS // SWA_KV_HEADS
assert COL_SQ % (PAIRS // SWA_KV_HEADS) == 0


def _swa_masks():
    v = np.arange(SWA_KEYS // SWA_BLOCK)[:, None, None]
    q = np.arange(SWA_BLOCK)[None, :, None]
    k = np.arange(SWA_KEYS)[None, None, :]
    return np.where(np.abs(k - v * SWA_BLOCK - q) <= SWA_WINDOW, 0.0, NEG_INF).astype(np.float32)


def _swa_kernel(sink_ref, mask_ref, q_ref, k_ref, v_ref, o_ref, *, seq):
    kv = pl.program_id(1)
    rows = SWA_GROUP * SWA_BLOCK
    lane = lax.broadcasted_iota(jnp.int32, (SWA_BLOCK, LANES), 1)
    upper = lane >= HEAD_DIM
    head = lax.broadcasted_iota(jnp.int32, (rows, 1), 0) // SWA_BLOCK
    sink = jnp.zeros((rows, 1), F32)
    for g in range(SWA_GROUP):
        sink = jnp.where(head == g, sink_ref[kv * SWA_GROUP + g], sink)
    sink = sink * LOG2E
    for j0 in range(0, SWA_BLOCKS_PER_STEP, SWA_INTERLEAVE):
        blocks = range(j0, j0 + SWA_INTERLEAVE)
        starts, scores, probs = {}, {}, {}
        for j in blocks:
            n = pl.program_id(2) * SWA_BLOCKS_PER_STEP + j
            start = pl.multiple_of(jnp.clip((n - 1) * SWA_BLOCK, 0, seq - SWA_KEYS), SWA_BLOCK)
            starts[j] = start
            ks = k_ref[0, 0, pl.ds(start, SWA_KEYS), :]
            qs = []
            for g in range(SWA_GROUP):
                q = q_ref[0, g // 2, j * SWA_BLOCK:(j + 1) * SWA_BLOCK, :]
                qs.append(jnp.where(upper if g % 2 else ~upper, q, jnp.zeros_like(q)))
            s = lax.dot_general(jnp.concatenate(qs, axis=0), ks, (((1,), (1,)), ((), ())), preferred_element_type=F32)
            mask = mask_ref[(n * SWA_BLOCK - start) // SWA_BLOCK]
            scores[j] = s + jnp.concatenate([mask] * SWA_GROUP, axis=0)
        for j in blocks:
            s = scores[j]
            m = jnp.maximum(jnp.max(s, axis=-1, keepdims=True), sink)
            e = jnp.exp2(s - m)
            probs[j] = (e.astype(BF16), jnp.sum(e, axis=-1, keepdims=True) + jnp.exp2(sink - m))
        for j in blocks:
            e, l = probs[j]
            vs = v_ref[0, 0, pl.ds(starts[j], SWA_KEYS), :]
            o = jnp.dot(e, vs, preferred_element_type=F32) / l
            for pair in range(SWA_GROUP // 2):
                even = o[(2 * pair) * SWA_BLOCK:(2 * pair + 1) * SWA_BLOCK]
                odd = o[(2 * pair + 1) * SWA_BLOCK:(2 * pair + 2) * SWA_BLOCK]
                o_ref[0, pair, j * SWA_BLOCK:(j + 1) * SWA_BLOCK, :] = jnp.where(upper, odd, even).astype(BF16)


def _swa(qkv, sinks):
    b, _, s, _ = qkv.shape
    pairs_per_kv = PAIRS // SWA_KV_HEADS
    tq = SWA_BLOCKS_PER_STEP * SWA_BLOCK
    masks = _swa_masks()
    return pl.pallas_call(
        functools.partial(_swa_kernel, seq=s),
        out_shape=jax.ShapeDtypeStruct((b, PAIRS, s, LANES), BF16),
        grid=(b, SWA_KV_HEADS, s // tq),
        in_specs=[pl.BlockSpec(memory_space=pltpu.SMEM),
                  pl.BlockSpec(masks.shape, lambda bi, kv, n: (0, 0, 0)),
                  pl.BlockSpec((1, pairs_per_kv, tq, LANES),
                               lambda bi, kv, n: (bi, COL_SQ // pairs_per_kv + kv, n, 0)),
                  pl.BlockSpec((1, 1, s, LANES), lambda bi, kv, n: (bi, COL_SK + kv, 0, 0)),
                  pl.BlockSpec((1, 1, s, LANES), lambda bi, kv, n: (bi, COL_SV + kv, 0, 0))],
        out_specs=pl.BlockSpec((1, pairs_per_kv, tq, LANES), lambda bi, kv, n: (bi, kv, n, 0)),
        compiler_params=_params("arbitrary", "arbitrary", "arbitrary"),
        name="swa",
    )(sinks, jnp.asarray(masks), qkv, qkv, qkv)


ROUTER_COLS = N_GROUPS + N_EXPERTS


def _split2(v):
    hi = v.astype(BF16)
    lo = (v - hi.astype(F32)).astype(BF16)
    return hi, lo


ROUTER_ROWS = 40


def _route(logits_t):
    sub = lax.broadcasted_iota(jnp.int32, logits_t.shape, 0)
    big = jnp.int32(LANES)
    gmask = sub < N_GROUPS
    gl = jnp.where(gmask, logits_t, NEG_INF)
    gmax = jnp.max(gl, axis=0, keepdims=True)
    g_top = jnp.min(jnp.where(gmask & (gl == gmax), sub, big), axis=0, keepdims=True)
    g_weight = 1.0 / jnp.sum(jnp.where(gmask, jnp.exp(gl - gmax), 0.0), axis=0, keepdims=True)
    lo = ROUTER_BASE + g_top * EXPERTS_PER_GROUP
    emask = (sub >= lo) & (sub < lo + EXPERTS_PER_GROUP)
    el = jnp.where(emask, logits_t, NEG_INF)
    m1 = jnp.max(el, axis=0, keepdims=True)
    i1 = jnp.min(jnp.where(emask & (el == m1), sub, big), axis=0, keepdims=True)
    emask2 = emask & (sub != i1)
    el2 = jnp.where(emask2, logits_t, NEG_INF)
    m2 = jnp.max(el2, axis=0, keepdims=True)
    i2 = jnp.min(jnp.where(emask2 & (el2 == m2), sub, big), axis=0, keepdims=True)
    e2 = jnp.exp(m2 - m1)
    w1 = g_weight / (1.0 + e2)
    w2 = g_weight * e2 / (1.0 + e2)
    return i1 - ROUTER_BASE, i2 - ROUTER_BASE, w1, w2


TOKEN_ROWS = D_MODEL // (2 * LANES)
WORD = jnp.uint32


def _to_token_tiles(ref, v, base=0):
    t, d = v.shape
    words = pltpu.pack_elementwise([v[:, :d // 2], v[:, d // 2:]], packed_dtype=BF16)
    for s in range(TOKEN_ROWS):
        ref[pl.ds(base + s, t, stride=TOKEN_ROWS), :] = words[:, s * LANES:(s + 1) * LANES]


def _token_words(ref, t, base=0):
    return jnp.concatenate([ref[pl.ds(base + s, t, stride=TOKEN_ROWS), :] for s in range(TOKEN_ROWS)], axis=-1)


def _from_token_tiles(ref, t, base=0):
    words = _token_words(ref, t, base)
    halves = [pltpu.unpack_elementwise(words, index=j, packed_dtype=BF16, unpacked_dtype=F32) for j in range(2)]
    return jnp.concatenate(halves, axis=-1)


MIX_CHUNK = 256


def _mix_kernel(na_ref, sw_ref, x_ref, bna_ref, bsw_ref, wo_ref, gpm_ref, ga_ref, gpf_ref, scf_ref, shf_ref,
                wr_ref, x1_ref, h2_ref, r_ref, rt_ref, cntc_ref, cntr_ref):
    first_step = (pl.program_id(0) == 0) & (pl.program_id(1) == 0)

    @pl.when(first_step)
    def _():
        cntc_ref[...] = jnp.zeros_like(cntc_ref)
        cntr_ref[...] = jnp.zeros_like(cntr_ref)

    t = MIX_CHUNK
    chunks = range(x_ref.shape[1] // t)
    row_slices = [slice(c * t, (c + 1) * t) for c in chunks]
    mixes, h2s, all_logits = [], [], []
    for rows in row_slices:
        def heads(ref):
            return jnp.concatenate([ref[0, j, rows, :] for j in range(PAIRS)], axis=-1).astype(F32)

        na = (_rms(heads(na_ref)) * bna_ref[...]).astype(BF16)
        sw = (_rms(heads(sw_ref)) * bsw_ref[...]).astype(BF16)
        mixes.append(jnp.dot(na, wo_ref[:NA_WIDTH, :], preferred_element_type=F32)
                     + jnp.dot(sw, wo_ref[NA_WIDTH:, :], preferred_element_type=F32))
    gate_gain = ga_ref[0] * gpm_ref[...]
    ffn_gain = gpf_ref[...] * (1.0 + scf_ref[0])
    for c, rows in zip(chunks, row_slices):
        x1 = x_ref[0, rows, :] + _rms(mixes[c]) * gate_gain
        x1_ref[0, rows, :] = x1
        h2 = _rms(x1) * ffn_gain + shf_ref[0]
        _to_token_tiles(h2_ref, h2, base=c * t * TOKEN_ROWS)
        h2s.append(h2)
    for c in chunks:
        h_hi, h_lo = _split2(h2s[c])
        both = (jnp.dot(h_hi, wr_ref[...], preferred_element_type=F32)
                + jnp.dot(h_lo, wr_ref[...], preferred_element_type=F32))
        all_logits.append(both + pltpu.roll(both, LANES - ROUTER_COLS, 1))
    for c, rows in zip(chunks, row_slices):
        e1, e2, w1, w2 = _route(jnp.transpose(all_logits[c])[:ROUTER_ROWS])
        fields = [None] * 4
        fields[R_E1], fields[R_E2], fields[R_W1], fields[R_W2] = e1.astype(F32), e2.astype(F32), w1, w2
        rt = jnp.concatenate(fields + [jnp.zeros((SUBLANES - 4, t), F32)], axis=0)
        rt_ref[:, rows] = rt
        r = jnp.transpose(jnp.concatenate([rt, jnp.zeros((LANES - SUBLANES, t), F32)], axis=0))
        r_ref[rows, :] = r
        sub = lax.broadcasted_iota(jnp.int32, (N_EXPERTS, t), 0)
        on_sub = ((sub == e1) | (sub == e2)).astype(F32)
        cntc_ref[...] += jnp.broadcast_to(jnp.sum(on_sub, axis=1, keepdims=True), cntc_ref.shape)
        lane = lax.broadcasted_iota(jnp.int32, r.shape, 1).astype(F32)
        on_lane = ((lane == r[:, R_E1:R_E1 + 1]) | (lane == r[:, R_E2:R_E2 + 1])).astype(F32)
        cntr_ref[...] += jnp.broadcast_to(jnp.sum(on_lane, axis=0, keepdims=True), cntr_ref.shape)


def _mix(na, sw, x, beta_na, beta_swa, w_out, g_post_mix, gate_a, g_pre_ffn, scale_f, shift_f, w_router3):
    b, s, d = x.shape
    tm = 1024
    nt = s // tm
    row = lambda bi, i: (bi, i, 0)
    const2 = lambda bi, i: (0, 0)
    per_b = lambda bi, i: (bi, 0, 0)
    return pl.pallas_call(
        _mix_kernel,
        out_shape=(jax.ShapeDtypeStruct((b, s, d), F32),
                   jax.ShapeDtypeStruct((b * s * TOKEN_ROWS, LANES), WORD),
                   jax.ShapeDtypeStruct((b * s, LANES), F32),
                   jax.ShapeDtypeStruct((SUBLANES, b * s), F32),
                   jax.ShapeDtypeStruct((N_EXPERTS, LANES), F32),
                   jax.ShapeDtypeStruct((SUBLANES, LANES), F32)),
        grid=(b, s // tm),
        in_specs=[pl.BlockSpec((1, PAIRS, tm, LANES), lambda bi, i: (bi, 0, i, 0)),
                  pl.BlockSpec((1, PAIRS, tm, LANES), lambda bi, i: (bi, 0, i, 0)),
                  pl.BlockSpec((1, tm, d), row),
                  pl.BlockSpec((1, NA_WIDTH), const2),
                  pl.BlockSpec((1, SWA_WIDTH), const2),
                  pl.BlockSpec((NA_WIDTH + SWA_WIDTH, d), const2),
                  pl.BlockSpec((1, d), const2),
                  pl.BlockSpec((1, 1, d), per_b),
                  pl.BlockSpec((1, d), const2),
                  pl.BlockSpec((1, 1, d), per_b),
                  pl.BlockSpec((1, 1, d), per_b),
                  pl.BlockSpec((d, LANES), const2)],
        out_specs=(pl.BlockSpec((1, tm, d), row),
                   pl.BlockSpec((tm * TOKEN_ROWS, LANES), lambda bi, i: (bi * nt + i, 0)),
                   pl.BlockSpec((tm, LANES), lambda bi, i: (bi * nt + i, 0)),
                   pl.BlockSpec((SUBLANES, tm), lambda bi, i: (0, bi * nt + i)),
                   pl.BlockSpec((N_EXPERTS, LANES), const2),
                   pl.BlockSpec((SUBLANES, LANES), const2)),
        compiler_params=_params("arbitrary", "arbitrary"),
        name="mix",
    )(na, sw, x, beta_na.reshape(1, -1), beta_swa.reshape(1, -1), w_out, g_post_mix.reshape(1, d), gate_a,
      g_pre_ffn.reshape(1, d), scale_f, shift_f, w_router3)


MOE_TILE = 256
PLAN_T = 1024
I_TILE, I_EXPERT, I_LO, I_HI, I_FIRST, I_NEW, I_NEXT, I_ORDER = range(8)


def _plan_kernel(rt_ref, cntc_ref, cntr_ref, pos_ref, items_ref, start_ref, carry_ref, *, n_tiles, n_items):
    i = pl.program_id(0)
    sub = lax.broadcasted_iota(jnp.int32, (N_EXPERTS, LANES), 0)
    lane = lax.broadcasted_iota(jnp.int32, (N_EXPERTS, LANES), 1)

    @pl.when(i == 0)
    def _():
        c_col = cntc_ref[:, 0:1]
        c_row = cntr_ref[0:1, :]
        s_col = jnp.sum(jnp.where(lane < sub, c_row, 0.0), axis=1, keepdims=True)
        s_row = jnp.sum(jnp.where(sub < lane, c_col, 0.0), axis=0, keepdims=True)
        start_ref[...] = jnp.broadcast_to(s_col, start_ref.shape)
        carry_ref[...] = jnp.zeros_like(carry_ref)

        def tiles_of(s, c):
            first = jnp.floor(s * (1.0 / MOE_TILE))
            last = jnp.floor((s + c - 1.0) * (1.0 / MOE_TILE))
            return first, jnp.where(c > 0.0, last - first + 1.0, 0.0)

        f_col, n_col = tiles_of(s_col, c_col)
        _, n_row = tiles_of(s_row, c_row)
        i_col = jnp.sum(jnp.where(lane < sub, n_row, 0.0), axis=1, keepdims=True)
        total = jnp.sum(n_col, axis=0, keepdims=True)
        k = lax.broadcasted_iota(jnp.int32, (N_EXPERTS, n_items), 1).astype(F32)
        subk = lax.broadcasted_iota(jnp.int32, (N_EXPERTS, n_items), 0).astype(F32)
        ek = jnp.sum(jnp.where(i_col + n_col <= k, 1.0, 0.0), axis=0, keepdims=True)
        k0 = k[0:1]
        valid = k0 < total
        sel = subk == ek

        def pick(v):
            return jnp.sum(jnp.where(sel, v, 0.0), axis=0, keepdims=True)

        i_k, f_k, s_k, c_k = pick(i_col), pick(f_col), pick(s_col), pick(c_col)
        tile = f_k + (k0 - i_k)
        row0 = tile * MOE_TILE
        lo = jnp.maximum(s_k, row0) - row0
        hi = jnp.minimum(s_k + c_k, row0 + MOE_TILE) - row0
        present = n_col > 0.0
        last_expert = jnp.sum(jnp.where(i_col + n_col <= total - 1.0, 1.0, 0.0), axis=0, keepdims=True)
        nxt = jnp.min(jnp.where(present & (subk > ek), subk, float(N_EXPERTS)), axis=0, keepdims=True)
        order = jnp.sum(jnp.where(present & (subk < ek), 1.0, 0.0), axis=0, keepdims=True)
        rows = [jnp.where(valid, tile, n_tiles - 1.0), jnp.where(valid, ek, last_expert),
                jnp.where(valid, lo, 0.0), jnp.where(valid, hi, 0.0),
                jnp.where(valid & (lo == 0.0), 1.0, 0.0),
                jnp.where(valid & (k0 == i_k), 1.0, 0.0), jnp.where(valid, nxt, float(N_EXPERTS)),
                jnp.where(valid, order, 0.0)]
        assert len(rows) == SUBLANES
        items_ref[...] = jnp.concatenate(rows, axis=0).astype(jnp.int32)

    t = rt_ref.shape[1]
    e1 = rt_ref[R_E1:R_E1 + 1, :]
    e2 = rt_ref[R_E2:R_E2 + 1, :]
    sub_t = lax.broadcasted_iota(jnp.int32, (N_EXPERTS, t), 0).astype(F32)
    oh1 = sub_t == e1
    oh2 = sub_t == e2
    oh = (oh1 | oh2).astype(F32)
    before = (lax.broadcasted_iota(jnp.int32, (t, t), 0) < lax.broadcasted_iota(jnp.int32, (t, t), 1)).astype(BF16)
    rank = jnp.dot(oh.astype(BF16), before, preferred_element_type=F32)
    base = start_ref[:, 0:1] + carry_ref[:, 0:1] + rank
    pos1 = jnp.sum(jnp.where(oh1, base, 0.0), axis=0, keepdims=True)
    pos2 = jnp.sum(jnp.where(oh2, base, 0.0), axis=0, keepdims=True)
    carry_ref[...] += jnp.broadcast_to(jnp.sum(oh, axis=1, keepdims=True), carry_ref.shape)
    pos = jnp.concatenate([pos1, pos2] + [jnp.zeros_like(pos1)] * (SUBLANES - 2), axis=0)
    pos_ref[...] = pos.astype(jnp.int32)


def _plan(rt, cntc, cntr):
    n = rt.shape[1]
    n_tiles = 2 * n // MOE_TILE
    n_items = 2 * LANES
    assert n_tiles + N_EXPERTS <= n_items
    return pl.pallas_call(
        functools.partial(_plan_kernel, n_tiles=n_tiles, n_items=n_items),
        out_shape=(jax.ShapeDtypeStruct((SUBLANES, n), jnp.int32),
                   jax.ShapeDtypeStruct((SUBLANES, n_items), jnp.int32)),
        grid=(n // PLAN_T,),
        in_specs=[pl.BlockSpec((SUBLANES, PLAN_T), lambda i: (0, i)),
                  pl.BlockSpec((N_EXPERTS, LANES), lambda i: (0, 0)),
                  pl.BlockSpec((SUBLANES, LANES), lambda i: (0, 0))],
        out_specs=(pl.BlockSpec((SUBLANES, PLAN_T), lambda i: (0, i)),
                   pl.BlockSpec((SUBLANES, n_items), lambda i: (0, 0))),
        scratch_shapes=[pltpu.VMEM((N_EXPERTS, LANES), F32), pltpu.VMEM((N_EXPERTS, LANES), F32)],
        compiler_params=_params("arbitrary"),
        name="plan",
    )(rt, cntc, cntr)


DISPATCH_T = 1024


def _token_rows(ref, index):
    return ref.at[pl.ds(pl.multiple_of(index * TOKEN_ROWS, TOKEN_ROWS), TOKEN_ROWS), :]


def _dispatch_kernel(pos_ref, h_ref, xs_hbm, sem):
    def body(r, carry):
        src = _token_rows(h_ref, r)
        for k in range(2):
            pltpu.make_async_copy(src, _token_rows(xs_hbm, pos_ref[k, r]), sem.at[0]).start(priority=k)
        return carry

    lax.fori_loop(0, DISPATCH_T, body, 0, unroll=8)
    for k in range(2):
        pltpu.make_async_copy(h_ref, xs_hbm.at[pl.ds(0, DISPATCH_T * TOKEN_ROWS), :], sem.at[0]).wait()


def _dispatch(pos, h2):
    n = pos.shape[1]
    return pl.pallas_call(
        _dispatch_kernel,
        out_shape=jax.ShapeDtypeStruct((2 * n * TOKEN_ROWS, LANES), WORD),
        grid=(n // DISPATCH_T,),
        in_specs=[pl.BlockSpec((SUBLANES, DISPATCH_T), lambda i: (0, i), memory_space=pltpu.SMEM),
                  pl.BlockSpec((DISPATCH_T * TOKEN_ROWS, LANES), lambda i: (i, 0))],
        out_specs=pl.BlockSpec(memory_space=pl.ANY),
        scratch_shapes=[pltpu.SemaphoreType.DMA((1,))],
        compiler_params=_params("arbitrary"),
        name="dispatch",
    )(pos, h2)


def _expert_kernel(tile_ref, exp_ref, lo_ref, hi_ref, first_ref, new_ref, next_ref, order_ref,
                   xs_ref, wg_hbm, wu_hbm, wd_hbm, o_ref, wg_buf, wu_buf, wd_buf, sem):
    k = pl.program_id(0)
    lo = lo_ref[k]
    hi = hi_ref[k]
    slot = order_ref[k] % 2

    def weight_copies(expert, sl):
        return [pltpu.make_async_copy(src.at[expert], dst.at[sl], sem.at[sl])
                for src, dst in ((wg_hbm, wg_buf), (wu_hbm, wu_buf), (wd_hbm, wd_buf))]

    @pl.when(k == 0)
    def _():
        for copy in weight_copies(exp_ref[0], 0):
            copy.start()

    @pl.when(new_ref[k] == 1)
    def _():
        for copy in weight_copies(exp_ref[k], slot):
            copy.wait()

        @pl.when(next_ref[k] < N_EXPERTS)
        def _():
            for copy in weight_copies(next_ref[k], 1 - slot):
                copy.start()

    @pl.when(first_ref[k] == 1)
    def _():
        o_ref[...] = jnp.zeros_like(o_ref)

    @pl.when(hi > lo)
    def _():
        x = _from_token_tiles(xs_ref, MOE_TILE).astype(BF16)
        gate = jnp.dot(x, wg_buf[slot].astype(BF16), preferred_element_type=F32)
        up = jnp.dot(x, wu_buf[slot].astype(BF16), preferred_element_type=F32)
        he = (gate * jax.nn.sigmoid(gate) * up).astype(BF16)
        ye = jnp.dot(he, wd_buf[slot].astype(BF16), preferred_element_type=F32)
        row = lax.broadcasted_iota(jnp.int32, (MOE_TILE, 1), 0)
        mine = (row >= lo) & (row < hi)
        _to_token_tiles(o_ref, jnp.where(mine, ye, _from_token_tiles(o_ref, MOE_TILE)))


def _experts(items, xs, w_gate, w_up, w_down):
    n_rows = xs.shape[0] // TOKEN_ROWS
    n_items = n_rows // MOE_TILE + N_EXPERTS
    d = w_gate.shape[1]
    tile_map = lambda k, tile, *_: (tile[k], 0)
    grid_spec = pltpu.PrefetchScalarGridSpec(
        num_scalar_prefetch=SUBLANES,
        grid=(n_items,),
        in_specs=[pl.BlockSpec((MOE_TILE * TOKEN_ROWS, LANES), tile_map),
                  pl.BlockSpec(memory_space=pl.ANY),
                  pl.BlockSpec(memory_space=pl.ANY),
                  pl.BlockSpec(memory_space=pl.ANY)],
        out_specs=pl.BlockSpec((MOE_TILE * TOKEN_ROWS, LANES), tile_map),
        scratch_shapes=[pltpu.VMEM((2, d, EXPERT_FF), F32), pltpu.VMEM((2, d, EXPERT_FF), F32),
                        pltpu.VMEM((2, EXPERT_FF, d), F32), pltpu.SemaphoreType.DMA((2,))])
    return pl.pallas_call(
        _expert_kernel,
        out_shape=jax.ShapeDtypeStruct(xs.shape, WORD),
        grid_spec=grid_spec,
        compiler_params=_params("arbitrary"),
        name="experts",
    )(*(items[j, :n_items] for j in range(SUBLANES)), xs, w_gate, w_up, w_down)


COMBINE_T = 256


def _combine_kernel(pos_ref, posn_ref, ys_hbm, r_ref, x1_ref, gf_ref, gpost_ref, o_ref, buf, sem):
    i = pl.program_id(0)
    n = pl.num_programs(0)
    slot = i % 2
    slot_rows = COMBINE_T * TOKEN_ROWS

    def start_row(p_ref, sl, r):
        for k in range(2):
            dst = buf.at[sl, pl.ds(pl.multiple_of(k * slot_rows + r * TOKEN_ROWS, TOKEN_ROWS), TOKEN_ROWS), :]
            pltpu.make_async_copy(_token_rows(ys_hbm, p_ref[k, r]), dst, sem.at[sl]).start(priority=k)

    def wait_slot(sl):
        pltpu.make_async_copy(ys_hbm.at[pl.ds(0, 2 * slot_rows), :], buf.at[sl], sem.at[sl]).wait()

    def issue(p_ref, sl):
        def body(r, carry):
            start_row(p_ref, sl, r)
            return carry

        lax.fori_loop(0, COMBINE_T, body, 0, unroll=8)

    @pl.when(i == 0)
    def _():
        issue(pos_ref, 0)

    @pl.when(i + 1 < n)
    def _():
        issue(posn_ref, 1 - slot)

    wait_slot(slot)
    ya = _from_token_tiles(buf.at[slot], COMBINE_T)
    yb = _from_token_tiles(buf.at[slot], COMBINE_T, base=slot_rows)
    r = r_ref[...]
    y = r[:, R_W1:R_W1 + 1] * ya + r[:, R_W2:R_W2 + 1] * yb
    o_ref[0] = x1_ref[0] + gf_ref[0] * (_rms(y) * gpost_ref[...])


def _combine(pos, ys, r, x1, gate_f, g_post_ffn):
    b, s, d = x1.shape
    nt = s // COMBINE_T
    n_steps = b * nt
    row = lambda i: (i // nt, i % nt, 0)
    pos_spec = lambda f: pl.BlockSpec((SUBLANES, COMBINE_T), lambda i: (0, f(i)), memory_space=pltpu.SMEM)
    return pl.pallas_call(
        _combine_kernel,
        out_shape=jax.ShapeDtypeStruct((b, s, d), F32),
        grid=(n_steps,),
        in_specs=[pos_spec(lambda i: i),
                  pos_spec(lambda i: jnp.minimum(i + 1, n_steps - 1)),
                  pl.BlockSpec(memory_space=pl.ANY),
                  pl.BlockSpec((COMBINE_T, LANES), lambda i: (i, 0)),
                  pl.BlockSpec((1, COMBINE_T, d), row),
                  pl.BlockSpec((1, 1, d), lambda i: (i // nt, 0, 0)),
                  pl.BlockSpec((1, d), lambda i: (0, 0))],
        out_specs=pl.BlockSpec((1, COMBINE_T, d), row),
        scratch_shapes=[pltpu.VMEM((2, 2 * COMBINE_T * TOKEN_ROWS, LANES), WORD), pltpu.SemaphoreType.DMA((2,))],
        compiler_params=_params("arbitrary"),
        name="combine",
    )(pos, pos, ys, r, x1, gate_f, g_post_ffn.reshape(1, d))


def _rope_tables(s):
    half = HEAD_DIM // 2
    inv = ROPE_THETA ** (-np.arange(half, dtype=np.float64) * 2.0 / HEAD_DIM)
    ang = np.arange(s, dtype=np.float64)[:, None] * inv[None, :]
    cos = np.cos(ang)
    sin = np.sin(ang)
    cos = np.concatenate([cos, cos, cos, cos], axis=-1)
    sin_signed = np.concatenate([-sin, sin, -sin, sin], axis=-1)
    return jnp.asarray(cos, F32), jnp.asarray(sin_signed, F32)


def _router_weights(w_group_router, w_expert_router):
    d = w_group_router.shape[0]
    we = jnp.transpose(w_expert_router, (1, 0, 2)).reshape(d, N_EXPERTS)
    hi, lo = _split2(jnp.concatenate([w_group_router, we], axis=-1))
    return jnp.concatenate([hi, lo, jnp.zeros((d, LANES - 2 * ROUTER_COLS), BF16)], axis=-1)


def kernel(x, c, w_ada, b_ada, g_pre_mix, w_in, na_rpb, swa_sinks, beta_na, beta_swa, w_out, g_post_mix, g_pre_ffn,
           w_group_router, w_expert_router, w_gate, w_up, w_down, g_post_ffn):
    b, s, d = x.shape
    depth = w_ada.shape[0]
    cos, sin_signed = _rope_tables(s)
    for l in range(depth):
        mod = _adaln(c, w_ada[l], b_ada[l]).reshape(b, N_MOD, 1, d)
        shift_a, scale_a, gate_a, shift_f, scale_f, gate_f = (mod[:, k] for k in range(N_MOD))
        qkv, na_kg, na_vg = _qkv(x, g_pre_mix[l], scale_a, shift_a, w_in[l].astype(BF16), cos, sin_signed)
        na = _na(qkv, na_kg, na_vg, na_rpb[l])
        sw = _swa(qkv, swa_sinks[l])
        x1, h2, r, rt, cntc, cntr = _mix(na, sw, x, beta_na[l], beta_swa[l], w_out[l].astype(BF16), g_post_mix[l],
                                          gate_a, g_pre_ffn[l], scale_f, shift_f,
                                          _router_weights(w_group_router[l], w_expert_router[l]))
        pos, items = _plan(rt, cntc, cntr)
        xs = _dispatch(pos, h2)
        ys = _experts(items, xs, w_gate[l], w_up[l], w_down[l])
        x = _combine(pos, ys, r, x1, gate_f, g_post_ffn[l])
    return x
```

```python
import functools

import jax
import jax.numpy as jnp
import numpy as np
from jax import lax
from jax.experimental import pallas as pl
from jax.experimental.pallas import tpu as pltpu

D_MODEL = 1024
GRID_W = 64
HEAD_DIM = 64
NA_HEADS = 8
NA_KH = 8
NA_KW = 16
SWA_HEADS = 8
SWA_KV_HEADS = 2
SWA_WINDOW = 128
SWA_BLOCK = 128
ROPE_THETA = 10000.0
NA_WIDTH = NA_HEADS * HEAD_DIM
SWA_WIDTH = SWA_HEADS * HEAD_DIM
N_GROUPS = 4
EXPERTS_PER_GROUP = 8
N_EXPERTS = N_GROUPS * EXPERTS_PER_GROUP
EXPERT_FF = 256
N_MOD = 6
EPS = 1e-6
NEG_INF = -1e30

LANES = 128
SUBLANES = 8
R_E1, R_E2, R_W1, R_W2 = range(4)
PAIRS = NA_HEADS // 2
W_NQ, W_NK, W_NV, W_SQ, W_SKV = 0, 4, 8, 12, 16
COL_NQ, COL_SQ, COL_SK, COL_SV = 0, 4, 8, 10
QKV_TILES = 12
NA_GROUPS = GRID_W // NA_KW
NA_WIN = 2 * NA_KW
NA_WIN_START = tuple(min(max(NA_KW * g - NA_KW // 2, 0), GRID_W - NA_WIN) for g in range(NA_GROUPS))
ROUTER_BASE = N_GROUPS
VMEM_LIMIT = 56 * 1024 * 1024

F32 = jnp.float32
BF16 = jnp.bfloat16
LOG2E = 1.4426950408889634


def _rms(v):
    return v * lax.rsqrt(jnp.mean(v * v, axis=-1, keepdims=True) + EPS)


def _params(*sem):
    return pltpu.CompilerParams(dimension_semantics=sem, vmem_limit_bytes=VMEM_LIMIT)


def _adaln_kernel(c_ref, w_ref, b_ref, o_ref, *, batch):
    c = c_ref[...]
    a_t = jnp.transpose(c * jax.nn.sigmoid(c))
    w = w_ref[...]
    rows = [jnp.sum(a_t[:, bi:bi + 1] * w, axis=0, keepdims=True) for bi in range(batch)]
    rows.append(jnp.zeros((c.shape[0] - batch, w.shape[1]), F32))
    o_ref[...] = jnp.concatenate(rows, axis=0) + b_ref[...]


def _adaln(c, w_ada, b_ada):
    batch, d = c.shape
    n = w_ada.shape[1]
    tn = 1024
    b = SUBLANES
    assert batch < b
    c = jnp.pad(c, ((0, b - batch), (0, 0)))
    return pl.pallas_call(
        functools.partial(_adaln_kernel, batch=batch),
        out_shape=jax.ShapeDtypeStruct((b, n), F32),
        grid=(n // tn,),
        in_specs=[pl.BlockSpec((b, d), lambda j: (0, 0)),
                  pl.BlockSpec((d, tn), lambda j: (0, j)),
                  pl.BlockSpec((1, tn), lambda j: (0, j))],
        out_specs=pl.BlockSpec((b, tn), lambda j: (0, j)),
        compiler_params=_params("arbitrary"),
        name="adaln",
    )(c, w_ada, b_ada.reshape(1, n))[:batch]


def _rope(v, cos, sin_signed, first_half):
    rot = jnp.where(first_half, pltpu.roll(v, LANES - HEAD_DIM // 2, 1), pltpu.roll(v, HEAD_DIM // 2, 1))
    return v * cos + rot * sin_signed


QKV_CHUNK = 256


def _column_windows(v):
    grid_rows = v.shape[0] // GRID_W
    return [jnp.concatenate([v[r * GRID_W + w0:r * GRID_W + w0 + NA_WIN] for r in range(grid_rows)], axis=0)
            for w0 in NA_WIN_START]


def _qkv_kernel(x_ref, g_ref, sc_ref, sh_ref, w_ref, cos_ref, sin_ref, o_ref, kg_ref, vg_ref):
    scale = HEAD_DIM ** -0.5 * LOG2E
    lane = lax.broadcasted_iota(jnp.int32, (QKV_CHUNK, LANES), 1)
    first_half = (lane % HEAD_DIM) < HEAD_DIM // 2
    upper = lane >= HEAD_DIM

    def tile(v, j):
        return v[:, j * LANES:(j + 1) * LANES]

    for c in range(x_ref.shape[1] // QKV_CHUNK):
        rows = slice(c * QKV_CHUNK, (c + 1) * QKV_CHUNK)
        h = (_rms(x_ref[0, rows, :]) * g_ref[...]) * (1.0 + sc_ref[0]) + sh_ref[0]
        h = h.astype(BF16)
        cos = cos_ref[rows, :]
        sin = sin_ref[rows, :]

        def proj(col, width):
            return jnp.dot(h, w_ref[:, col * LANES:(col + width) * LANES], preferred_element_type=F32)

        nq, nk, nv, sq = proj(W_NQ, 4), proj(W_NK, 4), proj(W_NV, 4), proj(W_SQ, 4)
        win_rows = slice(c * QKV_CHUNK // 2, (c + 1) * QKV_CHUNK // 2)
        for j in range(PAIRS):
            o_ref[0, COL_NQ + j, rows, :] = (tile(nq, j) * scale).astype(BF16)
            o_ref[0, COL_SQ + j, rows, :] = (_rope(tile(sq, j), cos, sin, first_half) * scale).astype(BF16)
            for ref, val in ((kg_ref, tile(nk, j)), (vg_ref, tile(nv, j))):
                for g, win in enumerate(_column_windows(val)):
                    ref[0, j, g, win_rows, :] = win.astype(BF16)
        skv = proj(W_SKV, 2)
        k = _rope(tile(skv, 0), cos, sin, first_half)
        v = tile(skv, 1)
        for t, col in ((k, COL_SK), (v, COL_SV)):
            swapped = pltpu.roll(t, HEAD_DIM, 1)
            o_ref[0, col, rows, :] = jnp.where(upper, swapped, t).astype(BF16)
            o_ref[0, col + 1, rows, :] = jnp.where(upper, t, swapped).astype(BF16)


def _qkv(x, g, scale_a, shift_a, w_in, cos, sin):
    b, s, d = x.shape
    tm = 1024
    n_in = w_in.shape[1]
    windows = jax.ShapeDtypeStruct((b, PAIRS, NA_GROUPS, s // 2, LANES), BF16)
    windows_spec = pl.BlockSpec((1, PAIRS, NA_GROUPS, tm // 2, LANES), lambda bi, i: (bi, 0, 0, i, 0))
    return pl.pallas_call(
        _qkv_kernel,
        out_shape=(jax.ShapeDtypeStruct((b, QKV_TILES, s, LANES), BF16), windows, windows),
        grid=(b, s // tm),
        in_specs=[pl.BlockSpec((1, tm, d), lambda bi, i: (bi, i, 0)),
                  pl.BlockSpec((1, d), lambda bi, i: (0, 0)),
                  pl.BlockSpec((1, 1, d), lambda bi, i: (bi, 0, 0)),
                  pl.BlockSpec((1, 1, d), lambda bi, i: (bi, 0, 0)),
                  pl.BlockSpec((d, n_in), lambda bi, i: (0, 0)),
                  pl.BlockSpec((tm, LANES), lambda bi, i: (i, 0)),
                  pl.BlockSpec((tm, LANES), lambda bi, i: (i, 0))],
        out_specs=(pl.BlockSpec((1, QKV_TILES, tm, LANES), lambda bi, i: (bi, 0, i, 0)), windows_spec, windows_spec),
        compiler_params=_params("arbitrary", "arbitrary"),
        name="qkv",
    )(x, g.reshape(1, d), scale_a, shift_a, w_in, cos, sin)


NA_QROWS = 8
NA_KROWS = NA_QROWS + NA_KH
NA_BLOCKS_PER_STEP = 4
NA_INTERLEAVE = 4
NA_Q = NA_QROWS * NA_KW
NA_K = NA_KROWS * NA_WIN
NA_RPB_ROWS = 2 * NA_KH - 1
NA_RPB_COLS = 2 * NA_KW - 1


def _clamp(v, lo, hi):
    return min(max(v, lo), hi)


def _na_first_key_row(block, rows, clip):
    return clip(block * NA_QROWS - NA_KH // 2, 0, rows - NA_KROWS)


def _na_group_tables():
    def geometry(g):
        cols = [NA_KW * g + cc for cc in range(NA_KW)]
        return (NA_WIN_START[g] - NA_KW * g,) + tuple(_clamp(c - NA_KW // 2, 0, GRID_W - NA_KW) - c for c in cols)

    seen, table_of_group, representatives = {}, [], []
    for g in range(NA_GROUPS):
        key = geometry(g)
        if key not in seen:
            seen[key] = len(representatives)
            representatives.append(g)
        table_of_group.append(seen[key])
    return table_of_group, representatives


def _na_block_types(rows):
    def geometry(block):
        r = block * NA_QROWS
        a = _na_first_key_row(block, rows, _clamp)
        return (a - r,) + tuple(_clamp(r + j - NA_KH // 2, 0, rows - NA_KH) - r for j in range(NA_QROWS))

    n_blocks = rows // NA_QROWS
    interior = geometry(n_blocks // 2)
    lead = next(b for b in range(n_blocks) if geometry(b) == interior)
    trail = next(b for b in range(n_blocks) if geometry(n_blocks - 1 - b) == interior)
    assert all(geometry(b) == interior for b in range(lead, n_blocks - trail))
    return lead, trail


def _na_kernel(q_ref, k_ref, v_ref, bias_ref, o_ref, *, rows):
    upper = lax.broadcasted_iota(jnp.int32, (NA_Q, LANES), 1) >= HEAD_DIM
    key_upper = lax.broadcasted_iota(jnp.int32, (NA_K, LANES), 1) >= HEAD_DIM
    lead, trail = _na_block_types(rows)
    first_trailing = rows // NA_QROWS - trail
    n_types = lead + trail + 1
    table_of_group, _ = _na_group_tables()
    for c0 in range(0, NA_BLOCKS_PER_STEP, NA_INTERLEAVE):
        blocks = range(c0, c0 + NA_INTERLEAVE)
        starts, types, pieces = {}, {}, {}
        for c in blocks:
            block = pl.program_id(2) * NA_BLOCKS_PER_STEP + c
            a = _na_first_key_row(block, rows, jnp.clip)
            types[c] = jnp.where(block < lead, block,
                                 jnp.where(block >= first_trailing, block - first_trailing + lead + 1, lead))
            starts[c] = pl.multiple_of(a * NA_WIN, NA_WIN)
            for g in range(NA_GROUPS):
                pieces[c, g] = [slice((c * NA_QROWS + rr) * GRID_W + g * NA_KW,
                                      (c * NA_QROWS + rr) * GRID_W + (g + 1) * NA_KW) for rr in range(NA_QROWS)]
        chains = [(c, g, hh) for c in blocks for g in range(NA_GROUPS) for hh in range(2)]
        scores, probs, outs = {}, {}, {}
        for c, g, hh in chains:
            q = jnp.concatenate([q_ref[0, 0, rws, :] for rws in pieces[c, g]], axis=0)
            qm = jnp.where(upper if hh else ~upper, q, jnp.zeros_like(q))
            ks = k_ref[0, 0, g, pl.ds(starts[c], NA_K), :]
            s = lax.dot_general(qm, ks, (((1,), (1,)), ((), ())), preferred_element_type=F32)
            scores[c, g, hh] = s + bias_ref[hh, table_of_group[g] * n_types + types[c]]
        for chain in chains:
            s = scores[chain]
            probs[chain] = jnp.exp2(s - jnp.max(s, axis=-1, keepdims=True)).astype(BF16)
        for c, g, hh in chains:
            vs = v_ref[0, 0, g, pl.ds(starts[c], NA_K), :]
            v1 = jnp.where(key_upper if hh else ~key_upper, vs, jnp.ones_like(vs))
            o = jnp.dot(probs[c, g, hh], v1, preferred_element_type=F32)
            outs[c, g, hh] = o / o[:, (1 - hh) * HEAD_DIM:(1 - hh) * HEAD_DIM + 1]
        for c in blocks:
            for g in range(NA_GROUPS):
                out = jnp.where(upper, outs[c, g, 1], outs[c, g, 0]).astype(BF16)
                for rr, rws in enumerate(pieces[c, g]):
                    o_ref[0, 0, rws, :] = out[rr * NA_KW:(rr + 1) * NA_KW]


def _na_bias_kernel(rpb_ref, o_ref, *, rows):
    h = pl.program_id(0)
    cc = lax.broadcasted_iota(jnp.int32, (NA_KW, LANES), 0)
    lane = lax.broadcasted_iota(jnp.int32, (NA_KW, LANES), 1)
    w = lane % NA_WIN
    key_row_in_tile = lane // NA_WIN
    rows_per_tile = LANES // NA_WIN
    neg = jnp.full((NA_KW, LANES), NEG_INF, F32)
    base = h * NA_RPB_ROWS * NA_RPB_COLS
    lead, trail = _na_block_types(rows)
    n_blocks = rows // NA_QROWS
    type_blocks = list(range(lead + 1)) + list(range(n_blocks - trail, n_blocks))
    for table, g in enumerate(_na_group_tables()[1]):
        qc = g * NA_KW + cc
        kc = NA_WIN_START[g] + w
        c0 = jnp.clip(qc - NA_KW // 2, 0, GRID_W - NA_KW)
        in_cols = (kc >= c0) & (kc < c0 + NA_KW)
        dc = kc - qc + NA_KW - 1
        by_row_offset = []
        for d in range(NA_RPB_ROWS):
            acc = neg
            for dd in range(NA_RPB_COLS):
                acc = jnp.where(dc == dd, rpb_ref[base + d * NA_RPB_COLS + dd], acc)
            by_row_offset.append(jnp.where(in_cols, acc * LOG2E, NEG_INF))
        for ty, block in enumerate(type_blocks):
            r = block * NA_QROWS
            a = _na_first_key_row(block, rows, _clamp)
            for j in range(NA_QROWS):
                r0 = _clamp(r + j - NA_KH // 2, 0, rows - NA_KH)
                for t in range(NA_K // LANES):
                    tile = neg
                    for part in range(rows_per_tile):
                        i = t * rows_per_tile + part
                        if r0 <= a + i < r0 + NA_KH:
                            tile = jnp.where(key_row_in_tile == part, by_row_offset[a + i - (r + j) + NA_KH - 1], tile)
                    o_ref[0, table * len(type_blocks) + ty, j * NA_KW:(j + 1) * NA_KW, t * LANES:(t + 1) * LANES] = tile


def _na_bias(rpb, rows):
    n_tables = len(_na_group_tables()[1]) * (sum(_na_block_types(rows)) + 1)
    return pl.pallas_call(
        functools.partial(_na_bias_kernel, rows=rows),
        out_shape=jax.ShapeDtypeStruct((NA_HEADS, n_tables, NA_Q, NA_K), F32),
        grid=(NA_HEADS,),
        in_specs=[pl.BlockSpec(memory_space=pltpu.SMEM)],
        out_specs=pl.BlockSpec((1, n_tables, NA_Q, NA_K), lambda h: (h, 0, 0, 0)),
        compiler_params=_params("arbitrary"),
        name="na_bias",
    )(rpb.astype(F32).reshape(-1))


def _na(qkv, kg, vg, rpb):
    b, _, s, _ = qkv.shape
    rows = s // GRID_W
    assert rows % (NA_QROWS * NA_BLOCKS_PER_STEP) == 0
    tq = NA_BLOCKS_PER_STEP * NA_QROWS * GRID_W
    bias = _na_bias(rpb, rows)
    windows_spec = pl.BlockSpec((1, 1) + kg.shape[2:], lambda bi, p, i: (bi, p, 0, 0, 0))
    return pl.pallas_call(
        functools.partial(_na_kernel, rows=rows),
        out_shape=jax.ShapeDtypeStruct((b, PAIRS, s, LANES), BF16),
        grid=(b, PAIRS, s // tq),
        in_specs=[pl.BlockSpec((1, 1, tq, LANES), lambda bi, p, i: (bi, COL_NQ + p, i, 0)),
                  windows_spec,
                  windows_spec,
                  pl.BlockSpec((2,) + bias.shape[1:], lambda bi, p, i: (p, 0, 0, 0))],
        out_specs=pl.BlockSpec((1, 1, tq, LANES), lambda bi, p, i: (bi, p, i, 0)),
        compiler_params=_params("arbitrary", "arbitrary", "arbitrary"),
        name="na",
    )(qkv, kg, vg, bias)


SWA_KEYS = 3 * SWA_BLOCK
SWA_BLOCKS_PER_STEP = 8
SWA_INTERLEAVE = 2
SWA_GROUP = SWA_HEADS // SWA_KV_HEADS
assert COL_SQ % (PAIRS // SWA_KV_HEADS) == 0


def _swa_masks():
    v = np.arange(SWA_KEYS // SWA_BLOCK)[:, None, None]
    q = np.arange(SWA_BLOCK)[None, :, None]
    k = np.arange(SWA_KEYS)[None, None, :]
    return np.where(np.abs(k - v * SWA_BLOCK - q) <= SWA_WINDOW, 0.0, NEG_INF).astype(np.float32)


def _swa_kernel(sink_ref, mask_ref, q_ref, k_ref, v_ref, o_ref, *, seq):
    kv = pl.program_id(1)
    rows = SWA_GROUP * SWA_BLOCK
    lane = lax.broadcasted_iota(jnp.int32, (SWA_BLOCK, LANES), 1)
    upper = lane >= HEAD_DIM
    head = lax.broadcasted_iota(jnp.int32, (rows, 1), 0) // SWA_BLOCK
    sink = jnp.zeros((rows, 1), F32)
    for g in range(SWA_GROUP):
        sink = jnp.where(head == g, sink_ref[kv * SWA_GROUP + g], sink)
    sink = sink * LOG2E
    for j0 in range(0, SWA_BLOCKS_PER_STEP, SWA_INTERLEAVE):
        blocks = range(j0, j0 + SWA_INTERLEAVE)
        starts, scores, probs = {}, {}, {}
        for j in blocks:
            n = pl.program_id(2) * SWA_BLOCKS_PER_STEP + j
            start = pl.multiple_of(jnp.clip((n - 1) * SWA_BLOCK, 0, seq - SWA_KEYS), SWA_BLOCK)
            starts[j] = start
            ks = k_ref[0, 0, pl.ds(start, SWA_KEYS), :]
            qs = []
            for g in range(SWA_GROUP):
                q = q_ref[0, g // 2, j * SWA_BLOCK:(j + 1) * SWA_BLOCK, :]
                qs.append(jnp.where(upper if g % 2 else ~upper, q, jnp.zeros_like(q)))
            s = lax.dot_general(jnp.concatenate(qs, axis=0), ks, (((1,), (1,)), ((), ())), preferred_element_type=F32)
            mask = mask_ref[(n * SWA_BLOCK - start) // SWA_BLOCK]
            scores[j] = s + jnp.concatenate([mask] * SWA_GROUP, axis=0)
        for j in blocks:
            s = scores[j]
            m = jnp.maximum(jnp.max(s, axis=-1, keepdims=True), sink)
            e = jnp.exp2(s - m)
            probs[j] = (e.astype(BF16), jnp.sum(e, axis=-1, keepdims=True) + jnp.exp2(sink - m))
        for j in blocks:
            e, l = probs[j]
            vs = v_ref[0, 0, pl.ds(starts[j], SWA_KEYS), :]
            o = jnp.dot(e, vs, preferred_element_type=F32) / l
            for pair in range(SWA_GROUP // 2):
                even = o[(2 * pair) * SWA_BLOCK:(2 * pair + 1) * SWA_BLOCK]
                odd = o[(2 * pair + 1) * SWA_BLOCK:(2 * pair + 2) * SWA_BLOCK]
                o_ref[0, pair, j * SWA_BLOCK:(j + 1) * SWA_BLOCK, :] = jnp.where(upper, odd, even).astype(BF16)


def _swa(qkv, sinks):
    b, _, s, _ = qkv.shape
    pairs_per_kv = PAIRS // SWA_KV_HEADS
    tq = SWA_BLOCKS_PER_STEP * SWA_BLOCK
    masks = _swa_masks()
    return pl.pallas_call(
        functools.partial(_swa_kernel, seq=s),
        out_shape=jax.ShapeDtypeStruct((b, PAIRS, s, LANES), BF16),
        grid=(b, SWA_KV_HEADS, s // tq),
        in_specs=[pl.BlockSpec(memory_space=pltpu.SMEM),
                  pl.BlockSpec(masks.shape, lambda bi, kv, n: (0, 0, 0)),
                  pl.BlockSpec((1, pairs_per_kv, tq, LANES),
                               lambda bi, kv, n: (bi, COL_SQ // pairs_per_kv + kv, n, 0)),
                  pl.BlockSpec((1, 1, s, LANES), lambda bi, kv, n: (bi, COL_SK + kv, 0, 0)),
                  pl.BlockSpec((1, 1, s, LANES), lambda bi, kv, n: (bi, COL_SV + kv, 0, 0))],
        out_specs=pl.BlockSpec((1, pairs_per_kv, tq, LANES), lambda bi, kv, n: (bi, kv, n, 0)),
        compiler_params=_params("arbitrary", "arbitrary", "arbitrary"),
        name="swa",
    )(sinks, jnp.asarray(masks), qkv, qkv, qkv)


ROUTER_COLS = N_GROUPS + N_EXPERTS


def _split2(v):
    hi = v.astype(BF16)
    lo = (v - hi.astype(F32)).astype(BF16)
    return hi, lo


ROUTER_ROWS = 40


def _route(logits_t):
    sub = lax.broadcasted_iota(jnp.int32, logits_t.shape, 0)
    big = jnp.int32(LANES)
    gmask = sub < N_GROUPS
    gl = jnp.where(gmask, logits_t, NEG_INF)
    gmax = jnp.max(gl, axis=0, keepdims=True)
    g_top = jnp.min(jnp.where(gmask & (gl == gmax), sub, big), axis=0, keepdims=True)
    g_weight = 1.0 / jnp.sum(jnp.where(gmask, jnp.exp(gl - gmax), 0.0), axis=0, keepdims=True)
    lo = ROUTER_BASE + g_top * EXPERTS_PER_GROUP
    emask = (sub >= lo) & (sub < lo + EXPERTS_PER_GROUP)
    el = jnp.where(emask, logits_t, NEG_INF)
    m1 = jnp.max(el, axis=0, keepdims=True)
    i1 = jnp.min(jnp.where(emask & (el == m1), sub, big), axis=0, keepdims=True)
    emask2 = emask & (sub != i1)
    el2 = jnp.where(emask2, logits_t, NEG_INF)
    m2 = jnp.max(el2, axis=0, keepdims=True)
    i2 = jnp.min(jnp.where(emask2 & (el2 == m2), sub, big), axis=0, keepdims=True)
    e2 = jnp.exp(m2 - m1)
    w1 = g_weight / (1.0 + e2)
    w2 = g_weight * e2 / (1.0 + e2)
    return i1 - ROUTER_BASE, i2 - ROUTER_BASE, w1, w2


TOKEN_ROWS = D_MODEL // (2 * LANES)
WORD = jnp.uint32


def _to_token_tiles(ref, v, base=0):
    t, d = v.shape
    words = pltpu.pack_elementwise([v[:, :d // 2], v[:, d // 2:]], packed_dtype=BF16)
    for s in range(TOKEN_ROWS):
        ref[pl.ds(base + s, t, stride=TOKEN_ROWS), :] = words[:, s * LANES:(s + 1) * LANES]


def _token_words(ref, t, base=0):
    return jnp.concatenate([ref[pl.ds(base + s, t, stride=TOKEN_ROWS), :] for s in range(TOKEN_ROWS)], axis=-1)


def _from_token_tiles(ref, t, base=0):
    words = _token_words(ref, t, base)
    halves = [pltpu.unpack_elementwise(words, index=j, packed_dtype=BF16, unpacked_dtype=F32) for j in range(2)]
    return jnp.concatenate(halves, axis=-1)


MIX_CHUNK = 256


def _mix_kernel(na_ref, sw_ref, x_ref, bna_ref, bsw_ref, wo_ref, gpm_ref, ga_ref, gpf_ref, scf_ref, shf_ref,
                wr_ref, x1_ref, h2_ref, r_ref, rt_ref, cntc_ref, cntr_ref):
    first_step = (pl.program_id(0) == 0) & (pl.program_id(1) == 0)

    @pl.when(first_step)
    def _():
        cntc_ref[...] = jnp.zeros_like(cntc_ref)
        cntr_ref[...] = jnp.zeros_like(cntr_ref)

    t = MIX_CHUNK
    chunks = range(x_ref.shape[1] // t)
    row_slices = [slice(c * t, (c + 1) * t) for c in chunks]
    mixes, h2s, all_logits = [], [], []
    for rows in row_slices:
        def heads(ref):
            return jnp.concatenate([ref[0, j, rows, :] for j in range(PAIRS)], axis=-1).astype(F32)

        na = (_rms(heads(na_ref)) * bna_ref[...]).astype(BF16)
        sw = (_rms(heads(sw_ref)) * bsw_ref[...]).astype(BF16)
        mixes.append(jnp.dot(na, wo_ref[:NA_WIDTH, :], preferred_element_type=F32)
                     + jnp.dot(sw, wo_ref[NA_WIDTH:, :], preferred_element_type=F32))
    gate_gain = ga_ref[0] * gpm_ref[...]
    ffn_gain = gpf_ref[...] * (1.0 + scf_ref[0])
    for c, rows in zip(chunks, row_slices):
        x1 = x_ref[0, rows, :] + _rms(mixes[c]) * gate_gain
        x1_ref[0, rows, :] = x1
        h2 = _rms(x1) * ffn_gain + shf_ref[0]
        _to_token_tiles(h2_ref, h2, base=c * t * TOKEN_ROWS)
        h2s.append(h2)
    for c in chunks:
        h_hi, h_lo = _split2(h2s[c])
        both = (jnp.dot(h_hi, wr_ref[...], preferred_element_type=F32)
                + jnp.dot(h_lo, wr_ref[...], preferred_element_type=F32))
        all_logits.append(both + pltpu.roll(both, LANES - ROUTER_COLS, 1))
    for c, rows in zip(chunks, row_slices):
        e1, e2, w1, w2 = _route(jnp.transpose(all_logits[c])[:ROUTER_ROWS])
        fields = [None] * 4
        fields[R_E1], fields[R_E2], fields[R_W1], fields[R_W2] = e1.astype(F32), e2.astype(F32), w1, w2
        rt = jnp.concatenate(fields + [jnp.zeros((SUBLANES - 4, t), F32)], axis=0)
        rt_ref[:, rows] = rt
        r = jnp.transpose(jnp.concatenate([rt, jnp.zeros((LANES - SUBLANES, t), F32)], axis=0))
        r_ref[rows, :] = r
        sub = lax.broadcasted_iota(jnp.int32, (N_EXPERTS, t), 0)
        on_sub = ((sub == e1) | (sub == e2)).astype(F32)
        cntc_ref[...] += jnp.broadcast_to(jnp.sum(on_sub, axis=1, keepdims=True), cntc_ref.shape)
        lane = lax.broadcasted_iota(jnp.int32, r.shape, 1).astype(F32)
        on_lane = ((lane == r[:, R_E1:R_E1 + 1]) | (lane == r[:, R_E2:R_E2 + 1])).astype(F32)
        cntr_ref[...] += jnp.broadcast_to(jnp.sum(on_lane, axis=0, keepdims=True), cntr_ref.shape)


def _mix(na, sw, x, beta_na, beta_swa, w_out, g_post_mix, gate_a, g_pre_ffn, scale_f, shift_f, w_router3):
    b, s, d = x.shape
    tm = 1024
    nt = s // tm
    row = lambda bi, i: (bi, i, 0)
    const2 = lambda bi, i: (0, 0)
    per_b = lambda bi, i: (bi, 0, 0)
    return pl.pallas_call(
        _mix_kernel,
        out_shape=(jax.ShapeDtypeStruct((b, s, d), F32),
                   jax.ShapeDtypeStruct((b * s * TOKEN_ROWS, LANES), WORD),
                   jax.ShapeDtypeStruct((b * s, LANES), F32),
                   jax.ShapeDtypeStruct((SUBLANES, b * s), F32),
                   jax.ShapeDtypeStruct((N_EXPERTS, LANES), F32),
                   jax.ShapeDtypeStruct((SUBLANES, LANES), F32)),
        grid=(b, s // tm),
        in_specs=[pl.BlockSpec((1, PAIRS, tm, LANES), lambda bi, i: (bi, 0, i, 0)),
                  pl.BlockSpec((1, PAIRS, tm, LANES), lambda bi, i: (bi, 0, i, 0)),
                  pl.BlockSpec((1, tm, d), row),
                  pl.BlockSpec((1, NA_WIDTH), const2),
                  pl.BlockSpec((1, SWA_WIDTH), const2),
                  pl.BlockSpec((NA_WIDTH + SWA_WIDTH, d), const2),
                  pl.BlockSpec((1, d), const2),
                  pl.BlockSpec((1, 1, d), per_b),
                  pl.BlockSpec((1, d), const2),
                  pl.BlockSpec((1, 1, d), per_b),
                  pl.BlockSpec((1, 1, d), per_b),
                  pl.BlockSpec((d, LANES), const2)],
        out_specs=(pl.BlockSpec((1, tm, d), row),
                   pl.BlockSpec((tm * TOKEN_ROWS, LANES), lambda bi, i: (bi * nt + i, 0)),
                   pl.BlockSpec((tm, LANES), lambda bi, i: (bi * nt + i, 0)),
                   pl.BlockSpec((SUBLANES, tm), lambda bi, i: (0, bi * nt + i)),
                   pl.BlockSpec((N_EXPERTS, LANES), const2),
                   pl.BlockSpec((SUBLANES, LANES), const2)),
        compiler_params=_params("arbitrary", "arbitrary"),
        name="mix",
    )(na, sw, x, beta_na.reshape(1, -1), beta_swa.reshape(1, -1), w_out, g_post_mix.reshape(1, d), gate_a,
      g_pre_ffn.reshape(1, d), scale_f, shift_f, w_router3)


MOE_TILE = 512
MOE_SUBTILE = 256
PLAN_T = 1024
I_TILE, I_EXPERT, I_LO, I_HI, I_FIRST, I_NEW, I_NEXT, I_ORDER = range(8)


def _plan_kernel(rt_ref, cntc_ref, cntr_ref, pos_ref, items_ref, start_ref, carry_ref, *, n_tiles, n_items):
    i = pl.program_id(0)
    sub = lax.broadcasted_iota(jnp.int32, (N_EXPERTS, LANES), 0)
    lane = lax.broadcasted_iota(jnp.int32, (N_EXPERTS, LANES), 1)

    @pl.when(i == 0)
    def _():
        c_col = cntc_ref[:, 0:1]
        c_row = cntr_ref[0:1, :]
        s_col = jnp.sum(jnp.where(lane < sub, c_row, 0.0), axis=1, keepdims=True)
        s_row = jnp.sum(jnp.where(sub < lane, c_col, 0.0), axis=0, keepdims=True)
        start_ref[...] = jnp.broadcast_to(s_col, start_ref.shape)
        carry_ref[...] = jnp.zeros_like(carry_ref)

        def tiles_of(s, c):
            first = jnp.floor(s * (1.0 / MOE_TILE))
            last = jnp.floor((s + c - 1.0) * (1.0 / MOE_TILE))
            return first, jnp.where(c > 0.0, last - first + 1.0, 0.0)

        f_col, n_col = tiles_of(s_col, c_col)
        _, n_row = tiles_of(s_row, c_row)
        i_col = jnp.sum(jnp.where(lane < sub, n_row, 0.0), axis=1, keepdims=True)
        total = jnp.sum(n_col, axis=0, keepdims=True)
        k = lax.broadcasted_iota(jnp.int32, (N_EXPERTS, n_items), 1).astype(F32)
        subk = lax.broadcasted_iota(jnp.int32, (N_EXPERTS, n_items), 0).astype(F32)
        ek = jnp.sum(jnp.where(i_col + n_col <= k, 1.0, 0.0), axis=0, keepdims=True)
        k0 = k[0:1]
        valid = k0 < total
        sel = subk == ek

        def pick(v):
            return jnp.sum(jnp.where(sel, v, 0.0), axis=0, keepdims=True)

        i_k, f_k, s_k, c_k = pick(i_col), pick(f_col), pick(s_col), pick(c_col)
        tile = f_k + (k0 - i_k)
        row0 = tile * MOE_TILE
        lo = jnp.maximum(s_k, row0) - row0
        hi = jnp.minimum(s_k + c_k, row0 + MOE_TILE) - row0
        present = n_col > 0.0
        last_expert = jnp.sum(jnp.where(i_col + n_col <= total - 1.0, 1.0, 0.0), axis=0, keepdims=True)
        nxt = jnp.min(jnp.where(present & (subk > ek), subk, float(N_EXPERTS)), axis=0, keepdims=True)
        order = jnp.sum(jnp.where(present & (subk < ek), 1.0, 0.0), axis=0, keepdims=True)
        rows = [jnp.where(valid, tile, n_tiles - 1.0), jnp.where(valid, ek, last_expert),
                jnp.where(valid, lo, 0.0), jnp.where(valid, hi, 0.0),
                jnp.where(valid & (lo == 0.0), 1.0, 0.0),
                jnp.where(valid & (k0 == i_k), 1.0, 0.0), jnp.where(valid, nxt, float(N_EXPERTS)),
                jnp.where(valid, order, 0.0)]
        assert len(rows) == SUBLANES
        items_ref[...] = jnp.concatenate(rows, axis=0).astype(jnp.int32)

    t = rt_ref.shape[1]
    e1 = rt_ref[R_E1:R_E1 + 1, :]
    e2 = rt_ref[R_E2:R_E2 + 1, :]
    sub_t = lax.broadcasted_iota(jnp.int32, (N_EXPERTS, t), 0).astype(F32)
    oh1 = sub_t == e1
    oh2 = sub_t == e2
    oh = (oh1 | oh2).astype(F32)
    before = (lax.broadcasted_iota(jnp.int32, (t, t), 0) < lax.broadcasted_iota(jnp.int32, (t, t), 1)).astype(BF16)
    rank = jnp.dot(oh.astype(BF16), before, preferred_element_type=F32)
    base = start_ref[:, 0:1] + carry_ref[:, 0:1] + rank
    pos1 = jnp.sum(jnp.where(oh1, base, 0.0), axis=0, keepdims=True)
    pos2 = jnp.sum(jnp.where(oh2, base, 0.0), axis=0, keepdims=True)
    carry_ref[...] += jnp.broadcast_to(jnp.sum(oh, axis=1, keepdims=True), carry_ref.shape)
    pos = jnp.concatenate([pos1, pos2] + [jnp.zeros_like(pos1)] * (SUBLANES - 2), axis=0)
    pos_ref[...] = pos.astype(jnp.int32)


def _plan(rt, cntc, cntr):
    n = rt.shape[1]
    n_tiles = 2 * n // MOE_TILE
    n_items = 2 * LANES
    assert n_tiles + N_EXPERTS <= n_items
    return pl.pallas_call(
        functools.partial(_plan_kernel, n_tiles=n_tiles, n_items=n_items),
        out_shape=(jax.ShapeDtypeStruct((SUBLANES, n), jnp.int32),
                   jax.ShapeDtypeStruct((SUBLANES, n_items), jnp.int32)),
        grid=(n // PLAN_T,),
        in_specs=[pl.BlockSpec((SUBLANES, PLAN_T), lambda i: (0, i)),
                  pl.BlockSpec((N_EXPERTS, LANES), lambda i: (0, 0)),
                  pl.BlockSpec((SUBLANES, LANES), lambda i: (0, 0))],
        out_specs=(pl.BlockSpec((SUBLANES, PLAN_T), lambda i: (0, i)),
                   pl.BlockSpec((SUBLANES, n_items), lambda i: (0, 0))),
        scratch_shapes=[pltpu.VMEM((N_EXPERTS, LANES), F32), pltpu.VMEM((N_EXPERTS, LANES), F32)],
        compiler_params=_params("arbitrary"),
        name="plan",
    )(rt, cntc, cntr)


DISPATCH_T = 1024


def _token_rows(ref, index):
    return ref.at[pl.ds(pl.multiple_of(index * TOKEN_ROWS, TOKEN_ROWS), TOKEN_ROWS), :]


def _dispatch_kernel(pos_ref, h_ref, xs_hbm, sem):
    def body(r, carry):
        src = _token_rows(h_ref, r)
        for k in range(2):
            pltpu.make_async_copy(src, _token_rows(xs_hbm, pos_ref[k, r]), sem.at[0]).start(priority=k)
        return carry

    lax.fori_loop(0, DISPATCH_T, body, 0, unroll=8)
    for k in range(2):
        pltpu.make_async_copy(h_ref, xs_hbm.at[pl.ds(0, DISPATCH_T * TOKEN_ROWS), :], sem.at[0]).wait()


def _dispatch(pos, h2):
    n = pos.shape[1]
    return pl.pallas_call(
        _dispatch_kernel,
        out_shape=jax.ShapeDtypeStruct((2 * n * TOKEN_ROWS, LANES), WORD),
        grid=(n // DISPATCH_T,),
        in_specs=[pl.BlockSpec((SUBLANES, DISPATCH_T), lambda i: (0, i), memory_space=pltpu.SMEM),
                  pl.BlockSpec((DISPATCH_T * TOKEN_ROWS, LANES), lambda i: (i, 0))],
        out_specs=pl.BlockSpec(memory_space=pl.ANY),
        scratch_shapes=[pltpu.SemaphoreType.DMA((1,))],
        compiler_params=_params("arbitrary"),
        name="dispatch",
    )(pos, h2)


def _expert_kernel(tile_ref, exp_ref, lo_ref, hi_ref, first_ref, new_ref, next_ref, order_ref,
                   xs_ref, wg_hbm, wu_hbm, wd_hbm, o_ref, wg_buf, wu_buf, wd_buf, sem):
    k = pl.program_id(0)
    lo = lo_ref[k]
    hi = hi_ref[k]
    slot = order_ref[k] % 2

    def weight_copies(expert, sl):
        return [pltpu.make_async_copy(src.at[expert], dst.at[sl], sem.at[sl])
                for src, dst in ((wg_hbm, wg_buf), (wu_hbm, wu_buf), (wd_hbm, wd_buf))]

    @pl.when(k == 0)
    def _():
        for copy in weight_copies(exp_ref[0], 0):
            copy.start()

    @pl.when(new_ref[k] == 1)
    def _():
        for copy in weight_copies(exp_ref[k], slot):
            copy.wait()

        @pl.when(next_ref[k] < N_EXPERTS)
        def _():
            for copy in weight_copies(next_ref[k], 1 - slot):
                copy.start()

    @pl.when(first_ref[k] == 1)
    def _():
        o_ref[...] = jnp.zeros_like(o_ref)

    for part in range(MOE_TILE // MOE_SUBTILE):
        first_row = part * MOE_SUBTILE
        base = first_row * TOKEN_ROWS

        @pl.when((hi > first_row) & (lo < first_row + MOE_SUBTILE))
        def _():
            x = _from_token_tiles(xs_ref, MOE_SUBTILE, base=base).astype(BF16)
            gate = jnp.dot(x, wg_buf[slot].astype(BF16), preferred_element_type=F32)
            up = jnp.dot(x, wu_buf[slot].astype(BF16), preferred_element_type=F32)
            he = (gate * jax.nn.sigmoid(gate) * up).astype(BF16)
            ye = jnp.dot(he, wd_buf[slot].astype(BF16), preferred_element_type=F32)
            row = first_row + lax.broadcasted_iota(jnp.int32, (MOE_SUBTILE, 1), 0)
            mine = (row >= lo) & (row < hi)
            _to_token_tiles(o_ref, jnp.where(mine, ye, _from_token_tiles(o_ref, MOE_SUBTILE, base=base)), base=base)


def _experts(items, xs, w_gate, w_up, w_down):
    n_rows = xs.shape[0] // TOKEN_ROWS
    n_items = n_rows // MOE_TILE + N_EXPERTS
    d = w_gate.shape[1]
    tile_map = lambda k, tile, *_: (tile[k], 0)
    grid_spec = pltpu.PrefetchScalarGridSpec(
        num_scalar_prefetch=SUBLANES,
        grid=(n_items,),
        in_specs=[pl.BlockSpec((MOE_TILE * TOKEN_ROWS, LANES), tile_map),
                  pl.BlockSpec(memory_space=pl.ANY),
                  pl.BlockSpec(memory_space=pl.ANY),
                  pl.BlockSpec(memory_space=pl.ANY)],
        out_specs=pl.BlockSpec((MOE_TILE * TOKEN_ROWS, LANES), tile_map),
        scratch_shapes=[pltpu.VMEM((2, d, EXPERT_FF), F32), pltpu.VMEM((2, d, EXPERT_FF), F32),
                        pltpu.VMEM((2, EXPERT_FF, d), F32), pltpu.SemaphoreType.DMA((2,))])
    return pl.pallas_call(
        _expert_kernel,
        out_shape=jax.ShapeDtypeStruct(xs.shape, WORD),
        grid_spec=grid_spec,
        compiler_params=_params("arbitrary"),
        name="experts",
    )(*(items[j, :n_items] for j in range(SUBLANES)), xs, w_gate, w_up, w_down)


COMBINE_T = 256


def _combine_kernel(pos_ref, posn_ref, ys_hbm, r_ref, x1_ref, gf_ref, gpost_ref, o_ref, buf, sem):
    i = pl.program_id(0)
    n = pl.num_programs(0)
    slot = i % 2
    slot_rows = COMBINE_T * TOKEN_ROWS

    def start_row(p_ref, sl, r):
        for k in range(2):
            dst = buf.at[sl, pl.ds(pl.multiple_of(k * slot_rows + r * TOKEN_ROWS, TOKEN_ROWS), TOKEN_ROWS), :]
            pltpu.make_async_copy(_token_rows(ys_hbm, p_ref[k, r]), dst, sem.at[sl]).start(priority=k)

    def wait_slot(sl):
        pltpu.make_async_copy(ys_hbm.at[pl.ds(0, 2 * slot_rows), :], buf.at[sl], sem.at[sl]).wait()

    def issue(p_ref, sl):
        def body(r, carry):
            start_row(p_ref, sl, r)
            return carry

        lax.fori_loop(0, COMBINE_T, body, 0, unroll=8)

    @pl.when(i == 0)
    def _():
        issue(pos_ref, 0)

    @pl.when(i + 1 < n)
    def _():
        issue(posn_ref, 1 - slot)

    wait_slot(slot)
    ya = _from_token_tiles(buf.at[slot], COMBINE_T)
    yb = _from_token_tiles(buf.at[slot], COMBINE_T, base=slot_rows)
    r = r_ref[...]
    y = r[:, R_W1:R_W1 + 1] * ya + r[:, R_W2:R_W2 + 1] * yb
    o_ref[0] = x1_ref[0] + gf_ref[0] * (_rms(y) * gpost_ref[...])


def _combine(pos, ys, r, x1, gate_f, g_post_ffn):
    b, s, d = x1.shape
    nt = s // COMBINE_T
    n_steps = b * nt
    row = lambda i: (i // nt, i % nt, 0)
    pos_spec = lambda f: pl.BlockSpec((SUBLANES, COMBINE_T), lambda i: (0, f(i)), memory_space=pltpu.SMEM)
    return pl.pallas_call(
        _combine_kernel,
        out_shape=jax.ShapeDtypeStruct((b, s, d), F32),
        grid=(n_steps,),
        in_specs=[pos_spec(lambda i: i),
                  pos_spec(lambda i: jnp.minimum(i + 1, n_steps - 1)),
                  pl.BlockSpec(memory_space=pl.ANY),
                  pl.BlockSpec((COMBINE_T, LANES), lambda i: (i, 0)),
                  pl.BlockSpec((1, COMBINE_T, d), row),
                  pl.BlockSpec((1, 1, d), lambda i: (i // nt, 0, 0)),
                  pl.BlockSpec((1, d), lambda i: (0, 0))],
        out_specs=pl.BlockSpec((1, COMBINE_T, d), row),
        scratch_shapes=[pltpu.VMEM((2, 2 * COMBINE_T * TOKEN_ROWS, LANES), WORD), pltpu.SemaphoreType.DMA((2,))],
        compiler_params=_params("arbitrary"),
        name="combine",
    )(pos, pos, ys, r, x1, gate_f, g_post_ffn.reshape(1, d))


def _rope_tables(s):
    half = HEAD_DIM // 2
    inv = ROPE_THETA ** (-np.arange(half, dtype=np.float64) * 2.0 / HEAD_DIM)
    ang = np.arange(s, dtype=np.float64)[:, None] * inv[None, :]
    cos = np.cos(ang)
    sin = np.sin(ang)
    cos = np.concatenate([cos, cos, cos, cos], axis=-1)
    sin_signed = np.concatenate([-sin, sin, -sin, sin], axis=-1)
    return jnp.asarray(cos, F32), jnp.asarray(sin_signed, F32)


def _router_weights(w_group_router, w_expert_router):
    d = w_group_router.shape[0]
    we = jnp.transpose(w_expert_router, (1, 0, 2)).reshape(d, N_EXPERTS)
    hi, lo = _split2(jnp.concatenate([w_group_router, we], axis=-1))
    return jnp.concatenate([hi, lo, jnp.zeros((d, LANES - 2 * ROUTER_COLS), BF16)], axis=-1)


def kernel(x, c, w_ada, b_ada, g_pre_mix, w_in, na_rpb, swa_sinks, beta_na, beta_swa, w_out, g_post_mix, g_pre_ffn,
           w_group_router, w_expert_router, w_gate, w_up, w_down, g_post_ffn):
    b, s, d = x.shape
    depth = w_ada.shape[0]
    cos, sin_signed = _rope_tables(s)
    for l in range(depth):
        mod = _adaln(c, w_ada[l], b_ada[l]).reshape(b, N_MOD, 1, d)
        shift_a, scale_a, gate_a, shift_f, scale_f, gate_f = (mod[:, k] for k in range(N_MOD))
        qkv, na_kg, na_vg = _qkv(x, g_pre_mix[l], scale_a, shift_a, w_in[l].astype(BF16), cos, sin_signed)
        na = _na(qkv, na_kg, na_vg, na_rpb[l])
        sw = _swa(qkv, swa_sinks[l])
        x1, h2, r, rt, cntc, cntr = _mix(na, sw, x, beta_na[l], beta_swa[l], w_out[l].astype(BF16), g_post_mix[l],
                                          gate_a, g_pre_ffn[l], scale_f, shift_f,
                                          _router_weights(w_group_router[l], w_expert_router[l]))
        pos, items = _plan(rt, cntc, cntr)
        xs = _dispatch(pos, h2)
        ys = _experts(items, xs, w_gate[l], w_up[l], w_down[l])
        x = _combine(pos, ys, r, x1, gate_f, g_post_ffn[l])
    return x
```

```python
import functools

import jax
import jax.numpy as jnp
import numpy as np
from jax import lax
from jax.experimental import pallas as pl
from jax.experimental.pallas import tpu as pltpu

D_MODEL = 1024
GRID_W = 64
HEAD_DIM = 64
NA_HEADS = 8
NA_KH = 8
NA_KW = 16
SWA_HEADS = 8
SWA_KV_HEADS = 2
SWA_WINDOW = 128
SWA_BLOCK = 128
ROPE_THETA = 10000.0
NA_WIDTH = NA_HEADS * HEAD_DIM
SWA_WIDTH = SWA_HEADS * HEAD_DIM
N_GROUPS = 4
EXPERTS_PER_GROUP = 8
N_EXPERTS = N_GROUPS * EXPERTS_PER_GROUP
EXPERT_FF = 256
N_MOD = 6
EPS = 1e-6
NEG_INF = -1e30

LANES = 128
SUBLANES = 8
R_E1, R_E2, R_W1, R_W2 = range(4)
PAIRS = NA_HEADS // 2
W_NQ, W_NK, W_NV, W_SQ, W_SKV = 0, 4, 8, 12, 16
COL_NQ, COL_SQ, COL_SK, COL_SV = 0, 4, 8, 10
QKV_TILES = 12
NA_GROUPS = GRID_W // NA_KW
NA_WIN = 2 * NA_KW
NA_WIN_START = tuple(min(max(NA_KW * g - NA_KW // 2, 0), GRID_W - NA_WIN) for g in range(NA_GROUPS))
ROUTER_BASE = N_GROUPS
VMEM_LIMIT = 56 * 1024 * 1024

F32 = jnp.float32
BF16 = jnp.bfloat16
LOG2E = 1.4426950408889634


def _rms(v):
    return v * lax.rsqrt(jnp.mean(v * v, axis=-1, keepdims=True) + EPS)


def _params(*sem):
    return pltpu.CompilerParams(dimension_semantics=sem, vmem_limit_bytes=VMEM_LIMIT)


def _adaln_kernel(c_ref, w_ref, b_ref, o_ref, *, batch):
    c = c_ref[...]
    a_t = jnp.transpose(c * jax.nn.sigmoid(c))
    w = w_ref[...]
    rows = [jnp.sum(a_t[:, bi:bi + 1] * w, axis=0, keepdims=True) for bi in range(batch)]
    rows.append(jnp.zeros((c.shape[0] - batch, w.shape[1]), F32))
    o_ref[...] = jnp.concatenate(rows, axis=0) + b_ref[...]


def _adaln(c, w_ada, b_ada):
    batch, d = c.shape
    n = w_ada.shape[1]
    tn = 1024
    b = SUBLANES
    assert batch < b
    c = jnp.pad(c, ((0, b - batch), (0, 0)))
    return pl.pallas_call(
        functools.partial(_adaln_kernel, batch=batch),
        out_shape=jax.ShapeDtypeStruct((b, n), F32),
        grid=(n // tn,),
        in_specs=[pl.BlockSpec((b, d), lambda j: (0, 0)),
                  pl.BlockSpec((d, tn), lambda j: (0, j)),
                  pl.BlockSpec((1, tn), lambda j: (0, j))],
        out_specs=pl.BlockSpec((b, tn), lambda j: (0, j)),
        compiler_params=_params("arbitrary"),
        name="adaln",
    )(c, w_ada, b_ada.reshape(1, n))[:batch]


def _rope(v, cos, sin_signed, first_half):
    rot = jnp.where(first_half, pltpu.roll(v, LANES - HEAD_DIM // 2, 1), pltpu.roll(v, HEAD_DIM // 2, 1))
    return v * cos + rot * sin_signed


QKV_CHUNK = 256


def _column_windows(v):
    grid_rows = v.shape[0] // GRID_W
    return [jnp.concatenate([v[r * GRID_W + w0:r * GRID_W + w0 + NA_WIN] for r in range(grid_rows)], axis=0)
            for w0 in NA_WIN_START]


def _qkv_kernel(x_ref, g_ref, sc_ref, sh_ref, w_ref, cos_ref, sin_ref, o_ref, kg_ref, vg_ref):
    scale = HEAD_DIM ** -0.5 * LOG2E
    lane = lax.broadcasted_iota(jnp.int32, (QKV_CHUNK, LANES), 1)
    first_half = (lane % HEAD_DIM) < HEAD_DIM // 2
    upper = lane >= HEAD_DIM

    def tile(v, j):
        return v[:, j * LANES:(j + 1) * LANES]

    for c in range(x_ref.shape[1] // QKV_CHUNK):
        rows = slice(c * QKV_CHUNK, (c + 1) * QKV_CHUNK)
        h = (_rms(x_ref[0, rows, :]) * g_ref[...]) * (1.0 + sc_ref[0]) + sh_ref[0]
        h = h.astype(BF16)
        cos = cos_ref[rows, :]
        sin = sin_ref[rows, :]

        def proj(col, width):
            return jnp.dot(h, w_ref[:, col * LANES:(col + width) * LANES], preferred_element_type=F32)

        nq, nk, nv, sq = proj(W_NQ, 4), proj(W_NK, 4), proj(W_NV, 4), proj(W_SQ, 4)
        win_rows = slice(c * QKV_CHUNK // 2, (c + 1) * QKV_CHUNK // 2)
        for j in range(PAIRS):
            o_ref[0, COL_NQ + j, rows, :] = (tile(nq, j) * scale).astype(BF16)
            o_ref[0, COL_SQ + j, rows, :] = (_rope(tile(sq, j), cos, sin, first_half) * scale).astype(BF16)
            for ref, val in ((kg_ref, tile(nk, j)), (vg_ref, tile(nv, j))):
                for g, win in enumerate(_column_windows(val)):
                    ref[0, j, g, win_rows, :] = win.astype(BF16)
        skv = proj(W_SKV, 2)
        k = _rope(tile(skv, 0), cos, sin, first_half)
        v = tile(skv, 1)
        for t, col in ((k, COL_SK), (v, COL_SV)):
            swapped = pltpu.roll(t, HEAD_DIM, 1)
            o_ref[0, col, rows, :] = jnp.where(upper, swapped, t).astype(BF16)
            o_ref[0, col + 1, rows, :] = jnp.where(upper, t, swapped).astype(BF16)


def _qkv(x, g, scale_a, shift_a, w_in, cos, sin):
    b, s, d = x.shape
    tm = 1024
    n_in = w_in.shape[1]
    windows = jax.ShapeDtypeStruct((b, PAIRS, NA_GROUPS, s // 2, LANES), BF16)
    windows_spec = pl.BlockSpec((1, PAIRS, NA_GROUPS, tm // 2, LANES), lambda bi, i: (bi, 0, 0, i, 0))
    return pl.pallas_call(
        _qkv_kernel,
        out_shape=(jax.ShapeDtypeStruct((b, QKV_TILES, s, LANES), BF16), windows, windows),
        grid=(b, s // tm),
        in_specs=[pl.BlockSpec((1, tm, d), lambda bi, i: (bi, i, 0)),
                  pl.BlockSpec((1, d), lambda bi, i: (0, 0)),
                  pl.BlockSpec((1, 1, d), lambda bi, i: (bi, 0, 0)),
                  pl.BlockSpec((1, 1, d), lambda bi, i: (bi, 0, 0)),
                  pl.BlockSpec((d, n_in), lambda bi, i: (0, 0)),
                  pl.BlockSpec((tm, LANES), lambda bi, i: (i, 0)),
                  pl.BlockSpec((tm, LANES), lambda bi, i: (i, 0))],
        out_specs=(pl.BlockSpec((1, QKV_TILES, tm, LANES), lambda bi, i: (bi, 0, i, 0)), windows_spec, windows_spec),
        compiler_params=_params("arbitrary", "arbitrary"),
        name="qkv",
    )(x, g.reshape(1, d), scale_a, shift_a, w_in, cos, sin)


NA_QROWS = 8
NA_KROWS = NA_QROWS + NA_KH
NA_BLOCKS_PER_STEP = 4
NA_INTERLEAVE = 4
NA_Q = NA_QROWS * NA_KW
NA_K = NA_KROWS * NA_WIN
NA_RPB_ROWS = 2 * NA_KH - 1
NA_RPB_COLS = 2 * NA_KW - 1


def _clamp(v, lo, hi):
    return min(max(v, lo), hi)


def _na_first_key_row(block, rows, clip):
    return clip(block * NA_QROWS - NA_KH // 2, 0, rows - NA_KROWS)


def _na_group_tables():
    def geometry(g):
        cols = [NA_KW * g + cc for cc in range(NA_KW)]
        return (NA_WIN_START[g] - NA_KW * g,) + tuple(_clamp(c - NA_KW // 2, 0, GRID_W - NA_KW) - c for c in cols)

    seen, table_of_group, representatives = {}, [], []
    for g in range(NA_GROUPS):
        key = geometry(g)
        if key not in seen:
            seen[key] = len(representatives)
            representatives.append(g)
        table_of_group.append(seen[key])
    return table_of_group, representatives


def _na_block_types(rows):
    def geometry(block):
        r = block * NA_QROWS
        a = _na_first_key_row(block, rows, _clamp)
        return (a - r,) + tuple(_clamp(r + j - NA_KH // 2, 0, rows - NA_KH) - r for j in range(NA_QROWS))

    n_blocks = rows // NA_QROWS
    interior = geometry(n_blocks // 2)
    lead = next(b for b in range(n_blocks) if geometry(b) == interior)
    trail = next(b for b in range(n_blocks) if geometry(n_blocks - 1 - b) == interior)
    assert all(geometry(b) == interior for b in range(lead, n_blocks - trail))
    return lead, trail


def _na_kernel(q_ref, k_ref, v_ref, bias_ref, o_ref, *, rows):
    upper = lax.broadcasted_iota(jnp.int32, (NA_Q, LANES), 1) >= HEAD_DIM
    key_upper = lax.broadcasted_iota(jnp.int32, (NA_K, LANES), 1) >= HEAD_DIM
    lead, trail = _na_block_types(rows)
    first_trailing = rows // NA_QROWS - trail
    n_types = lead + trail + 1
    table_of_group, _ = _na_group_tables()
    for c0 in range(0, NA_BLOCKS_PER_STEP, NA_INTERLEAVE):
        blocks = range(c0, c0 + NA_INTERLEAVE)
        starts, types, pieces = {}, {}, {}
        for c in blocks:
            block = pl.program_id(2) * NA_BLOCKS_PER_STEP + c
            a = _na_first_key_row(block, rows, jnp.clip)
            types[c] = jnp.where(block < lead, block,
                                 jnp.where(block >= first_trailing, block - first_trailing + lead + 1, lead))
            starts[c] = pl.multiple_of(a * NA_WIN, NA_WIN)
            for g in range(NA_GROUPS):
                pieces[c, g] = [slice((c * NA_QROWS + rr) * GRID_W + g * NA_KW,
                                      (c * NA_QROWS + rr) * GRID_W + (g + 1) * NA_KW) for rr in range(NA_QROWS)]
        chains = [(c, g, hh) for c in blocks for g in range(NA_GROUPS) for hh in range(2)]
        scores, probs, outs = {}, {}, {}
        for c, g, hh in chains:
            q = jnp.concatenate([q_ref[0, 0, rws, :] for rws in pieces[c, g]], axis=0)
            qm = jnp.where(upper if hh else ~upper, q, jnp.zeros_like(q))
            ks = k_ref[0, 0, g, pl.ds(starts[c], NA_K), :]
            s = lax.dot_general(qm, ks, (((1,), (1,)), ((), ())), preferred_element_type=F32)
            scores[c, g, hh] = s + bias_ref[hh, table_of_group[g] * n_types + types[c]]
        for chain in chains:
            s = scores[chain]
            probs[chain] = jnp.exp2(s - jnp.max(s, axis=-1, keepdims=True)).astype(BF16)
        for c, g, hh in chains:
            vs = v_ref[0, 0, g, pl.ds(starts[c], NA_K), :]
            v1 = jnp.where(key_upper if hh else ~key_upper, vs, jnp.ones_like(vs))
            o = jnp.dot(probs[c, g, hh], v1, preferred_element_type=F32)
            outs[c, g, hh] = o / o[:, (1 - hh) * HEAD_DIM:(1 - hh) * HEAD_DIM + 1]
        for c in blocks:
            for g in range(NA_GROUPS):
                out = jnp.where(upper, outs[c, g, 1], outs[c, g, 0]).astype(BF16)
                for rr, rws in enumerate(pieces[c, g]):
                    o_ref[0, 0, rws, :] = out[rr * NA_KW:(rr + 1) * NA_KW]


def _na_bias_kernel(rpb_ref, o_ref, *, rows):
    h = pl.program_id(0)
    cc = lax.broadcasted_iota(jnp.int32, (NA_KW, LANES), 0)
    lane = lax.broadcasted_iota(jnp.int32, (NA_KW, LANES), 1)
    w = lane % NA_WIN
    key_row_in_tile = lane // NA_WIN
    rows_per_tile = LANES // NA_WIN
    neg = jnp.full((NA_KW, LANES), NEG_INF, F32)
    base = h * NA_RPB_ROWS * NA_RPB_COLS
    lead, trail = _na_block_types(rows)
    n_blocks = rows // NA_QROWS
    type_blocks = list(range(lead + 1)) + list(range(n_blocks - trail, n_blocks))
    for table, g in enumerate(_na_group_tables()[1]):
        qc = g * NA_KW + cc
        kc = NA_WIN_START[g] + w
        c0 = jnp.clip(qc - NA_KW // 2, 0, GRID_W - NA_KW)
        in_cols = (kc >= c0) & (kc < c0 + NA_KW)
        dc = kc - qc + NA_KW - 1
        by_row_offset = []
        for d in range(NA_RPB_ROWS):
            acc = neg
            for dd in range(NA_RPB_COLS):
                acc = jnp.where(dc == dd, rpb_ref[base + d * NA_RPB_COLS + dd], acc)
            by_row_offset.append(jnp.where(in_cols, acc * LOG2E, NEG_INF))
        for ty, block in enumerate(type_blocks):
            r = block * NA_QROWS
            a = _na_first_key_row(block, rows, _clamp)
            for j in range(NA_QROWS):
                r0 = _clamp(r + j - NA_KH // 2, 0, rows - NA_KH)
                for t in range(NA_K // LANES):
                    tile = neg
                    for part in range(rows_per_tile):
                        i = t * rows_per_tile + part
                        if r0 <= a + i < r0 + NA_KH:
                            tile = jnp.where(key_row_in_tile == part, by_row_offset[a + i - (r + j) + NA_KH - 1], tile)
                    o_ref[0, table * len(type_blocks) + ty, j * NA_KW:(j + 1) * NA_KW, t * LANES:(t + 1) * LANES] = tile


def _na_bias(rpb, rows):
    n_tables = len(_na_group_tables()[1]) * (sum(_na_block_types(rows)) + 1)
    return pl.pallas_call(
        functools.partial(_na_bias_kernel, rows=rows),
        out_shape=jax.ShapeDtypeStruct((NA_HEADS, n_tables, NA_Q, NA_K), F32),
        grid=(NA_HEADS,),
        in_specs=[pl.BlockSpec(memory_space=pltpu.SMEM)],
        out_specs=pl.BlockSpec((1, n_tables, NA_Q, NA_K), lambda h: (h, 0, 0, 0)),
        compiler_params=_params("arbitrary"),
        name="na_bias",
    )(rpb.astype(F32).reshape(-1))


def _na(qkv, kg, vg, rpb):
    b, _, s, _ = qkv.shape
    rows = s // GRID_W
    assert rows % (NA_QROWS * NA_BLOCKS_PER_STEP) == 0
    tq = NA_BLOCKS_PER_STEP * NA_QROWS * GRID_W
    bias = _na_bias(rpb, rows)
    windows_spec = pl.BlockSpec((1, 1) + kg.shape[2:], lambda bi, p, i: (bi, p, 0, 0, 0))
    return pl.pallas_call(
        functools.partial(_na_kernel, rows=rows),
        out_shape=jax.ShapeDtypeStruct((b, PAIRS, s, LANES), BF16),
        grid=(b, PAIRS, s // tq),
        in_specs=[pl.BlockSpec((1, 1, tq, LANES), lambda bi, p, i: (bi, COL_NQ + p, i, 0)),
                  windows_spec,
                  windows_spec,
                  pl.BlockSpec((2,) + bias.shape[1:], lambda bi, p, i: (p, 0, 0, 0))],
        out_specs=pl.BlockSpec((1, 1, tq, LANES), lambda bi, p, i: (bi, p, i, 0)),
        compiler_params=_params("arbitrary", "arbitrary", "arbitrary"),
        name="na",
    )(qkv, kg, vg, bias)


SWA_KEYS = 3 * SWA_BLOCK
SWA_BLOCKS_PER_STEP = 8
SWA_INTERLEAVE = 2
SWA_GROUP = SWA_HEADS // SWA_KV_HEADS
assert COL_SQ % (PAIRS // SWA_KV_HEADS) == 0


def _swa_masks():
    v = np.arange(SWA_KEYS // SWA_BLOCK)[:, None, None]
    q = np.arange(SWA_BLOCK)[None, :, None]
    k = np.arange(SWA_KEYS)[None, None, :]
    return np.where(np.abs(k - v * SWA_BLOCK - q) <= SWA_WINDOW, 0.0, NEG_INF).astype(np.float32)


def _swa_kernel(sink_ref, mask_ref, q_ref, k_ref, v_ref, o_ref, *, seq):
    kv = pl.program_id(1)
    rows = SWA_GROUP * SWA_BLOCK
    lane = lax.broadcasted_iota(jnp.int32, (SWA_BLOCK, LANES), 1)
    upper = lane >= HEAD_DIM
    head = lax.broadcasted_iota(jnp.int32, (rows, 1), 0) // SWA_BLOCK
    sink = jnp.zeros((rows, 1), F32)
    for g in range(SWA_GROUP):
        sink = jnp.where(head == g, sink_ref[kv * SWA_GROUP + g], sink)
    sink = sink * LOG2E
    for j0 in range(0, SWA_BLOCKS_PER_STEP, SWA_INTERLEAVE):
        blocks = range(j0, j0 + SWA_INTERLEAVE)
        starts, scores, probs = {}, {}, {}
        for j in blocks:
            n = pl.program_id(2) * SWA_BLOCKS_PER_STEP + j
            start = pl.multiple_of(jnp.clip((n - 1) * SWA_BLOCK, 0, seq - SWA_KEYS), SWA_BLOCK)
            starts[j] = start
            ks = k_ref[0, 0, pl.ds(start, SWA_KEYS), :]
            qs = []
            for g in range(SWA_GROUP):
                q = q_ref[0, g // 2, j * SWA_BLOCK:(j + 1) * SWA_BLOCK, :]
                qs.append(jnp.where(upper if g % 2 else ~upper, q, jnp.zeros_like(q)))
            s = lax.dot_general(jnp.concatenate(qs, axis=0), ks, (((1,), (1,)), ((), ())), preferred_element_type=F32)
            mask = mask_ref[(n * SWA_BLOCK - start) // SWA_BLOCK]
            scores[j] = s + jnp.concatenate([mask] * SWA_GROUP, axis=0)
        for j in blocks:
            s = scores[j]
            m = jnp.maximum(jnp.max(s, axis=-1, keepdims=True), sink)
            e = jnp.exp2(s - m)
            probs[j] = (e.astype(BF16), jnp.sum(e, axis=-1, keepdims=True) + jnp.exp2(sink - m))
        for j in blocks:
            e, l = probs[j]
            vs = v_ref[0, 0, pl.ds(starts[j], SWA_KEYS), :]
            o = jnp.dot(e, vs, preferred_element_type=F32) / l
            for pair in range(SWA_GROUP // 2):
                even = o[(2 * pair) * SWA_BLOCK:(2 * pair + 1) * SWA_BLOCK]
                odd = o[(2 * pair + 1) * SWA_BLOCK:(2 * pair + 2) * SWA_BLOCK]
                o_ref[0, pair, j * SWA_BLOCK:(j + 1) * SWA_BLOCK, :] = jnp.where(upper, odd, even).astype(BF16)


def _swa(qkv, sinks):
    b, _, s, _ = qkv.shape
    pairs_per_kv = PAIRS // SWA_KV_HEADS
    tq = SWA_BLOCKS_PER_STEP * SWA_BLOCK
    masks = _swa_masks()
    return pl.pallas_call(
        functools.partial(_swa_kernel, seq=s),
        out_shape=jax.ShapeDtypeStruct((b, PAIRS, s, LANES), BF16),
        grid=(b, SWA_KV_HEADS, s // tq),
        in_specs=[pl.BlockSpec(memory_space=pltpu.SMEM),
                  pl.BlockSpec(masks.shape, lambda bi, kv, n: (0, 0, 0)),
                  pl.BlockSpec((1, pairs_per_kv, tq, LANES),
                               lambda bi, kv, n: (bi, COL_SQ // pairs_per_kv + kv, n, 0)),
                  pl.BlockSpec((1, 1, s, LANES), lambda bi, kv, n: (bi, COL_SK + kv, 0, 0)),
                  pl.BlockSpec((1, 1, s, LANES), lambda bi, kv, n: (bi, COL_SV + kv, 0, 0))],
        out_specs=pl.BlockSpec((1, pairs_per_kv, tq, LANES), lambda bi, kv, n: (bi, kv, n, 0)),
        compiler_params=_params("arbitrary", "arbitrary", "arbitrary"),
        name="swa",
    )(sinks, jnp.asarray(masks), qkv, qkv, qkv)


ROUTER_COLS = N_GROUPS + N_EXPERTS


def _split2(v):
    hi = v.astype(BF16)
    lo = (v - hi.astype(F32)).astype(BF16)
    return hi, lo


ROUTER_ROWS = 40


def _route(logits_t):
    sub = lax.broadcasted_iota(jnp.int32, logits_t.shape, 0)
    big = jnp.int32(LANES)
    gmask = sub < N_GROUPS
    gl = jnp.where(gmask, logits_t, NEG_INF)
    gmax = jnp.max(gl, axis=0, keepdims=True)
    g_top = jnp.min(jnp.where(gmask & (gl == gmax), sub, big), axis=0, keepdims=True)
    g_weight = 1.0 / jnp.sum(jnp.where(gmask, jnp.exp(gl - gmax), 0.0), axis=0, keepdims=True)
    lo = ROUTER_BASE + g_top * EXPERTS_PER_GROUP
    emask = (sub >= lo) & (sub < lo + EXPERTS_PER_GROUP)
    el = jnp.where(emask, logits_t, NEG_INF)
    m1 = jnp.max(el, axis=0, keepdims=True)
    i1 = jnp.min(jnp.where(emask & (el == m1), sub, big), axis=0, keepdims=True)
    emask2 = emask & (sub != i1)
    el2 = jnp.where(emask2, logits_t, NEG_INF)
    m2 = jnp.max(el2, axis=0, keepdims=True)
    i2 = jnp.min(jnp.where(emask2 & (el2 == m2), sub, big), axis=0, keepdims=True)
    e2 = jnp.exp(m2 - m1)
    w1 = g_weight / (1.0 + e2)
    w2 = g_weight * e2 / (1.0 + e2)
    return i1 - ROUTER_BASE, i2 - ROUTER_BASE, w1, w2


TOKEN_ROWS = D_MODEL // (2 * LANES)
WORD = jnp.uint32


def _to_token_tiles(ref, v, base=0):
    t, d = v.shape
    words = pltpu.pack_elementwise([v[:, :d // 2], v[:, d // 2:]], packed_dtype=BF16)
    for s in range(TOKEN_ROWS):
        ref[pl.ds(base + s, t, stride=TOKEN_ROWS), :] = words[:, s * LANES:(s + 1) * LANES]


def _token_words(ref, t, base=0):
    return jnp.concatenate([ref[pl.ds(base + s, t, stride=TOKEN_ROWS), :] for s in range(TOKEN_ROWS)], axis=-1)


def _from_token_tiles(ref, t, base=0):
    words = _token_words(ref, t, base)
    halves = [pltpu.unpack_elementwise(words, index=j, packed_dtype=BF16, unpacked_dtype=F32) for j in range(2)]
    return jnp.concatenate(halves, axis=-1)


MIX_CHUNK = 256


def _mix_kernel(na_ref, sw_ref, x_ref, bna_ref, bsw_ref, wo_ref, gpm_ref, ga_ref, gpf_ref, scf_ref, shf_ref,
                wr_ref, x1_ref, h2_ref, r_ref, rt_ref, cntc_ref, cntr_ref):
    first_step = (pl.program_id(0) == 0) & (pl.program_id(1) == 0)

    @pl.when(first_step)
    def _():
        cntc_ref[...] = jnp.zeros_like(cntc_ref)
        cntr_ref[...] = jnp.zeros_like(cntr_ref)

    t = MIX_CHUNK
    chunks = range(x_ref.shape[1] // t)
    row_slices = [slice(c * t, (c + 1) * t) for c in chunks]
    mixes, h2s, all_logits = [], [], []
    for rows in row_slices:
        def heads(ref):
            return jnp.concatenate([ref[0, j, rows, :] for j in range(PAIRS)], axis=-1).astype(F32)

        na = (_rms(heads(na_ref)) * bna_ref[...]).astype(BF16)
        sw = (_rms(heads(sw_ref)) * bsw_ref[...]).astype(BF16)
        mixes.append(jnp.dot(na, wo_ref[:NA_WIDTH, :], preferred_element_type=F32)
                     + jnp.dot(sw, wo_ref[NA_WIDTH:, :], preferred_element_type=F32))
    gate_gain = ga_ref[0] * gpm_ref[...]
    ffn_gain = gpf_ref[...] * (1.0 + scf_ref[0])
    for c, rows in zip(chunks, row_slices):
        x1 = x_ref[0, rows, :] + _rms(mixes[c]) * gate_gain
        x1_ref[0, rows, :] = x1
        h2 = _rms(x1) * ffn_gain + shf_ref[0]
        _to_token_tiles(h2_ref, h2, base=c * t * TOKEN_ROWS)
        h2s.append(h2)
    for c in chunks:
        h_hi, h_lo = _split2(h2s[c])
        both = (jnp.dot(h_hi, wr_ref[...], preferred_element_type=F32)
                + jnp.dot(h_lo, wr_ref[...], preferred_element_type=F32))
        all_logits.append(both + pltpu.roll(both, LANES - ROUTER_COLS, 1))
    for c, rows in zip(chunks, row_slices):
        e1, e2, w1, w2 = _route(jnp.transpose(all_logits[c])[:ROUTER_ROWS])
        fields = [None] * 4
        fields[R_E1], fields[R_E2], fields[R_W1], fields[R_W2] = e1.astype(F32), e2.astype(F32), w1, w2
        rt = jnp.concatenate(fields + [jnp.zeros((SUBLANES - 4, t), F32)], axis=0)
        rt_ref[:, rows] = rt
        r = jnp.transpose(jnp.concatenate([rt, jnp.zeros((LANES - SUBLANES, t), F32)], axis=0))
        r_ref[rows, :] = r
        sub = lax.broadcasted_iota(jnp.int32, (N_EXPERTS, t), 0)
        on_sub = ((sub == e1) | (sub == e2)).astype(F32)
        cntc_ref[...] += jnp.broadcast_to(jnp.sum(on_sub, axis=1, keepdims=True), cntc_ref.shape)
        lane = lax.broadcasted_iota(jnp.int32, r.shape, 1).astype(F32)
        on_lane = ((lane == r[:, R_E1:R_E1 + 1]) | (lane == r[:, R_E2:R_E2 + 1])).astype(F32)
        cntr_ref[...] += jnp.broadcast_to(jnp.sum(on_lane, axis=0, keepdims=True), cntr_ref.shape)


def _mix(na, sw, x, beta_na, beta_swa, w_out, g_post_mix, gate_a, g_pre_ffn, scale_f, shift_f, w_router3):
    b, s, d = x.shape
    tm = 1024
    nt = s // tm
    row = lambda bi, i: (bi, i, 0)
    const2 = lambda bi, i: (0, 0)
    per_b = lambda bi, i: (bi, 0, 0)
    return pl.pallas_call(
        _mix_kernel,
        out_shape=(jax.ShapeDtypeStruct((b, s, d), F32),
                   jax.ShapeDtypeStruct((b * s * TOKEN_ROWS, LANES), WORD),
                   jax.ShapeDtypeStruct((b * s, LANES), F32),
                   jax.ShapeDtypeStruct((SUBLANES, b * s), F32),
                   jax.ShapeDtypeStruct((N_EXPERTS, LANES), F32),
                   jax.ShapeDtypeStruct((SUBLANES, LANES), F32)),
        grid=(b, s // tm),
        in_specs=[pl.BlockSpec((1, PAIRS, tm, LANES), lambda bi, i: (bi, 0, i, 0)),
                  pl.BlockSpec((1, PAIRS, tm, LANES), lambda bi, i: (bi, 0, i, 0)),
                  pl.BlockSpec((1, tm, d), row),
                  pl.BlockSpec((1, NA_WIDTH), const2),
                  pl.BlockSpec((1, SWA_WIDTH), const2),
                  pl.BlockSpec((NA_WIDTH + SWA_WIDTH, d), const2),
                  pl.BlockSpec((1, d), const2),
                  pl.BlockSpec((1, 1, d), per_b),
                  pl.BlockSpec((1, d), const2),
                  pl.BlockSpec((1, 1, d), per_b),
                  pl.BlockSpec((1, 1, d), per_b),
                  pl.BlockSpec((d, LANES), const2)],
        out_specs=(pl.BlockSpec((1, tm, d), row),
                   pl.BlockSpec((tm * TOKEN_ROWS, LANES), lambda bi, i: (bi * nt + i, 0)),
                   pl.BlockSpec((tm, LANES), lambda bi, i: (bi * nt + i, 0)),
                   pl.BlockSpec((SUBLANES, tm), lambda bi, i: (0, bi * nt + i)),
                   pl.BlockSpec((N_EXPERTS, LANES), const2),
                   pl.BlockSpec((SUBLANES, LANES), const2)),
        compiler_params=_params("arbitrary", "arbitrary"),
        name="mix",
    )(na, sw, x, beta_na.reshape(1, -1), beta_swa.reshape(1, -1), w_out, g_post_mix.reshape(1, d), gate_a,
      g_pre_ffn.reshape(1, d), scale_f, shift_f, w_router3)


MOE_TILE = 1024
MOE_SUBTILE = 256
PLAN_T = 1024
I_TILE, I_EXPERT, I_LO, I_HI, I_FIRST, I_NEW, I_NEXT, I_ORDER = range(8)


def _plan_kernel(rt_ref, cntc_ref, cntr_ref, pos_ref, items_ref, start_ref, carry_ref, *, n_tiles, n_items):
    i = pl.program_id(0)
    sub = lax.broadcasted_iota(jnp.int32, (N_EXPERTS, LANES), 0)
    lane = lax.broadcasted_iota(jnp.int32, (N_EXPERTS, LANES), 1)

    @pl.when(i == 0)
    def _():
        c_col = cntc_ref[:, 0:1]
        c_row = cntr_ref[0:1, :]
        s_col = jnp.sum(jnp.where(lane < sub, c_row, 0.0), axis=1, keepdims=True)
        s_row = jnp.sum(jnp.where(sub < lane, c_col, 0.0), axis=0, keepdims=True)
        start_ref[...] = jnp.broadcast_to(s_col, start_ref.shape)
        carry_ref[...] = jnp.zeros_like(carry_ref)

        def tiles_of(s, c):
            first = jnp.floor(s * (1.0 / MOE_TILE))
            last = jnp.floor((s + c - 1.0) * (1.0 / MOE_TILE))
            return first, jnp.where(c > 0.0, last - first + 1.0, 0.0)

        f_col, n_col = tiles_of(s_col, c_col)
        _, n_row = tiles_of(s_row, c_row)
        i_col = jnp.sum(jnp.where(lane < sub, n_row, 0.0), axis=1, keepdims=True)
        total = jnp.sum(n_col, axis=0, keepdims=True)
        k = lax.broadcasted_iota(jnp.int32, (N_EXPERTS, n_items), 1).astype(F32)
        subk = lax.broadcasted_iota(jnp.int32, (N_EXPERTS, n_items), 0).astype(F32)
        ek = jnp.sum(jnp.where(i_col + n_col <= k, 1.0, 0.0), axis=0, keepdims=True)
        k0 = k[0:1]
        valid = k0 < total
        sel = subk == ek

        def pick(v):
            return jnp.sum(jnp.where(sel, v, 0.0), axis=0, keepdims=True)

        i_k, f_k, s_k, c_k = pick(i_col), pick(f_col), pick(s_col), pick(c_col)
        tile = f_k + (k0 - i_k)
        row0 = tile * MOE_TILE
        lo = jnp.maximum(s_k, row0) - row0
        hi = jnp.minimum(s_k + c_k, row0 + MOE_TILE) - row0
        present = n_col > 0.0
        last_expert = jnp.sum(jnp.where(i_col + n_col <= total - 1.0, 1.0, 0.0), axis=0, keepdims=True)
        nxt = jnp.min(jnp.where(present & (subk > ek), subk, float(N_EXPERTS)), axis=0, keepdims=True)
        order = jnp.sum(jnp.where(present & (subk < ek), 1.0, 0.0), axis=0, keepdims=True)
        rows = [jnp.where(valid, tile, n_tiles - 1.0), jnp.where(valid, ek, last_expert),
                jnp.where(valid, lo, 0.0), jnp.where(valid, hi, 0.0),
                jnp.where(valid & (lo == 0.0), 1.0, 0.0),
                jnp.where(valid & (k0 == i_k), 1.0, 0.0), jnp.where(valid, nxt, float(N_EXPERTS)),
                jnp.where(valid, order, 0.0)]
        assert len(rows) == SUBLANES
        items_ref[...] = jnp.concatenate(rows, axis=0).astype(jnp.int32)

    t = rt_ref.shape[1]
    e1 = rt_ref[R_E1:R_E1 + 1, :]
    e2 = rt_ref[R_E2:R_E2 + 1, :]
    sub_t = lax.broadcasted_iota(jnp.int32, (N_EXPERTS, t), 0).astype(F32)
    oh1 = sub_t == e1
    oh2 = sub_t == e2
    oh = (oh1 | oh2).astype(F32)
    before = (lax.broadcasted_iota(jnp.int32, (t, t), 0) < lax.broadcasted_iota(jnp.int32, (t, t), 1)).astype(BF16)
    rank = jnp.dot(oh.astype(BF16), before, preferred_element_type=F32)
    base = start_ref[:, 0:1] + carry_ref[:, 0:1] + rank
    pos1 = jnp.sum(jnp.where(oh1, base, 0.0), axis=0, keepdims=True)
    pos2 = jnp.sum(jnp.where(oh2, base, 0.0), axis=0, keepdims=True)
    carry_ref[...] += jnp.broadcast_to(jnp.sum(oh, axis=1, keepdims=True), carry_ref.shape)
    pos = jnp.concatenate([pos1, pos2] + [jnp.zeros_like(pos1)] * (SUBLANES - 2), axis=0)
    pos_ref[...] = pos.astype(jnp.int32)


def _plan(rt, cntc, cntr):
    n = rt.shape[1]
    n_tiles = 2 * n // MOE_TILE
    n_items = 2 * LANES
    assert n_tiles + N_EXPERTS <= n_items
    return pl.pallas_call(
        functools.partial(_plan_kernel, n_tiles=n_tiles, n_items=n_items),
        out_shape=(jax.ShapeDtypeStruct((SUBLANES, n), jnp.int32),
                   jax.ShapeDtypeStruct((SUBLANES, n_items), jnp.int32)),
        grid=(n // PLAN_T,),
        in_specs=[pl.BlockSpec((SUBLANES, PLAN_T), lambda i: (0, i)),
                  pl.BlockSpec((N_EXPERTS, LANES), lambda i: (0, 0)),
                  pl.BlockSpec((SUBLANES, LANES), lambda i: (0, 0))],
        out_specs=(pl.BlockSpec((SUBLANES, PLAN_T), lambda i: (0, i)),
                   pl.BlockSpec((SUBLANES, n_items), lambda i: (0, 0))),
        scratch_shapes=[pltpu.VMEM((N_EXPERTS, LANES), F32), pltpu.VMEM((N_EXPERTS, LANES), F32)],
        compiler_params=_params("arbitrary"),
        name="plan",
    )(rt, cntc, cntr)


DISPATCH_T = 1024


def _token_rows(ref, index):
    return ref.at[pl.ds(pl.multiple_of(index * TOKEN_ROWS, TOKEN_ROWS), TOKEN_ROWS), :]


def _dispatch_kernel(pos_ref, h_ref, xs_hbm, sem):
    def body(r, carry):
        src = _token_rows(h_ref, r)
        for k in range(2):
            pltpu.make_async_copy(src, _token_rows(xs_hbm, pos_ref[k, r]), sem.at[0]).start(priority=k)
        return carry

    lax.fori_loop(0, DISPATCH_T, body, 0, unroll=8)
    for k in range(2):
        pltpu.make_async_copy(h_ref, xs_hbm.at[pl.ds(0, DISPATCH_T * TOKEN_ROWS), :], sem.at[0]).wait()


def _dispatch(pos, h2):
    n = pos.shape[1]
    return pl.pallas_call(
        _dispatch_kernel,
        out_shape=jax.ShapeDtypeStruct((2 * n * TOKEN_ROWS, LANES), WORD),
        grid=(n // DISPATCH_T,),
        in_specs=[pl.BlockSpec((SUBLANES, DISPATCH_T), lambda i: (0, i), memory_space=pltpu.SMEM),
                  pl.BlockSpec((DISPATCH_T * TOKEN_ROWS, LANES), lambda i: (i, 0))],
        out_specs=pl.BlockSpec(memory_space=pl.ANY),
        scratch_shapes=[pltpu.SemaphoreType.DMA((1,))],
        compiler_params=_params("arbitrary"),
        name="dispatch",
    )(pos, h2)


def _expert_kernel(tile_ref, exp_ref, lo_ref, hi_ref, first_ref, new_ref, next_ref, order_ref,
                   xs_ref, wg_hbm, wu_hbm, wd_hbm, o_ref, wg_buf, wu_buf, wd_buf, sem):
    k = pl.program_id(0)
    lo = lo_ref[k]
    hi = hi_ref[k]
    slot = order_ref[k] % 2

    def weight_copies(expert, sl):
        return [pltpu.make_async_copy(src.at[expert], dst.at[sl], sem.at[sl])
                for src, dst in ((wg_hbm, wg_buf), (wu_hbm, wu_buf), (wd_hbm, wd_buf))]

    @pl.when(k == 0)
    def _():
        for copy in weight_copies(exp_ref[0], 0):
            copy.start()

    @pl.when(new_ref[k] == 1)
    def _():
        for copy in weight_copies(exp_ref[k], slot):
            copy.wait()

        @pl.when(next_ref[k] < N_EXPERTS)
        def _():
            for copy in weight_copies(next_ref[k], 1 - slot):
                copy.start()

    @pl.when(first_ref[k] == 1)
    def _():
        o_ref[...] = jnp.zeros_like(o_ref)

    for part in range(MOE_TILE // MOE_SUBTILE):
        first_row = part * MOE_SUBTILE
        base = first_row * TOKEN_ROWS

        @pl.when((hi > first_row) & (lo < first_row + MOE_SUBTILE))
        def _():
            x = _from_token_tiles(xs_ref, MOE_SUBTILE, base=base).astype(BF16)
            gate = jnp.dot(x, wg_buf[slot].astype(BF16), preferred_element_type=F32)
            up = jnp.dot(x, wu_buf[slot].astype(BF16), preferred_element_type=F32)
            he = (gate * jax.nn.sigmoid(gate) * up).astype(BF16)
            ye = jnp.dot(he, wd_buf[slot].astype(BF16), preferred_element_type=F32)
            row = first_row + lax.broadcasted_iota(jnp.int32, (MOE_SUBTILE, 1), 0)
            mine = (row >= lo) & (row < hi)
            _to_token_tiles(o_ref, jnp.where(mine, ye, _from_token_tiles(o_ref, MOE_SUBTILE, base=base)), base=base)


def _experts(items, xs, w_gate, w_up, w_down):
    n_rows = xs.shape[0] // TOKEN_ROWS
    n_items = n_rows // MOE_TILE + N_EXPERTS
    d = w_gate.shape[1]
    tile_map = lambda k, tile, *_: (tile[k], 0)
    grid_spec = pltpu.PrefetchScalarGridSpec(
        num_scalar_prefetch=SUBLANES,
        grid=(n_items,),
        in_specs=[pl.BlockSpec((MOE_TILE * TOKEN_ROWS, LANES), tile_map),
                  pl.BlockSpec(memory_space=pl.ANY),
                  pl.BlockSpec(memory_space=pl.ANY),
                  pl.BlockSpec(memory_space=pl.ANY)],
        out_specs=pl.BlockSpec((MOE_TILE * TOKEN_ROWS, LANES), tile_map),
        scratch_shapes=[pltpu.VMEM((2, d, EXPERT_FF), F32), pltpu.VMEM((2, d, EXPERT_FF), F32),
                        pltpu.VMEM((2, EXPERT_FF, d), F32), pltpu.SemaphoreType.DMA((2,))])
    return pl.pallas_call(
        _expert_kernel,
        out_shape=jax.ShapeDtypeStruct(xs.shape, WORD),
        grid_spec=grid_spec,
        compiler_params=_params("arbitrary"),
        name="experts",
    )(*(items[j, :n_items] for j in range(SUBLANES)), xs, w_gate, w_up, w_down)


COMBINE_T = 512


def _combine_kernel(pos_ref, posn_ref, ys_hbm, r_ref, x1_ref, gf_ref, gpost_ref, o_ref, buf, sem):
    i = pl.program_id(0)
    n = pl.num_programs(0)
    slot = i % 2
    slot_rows = COMBINE_T * TOKEN_ROWS

    def start_row(p_ref, sl, r):
        for k in range(2):
            dst = buf.at[sl, pl.ds(pl.multiple_of(k * slot_rows + r * TOKEN_ROWS, TOKEN_ROWS), TOKEN_ROWS), :]
            pltpu.make_async_copy(_token_rows(ys_hbm, p_ref[k, r]), dst, sem.at[sl]).start(priority=k)

    def wait_slot(sl):
        pltpu.make_async_copy(ys_hbm.at[pl.ds(0, 2 * slot_rows), :], buf.at[sl], sem.at[sl]).wait()

    def issue(p_ref, sl):
        def body(r, carry):
            start_row(p_ref, sl, r)
            return carry

        lax.fori_loop(0, COMBINE_T, body, 0, unroll=8)

    @pl.when(i == 0)
    def _():
        issue(pos_ref, 0)

    @pl.when(i + 1 < n)
    def _():
        issue(posn_ref, 1 - slot)

    wait_slot(slot)
    ya = _from_token_tiles(buf.at[slot], COMBINE_T)
    yb = _from_token_tiles(buf.at[slot], COMBINE_T, base=slot_rows)
    r = r_ref[...]
    y = r[:, R_W1:R_W1 + 1] * ya + r[:, R_W2:R_W2 + 1] * yb
    o_ref[0] = x1_ref[0] + gf_ref[0] * (_rms(y) * gpost_ref[...])


def _combine(pos, ys, r, x1, gate_f, g_post_ffn):
    b, s, d = x1.shape
    nt = s // COMBINE_T
    n_steps = b * nt
    row = lambda i: (i // nt, i % nt, 0)
    pos_spec = lambda f: pl.BlockSpec((SUBLANES, COMBINE_T), lambda i: (0, f(i)), memory_space=pltpu.SMEM)
    return pl.pallas_call(
        _combine_kernel,
        out_shape=jax.ShapeDtypeStruct((b, s, d), F32),
        grid=(n_steps,),
        in_specs=[pos_spec(lambda i: i),
                  pos_spec(lambda i: jnp.minimum(i + 1, n_steps - 1)),
                  pl.BlockSpec(memory_space=pl.ANY),
                  pl.BlockSpec((COMBINE_T, LANES), lambda i: (i, 0)),
                  pl.BlockSpec((1, COMBINE_T, d), row),
                  pl.BlockSpec((1, 1, d), lambda i: (i // nt, 0, 0)),
                  pl.BlockSpec((1, d), lambda i: (0, 0))],
        out_specs=pl.BlockSpec((1, COMBINE_T, d), row),
        scratch_shapes=[pltpu.VMEM((2, 2 * COMBINE_T * TOKEN_ROWS, LANES), WORD), pltpu.SemaphoreType.DMA((2,))],
        compiler_params=_params("arbitrary"),
        name="combine",
    )(pos, pos, ys, r, x1, gate_f, g_post_ffn.reshape(1, d))


def _rope_tables(s):
    half = HEAD_DIM // 2
    inv = ROPE_THETA ** (-np.arange(half, dtype=np.float64) * 2.0 / HEAD_DIM)
    ang = np.arange(s, dtype=np.float64)[:, None] * inv[None, :]
    cos = np.cos(ang)
    sin = np.sin(ang)
    cos = np.concatenate([cos, cos, cos, cos], axis=-1)
    sin_signed = np.concatenate([-sin, sin, -sin, sin], axis=-1)
    return jnp.asarray(cos, F32), jnp.asarray(sin_signed, F32)


def _router_weights(w_group_router, w_expert_router):
    d = w_group_router.shape[0]
    we = jnp.transpose(w_expert_router, (1, 0, 2)).reshape(d, N_EXPERTS)
    hi, lo = _split2(jnp.concatenate([w_group_router, we], axis=-1))
    return jnp.concatenate([hi, lo, jnp.zeros((d, LANES - 2 * ROUTER_COLS), BF16)], axis=-1)


def kernel(x, c, w_ada, b_ada, g_pre_mix, w_in, na_rpb, swa_sinks, beta_na, beta_swa, w_out, g_post_mix, g_pre_ffn,
           w_group_router, w_expert_router, w_gate, w_up, w_down, g_post_ffn):
    b, s, d = x.shape
    depth = w_ada.shape[0]
    cos, sin_signed = _rope_tables(s)
    for l in range(depth):
        mod = _adaln(c, w_ada[l], b_ada[l]).reshape(b, N_MOD, 1, d)
        shift_a, scale_a, gate_a, shift_f, scale_f, gate_f = (mod[:, k] for k in range(N_MOD))
        qkv, na_kg, na_vg = _qkv(x, g_pre_mix[l], scale_a, shift_a, w_in[l].astype(BF16), cos, sin_signed)
        na = _na(qkv, na_kg, na_vg, na_rpb[l])
        sw = _swa(qkv, swa_sinks[l])
        x1, h2, r, rt, cntc, cntr = _mix(na, sw, x, beta_na[l], beta_swa[l], w_out[l].astype(BF16), g_post_mix[l],
                                          gate_a, g_pre_ffn[l], scale_f, shift_f,
                                          _router_weights(w_group_router[l], w_expert_router[l]))
        pos, items = _plan(rt, cntc, cntr)
        xs = _dispatch(pos, h2)
        ys = _experts(items, xs, w_gate[l], w_up[l], w_down[l])
        x = _combine(pos, ys, r, x1, gate_f, g_post_ffn[l])
    return x
```

```python
import functools

import jax
import jax.numpy as jnp
import numpy as np
from jax import lax
from jax.experimental import pallas as pl
from jax.experimental.pallas import tpu as pltpu

D_MODEL = 1024
GRID_W = 64
HEAD_DIM = 64
NA_HEADS = 8
NA_KH = 8
NA_KW = 16
SWA_HEADS = 8
SWA_KV_HEADS = 2
SWA_WINDOW = 128
SWA_BLOCK = 128
ROPE_THETA = 10000.0
NA_WIDTH = NA_HEADS * HEAD_DIM
SWA_WIDTH = SWA_HEADS * HEAD_DIM
N_GROUPS = 4
EXPERTS_PER_GROUP = 8
N_EXPERTS = N_GROUPS * EXPERTS_PER_GROUP
EXPERT_FF = 256
N_MOD = 6
EPS = 1e-6
NEG_INF = -1e30

LANES = 128
SUBLANES = 8
R_E1, R_E2, R_W1, R_W2 = range(4)
PAIRS = NA_HEADS // 2
W_NQ, W_NK, W_NV, W_SQ, W_SKV = 0, 4, 8, 12, 16
COL_NQ, COL_SQ, COL_SK, COL_SV = 0, 4, 8, 10
QKV_TILES = 12
NA_GROUPS = GRID_W // NA_KW
NA_WIN = 2 * NA_KW
NA_WIN_START = tuple(min(max(NA_KW * g - NA_KW // 2, 0), GRID_W - NA_WIN) for g in range(NA_GROUPS))
ROUTER_BASE = N_GROUPS
VMEM_LIMIT = 56 * 1024 * 1024

F32 = jnp.float32
BF16 = jnp.bfloat16
LOG2E = 1.4426950408889634


def _rms(v):
    return v * lax.rsqrt(jnp.mean(v * v, axis=-1, keepdims=True) + EPS)


def _params(*sem):
    return pltpu.CompilerParams(dimension_semantics=sem, vmem_limit_bytes=VMEM_LIMIT)


def _adaln_kernel(c_ref, w_ref, b_ref, o_ref, *, batch):
    c = c_ref[...]
    a_t = jnp.transpose(c * jax.nn.sigmoid(c))
    w = w_ref[...]
    rows = [jnp.sum(a_t[:, bi:bi + 1] * w, axis=0, keepdims=True) for bi in range(batch)]
    rows.append(jnp.zeros((c.shape[0] - batch, w.shape[1]), F32))
    o_ref[...] = jnp.concatenate(rows, axis=0) + b_ref[...]


def _adaln(c, w_ada, b_ada):
    batch, d = c.shape
    n = w_ada.shape[1]
    tn = 1024
    b = SUBLANES
    assert batch < b
    c = jnp.pad(c, ((0, b - batch), (0, 0)))
    return pl.pallas_call(
        functools.partial(_adaln_kernel, batch=batch),
        out_shape=jax.ShapeDtypeStruct((b, n), F32),
        grid=(n // tn,),
        in_specs=[pl.BlockSpec((b, d), lambda j: (0, 0)),
                  pl.BlockSpec((d, tn), lambda j: (0, j)),
                  pl.BlockSpec((1, tn), lambda j: (0, j))],
        out_specs=pl.BlockSpec((b, tn), lambda j: (0, j)),
        compiler_params=_params("arbitrary"),
        name="adaln",
    )(c, w_ada, b_ada.reshape(1, n))[:batch]


def _rope(v, cos, sin_signed, first_half):
    rot = jnp.where(first_half, pltpu.roll(v, LANES - HEAD_DIM // 2, 1), pltpu.roll(v, HEAD_DIM // 2, 1))
    return v * cos + rot * sin_signed


QKV_CHUNK = 256


def _column_windows(v):
    grid_rows = v.shape[0] // GRID_W
    return [jnp.concatenate([v[r * GRID_W + w0:r * GRID_W + w0 + NA_WIN] for r in range(grid_rows)], axis=0)
            for w0 in NA_WIN_START]


def _qkv_kernel(x_ref, g_ref, sc_ref, sh_ref, w_ref, cos_ref, sin_ref, o_ref, kg_ref, vg_ref):
    scale = HEAD_DIM ** -0.5 * LOG2E
    lane = lax.broadcasted_iota(jnp.int32, (QKV_CHUNK, LANES), 1)
    first_half = (lane % HEAD_DIM) < HEAD_DIM // 2
    upper = lane >= HEAD_DIM

    def tile(v, j):
        return v[:, j * LANES:(j + 1) * LANES]

    for c in range(x_ref.shape[1] // QKV_CHUNK):
        rows = slice(c * QKV_CHUNK, (c + 1) * QKV_CHUNK)
        h = (_rms(x_ref[0, rows, :]) * g_ref[...]) * (1.0 + sc_ref[0]) + sh_ref[0]
        h = h.astype(BF16)
        cos = cos_ref[rows, :]
        sin = sin_ref[rows, :]

        def proj(col, width):
            return jnp.dot(h, w_ref[:, col * LANES:(col + width) * LANES], preferred_element_type=F32)

        nq, nk, nv, sq = proj(W_NQ, 4), proj(W_NK, 4), proj(W_NV, 4), proj(W_SQ, 4)
        win_rows = slice(c * QKV_CHUNK // 2, (c + 1) * QKV_CHUNK // 2)
        for j in range(PAIRS):
            o_ref[0, COL_NQ + j, rows, :] = (tile(nq, j) * scale).astype(BF16)
            o_ref[0, COL_SQ + j, rows, :] = (_rope(tile(sq, j), cos, sin, first_half) * scale).astype(BF16)
            for ref, val in ((kg_ref, tile(nk, j)), (vg_ref, tile(nv, j))):
                for g, win in enumerate(_column_windows(val)):
                    ref[0, j, g, win_rows, :] = win.astype(BF16)
        skv = proj(W_SKV, 2)
        k = _rope(tile(skv, 0), cos, sin, first_half)
        v = tile(skv, 1)
        for t, col in ((k, COL_SK), (v, COL_SV)):
            swapped = pltpu.roll(t, HEAD_DIM, 1)
            o_ref[0, col, rows, :] = jnp.where(upper, swapped, t).astype(BF16)
            o_ref[0, col + 1, rows, :] = jnp.where(upper, t, swapped).astype(BF16)


def _qkv(x, g, scale_a, shift_a, w_in, cos, sin):
    b, s, d = x.shape
    tm = 1024
    n_in = w_in.shape[1]
    windows = jax.ShapeDtypeStruct((b, PAIRS, NA_GROUPS, s // 2, LANES), BF16)
    windows_spec = pl.BlockSpec((1, PAIRS, NA_GROUPS, tm // 2, LANES), lambda bi, i: (bi, 0, 0, i, 0))
    return pl.pallas_call(
        _qkv_kernel,
        out_shape=(jax.ShapeDtypeStruct((b, QKV_TILES, s, LANES), BF16), windows, windows),
        grid=(b, s // tm),
        in_specs=[pl.BlockSpec((1, tm, d), lambda bi, i: (bi, i, 0)),
                  pl.BlockSpec((1, d), lambda bi, i: (0, 0)),
                  pl.BlockSpec((1, 1, d), lambda bi, i: (bi, 0, 0)),
                  pl.BlockSpec((1, 1, d), lambda bi, i: (bi, 0, 0)),
                  pl.BlockSpec((d, n_in), lambda bi, i: (0, 0)),
                  pl.BlockSpec((tm, LANES), lambda bi, i: (i, 0)),
                  pl.BlockSpec((tm, LANES), lambda bi, i: (i, 0))],
        out_specs=(pl.BlockSpec((1, QKV_TILES, tm, LANES), lambda bi, i: (bi, 0, i, 0)), windows_spec, windows_spec),
        compiler_params=_params("arbitrary", "arbitrary"),
        name="qkv",
    )(x, g.reshape(1, d), scale_a, shift_a, w_in, cos, sin)


NA_QROWS = 8
NA_KROWS = NA_QROWS + NA_KH
NA_BLOCKS_PER_STEP = 8
NA_INTERLEAVE = 4
NA_Q = NA_QROWS * NA_KW
NA_K = NA_KROWS * NA_WIN
NA_RPB_ROWS = 2 * NA_KH - 1
NA_RPB_COLS = 2 * NA_KW - 1


def _clamp(v, lo, hi):
    return min(max(v, lo), hi)


def _na_first_key_row(block, rows, clip):
    return clip(block * NA_QROWS - NA_KH // 2, 0, rows - NA_KROWS)


def _na_group_tables():
    def geometry(g):
        cols = [NA_KW * g + cc for cc in range(NA_KW)]
        return (NA_WIN_START[g] - NA_KW * g,) + tuple(_clamp(c - NA_KW // 2, 0, GRID_W - NA_KW) - c for c in cols)

    seen, table_of_group, representatives = {}, [], []
    for g in range(NA_GROUPS):
        key = geometry(g)
        if key not in seen:
            seen[key] = len(representatives)
            representatives.append(g)
        table_of_group.append(seen[key])
    return table_of_group, representatives


def _na_block_types(rows):
    def geometry(block):
        r = block * NA_QROWS
        a = _na_first_key_row(block, rows, _clamp)
        return (a - r,) + tuple(_clamp(r + j - NA_KH // 2, 0, rows - NA_KH) - r for j in range(NA_QROWS))

    n_blocks = rows // NA_QROWS
    interior = geometry(n_blocks // 2)
    lead = next(b for b in range(n_blocks) if geometry(b) == interior)
    trail = next(b for b in range(n_blocks) if geometry(n_blocks - 1 - b) == interior)
    assert all(geometry(b) == interior for b in range(lead, n_blocks - trail))
    return lead, trail


def _na_kernel(q_ref, k_ref, v_ref, bias_ref, o_ref, *, rows):
    upper = lax.broadcasted_iota(jnp.int32, (NA_Q, LANES), 1) >= HEAD_DIM
    key_upper = lax.broadcasted_iota(jnp.int32, (NA_K, LANES), 1) >= HEAD_DIM
    lead, trail = _na_block_types(rows)
    first_trailing = rows // NA_QROWS - trail
    n_types = lead + trail + 1
    table_of_group, _ = _na_group_tables()
    for c0 in range(0, NA_BLOCKS_PER_STEP, NA_INTERLEAVE):
        blocks = range(c0, c0 + NA_INTERLEAVE)
        starts, types, pieces = {}, {}, {}
        for c in blocks:
            block = pl.program_id(2) * NA_BLOCKS_PER_STEP + c
            a = _na_first_key_row(block, rows, jnp.clip)
            types[c] = jnp.where(block < lead, block,
                                 jnp.where(block >= first_trailing, block - first_trailing + lead + 1, lead))
            starts[c] = pl.multiple_of(a * NA_WIN, NA_WIN)
            for g in range(NA_GROUPS):
                pieces[c, g] = [slice((c * NA_QROWS + rr) * GRID_W + g * NA_KW,
                                      (c * NA_QROWS + rr) * GRID_W + (g + 1) * NA_KW) for rr in range(NA_QROWS)]
        chains = [(c, g, hh) for c in blocks for g in range(NA_GROUPS) for hh in range(2)]
        scores, probs, outs = {}, {}, {}
        for c, g, hh in chains:
            q = jnp.concatenate([q_ref[0, 0, rws, :] for rws in pieces[c, g]], axis=0)
            qm = jnp.where(upper if hh else ~upper, q, jnp.zeros_like(q))
            ks = k_ref[0, 0, g, pl.ds(starts[c], NA_K), :]
            s = lax.dot_general(qm, ks, (((1,), (1,)), ((), ())), preferred_element_type=F32)
            scores[c, g, hh] = s + bias_ref[hh, table_of_group[g] * n_types + types[c]]
        for chain in chains:
            s = scores[chain]
            probs[chain] = jnp.exp2(s - jnp.max(s, axis=-1, keepdims=True)).astype(BF16)
        for c, g, hh in chains:
            vs = v_ref[0, 0, g, pl.ds(starts[c], NA_K), :]
            v1 = jnp.where(key_upper if hh else ~key_upper, vs, jnp.ones_like(vs))
            o = jnp.dot(probs[c, g, hh], v1, preferred_element_type=F32)
            outs[c, g, hh] = o / o[:, (1 - hh) * HEAD_DIM:(1 - hh) * HEAD_DIM + 1]
        for c in blocks:
            for g in range(NA_GROUPS):
                out = jnp.where(upper, outs[c, g, 1], outs[c, g, 0]).astype(BF16)
                for rr, rws in enumerate(pieces[c, g]):
                    o_ref[0, 0, rws, :] = out[rr * NA_KW:(rr + 1) * NA_KW]


def _na_bias_kernel(rpb_ref, o_ref, *, rows):
    h = pl.program_id(0)
    cc = lax.broadcasted_iota(jnp.int32, (NA_KW, LANES), 0)
    lane = lax.broadcasted_iota(jnp.int32, (NA_KW, LANES), 1)
    w = lane % NA_WIN
    key_row_in_tile = lane // NA_WIN
    rows_per_tile = LANES // NA_WIN
    neg = jnp.full((NA_KW, LANES), NEG_INF, F32)
    base = h * NA_RPB_ROWS * NA_RPB_COLS
    lead, trail = _na_block_types(rows)
    n_blocks = rows // NA_QROWS
    type_blocks = list(range(lead + 1)) + list(range(n_blocks - trail, n_blocks))
    for table, g in enumerate(_na_group_tables()[1]):
        qc = g * NA_KW + cc
        kc = NA_WIN_START[g] + w
        c0 = jnp.clip(qc - NA_KW // 2, 0, GRID_W - NA_KW)
        in_cols = (kc >= c0) & (kc < c0 + NA_KW)
        dc = kc - qc + NA_KW - 1
        by_row_offset = []
        for d in range(NA_RPB_ROWS):
            acc = neg
            for dd in range(NA_RPB_COLS):
                acc = jnp.where(dc == dd, rpb_ref[base + d * NA_RPB_COLS + dd], acc)
            by_row_offset.append(jnp.where(in_cols, acc * LOG2E, NEG_INF))
        for ty, block in enumerate(type_blocks):
            r = block * NA_QROWS
            a = _na_first_key_row(block, rows, _clamp)
            for j in range(NA_QROWS):
                r0 = _clamp(r + j - NA_KH // 2, 0, rows - NA_KH)
                for t in range(NA_K // LANES):
                    tile = neg
                    for part in range(rows_per_tile):
                        i = t * rows_per_tile + part
                        if r0 <= a + i < r0 + NA_KH:
                            tile = jnp.where(key_row_in_tile == part, by_row_offset[a + i - (r + j) + NA_KH - 1], tile)
                    o_ref[0, table * len(type_blocks) + ty, j * NA_KW:(j + 1) * NA_KW, t * LANES:(t + 1) * LANES] = tile


def _na_bias(rpb, rows):
    n_tables = len(_na_group_tables()[1]) * (sum(_na_block_types(rows)) + 1)
    return pl.pallas_call(
        functools.partial(_na_bias_kernel, rows=rows),
        out_shape=jax.ShapeDtypeStruct((NA_HEADS, n_tables, NA_Q, NA_K), F32),
        grid=(NA_HEADS,),
        in_specs=[pl.BlockSpec(memory_space=pltpu.SMEM)],
        out_specs=pl.BlockSpec((1, n_tables, NA_Q, NA_K), lambda h: (h, 0, 0, 0)),
        compiler_params=_params("arbitrary"),
        name="na_bias",
    )(rpb.astype(F32).reshape(-1))


def _na(qkv, kg, vg, rpb):
    b, _, s, _ = qkv.shape
    rows = s // GRID_W
    assert rows % (NA_QROWS * NA_BLOCKS_PER_STEP) == 0
    tq = NA_BLOCKS_PER_STEP * NA_QROWS * GRID_W
    bias = _na_bias(rpb, rows)
    windows_spec = pl.BlockSpec((1, 1) + kg.shape[2:], lambda bi, p, i: (bi, p, 0, 0, 0))
    return pl.pallas_call(
        functools.partial(_na_kernel, rows=rows),
        out_shape=jax.ShapeDtypeStruct((b, PAIRS, s, LANES), BF16),
        grid=(b, PAIRS, s // tq),
        in_specs=[pl.BlockSpec((1, 1, tq, LANES), lambda bi, p, i: (bi, COL_NQ + p, i, 0)),
                  windows_spec,
                  windows_spec,
                  pl.BlockSpec((2,) + bias.shape[1:], lambda bi, p, i: (p, 0, 0, 0))],
        out_specs=pl.BlockSpec((1, 1, tq, LANES), lambda bi, p, i: (bi, p, i, 0)),
        compiler_params=_params("arbitrary", "arbitrary", "arbitrary"),
        name="na",
    )(qkv, kg, vg, bias)


SWA_KEYS = 3 * SWA_BLOCK
SWA_BLOCKS_PER_STEP = 16
SWA_INTERLEAVE = 2
SWA_GROUP = SWA_HEADS // SWA_KV_HEADS
assert COL_SQ % (PAIRS // SWA_KV_HEADS) == 0


def _swa_masks():
    v = np.arange(SWA_KEYS // SWA_BLOCK)[:, None, None]
    q = np.arange(SWA_BLOCK)[None, :, None]
    k = np.arange(SWA_KEYS)[None, None, :]
    return np.where(np.abs(k - v * SWA_BLOCK - q) <= SWA_WINDOW, 0.0, NEG_INF).astype(np.float32)


def _swa_kernel(sink_ref, mask_ref, q_ref, k_ref, v_ref, o_ref, *, seq):
    kv = pl.program_id(1)
    rows = SWA_GROUP * SWA_BLOCK
    lane = lax.broadcasted_iota(jnp.int32, (SWA_BLOCK, LANES), 1)
    upper = lane >= HEAD_DIM
    head = lax.broadcasted_iota(jnp.int32, (rows, 1), 0) // SWA_BLOCK
    sink = jnp.zeros((rows, 1), F32)
    for g in range(SWA_GROUP):
        sink = jnp.where(head == g, sink_ref[kv * SWA_GROUP + g], sink)
    sink = sink * LOG2E
    for j0 in range(0, SWA_BLOCKS_PER_STEP, SWA_INTERLEAVE):
        blocks = range(j0, j0 + SWA_INTERLEAVE)
        starts, scores, probs = {}, {}, {}
        for j in blocks:
            n = pl.program_id(2) * SWA_BLOCKS_PER_STEP + j
            start = pl.multiple_of(jnp.clip((n - 1) * SWA_BLOCK, 0, seq - SWA_KEYS), SWA_BLOCK)
            starts[j] = start
            ks = k_ref[0, 0, pl.ds(start, SWA_KEYS), :]
            qs = []
            for g in range(SWA_GROUP):
                q = q_ref[0, g // 2, j * SWA_BLOCK:(j + 1) * SWA_BLOCK, :]
                qs.append(jnp.where(upper if g % 2 else ~upper, q, jnp.zeros_like(q)))
            s = lax.dot_general(jnp.concatenate(qs, axis=0), ks, (((1,), (1,)), ((), ())), preferred_element_type=F32)
            mask = mask_ref[(n * SWA_BLOCK - start) // SWA_BLOCK]
            scores[j] = s + jnp.concatenate([mask] * SWA_GROUP, axis=0)
        for j in blocks:
            s = scores[j]
            m = jnp.maximum(jnp.max(s, axis=-1, keepdims=True), sink)
            e = jnp.exp2(s - m)
            probs[j] = (e.astype(BF16), jnp.sum(e, axis=-1, keepdims=True) + jnp.exp2(sink - m))
        for j in blocks:
            e, l = probs[j]
            vs = v_ref[0, 0, pl.ds(starts[j], SWA_KEYS), :]
            o = jnp.dot(e, vs, preferred_element_type=F32) / l
            for pair in range(SWA_GROUP // 2):
                even = o[(2 * pair) * SWA_BLOCK:(2 * pair + 1) * SWA_BLOCK]
                odd = o[(2 * pair + 1) * SWA_BLOCK:(2 * pair + 2) * SWA_BLOCK]
                o_ref[0, pair, j * SWA_BLOCK:(j + 1) * SWA_BLOCK, :] = jnp.where(upper, odd, even).astype(BF16)


def _swa(qkv, sinks):
    b, _, s, _ = qkv.shape
    pairs_per_kv = PAIRS // SWA_KV_HEADS
    tq = SWA_BLOCKS_PER_STEP * SWA_BLOCK
    masks = _swa_masks()
    return pl.pallas_call(
        functools.partial(_swa_kernel, seq=s),
        out_shape=jax.ShapeDtypeStruct((b, PAIRS, s, LANES), BF16),
        grid=(b, SWA_KV_HEADS, s // tq),
        in_specs=[pl.BlockSpec(memory_space=pltpu.SMEM),
                  pl.BlockSpec(masks.shape, lambda bi, kv, n: (0, 0, 0)),
                  pl.BlockSpec((1, pairs_per_kv, tq, LANES),
                               lambda bi, kv, n: (bi, COL_SQ // pairs_per_kv + kv, n, 0)),
                  pl.BlockSpec((1, 1, s, LANES), lambda bi, kv, n: (bi, COL_SK + kv, 0, 0)),
                  pl.BlockSpec((1, 1, s, LANES), lambda bi, kv, n: (bi, COL_SV + kv, 0, 0))],
        out_specs=pl.BlockSpec((1, pairs_per_kv, tq, LANES), lambda bi, kv, n: (bi, kv, n, 0)),
        compiler_params=_params("arbitrary", "arbitrary", "arbitrary"),
        name="swa",
    )(sinks, jnp.asarray(masks), qkv, qkv, qkv)


ROUTER_COLS = N_GROUPS + N_EXPERTS


def _split2(v):
    hi = v.astype(BF16)
    lo = (v - hi.astype(F32)).astype(BF16)
    return hi, lo


ROUTER_ROWS = 40


def _route(logits_t):
    sub = lax.broadcasted_iota(jnp.int32, logits_t.shape, 0)
    big = jnp.int32(LANES)
    gmask = sub < N_GROUPS
    gl = jnp.where(gmask, logits_t, NEG_INF)
    gmax = jnp.max(gl, axis=0, keepdims=True)
    g_top = jnp.min(jnp.where(gmask & (gl == gmax), sub, big), axis=0, keepdims=True)
    g_weight = 1.0 / jnp.sum(jnp.where(gmask, jnp.exp(gl - gmax), 0.0), axis=0, keepdims=True)
    lo = ROUTER_BASE + g_top * EXPERTS_PER_GROUP
    emask = (sub >= lo) & (sub < lo + EXPERTS_PER_GROUP)
    el = jnp.where(emask, logits_t, NEG_INF)
    m1 = jnp.max(el, axis=0, keepdims=True)
    i1 = jnp.min(jnp.where(emask & (el == m1), sub, big), axis=0, keepdims=True)
    emask2 = emask & (sub != i1)
    el2 = jnp.where(emask2, logits_t, NEG_INF)
    m2 = jnp.max(el2, axis=0, keepdims=True)
    i2 = jnp.min(jnp.where(emask2 & (el2 == m2), sub, big), axis=0, keepdims=True)
    e2 = jnp.exp(m2 - m1)
    w1 = g_weight / (1.0 + e2)
    w2 = g_weight * e2 / (1.0 + e2)
    return i1 - ROUTER_BASE, i2 - ROUTER_BASE, w1, w2


TOKEN_ROWS = D_MODEL // (2 * LANES)
WORD = jnp.uint32


def _to_token_tiles(ref, v, base=0):
    t, d = v.shape
    words = pltpu.pack_elementwise([v[:, :d // 2], v[:, d // 2:]], packed_dtype=BF16)
    for s in range(TOKEN_ROWS):
        ref[pl.ds(base + s, t, stride=TOKEN_ROWS), :] = words[:, s * LANES:(s + 1) * LANES]


def _token_words(ref, t, base=0):
    return jnp.concatenate([ref[pl.ds(base + s, t, stride=TOKEN_ROWS), :] for s in range(TOKEN_ROWS)], axis=-1)


def _from_token_tiles(ref, t, base=0):
    words = _token_words(ref, t, base)
    halves = [pltpu.unpack_elementwise(words, index=j, packed_dtype=BF16, unpacked_dtype=F32) for j in range(2)]
    return jnp.concatenate(halves, axis=-1)


MIX_CHUNK = 256


def _mix_kernel(na_ref, sw_ref, x_ref, bna_ref, bsw_ref, wo_ref, gpm_ref, ga_ref, gpf_ref, scf_ref, shf_ref,
                wr_ref, x1_ref, h2_ref, r_ref, rt_ref, cntc_ref, cntr_ref):
    first_step = (pl.program_id(0) == 0) & (pl.program_id(1) == 0)

    @pl.when(first_step)
    def _():
        cntc_ref[...] = jnp.zeros_like(cntc_ref)
        cntr_ref[...] = jnp.zeros_like(cntr_ref)

    t = MIX_CHUNK
    chunks = range(x_ref.shape[1] // t)
    row_slices = [slice(c * t, (c + 1) * t) for c in chunks]
    mixes, h2s, all_logits = [], [], []
    for rows in row_slices:
        def heads(ref):
            return jnp.concatenate([ref[0, j, rows, :] for j in range(PAIRS)], axis=-1).astype(F32)

        na = (_rms(heads(na_ref)) * bna_ref[...]).astype(BF16)
        sw = (_rms(heads(sw_ref)) * bsw_ref[...]).astype(BF16)
        mixes.append(jnp.dot(na, wo_ref[:NA_WIDTH, :], preferred_element_type=F32)
                     + jnp.dot(sw, wo_ref[NA_WIDTH:, :], preferred_element_type=F32))
    gate_gain = ga_ref[0] * gpm_ref[...]
    ffn_gain = gpf_ref[...] * (1.0 + scf_ref[0])
    for c, rows in zip(chunks, row_slices):
        x1 = x_ref[0, rows, :] + _rms(mixes[c]) * gate_gain
        x1_ref[0, rows, :] = x1
        h2 = _rms(x1) * ffn_gain + shf_ref[0]
        _to_token_tiles(h2_ref, h2, base=c * t * TOKEN_ROWS)
        h2s.append(h2)
    for c in chunks:
        h_hi, h_lo = _split2(h2s[c])
        both = (jnp.dot(h_hi, wr_ref[...], preferred_element_type=F32)
                + jnp.dot(h_lo, wr_ref[...], preferred_element_type=F32))
        all_logits.append(both + pltpu.roll(both, LANES - ROUTER_COLS, 1))
    for c, rows in zip(chunks, row_slices):
        e1, e2, w1, w2 = _route(jnp.transpose(all_logits[c])[:ROUTER_ROWS])
        fields = [None] * 4
        fields[R_E1], fields[R_E2], fields[R_W1], fields[R_W2] = e1.astype(F32), e2.astype(F32), w1, w2
        rt = jnp.concatenate(fields + [jnp.zeros((SUBLANES - 4, t), F32)], axis=0)
        rt_ref[:, rows] = rt
        r = jnp.transpose(jnp.concatenate([rt, jnp.zeros((LANES - SUBLANES, t), F32)], axis=0))
        r_ref[rows, :] = r
        sub = lax.broadcasted_iota(jnp.int32, (N_EXPERTS, t), 0)
        on_sub = ((sub == e1) | (sub == e2)).astype(F32)
        cntc_ref[...] += jnp.broadcast_to(jnp.sum(on_sub, axis=1, keepdims=True), cntc_ref.shape)
        lane = lax.broadcasted_iota(jnp.int32, r.shape, 1).astype(F32)
        on_lane = ((lane == r[:, R_E1:R_E1 + 1]) | (lane == r[:, R_E2:R_E2 + 1])).astype(F32)
        cntr_ref[...] += jnp.broadcast_to(jnp.sum(on_lane, axis=0, keepdims=True), cntr_ref.shape)


def _mix(na, sw, x, beta_na, beta_swa, w_out, g_post_mix, gate_a, g_pre_ffn, scale_f, shift_f, w_router3):
    b, s, d = x.shape
    tm = 1024
    nt = s // tm
    row = lambda bi, i: (bi, i, 0)
    const2 = lambda bi, i: (0, 0)
    per_b = lambda bi, i: (bi, 0, 0)
    return pl.pallas_call(
        _mix_kernel,
        out_shape=(jax.ShapeDtypeStruct((b, s, d), F32),
                   jax.ShapeDtypeStruct((b * s * TOKEN_ROWS, LANES), WORD),
                   jax.ShapeDtypeStruct((b * s, LANES), F32),
                   jax.ShapeDtypeStruct((SUBLANES, b * s), F32),
                   jax.ShapeDtypeStruct((N_EXPERTS, LANES), F32),
                   jax.ShapeDtypeStruct((SUBLANES, LANES), F32)),
        grid=(b, s // tm),
        in_specs=[pl.BlockSpec((1, PAIRS, tm, LANES), lambda bi, i: (bi, 0, i, 0)),
                  pl.BlockSpec((1, PAIRS, tm, LANES), lambda bi, i: (bi, 0, i, 0)),
                  pl.BlockSpec((1, tm, d), row),
                  pl.BlockSpec((1, NA_WIDTH), const2),
                  pl.BlockSpec((1, SWA_WIDTH), const2),
                  pl.BlockSpec((NA_WIDTH + SWA_WIDTH, d), const2),
                  pl.BlockSpec((1, d), const2),
                  pl.BlockSpec((1, 1, d), per_b),
                  pl.BlockSpec((1, d), const2),
                  pl.BlockSpec((1, 1, d), per_b),
                  pl.BlockSpec((1, 1, d), per_b),
                  pl.BlockSpec((d, LANES), const2)],
        out_specs=(pl.BlockSpec((1, tm, d), row),
                   pl.BlockSpec((tm * TOKEN_ROWS, LANES), lambda bi, i: (bi * nt + i, 0)),
                   pl.BlockSpec((tm, LANES), lambda bi, i: (bi * nt + i, 0)),
                   pl.BlockSpec((SUBLANES, tm), lambda bi, i: (0, bi * nt + i)),
                   pl.BlockSpec((N_EXPERTS, LANES), const2),
                   pl.BlockSpec((SUBLANES, LANES), const2)),
        compiler_params=_params("arbitrary", "arbitrary"),
        name="mix",
    )(na, sw, x, beta_na.reshape(1, -1), beta_swa.reshape(1, -1), w_out, g_post_mix.reshape(1, d), gate_a,
      g_pre_ffn.reshape(1, d), scale_f, shift_f, w_router3)


MOE_TILE = 1024
MOE_SUBTILE = 256
PLAN_T = 2048
I_TILE, I_EXPERT, I_LO, I_HI, I_FIRST, I_NEW, I_NEXT, I_ORDER = range(8)


def _plan_kernel(rt_ref, cntc_ref, cntr_ref, pos_ref, items_ref, start_ref, carry_ref, *, n_tiles, n_items):
    i = pl.program_id(0)
    sub = lax.broadcasted_iota(jnp.int32, (N_EXPERTS, LANES), 0)
    lane = lax.broadcasted_iota(jnp.int32, (N_EXPERTS, LANES), 1)

    @pl.when(i == 0)
    def _():
        c_col = cntc_ref[:, 0:1]
        c_row = cntr_ref[0:1, :]
        s_col = jnp.sum(jnp.where(lane < sub, c_row, 0.0), axis=1, keepdims=True)
        s_row = jnp.sum(jnp.where(sub < lane, c_col, 0.0), axis=0, keepdims=True)
        start_ref[...] = jnp.broadcast_to(s_col, start_ref.shape)
        carry_ref[...] = jnp.zeros_like(carry_ref)

        def tiles_of(s, c):
            first = jnp.floor(s * (1.0 / MOE_TILE))
            last = jnp.floor((s + c - 1.0) * (1.0 / MOE_TILE))
            return first, jnp.where(c > 0.0, last - first + 1.0, 0.0)

        f_col, n_col = tiles_of(s_col, c_col)
        _, n_row = tiles_of(s_row, c_row)
        i_col = jnp.sum(jnp.where(lane < sub, n_row, 0.0), axis=1, keepdims=True)
        total = jnp.sum(n_col, axis=0, keepdims=True)
        k = lax.broadcasted_iota(jnp.int32, (N_EXPERTS, n_items), 1).astype(F32)
        subk = lax.broadcasted_iota(jnp.int32, (N_EXPERTS, n_items), 0).astype(F32)
        ek = jnp.sum(jnp.where(i_col + n_col <= k, 1.0, 0.0), axis=0, keepdims=True)
        k0 = k[0:1]
        valid = k0 < total
        sel = subk == ek

        def pick(v):
            return jnp.sum(jnp.where(sel, v, 0.0), axis=0, keepdims=True)

        i_k, f_k, s_k, c_k = pick(i_col), pick(f_col), pick(s_col), pick(c_col)
        tile = f_k + (k0 - i_k)
        row0 = tile * MOE_TILE
        lo = jnp.maximum(s_k, row0) - row0
        hi = jnp.minimum(s_k + c_k, row0 + MOE_TILE) - row0
        present = n_col > 0.0
        last_expert = jnp.sum(jnp.where(i_col + n_col <= total - 1.0, 1.0, 0.0), axis=0, keepdims=True)
        nxt = jnp.min(jnp.where(present & (subk > ek), subk, float(N_EXPERTS)), axis=0, keepdims=True)
        order = jnp.sum(jnp.where(present & (subk < ek), 1.0, 0.0), axis=0, keepdims=True)
        rows = [jnp.where(valid, tile, n_tiles - 1.0), jnp.where(valid, ek, last_expert),
                jnp.where(valid, lo, 0.0), jnp.where(valid, hi, 0.0),
                jnp.where(valid & (lo == 0.0), 1.0, 0.0),
                jnp.where(valid & (k0 == i_k), 1.0, 0.0), jnp.where(valid, nxt, float(N_EXPERTS)),
                jnp.where(valid, order, 0.0)]
        assert len(rows) == SUBLANES
        items_ref[...] = jnp.concatenate(rows, axis=0).astype(jnp.int32)

    t = rt_ref.shape[1]
    e1 = rt_ref[R_E1:R_E1 + 1, :]
    e2 = rt_ref[R_E2:R_E2 + 1, :]
    sub_t = lax.broadcasted_iota(jnp.int32, (N_EXPERTS, t), 0).astype(F32)
    oh1 = sub_t == e1
    oh2 = sub_t == e2
    oh = (oh1 | oh2).astype(F32)
    before = (lax.broadcasted_iota(jnp.int32, (t, t), 0) < lax.broadcasted_iota(jnp.int32, (t, t), 1)).astype(BF16)
    rank = jnp.dot(oh.astype(BF16), before, preferred_element_type=F32)
    base = start_ref[:, 0:1] + carry_ref[:, 0:1] + rank
    pos1 = jnp.sum(jnp.where(oh1, base, 0.0), axis=0, keepdims=True)
    pos2 = jnp.sum(jnp.where(oh2, base, 0.0), axis=0, keepdims=True)
    carry_ref[...] += jnp.broadcast_to(jnp.sum(oh, axis=1, keepdims=True), carry_ref.shape)
    pos = jnp.concatenate([pos1, pos2] + [jnp.zeros_like(pos1)] * (SUBLANES - 2), axis=0)
    pos_ref[...] = pos.astype(jnp.int32)


def _plan(rt, cntc, cntr):
    n = rt.shape[1]
    n_tiles = 2 * n // MOE_TILE
    n_items = 2 * LANES
    assert n_tiles + N_EXPERTS <= n_items
    return pl.pallas_call(
        functools.partial(_plan_kernel, n_tiles=n_tiles, n_items=n_items),
        out_shape=(jax.ShapeDtypeStruct((SUBLANES, n), jnp.int32),
                   jax.ShapeDtypeStruct((SUBLANES, n_items), jnp.int32)),
        grid=(n // PLAN_T,),
        in_specs=[pl.BlockSpec((SUBLANES, PLAN_T), lambda i: (0, i)),
                  pl.BlockSpec((N_EXPERTS, LANES), lambda i: (0, 0)),
                  pl.BlockSpec((SUBLANES, LANES), lambda i: (0, 0))],
        out_specs=(pl.BlockSpec((SUBLANES, PLAN_T), lambda i: (0, i)),
                   pl.BlockSpec((SUBLANES, n_items), lambda i: (0, 0))),
        scratch_shapes=[pltpu.VMEM((N_EXPERTS, LANES), F32), pltpu.VMEM((N_EXPERTS, LANES), F32)],
        compiler_params=_params("arbitrary"),
        name="plan",
    )(rt, cntc, cntr)


DISPATCH_T = 2048


def _token_rows(ref, index):
    return ref.at[pl.ds(pl.multiple_of(index * TOKEN_ROWS, TOKEN_ROWS), TOKEN_ROWS), :]


def _dispatch_kernel(pos_ref, h_ref, xs_hbm, sem):
    def body(r, carry):
        src = _token_rows(h_ref, r)
        for k in range(2):
            pltpu.make_async_copy(src, _token_rows(xs_hbm, pos_ref[k, r]), sem.at[0]).start(priority=k)
        return carry

    lax.fori_loop(0, DISPATCH_T, body, 0, unroll=8)
    for k in range(2):
        pltpu.make_async_copy(h_ref, xs_hbm.at[pl.ds(0, DISPATCH_T * TOKEN_ROWS), :], sem.at[0]).wait()


def _dispatch(pos, h2):
    n = pos.shape[1]
    return pl.pallas_call(
        _dispatch_kernel,
        out_shape=jax.ShapeDtypeStruct((2 * n * TOKEN_ROWS, LANES), WORD),
        grid=(n // DISPATCH_T,),
        in_specs=[pl.BlockSpec((SUBLANES, DISPATCH_T), lambda i: (0, i), memory_space=pltpu.SMEM),
                  pl.BlockSpec((DISPATCH_T * TOKEN_ROWS, LANES), lambda i: (i, 0))],
        out_specs=pl.BlockSpec(memory_space=pl.ANY),
        scratch_shapes=[pltpu.SemaphoreType.DMA((1,))],
        compiler_params=_params("arbitrary"),
        name="dispatch",
    )(pos, h2)


def _expert_kernel(tile_ref, exp_ref, lo_ref, hi_ref, first_ref, new_ref, next_ref, order_ref,
                   xs_ref, wg_hbm, wu_hbm, wd_hbm, o_ref, wg_buf, wu_buf, wd_buf, sem):
    k = pl.program_id(0)
    lo = lo_ref[k]
    hi = hi_ref[k]
    slot = order_ref[k] % 2

    def weight_copies(expert, sl):
        return [pltpu.make_async_copy(src.at[expert], dst.at[sl], sem.at[sl])
                for src, dst in ((wg_hbm, wg_buf), (wu_hbm, wu_buf), (wd_hbm, wd_buf))]

    @pl.when(k == 0)
    def _():
        for copy in weight_copies(exp_ref[0], 0):
            copy.start()

    @pl.when(new_ref[k] == 1)
    def _():
        for copy in weight_copies(exp_ref[k], slot):
            copy.wait()

        @pl.when(next_ref[k] < N_EXPERTS)
        def _():
            for copy in weight_copies(next_ref[k], 1 - slot):
                copy.start()

    @pl.when(first_ref[k] == 1)
    def _():
        o_ref[...] = jnp.zeros_like(o_ref)

    for part in range(MOE_TILE // MOE_SUBTILE):
        first_row = part * MOE_SUBTILE
        base = first_row * TOKEN_ROWS

        @pl.when((hi > first_row) & (lo < first_row + MOE_SUBTILE))
        def _():
            x = _from_token_tiles(xs_ref, MOE_SUBTILE, base=base).astype(BF16)
            gate = jnp.dot(x, wg_buf[slot].astype(BF16), preferred_element_type=F32)
            up = jnp.dot(x, wu_buf[slot].astype(BF16), preferred_element_type=F32)
            he = (gate * jax.nn.sigmoid(gate) * up).astype(BF16)
            ye = jnp.dot(he, wd_buf[slot].astype(BF16), preferred_element_type=F32)
            row = first_row + lax.broadcasted_iota(jnp.int32, (MOE_SUBTILE, 1), 0)
            mine = (row >= lo) & (row < hi)
            _to_token_tiles(o_ref, jnp.where(mine, ye, _from_token_tiles(o_ref, MOE_SUBTILE, base=base)), base=base)


def _experts(items, xs, w_gate, w_up, w_down):
    n_rows = xs.shape[0] // TOKEN_ROWS
    n_items = n_rows // MOE_TILE + N_EXPERTS
    d = w_gate.shape[1]
    tile_map = lambda k, tile, *_: (tile[k], 0)
    grid_spec = pltpu.PrefetchScalarGridSpec(
        num_scalar_prefetch=SUBLANES,
        grid=(n_items,),
        in_specs=[pl.BlockSpec((MOE_TILE * TOKEN_ROWS, LANES), tile_map),
                  pl.BlockSpec(memory_space=pl.ANY),
                  pl.BlockSpec(memory_space=pl.ANY),
                  pl.BlockSpec(memory_space=pl.ANY)],
        out_specs=pl.BlockSpec((MOE_TILE * TOKEN_ROWS, LANES), tile_map),
        scratch_shapes=[pltpu.VMEM((2, d, EXPERT_FF), F32), pltpu.VMEM((2, d, EXPERT_FF), F32),
                        pltpu.VMEM((2, EXPERT_FF, d), F32), pltpu.SemaphoreType.DMA((2,))])
    return pl.pallas_call(
        _expert_kernel,
        out_shape=jax.ShapeDtypeStruct(xs.shape, WORD),
        grid_spec=grid_spec,
        compiler_params=_params("arbitrary"),
        name="experts",
    )(*(items[j, :n_items] for j in range(SUBLANES)), xs, w_gate, w_up, w_down)


COMBINE_T = 1024


def _combine_kernel(pos_ref, posn_ref, ys_hbm, r_ref, x1_ref, gf_ref, gpost_ref, o_ref, buf, sem):
    i = pl.program_id(0)
    n = pl.num_programs(0)
    slot = i % 2
    slot_rows = COMBINE_T * TOKEN_ROWS

    def start_row(p_ref, sl, r):
        for k in range(2):
            dst = buf.at[sl, pl.ds(pl.multiple_of(k * slot_rows + r * TOKEN_ROWS, TOKEN_ROWS), TOKEN_ROWS), :]
            pltpu.make_async_copy(_token_rows(ys_hbm, p_ref[k, r]), dst, sem.at[sl]).start(priority=k)

    def wait_slot(sl):
        pltpu.make_async_copy(ys_hbm.at[pl.ds(0, 2 * slot_rows), :], buf.at[sl], sem.at[sl]).wait()

    def issue(p_ref, sl):
        def body(r, carry):
            start_row(p_ref, sl, r)
            return carry

        lax.fori_loop(0, COMBINE_T, body, 0, unroll=8)

    @pl.when(i == 0)
    def _():
        issue(pos_ref, 0)

    @pl.when(i + 1 < n)
    def _():
        issue(posn_ref, 1 - slot)

    wait_slot(slot)
    ya = _from_token_tiles(buf.at[slot], COMBINE_T)
    yb = _from_token_tiles(buf.at[slot], COMBINE_T, base=slot_rows)
    r = r_ref[...]
    y = r[:, R_W1:R_W1 + 1] * ya + r[:, R_W2:R_W2 + 1] * yb
    o_ref[0] = x1_ref[0] + gf_ref[0] * (_rms(y) * gpost_ref[...])


def _combine(pos, ys, r, x1, gate_f, g_post_ffn):
    b, s, d = x1.shape
    nt = s // COMBINE_T
    n_steps = b * nt
    row = lambda i: (i // nt, i % nt, 0)
    pos_spec = lambda f: pl.BlockSpec((SUBLANES, COMBINE_T), lambda i: (0, f(i)), memory_space=pltpu.SMEM)
    return pl.pallas_call(
        _combine_kernel,
        out_shape=jax.ShapeDtypeStruct((b, s, d), F32),
        grid=(n_steps,),
        in_specs=[pos_spec(lambda i: i),
                  pos_spec(lambda i: jnp.minimum(i + 1, n_steps - 1)),
                  pl.BlockSpec(memory_space=pl.ANY),
                  pl.BlockSpec((COMBINE_T, LANES), lambda i: (i, 0)),
                  pl.BlockSpec((1, COMBINE_T, d), row),
                  pl.BlockSpec((1, 1, d), lambda i: (i // nt, 0, 0)),
                  pl.BlockSpec((1, d), lambda i: (0, 0))],
        out_specs=pl.BlockSpec((1, COMBINE_T, d), row),
        scratch_shapes=[pltpu.VMEM((2, 2 * COMBINE_T * TOKEN_ROWS, LANES), WORD), pltpu.SemaphoreType.DMA((2,))],
        compiler_params=_params("arbitrary"),
        name="combine",
    )(pos, pos, ys, r, x1, gate_f, g_post_ffn.reshape(1, d))


def _rope_tables(s):
    half = HEAD_DIM // 2
    inv = ROPE_THETA ** (-np.arange(half, dtype=np.float64) * 2.0 / HEAD_DIM)
    ang = np.arange(s, dtype=np.float64)[:, None] * inv[None, :]
    cos = np.cos(ang)
    sin = np.sin(ang)
    cos = np.concatenate([cos, cos, cos, cos], axis=-1)
    sin_signed = np.concatenate([-sin, sin, -sin, sin], axis=-1)
    return jnp.asarray(cos, F32), jnp.asarray(sin_signed, F32)


def _router_weights(w_group_router, w_expert_router):
    d = w_group_router.shape[0]
    we = jnp.transpose(w_expert_router, (1, 0, 2)).reshape(d, N_EXPERTS)
    hi, lo = _split2(jnp.concatenate([w_group_router, we], axis=-1))
    return jnp.concatenate([hi, lo, jnp.zeros((d, LANES - 2 * ROUTER_COLS), BF16)], axis=-1)


def kernel(x, c, w_ada, b_ada, g_pre_mix, w_in, na_rpb, swa_sinks, beta_na, beta_swa, w_out, g_post_mix, g_pre_ffn,
           w_group_router, w_expert_router, w_gate, w_up, w_down, g_post_ffn):
    b, s, d = x.shape
    depth = w_ada.shape[0]
    cos, sin_signed = _rope_tables(s)
    for l in range(depth):
        mod = _adaln(c, w_ada[l], b_ada[l]).reshape(b, N_MOD, 1, d)
        shift_a, scale_a, gate_a, shift_f, scale_f, gate_f = (mod[:, k] for k in range(N_MOD))
        qkv, na_kg, na_vg = _qkv(x, g_pre_mix[l], scale_a, shift_a, w_in[l].astype(BF16), cos, sin_signed)
        na = _na(qkv, na_kg, na_vg, na_rpb[l])
        sw = _swa(qkv, swa_sinks[l])
        x1, h2, r, rt, cntc, cntr = _mix(na, sw, x, beta_na[l], beta_swa[l], w_out[l].astype(BF16), g_post_mix[l],
                                          gate_a, g_pre_ffn[l], scale_f, shift_f,
                                          _router_weights(w_group_router[l], w_expert_router[l]))
        pos, items = _plan(rt, cntc, cntr)
        xs = _dispatch(pos, h2)
        ys = _experts(items, xs, w_gate[l], w_up[l], w_down[l])
        x = _combine(pos, ys, r, x1, gate_f, g_post_ffn[l])
    return x
```

```python
import functools

import jax
import jax.numpy as jnp
import numpy as np
from jax import lax
from jax.experimental import pallas as pl
from jax.experimental.pallas import tpu as pltpu

D_MODEL = 1024
GRID_W = 64
HEAD_DIM = 64
NA_HEADS = 8
NA_KH = 8
NA_KW = 16
SWA_HEADS = 8
SWA_KV_HEADS = 2
SWA_WINDOW = 128
SWA_BLOCK = 128
ROPE_THETA = 10000.0
NA_WIDTH = NA_HEADS * HEAD_DIM
SWA_WIDTH = SWA_HEADS * HEAD_DIM
N_GROUPS = 4
EXPERTS_PER_GROUP = 8
N_EXPERTS = N_GROUPS * EXPERTS_PER_GROUP
EXPERT_FF = 256
N_MOD = 6
EPS = 1e-6
NEG_INF = -1e30

LANES = 128
SUBLANES = 8
R_E1, R_E2, R_W1, R_W2 = range(4)
PAIRS = NA_HEADS // 2
W_NQ, W_NK, W_NV, W_SQ, W_SKV = 0, 4, 8, 12, 16
COL_NQ, COL_SQ, COL_SK, COL_SV = 0, 4, 8, 10
QKV_TILES = 12
NA_GROUPS = GRID_W // NA_KW
NA_WIN = 2 * NA_KW
NA_WIN_START = tuple(min(max(NA_KW * g - NA_KW // 2, 0), GRID_W - NA_WIN) for g in range(NA_GROUPS))
ROUTER_BASE = N_GROUPS
VMEM_LIMIT = 56 * 1024 * 1024

F32 = jnp.float32
BF16 = jnp.bfloat16
LOG2E = 1.4426950408889634


def _rms(v):
    return v * lax.rsqrt(jnp.mean(v * v, axis=-1, keepdims=True) + EPS)


def _params(*sem):
    return pltpu.CompilerParams(dimension_semantics=sem, vmem_limit_bytes=VMEM_LIMIT)


def _adaln_kernel(c_ref, w_ref, b_ref, o_ref, *, batch):
    c = c_ref[...]
    a_t = jnp.transpose(c * jax.nn.sigmoid(c))
    w = w_ref[...]
    rows = [jnp.sum(a_t[:, bi:bi + 1] * w, axis=0, keepdims=True) for bi in range(batch)]
    rows.append(jnp.zeros((c.shape[0] - batch, w.shape[1]), F32))
    o_ref[...] = jnp.concatenate(rows, axis=0) + b_ref[...]


def _adaln(c, w_ada, b_ada):
    batch, d = c.shape
    n = w_ada.shape[1]
    tn = 1024
    b = SUBLANES
    assert batch < b
    c = jnp.pad(c, ((0, b - batch), (0, 0)))
    return pl.pallas_call(
        functools.partial(_adaln_kernel, batch=batch),
        out_shape=jax.ShapeDtypeStruct((b, n), F32),
        grid=(n // tn,),
        in_specs=[pl.BlockSpec((b, d), lambda j: (0, 0)),
                  pl.BlockSpec((d, tn), lambda j: (0, j)),
                  pl.BlockSpec((1, tn), lambda j: (0, j))],
        out_specs=pl.BlockSpec((b, tn), lambda j: (0, j)),
        compiler_params=_params("arbitrary"),
        name="adaln",
    )(c, w_ada, b_ada.reshape(1, n))[:batch]


def _rope(v, cos, sin_signed, first_half):
    rot = jnp.where(first_half, pltpu.roll(v, LANES - HEAD_DIM // 2, 1), pltpu.roll(v, HEAD_DIM // 2, 1))
    return v * cos + rot * sin_signed


QKV_CHUNK = 256


def _column_windows(v):
    grid_rows = v.shape[0] // GRID_W
    return [jnp.concatenate([v[r * GRID_W + w0:r * GRID_W + w0 + NA_WIN] for r in range(grid_rows)], axis=0)
            for w0 in NA_WIN_START]


def _qkv_kernel(x_ref, g_ref, sc_ref, sh_ref, w_ref, cos_ref, sin_ref, o_ref, kg_ref, vg_ref):
    scale = HEAD_DIM ** -0.5 * LOG2E
    lane = lax.broadcasted_iota(jnp.int32, (QKV_CHUNK, LANES), 1)
    first_half = (lane % HEAD_DIM) < HEAD_DIM // 2
    upper = lane >= HEAD_DIM

    def tile(v, j):
        return v[:, j * LANES:(j + 1) * LANES]

    for c in range(x_ref.shape[1] // QKV_CHUNK):
        rows = slice(c * QKV_CHUNK, (c + 1) * QKV_CHUNK)
        h = (_rms(x_ref[0, rows, :]) * g_ref[...]) * (1.0 + sc_ref[0]) + sh_ref[0]
        h = h.astype(BF16)
        cos = cos_ref[rows, :]
        sin = sin_ref[rows, :]

        def proj(col, width):
            return jnp.dot(h, w_ref[:, col * LANES:(col + width) * LANES], preferred_element_type=F32)

        nq, nk, nv, sq = proj(W_NQ, 4), proj(W_NK, 4), proj(W_NV, 4), proj(W_SQ, 4)
        win_rows = slice(c * QKV_CHUNK // 2, (c + 1) * QKV_CHUNK // 2)
        for j in range(PAIRS):
            o_ref[0, COL_NQ + j, rows, :] = (tile(nq, j) * scale).astype(BF16)
            o_ref[0, COL_SQ + j, rows, :] = (_rope(tile(sq, j), cos, sin, first_half) * scale).astype(BF16)
            for ref, val in ((kg_ref, tile(nk, j)), (vg_ref, tile(nv, j))):
                for g, win in enumerate(_column_windows(val)):
                    ref[0, j, g, win_rows, :] = win.astype(BF16)
        skv = proj(W_SKV, 2)
        k = _rope(tile(skv, 0), cos, sin, first_half)
        v = tile(skv, 1)
        for t, col in ((k, COL_SK), (v, COL_SV)):
            swapped = pltpu.roll(t, HEAD_DIM, 1)
            o_ref[0, col, rows, :] = jnp.where(upper, swapped, t).astype(BF16)
            o_ref[0, col + 1, rows, :] = jnp.where(upper, t, swapped).astype(BF16)


def _qkv(x, g, scale_a, shift_a, w_in, cos, sin):
    b, s, d = x.shape
    tm = 1024
    n_in = w_in.shape[1]
    windows = jax.ShapeDtypeStruct((b, PAIRS, NA_GROUPS, s // 2, LANES), BF16)
    windows_spec = pl.BlockSpec((1, PAIRS, NA_GROUPS, tm // 2, LANES), lambda bi, i: (bi, 0, 0, i, 0))
    return pl.pallas_call(
        _qkv_kernel,
        out_shape=(jax.ShapeDtypeStruct((b, QKV_TILES, s, LANES), BF16), windows, windows),
        grid=(b, s // tm),
        in_specs=[pl.BlockSpec((1, tm, d), lambda bi, i: (bi, i, 0)),
                  pl.BlockSpec((1, d), lambda bi, i: (0, 0)),
                  pl.BlockSpec((1, 1, d), lambda bi, i: (bi, 0, 0)),
                  pl.BlockSpec((1, 1, d), lambda bi, i: (bi, 0, 0)),
                  pl.BlockSpec((d, n_in), lambda bi, i: (0, 0)),
                  pl.BlockSpec((tm, LANES), lambda bi, i: (i, 0)),
                  pl.BlockSpec((tm, LANES), lambda bi, i: (i, 0))],
        out_specs=(pl.BlockSpec((1, QKV_TILES, tm, LANES), lambda bi, i: (bi, 0, i, 0)), windows_spec, windows_spec),
        compiler_params=_params("arbitrary", "arbitrary"),
        name="qkv",
    )(x, g.reshape(1, d), scale_a, shift_a, w_in, cos, sin)


NA_QROWS = 8
NA_KROWS = NA_QROWS + NA_KH
NA_BLOCKS_PER_STEP = 16
NA_INTERLEAVE = 4
NA_Q = NA_QROWS * NA_KW
NA_K = NA_KROWS * NA_WIN
NA_RPB_ROWS = 2 * NA_KH - 1
NA_RPB_COLS = 2 * NA_KW - 1


def _clamp(v, lo, hi):
    return min(max(v, lo), hi)


def _na_first_key_row(block, rows, clip):
    return clip(block * NA_QROWS - NA_KH // 2, 0, rows - NA_KROWS)


def _na_group_tables():
    def geometry(g):
        cols = [NA_KW * g + cc for cc in range(NA_KW)]
        return (NA_WIN_START[g] - NA_KW * g,) + tuple(_clamp(c - NA_KW // 2, 0, GRID_W - NA_KW) - c for c in cols)

    seen, table_of_group, representatives = {}, [], []
    for g in range(NA_GROUPS):
        key = geometry(g)
        if key not in seen:
            seen[key] = len(representatives)
            representatives.append(g)
        table_of_group.append(seen[key])
    return table_of_group, representatives


def _na_block_types(rows):
    def geometry(block):
        r = block * NA_QROWS
        a = _na_first_key_row(block, rows, _clamp)
        return (a - r,) + tuple(_clamp(r + j - NA_KH // 2, 0, rows - NA_KH) - r for j in range(NA_QROWS))

    n_blocks = rows // NA_QROWS
    interior = geometry(n_blocks // 2)
    lead = next(b for b in range(n_blocks) if geometry(b) == interior)
    trail = next(b for b in range(n_blocks) if geometry(n_blocks - 1 - b) == interior)
    assert all(geometry(b) == interior for b in range(lead, n_blocks - trail))
    return lead, trail


def _na_kernel(q_ref, k_ref, v_ref, bias_ref, o_ref, *, rows):
    upper = lax.broadcasted_iota(jnp.int32, (NA_Q, LANES), 1) >= HEAD_DIM
    key_upper = lax.broadcasted_iota(jnp.int32, (NA_K, LANES), 1) >= HEAD_DIM
    lead, trail = _na_block_types(rows)
    first_trailing = rows // NA_QROWS - trail
    n_types = lead + trail + 1
    table_of_group, _ = _na_group_tables()
    for c0 in range(0, NA_BLOCKS_PER_STEP, NA_INTERLEAVE):
        blocks = range(c0, c0 + NA_INTERLEAVE)
        starts, types, pieces = {}, {}, {}
        for c in blocks:
            block = pl.program_id(2) * NA_BLOCKS_PER_STEP + c
            a = _na_first_key_row(block, rows, jnp.clip)
            types[c] = jnp.where(block < lead, block,
                                 jnp.where(block >= first_trailing, block - first_trailing + lead + 1, lead))
            starts[c] = pl.multiple_of(a * NA_WIN, NA_WIN)
            for g in range(NA_GROUPS):
                pieces[c, g] = [slice((c * NA_QROWS + rr) * GRID_W + g * NA_KW,
                                      (c * NA_QROWS + rr) * GRID_W + (g + 1) * NA_KW) for rr in range(NA_QROWS)]
        chains = [(c, g, hh) for c in blocks for g in range(NA_GROUPS) for hh in range(2)]
        scores, probs, outs = {}, {}, {}
        for c, g, hh in chains:
            q = jnp.concatenate([q_ref[0, 0, rws, :] for rws in pieces[c, g]], axis=0)
            qm = jnp.where(upper if hh else ~upper, q, jnp.zeros_like(q))
            ks = k_ref[0, 0, g, pl.ds(starts[c], NA_K), :]
            s = lax.dot_general(qm, ks, (((1,), (1,)), ((), ())), preferred_element_type=F32)
            scores[c, g, hh] = s + bias_ref[hh, table_of_group[g] * n_types + types[c]]
        for chain in chains:
            s = scores[chain]
            probs[chain] = jnp.exp2(s - jnp.max(s, axis=-1, keepdims=True)).astype(BF16)
        for c, g, hh in chains:
            vs = v_ref[0, 0, g, pl.ds(starts[c], NA_K), :]
            v1 = jnp.where(key_upper if hh else ~key_upper, vs, jnp.ones_like(vs))
            o = jnp.dot(probs[c, g, hh], v1, preferred_element_type=F32)
            outs[c, g, hh] = o / o[:, (1 - hh) * HEAD_DIM:(1 - hh) * HEAD_DIM + 1]
        for c in blocks:
            for g in range(NA_GROUPS):
                out = jnp.where(upper, outs[c, g, 1], outs[c, g, 0]).astype(BF16)
                for rr, rws in enumerate(pieces[c, g]):
                    o_ref[0, 0, rws, :] = out[rr * NA_KW:(rr + 1) * NA_KW]


def _na_bias_kernel(rpb_ref, o_ref, *, rows):
    h = pl.program_id(0)
    cc = lax.broadcasted_iota(jnp.int32, (NA_KW, LANES), 0)
    lane = lax.broadcasted_iota(jnp.int32, (NA_KW, LANES), 1)
    w = lane % NA_WIN
    key_row_in_tile = lane // NA_WIN
    rows_per_tile = LANES // NA_WIN
    neg = jnp.full((NA_KW, LANES), NEG_INF, F32)
    base = h * NA_RPB_ROWS * NA_RPB_COLS
    lead, trail = _na_block_types(rows)
    n_blocks = rows // NA_QROWS
    type_blocks = list(range(lead + 1)) + list(range(n_blocks - trail, n_blocks))
    for table, g in enumerate(_na_group_tables()[1]):
        qc = g * NA_KW + cc
        kc = NA_WIN_START[g] + w
        c0 = jnp.clip(qc - NA_KW // 2, 0, GRID_W - NA_KW)
        in_cols = (kc >= c0) & (kc < c0 + NA_KW)
        dc = kc - qc + NA_KW - 1
        by_row_offset = []
        for d in range(NA_RPB_ROWS):
            acc = neg
            for dd in range(NA_RPB_COLS):
                acc = jnp.where(dc == dd, rpb_ref[base + d * NA_RPB_COLS + dd], acc)
            by_row_offset.append(jnp.where(in_cols, acc * LOG2E, NEG_INF))
        for ty, block in enumerate(type_blocks):
            r = block * NA_QROWS
            a = _na_first_key_row(block, rows, _clamp)
            for j in range(NA_QROWS):
                r0 = _clamp(r + j - NA_KH // 2, 0, rows - NA_KH)
                for t in range(NA_K // LANES):
                    tile = neg
                    for part in range(rows_per_tile):
                        i = t * rows_per_tile + part
                        if r0 <= a + i < r0 + NA_KH:
                            tile = jnp.where(key_row_in_tile == part, by_row_offset[a + i - (r + j) + NA_KH - 1], tile)
                    o_ref[0, table * len(type_blocks) + ty, j * NA_KW:(j + 1) * NA_KW, t * LANES:(t + 1) * LANES] = tile


def _na_bias(rpb, rows):
    n_tables = len(_na_group_tables()[1]) * (sum(_na_block_types(rows)) + 1)
    return pl.pallas_call(
        functools.partial(_na_bias_kernel, rows=rows),
        out_shape=jax.ShapeDtypeStruct((NA_HEADS, n_tables, NA_Q, NA_K), F32),
        grid=(NA_HEADS,),
        in_specs=[pl.BlockSpec(memory_space=pltpu.SMEM)],
        out_specs=pl.BlockSpec((1, n_tables, NA_Q, NA_K), lambda h: (h, 0, 0, 0)),
        compiler_params=_params("arbitrary"),
        name="na_bias",
    )(rpb.astype(F32).reshape(-1))


def _na(qkv, kg, vg, rpb):
    b, _, s, _ = qkv.shape
    rows = s // GRID_W
    assert rows % (NA_QROWS * NA_BLOCKS_PER_STEP) == 0
    tq = NA_BLOCKS_PER_STEP * NA_QROWS * GRID_W
    bias = _na_bias(rpb, rows)
    windows_spec = pl.BlockSpec((1, 1) + kg.shape[2:], lambda bi, p, i: (bi, p, 0, 0, 0))
    return pl.pallas_call(
        functools.partial(_na_kernel, rows=rows),
        out_shape=jax.ShapeDtypeStruct((b, PAIRS, s, LANES), BF16),
        grid=(b, PAIRS, s // tq),
        in_specs=[pl.BlockSpec((1, 1, tq, LANES), lambda bi, p, i: (bi, COL_NQ + p, i, 0)),
                  windows_spec,
                  windows_spec,
                  pl.BlockSpec((2,) + bias.shape[1:], lambda bi, p, i: (p, 0, 0, 0))],
        out_specs=pl.BlockSpec((1, 1, tq, LANES), lambda bi, p, i: (bi, p, i, 0)),
        compiler_params=_params("arbitrary", "arbitrary", "arbitrary"),
        name="na",
    )(qkv, kg, vg, bias)


SWA_KEYS = 3 * SWA_BLOCK
SWA_BLOCKS_PER_STEP = 32
SWA_INTERLEAVE = 2
SWA_GROUP = SWA_HEADS // SWA_KV_HEADS
assert COL_SQ % (PAIRS // SWA_KV_HEADS) == 0


def _swa_masks():
    v = np.arange(SWA_KEYS // SWA_BLOCK)[:, None, None]
    q = np.arange(SWA_BLOCK)[None, :, None]
    k = np.arange(SWA_KEYS)[None, None, :]
    return np.where(np.abs(k - v * SWA_BLOCK - q) <= SWA_WINDOW, 0.0, NEG_INF).astype(np.float32)


def _swa_kernel(sink_ref, mask_ref, q_ref, k_ref, v_ref, o_ref, *, seq):
    kv = pl.program_id(1)
    rows = SWA_GROUP * SWA_BLOCK
    lane = lax.broadcasted_iota(jnp.int32, (SWA_BLOCK, LANES), 1)
    upper = lane >= HEAD_DIM
    head = lax.broadcasted_iota(jnp.int32, (rows, 1), 0) // SWA_BLOCK
    sink = jnp.zeros((rows, 1), F32)
    for g in range(SWA_GROUP):
        sink = jnp.where(head == g, sink_ref[kv * SWA_GROUP + g], sink)
    sink = sink * LOG2E
    for j0 in range(0, SWA_BLOCKS_PER_STEP, SWA_INTERLEAVE):
        blocks = range(j0, j0 + SWA_INTERLEAVE)
        starts, scores, probs = {}, {}, {}
        for j in blocks:
            n = pl.program_id(2) * SWA_BLOCKS_PER_STEP + j
            start = pl.multiple_of(jnp.clip((n - 1) * SWA_BLOCK, 0, seq - SWA_KEYS), SWA_BLOCK)
            starts[j] = start
            ks = k_ref[0, 0, pl.ds(start, SWA_KEYS), :]
            qs = []
            for g in range(SWA_GROUP):
                q = q_ref[0, g // 2, j * SWA_BLOCK:(j + 1) * SWA_BLOCK, :]
                qs.append(jnp.where(upper if g % 2 else ~upper, q, jnp.zeros_like(q)))
            s = lax.dot_general(jnp.concatenate(qs, axis=0), ks, (((1,), (1,)), ((), ())), preferred_element_type=F32)
            mask = mask_ref[(n * SWA_BLOCK - start) // SWA_BLOCK]
            scores[j] = s + jnp.concatenate([mask] * SWA_GROUP, axis=0)
        for j in blocks:
            s = scores[j]
            m = jnp.maximum(jnp.max(s, axis=-1, keepdims=True), sink)
            e = jnp.exp2(s - m)
            probs[j] = (e.astype(BF16), jnp.sum(e, axis=-1, keepdims=True) + jnp.exp2(sink - m))
        for j in blocks:
            e, l = probs[j]
            vs = v_ref[0, 0, pl.ds(starts[j], SWA_KEYS), :]
            o = jnp.dot(e, vs, preferred_element_type=F32) / l
            for pair in range(SWA_GROUP // 2):
                even = o[(2 * pair) * SWA_BLOCK:(2 * pair + 1) * SWA_BLOCK]
                odd = o[(2 * pair + 1) * SWA_BLOCK:(2 * pair + 2) * SWA_BLOCK]
                o_ref[0, pair, j * SWA_BLOCK:(j + 1) * SWA_BLOCK, :] = jnp.where(upper, odd, even).astype(BF16)


def _swa(qkv, sinks):
    b, _, s, _ = qkv.shape
    pairs_per_kv = PAIRS // SWA_KV_HEADS
    tq = SWA_BLOCKS_PER_STEP * SWA_BLOCK
    masks = _swa_masks()
    return pl.pallas_call(
        functools.partial(_swa_kernel, seq=s),
        out_shape=jax.ShapeDtypeStruct((b, PAIRS, s, LANES), BF16),
        grid=(b, SWA_KV_HEADS, s // tq),
        in_specs=[pl.BlockSpec(memory_space=pltpu.SMEM),
                  pl.BlockSpec(masks.shape, lambda bi, kv, n: (0, 0, 0)),
                  pl.BlockSpec((1, pairs_per_kv, tq, LANES),
                               lambda bi, kv, n: (bi, COL_SQ // pairs_per_kv + kv, n, 0)),
                  pl.BlockSpec((1, 1, s, LANES), lambda bi, kv, n: (bi, COL_SK + kv, 0, 0)),
                  pl.BlockSpec((1, 1, s, LANES), lambda bi, kv, n: (bi, COL_SV + kv, 0, 0))],
        out_specs=pl.BlockSpec((1, pairs_per_kv, tq, LANES), lambda bi, kv, n: (bi, kv, n, 0)),
        compiler_params=_params("arbitrary", "arbitrary", "arbitrary"),
        name="swa",
    )(sinks, jnp.asarray(masks), qkv, qkv, qkv)


ROUTER_COLS = N_GROUPS + N_EXPERTS


def _split2(v):
    hi = v.astype(BF16)
    lo = (v - hi.astype(F32)).astype(BF16)
    return hi, lo


ROUTER_ROWS = 40


def _route(logits_t):
    sub = lax.broadcasted_iota(jnp.int32, logits_t.shape, 0)
    big = jnp.int32(LANES)
    gmask = sub < N_GROUPS
    gl = jnp.where(gmask, logits_t, NEG_INF)
    gmax = jnp.max(gl, axis=0, keepdims=True)
    g_top = jnp.min(jnp.where(gmask & (gl == gmax), sub, big), axis=0, keepdims=True)
    g_weight = 1.0 / jnp.sum(jnp.where(gmask, jnp.exp(gl - gmax), 0.0), axis=0, keepdims=True)
    lo = ROUTER_BASE + g_top * EXPERTS_PER_GROUP
    emask = (sub >= lo) & (sub < lo + EXPERTS_PER_GROUP)
    el = jnp.where(emask, logits_t, NEG_INF)
    m1 = jnp.max(el, axis=0, keepdims=True)
    i1 = jnp.min(jnp.where(emask & (el == m1), sub, big), axis=0, keepdims=True)
    emask2 = emask & (sub != i1)
    el2 = jnp.where(emask2, logits_t, NEG_INF)
    m2 = jnp.max(el2, axis=0, keepdims=True)
    i2 = jnp.min(jnp.where(emask2 & (el2 == m2), sub, big), axis=0, keepdims=True)
    e2 = jnp.exp(m2 - m1)
    w1 = g_weight / (1.0 + e2)
    w2 = g_weight * e2 / (1.0 + e2)
    return i1 - ROUTER_BASE, i2 - ROUTER_BASE, w1, w2


TOKEN_ROWS = D_MODEL // (2 * LANES)
WORD = jnp.uint32


def _to_token_tiles(ref, v, base=0):
    t, d = v.shape
    words = pltpu.pack_elementwise([v[:, :d // 2], v[:, d // 2:]], packed_dtype=BF16)
    for s in range(TOKEN_ROWS):
        ref[pl.ds(base + s, t, stride=TOKEN_ROWS), :] = words[:, s * LANES:(s + 1) * LANES]


def _token_words(ref, t, base=0):
    return jnp.concatenate([ref[pl.ds(base + s, t, stride=TOKEN_ROWS), :] for s in range(TOKEN_ROWS)], axis=-1)


def _from_token_tiles(ref, t, base=0):
    words = _token_words(ref, t, base)
    halves = [pltpu.unpack_elementwise(words, index=j, packed_dtype=BF16, unpacked_dtype=F32) for j in range(2)]
    return jnp.concatenate(halves, axis=-1)


MIX_CHUNK = 256


def _mix_kernel(na_ref, sw_ref, x_ref, bna_ref, bsw_ref, wo_ref, gpm_ref, ga_ref, gpf_ref, scf_ref, shf_ref,
                wr_ref, x1_ref, h2_ref, r_ref, rt_ref, cntc_ref, cntr_ref):
    first_step = (pl.program_id(0) == 0) & (pl.program_id(1) == 0)

    @pl.when(first_step)
    def _():
        cntc_ref[...] = jnp.zeros_like(cntc_ref)
        cntr_ref[...] = jnp.zeros_like(cntr_ref)

    t = MIX_CHUNK
    chunks = range(x_ref.shape[1] // t)
    row_slices = [slice(c * t, (c + 1) * t) for c in chunks]
    mixes, h2s, all_logits = [], [], []
    for rows in row_slices:
        def heads(ref):
            return jnp.concatenate([ref[0, j, rows, :] for j in range(PAIRS)], axis=-1).astype(F32)

        na = (_rms(heads(na_ref)) * bna_ref[...]).astype(BF16)
        sw = (_rms(heads(sw_ref)) * bsw_ref[...]).astype(BF16)
        mixes.append(jnp.dot(na, wo_ref[:NA_WIDTH, :], preferred_element_type=F32)
                     + jnp.dot(sw, wo_ref[NA_WIDTH:, :], preferred_element_type=F32))
    gate_gain = ga_ref[0] * gpm_ref[...]
    ffn_gain = gpf_ref[...] * (1.0 + scf_ref[0])
    for c, rows in zip(chunks, row_slices):
        x1 = x_ref[0, rows, :] + _rms(mixes[c]) * gate_gain
        x1_ref[0, rows, :] = x1
        h2 = _rms(x1) * ffn_gain + shf_ref[0]
        _to_token_tiles(h2_ref, h2, base=c * t * TOKEN_ROWS)
        h2s.append(h2)
    for c in chunks:
        h_hi, h_lo = _split2(h2s[c])
        both = (jnp.dot(h_hi, wr_ref[...], preferred_element_type=F32)
                + jnp.dot(h_lo, wr_ref[...], preferred_element_type=F32))
        all_logits.append(both + pltpu.roll(both, LANES - ROUTER_COLS, 1))
    for c, rows in zip(chunks, row_slices):
        e1, e2, w1, w2 = _route(jnp.transpose(all_logits[c])[:ROUTER_ROWS])
        fields = [None] * 4
        fields[R_E1], fields[R_E2], fields[R_W1], fields[R_W2] = e1.astype(F32), e2.astype(F32), w1, w2
        rt = jnp.concatenate(fields + [jnp.zeros((SUBLANES - 4, t), F32)], axis=0)
        rt_ref[:, rows] = rt
        r = jnp.transpose(jnp.concatenate([rt, jnp.zeros((LANES - SUBLANES, t), F32)], axis=0))
        r_ref[rows, :] = r
        sub = lax.broadcasted_iota(jnp.int32, (N_EXPERTS, t), 0)
        on_sub = ((sub == e1) | (sub == e2)).astype(F32)
        cntc_ref[...] += jnp.broadcast_to(jnp.sum(on_sub, axis=1, keepdims=True), cntc_ref.shape)
        lane = lax.broadcasted_iota(jnp.int32, r.shape, 1).astype(F32)
        on_lane = ((lane == r[:, R_E1:R_E1 + 1]) | (lane == r[:, R_E2:R_E2 + 1])).astype(F32)
        cntr_ref[...] += jnp.broadcast_to(jnp.sum(on_lane, axis=0, keepdims=True), cntr_ref.shape)


def _mix(na, sw, x, beta_na, beta_swa, w_out, g_post_mix, gate_a, g_pre_ffn, scale_f, shift_f, w_router3):
    b, s, d = x.shape
    tm = 1024
    nt = s // tm
    row = lambda bi, i: (bi, i, 0)
    const2 = lambda bi, i: (0, 0)
    per_b = lambda bi, i: (bi, 0, 0)
    return pl.pallas_call(
        _mix_kernel,
        out_shape=(jax.ShapeDtypeStruct((b, s, d), F32),
                   jax.ShapeDtypeStruct((b * s * TOKEN_ROWS, LANES), WORD),
                   jax.ShapeDtypeStruct((b * s, LANES), F32),
                   jax.ShapeDtypeStruct((SUBLANES, b * s), F32),
                   jax.ShapeDtypeStruct((N_EXPERTS, LANES), F32),
                   jax.ShapeDtypeStruct((SUBLANES, LANES), F32)),
        grid=(b, s // tm),
        in_specs=[pl.BlockSpec((1, PAIRS, tm, LANES), lambda bi, i: (bi, 0, i, 0)),
                  pl.BlockSpec((1, PAIRS, tm, LANES), lambda bi, i: (bi, 0, i, 0)),
                  pl.BlockSpec((1, tm, d), row),
                  pl.BlockSpec((1, NA_WIDTH), const2),
                  pl.BlockSpec((1, SWA_WIDTH), const2),
                  pl.BlockSpec((NA_WIDTH + SWA_WIDTH, d), const2),
                  pl.BlockSpec((1, d), const2),
                  pl.BlockSpec((1, 1, d), per_b),
                  pl.BlockSpec((1, d), const2),
                  pl.BlockSpec((1, 1, d), per_b),
                  pl.BlockSpec((1, 1, d), per_b),
                  pl.BlockSpec((d, LANES), const2)],
        out_specs=(pl.BlockSpec((1, tm, d), row),
                   pl.BlockSpec((tm * TOKEN_ROWS, LANES), lambda bi, i: (bi * nt + i, 0)),
                   pl.BlockSpec((tm, LANES), lambda bi, i: (bi * nt + i, 0)),
                   pl.BlockSpec((SUBLANES, tm), lambda bi, i: (0, bi * nt + i)),
                   pl.BlockSpec((N_EXPERTS, LANES), const2),
                   pl.BlockSpec((SUBLANES, LANES), const2)),
        compiler_params=_params("arbitrary", "arbitrary"),
        name="mix",
    )(na, sw, x, beta_na.reshape(1, -1), beta_swa.reshape(1, -1), w_out, g_post_mix.reshape(1, d), gate_a,
      g_pre_ffn.reshape(1, d), scale_f, shift_f, w_router3)


MOE_TILE = 1024
MOE_SUBTILE = 256
PLAN_T = 2048
I_TILE, I_EXPERT, I_LO, I_HI, I_FIRST, I_NEW, I_NEXT, I_ORDER = range(8)


def _plan_kernel(rt_ref, cntc_ref, cntr_ref, pos_ref, items_ref, start_ref, carry_ref, *, n_tiles, n_items):
    i = pl.program_id(0)
    sub = lax.broadcasted_iota(jnp.int32, (N_EXPERTS, LANES), 0)
    lane = lax.broadcasted_iota(jnp.int32, (N_EXPERTS, LANES), 1)

    @pl.when(i == 0)
    def _():
        c_col = cntc_ref[:, 0:1]
        c_row = cntr_ref[0:1, :]
        s_col = jnp.sum(jnp.where(lane < sub, c_row, 0.0), axis=1, keepdims=True)
        s_row = jnp.sum(jnp.where(sub < lane, c_col, 0.0), axis=0, keepdims=True)
        start_ref[...] = jnp.broadcast_to(s_col, start_ref.shape)
        carry_ref[...] = jnp.zeros_like(carry_ref)

        def tiles_of(s, c):
            first = jnp.floor(s * (1.0 / MOE_TILE))
            last = jnp.floor((s + c - 1.0) * (1.0 / MOE_TILE))
            return first, jnp.where(c > 0.0, last - first + 1.0, 0.0)

        f_col, n_col = tiles_of(s_col, c_col)
        _, n_row = tiles_of(s_row, c_row)
        i_col = jnp.sum(jnp.where(lane < sub, n_row, 0.0), axis=1, keepdims=True)
        total = jnp.sum(n_col, axis=0, keepdims=True)
        k = lax.broadcasted_iota(jnp.int32, (N_EXPERTS, n_items), 1).astype(F32)
        subk = lax.broadcasted_iota(jnp.int32, (N_EXPERTS, n_items), 0).astype(F32)
        ek = jnp.sum(jnp.where(i_col + n_col <= k, 1.0, 0.0), axis=0, keepdims=True)
        k0 = k[0:1]
        valid = k0 < total
        sel = subk == ek

        def pick(v):
            return jnp.sum(jnp.where(sel, v, 0.0), axis=0, keepdims=True)

        i_k, f_k, s_k, c_k = pick(i_col), pick(f_col), pick(s_col), pick(c_col)
        tile = f_k + (k0 - i_k)
        row0 = tile * MOE_TILE
        lo = jnp.maximum(s_k, row0) - row0
        hi = jnp.minimum(s_k + c_k, row0 + MOE_TILE) - row0
        present = n_col > 0.0
        last_expert = jnp.sum(jnp.where(i_col + n_col <= total - 1.0, 1.0, 0.0), axis=0, keepdims=True)
        nxt = jnp.min(jnp.where(present & (subk > ek), subk, float(N_EXPERTS)), axis=0, keepdims=True)
        order = jnp.sum(jnp.where(present & (subk < ek), 1.0, 0.0), axis=0, keepdims=True)
        rows = [jnp.where(valid, tile, n_tiles - 1.0), jnp.where(valid, ek, last_expert),
                jnp.where(valid, lo, 0.0), jnp.where(valid, hi, 0.0),
                jnp.where(valid & (lo == 0.0), 1.0, 0.0),
                jnp.where(valid & (k0 == i_k), 1.0, 0.0), jnp.where(valid, nxt, float(N_EXPERTS)),
                jnp.where(valid, order, 0.0)]
        assert len(rows) == SUBLANES
        items_ref[...] = jnp.concatenate(rows, axis=0).astype(jnp.int32)

    t = rt_ref.shape[1]
    e1 = rt_ref[R_E1:R_E1 + 1, :]
    e2 = rt_ref[R_E2:R_E2 + 1, :]
    sub_t = lax.broadcasted_iota(jnp.int32, (N_EXPERTS, t), 0).astype(F32)
    oh1 = sub_t == e1
    oh2 = sub_t == e2
    oh = (oh1 | oh2).astype(F32)
    before = (lax.broadcasted_iota(jnp.int32, (t, t), 0) < lax.broadcasted_iota(jnp.int32, (t, t), 1)).astype(BF16)
    rank = jnp.dot(oh.astype(BF16), before, preferred_element_type=F32)
    base = start_ref[:, 0:1] + carry_ref[:, 0:1] + rank
    pos1 = jnp.sum(jnp.where(oh1, base, 0.0), axis=0, keepdims=True)
    pos2 = jnp.sum(jnp.where(oh2, base, 0.0), axis=0, keepdims=True)
    carry_ref[...] += jnp.broadcast_to(jnp.sum(oh, axis=1, keepdims=True), carry_ref.shape)
    pos = jnp.concatenate([pos1, pos2] + [jnp.zeros_like(pos1)] * (SUBLANES - 2), axis=0)
    pos_ref[...] = pos.astype(jnp.int32)


def _plan(rt, cntc, cntr):
    n = rt.shape[1]
    n_tiles = 2 * n // MOE_TILE
    n_items = 2 * LANES
    assert n_tiles + N_EXPERTS <= n_items
    return pl.pallas_call(
        functools.partial(_plan_kernel, n_tiles=n_tiles, n_items=n_items),
        out_shape=(jax.ShapeDtypeStruct((SUBLANES, n), jnp.int32),
                   jax.ShapeDtypeStruct((SUBLANES, n_items), jnp.int32)),
        grid=(n // PLAN_T,),
        in_specs=[pl.BlockSpec((SUBLANES, PLAN_T), lambda i: (0, i)),
                  pl.BlockSpec((N_EXPERTS, LANES), lambda i: (0, 0)),
                  pl.BlockSpec((SUBLANES, LANES), lambda i: (0, 0))],
        out_specs=(pl.BlockSpec((SUBLANES, PLAN_T), lambda i: (0, i)),
                   pl.BlockSpec((SUBLANES, n_items), lambda i: (0, 0))),
        scratch_shapes=[pltpu.VMEM((N_EXPERTS, LANES), F32), pltpu.VMEM((N_EXPERTS, LANES), F32)],
        compiler_params=_params("arbitrary"),
        name="plan",
    )(rt, cntc, cntr)


DISPATCH_T = 2048


def _token_rows(ref, index):
    return ref.at[pl.ds(pl.multiple_of(index * TOKEN_ROWS, TOKEN_ROWS), TOKEN_ROWS), :]


def _dispatch_kernel(pos_ref, h_ref, xs_hbm, sem):
    def body(r, carry):
        src = _token_rows(h_ref, r)
        for k in range(2):
            pltpu.make_async_copy(src, _token_rows(xs_hbm, pos_ref[k, r]), sem.at[0]).start(priority=k)
        return carry

    lax.fori_loop(0, DISPATCH_T, body, 0, unroll=8)
    for k in range(2):
        pltpu.make_async_copy(h_ref, xs_hbm.at[pl.ds(0, DISPATCH_T * TOKEN_ROWS), :], sem.at[0]).wait()


def _dispatch(pos, h2):
    n = pos.shape[1]
    return pl.pallas_call(
        _dispatch_kernel,
        out_shape=jax.ShapeDtypeStruct((2 * n * TOKEN_ROWS, LANES), WORD),
        grid=(n // DISPATCH_T,),
        in_specs=[pl.BlockSpec((SUBLANES, DISPATCH_T), lambda i: (0, i), memory_space=pltpu.SMEM),
                  pl.BlockSpec((DISPATCH_T * TOKEN_ROWS, LANES), lambda i: (i, 0))],
        out_specs=pl.BlockSpec(memory_space=pl.ANY),
        scratch_shapes=[pltpu.SemaphoreType.DMA((1,))],
        compiler_params=_params("arbitrary"),
        name="dispatch",
    )(pos, h2)


def _expert_kernel(tile_ref, exp_ref, lo_ref, hi_ref, first_ref, new_ref, next_ref, order_ref,
                   xs_ref, wg_hbm, wu_hbm, wd_hbm, o_ref, wg_buf, wu_buf, wd_buf, sem):
    k = pl.program_id(0)
    lo = lo_ref[k]
    hi = hi_ref[k]
    slot = order_ref[k] % 2

    def weight_copies(expert, sl):
        return [pltpu.make_async_copy(src.at[expert], dst.at[sl], sem.at[sl])
                for src, dst in ((wg_hbm, wg_buf), (wu_hbm, wu_buf), (wd_hbm, wd_buf))]

    @pl.when(k == 0)
    def _():
        for copy in weight_copies(exp_ref[0], 0):
            copy.start()

    @pl.when(new_ref[k] == 1)
    def _():
        for copy in weight_copies(exp_ref[k], slot):
            copy.wait()

        @pl.when(next_ref[k] < N_EXPERTS)
        def _():
            for copy in weight_copies(next_ref[k], 1 - slot):
                copy.start()

    @pl.when(first_ref[k] == 1)
    def _():
        o_ref[...] = jnp.zeros_like(o_ref)

    for part in range(MOE_TILE // MOE_SUBTILE):
        first_row = part * MOE_SUBTILE
        base = first_row * TOKEN_ROWS

        @pl.when((hi > first_row) & (lo < first_row + MOE_SUBTILE))
        def _():
            x = _from_token_tiles(xs_ref, MOE_SUBTILE, base=base).astype(BF16)
            gate = jnp.dot(x, wg_buf[slot].astype(BF16), preferred_element_type=F32)
            up = jnp.dot(x, wu_buf[slot].astype(BF16), preferred_element_type=F32)
            he = (gate * jax.nn.sigmoid(gate) * up).astype(BF16)
            ye = jnp.dot(he, wd_buf[slot].astype(BF16), preferred_element_type=F32)
            row = first_row + lax.broadcasted_iota(jnp.int32, (MOE_SUBTILE, 1), 0)
            mine = (row >= lo) & (row < hi)
            _to_token_tiles(o_ref, jnp.where(mine, ye, _from_token_tiles(o_ref, MOE_SUBTILE, base=base)), base=base)


def _experts(items, xs, w_gate, w_up, w_down):
    n_rows = xs.shape[0] // TOKEN_ROWS
    n_items = n_rows // MOE_TILE + N_EXPERTS
    d = w_gate.shape[1]
    tile_map = lambda k, tile, *_: (tile[k], 0)
    grid_spec = pltpu.PrefetchScalarGridSpec(
        num_scalar_prefetch=SUBLANES,
        grid=(n_items,),
        in_specs=[pl.BlockSpec((MOE_TILE * TOKEN_ROWS, LANES), tile_map),
                  pl.BlockSpec(memory_space=pl.ANY),
                  pl.BlockSpec(memory_space=pl.ANY),
                  pl.BlockSpec(memory_space=pl.ANY)],
        out_specs=pl.BlockSpec((MOE_TILE * TOKEN_ROWS, LANES), tile_map),
        scratch_shapes=[pltpu.VMEM((2, d, EXPERT_FF), F32), pltpu.VMEM((2, d, EXPERT_FF), F32),
                        pltpu.VMEM((2, EXPERT_FF, d), F32), pltpu.SemaphoreType.DMA((2,))])
    return pl.pallas_call(
        _expert_kernel,
        out_shape=jax.ShapeDtypeStruct(xs.shape, WORD),
        grid_spec=grid_spec,
        compiler_params=_params("arbitrary"),
        name="experts",
    )(*(items[j, :n_items] for j in range(SUBLANES)), xs, w_gate, w_up, w_down)


COMBINE_T = 512


def _combine_kernel(pos_ref, posn_ref, ys_hbm, r_ref, x1_ref, gf_ref, gpost_ref, o_ref, buf, sem):
    i = pl.program_id(0)
    n = pl.num_programs(0)
    slot = i % 2
    slot_rows = COMBINE_T * TOKEN_ROWS

    def start_row(p_ref, sl, r):
        for k in range(2):
            dst = buf.at[sl, pl.ds(pl.multiple_of(k * slot_rows + r * TOKEN_ROWS, TOKEN_ROWS), TOKEN_ROWS), :]
            pltpu.make_async_copy(_token_rows(ys_hbm, p_ref[k, r]), dst, sem.at[sl]).start(priority=k)

    def wait_slot(sl):
        pltpu.make_async_copy(ys_hbm.at[pl.ds(0, 2 * slot_rows), :], buf.at[sl], sem.at[sl]).wait()

    def issue(p_ref, sl):
        def body(r, carry):
            start_row(p_ref, sl, r)
            return carry

        lax.fori_loop(0, COMBINE_T, body, 0, unroll=8)

    @pl.when(i == 0)
    def _():
        issue(pos_ref, 0)

    @pl.when(i + 1 < n)
    def _():
        issue(posn_ref, 1 - slot)

    wait_slot(slot)
    ya = _from_token_tiles(buf.at[slot], COMBINE_T)
    yb = _from_token_tiles(buf.at[slot], COMBINE_T, base=slot_rows)
    r = r_ref[...]
    y = r[:, R_W1:R_W1 + 1] * ya + r[:, R_W2:R_W2 + 1] * yb
    o_ref[0] = x1_ref[0] + gf_ref[0] * (_rms(y) * gpost_ref[...])


def _combine(pos, ys, r, x1, gate_f, g_post_ffn):
    b, s, d = x1.shape
    nt = s // COMBINE_T
    n_steps = b * nt
    row = lambda i: (i // nt, i % nt, 0)
    pos_spec = lambda f: pl.BlockSpec((SUBLANES, COMBINE_T), lambda i: (0, f(i)), memory_space=pltpu.SMEM)
    return pl.pallas_call(
        _combine_kernel,
        out_shape=jax.ShapeDtypeStruct((b, s, d), F32),
        grid=(n_steps,),
        in_specs=[pos_spec(lambda i: i),
                  pos_spec(lambda i: jnp.minimum(i + 1, n_steps - 1)),
                  pl.BlockSpec(memory_space=pl.ANY),
                  pl.BlockSpec((COMBINE_T, LANES), lambda i: (i, 0)),
                  pl.BlockSpec((1, COMBINE_T, d), row),
                  pl.BlockSpec((1, 1, d), lambda i: (i // nt, 0, 0)),
                  pl.BlockSpec((1, d), lambda i: (0, 0))],
        out_specs=pl.BlockSpec((1, COMBINE_T, d), row),
        scratch_shapes=[pltpu.VMEM((2, 2 * COMBINE_T * TOKEN_ROWS, LANES), WORD), pltpu.SemaphoreType.DMA((2,))],
        compiler_params=_params("arbitrary"),
        name="combine",
    )(pos, pos, ys, r, x1, gate_f, g_post_ffn.reshape(1, d))


def _rope_tables(s):
    half = HEAD_DIM // 2
    inv = ROPE_THETA ** (-np.arange(half, dtype=np.float64) * 2.0 / HEAD_DIM)
    ang = np.arange(s, dtype=np.float64)[:, None] * inv[None, :]
    cos = np.cos(ang)
    sin = np.sin(ang)
    cos = np.concatenate([cos, cos, cos, cos], axis=-1)
    sin_signed = np.concatenate([-sin, sin, -sin, sin], axis=-1)
    return jnp.asarray(cos, F32), jnp.asarray(sin_signed, F32)


def _router_weights(w_group_router, w_expert_router):
    d = w_group_router.shape[0]
    we = jnp.transpose(w_expert_router, (1, 0, 2)).reshape(d, N_EXPERTS)
    hi, lo = _split2(jnp.concatenate([w_group_router, we], axis=-1))
    return jnp.concatenate([hi, lo, jnp.zeros((d, LANES - 2 * ROUTER_COLS), BF16)], axis=-1)


def kernel(x, c, w_ada, b_ada, g_pre_mix, w_in, na_rpb, swa_sinks, beta_na, beta_swa, w_out, g_post_mix, g_pre_ffn,
           w_group_router, w_expert_router, w_gate, w_up, w_down, g_post_ffn):
    b, s, d = x.shape
    depth = w_ada.shape[0]
    cos, sin_signed = _rope_tables(s)
    for l in range(depth):
        mod = _adaln(c, w_ada[l], b_ada[l]).reshape(b, N_MOD, 1, d)
        shift_a, scale_a, gate_a, shift_f, scale_f, gate_f = (mod[:, k] for k in range(N_MOD))
        qkv, na_kg, na_vg = _qkv(x, g_pre_mix[l], scale_a, shift_a, w_in[l].astype(BF16), cos, sin_signed)
        na = _na(qkv, na_kg, na_vg, na_rpb[l])
        sw = _swa(qkv, swa_sinks[l])
        x1, h2, r, rt, cntc, cntr = _mix(na, sw, x, beta_na[l], beta_swa[l], w_out[l].astype(BF16), g_post_mix[l],
                                          gate_a, g_pre_ffn[l], scale_f, shift_f,
                                          _router_weights(w_group_router[l], w_expert_router[l]))
        pos, items = _plan(rt, cntc, cntr)
        xs = _dispatch(pos, h2)
        ys = _experts(items, xs, w_gate[l], w_up[l], w_down[l])
        x = _combine(pos, ys, r, x1, gate_f, g_post_ffn[l])
    return x
```

```python
import functools

import jax
import jax.numpy as jnp
import numpy as np
from jax import lax
from jax.experimental import pallas as pl
from jax.experimental.pallas import tpu as pltpu

D_MODEL = 1024
GRID_W = 64
HEAD_DIM = 64
NA_HEADS = 8
NA_KH = 8
NA_KW = 16
SWA_HEADS = 8
SWA_KV_HEADS = 2
SWA_WINDOW = 128
SWA_BLOCK = 128
ROPE_THETA = 10000.0
NA_WIDTH = NA_HEADS * HEAD_DIM
SWA_WIDTH = SWA_HEADS * HEAD_DIM
N_GROUPS = 4
EXPERTS_PER_GROUP = 8
N_EXPERTS = N_GROUPS * EXPERTS_PER_GROUP
EXPERT_FF = 256
N_MOD = 6
EPS = 1e-6
NEG_INF = -1e30

LANES = 128
SUBLANES = 8
R_E1, R_E2, R_W1, R_W2 = range(4)
PAIRS = NA_HEADS // 2
W_NQ, W_NK, W_NV, W_SQ, W_SKV = 0, 4, 8, 12, 16
COL_NQ, COL_SQ, COL_SK, COL_SV = 0, 4, 8, 10
QKV_TILES = 12
NA_GROUPS = GRID_W // NA_KW
NA_WIN = 2 * NA_KW
NA_WIN_START = tuple(min(max(NA_KW * g - NA_KW // 2, 0), GRID_W - NA_WIN) for g in range(NA_GROUPS))
ROUTER_BASE = N_GROUPS
VMEM_LIMIT = 56 * 1024 * 1024

F32 = jnp.float32
BF16 = jnp.bfloat16
LOG2E = 1.4426950408889634


def _rms(v):
    return v * lax.rsqrt(jnp.mean(v * v, axis=-1, keepdims=True) + EPS)


def _params(*sem):
    return pltpu.CompilerParams(dimension_semantics=sem, vmem_limit_bytes=VMEM_LIMIT)


def _adaln_kernel(c_ref, w_ref, b_ref, o_ref, *, batch):
    c = c_ref[...]
    a_t = jnp.transpose(c * jax.nn.sigmoid(c))
    w = w_ref[...]
    rows = [jnp.sum(a_t[:, bi:bi + 1] * w, axis=0, keepdims=True) for bi in range(batch)]
    rows.append(jnp.zeros((c.shape[0] - batch, w.shape[1]), F32))
    o_ref[...] = jnp.concatenate(rows, axis=0) + b_ref[...]


def _adaln(c, w_ada, b_ada):
    batch, d = c.shape
    n = w_ada.shape[1]
    tn = 1024
    b = SUBLANES
    assert batch < b
    c = jnp.pad(c, ((0, b - batch), (0, 0)))
    return pl.pallas_call(
        functools.partial(_adaln_kernel, batch=batch),
        out_shape=jax.ShapeDtypeStruct((b, n), F32),
        grid=(n // tn,),
        in_specs=[pl.BlockSpec((b, d), lambda j: (0, 0)),
                  pl.BlockSpec((d, tn), lambda j: (0, j)),
                  pl.BlockSpec((1, tn), lambda j: (0, j))],
        out_specs=pl.BlockSpec((b, tn), lambda j: (0, j)),
        compiler_params=_params("arbitrary"),
        name="adaln",
    )(c, w_ada, b_ada.reshape(1, n))[:batch]


def _rope(v, cos, sin_signed, first_half):
    rot = jnp.where(first_half, pltpu.roll(v, LANES - HEAD_DIM // 2, 1), pltpu.roll(v, HEAD_DIM // 2, 1))
    return v * cos + rot * sin_signed


QKV_CHUNK = 256


def _column_windows(v):
    grid_rows = v.shape[0] // GRID_W
    return [jnp.concatenate([v[r * GRID_W + w0:r * GRID_W + w0 + NA_WIN] for r in range(grid_rows)], axis=0)
            for w0 in NA_WIN_START]


def _qkv_kernel(x_ref, g_ref, sc_ref, sh_ref, w_ref, cos_ref, sin_ref, o_ref, kg_ref, vg_ref):
    scale = HEAD_DIM ** -0.5 * LOG2E
    lane = lax.broadcasted_iota(jnp.int32, (QKV_CHUNK, LANES), 1)
    first_half = (lane % HEAD_DIM) < HEAD_DIM // 2
    upper = lane >= HEAD_DIM

    def tile(v, j):
        return v[:, j * LANES:(j + 1) * LANES]

    for c in range(x_ref.shape[1] // QKV_CHUNK):
        rows = slice(c * QKV_CHUNK, (c + 1) * QKV_CHUNK)
        h = (_rms(x_ref[0, rows, :]) * g_ref[...]) * (1.0 + sc_ref[0]) + sh_ref[0]
        h = h.astype(BF16)
        cos = cos_ref[rows, :]
        sin = sin_ref[rows, :]

        def proj(col, width):
            return jnp.dot(h, w_ref[:, col * LANES:(col + width) * LANES], preferred_element_type=F32)

        nq, nk, nv, sq = proj(W_NQ, 4), proj(W_NK, 4), proj(W_NV, 4), proj(W_SQ, 4)
        win_rows = slice(c * QKV_CHUNK // 2, (c + 1) * QKV_CHUNK // 2)
        for j in range(PAIRS):
            o_ref[0, COL_NQ + j, rows, :] = (tile(nq, j) * scale).astype(BF16)
            o_ref[0, COL_SQ + j, rows, :] = (_rope(tile(sq, j), cos, sin, first_half) * scale).astype(BF16)
            for ref, val in ((kg_ref, tile(nk, j)), (vg_ref, tile(nv, j))):
                for g, win in enumerate(_column_windows(val)):
                    ref[0, j, g, win_rows, :] = win.astype(BF16)
        skv = proj(W_SKV, 2)
        k = _rope(tile(skv, 0), cos, sin, first_half)
        v = tile(skv, 1)
        for t, col in ((k, COL_SK), (v, COL_SV)):
            swapped = pltpu.roll(t, HEAD_DIM, 1)
            o_ref[0, col, rows, :] = jnp.where(upper, swapped, t).astype(BF16)
            o_ref[0, col + 1, rows, :] = jnp.where(upper, t, swapped).astype(BF16)


def _qkv(x, g, scale_a, shift_a, w_in, cos, sin):
    b, s, d = x.shape
    tm = 1024
    n_in = w_in.shape[1]
    windows = jax.ShapeDtypeStruct((b, PAIRS, NA_GROUPS, s // 2, LANES), BF16)
    windows_spec = pl.BlockSpec((1, PAIRS, NA_GROUPS, tm // 2, LANES), lambda bi, i: (bi, 0, 0, i, 0))
    return pl.pallas_call(
        _qkv_kernel,
        out_shape=(jax.ShapeDtypeStruct((b, QKV_TILES, s, LANES), BF16), windows, windows),
        grid=(b, s // tm),
        in_specs=[pl.BlockSpec((1, tm, d), lambda bi, i: (bi, i, 0)),
                  pl.BlockSpec((1, d), lambda bi, i: (0, 0)),
                  pl.BlockSpec((1, 1, d), lambda bi, i: (bi, 0, 0)),
                  pl.BlockSpec((1, 1, d), lambda bi, i: (bi, 0, 0)),
                  pl.BlockSpec((d, n_in), lambda bi, i: (0, 0)),
                  pl.BlockSpec((tm, LANES), lambda bi, i: (i, 0)),
                  pl.BlockSpec((tm, LANES), lambda bi, i: (i, 0))],
        out_specs=(pl.BlockSpec((1, QKV_TILES, tm, LANES), lambda bi, i: (bi, 0, i, 0)), windows_spec, windows_spec),
        compiler_params=_params("arbitrary", "arbitrary"),
        name="qkv",
    )(x, g.reshape(1, d), scale_a, shift_a, w_in, cos, sin)


NA_QROWS = 8
NA_KROWS = NA_QROWS + NA_KH
NA_BLOCKS_PER_STEP = 8
NA_INTERLEAVE = 8
NA_Q = NA_QROWS * NA_KW
NA_K = NA_KROWS * NA_WIN
NA_RPB_ROWS = 2 * NA_KH - 1
NA_RPB_COLS = 2 * NA_KW - 1


def _clamp(v, lo, hi):
    return min(max(v, lo), hi)


def _na_first_key_row(block, rows, clip):
    return clip(block * NA_QROWS - NA_KH // 2, 0, rows - NA_KROWS)


def _na_group_tables():
    def geometry(g):
        cols = [NA_KW * g + cc for cc in range(NA_KW)]
        return (NA_WIN_START[g] - NA_KW * g,) + tuple(_clamp(c - NA_KW // 2, 0, GRID_W - NA_KW) - c for c in cols)

    seen, table_of_group, representatives = {}, [], []
    for g in range(NA_GROUPS):
        key = geometry(g)
        if key not in seen:
            seen[key] = len(representatives)
            representatives.append(g)
        table_of_group.append(seen[key])
    return table_of_group, representatives


def _na_block_types(rows):
    def geometry(block):
        r = block * NA_QROWS
        a = _na_first_key_row(block, rows, _clamp)
        return (a - r,) + tuple(_clamp(r + j - NA_KH // 2, 0, rows - NA_KH) - r for j in range(NA_QROWS))

    n_blocks = rows // NA_QROWS
    interior = geometry(n_blocks // 2)
    lead = next(b for b in range(n_blocks) if geometry(b) == interior)
    trail = next(b for b in range(n_blocks) if geometry(n_blocks - 1 - b) == interior)
    assert all(geometry(b) == interior for b in range(lead, n_blocks - trail))
    return lead, trail


def _na_kernel(q_ref, k_ref, v_ref, bias_ref, o_ref, *, rows):
    upper = lax.broadcasted_iota(jnp.int32, (NA_Q, LANES), 1) >= HEAD_DIM
    key_upper = lax.broadcasted_iota(jnp.int32, (NA_K, LANES), 1) >= HEAD_DIM
    lead, trail = _na_block_types(rows)
    first_trailing = rows // NA_QROWS - trail
    n_types = lead + trail + 1
    table_of_group, _ = _na_group_tables()
    for c0 in range(0, NA_BLOCKS_PER_STEP, NA_INTERLEAVE):
        blocks = range(c0, c0 + NA_INTERLEAVE)
        starts, types, pieces = {}, {}, {}
        for c in blocks:
            block = pl.program_id(2) * NA_BLOCKS_PER_STEP + c
            a = _na_first_key_row(block, rows, jnp.clip)
            types[c] = jnp.where(block < lead, block,
                                 jnp.where(block >= first_trailing, block - first_trailing + lead + 1, lead))
            starts[c] = pl.multiple_of(a * NA_WIN, NA_WIN)
            for g in range(NA_GROUPS):
                pieces[c, g] = [slice((c * NA_QROWS + rr) * GRID_W + g * NA_KW,
                                      (c * NA_QROWS + rr) * GRID_W + (g + 1) * NA_KW) for rr in range(NA_QROWS)]
        chains = [(c, g, hh) for c in blocks for g in range(NA_GROUPS) for hh in range(2)]
        scores, probs, outs = {}, {}, {}
        for c, g, hh in chains:
            q = jnp.concatenate([q_ref[0, 0, rws, :] for rws in pieces[c, g]], axis=0)
            qm = jnp.where(upper if hh else ~upper, q, jnp.zeros_like(q))
            ks = k_ref[0, 0, g, pl.ds(starts[c], NA_K), :]
            s = lax.dot_general(qm, ks, (((1,), (1,)), ((), ())), preferred_element_type=F32)
            scores[c, g, hh] = s + bias_ref[hh, table_of_group[g] * n_types + types[c]]
        for chain in chains:
            s = scores[chain]
            probs[chain] = jnp.exp2(s - jnp.max(s, axis=-1, keepdims=True)).astype(BF16)
        for c, g, hh in chains:
            vs = v_ref[0, 0, g, pl.ds(starts[c], NA_K), :]
            v1 = jnp.where(key_upper if hh else ~key_upper, vs, jnp.ones_like(vs))
            o = jnp.dot(probs[c, g, hh], v1, preferred_element_type=F32)
            outs[c, g, hh] = o / o[:, (1 - hh) * HEAD_DIM:(1 - hh) * HEAD_DIM + 1]
        for c in blocks:
            for g in range(NA_GROUPS):
                out = jnp.where(upper, outs[c, g, 1], outs[c, g, 0]).astype(BF16)
                for rr, rws in enumerate(pieces[c, g]):
                    o_ref[0, 0, rws, :] = out[rr * NA_KW:(rr + 1) * NA_KW]


def _na_bias_kernel(rpb_ref, o_ref, *, rows):
    h = pl.program_id(0)
    cc = lax.broadcasted_iota(jnp.int32, (NA_KW, LANES), 0)
    lane = lax.broadcasted_iota(jnp.int32, (NA_KW, LANES), 1)
    w = lane % NA_WIN
    key_row_in_tile = lane // NA_WIN
    rows_per_tile = LANES // NA_WIN
    neg = jnp.full((NA_KW, LANES), NEG_INF, F32)
    base = h * NA_RPB_ROWS * NA_RPB_COLS
    lead, trail = _na_block_types(rows)
    n_blocks = rows // NA_QROWS
    type_blocks = list(range(lead + 1)) + list(range(n_blocks - trail, n_blocks))
    for table, g in enumerate(_na_group_tables()[1]):
        qc = g * NA_KW + cc
        kc = NA_WIN_START[g] + w
        c0 = jnp.clip(qc - NA_KW // 2, 0, GRID_W - NA_KW)
        in_cols = (kc >= c0) & (kc < c0 + NA_KW)
        dc = kc - qc + NA_KW - 1
        by_row_offset = []
        for d in range(NA_RPB_ROWS):
            acc = neg
            for dd in range(NA_RPB_COLS):
                acc = jnp.where(dc == dd, rpb_ref[base + d * NA_RPB_COLS + dd], acc)
            by_row_offset.append(jnp.where(in_cols, acc * LOG2E, NEG_INF))
        for ty, block in enumerate(type_blocks):
            r = block * NA_QROWS
            a = _na_first_key_row(block, rows, _clamp)
            for j in range(NA_QROWS):
                r0 = _clamp(r + j - NA_KH // 2, 0, rows - NA_KH)
                for t in range(NA_K // LANES):
                    tile = neg
                    for part in range(rows_per_tile):
                        i = t * rows_per_tile + part
                        if r0 <= a + i < r0 + NA_KH:
                            tile = jnp.where(key_row_in_tile == part, by_row_offset[a + i - (r + j) + NA_KH - 1], tile)
                    o_ref[0, table * len(type_blocks) + ty, j * NA_KW:(j + 1) * NA_KW, t * LANES:(t + 1) * LANES] = tile


def _na_bias(rpb, rows):
    n_tables = len(_na_group_tables()[1]) * (sum(_na_block_types(rows)) + 1)
    return pl.pallas_call(
        functools.partial(_na_bias_kernel, rows=rows),
        out_shape=jax.ShapeDtypeStruct((NA_HEADS, n_tables, NA_Q, NA_K), F32),
        grid=(NA_HEADS,),
        in_specs=[pl.BlockSpec(memory_space=pltpu.SMEM)],
        out_specs=pl.BlockSpec((1, n_tables, NA_Q, NA_K), lambda h: (h, 0, 0, 0)),
        compiler_params=_params("arbitrary"),
        name="na_bias",
    )(rpb.astype(F32).reshape(-1))


def _na(qkv, kg, vg, rpb):
    b, _, s, _ = qkv.shape
    rows = s // GRID_W
    assert rows % (NA_QROWS * NA_BLOCKS_PER_STEP) == 0
    tq = NA_BLOCKS_PER_STEP * NA_QROWS * GRID_W
    bias = _na_bias(rpb, rows)
    windows_spec = pl.BlockSpec((1, 1) + kg.shape[2:], lambda bi, p, i: (bi, p, 0, 0, 0))
    return pl.pallas_call(
        functools.partial(_na_kernel, rows=rows),
        out_shape=jax.ShapeDtypeStruct((b, PAIRS, s, LANES), BF16),
        grid=(b, PAIRS, s // tq),
        in_specs=[pl.BlockSpec((1, 1, tq, LANES), lambda bi, p, i: (bi, COL_NQ + p, i, 0)),
                  windows_spec,
                  windows_spec,
                  pl.BlockSpec((2,) + bias.shape[1:], lambda bi, p, i: (p, 0, 0, 0))],
        out_specs=pl.BlockSpec((1, 1, tq, LANES), lambda bi, p, i: (bi, p, i, 0)),
        compiler_params=_params("arbitrary", "arbitrary", "arbitrary"),
        name="na",
    )(qkv, kg, vg, bias)


SWA_KEYS = 3 * SWA_BLOCK
SWA_BLOCKS_PER_STEP = 32
SWA_INTERLEAVE = 2
SWA_GROUP = SWA_HEADS // SWA_KV_HEADS
assert COL_SQ % (PAIRS // SWA_KV_HEADS) == 0


def _swa_masks():
    v = np.arange(SWA_KEYS // SWA_BLOCK)[:, None, None]
    q = np.arange(SWA_BLOCK)[None, :, None]
    k = np.arange(SWA_KEYS)[None, None, :]
    return np.where(np.abs(k - v * SWA_BLOCK - q) <= SWA_WINDOW, 0.0, NEG_INF).astype(np.float32)


def _swa_kernel(sink_ref, mask_ref, q_ref, k_ref, v_ref, o_ref, *, seq):
    kv = pl.program_id(1)
    rows = SWA_GROUP * SWA_BLOCK
    lane = lax.broadcasted_iota(jnp.int32, (SWA_BLOCK, LANES), 1)
    upper = lane >= HEAD_DIM
    head = lax.broadcasted_iota(jnp.int32, (rows, 1), 0) // SWA_BLOCK
    sink = jnp.zeros((rows, 1), F32)
    for g in range(SWA_GROUP):
        sink = jnp.where(head == g, sink_ref[kv * SWA_GROUP + g], sink)
    sink = sink * LOG2E
    for j0 in range(0, SWA_BLOCKS_PER_STEP, SWA_INTERLEAVE):
        blocks = range(j0, j0 + SWA_INTERLEAVE)
        starts, scores, probs = {}, {}, {}
        for j in blocks:
            n = pl.program_id(2) * SWA_BLOCKS_PER_STEP + j
            start = pl.multiple_of(jnp.clip((n - 1) * SWA_BLOCK, 0, seq - SWA_KEYS), SWA_BLOCK)
            starts[j] = start
            ks = k_ref[0, 0, pl.ds(start, SWA_KEYS), :]
            qs = []
            for g in range(SWA_GROUP):
                q = q_ref[0, g // 2, j * SWA_BLOCK:(j + 1) * SWA_BLOCK, :]
                qs.append(jnp.where(upper if g % 2 else ~upper, q, jnp.zeros_like(q)))
            s = lax.dot_general(jnp.concatenate(qs, axis=0), ks, (((1,), (1,)), ((), ())), preferred_element_type=F32)
            mask = mask_ref[(n * SWA_BLOCK - start) // SWA_BLOCK]
            scores[j] = s + jnp.concatenate([mask] * SWA_GROUP, axis=0)
        for j in blocks:
            s = scores[j]
            m = jnp.maximum(jnp.max(s, axis=-1, keepdims=True), sink)
            e = jnp.exp2(s - m)
            probs[j] = (e.astype(BF16), jnp.sum(e, axis=-1, keepdims=True) + jnp.exp2(sink - m))
        for j in blocks:
            e, l = probs[j]
            vs = v_ref[0, 0, pl.ds(starts[j], SWA_KEYS), :]
            o = jnp.dot(e, vs, preferred_element_type=F32) / l
            for pair in range(SWA_GROUP // 2):
                even = o[(2 * pair) * SWA_BLOCK:(2 * pair + 1) * SWA_BLOCK]
                odd = o[(2 * pair + 1) * SWA_BLOCK:(2 * pair + 2) * SWA_BLOCK]
                o_ref[0, pair, j * SWA_BLOCK:(j + 1) * SWA_BLOCK, :] = jnp.where(upper, odd, even).astype(BF16)


def _swa(qkv, sinks):
    b, _, s, _ = qkv.shape
    pairs_per_kv = PAIRS // SWA_KV_HEADS
    tq = SWA_BLOCKS_PER_STEP * SWA_BLOCK
    masks = _swa_masks()
    return pl.pallas_call(
        functools.partial(_swa_kernel, seq=s),
        out_shape=jax.ShapeDtypeStruct((b, PAIRS, s, LANES), BF16),
        grid=(b, SWA_KV_HEADS, s // tq),
        in_specs=[pl.BlockSpec(memory_space=pltpu.SMEM),
                  pl.BlockSpec(masks.shape, lambda bi, kv, n: (0, 0, 0)),
                  pl.BlockSpec((1, pairs_per_kv, tq, LANES),
                               lambda bi, kv, n: (bi, COL_SQ // pairs_per_kv + kv, n, 0)),
                  pl.BlockSpec((1, 1, s, LANES), lambda bi, kv, n: (bi, COL_SK + kv, 0, 0)),
                  pl.BlockSpec((1, 1, s, LANES), lambda bi, kv, n: (bi, COL_SV + kv, 0, 0))],
        out_specs=pl.BlockSpec((1, pairs_per_kv, tq, LANES), lambda bi, kv, n: (bi, kv, n, 0)),
        compiler_params=_params("arbitrary", "arbitrary", "arbitrary"),
        name="swa",
    )(sinks, jnp.asarray(masks), qkv, qkv, qkv)


ROUTER_COLS = N_GROUPS + N_EXPERTS


def _split2(v):
    hi = v.astype(BF16)
    lo = (v - hi.astype(F32)).astype(BF16)
    return hi, lo


ROUTER_ROWS = 40


def _route(logits_t):
    sub = lax.broadcasted_iota(jnp.int32, logits_t.shape, 0)
    big = jnp.int32(LANES)
    gmask = sub < N_GROUPS
    gl = jnp.where(gmask, logits_t, NEG_INF)
    gmax = jnp.max(gl, axis=0, keepdims=True)
    g_top = jnp.min(jnp.where(gmask & (gl == gmax), sub, big), axis=0, keepdims=True)
    g_weight = 1.0 / jnp.sum(jnp.where(gmask, jnp.exp(gl - gmax), 0.0), axis=0, keepdims=True)
    lo = ROUTER_BASE + g_top * EXPERTS_PER_GROUP
    emask = (sub >= lo) & (sub < lo + EXPERTS_PER_GROUP)
    el = jnp.where(emask, logits_t, NEG_INF)
    m1 = jnp.max(el, axis=0, keepdims=True)
    i1 = jnp.min(jnp.where(emask & (el == m1), sub, big), axis=0, keepdims=True)
    emask2 = emask & (sub != i1)
    el2 = jnp.where(emask2, logits_t, NEG_INF)
    m2 = jnp.max(el2, axis=0, keepdims=True)
    i2 = jnp.min(jnp.where(emask2 & (el2 == m2), sub, big), axis=0, keepdims=True)
    e2 = jnp.exp(m2 - m1)
    w1 = g_weight / (1.0 + e2)
    w2 = g_weight * e2 / (1.0 + e2)
    return i1 - ROUTER_BASE, i2 - ROUTER_BASE, w1, w2


TOKEN_ROWS = D_MODEL // (2 * LANES)
WORD = jnp.uint32


def _to_token_tiles(ref, v, base=0):
    t, d = v.shape
    words = pltpu.pack_elementwise([v[:, :d // 2], v[:, d // 2:]], packed_dtype=BF16)
    for s in range(TOKEN_ROWS):
        ref[pl.ds(base + s, t, stride=TOKEN_ROWS), :] = words[:, s * LANES:(s + 1) * LANES]


def _token_words(ref, t, base=0):
    return jnp.concatenate([ref[pl.ds(base + s, t, stride=TOKEN_ROWS), :] for s in range(TOKEN_ROWS)], axis=-1)


def _from_token_tiles(ref, t, base=0):
    words = _token_words(ref, t, base)
    halves = [pltpu.unpack_elementwise(words, index=j, packed_dtype=BF16, unpacked_dtype=F32) for j in range(2)]
    return jnp.concatenate(halves, axis=-1)


MIX_CHUNK = 256


def _mix_kernel(na_ref, sw_ref, x_ref, bna_ref, bsw_ref, wo_ref, gpm_ref, ga_ref, gpf_ref, scf_ref, shf_ref,
                wr_ref, x1_ref, h2_ref, r_ref, rt_ref, cntc_ref, cntr_ref):
    first_step = (pl.program_id(0) == 0) & (pl.program_id(1) == 0)

    @pl.when(first_step)
    def _():
        cntc_ref[...] = jnp.zeros_like(cntc_ref)
        cntr_ref[...] = jnp.zeros_like(cntr_ref)

    t = MIX_CHUNK
    chunks = range(x_ref.shape[1] // t)
    row_slices = [slice(c * t, (c + 1) * t) for c in chunks]
    mixes, h2s, all_logits = [], [], []
    for rows in row_slices:
        def heads(ref):
            return jnp.concatenate([ref[0, j, rows, :] for j in range(PAIRS)], axis=-1).astype(F32)

        na = (_rms(heads(na_ref)) * bna_ref[...]).astype(BF16)
        sw = (_rms(heads(sw_ref)) * bsw_ref[...]).astype(BF16)
        mixes.append(jnp.dot(na, wo_ref[:NA_WIDTH, :], preferred_element_type=F32)
                     + jnp.dot(sw, wo_ref[NA_WIDTH:, :], preferred_element_type=F32))
    gate_gain = ga_ref[0] * gpm_ref[...]
    ffn_gain = gpf_ref[...] * (1.0 + scf_ref[0])
    for c, rows in zip(chunks, row_slices):
        x1 = x_ref[0, rows, :] + _rms(mixes[c]) * gate_gain
        x1_ref[0, rows, :] = x1
        h2 = _rms(x1) * ffn_gain + shf_ref[0]
        _to_token_tiles(h2_ref, h2, base=c * t * TOKEN_ROWS)
        h2s.append(h2)
    for c in chunks:
        h_hi, h_lo = _split2(h2s[c])
        both = (jnp.dot(h_hi, wr_ref[...], preferred_element_type=F32)
                + jnp.dot(h_lo, wr_ref[...], preferred_element_type=F32))
        all_logits.append(both + pltpu.roll(both, LANES - ROUTER_COLS, 1))
    for c, rows in zip(chunks, row_slices):
        e1, e2, w1, w2 = _route(jnp.transpose(all_logits[c])[:ROUTER_ROWS])
        fields = [None] * 4
        fields[R_E1], fields[R_E2], fields[R_W1], fields[R_W2] = e1.astype(F32), e2.astype(F32), w1, w2
        rt = jnp.concatenate(fields + [jnp.zeros((SUBLANES - 4, t), F32)], axis=0)
        rt_ref[:, rows] = rt
        r = jnp.transpose(jnp.concatenate([rt, jnp.zeros((LANES - SUBLANES, t), F32)], axis=0))
        r_ref[rows, :] = r
        sub = lax.broadcasted_iota(jnp.int32, (N_EXPERTS, t), 0)
        on_sub = ((sub == e1) | (sub == e2)).astype(F32)
        cntc_ref[...] += jnp.broadcast_to(jnp.sum(on_sub, axis=1, keepdims=True), cntc_ref.shape)
        lane = lax.broadcasted_iota(jnp.int32, r.shape, 1).astype(F32)
        on_lane = ((lane == r[:, R_E1:R_E1 + 1]) | (lane == r[:, R_E2:R_E2 + 1])).astype(F32)
        cntr_ref[...] += jnp.broadcast_to(jnp.sum(on_lane, axis=0, keepdims=True), cntr_ref.shape)


def _mix(na, sw, x, beta_na, beta_swa, w_out, g_post_mix, gate_a, g_pre_ffn, scale_f, shift_f, w_router3):
    b, s, d = x.shape
    tm = 1024
    nt = s // tm
    row = lambda bi, i: (bi, i, 0)
    const2 = lambda bi, i: (0, 0)
    per_b = lambda bi, i: (bi, 0, 0)
    return pl.pallas_call(
        _mix_kernel,
        out_shape=(jax.ShapeDtypeStruct((b, s, d), F32),
                   jax.ShapeDtypeStruct((b * s * TOKEN_ROWS, LANES), WORD),
                   jax.ShapeDtypeStruct((b * s, LANES), F32),
                   jax.ShapeDtypeStruct((SUBLANES, b * s), F32),
                   jax.ShapeDtypeStruct((N_EXPERTS, LANES), F32),
                   jax.ShapeDtypeStruct((SUBLANES, LANES), F32)),
        grid=(b, s // tm),
        in_specs=[pl.BlockSpec((1, PAIRS, tm, LANES), lambda bi, i: (bi, 0, i, 0)),
                  pl.BlockSpec((1, PAIRS, tm, LANES), lambda bi, i: (bi, 0, i, 0)),
                  pl.BlockSpec((1, tm, d), row),
                  pl.BlockSpec((1, NA_WIDTH), const2),
                  pl.BlockSpec((1, SWA_WIDTH), const2),
                  pl.BlockSpec((NA_WIDTH + SWA_WIDTH, d), const2),
                  pl.BlockSpec((1, d), const2),
                  pl.BlockSpec((1, 1, d), per_b),
                  pl.BlockSpec((1, d), const2),
                  pl.BlockSpec((1, 1, d), per_b),
                  pl.BlockSpec((1, 1, d), per_b),
                  pl.BlockSpec((d, LANES), const2)],
        out_specs=(pl.BlockSpec((1, tm, d), row),
                   pl.BlockSpec((tm * TOKEN_ROWS, LANES), lambda bi, i: (bi * nt + i, 0)),
                   pl.BlockSpec((tm, LANES), lambda bi, i: (bi * nt + i, 0)),
                   pl.BlockSpec((SUBLANES, tm), lambda bi, i: (0, bi * nt + i)),
                   pl.BlockSpec((N_EXPERTS, LANES), const2),
                   pl.BlockSpec((SUBLANES, LANES), const2)),
        compiler_params=_params("arbitrary", "arbitrary"),
        name="mix",
    )(na, sw, x, beta_na.reshape(1, -1), beta_swa.reshape(1, -1), w_out, g_post_mix.reshape(1, d), gate_a,
      g_pre_ffn.reshape(1, d), scale_f, shift_f, w_router3)


MOE_TILE = 1024
MOE_SUBTILE = 256
PLAN_T = 2048
I_TILE, I_EXPERT, I_LO, I_HI, I_FIRST, I_NEW, I_NEXT, I_ORDER = range(8)


def _plan_kernel(rt_ref, cntc_ref, cntr_ref, pos_ref, items_ref, start_ref, carry_ref, *, n_tiles, n_items):
    i = pl.program_id(0)
    sub = lax.broadcasted_iota(jnp.int32, (N_EXPERTS, LANES), 0)
    lane = lax.broadcasted_iota(jnp.int32, (N_EXPERTS, LANES), 1)

    @pl.when(i == 0)
    def _():
        c_col = cntc_ref[:, 0:1]
        c_row = cntr_ref[0:1, :]
        s_col = jnp.sum(jnp.where(lane < sub, c_row, 0.0), axis=1, keepdims=True)
        s_row = jnp.sum(jnp.where(sub < lane, c_col, 0.0), axis=0, keepdims=True)
        start_ref[...] = jnp.broadcast_to(s_col, start_ref.shape)
        carry_ref[...] = jnp.zeros_like(carry_ref)

        def tiles_of(s, c):
            first = jnp.floor(s * (1.0 / MOE_TILE))
            last = jnp.floor((s + c - 1.0) * (1.0 / MOE_TILE))
            return first, jnp.where(c > 0.0, last - first + 1.0, 0.0)

        f_col, n_col = tiles_of(s_col, c_col)
        _, n_row = tiles_of(s_row, c_row)
        i_col = jnp.sum(jnp.where(lane < sub, n_row, 0.0), axis=1, keepdims=True)
        total = jnp.sum(n_col, axis=0, keepdims=True)
        k = lax.broadcasted_iota(jnp.int32, (N_EXPERTS, n_items), 1).astype(F32)
        subk = lax.broadcasted_iota(jnp.int32, (N_EXPERTS, n_items), 0).astype(F32)
        ek = jnp.sum(jnp.where(i_col + n_col <= k, 1.0, 0.0), axis=0, keepdims=True)
        k0 = k[0:1]
        valid = k0 < total
        sel = subk == ek

        def pick(v):
            return jnp.sum(jnp.where(sel, v, 0.0), axis=0, keepdims=True)

        i_k, f_k, s_k, c_k = pick(i_col), pick(f_col), pick(s_col), pick(c_col)
        tile = f_k + (k0 - i_k)
        row0 = tile * MOE_TILE
        lo = jnp.maximum(s_k, row0) - row0
        hi = jnp.minimum(s_k + c_k, row0 + MOE_TILE) - row0
        present = n_col > 0.0
        last_expert = jnp.sum(jnp.where(i_col + n_col <= total - 1.0, 1.0, 0.0), axis=0, keepdims=True)
        nxt = jnp.min(jnp.where(present & (subk > ek), subk, float(N_EXPERTS)), axis=0, keepdims=True)
        order = jnp.sum(jnp.where(present & (subk < ek), 1.0, 0.0), axis=0, keepdims=True)
        rows = [jnp.where(valid, tile, n_tiles - 1.0), jnp.where(valid, ek, last_expert),
                jnp.where(valid, lo, 0.0), jnp.where(valid, hi, 0.0),
                jnp.where(valid & (lo == 0.0), 1.0, 0.0),
                jnp.where(valid & (k0 == i_k), 1.0, 0.0), jnp.where(valid, nxt, float(N_EXPERTS)),
                jnp.where(valid, order, 0.0)]
        assert len(rows) == SUBLANES
        items_ref[...] = jnp.concatenate(rows, axis=0).astype(jnp.int32)

    t = rt_ref.shape[1]
    e1 = rt_ref[R_E1:R_E1 + 1, :]
    e2 = rt_ref[R_E2:R_E2 + 1, :]
    sub_t = lax.broadcasted_iota(jnp.int32, (N_EXPERTS, t), 0).astype(F32)
    oh1 = sub_t == e1
    oh2 = sub_t == e2
    oh = (oh1 | oh2).astype(F32)
    before = (lax.broadcasted_iota(jnp.int32, (t, t), 0) < lax.broadcasted_iota(jnp.int32, (t, t), 1)).astype(BF16)
    rank = jnp.dot(oh.astype(BF16), before, preferred_element_type=F32)
    base = start_ref[:, 0:1] + carry_ref[:, 0:1] + rank
    pos1 = jnp.sum(jnp.where(oh1, base, 0.0), axis=0, keepdims=True)
    pos2 = jnp.sum(jnp.where(oh2, base, 0.0), axis=0, keepdims=True)
    carry_ref[...] += jnp.broadcast_to(jnp.sum(oh, axis=1, keepdims=True), carry_ref.shape)
    pos = jnp.concatenate([pos1, pos2] + [jnp.zeros_like(pos1)] * (SUBLANES - 2), axis=0)
    pos_ref[...] = pos.astype(jnp.int32)


def _plan(rt, cntc, cntr):
    n = rt.shape[1]
    n_tiles = 2 * n // MOE_TILE
    n_items = 2 * LANES
    assert n_tiles + N_EXPERTS <= n_items
    return pl.pallas_call(
        functools.partial(_plan_kernel, n_tiles=n_tiles, n_items=n_items),
        out_shape=(jax.ShapeDtypeStruct((SUBLANES, n), jnp.int32),
                   jax.ShapeDtypeStruct((SUBLANES, n_items), jnp.int32)),
        grid=(n // PLAN_T,),
        in_specs=[pl.BlockSpec((SUBLANES, PLAN_T), lambda i: (0, i)),
                  pl.BlockSpec((N_EXPERTS, LANES), lambda i: (0, 0)),
                  pl.BlockSpec((SUBLANES, LANES), lambda i: (0, 0))],
        out_specs=(pl.BlockSpec((SUBLANES, PLAN_T), lambda i: (0, i)),
                   pl.BlockSpec((SUBLANES, n_items), lambda i: (0, 0))),
        scratch_shapes=[pltpu.VMEM((N_EXPERTS, LANES), F32), pltpu.VMEM((N_EXPERTS, LANES), F32)],
        compiler_params=_params("arbitrary"),
        name="plan",
    )(rt, cntc, cntr)


DISPATCH_T = 2048


def _token_rows(ref, index):
    return ref.at[pl.ds(pl.multiple_of(index * TOKEN_ROWS, TOKEN_ROWS), TOKEN_ROWS), :]


def _dispatch_kernel(pos_ref, h_ref, xs_hbm, sem):
    def body(r, carry):
        src = _token_rows(h_ref, r)
        for k in range(2):
            pltpu.make_async_copy(src, _token_rows(xs_hbm, pos_ref[k, r]), sem.at[0]).start(priority=k)
        return carry

    lax.fori_loop(0, DISPATCH_T, body, 0, unroll=8)
    for k in range(2):
        pltpu.make_async_copy(h_ref, xs_hbm.at[pl.ds(0, DISPATCH_T * TOKEN_ROWS), :], sem.at[0]).wait()


def _dispatch(pos, h2):
    n = pos.shape[1]
    return pl.pallas_call(
        _dispatch_kernel,
        out_shape=jax.ShapeDtypeStruct((2 * n * TOKEN_ROWS, LANES), WORD),
        grid=(n // DISPATCH_T,),
        in_specs=[pl.BlockSpec((SUBLANES, DISPATCH_T), lambda i: (0, i), memory_space=pltpu.SMEM),
                  pl.BlockSpec((DISPATCH_T * TOKEN_ROWS, LANES), lambda i: (i, 0))],
        out_specs=pl.BlockSpec(memory_space=pl.ANY),
        scratch_shapes=[pltpu.SemaphoreType.DMA((1,))],
        compiler_params=_params("arbitrary"),
        name="dispatch",
    )(pos, h2)


def _expert_kernel(tile_ref, exp_ref, lo_ref, hi_ref, first_ref, new_ref, next_ref, order_ref,
                   xs_ref, wg_hbm, wu_hbm, wd_hbm, o_ref, wg_buf, wu_buf, wd_buf, sem):
    k = pl.program_id(0)
    lo = lo_ref[k]
    hi = hi_ref[k]
    slot = order_ref[k] % 2

    def weight_copies(expert, sl):
        return [pltpu.make_async_copy(src.at[expert], dst.at[sl], sem.at[sl])
                for src, dst in ((wg_hbm, wg_buf), (wu_hbm, wu_buf), (wd_hbm, wd_buf))]

    @pl.when(k == 0)
    def _():
        for copy in weight_copies(exp_ref[0], 0):
            copy.start()

    @pl.when(new_ref[k] == 1)
    def _():
        for copy in weight_copies(exp_ref[k], slot):
            copy.wait()

        @pl.when(next_ref[k] < N_EXPERTS)
        def _():
            for copy in weight_copies(next_ref[k], 1 - slot):
                copy.start()

    @pl.when(first_ref[k] == 1)
    def _():
        o_ref[...] = jnp.zeros_like(o_ref)

    for part in range(MOE_TILE // MOE_SUBTILE):
        first_row = part * MOE_SUBTILE
        base = first_row * TOKEN_ROWS

        @pl.when((hi > first_row) & (lo < first_row + MOE_SUBTILE))
        def _():
            x = _from_token_tiles(xs_ref, MOE_SUBTILE, base=base).astype(BF16)
            gate = jnp.dot(x, wg_buf[slot].astype(BF16), preferred_element_type=F32)
            up = jnp.dot(x, wu_buf[slot].astype(BF16), preferred_element_type=F32)
            he = (gate * jax.nn.sigmoid(gate) * up).astype(BF16)
            ye = jnp.dot(he, wd_buf[slot].astype(BF16), preferred_element_type=F32)
            row = first_row + lax.broadcasted_iota(jnp.int32, (MOE_SUBTILE, 1), 0)
            mine = (row >= lo) & (row < hi)
            _to_token_tiles(o_ref, jnp.where(mine, ye, _from_token_tiles(o_ref, MOE_SUBTILE, base=base)), base=base)


def _experts(items, xs, w_gate, w_up, w_down):
    n_rows = xs.shape[0] // TOKEN_ROWS
    n_items = n_rows // MOE_TILE + N_EXPERTS
    d = w_gate.shape[1]
    tile_map = lambda k, tile, *_: (tile[k], 0)
    grid_spec = pltpu.PrefetchScalarGridSpec(
        num_scalar_prefetch=SUBLANES,
        grid=(n_items,),
        in_specs=[pl.BlockSpec((MOE_TILE * TOKEN_ROWS, LANES), tile_map),
                  pl.BlockSpec(memory_space=pl.ANY),
                  pl.BlockSpec(memory_space=pl.ANY),
                  pl.BlockSpec(memory_space=pl.ANY)],
        out_specs=pl.BlockSpec((MOE_TILE * TOKEN_ROWS, LANES), tile_map),
        scratch_shapes=[pltpu.VMEM((2, d, EXPERT_FF), F32), pltpu.VMEM((2, d, EXPERT_FF), F32),
                        pltpu.VMEM((2, EXPERT_FF, d), F32), pltpu.SemaphoreType.DMA((2,))])
    return pl.pallas_call(
        _expert_kernel,
        out_shape=jax.ShapeDtypeStruct(xs.shape, WORD),
        grid_spec=grid_spec,
        compiler_params=_params("arbitrary"),
        name="experts",
    )(*(items[j, :n_items] for j in range(SUBLANES)), xs, w_gate, w_up, w_down)


COMBINE_T = 512


def _combine_kernel(pos_ref, posn_ref, ys_hbm, r_ref, x1_ref, gf_ref, gpost_ref, o_ref, buf, sem):
    i = pl.program_id(0)
    n = pl.num_programs(0)
    slot = i % 2
    slot_rows = COMBINE_T * TOKEN_ROWS

    def start_row(p_ref, sl, r):
        for k in range(2):
            dst = buf.at[sl, pl.ds(pl.multiple_of(k * slot_rows + r * TOKEN_ROWS, TOKEN_ROWS), TOKEN_ROWS), :]
            pltpu.make_async_copy(_token_rows(ys_hbm, p_ref[k, r]), dst, sem.at[sl]).start(priority=k)

    def wait_slot(sl):
        pltpu.make_async_copy(ys_hbm.at[pl.ds(0, 2 * slot_rows), :], buf.at[sl], sem.at[sl]).wait()

    def issue(p_ref, sl):
        def body(r, carry):
            start_row(p_ref, sl, r)
            return carry

        lax.fori_loop(0, COMBINE_T, body, 0, unroll=8)

    @pl.when(i == 0)
    def _():
        issue(pos_ref, 0)

    @pl.when(i + 1 < n)
    def _():
        issue(posn_ref, 1 - slot)

    wait_slot(slot)
    ya = _from_token_tiles(buf.at[slot], COMBINE_T)
    yb = _from_token_tiles(buf.at[slot], COMBINE_T, base=slot_rows)
    r = r_ref[...]
    y = r[:, R_W1:R_W1 + 1] * ya + r[:, R_W2:R_W2 + 1] * yb
    o_ref[0] = x1_ref[0] + gf_ref[0] * (_rms(y) * gpost_ref[...])


def _combine(pos, ys, r, x1, gate_f, g_post_ffn):
    b, s, d = x1.shape
    nt = s // COMBINE_T
    n_steps = b * nt
    row = lambda i: (i // nt, i % nt, 0)
    pos_spec = lambda f: pl.BlockSpec((SUBLANES, COMBINE_T), lambda i: (0, f(i)), memory_space=pltpu.SMEM)
    return pl.pallas_call(
        _combine_kernel,
        out_shape=jax.ShapeDtypeStruct((b, s, d), F32),
        grid=(n_steps,),
        in_specs=[pos_spec(lambda i: i),
                  pos_spec(lambda i: jnp.minimum(i + 1, n_steps - 1)),
                  pl.BlockSpec(memory_space=pl.ANY),
                  pl.BlockSpec((COMBINE_T, LANES), lambda i: (i, 0)),
                  pl.BlockSpec((1, COMBINE_T, d), row),
                  pl.BlockSpec((1, 1, d), lambda i: (i // nt, 0, 0)),
                  pl.BlockSpec((1, d), lambda i: (0, 0))],
        out_specs=pl.BlockSpec((1, COMBINE_T, d), row),
        scratch_shapes=[pltpu.VMEM((2, 2 * COMBINE_T * TOKEN_ROWS, LANES), WORD), pltpu.SemaphoreType.DMA((2,))],
        compiler_params=_params("arbitrary"),
        name="combine",
    )(pos, pos, ys, r, x1, gate_f, g_post_ffn.reshape(1, d))


def _rope_tables(s):
    half = HEAD_DIM // 2
    inv = ROPE_THETA ** (-np.arange(half, dtype=np.float64) * 2.0 / HEAD_DIM)
    ang = np.arange(s, dtype=np.float64)[:, None] * inv[None, :]
    cos = np.cos(ang)
    sin = np.sin(ang)
    cos = np.concatenate([cos, cos, cos, cos], axis=-1)
    sin_signed = np.concatenate([-sin, sin, -sin, sin], axis=-1)
    return jnp.asarray(cos, F32), jnp.asarray(sin_signed, F32)


def _router_weights(w_group_router, w_expert_router):
    d = w_group_router.shape[0]
    we = jnp.transpose(w_expert_router, (1, 0, 2)).reshape(d, N_EXPERTS)
    hi, lo = _split2(jnp.concatenate([w_group_router, we], axis=-1))
    return jnp.concatenate([hi, lo, jnp.zeros((d, LANES - 2 * ROUTER_COLS), BF16)], axis=-1)


def kernel(x, c, w_ada, b_ada, g_pre_mix, w_in, na_rpb, swa_sinks, beta_na, beta_swa, w_out, g_post_mix, g_pre_ffn,
           w_group_router, w_expert_router, w_gate, w_up, w_down, g_post_ffn):
    b, s, d = x.shape
    depth = w_ada.shape[0]
    cos, sin_signed = _rope_tables(s)
    for l in range(depth):
        mod = _adaln(c, w_ada[l], b_ada[l]).reshape(b, N_MOD, 1, d)
        shift_a, scale_a, gate_a, shift_f, scale_f, gate_f = (mod[:, k] for k in range(N_MOD))
        qkv, na_kg, na_vg = _qkv(x, g_pre_mix[l], scale_a, shift_a, w_in[l].astype(BF16), cos, sin_signed)
        na = _na(qkv, na_kg, na_vg, na_rpb[l])
        sw = _swa(qkv, swa_sinks[l])
        x1, h2, r, rt, cntc, cntr = _mix(na, sw, x, beta_na[l], beta_swa[l], w_out[l].astype(BF16), g_post_mix[l],
                                          gate_a, g_pre_ffn[l], scale_f, shift_f,
                                          _router_weights(w_group_router[l], w_expert_router[l]))
        pos, items = _plan(rt, cntc, cntr)
        xs = _dispatch(pos, h2)
        ys = _experts(items, xs, w_gate[l], w_up[l], w_down[l])
        x = _combine(pos, ys, r, x1, gate_f, g_post_ffn[l])
    return x
```

```python
import functools

import jax
import jax.numpy as jnp
import numpy as np
from jax import lax
from jax.experimental import pallas as pl
from jax.experimental.pallas import tpu as pltpu

D_MODEL = 1024
GRID_W = 64
HEAD_DIM = 64
NA_HEADS = 8
NA_KH = 8
NA_KW = 16
SWA_HEADS = 8
SWA_KV_HEADS = 2
SWA_WINDOW = 128
SWA_BLOCK = 128
ROPE_THETA = 10000.0
NA_WIDTH = NA_HEADS * HEAD_DIM
SWA_WIDTH = SWA_HEADS * HEAD_DIM
N_GROUPS = 4
EXPERTS_PER_GROUP = 8
N_EXPERTS = N_GROUPS * EXPERTS_PER_GROUP
EXPERT_FF = 256
N_MOD = 6
EPS = 1e-6
NEG_INF = -1e30

LANES = 128
SUBLANES = 8
R_E1, R_E2, R_W1, R_W2 = range(4)
PAIRS = NA_HEADS // 2
W_NQ, W_NK, W_NV, W_SQ, W_SKV = 0, 4, 8, 12, 16
COL_NQ, COL_SQ, COL_SK, COL_SV = 0, 4, 8, 10
QKV_TILES = 12
NA_GROUPS = GRID_W // NA_KW
NA_WIN = 2 * NA_KW
NA_WIN_START = tuple(min(max(NA_KW * g - NA_KW // 2, 0), GRID_W - NA_WIN) for g in range(NA_GROUPS))
ROUTER_BASE = N_GROUPS
VMEM_LIMIT = 56 * 1024 * 1024

F32 = jnp.float32
BF16 = jnp.bfloat16
LOG2E = 1.4426950408889634


def _rms(v):
    return v * lax.rsqrt(jnp.mean(v * v, axis=-1, keepdims=True) + EPS)


def _params(*sem):
    return pltpu.CompilerParams(dimension_semantics=sem, vmem_limit_bytes=VMEM_LIMIT)


def _adaln_kernel(c_ref, w_ref, b_ref, o_ref, *, batch):
    c = c_ref[...]
    a_t = jnp.transpose(c * jax.nn.sigmoid(c))
    w = w_ref[...]
    rows = [jnp.sum(a_t[:, bi:bi + 1] * w, axis=0, keepdims=True) for bi in range(batch)]
    rows.append(jnp.zeros((c.shape[0] - batch, w.shape[1]), F32))
    o_ref[...] = jnp.concatenate(rows, axis=0) + b_ref[...]


def _adaln(c, w_ada, b_ada):
    batch, d = c.shape
    n = w_ada.shape[1]
    tn = 1024
    b = SUBLANES
    assert batch < b
    c = jnp.pad(c, ((0, b - batch), (0, 0)))
    return pl.pallas_call(
        functools.partial(_adaln_kernel, batch=batch),
        out_shape=jax.ShapeDtypeStruct((b, n), F32),
        grid=(n // tn,),
        in_specs=[pl.BlockSpec((b, d), lambda j: (0, 0)),
                  pl.BlockSpec((d, tn), lambda j: (0, j)),
                  pl.BlockSpec((1, tn), lambda j: (0, j))],
        out_specs=pl.BlockSpec((b, tn), lambda j: (0, j)),
        compiler_params=_params("arbitrary"),
        name="adaln",
    )(c, w_ada, b_ada.reshape(1, n))[:batch]


def _rope(v, cos, sin_signed, first_half):
    rot = jnp.where(first_half, pltpu.roll(v, LANES - HEAD_DIM // 2, 1), pltpu.roll(v, HEAD_DIM // 2, 1))
    return v * cos + rot * sin_signed


QKV_CHUNK = 256


def _column_windows(v):
    grid_rows = v.shape[0] // GRID_W
    return [jnp.concatenate([v[r * GRID_W + w0:r * GRID_W + w0 + NA_WIN] for r in range(grid_rows)], axis=0)
            for w0 in NA_WIN_START]


def _qkv_kernel(x_ref, g_ref, sc_ref, sh_ref, w_ref, cos_ref, sin_ref, o_ref, kg_ref, vg_ref):
    scale = HEAD_DIM ** -0.5 * LOG2E
    lane = lax.broadcasted_iota(jnp.int32, (QKV_CHUNK, LANES), 1)
    first_half = (lane % HEAD_DIM) < HEAD_DIM // 2
    upper = lane >= HEAD_DIM

    def tile(v, j):
        return v[:, j * LANES:(j + 1) * LANES]

    for c in range(x_ref.shape[1] // QKV_CHUNK):
        rows = slice(c * QKV_CHUNK, (c + 1) * QKV_CHUNK)
        h = (_rms(x_ref[0, rows, :]) * g_ref[...]) * (1.0 + sc_ref[0]) + sh_ref[0]
        h = h.astype(BF16)
        cos = cos_ref[rows, :]
        sin = sin_ref[rows, :]

        def proj(col, width):
            return jnp.dot(h, w_ref[:, col * LANES:(col + width) * LANES], preferred_element_type=F32)

        nq, nk, nv, sq = proj(W_NQ, 4), proj(W_NK, 4), proj(W_NV, 4), proj(W_SQ, 4)
        win_rows = slice(c * QKV_CHUNK // 2, (c + 1) * QKV_CHUNK // 2)
        for j in range(PAIRS):
            o_ref[0, COL_NQ + j, rows, :] = (tile(nq, j) * scale).astype(BF16)
            o_ref[0, COL_SQ + j, rows, :] = (_rope(tile(sq, j), cos, sin, first_half) * scale).astype(BF16)
            for ref, val in ((kg_ref, tile(nk, j)), (vg_ref, tile(nv, j))):
                for g, win in enumerate(_column_windows(val)):
                    ref[0, j, g, win_rows, :] = win.astype(BF16)
        skv = proj(W_SKV, 2)
        k = _rope(tile(skv, 0), cos, sin, first_half)
        v = tile(skv, 1)
        for t, col in ((k, COL_SK), (v, COL_SV)):
            swapped = pltpu.roll(t, HEAD_DIM, 1)
            o_ref[0, col, rows, :] = jnp.where(upper, swapped, t).astype(BF16)
            o_ref[0, col + 1, rows, :] = jnp.where(upper, t, swapped).astype(BF16)


def _qkv(x, g, scale_a, shift_a, w_in, cos, sin):
    b, s, d = x.shape
    tm = 1024
    n_in = w_in.shape[1]
    windows = jax.ShapeDtypeStruct((b, PAIRS, NA_GROUPS, s // 2, LANES), BF16)
    windows_spec = pl.BlockSpec((1, PAIRS, NA_GROUPS, tm // 2, LANES), lambda bi, i: (bi, 0, 0, i, 0))
    return pl.pallas_call(
        _qkv_kernel,
        out_shape=(jax.ShapeDtypeStruct((b, QKV_TILES, s, LANES), BF16), windows, windows),
        grid=(b, s // tm),
        in_specs=[pl.BlockSpec((1, tm, d), lambda bi, i: (bi, i, 0)),
                  pl.BlockSpec((1, d), lambda bi, i: (0, 0)),
                  pl.BlockSpec((1, 1, d), lambda bi, i: (bi, 0, 0)),
                  pl.BlockSpec((1, 1, d), lambda bi, i: (bi, 0, 0)),
                  pl.BlockSpec((d, n_in), lambda bi, i: (0, 0)),
                  pl.BlockSpec((tm, LANES), lambda bi, i: (i, 0)),
                  pl.BlockSpec((tm, LANES), lambda bi, i: (i, 0))],
        out_specs=(pl.BlockSpec((1, QKV_TILES, tm, LANES), lambda bi, i: (bi, 0, i, 0)), windows_spec, windows_spec),
        compiler_params=_params("arbitrary", "arbitrary"),
        name="qkv",
    )(x, g.reshape(1, d), scale_a, shift_a, w_in, cos, sin)


NA_QROWS = 8
NA_KROWS = NA_QROWS + NA_KH
NA_BLOCKS_PER_STEP = 8
NA_INTERLEAVE = 8
NA_Q = NA_QROWS * NA_KW
NA_K = NA_KROWS * NA_WIN
NA_RPB_ROWS = 2 * NA_KH - 1
NA_RPB_COLS = 2 * NA_KW - 1


def _clamp(v, lo, hi):
    return min(max(v, lo), hi)


def _na_first_key_row(block, rows, clip):
    return clip(block * NA_QROWS - NA_KH // 2, 0, rows - NA_KROWS)


def _na_group_tables():
    def geometry(g):
        cols = [NA_KW * g + cc for cc in range(NA_KW)]
        return (NA_WIN_START[g] - NA_KW * g,) + tuple(_clamp(c - NA_KW // 2, 0, GRID_W - NA_KW) - c for c in cols)

    seen, table_of_group, representatives = {}, [], []
    for g in range(NA_GROUPS):
        key = geometry(g)
        if key not in seen:
            seen[key] = len(representatives)
            representatives.append(g)
        table_of_group.append(seen[key])
    return table_of_group, representatives


def _na_block_types(rows):
    def geometry(block):
        r = block * NA_QROWS
        a = _na_first_key_row(block, rows, _clamp)
        return (a - r,) + tuple(_clamp(r + j - NA_KH // 2, 0, rows - NA_KH) - r for j in range(NA_QROWS))

    n_blocks = rows // NA_QROWS
    interior = geometry(n_blocks // 2)
    lead = next(b for b in range(n_blocks) if geometry(b) == interior)
    trail = next(b for b in range(n_blocks) if geometry(n_blocks - 1 - b) == interior)
    assert all(geometry(b) == interior for b in range(lead, n_blocks - trail))
    return lead, trail


def _na_kernel(q_ref, k_ref, v_ref, bias_ref, o_ref, *, rows):
    upper = lax.broadcasted_iota(jnp.int32, (NA_Q, LANES), 1) >= HEAD_DIM
    key_upper = lax.broadcasted_iota(jnp.int32, (NA_K, LANES), 1) >= HEAD_DIM
    lead, trail = _na_block_types(rows)
    first_trailing = rows // NA_QROWS - trail
    n_types = lead + trail + 1
    table_of_group, _ = _na_group_tables()
    for c0 in range(0, NA_BLOCKS_PER_STEP, NA_INTERLEAVE):
        blocks = range(c0, c0 + NA_INTERLEAVE)
        starts, types, pieces = {}, {}, {}
        for c in blocks:
            block = pl.program_id(2) * NA_BLOCKS_PER_STEP + c
            a = _na_first_key_row(block, rows, jnp.clip)
            types[c] = jnp.where(block < lead, block,
                                 jnp.where(block >= first_trailing, block - first_trailing + lead + 1, lead))
            starts[c] = pl.multiple_of(a * NA_WIN, NA_WIN)
            for g in range(NA_GROUPS):
                pieces[c, g] = [slice((c * NA_QROWS + rr) * GRID_W + g * NA_KW,
                                      (c * NA_QROWS + rr) * GRID_W + (g + 1) * NA_KW) for rr in range(NA_QROWS)]
        chains = [(c, g, hh) for c in blocks for g in range(NA_GROUPS) for hh in range(2)]
        scores, probs, outs = {}, {}, {}
        for c, g, hh in chains:
            q = jnp.concatenate([q_ref[0, 0, rws, :] for rws in pieces[c, g]], axis=0)
            qm = jnp.where(upper if hh else ~upper, q, jnp.zeros_like(q))
            ks = k_ref[0, 0, g, pl.ds(starts[c], NA_K), :]
            s = lax.dot_general(qm, ks, (((1,), (1,)), ((), ())), preferred_element_type=F32)
            scores[c, g, hh] = s + bias_ref[hh, table_of_group[g] * n_types + types[c]]
        for chain in chains:
            s = scores[chain]
            probs[chain] = jnp.exp2(s - jnp.max(s, axis=-1, keepdims=True)).astype(BF16)
        for c, g, hh in chains:
            vs = v_ref[0, 0, g, pl.ds(starts[c], NA_K), :]
            v1 = jnp.where(key_upper if hh else ~key_upper, vs, jnp.ones_like(vs))
            o = jnp.dot(probs[c, g, hh], v1, preferred_element_type=F32)
            outs[c, g, hh] = o / o[:, (1 - hh) * HEAD_DIM:(1 - hh) * HEAD_DIM + 1]
        for c in blocks:
            for g in range(NA_GROUPS):
                out = jnp.where(upper, outs[c, g, 1], outs[c, g, 0]).astype(BF16)
                for rr, rws in enumerate(pieces[c, g]):
                    o_ref[0, 0, rws, :] = out[rr * NA_KW:(rr + 1) * NA_KW]


def _na_bias_kernel(rpb_ref, o_ref, *, rows):
    h = pl.program_id(0)
    cc = lax.broadcasted_iota(jnp.int32, (NA_KW, LANES), 0)
    lane = lax.broadcasted_iota(jnp.int32, (NA_KW, LANES), 1)
    w = lane % NA_WIN
    key_row_in_tile = lane // NA_WIN
    rows_per_tile = LANES // NA_WIN
    neg = jnp.full((NA_KW, LANES), NEG_INF, F32)
    base = h * NA_RPB_ROWS * NA_RPB_COLS
    lead, trail = _na_block_types(rows)
    n_blocks = rows // NA_QROWS
    type_blocks = list(range(lead + 1)) + list(range(n_blocks - trail, n_blocks))
    for table, g in enumerate(_na_group_tables()[1]):
        qc = g * NA_KW + cc
        kc = NA_WIN_START[g] + w
        c0 = jnp.clip(qc - NA_KW // 2, 0, GRID_W - NA_KW)
        in_cols = (kc >= c0) & (kc < c0 + NA_KW)
        dc = kc - qc + NA_KW - 1
        by_row_offset = []
        for d in range(NA_RPB_ROWS):
            acc = neg
            for dd in range(NA_RPB_COLS):
                acc = jnp.where(dc == dd, rpb_ref[base + d * NA_RPB_COLS + dd], acc)
            by_row_offset.append(jnp.where(in_cols, acc * LOG2E, NEG_INF))
        for ty, block in enumerate(type_blocks):
            r = block * NA_QROWS
            a = _na_first_key_row(block, rows, _clamp)
            for j in range(NA_QROWS):
                r0 = _clamp(r + j - NA_KH // 2, 0, rows - NA_KH)
                for t in range(NA_K // LANES):
                    tile = neg
                    for part in range(rows_per_tile):
                        i = t * rows_per_tile + part
                        if r0 <= a + i < r0 + NA_KH:
                            tile = jnp.where(key_row_in_tile == part, by_row_offset[a + i - (r + j) + NA_KH - 1], tile)
                    o_ref[0, table * len(type_blocks) + ty, j * NA_KW:(j + 1) * NA_KW, t * LANES:(t + 1) * LANES] = tile


def _na_bias(rpb, rows):
    n_tables = len(_na_group_tables()[1]) * (sum(_na_block_types(rows)) + 1)
    return pl.pallas_call(
        functools.partial(_na_bias_kernel, rows=rows),
        out_shape=jax.ShapeDtypeStruct((NA_HEADS, n_tables, NA_Q, NA_K), F32),
        grid=(NA_HEADS,),
        in_specs=[pl.BlockSpec(memory_space=pltpu.SMEM)],
        out_specs=pl.BlockSpec((1, n_tables, NA_Q, NA_K), lambda h: (h, 0, 0, 0)),
        compiler_params=_params("arbitrary"),
        name="na_bias",
    )(rpb.astype(F32).reshape(-1))


def _na(qkv, kg, vg, rpb):
    b, _, s, _ = qkv.shape
    rows = s // GRID_W
    assert rows % (NA_QROWS * NA_BLOCKS_PER_STEP) == 0
    tq = NA_BLOCKS_PER_STEP * NA_QROWS * GRID_W
    bias = _na_bias(rpb, rows)
    windows_spec = pl.BlockSpec((1, 1) + kg.shape[2:], lambda bi, p, i: (bi, p, 0, 0, 0))
    return pl.pallas_call(
        functools.partial(_na_kernel, rows=rows),
        out_shape=jax.ShapeDtypeStruct((b, PAIRS, s, LANES), BF16),
        grid=(b, PAIRS, s // tq),
        in_specs=[pl.BlockSpec((1, 1, tq, LANES), lambda bi, p, i: (bi, COL_NQ + p, i, 0)),
                  windows_spec,
                  windows_spec,
                  pl.BlockSpec((2,) + bias.shape[1:], lambda bi, p, i: (p, 0, 0, 0))],
        out_specs=pl.BlockSpec((1, 1, tq, LANES), lambda bi, p, i: (bi, p, i, 0)),
        compiler_params=_params("arbitrary", "arbitrary", "arbitrary"),
        name="na",
    )(qkv, kg, vg, bias)


SWA_KEYS = 3 * SWA_BLOCK
SWA_BLOCKS_PER_STEP = 32
SWA_INTERLEAVE = 2
SWA_GROUP = SWA_HEADS // SWA_KV_HEADS
assert COL_SQ % (PAIRS // SWA_KV_HEADS) == 0


def _swa_masks():
    v = np.arange(SWA_KEYS // SWA_BLOCK)[:, None, None]
    q = np.arange(SWA_BLOCK)[None, :, None]
    k = np.arange(SWA_KEYS)[None, None, :]
    return np.where(np.abs(k - v * SWA_BLOCK - q) <= SWA_WINDOW, 0.0, NEG_INF).astype(np.float32)


def _swa_kernel(sink_ref, mask_ref, q_ref, k_ref, v_ref, o_ref, *, seq):
    kv = pl.program_id(1)
    rows = SWA_GROUP * SWA_BLOCK
    lane = lax.broadcasted_iota(jnp.int32, (SWA_BLOCK, LANES), 1)
    upper = lane >= HEAD_DIM
    head = lax.broadcasted_iota(jnp.int32, (rows, 1), 0) // SWA_BLOCK
    sink = jnp.zeros((rows, 1), F32)
    for g in range(SWA_GROUP):
        sink = jnp.where(head == g, sink_ref[kv * SWA_GROUP + g], sink)
    sink = sink * LOG2E
    for j0 in range(0, SWA_BLOCKS_PER_STEP, SWA_INTERLEAVE):
        blocks = range(j0, j0 + SWA_INTERLEAVE)
        starts, scores, probs = {}, {}, {}
        for j in blocks:
            n = pl.program_id(2) * SWA_BLOCKS_PER_STEP + j
            start = pl.multiple_of(jnp.clip((n - 1) * SWA_BLOCK, 0, seq - SWA_KEYS), SWA_BLOCK)
            starts[j] = start
            ks = k_ref[0, 0, pl.ds(start, SWA_KEYS), :]
            qs = []
            for g in range(SWA_GROUP):
                q = q_ref[0, g // 2, j * SWA_BLOCK:(j + 1) * SWA_BLOCK, :]
                qs.append(jnp.where(upper if g % 2 else ~upper, q, jnp.zeros_like(q)))
            s = lax.dot_general(jnp.concatenate(qs, axis=0), ks, (((1,), (1,)), ((), ())), preferred_element_type=F32)
            mask = mask_ref[(n * SWA_BLOCK - start) // SWA_BLOCK]
            scores[j] = s + jnp.concatenate([mask] * SWA_GROUP, axis=0)
        for j in blocks:
            s = scores[j]
            m = jnp.maximum(jnp.max(s, axis=-1, keepdims=True), sink)
            e = jnp.exp2(s - m)
            probs[j] = (e.astype(BF16), jnp.sum(e, axis=-1, keepdims=True) + jnp.exp2(sink - m))
        for j in blocks:
            e, l = probs[j]
            vs = v_ref[0, 0, pl.ds(starts[j], SWA_KEYS), :]
            o = jnp.dot(e, vs, preferred_element_type=F32) / l
            for pair in range(SWA_GROUP // 2):
                even = o[(2 * pair) * SWA_BLOCK:(2 * pair + 1) * SWA_BLOCK]
                odd = o[(2 * pair + 1) * SWA_BLOCK:(2 * pair + 2) * SWA_BLOCK]
                o_ref[0, pair, j * SWA_BLOCK:(j + 1) * SWA_BLOCK, :] = jnp.where(upper, odd, even).astype(BF16)


def _swa(qkv, sinks):
    b, _, s, _ = qkv.shape
    pairs_per_kv = PAIRS // SWA_KV_HEADS
    tq = SWA_BLOCKS_PER_STEP * SWA_BLOCK
    masks = _swa_masks()
    return pl.pallas_call(
        functools.partial(_swa_kernel, seq=s),
        out_shape=jax.ShapeDtypeStruct((b, PAIRS, s, LANES), BF16),
        grid=(b, SWA_KV_HEADS, s // tq),
        in_specs=[pl.BlockSpec(memory_space=pltpu.SMEM),
                  pl.BlockSpec(masks.shape, lambda bi, kv, n: (0, 0, 0)),
                  pl.BlockSpec((1, pairs_per_kv, tq, LANES),
                               lambda bi, kv, n: (bi, COL_SQ // pairs_per_kv + kv, n, 0)),
                  pl.BlockSpec((1, 1, s, LANES), lambda bi, kv, n: (bi, COL_SK + kv, 0, 0)),
                  pl.BlockSpec((1, 1, s, LANES), lambda bi, kv, n: (bi, COL_SV + kv, 0, 0))],
        out_specs=pl.BlockSpec((1, pairs_per_kv, tq, LANES), lambda bi, kv, n: (bi, kv, n, 0)),
        compiler_params=_params("arbitrary", "arbitrary", "arbitrary"),
        name="swa",
    )(sinks, jnp.asarray(masks), qkv, qkv, qkv)


ROUTER_COLS = N_GROUPS + N_EXPERTS


def _split2(v):
    hi = v.astype(BF16)
    lo = (v - hi.astype(F32)).astype(BF16)
    return hi, lo


ROUTER_ROWS = 40


def _route(logits_t):
    sub = lax.broadcasted_iota(jnp.int32, logits_t.shape, 0)
    big = jnp.int32(LANES)
    gmask = sub < N_GROUPS
    gl = jnp.where(gmask, logits_t, NEG_INF)
    gmax = jnp.max(gl, axis=0, keepdims=True)
    g_top = jnp.min(jnp.where(gmask & (gl == gmax), sub, big), axis=0, keepdims=True)
    g_weight = 1.0 / jnp.sum(jnp.where(gmask, jnp.exp(gl - gmax), 0.0), axis=0, keepdims=True)
    lo = ROUTER_BASE + g_top * EXPERTS_PER_GROUP
    emask = (sub >= lo) & (sub < lo + EXPERTS_PER_GROUP)
    el = jnp.where(emask, logits_t, NEG_INF)
    m1 = jnp.max(el, axis=0, keepdims=True)
    i1 = jnp.min(jnp.where(emask & (el == m1), sub, big), axis=0, keepdims=True)
    emask2 = emask & (sub != i1)
    el2 = jnp.where(emask2, logits_t, NEG_INF)
    m2 = jnp.max(el2, axis=0, keepdims=True)
    i2 = jnp.min(jnp.where(emask2 & (el2 == m2), sub, big), axis=0, keepdims=True)
    e2 = jnp.exp(m2 - m1)
    w1 = g_weight / (1.0 + e2)
    w2 = g_weight * e2 / (1.0 + e2)
    return i1 - ROUTER_BASE, i2 - ROUTER_BASE, w1, w2


TOKEN_ROWS = D_MODEL // (2 * LANES)
WORD = jnp.uint32


def _to_token_tiles(ref, v, base=0):
    t, d = v.shape
    words = pltpu.pack_elementwise([v[:, :d // 2], v[:, d // 2:]], packed_dtype=BF16)
    for s in range(TOKEN_ROWS):
        ref[pl.ds(base + s, t, stride=TOKEN_ROWS), :] = words[:, s * LANES:(s + 1) * LANES]


def _token_words(ref, t, base=0):
    return jnp.concatenate([ref[pl.ds(base + s, t, stride=TOKEN_ROWS), :] for s in range(TOKEN_ROWS)], axis=-1)


def _from_token_tiles(ref, t, base=0):
    words = _token_words(ref, t, base)
    halves = [pltpu.unpack_elementwise(words, index=j, packed_dtype=BF16, unpacked_dtype=F32) for j in range(2)]
    return jnp.concatenate(halves, axis=-1)


MIX_CHUNK = 256


def _mix_kernel(na_ref, sw_ref, x_ref, bna_ref, bsw_ref, wo_ref, gpm_ref, ga_ref, gpf_ref, scf_ref, shf_ref,
                wr_ref, x1_ref, h2_ref, r_ref, rt_ref, cntc_ref, cntr_ref):
    first_step = (pl.program_id(0) == 0) & (pl.program_id(1) == 0)

    @pl.when(first_step)
    def _():
        cntc_ref[...] = jnp.zeros_like(cntc_ref)
        cntr_ref[...] = jnp.zeros_like(cntr_ref)

    t = MIX_CHUNK
    chunks = range(x_ref.shape[1] // t)
    row_slices = [slice(c * t, (c + 1) * t) for c in chunks]
    mixes, h2s, all_logits = [], [], []
    for rows in row_slices:
        def heads(ref):
            return jnp.concatenate([ref[0, j, rows, :] for j in range(PAIRS)], axis=-1).astype(F32)

        na = (_rms(heads(na_ref)) * bna_ref[...]).astype(BF16)
        sw = (_rms(heads(sw_ref)) * bsw_ref[...]).astype(BF16)
        mixes.append(jnp.dot(na, wo_ref[:NA_WIDTH, :], preferred_element_type=F32)
                     + jnp.dot(sw, wo_ref[NA_WIDTH:, :], preferred_element_type=F32))
    gate_gain = ga_ref[0] * gpm_ref[...]
    ffn_gain = gpf_ref[...] * (1.0 + scf_ref[0])
    for c, rows in zip(chunks, row_slices):
        x1 = x_ref[0, rows, :] + _rms(mixes[c]) * gate_gain
        x1_ref[0, rows, :] = x1
        h2 = _rms(x1) * ffn_gain + shf_ref[0]
        _to_token_tiles(h2_ref, h2, base=c * t * TOKEN_ROWS)
        h2s.append(h2)
    for c in chunks:
        h_hi, h_lo = _split2(h2s[c])
        both = (jnp.dot(h_hi, wr_ref[...], preferred_element_type=F32)
                + jnp.dot(h_lo, wr_ref[...], preferred_element_type=F32))
        all_logits.append(both + pltpu.roll(both, LANES - ROUTER_COLS, 1))
    for c, rows in zip(chunks, row_slices):
        e1, e2, w1, w2 = _route(jnp.transpose(all_logits[c])[:ROUTER_ROWS])
        fields = [None] * 4
        fields[R_E1], fields[R_E2], fields[R_W1], fields[R_W2] = e1.astype(F32), e2.astype(F32), w1, w2
        rt = jnp.concatenate(fields + [jnp.zeros((SUBLANES - 4, t), F32)], axis=0)
        rt_ref[:, rows] = rt
        r = jnp.transpose(jnp.concatenate([rt, jnp.zeros((LANES - SUBLANES, t), F32)], axis=0))
        r_ref[rows, :] = r
        sub = lax.broadcasted_iota(jnp.int32, (N_EXPERTS, t), 0)
        on_sub = ((sub == e1) | (sub == e2)).astype(F32)
        cntc_ref[...] += jnp.broadcast_to(jnp.sum(on_sub, axis=1, keepdims=True), cntc_ref.shape)
        lane = lax.broadcasted_iota(jnp.int32, r.shape, 1).astype(F32)
        on_lane = ((lane == r[:, R_E1:R_E1 + 1]) | (lane == r[:, R_E2:R_E2 + 1])).astype(F32)
        cntr_ref[...] += jnp.broadcast_to(jnp.sum(on_lane, axis=0, keepdims=True), cntr_ref.shape)


def _mix(na, sw, x, beta_na, beta_swa, w_out, g_post_mix, gate_a, g_pre_ffn, scale_f, shift_f, w_router3):
    b, s, d = x.shape
    tm = 1024
    nt = s // tm
    row = lambda bi, i: (bi, i, 0)
    const2 = lambda bi, i: (0, 0)
    per_b = lambda bi, i: (bi, 0, 0)
    return pl.pallas_call(
        _mix_kernel,
        out_shape=(jax.ShapeDtypeStruct((b, s, d), F32),
                   jax.ShapeDtypeStruct((b * s * TOKEN_ROWS, LANES), WORD),
                   jax.ShapeDtypeStruct((b * s, LANES), F32),
                   jax.ShapeDtypeStruct((SUBLANES, b * s), F32),
                   jax.ShapeDtypeStruct((N_EXPERTS, LANES), F32),
                   jax.ShapeDtypeStruct((SUBLANES, LANES), F32)),
        grid=(b, s // tm),
        in_specs=[pl.BlockSpec((1, PAIRS, tm, LANES), lambda bi, i: (bi, 0, i, 0)),
                  pl.BlockSpec((1, PAIRS, tm, LANES), lambda bi, i: (bi, 0, i, 0)),
                  pl.BlockSpec((1, tm, d), row),
                  pl.BlockSpec((1, NA_WIDTH), const2),
                  pl.BlockSpec((1, SWA_WIDTH), const2),
                  pl.BlockSpec((NA_WIDTH + SWA_WIDTH, d), const2),
                  pl.BlockSpec((1, d), const2),
                  pl.BlockSpec((1, 1, d), per_b),
                  pl.BlockSpec((1, d), const2),
                  pl.BlockSpec((1, 1, d), per_b),
                  pl.BlockSpec((1, 1, d), per_b),
                  pl.BlockSpec((d, LANES), const2)],
        out_specs=(pl.BlockSpec((1, tm, d), row),
                   pl.BlockSpec((tm * TOKEN_ROWS, LANES), lambda bi, i: (bi * nt + i, 0)),
                   pl.BlockSpec((tm, LANES), lambda bi, i: (bi * nt + i, 0)),
                   pl.BlockSpec((SUBLANES, tm), lambda bi, i: (0, bi * nt + i)),
                   pl.BlockSpec((N_EXPERTS, LANES), const2),
                   pl.BlockSpec((SUBLANES, LANES), const2)),
        compiler_params=_params("arbitrary", "arbitrary"),
        name="mix",
    )(na, sw, x, beta_na.reshape(1, -1), beta_swa.reshape(1, -1), w_out, g_post_mix.reshape(1, d), gate_a,
      g_pre_ffn.reshape(1, d), scale_f, shift_f, w_router3)


MOE_TILE = 2048
MOE_SUBTILE = 256
PLAN_T = 2048
I_TILE, I_EXPERT, I_LO, I_HI, I_FIRST, I_NEW, I_NEXT, I_ORDER = range(8)


def _plan_kernel(rt_ref, cntc_ref, cntr_ref, pos_ref, items_ref, start_ref, carry_ref, *, n_tiles, n_items):
    i = pl.program_id(0)
    sub = lax.broadcasted_iota(jnp.int32, (N_EXPERTS, LANES), 0)
    lane = lax.broadcasted_iota(jnp.int32, (N_EXPERTS, LANES), 1)

    @pl.when(i == 0)
    def _():
        c_col = cntc_ref[:, 0:1]
        c_row = cntr_ref[0:1, :]
        s_col = jnp.sum(jnp.where(lane < sub, c_row, 0.0), axis=1, keepdims=True)
        s_row = jnp.sum(jnp.where(sub < lane, c_col, 0.0), axis=0, keepdims=True)
        start_ref[...] = jnp.broadcast_to(s_col, start_ref.shape)
        carry_ref[...] = jnp.zeros_like(carry_ref)

        def tiles_of(s, c):
            first = jnp.floor(s * (1.0 / MOE_TILE))
            last = jnp.floor((s + c - 1.0) * (1.0 / MOE_TILE))
            return first, jnp.where(c > 0.0, last - first + 1.0, 0.0)

        f_col, n_col = tiles_of(s_col, c_col)
        _, n_row = tiles_of(s_row, c_row)
        i_col = jnp.sum(jnp.where(lane < sub, n_row, 0.0), axis=1, keepdims=True)
        total = jnp.sum(n_col, axis=0, keepdims=True)
        k = lax.broadcasted_iota(jnp.int32, (N_EXPERTS, n_items), 1).astype(F32)
        subk = lax.broadcasted_iota(jnp.int32, (N_EXPERTS, n_items), 0).astype(F32)
        ek = jnp.sum(jnp.where(i_col + n_col <= k, 1.0, 0.0), axis=0, keepdims=True)
        k0 = k[0:1]
        valid = k0 < total
        sel = subk == ek

        def pick(v):
            return jnp.sum(jnp.where(sel, v, 0.0), axis=0, keepdims=True)

        i_k, f_k, s_k, c_k = pick(i_col), pick(f_col), pick(s_col), pick(c_col)
        tile = f_k + (k0 - i_k)
        row0 = tile * MOE_TILE
        lo = jnp.maximum(s_k, row0) - row0
        hi = jnp.minimum(s_k + c_k, row0 + MOE_TILE) - row0
        present = n_col > 0.0
        last_expert = jnp.sum(jnp.where(i_col + n_col <= total - 1.0, 1.0, 0.0), axis=0, keepdims=True)
        nxt = jnp.min(jnp.where(present & (subk > ek), subk, float(N_EXPERTS)), axis=0, keepdims=True)
        order = jnp.sum(jnp.where(present & (subk < ek), 1.0, 0.0), axis=0, keepdims=True)
        rows = [jnp.where(valid, tile, n_tiles - 1.0), jnp.where(valid, ek, last_expert),
                jnp.where(valid, lo, 0.0), jnp.where(valid, hi, 0.0),
                jnp.where(valid & (lo == 0.0), 1.0, 0.0),
                jnp.where(valid & (k0 == i_k), 1.0, 0.0), jnp.where(valid, nxt, float(N_EXPERTS)),
                jnp.where(valid, order, 0.0)]
        assert len(rows) == SUBLANES
        items_ref[...] = jnp.concatenate(rows, axis=0).astype(jnp.int32)

    t = rt_ref.shape[1]
    e1 = rt_ref[R_E1:R_E1 + 1, :]
    e2 = rt_ref[R_E2:R_E2 + 1, :]
    sub_t = lax.broadcasted_iota(jnp.int32, (N_EXPERTS, t), 0).astype(F32)
    oh1 = sub_t == e1
    oh2 = sub_t == e2
    oh = (oh1 | oh2).astype(F32)
    before = (lax.broadcasted_iota(jnp.int32, (t, t), 0) < lax.broadcasted_iota(jnp.int32, (t, t), 1)).astype(BF16)
    rank = jnp.dot(oh.astype(BF16), before, preferred_element_type=F32)
    base = start_ref[:, 0:1] + carry_ref[:, 0:1] + rank
    pos1 = jnp.sum(jnp.where(oh1, base, 0.0), axis=0, keepdims=True)
    pos2 = jnp.sum(jnp.where(oh2, base, 0.0), axis=0, keepdims=True)
    carry_ref[...] += jnp.broadcast_to(jnp.sum(oh, axis=1, keepdims=True), carry_ref.shape)
    pos = jnp.concatenate([pos1, pos2] + [jnp.zeros_like(pos1)] * (SUBLANES - 2), axis=0)
    pos_ref[...] = pos.astype(jnp.int32)


def _plan(rt, cntc, cntr):
    n = rt.shape[1]
    n_tiles = 2 * n // MOE_TILE
    n_items = 2 * LANES
    assert n_tiles + N_EXPERTS <= n_items
    return pl.pallas_call(
        functools.partial(_plan_kernel, n_tiles=n_tiles, n_items=n_items),
        out_shape=(jax.ShapeDtypeStruct((SUBLANES, n), jnp.int32),
                   jax.ShapeDtypeStruct((SUBLANES, n_items), jnp.int32)),
        grid=(n // PLAN_T,),
        in_specs=[pl.BlockSpec((SUBLANES, PLAN_T), lambda i: (0, i)),
                  pl.BlockSpec((N_EXPERTS, LANES), lambda i: (0, 0)),
                  pl.BlockSpec((SUBLANES, LANES), lambda i: (0, 0))],
        out_specs=(pl.BlockSpec((SUBLANES, PLAN_T), lambda i: (0, i)),
                   pl.BlockSpec((SUBLANES, n_items), lambda i: (0, 0))),
        scratch_shapes=[pltpu.VMEM((N_EXPERTS, LANES), F32), pltpu.VMEM((N_EXPERTS, LANES), F32)],
        compiler_params=_params("arbitrary"),
        name="plan",
    )(rt, cntc, cntr)


DISPATCH_T = 2048


def _token_rows(ref, index):
    return ref.at[pl.ds(pl.multiple_of(index * TOKEN_ROWS, TOKEN_ROWS), TOKEN_ROWS), :]


def _dispatch_kernel(pos_ref, h_ref, xs_hbm, sem):
    def body(r, carry):
        src = _token_rows(h_ref, r)
        for k in range(2):
            pltpu.make_async_copy(src, _token_rows(xs_hbm, pos_ref[k, r]), sem.at[0]).start(priority=k)
        return carry

    lax.fori_loop(0, DISPATCH_T, body, 0, unroll=8)
    for k in range(2):
        pltpu.make_async_copy(h_ref, xs_hbm.at[pl.ds(0, DISPATCH_T * TOKEN_ROWS), :], sem.at[0]).wait()


def _dispatch(pos, h2):
    n = pos.shape[1]
    return pl.pallas_call(
        _dispatch_kernel,
        out_shape=jax.ShapeDtypeStruct((2 * n * TOKEN_ROWS, LANES), WORD),
        grid=(n // DISPATCH_T,),
        in_specs=[pl.BlockSpec((SUBLANES, DISPATCH_T), lambda i: (0, i), memory_space=pltpu.SMEM),
                  pl.BlockSpec((DISPATCH_T * TOKEN_ROWS, LANES), lambda i: (i, 0))],
        out_specs=pl.BlockSpec(memory_space=pl.ANY),
        scratch_shapes=[pltpu.SemaphoreType.DMA((1,))],
        compiler_params=_params("arbitrary"),
        name="dispatch",
    )(pos, h2)


def _expert_kernel(tile_ref, exp_ref, lo_ref, hi_ref, first_ref, new_ref, next_ref, order_ref,
                   xs_ref, wg_hbm, wu_hbm, wd_hbm, o_ref, wg_buf, wu_buf, wd_buf, sem):
    k = pl.program_id(0)
    lo = lo_ref[k]
    hi = hi_ref[k]
    slot = order_ref[k] % 2

    def weight_copies(expert, sl):
        return [pltpu.make_async_copy(src.at[expert], dst.at[sl], sem.at[sl])
                for src, dst in ((wg_hbm, wg_buf), (wu_hbm, wu_buf), (wd_hbm, wd_buf))]

    @pl.when(k == 0)
    def _():
        for copy in weight_copies(exp_ref[0], 0):
            copy.start()

    @pl.when(new_ref[k] == 1)
    def _():
        for copy in weight_copies(exp_ref[k], slot):
            copy.wait()

        @pl.when(next_ref[k] < N_EXPERTS)
        def _():
            for copy in weight_copies(next_ref[k], 1 - slot):
                copy.start()

    @pl.when(first_ref[k] == 1)
    def _():
        o_ref[...] = jnp.zeros_like(o_ref)

    for part in range(MOE_TILE // MOE_SUBTILE):
        first_row = part * MOE_SUBTILE
        base = first_row * TOKEN_ROWS

        @pl.when((hi > first_row) & (lo < first_row + MOE_SUBTILE))
        def _():
            x = _from_token_tiles(xs_ref, MOE_SUBTILE, base=base).astype(BF16)
            gate = jnp.dot(x, wg_buf[slot].astype(BF16), preferred_element_type=F32)
            up = jnp.dot(x, wu_buf[slot].astype(BF16), preferred_element_type=F32)
            he = (gate * jax.nn.sigmoid(gate) * up).astype(BF16)
            ye = jnp.dot(he, wd_buf[slot].astype(BF16), preferred_element_type=F32)
            row = first_row + lax.broadcasted_iota(jnp.int32, (MOE_SUBTILE, 1), 0)
            mine = (row >= lo) & (row < hi)
            _to_token_tiles(o_ref, jnp.where(mine, ye, _from_token_tiles(o_ref, MOE_SUBTILE, base=base)), base=base)


def _experts(items, xs, w_gate, w_up, w_down):
    n_rows = xs.shape[0] // TOKEN_ROWS
    n_items = n_rows // MOE_TILE + N_EXPERTS
    d = w_gate.shape[1]
    tile_map = lambda k, tile, *_: (tile[k], 0)
    grid_spec = pltpu.PrefetchScalarGridSpec(
        num_scalar_prefetch=SUBLANES,
        grid=(n_items,),
        in_specs=[pl.BlockSpec((MOE_TILE * TOKEN_ROWS, LANES), tile_map),
                  pl.BlockSpec(memory_space=pl.ANY),
                  pl.BlockSpec(memory_space=pl.ANY),
                  pl.BlockSpec(memory_space=pl.ANY)],
        out_specs=pl.BlockSpec((MOE_TILE * TOKEN_ROWS, LANES), tile_map),
        scratch_shapes=[pltpu.VMEM((2, d, EXPERT_FF), F32), pltpu.VMEM((2, d, EXPERT_FF), F32),
                        pltpu.VMEM((2, EXPERT_FF, d), F32), pltpu.SemaphoreType.DMA((2,))])
    return pl.pallas_call(
        _expert_kernel,
        out_shape=jax.ShapeDtypeStruct(xs.shape, WORD),
        grid_spec=grid_spec,
        compiler_params=_params("arbitrary"),
        name="experts",
    )(*(items[j, :n_items] for j in range(SUBLANES)), xs, w_gate, w_up, w_down)


COMBINE_T = 512


def _combine_kernel(pos_ref, posn_ref, ys_hbm, r_ref, x1_ref, gf_ref, gpost_ref, o_ref, buf, sem):
    i = pl.program_id(0)
    n = pl.num_programs(0)
    slot = i % 2
    slot_rows = COMBINE_T * TOKEN_ROWS

    def start_row(p_ref, sl, r):
        for k in range(2):
            dst = buf.at[sl, pl.ds(pl.multiple_of(k * slot_rows + r * TOKEN_ROWS, TOKEN_ROWS), TOKEN_ROWS), :]
            pltpu.make_async_copy(_token_rows(ys_hbm, p_ref[k, r]), dst, sem.at[sl]).start(priority=k)

    def wait_slot(sl):
        pltpu.make_async_copy(ys_hbm.at[pl.ds(0, 2 * slot_rows), :], buf.at[sl], sem.at[sl]).wait()

    def issue(p_ref, sl):
        def body(r, carry):
            start_row(p_ref, sl, r)
            return carry

        lax.fori_loop(0, COMBINE_T, body, 0, unroll=8)

    @pl.when(i == 0)
    def _():
        issue(pos_ref, 0)

    @pl.when(i + 1 < n)
    def _():
        issue(posn_ref, 1 - slot)

    wait_slot(slot)
    ya = _from_token_tiles(buf.at[slot], COMBINE_T)
    yb = _from_token_tiles(buf.at[slot], COMBINE_T, base=slot_rows)
    r = r_ref[...]
    y = r[:, R_W1:R_W1 + 1] * ya + r[:, R_W2:R_W2 + 1] * yb
    o_ref[0] = x1_ref[0] + gf_ref[0] * (_rms(y) * gpost_ref[...])


def _combine(pos, ys, r, x1, gate_f, g_post_ffn):
    b, s, d = x1.shape
    nt = s // COMBINE_T
    n_steps = b * nt
    row = lambda i: (i // nt, i % nt, 0)
    pos_spec = lambda f: pl.BlockSpec((SUBLANES, COMBINE_T), lambda i: (0, f(i)), memory_space=pltpu.SMEM)
    return pl.pallas_call(
        _combine_kernel,
        out_shape=jax.ShapeDtypeStruct((b, s, d), F32),
        grid=(n_steps,),
        in_specs=[pos_spec(lambda i: i),
                  pos_spec(lambda i: jnp.minimum(i + 1, n_steps - 1)),
                  pl.BlockSpec(memory_space=pl.ANY),
                  pl.BlockSpec((COMBINE_T, LANES), lambda i: (i, 0)),
                  pl.BlockSpec((1, COMBINE_T, d), row),
                  pl.BlockSpec((1, 1, d), lambda i: (i // nt, 0, 0)),
                  pl.BlockSpec((1, d), lambda i: (0, 0))],
        out_specs=pl.BlockSpec((1, COMBINE_T, d), row),
        scratch_shapes=[pltpu.VMEM((2, 2 * COMBINE_T * TOKEN_ROWS, LANES), WORD), pltpu.SemaphoreType.DMA((2,))],
        compiler_params=_params("arbitrary"),
        name="combine",
    )(pos, pos, ys, r, x1, gate_f, g_post_ffn.reshape(1, d))


def _rope_tables(s):
    half = HEAD_DIM // 2
    inv = ROPE_THETA ** (-np.arange(half, dtype=np.float64) * 2.0 / HEAD_DIM)
    ang = np.arange(s, dtype=np.float64)[:, None] * inv[None, :]
    cos = np.cos(ang)
    sin = np.sin(ang)
    cos = np.concatenate([cos, cos, cos, cos], axis=-1)
    sin_signed = np.concatenate([-sin, sin, -sin, sin], axis=-1)
    return jnp.asarray(cos, F32), jnp.asarray(sin_signed, F32)


def _router_weights(w_group_router, w_expert_router):
    d = w_group_router.shape[0]
    we = jnp.transpose(w_expert_router, (1, 0, 2)).reshape(d, N_EXPERTS)
    hi, lo = _split2(jnp.concatenate([w_group_router, we], axis=-1))
    return jnp.concatenate([hi, lo, jnp.zeros((d, LANES - 2 * ROUTER_COLS), BF16)], axis=-1)


def kernel(x, c, w_ada, b_ada, g_pre_mix, w_in, na_rpb, swa_sinks, beta_na, beta_swa, w_out, g_post_mix, g_pre_ffn,
           w_group_router, w_expert_router, w_gate, w_up, w_down, g_post_ffn):
    b, s, d = x.shape
    depth = w_ada.shape[0]
    cos, sin_signed = _rope_tables(s)
    for l in range(depth):
        mod = _adaln(c, w_ada[l], b_ada[l]).reshape(b, N_MOD, 1, d)
        shift_a, scale_a, gate_a, shift_f, scale_f, gate_f = (mod[:, k] for k in range(N_MOD))
        qkv, na_kg, na_vg = _qkv(x, g_pre_mix[l], scale_a, shift_a, w_in[l].astype(BF16), cos, sin_signed)
        na = _na(qkv, na_kg, na_vg, na_rpb[l])
        sw = _swa(qkv, swa_sinks[l])
        x1, h2, r, rt, cntc, cntr = _mix(na, sw, x, beta_na[l], beta_swa[l], w_out[l].astype(BF16), g_post_mix[l],
                                          gate_a, g_pre_ffn[l], scale_f, shift_f,
                                          _router_weights(w_group_router[l], w_expert_router[l]))
        pos, items = _plan(rt, cntc, cntr)
        xs = _dispatch(pos, h2)
        ys = _experts(items, xs, w_gate[l], w_up[l], w_down[l])
        x = _combine(pos, ys, r, x1, gate_f, g_post_ffn[l])
    return x
```

```python
import functools

import jax
import jax.numpy as jnp
import numpy as np
from jax import lax
from jax.experimental import pallas as pl
from jax.experimental.pallas import tpu as pltpu

D_MODEL = 1024
GRID_W = 64
HEAD_DIM = 64
NA_HEADS = 8
NA_KH = 8
NA_KW = 16
SWA_HEADS = 8
SWA_KV_HEADS = 2
SWA_WINDOW = 128
SWA_BLOCK = 128
ROPE_THETA = 10000.0
NA_WIDTH = NA_HEADS * HEAD_DIM
SWA_WIDTH = SWA_HEADS * HEAD_DIM
N_GROUPS = 4
EXPERTS_PER_GROUP = 8
N_EXPERTS = N_GROUPS * EXPERTS_PER_GROUP
EXPERT_FF = 256
N_MOD = 6
EPS = 1e-6
NEG_INF = -1e30

LANES = 128
SUBLANES = 8
R_E1, R_E2, R_W1, R_W2 = range(4)
PAIRS = NA_HEADS // 2
W_NQ, W_NK, W_NV, W_SQ, W_SKV = 0, 4, 8, 12, 16
COL_NQ, COL_SQ, COL_SK, COL_SV = 0, 4, 8, 10
QKV_TILES = 12
NA_GROUPS = GRID_W // NA_KW
NA_WIN = 2 * NA_KW
NA_WIN_START = tuple(min(max(NA_KW * g - NA_KW // 2, 0), GRID_W - NA_WIN) for g in range(NA_GROUPS))
ROUTER_BASE = N_GROUPS
VMEM_LIMIT = 56 * 1024 * 1024

F32 = jnp.float32
BF16 = jnp.bfloat16
LOG2E = 1.4426950408889634


def _rms(v):
    return v * lax.rsqrt(jnp.mean(v * v, axis=-1, keepdims=True) + EPS)


def _params(*sem):
    return pltpu.CompilerParams(dimension_semantics=sem, vmem_limit_bytes=VMEM_LIMIT)


def _adaln_kernel(c_ref, w_ref, b_ref, o_ref, *, batch):
    c = c_ref[...]
    a_t = jnp.transpose(c * jax.nn.sigmoid(c))
    w = w_ref[...]
    rows = [jnp.sum(a_t[:, bi:bi + 1] * w, axis=0, keepdims=True) for bi in range(batch)]
    rows.append(jnp.zeros((c.shape[0] - batch, w.shape[1]), F32))
    o_ref[...] = jnp.concatenate(rows, axis=0) + b_ref[...]


def _adaln(c, w_ada, b_ada):
    batch, d = c.shape
    n = w_ada.shape[1]
    tn = 1024
    b = SUBLANES
    assert batch < b
    c = jnp.pad(c, ((0, b - batch), (0, 0)))
    return pl.pallas_call(
        functools.partial(_adaln_kernel, batch=batch),
        out_shape=jax.ShapeDtypeStruct((b, n), F32),
        grid=(n // tn,),
        in_specs=[pl.BlockSpec((b, d), lambda j: (0, 0)),
                  pl.BlockSpec((d, tn), lambda j: (0, j)),
                  pl.BlockSpec((1, tn), lambda j: (0, j))],
        out_specs=pl.BlockSpec((b, tn), lambda j: (0, j)),
        compiler_params=_params("arbitrary"),
        name="adaln",
    )(c, w_ada, b_ada.reshape(1, n))[:batch]


def _rope(v, cos, sin_signed, first_half):
    rot = jnp.where(first_half, pltpu.roll(v, LANES - HEAD_DIM // 2, 1), pltpu.roll(v, HEAD_DIM // 2, 1))
    return v * cos + rot * sin_signed


QKV_CHUNK = 256


def _column_windows(v):
    grid_rows = v.shape[0] // GRID_W
    return [jnp.concatenate([v[r * GRID_W + w0:r * GRID_W + w0 + NA_WIN] for r in range(grid_rows)], axis=0)
            for w0 in NA_WIN_START]


def _qkv_kernel(x_ref, g_ref, sc_ref, sh_ref, w_ref, cos_ref, sin_ref, o_ref, kg_ref, vg_ref):
    scale = HEAD_DIM ** -0.5 * LOG2E
    lane = lax.broadcasted_iota(jnp.int32, (QKV_CHUNK, LANES), 1)
    first_half = (lane % HEAD_DIM) < HEAD_DIM // 2
    upper = lane >= HEAD_DIM

    def tile(v, j):
        return v[:, j * LANES:(j + 1) * LANES]

    for c in range(x_ref.shape[1] // QKV_CHUNK):
        rows = slice(c * QKV_CHUNK, (c + 1) * QKV_CHUNK)
        h = (_rms(x_ref[0, rows, :]) * g_ref[...]) * (1.0 + sc_ref[0]) + sh_ref[0]
        h = h.astype(BF16)
        cos = cos_ref[rows, :]
        sin = sin_ref[rows, :]

        def proj(col, width):
            return jnp.dot(h, w_ref[:, col * LANES:(col + width) * LANES], preferred_element_type=F32)

        nq, nk, nv, sq = proj(W_NQ, 4), proj(W_NK, 4), proj(W_NV, 4), proj(W_SQ, 4)
        win_rows = slice(c * QKV_CHUNK // 2, (c + 1) * QKV_CHUNK // 2)
        for j in range(PAIRS):
            o_ref[0, COL_NQ + j, rows, :] = (tile(nq, j) * scale).astype(BF16)
            o_ref[0, COL_SQ + j, rows, :] = (_rope(tile(sq, j), cos, sin, first_half) * scale).astype(BF16)
            for ref, val in ((kg_ref, tile(nk, j)), (vg_ref, tile(nv, j))):
                for g, win in enumerate(_column_windows(val)):
                    ref[0, j, g, win_rows, :] = win.astype(BF16)
        skv = proj(W_SKV, 2)
        k = _rope(tile(skv, 0), cos, sin, first_half)
        v = tile(skv, 1)
        for t, col in ((k, COL_SK), (v, COL_SV)):
            swapped = pltpu.roll(t, HEAD_DIM, 1)
            o_ref[0, col, rows, :] = jnp.where(upper, swapped, t).astype(BF16)
            o_ref[0, col + 1, rows, :] = jnp.where(upper, t, swapped).astype(BF16)


def _qkv(x, g, scale_a, shift_a, w_in, cos, sin):
    b, s, d = x.shape
    tm = 1024
    n_in = w_in.shape[1]
    windows = jax.ShapeDtypeStruct((b, PAIRS, NA_GROUPS, s // 2, LANES), BF16)
    windows_spec = pl.BlockSpec((1, PAIRS, NA_GROUPS, tm // 2, LANES), lambda bi, i: (bi, 0, 0, i, 0))
    return pl.pallas_call(
        _qkv_kernel,
        out_shape=(jax.ShapeDtypeStruct((b, QKV_TILES, s, LANES), BF16), windows, windows),
        grid=(b, s // tm),
        in_specs=[pl.BlockSpec((1, tm, d), lambda bi, i: (bi, i, 0)),
                  pl.BlockSpec((1, d), lambda bi, i: (0, 0)),
                  pl.BlockSpec((1, 1, d), lambda bi, i: (bi, 0, 0)),
                  pl.BlockSpec((1, 1, d), lambda bi, i: (bi, 0, 0)),
                  pl.BlockSpec((d, n_in), lambda bi, i: (0, 0)),
                  pl.BlockSpec((tm, LANES), lambda bi, i: (i, 0)),
                  pl.BlockSpec((tm, LANES), lambda bi, i: (i, 0))],
        out_specs=(pl.BlockSpec((1, QKV_TILES, tm, LANES), lambda bi, i: (bi, 0, i, 0)), windows_spec, windows_spec),
        compiler_params=_params("arbitrary", "arbitrary"),
        name="qkv",
    )(x, g.reshape(1, d), scale_a, shift_a, w_in, cos, sin)


NA_QROWS = 8
NA_KROWS = NA_QROWS + NA_KH
NA_BLOCKS_PER_STEP = 8
NA_INTERLEAVE = 8
NA_Q = NA_QROWS * NA_KW
NA_K = NA_KROWS * NA_WIN
NA_RPB_ROWS = 2 * NA_KH - 1
NA_RPB_COLS = 2 * NA_KW - 1


def _clamp(v, lo, hi):
    return min(max(v, lo), hi)


def _na_first_key_row(block, rows, clip):
    return clip(block * NA_QROWS - NA_KH // 2, 0, rows - NA_KROWS)


def _na_group_tables():
    def geometry(g):
        cols = [NA_KW * g + cc for cc in range(NA_KW)]
        return (NA_WIN_START[g] - NA_KW * g,) + tuple(_clamp(c - NA_KW // 2, 0, GRID_W - NA_KW) - c for c in cols)

    seen, table_of_group, representatives = {}, [], []
    for g in range(NA_GROUPS):
        key = geometry(g)
        if key not in seen:
            seen[key] = len(representatives)
            representatives.append(g)
        table_of_group.append(seen[key])
    return table_of_group, representatives


def _na_block_types(rows):
    def geometry(block):
        r = block * NA_QROWS
        a = _na_first_key_row(block, rows, _clamp)
        return (a - r,) + tuple(_clamp(r + j - NA_KH // 2, 0, rows - NA_KH) - r for j in range(NA_QROWS))

    n_blocks = rows // NA_QROWS
    interior = geometry(n_blocks // 2)
    lead = next(b for b in range(n_blocks) if geometry(b) == interior)
    trail = next(b for b in range(n_blocks) if geometry(n_blocks - 1 - b) == interior)
    assert all(geometry(b) == interior for b in range(lead, n_blocks - trail))
    return lead, trail


def _na_kernel(q_ref, k_ref, v_ref, bias_ref, o_ref, *, rows):
    upper = lax.broadcasted_iota(jnp.int32, (NA_Q, LANES), 1) >= HEAD_DIM
    key_upper = lax.broadcasted_iota(jnp.int32, (NA_K, LANES), 1) >= HEAD_DIM
    lead, trail = _na_block_types(rows)
    first_trailing = rows // NA_QROWS - trail
    n_types = lead + trail + 1
    table_of_group, _ = _na_group_tables()
    for c0 in range(0, NA_BLOCKS_PER_STEP, NA_INTERLEAVE):
        blocks = range(c0, c0 + NA_INTERLEAVE)
        starts, types, pieces = {}, {}, {}
        for c in blocks:
            block = pl.program_id(2) * NA_BLOCKS_PER_STEP + c
            a = _na_first_key_row(block, rows, jnp.clip)
            types[c] = jnp.where(block < lead, block,
                                 jnp.where(block >= first_trailing, block - first_trailing + lead + 1, lead))
            starts[c] = pl.multiple_of(a * NA_WIN, NA_WIN)
            for g in range(NA_GROUPS):
                pieces[c, g] = [slice((c * NA_QROWS + rr) * GRID_W + g * NA_KW,
                                      (c * NA_QROWS + rr) * GRID_W + (g + 1) * NA_KW) for rr in range(NA_QROWS)]
        chains = [(c, g, hh) for c in blocks for g in range(NA_GROUPS) for hh in range(2)]
        scores, probs, outs = {}, {}, {}
        for c, g, hh in chains:
            q = jnp.concatenate([q_ref[0, 0, rws, :] for rws in pieces[c, g]], axis=0)
            qm = jnp.where(upper if hh else ~upper, q, jnp.zeros_like(q))
            ks = k_ref[0, 0, g, pl.ds(starts[c], NA_K), :]
            s = lax.dot_general(qm, ks, (((1,), (1,)), ((), ())), preferred_element_type=F32)
            scores[c, g, hh] = s + bias_ref[hh, table_of_group[g] * n_types + types[c]]
        for chain in chains:
            s = scores[chain]
            probs[chain] = jnp.exp2(s - jnp.max(s, axis=-1, keepdims=True)).astype(BF16)
        for c, g, hh in chains:
            vs = v_ref[0, 0, g, pl.ds(starts[c], NA_K), :]
            v1 = jnp.where(key_upper if hh else ~key_upper, vs, jnp.ones_like(vs))
            o = jnp.dot(probs[c, g, hh], v1, preferred_element_type=F32)
            outs[c, g, hh] = o / o[:, (1 - hh) * HEAD_DIM:(1 - hh) * HEAD_DIM + 1]
        for c in blocks:
            for g in range(NA_GROUPS):
                out = jnp.where(upper, outs[c, g, 1], outs[c, g, 0]).astype(BF16)
                for rr, rws in enumerate(pieces[c, g]):
                    o_ref[0, 0, rws, :] = out[rr * NA_KW:(rr + 1) * NA_KW]


def _na_bias_kernel(rpb_ref, o_ref, *, rows):
    h = pl.program_id(0)
    cc = lax.broadcasted_iota(jnp.int32, (NA_KW, LANES), 0)
    lane = lax.broadcasted_iota(jnp.int32, (NA_KW, LANES), 1)
    w = lane % NA_WIN
    key_row_in_tile = lane // NA_WIN
    rows_per_tile = LANES // NA_WIN
    neg = jnp.full((NA_KW, LANES), NEG_INF, F32)
    base = h * NA_RPB_ROWS * NA_RPB_COLS
    lead, trail = _na_block_types(rows)
    n_blocks = rows // NA_QROWS
    type_blocks = list(range(lead + 1)) + list(range(n_blocks - trail, n_blocks))
    for table, g in enumerate(_na_group_tables()[1]):
        qc = g * NA_KW + cc
        kc = NA_WIN_START[g] + w
        c0 = jnp.clip(qc - NA_KW // 2, 0, GRID_W - NA_KW)
        in_cols = (kc >= c0) & (kc < c0 + NA_KW)
        dc = kc - qc + NA_KW - 1
        by_row_offset = []
        for d in range(NA_RPB_ROWS):
            acc = neg
            for dd in range(NA_RPB_COLS):
                acc = jnp.where(dc == dd, rpb_ref[base + d * NA_RPB_COLS + dd], acc)
            by_row_offset.append(jnp.where(in_cols, acc * LOG2E, NEG_INF))
        for ty, block in enumerate(type_blocks):
            r = block * NA_QROWS
            a = _na_first_key_row(block, rows, _clamp)
            for j in range(NA_QROWS):
                r0 = _clamp(r + j - NA_KH // 2, 0, rows - NA_KH)
                for t in range(NA_K // LANES):
                    tile = neg
                    for part in range(rows_per_tile):
                        i = t * rows_per_tile + part
                        if r0 <= a + i < r0 + NA_KH:
                            tile = jnp.where(key_row_in_tile == part, by_row_offset[a + i - (r + j) + NA_KH - 1], tile)
                    o_ref[0, table * len(type_blocks) + ty, j * NA_KW:(j + 1) * NA_KW, t * LANES:(t + 1) * LANES] = tile


def _na_bias(rpb, rows):
    n_tables = len(_na_group_tables()[1]) * (sum(_na_block_types(rows)) + 1)
    return pl.pallas_call(
        functools.partial(_na_bias_kernel, rows=rows),
        out_shape=jax.ShapeDtypeStruct((NA_HEADS, n_tables, NA_Q, NA_K), F32),
        grid=(NA_HEADS,),
        in_specs=[pl.BlockSpec(memory_space=pltpu.SMEM)],
        out_specs=pl.BlockSpec((1, n_tables, NA_Q, NA_K), lambda h: (h, 0, 0, 0)),
        compiler_params=_params("arbitrary"),
        name="na_bias",
    )(rpb.astype(F32).reshape(-1))


def _na(qkv, kg, vg, rpb):
    b, _, s, _ = qkv.shape
    rows = s // GRID_W
    assert rows % (NA_QROWS * NA_BLOCKS_PER_STEP) == 0
    tq = NA_BLOCKS_PER_STEP * NA_QROWS * GRID_W
    bias = _na_bias(rpb, rows)
    windows_spec = pl.BlockSpec((1, 1) + kg.shape[2:], lambda bi, p, i: (bi, p, 0, 0, 0))
    return pl.pallas_call(
        functools.partial(_na_kernel, rows=rows),
        out_shape=jax.ShapeDtypeStruct((b, PAIRS, s, LANES), BF16),
        grid=(b, PAIRS, s // tq),
        in_specs=[pl.BlockSpec((1, 1, tq, LANES), lambda bi, p, i: (bi, COL_NQ + p, i, 0)),
                  windows_spec,
                  windows_spec,
                  pl.BlockSpec((2,) + bias.shape[1:], lambda bi, p, i: (p, 0, 0, 0))],
        out_specs=pl.BlockSpec((1, 1, tq, LANES), lambda bi, p, i: (bi, p, i, 0)),
        compiler_params=_params("arbitrary", "arbitrary", "arbitrary"),
        name="na",
    )(qkv, kg, vg, bias)


SWA_KEYS = 3 * SWA_BLOCK
SWA_BLOCKS_PER_STEP = 32
SWA_INTERLEAVE = 2
SWA_GROUP = SWA_HEADS // SWA_KV_HEADS
assert COL_SQ % (PAIRS // SWA_KV_HEADS) == 0


def _swa_masks():
    v = np.arange(SWA_KEYS // SWA_BLOCK)[:, None, None]
    q = np.arange(SWA_BLOCK)[None, :, None]
    k = np.arange(SWA_KEYS)[None, None, :]
    return np.where(np.abs(k - v * SWA_BLOCK - q) <= SWA_WINDOW, 0.0, NEG_INF).astype(np.float32)


def _swa_kernel(sink_ref, mask_ref, q_ref, k_ref, v_ref, o_ref, *, seq):
    kv = pl.program_id(1)
    rows = SWA_GROUP * SWA_BLOCK
    lane = lax.broadcasted_iota(jnp.int32, (SWA_BLOCK, LANES), 1)
    upper = lane >= HEAD_DIM
    head = lax.broadcasted_iota(jnp.int32, (rows, 1), 0) // SWA_BLOCK
    sink = jnp.zeros((rows, 1), F32)
    for g in range(SWA_GROUP):
        sink = jnp.where(head == g, sink_ref[kv * SWA_GROUP + g], sink)
    sink = sink * LOG2E
    for j0 in range(0, SWA_BLOCKS_PER_STEP, SWA_INTERLEAVE):
        blocks = range(j0, j0 + SWA_INTERLEAVE)
        starts, scores, probs = {}, {}, {}
        for j in blocks:
            n = pl.program_id(2) * SWA_BLOCKS_PER_STEP + j
            start = pl.multiple_of(jnp.clip((n - 1) * SWA_BLOCK, 0, seq - SWA_KEYS), SWA_BLOCK)
            starts[j] = start
            ks = k_ref[0, 0, pl.ds(start, SWA_KEYS), :]
            qs = []
            for g in range(SWA_GROUP):
                q = q_ref[0, g // 2, j * SWA_BLOCK:(j + 1) * SWA_BLOCK, :]
                qs.append(jnp.where(upper if g % 2 else ~upper, q, jnp.zeros_like(q)))
            s = lax.dot_general(jnp.concatenate(qs, axis=0), ks, (((1,), (1,)), ((), ())), preferred_element_type=F32)
            mask = mask_ref[(n * SWA_BLOCK - start) // SWA_BLOCK]
            scores[j] = s + jnp.concatenate([mask] * SWA_GROUP, axis=0)
        for j in blocks:
            s = scores[j]
            m = jnp.maximum(jnp.max(s, axis=-1, keepdims=True), sink)
            e = jnp.exp2(s - m)
            probs[j] = (e.astype(BF16), jnp.sum(e, axis=-1, keepdims=True) + jnp.exp2(sink - m))
        for j in blocks:
            e, l = probs[j]
            vs = v_ref[0, 0, pl.ds(starts[j], SWA_KEYS), :]
            o = jnp.dot(e, vs, preferred_element_type=F32) / l
            for pair in range(SWA_GROUP // 2):
                even = o[(2 * pair) * SWA_BLOCK:(2 * pair + 1) * SWA_BLOCK]
                odd = o[(2 * pair + 1) * SWA_BLOCK:(2 * pair + 2) * SWA_BLOCK]
                o_ref[0, pair, j * SWA_BLOCK:(j + 1) * SWA_BLOCK, :] = jnp.where(upper, odd, even).astype(BF16)


def _swa(qkv, sinks):
    b, _, s, _ = qkv.shape
    pairs_per_kv = PAIRS // SWA_KV_HEADS
    tq = SWA_BLOCKS_PER_STEP * SWA_BLOCK
    masks = _swa_masks()
    return pl.pallas_call(
        functools.partial(_swa_kernel, seq=s),
        out_shape=jax.ShapeDtypeStruct((b, PAIRS, s, LANES), BF16),
        grid=(b, SWA_KV_HEADS, s // tq),
        in_specs=[pl.BlockSpec(memory_space=pltpu.SMEM),
                  pl.BlockSpec(masks.shape, lambda bi, kv, n: (0, 0, 0)),
                  pl.BlockSpec((1, pairs_per_kv, tq, LANES),
                               lambda bi, kv, n: (bi, COL_SQ // pairs_per_kv + kv, n, 0)),
                  pl.BlockSpec((1, 1, s, LANES), lambda bi, kv, n: (bi, COL_SK + kv, 0, 0)),
                  pl.BlockSpec((1, 1, s, LANES), lambda bi, kv, n: (bi, COL_SV + kv, 0, 0))],
        out_specs=pl.BlockSpec((1, pairs_per_kv, tq, LANES), lambda bi, kv, n: (bi, kv, n, 0)),
        compiler_params=_params("arbitrary", "arbitrary", "arbitrary"),
        name="swa",
    )(sinks, jnp.asarray(masks), qkv, qkv, qkv)


ROUTER_COLS = N_GROUPS + N_EXPERTS


def _split2(v):
    hi = v.astype(BF16)
    lo = (v - hi.astype(F32)).astype(BF16)
    return hi, lo


ROUTER_ROWS = 40


def _route(logits_t):
    sub = lax.broadcasted_iota(jnp.int32, logits_t.shape, 0)
    big = jnp.int32(LANES)
    gmask = sub < N_GROUPS
    gl = jnp.where(gmask, logits_t, NEG_INF)
    gmax = jnp.max(gl, axis=0, keepdims=True)
    g_top = jnp.min(jnp.where(gmask & (gl == gmax), sub, big), axis=0, keepdims=True)
    g_weight = 1.0 / jnp.sum(jnp.where(gmask, jnp.exp(gl - gmax), 0.0), axis=0, keepdims=True)
    lo = ROUTER_BASE + g_top * EXPERTS_PER_GROUP
    emask = (sub >= lo) & (sub < lo + EXPERTS_PER_GROUP)
    el = jnp.where(emask, logits_t, NEG_INF)
    m1 = jnp.max(el, axis=0, keepdims=True)
    i1 = jnp.min(jnp.where(emask & (el == m1), sub, big), axis=0, keepdims=True)
    emask2 = emask & (sub != i1)
    el2 = jnp.where(emask2, logits_t, NEG_INF)
    m2 = jnp.max(el2, axis=0, keepdims=True)
    i2 = jnp.min(jnp.where(emask2 & (el2 == m2), sub, big), axis=0, keepdims=True)
    e2 = jnp.exp(m2 - m1)
    w1 = g_weight / (1.0 + e2)
    w2 = g_weight * e2 / (1.0 + e2)
    return i1 - ROUTER_BASE, i2 - ROUTER_BASE, w1, w2


TOKEN_ROWS = D_MODEL // (2 * LANES)
WORD = jnp.uint32


def _to_token_tiles(ref, v, base=0):
    t, d = v.shape
    words = pltpu.pack_elementwise([v[:, :d // 2], v[:, d // 2:]], packed_dtype=BF16)
    for s in range(TOKEN_ROWS):
        ref[pl.ds(base + s, t, stride=TOKEN_ROWS), :] = words[:, s * LANES:(s + 1) * LANES]


def _token_words(ref, t, base=0):
    return jnp.concatenate([ref[pl.ds(base + s, t, stride=TOKEN_ROWS), :] for s in range(TOKEN_ROWS)], axis=-1)


def _from_token_tiles(ref, t, base=0):
    words = _token_words(ref, t, base)
    halves = [pltpu.unpack_elementwise(words, index=j, packed_dtype=BF16, unpacked_dtype=F32) for j in range(2)]
    return jnp.concatenate(halves, axis=-1)


MIX_CHUNK = 256


def _mix_kernel(na_ref, sw_ref, x_ref, bna_ref, bsw_ref, wo_ref, gpm_ref, ga_ref, gpf_ref, scf_ref, shf_ref,
                wr_ref, x1_ref, h2_ref, r_ref, rt_ref, cntc_ref, cntr_ref):
    first_step = (pl.program_id(0) == 0) & (pl.program_id(1) == 0)

    @pl.when(first_step)
    def _():
        cntc_ref[...] = jnp.zeros_like(cntc_ref)
        cntr_ref[...] = jnp.zeros_like(cntr_ref)

    t = MIX_CHUNK
    chunks = range(x_ref.shape[1] // t)
    row_slices = [slice(c * t, (c + 1) * t) for c in chunks]
    mixes, h2s, all_logits = [], [], []
    for rows in row_slices:
        def heads(ref):
            return jnp.concatenate([ref[0, j, rows, :] for j in range(PAIRS)], axis=-1).astype(F32)

        na = (_rms(heads(na_ref)) * bna_ref[...]).astype(BF16)
        sw = (_rms(heads(sw_ref)) * bsw_ref[...]).astype(BF16)
        mixes.append(jnp.dot(na, wo_ref[:NA_WIDTH, :], preferred_element_type=F32)
                     + jnp.dot(sw, wo_ref[NA_WIDTH:, :], preferred_element_type=F32))
    gate_gain = ga_ref[0] * gpm_ref[...]
    ffn_gain = gpf_ref[...] * (1.0 + scf_ref[0])
    for c, rows in zip(chunks, row_slices):
        x1 = x_ref[0, rows, :] + _rms(mixes[c]) * gate_gain
        x1_ref[0, rows, :] = x1
        h2 = _rms(x1) * ffn_gain + shf_ref[0]
        _to_token_tiles(h2_ref, h2, base=c * t * TOKEN_ROWS)
        h2s.append(h2)
    for c in chunks:
        h_hi, h_lo = _split2(h2s[c])
        both = (jnp.dot(h_hi, wr_ref[...], preferred_element_type=F32)
                + jnp.dot(h_lo, wr_ref[...], preferred_element_type=F32))
        all_logits.append(both + pltpu.roll(both, LANES - ROUTER_COLS, 1))
    for c, rows in zip(chunks, row_slices):
        e1, e2, w1, w2 = _route(jnp.transpose(all_logits[c])[:ROUTER_ROWS])
        fields = [None] * 4
        fields[R_E1], fields[R_E2], fields[R_W1], fields[R_W2] = e1.astype(F32), e2.astype(F32), w1, w2
        rt = jnp.concatenate(fields + [jnp.zeros((SUBLANES - 4, t), F32)], axis=0)
        rt_ref[:, rows] = rt
        r = jnp.transpose(jnp.concatenate([rt, jnp.zeros((LANES - SUBLANES, t), F32)], axis=0))
        r_ref[rows, :] = r
        sub = lax.broadcasted_iota(jnp.int32, (N_EXPERTS, t), 0)
        on_sub = ((sub == e1) | (sub == e2)).astype(F32)
        cntc_ref[...] += jnp.broadcast_to(jnp.sum(on_sub, axis=1, keepdims=True), cntc_ref.shape)
        lane = lax.broadcasted_iota(jnp.int32, r.shape, 1).astype(F32)
        on_lane = ((lane == r[:, R_E1:R_E1 + 1]) | (lane == r[:, R_E2:R_E2 + 1])).astype(F32)
        cntr_ref[...] += jnp.broadcast_to(jnp.sum(on_lane, axis=0, keepdims=True), cntr_ref.shape)


def _mix(na, sw, x, beta_na, beta_swa, w_out, g_post_mix, gate_a, g_pre_ffn, scale_f, shift_f, w_router3):
    b, s, d = x.shape
    tm = 1024
    nt = s // tm
    row = lambda bi, i: (bi, i, 0)
    const2 = lambda bi, i: (0, 0)
    per_b = lambda bi, i: (bi, 0, 0)
    return pl.pallas_call(
        _mix_kernel,
        out_shape=(jax.ShapeDtypeStruct((b, s, d), F32),
                   jax.ShapeDtypeStruct((b * s * TOKEN_ROWS, LANES), WORD),
                   jax.ShapeDtypeStruct((b * s, LANES), F32),
                   jax.ShapeDtypeStruct((SUBLANES, b * s), F32),
                   jax.ShapeDtypeStruct((N_EXPERTS, LANES), F32),
                   jax.ShapeDtypeStruct((SUBLANES, LANES), F32)),
        grid=(b, s // tm),
        in_specs=[pl.BlockSpec((1, PAIRS, tm, LANES), lambda bi, i: (bi, 0, i, 0)),
                  pl.BlockSpec((1, PAIRS, tm, LANES), lambda bi, i: (bi, 0, i, 0)),
                  pl.BlockSpec((1, tm, d), row),
                  pl.BlockSpec((1, NA_WIDTH), const2),
                  pl.BlockSpec((1, SWA_WIDTH), const2),
                  pl.BlockSpec((NA_WIDTH + SWA_WIDTH, d), const2),
                  pl.BlockSpec((1, d), const2),
                  pl.BlockSpec((1, 1, d), per_b),
                  pl.BlockSpec((1, d), const2),
                  pl.BlockSpec((1, 1, d), per_b),
                  pl.BlockSpec((1, 1, d), per_b),
                  pl.BlockSpec((d, LANES), const2)],
        out_specs=(pl.BlockSpec((1, tm, d), row),
                   pl.BlockSpec((tm * TOKEN_ROWS, LANES), lambda bi, i: (bi * nt + i, 0)),
                   pl.BlockSpec((tm, LANES), lambda bi, i: (bi * nt + i, 0)),
                   pl.BlockSpec((SUBLANES, tm), lambda bi, i: (0, bi * nt + i)),
                   pl.BlockSpec((N_EXPERTS, LANES), const2),
                   pl.BlockSpec((SUBLANES, LANES), const2)),
        compiler_params=_params("arbitrary", "arbitrary"),
        name="mix",
    )(na, sw, x, beta_na.reshape(1, -1), beta_swa.reshape(1, -1), w_out, g_post_mix.reshape(1, d), gate_a,
      g_pre_ffn.reshape(1, d), scale_f, shift_f, w_router3)


MOE_TILE = 1024
MOE_SUBTILE = 256
PLAN_T = 2048
I_TILE, I_EXPERT, I_LO, I_HI, I_FIRST, I_NEW, I_NEXT, I_ORDER = range(8)


def _plan_kernel(rt_ref, cntc_ref, cntr_ref, pos_ref, items_ref, start_ref, carry_ref, *, n_tiles, n_items):
    i = pl.program_id(0)
    sub = lax.broadcasted_iota(jnp.int32, (N_EXPERTS, LANES), 0)
    lane = lax.broadcasted_iota(jnp.int32, (N_EXPERTS, LANES), 1)

    @pl.when(i == 0)
    def _():
        c_col = cntc_ref[:, 0:1]
        c_row = cntr_ref[0:1, :]
        s_col = jnp.sum(jnp.where(lane < sub, c_row, 0.0), axis=1, keepdims=True)
        s_row = jnp.sum(jnp.where(sub < lane, c_col, 0.0), axis=0, keepdims=True)
        start_ref[...] = jnp.broadcast_to(s_col, start_ref.shape)
        carry_ref[...] = jnp.zeros_like(carry_ref)

        def tiles_of(s, c):
            first = jnp.floor(s * (1.0 / MOE_TILE))
            last = jnp.floor((s + c - 1.0) * (1.0 / MOE_TILE))
            return first, jnp.where(c > 0.0, last - first + 1.0, 0.0)

        f_col, n_col = tiles_of(s_col, c_col)
        _, n_row = tiles_of(s_row, c_row)
        i_col = jnp.sum(jnp.where(lane < sub, n_row, 0.0), axis=1, keepdims=True)
        total = jnp.sum(n_col, axis=0, keepdims=True)
        k = lax.broadcasted_iota(jnp.int32, (N_EXPERTS, n_items), 1).astype(F32)
        subk = lax.broadcasted_iota(jnp.int32, (N_EXPERTS, n_items), 0).astype(F32)
        ek = jnp.sum(jnp.where(i_col + n_col <= k, 1.0, 0.0), axis=0, keepdims=True)
        k0 = k[0:1]
        valid = k0 < total
        sel = subk == ek

        def pick(v):
            return jnp.sum(jnp.where(sel, v, 0.0), axis=0, keepdims=True)

        i_k, f_k, s_k, c_k = pick(i_col), pick(f_col), pick(s_col), pick(c_col)
        tile = f_k + (k0 - i_k)
        row0 = tile * MOE_TILE
        lo = jnp.maximum(s_k, row0) - row0
        hi = jnp.minimum(s_k + c_k, row0 + MOE_TILE) - row0
        present = n_col > 0.0
        last_expert = jnp.sum(jnp.where(i_col + n_col <= total - 1.0, 1.0, 0.0), axis=0, keepdims=True)
        nxt = jnp.min(jnp.where(present & (subk > ek), subk, float(N_EXPERTS)), axis=0, keepdims=True)
        order = jnp.sum(jnp.where(present & (subk < ek), 1.0, 0.0), axis=0, keepdims=True)
        rows = [jnp.where(valid, tile, n_tiles - 1.0), jnp.where(valid, ek, last_expert),
                jnp.where(valid, lo, 0.0), jnp.where(valid, hi, 0.0),
                jnp.where(valid & (lo == 0.0), 1.0, 0.0),
                jnp.where(valid & (k0 == i_k), 1.0, 0.0), jnp.where(valid, nxt, float(N_EXPERTS)),
                jnp.where(valid, order, 0.0)]
        assert len(rows) == SUBLANES
        items_ref[...] = jnp.concatenate(rows, axis=0).astype(jnp.int32)

    t = rt_ref.shape[1]
    e1 = rt_ref[R_E1:R_E1 + 1, :]
    e2 = rt_ref[R_E2:R_E2 + 1, :]
    sub_t = lax.broadcasted_iota(jnp.int32, (N_EXPERTS, t), 0).astype(F32)
    oh1 = sub_t == e1
    oh2 = sub_t == e2
    oh = (oh1 | oh2).astype(F32)
    before = (lax.broadcasted_iota(jnp.int32, (t, t), 0) < lax.broadcasted_iota(jnp.int32, (t, t), 1)).astype(BF16)
    rank = jnp.dot(oh.astype(BF16), before, preferred_element_type=F32)
    base = start_ref[:, 0:1] + carry_ref[:, 0:1] + rank
    pos1 = jnp.sum(jnp.where(oh1, base, 0.0), axis=0, keepdims=True)
    pos2 = jnp.sum(jnp.where(oh2, base, 0.0), axis=0, keepdims=True)
    carry_ref[...] += jnp.broadcast_to(jnp.sum(oh, axis=1, keepdims=True), carry_ref.shape)
    pos = jnp.concatenate([pos1, pos2] + [jnp.zeros_like(pos1)] * (SUBLANES - 2), axis=0)
    pos_ref[...] = pos.astype(jnp.int32)


def _plan(rt, cntc, cntr):
    n = rt.shape[1]
    n_tiles = 2 * n // MOE_TILE
    n_items = 2 * LANES
    assert n_tiles + N_EXPERTS <= n_items
    return pl.pallas_call(
        functools.partial(_plan_kernel, n_tiles=n_tiles, n_items=n_items),
        out_shape=(jax.ShapeDtypeStruct((SUBLANES, n), jnp.int32),
                   jax.ShapeDtypeStruct((SUBLANES, n_items), jnp.int32)),
        grid=(n // PLAN_T,),
        in_specs=[pl.BlockSpec((SUBLANES, PLAN_T), lambda i: (0, i)),
                  pl.BlockSpec((N_EXPERTS, LANES), lambda i: (0, 0)),
                  pl.BlockSpec((SUBLANES, LANES), lambda i: (0, 0))],
        out_specs=(pl.BlockSpec((SUBLANES, PLAN_T), lambda i: (0, i)),
                   pl.BlockSpec((SUBLANES, n_items), lambda i: (0, 0))),
        scratch_shapes=[pltpu.VMEM((N_EXPERTS, LANES), F32), pltpu.VMEM((N_EXPERTS, LANES), F32)],
        compiler_params=_params("arbitrary"),
        name="plan",
    )(rt, cntc, cntr)


DISPATCH_T = 2048


def _token_rows(ref, index):
    return ref.at[pl.ds(pl.multiple_of(index * TOKEN_ROWS, TOKEN_ROWS), TOKEN_ROWS), :]


def _dispatch_kernel(pos_ref, h_ref, xs_hbm, sem):
    def body(r, carry):
        src = _token_rows(h_ref, r)
        for k in range(2):
            pltpu.make_async_copy(src, _token_rows(xs_hbm, pos_ref[k, r]), sem.at[0]).start(priority=k)
        return carry

    lax.fori_loop(0, DISPATCH_T, body, 0, unroll=8)
    for k in range(2):
        pltpu.make_async_copy(h_ref, xs_hbm.at[pl.ds(0, DISPATCH_T * TOKEN_ROWS), :], sem.at[0]).wait()


def _dispatch(pos, h2):
    n = pos.shape[1]
    return pl.pallas_call(
        _dispatch_kernel,
        out_shape=jax.ShapeDtypeStruct((2 * n * TOKEN_ROWS, LANES), WORD),
        grid=(n // DISPATCH_T,),
        in_specs=[pl.BlockSpec((SUBLANES, DISPATCH_T), lambda i: (0, i), memory_space=pltpu.SMEM),
                  pl.BlockSpec((DISPATCH_T * TOKEN_ROWS, LANES), lambda i: (i, 0))],
        out_specs=pl.BlockSpec(memory_space=pl.ANY),
        scratch_shapes=[pltpu.SemaphoreType.DMA((1,))],
        compiler_params=_params("arbitrary"),
        name="dispatch",
    )(pos, h2)


def _expert_kernel(tile_ref, exp_ref, lo_ref, hi_ref, first_ref, new_ref, next_ref, order_ref,
                   xs_ref, wg_hbm, wu_hbm, wd_hbm, o_ref, wg_buf, wu_buf, wd_buf, sem):
    k = pl.program_id(0)
    lo = lo_ref[k]
    hi = hi_ref[k]
    slot = order_ref[k] % 2

    def weight_copies(expert, sl):
        return [pltpu.make_async_copy(src.at[expert], dst.at[sl], sem.at[sl])
                for src, dst in ((wg_hbm, wg_buf), (wu_hbm, wu_buf), (wd_hbm, wd_buf))]

    @pl.when(k == 0)
    def _():
        for copy in weight_copies(exp_ref[0], 0):
            copy.start()

    @pl.when(new_ref[k] == 1)
    def _():
        for copy in weight_copies(exp_ref[k], slot):
            copy.wait()

        @pl.when(next_ref[k] < N_EXPERTS)
        def _():
            for copy in weight_copies(next_ref[k], 1 - slot):
                copy.start()

    @pl.when(first_ref[k] == 1)
    def _():
        o_ref[...] = jnp.zeros_like(o_ref)

    for part in range(MOE_TILE // MOE_SUBTILE):
        first_row = part * MOE_SUBTILE
        base = first_row * TOKEN_ROWS

        @pl.when((hi > first_row) & (lo < first_row + MOE_SUBTILE))
        def _():
            x = _from_token_tiles(xs_ref, MOE_SUBTILE, base=base).astype(BF16)
            gate = jnp.dot(x, wg_buf[slot].astype(BF16), preferred_element_type=F32)
            up = jnp.dot(x, wu_buf[slot].astype(BF16), preferred_element_type=F32)
            he = (gate * jax.nn.sigmoid(gate) * up).astype(BF16)
            ye = jnp.dot(he, wd_buf[slot].astype(BF16), preferred_element_type=F32)
            row = first_row + lax.broadcasted_iota(jnp.int32, (MOE_SUBTILE, 1), 0)
            mine = (row >= lo) & (row < hi)
            _to_token_tiles(o_ref, jnp.where(mine, ye, _from_token_tiles(o_ref, MOE_SUBTILE, base=base)), base=base)


def _experts(items, xs, w_gate, w_up, w_down):
    n_rows = xs.shape[0] // TOKEN_ROWS
    n_items = n_rows // MOE_TILE + N_EXPERTS
    d = w_gate.shape[1]
    tile_map = lambda k, tile, *_: (tile[k], 0)
    grid_spec = pltpu.PrefetchScalarGridSpec(
        num_scalar_prefetch=SUBLANES,
        grid=(n_items,),
        in_specs=[pl.BlockSpec((MOE_TILE * TOKEN_ROWS, LANES), tile_map),
                  pl.BlockSpec(memory_space=pl.ANY),
                  pl.BlockSpec(memory_space=pl.ANY),
                  pl.BlockSpec(memory_space=pl.ANY)],
        out_specs=pl.BlockSpec((MOE_TILE * TOKEN_ROWS, LANES), tile_map),
        scratch_shapes=[pltpu.VMEM((2, d, EXPERT_FF), F32), pltpu.VMEM((2, d, EXPERT_FF), F32),
                        pltpu.VMEM((2, EXPERT_FF, d), F32), pltpu.SemaphoreType.DMA((2,))])
    return pl.pallas_call(
        _expert_kernel,
        out_shape=jax.ShapeDtypeStruct(xs.shape, WORD),
        grid_spec=grid_spec,
        compiler_params=_params("arbitrary"),
        name="experts",
    )(*(items[j, :n_items] for j in range(SUBLANES)), xs, w_gate, w_up, w_down)


COMBINE_T = 512


def _combine_kernel(pos_ref, posn_ref, ys_hbm, r_ref, x1_ref, gf_ref, gpost_ref, o_ref, buf, sem):
    i = pl.program_id(0)
    n = pl.num_programs(0)
    slot = i % 2
    slot_rows = COMBINE_T * TOKEN_ROWS

    def start_row(p_ref, sl, r):
        for k in range(2):
            dst = buf.at[sl, pl.ds(pl.multiple_of(k * slot_rows + r * TOKEN_ROWS, TOKEN_ROWS), TOKEN_ROWS), :]
            pltpu.make_async_copy(_token_rows(ys_hbm, p_ref[k, r]), dst, sem.at[sl]).start(priority=k)

    def wait_slot(sl):
        pltpu.make_async_copy(ys_hbm.at[pl.ds(0, 2 * slot_rows), :], buf.at[sl], sem.at[sl]).wait()

    def issue(p_ref, sl):
        def body(r, carry):
            start_row(p_ref, sl, r)
            return carry

        lax.fori_loop(0, COMBINE_T, body, 0, unroll=8)

    @pl.when(i == 0)
    def _():
        issue(pos_ref, 0)

    @pl.when(i + 1 < n)
    def _():
        issue(posn_ref, 1 - slot)

    wait_slot(slot)
    ya = _from_token_tiles(buf.at[slot], COMBINE_T)
    yb = _from_token_tiles(buf.at[slot], COMBINE_T, base=slot_rows)
    r = r_ref[...]
    y = r[:, R_W1:R_W1 + 1] * ya + r[:, R_W2:R_W2 + 1] * yb
    o_ref[0] = x1_ref[0] + gf_ref[0] * (_rms(y) * gpost_ref[...])


def _combine(pos, ys, r, x1, gate_f, g_post_ffn):
    b, s, d = x1.shape
    nt = s // COMBINE_T
    n_steps = b * nt
    row = lambda i: (i // nt, i % nt, 0)
    pos_spec = lambda f: pl.BlockSpec((SUBLANES, COMBINE_T), lambda i: (0, f(i)), memory_space=pltpu.SMEM)
    return pl.pallas_call(
        _combine_kernel,
        out_shape=jax.ShapeDtypeStruct((b, s, d), F32),
        grid=(n_steps,),
        in_specs=[pos_spec(lambda i: i),
                  pos_spec(lambda i: jnp.minimum(i + 1, n_steps - 1)),
                  pl.BlockSpec(memory_space=pl.ANY),
                  pl.BlockSpec((COMBINE_T, LANES), lambda i: (i, 0)),
                  pl.BlockSpec((1, COMBINE_T, d), row),
                  pl.BlockSpec((1, 1, d), lambda i: (i // nt, 0, 0)),
                  pl.BlockSpec((1, d), lambda i: (0, 0))],
        out_specs=pl.BlockSpec((1, COMBINE_T, d), row),
        scratch_shapes=[pltpu.VMEM((2, 2 * COMBINE_T * TOKEN_ROWS, LANES), WORD), pltpu.SemaphoreType.DMA((2,))],
        compiler_params=_params("arbitrary"),
        name="combine",
    )(pos, pos, ys, r, x1, gate_f, g_post_ffn.reshape(1, d))


def _rope_tables(s):
    half = HEAD_DIM // 2
    inv = ROPE_THETA ** (-np.arange(half, dtype=np.float64) * 2.0 / HEAD_DIM)
    ang = np.arange(s, dtype=np.float64)[:, None] * inv[None, :]
    cos = np.cos(ang)
    sin = np.sin(ang)
    cos = np.concatenate([cos, cos, cos, cos], axis=-1)
    sin_signed = np.concatenate([-sin, sin, -sin, sin], axis=-1)
    return jnp.asarray(cos, F32), jnp.asarray(sin_signed, F32)


def _router_weights(w_group_router, w_expert_router):
    d = w_group_router.shape[0]
    we = jnp.transpose(w_expert_router, (1, 0, 2)).reshape(d, N_EXPERTS)
    hi, lo = _split2(jnp.concatenate([w_group_router, we], axis=-1))
    return jnp.concatenate([hi, lo, jnp.zeros((d, LANES - 2 * ROUTER_COLS), BF16)], axis=-1)


def kernel(x, c, w_ada, b_ada, g_pre_mix, w_in, na_rpb, swa_sinks, beta_na, beta_swa, w_out, g_post_mix, g_pre_ffn,
           w_group_router, w_expert_router, w_gate, w_up, w_down, g_post_ffn):
    b, s, d = x.shape
    depth = w_ada.shape[0]
    cos, sin_signed = _rope_tables(s)
    for l in range(depth):
        mod = _adaln(c, w_ada[l], b_ada[l]).reshape(b, N_MOD, 1, d)
        shift_a, scale_a, gate_a, shift_f, scale_f, gate_f = (mod[:, k] for k in range(N_MOD))
        qkv, na_kg, na_vg = _qkv(x, g_pre_mix[l], scale_a, shift_a, w_in[l].astype(BF16), cos, sin_signed)
        na = _na(qkv, na_kg, na_vg, na_rpb[l])
        sw = _swa(qkv, swa_sinks[l])
        x1, h2, r, rt, cntc, cntr = _mix(na, sw, x, beta_na[l], beta_swa[l], w_out[l].astype(BF16), g_post_mix[l],
                                          gate_a, g_pre_ffn[l], scale_f, shift_f,
                                          _router_weights(w_group_router[l], w_expert_router[l]))
        pos, items = _plan(rt, cntc, cntr)
        xs = _dispatch(pos, h2)
        ys = _experts(items, xs, w_gate[l], w_up[l], w_down[l])
        x = _combine(pos, ys, r, x1, gate_f, g_post_ffn[l])
    return x
```

```python
import functools

import jax
import jax.numpy as jnp
import numpy as np
from jax import lax
from jax.experimental import pallas as pl
from jax.experimental.pallas import tpu as pltpu

D_MODEL = 1024
GRID_W = 64
HEAD_DIM = 64
NA_HEADS = 8
NA_KH = 8
NA_KW = 16
SWA_HEADS = 8
SWA_KV_HEADS = 2
SWA_WINDOW = 128
SWA_BLOCK = 128
ROPE_THETA = 10000.0
NA_WIDTH = NA_HEADS * HEAD_DIM
SWA_WIDTH = SWA_HEADS * HEAD_DIM
N_GROUPS = 4
EXPERTS_PER_GROUP = 8
N_EXPERTS = N_GROUPS * EXPERTS_PER_GROUP
EXPERT_FF = 256
N_MOD = 6
EPS = 1e-6
NEG_INF = -1e30

LANES = 128
SUBLANES = 8
R_E1, R_E2, R_W1, R_W2 = range(4)
PAIRS = NA_HEADS // 2
W_NQ, W_NK, W_NV, W_SQ, W_SKV = 0, 4, 8, 12, 16
COL_NQ, COL_SQ, COL_SK, COL_SV = 0, 4, 8, 10
QKV_TILES = 12
NA_GROUPS = GRID_W // NA_KW
NA_WIN = 2 * NA_KW
NA_WIN_START = tuple(min(max(NA_KW * g - NA_KW // 2, 0), GRID_W - NA_WIN) for g in range(NA_GROUPS))
ROUTER_BASE = N_GROUPS
VMEM_LIMIT = 56 * 1024 * 1024

F32 = jnp.float32
BF16 = jnp.bfloat16
LOG2E = 1.4426950408889634


def _rms(v):
    return v * lax.rsqrt(jnp.mean(v * v, axis=-1, keepdims=True) + EPS)


def _params(*sem):
    return pltpu.CompilerParams(dimension_semantics=sem, vmem_limit_bytes=VMEM_LIMIT)


def _adaln_kernel(c_ref, w_ref, b_ref, o_ref, *, batch):
    c = c_ref[...]
    a_t = jnp.transpose(c * jax.nn.sigmoid(c))
    w = w_ref[...]
    rows = [jnp.sum(a_t[:, bi:bi + 1] * w, axis=0, keepdims=True) for bi in range(batch)]
    rows.append(jnp.zeros((c.shape[0] - batch, w.shape[1]), F32))
    o_ref[...] = jnp.concatenate(rows, axis=0) + b_ref[...]


def _adaln(c, w_ada, b_ada):
    batch, d = c.shape
    n = w_ada.shape[1]
    tn = 1024
    b = SUBLANES
    assert batch < b
    c = jnp.pad(c, ((0, b - batch), (0, 0)))
    return pl.pallas_call(
        functools.partial(_adaln_kernel, batch=batch),
        out_shape=jax.ShapeDtypeStruct((b, n), F32),
        grid=(n // tn,),
        in_specs=[pl.BlockSpec((b, d), lambda j: (0, 0)),
                  pl.BlockSpec((d, tn), lambda j: (0, j)),
                  pl.BlockSpec((1, tn), lambda j: (0, j))],
        out_specs=pl.BlockSpec((b, tn), lambda j: (0, j)),
        compiler_params=_params("arbitrary"),
        name="adaln",
    )(c, w_ada, b_ada.reshape(1, n))[:batch]


def _rope(v, cos, sin_signed, first_half):
    rot = jnp.where(first_half, pltpu.roll(v, LANES - HEAD_DIM // 2, 1), pltpu.roll(v, HEAD_DIM // 2, 1))
    return v * cos + rot * sin_signed


QKV_CHUNK = 256


def _column_windows(v):
    grid_rows = v.shape[0] // GRID_W
    return [jnp.concatenate([v[r * GRID_W + w0:r * GRID_W + w0 + NA_WIN] for r in range(grid_rows)], axis=0)
            for w0 in NA_WIN_START]


def _qkv_kernel(x_ref, g_ref, sc_ref, sh_ref, w32_ref, cos_ref, sin_ref, o_ref, kg_ref, vg_ref, w_ref):
    w_ref[...] = w32_ref[...].astype(BF16)
    scale = HEAD_DIM ** -0.5 * LOG2E
    lane = lax.broadcasted_iota(jnp.int32, (QKV_CHUNK, LANES), 1)
    first_half = (lane % HEAD_DIM) < HEAD_DIM // 2
    upper = lane >= HEAD_DIM

    def tile(v, j):
        return v[:, j * LANES:(j + 1) * LANES]

    for c in range(x_ref.shape[1] // QKV_CHUNK):
        rows = slice(c * QKV_CHUNK, (c + 1) * QKV_CHUNK)
        h = (_rms(x_ref[0, rows, :]) * g_ref[...]) * (1.0 + sc_ref[0]) + sh_ref[0]
        h = h.astype(BF16)
        cos = cos_ref[rows, :]
        sin = sin_ref[rows, :]

        def proj(col, width):
            return jnp.dot(h, w_ref[:, col * LANES:(col + width) * LANES], preferred_element_type=F32)

        nq, nk, nv, sq = proj(W_NQ, 4), proj(W_NK, 4), proj(W_NV, 4), proj(W_SQ, 4)
        win_rows = slice(c * QKV_CHUNK // 2, (c + 1) * QKV_CHUNK // 2)
        for j in range(PAIRS):
            o_ref[0, COL_NQ + j, rows, :] = (tile(nq, j) * scale).astype(BF16)
            o_ref[0, COL_SQ + j, rows, :] = (_rope(tile(sq, j), cos, sin, first_half) * scale).astype(BF16)
            for ref, val in ((kg_ref, tile(nk, j)), (vg_ref, tile(nv, j))):
                for g, win in enumerate(_column_windows(val)):
                    ref[0, j, g, win_rows, :] = win.astype(BF16)
        skv = proj(W_SKV, 2)
        k = _rope(tile(skv, 0), cos, sin, first_half)
        v = tile(skv, 1)
        for t, col in ((k, COL_SK), (v, COL_SV)):
            swapped = pltpu.roll(t, HEAD_DIM, 1)
            o_ref[0, col, rows, :] = jnp.where(upper, swapped, t).astype(BF16)
            o_ref[0, col + 1, rows, :] = jnp.where(upper, t, swapped).astype(BF16)


def _qkv(x, g, scale_a, shift_a, w_in, cos, sin):
    b, s, d = x.shape
    tm = 1024
    n_in = w_in.shape[1]
    windows = jax.ShapeDtypeStruct((b, PAIRS, NA_GROUPS, s // 2, LANES), BF16)
    windows_spec = pl.BlockSpec((1, PAIRS, NA_GROUPS, tm // 2, LANES), lambda bi, i: (bi, 0, 0, i, 0))
    return pl.pallas_call(
        _qkv_kernel,
        out_shape=(jax.ShapeDtypeStruct((b, QKV_TILES, s, LANES), BF16), windows, windows),
        grid=(b, s // tm),
        in_specs=[pl.BlockSpec((1, tm, d), lambda bi, i: (bi, i, 0)),
                  pl.BlockSpec((1, d), lambda bi, i: (0, 0)),
                  pl.BlockSpec((1, 1, d), lambda bi, i: (bi, 0, 0)),
                  pl.BlockSpec((1, 1, d), lambda bi, i: (bi, 0, 0)),
                  pl.BlockSpec((d, n_in), lambda bi, i: (0, 0)),
                  pl.BlockSpec((tm, LANES), lambda bi, i: (i, 0)),
                  pl.BlockSpec((tm, LANES), lambda bi, i: (i, 0))],
        out_specs=(pl.BlockSpec((1, QKV_TILES, tm, LANES), lambda bi, i: (bi, 0, i, 0)), windows_spec, windows_spec),
        scratch_shapes=[pltpu.VMEM((d, n_in), BF16)],
        compiler_params=_params("arbitrary", "arbitrary"),
        name="qkv",
    )(x, g.reshape(1, d), scale_a, shift_a, w_in, cos, sin)


NA_QROWS = 8
NA_KROWS = NA_QROWS + NA_KH
NA_BLOCKS_PER_STEP = 8
NA_INTERLEAVE = 8
NA_Q = NA_QROWS * NA_KW
NA_K = NA_KROWS * NA_WIN
NA_RPB_ROWS = 2 * NA_KH - 1
NA_RPB_COLS = 2 * NA_KW - 1


def _clamp(v, lo, hi):
    return min(max(v, lo), hi)


def _na_first_key_row(block, rows, clip):
    return clip(block * NA_QROWS - NA_KH // 2, 0, rows - NA_KROWS)


def _na_group_tables():
    def geometry(g):
        cols = [NA_KW * g + cc for cc in range(NA_KW)]
        return (NA_WIN_START[g] - NA_KW * g,) + tuple(_clamp(c - NA_KW // 2, 0, GRID_W - NA_KW) - c for c in cols)

    seen, table_of_group, representatives = {}, [], []
    for g in range(NA_GROUPS):
        key = geometry(g)
        if key not in seen:
            seen[key] = len(representatives)
            representatives.append(g)
        table_of_group.append(seen[key])
    return table_of_group, representatives


def _na_block_types(rows):
    def geometry(block):
        r = block * NA_QROWS
        a = _na_first_key_row(block, rows, _clamp)
        return (a - r,) + tuple(_clamp(r + j - NA_KH // 2, 0, rows - NA_KH) - r for j in range(NA_QROWS))

    n_blocks = rows // NA_QROWS
    interior = geometry(n_blocks // 2)
    lead = next(b for b in range(n_blocks) if geometry(b) == interior)
    trail = next(b for b in range(n_blocks) if geometry(n_blocks - 1 - b) == interior)
    assert all(geometry(b) == interior for b in range(lead, n_blocks - trail))
    return lead, trail


def _na_kernel(q_ref, k_ref, v_ref, bias_ref, o_ref, *, rows):
    upper = lax.broadcasted_iota(jnp.int32, (NA_Q, LANES), 1) >= HEAD_DIM
    key_upper = lax.broadcasted_iota(jnp.int32, (NA_K, LANES), 1) >= HEAD_DIM
    lead, trail = _na_block_types(rows)
    first_trailing = rows // NA_QROWS - trail
    n_types = lead + trail + 1
    table_of_group, _ = _na_group_tables()
    for c0 in range(0, NA_BLOCKS_PER_STEP, NA_INTERLEAVE):
        blocks = range(c0, c0 + NA_INTERLEAVE)
        starts, types, pieces = {}, {}, {}
        for c in blocks:
            block = pl.program_id(2) * NA_BLOCKS_PER_STEP + c
            a = _na_first_key_row(block, rows, jnp.clip)
            types[c] = jnp.where(block < lead, block,
                                 jnp.where(block >= first_trailing, block - first_trailing + lead + 1, lead))
            starts[c] = pl.multiple_of(a * NA_WIN, NA_WIN)
            for g in range(NA_GROUPS):
                pieces[c, g] = [slice((c * NA_QROWS + rr) * GRID_W + g * NA_KW,
                                      (c * NA_QROWS + rr) * GRID_W + (g + 1) * NA_KW) for rr in range(NA_QROWS)]
        chains = [(c, g, hh) for c in blocks for g in range(NA_GROUPS) for hh in range(2)]
        scores, probs, outs = {}, {}, {}
        for c, g, hh in chains:
            q = jnp.concatenate([q_ref[0, 0, rws, :] for rws in pieces[c, g]], axis=0)
            qm = jnp.where(upper if hh else ~upper, q, jnp.zeros_like(q))
            ks = k_ref[0, 0, g, pl.ds(starts[c], NA_K), :]
            s = lax.dot_general(qm, ks, (((1,), (1,)), ((), ())), preferred_element_type=F32)
            scores[c, g, hh] = s + bias_ref[hh, table_of_group[g] * n_types + types[c]]
        for chain in chains:
            s = scores[chain]
            probs[chain] = jnp.exp2(s - jnp.max(s, axis=-1, keepdims=True)).astype(BF16)
        for c, g, hh in chains:
            vs = v_ref[0, 0, g, pl.ds(starts[c], NA_K), :]
            v1 = jnp.where(key_upper if hh else ~key_upper, vs, jnp.ones_like(vs))
            o = jnp.dot(probs[c, g, hh], v1, preferred_element_type=F32)
            outs[c, g, hh] = o / o[:, (1 - hh) * HEAD_DIM:(1 - hh) * HEAD_DIM + 1]
        for c in blocks:
            for g in range(NA_GROUPS):
                out = jnp.where(upper, outs[c, g, 1], outs[c, g, 0]).astype(BF16)
                for rr, rws in enumerate(pieces[c, g]):
                    o_ref[0, 0, rws, :] = out[rr * NA_KW:(rr + 1) * NA_KW]


def _na_bias_kernel(rpb_ref, o_ref, *, rows):
    h = pl.program_id(0)
    cc = lax.broadcasted_iota(jnp.int32, (NA_KW, LANES), 0)
    lane = lax.broadcasted_iota(jnp.int32, (NA_KW, LANES), 1)
    w = lane % NA_WIN
    key_row_in_tile = lane // NA_WIN
    rows_per_tile = LANES // NA_WIN
    neg = jnp.full((NA_KW, LANES), NEG_INF, F32)
    base = h * NA_RPB_ROWS * NA_RPB_COLS
    lead, trail = _na_block_types(rows)
    n_blocks = rows // NA_QROWS
    type_blocks = list(range(lead + 1)) + list(range(n_blocks - trail, n_blocks))
    for table, g in enumerate(_na_group_tables()[1]):
        qc = g * NA_KW + cc
        kc = NA_WIN_START[g] + w
        c0 = jnp.clip(qc - NA_KW // 2, 0, GRID_W - NA_KW)
        in_cols = (kc >= c0) & (kc < c0 + NA_KW)
        dc = kc - qc + NA_KW - 1
        by_row_offset = []
        for d in range(NA_RPB_ROWS):
            acc = neg
            for dd in range(NA_RPB_COLS):
                acc = jnp.where(dc == dd, rpb_ref[base + d * NA_RPB_COLS + dd], acc)
            by_row_offset.append(jnp.where(in_cols, acc * LOG2E, NEG_INF))
        for ty, block in enumerate(type_blocks):
            r = block * NA_QROWS
            a = _na_first_key_row(block, rows, _clamp)
            for j in range(NA_QROWS):
                r0 = _clamp(r + j - NA_KH // 2, 0, rows - NA_KH)
                for t in range(NA_K // LANES):
                    tile = neg
                    for part in range(rows_per_tile):
                        i = t * rows_per_tile + part
                        if r0 <= a + i < r0 + NA_KH:
                            tile = jnp.where(key_row_in_tile == part, by_row_offset[a + i - (r + j) + NA_KH - 1], tile)
                    o_ref[0, table * len(type_blocks) + ty, j * NA_KW:(j + 1) * NA_KW, t * LANES:(t + 1) * LANES] = tile


def _na_bias(rpb, rows):
    n_tables = len(_na_group_tables()[1]) * (sum(_na_block_types(rows)) + 1)
    return pl.pallas_call(
        functools.partial(_na_bias_kernel, rows=rows),
        out_shape=jax.ShapeDtypeStruct((NA_HEADS, n_tables, NA_Q, NA_K), F32),
        grid=(NA_HEADS,),
        in_specs=[pl.BlockSpec(memory_space=pltpu.SMEM)],
        out_specs=pl.BlockSpec((1, n_tables, NA_Q, NA_K), lambda h: (h, 0, 0, 0)),
        compiler_params=_params("arbitrary"),
        name="na_bias",
    )(rpb.astype(F32).reshape(-1))


def _na(qkv, kg, vg, rpb):
    b, _, s, _ = qkv.shape
    rows = s // GRID_W
    assert rows % (NA_QROWS * NA_BLOCKS_PER_STEP) == 0
    tq = NA_BLOCKS_PER_STEP * NA_QROWS * GRID_W
    bias = _na_bias(rpb, rows)
    windows_spec = pl.BlockSpec((1, 1) + kg.shape[2:], lambda bi, p, i: (bi, p, 0, 0, 0))
    return pl.pallas_call(
        functools.partial(_na_kernel, rows=rows),
        out_shape=jax.ShapeDtypeStruct((b, PAIRS, s, LANES), BF16),
        grid=(b, PAIRS, s // tq),
        in_specs=[pl.BlockSpec((1, 1, tq, LANES), lambda bi, p, i: (bi, COL_NQ + p, i, 0)),
                  windows_spec,
                  windows_spec,
                  pl.BlockSpec((2,) + bias.shape[1:], lambda bi, p, i: (p, 0, 0, 0))],
        out_specs=pl.BlockSpec((1, 1, tq, LANES), lambda bi, p, i: (bi, p, i, 0)),
        compiler_params=_params("arbitrary", "arbitrary", "arbitrary"),
        name="na",
    )(qkv, kg, vg, bias)


SWA_KEYS = 3 * SWA_BLOCK
SWA_BLOCKS_PER_STEP = 32
SWA_INTERLEAVE = 2
SWA_GROUP = SWA_HEADS // SWA_KV_HEADS
assert COL_SQ % (PAIRS // SWA_KV_HEADS) == 0


def _swa_masks():
    v = np.arange(SWA_KEYS // SWA_BLOCK)[:, None, None]
    q = np.arange(SWA_BLOCK)[None, :, None]
    k = np.arange(SWA_KEYS)[None, None, :]
    return np.where(np.abs(k - v * SWA_BLOCK - q) <= SWA_WINDOW, 0.0, NEG_INF).astype(np.float32)


def _swa_kernel(sink_ref, mask_ref, q_ref, k_ref, v_ref, o_ref, *, seq):
    kv = pl.program_id(1)
    rows = SWA_GROUP * SWA_BLOCK
    lane = lax.broadcasted_iota(jnp.int32, (SWA_BLOCK, LANES), 1)
    upper = lane >= HEAD_DIM
    head = lax.broadcasted_iota(jnp.int32, (rows, 1), 0) // SWA_BLOCK
    sink = jnp.zeros((rows, 1), F32)
    for g in range(SWA_GROUP):
        sink = jnp.where(head == g, sink_ref[kv * SWA_GROUP + g], sink)
    sink = sink * LOG2E
    for j0 in range(0, SWA_BLOCKS_PER_STEP, SWA_INTERLEAVE):
        blocks = range(j0, j0 + SWA_INTERLEAVE)
        starts, scores, probs = {}, {}, {}
        for j in blocks:
            n = pl.program_id(2) * SWA_BLOCKS_PER_STEP + j
            start = pl.multiple_of(jnp.clip((n - 1) * SWA_BLOCK, 0, seq - SWA_KEYS), SWA_BLOCK)
            starts[j] = start
            ks = k_ref[0, 0, pl.ds(start, SWA_KEYS), :]
            qs = []
            for g in range(SWA_GROUP):
                q = q_ref[0, g // 2, j * SWA_BLOCK:(j + 1) * SWA_BLOCK, :]
                qs.append(jnp.where(upper if g % 2 else ~upper, q, jnp.zeros_like(q)))
            s = lax.dot_general(jnp.concatenate(qs, axis=0), ks, (((1,), (1,)), ((), ())), preferred_element_type=F32)
            mask = mask_ref[(n * SWA_BLOCK - start) // SWA_BLOCK]
            scores[j] = s + jnp.concatenate([mask] * SWA_GROUP, axis=0)
        for j in blocks:
            s = scores[j]
            m = jnp.maximum(jnp.max(s, axis=-1, keepdims=True), sink)
            e = jnp.exp2(s - m)
            probs[j] = (e.astype(BF16), jnp.sum(e, axis=-1, keepdims=True) + jnp.exp2(sink - m))
        for j in blocks:
            e, l = probs[j]
            vs = v_ref[0, 0, pl.ds(starts[j], SWA_KEYS), :]
            o = jnp.dot(e, vs, preferred_element_type=F32) / l
            for pair in range(SWA_GROUP // 2):
                even = o[(2 * pair) * SWA_BLOCK:(2 * pair + 1) * SWA_BLOCK]
                odd = o[(2 * pair + 1) * SWA_BLOCK:(2 * pair + 2) * SWA_BLOCK]
                o_ref[0, pair, j * SWA_BLOCK:(j + 1) * SWA_BLOCK, :] = jnp.where(upper, odd, even).astype(BF16)


def _swa(qkv, sinks):
    b, _, s, _ = qkv.shape
    pairs_per_kv = PAIRS // SWA_KV_HEADS
    tq = SWA_BLOCKS_PER_STEP * SWA_BLOCK
    masks = _swa_masks()
    return pl.pallas_call(
        functools.partial(_swa_kernel, seq=s),
        out_shape=jax.ShapeDtypeStruct((b, PAIRS, s, LANES), BF16),
        grid=(b, SWA_KV_HEADS, s // tq),
        in_specs=[pl.BlockSpec(memory_space=pltpu.SMEM),
                  pl.BlockSpec(masks.shape, lambda bi, kv, n: (0, 0, 0)),
                  pl.BlockSpec((1, pairs_per_kv, tq, LANES),
                               lambda bi, kv, n: (bi, COL_SQ // pairs_per_kv + kv, n, 0)),
                  pl.BlockSpec((1, 1, s, LANES), lambda bi, kv, n: (bi, COL_SK + kv, 0, 0)),
                  pl.BlockSpec((1, 1, s, LANES), lambda bi, kv, n: (bi, COL_SV + kv, 0, 0))],
        out_specs=pl.BlockSpec((1, pairs_per_kv, tq, LANES), lambda bi, kv, n: (bi, kv, n, 0)),
        compiler_params=_params("arbitrary", "arbitrary", "arbitrary"),
        name="swa",
    )(sinks, jnp.asarray(masks), qkv, qkv, qkv)


ROUTER_COLS = N_GROUPS + N_EXPERTS


def _split2(v):
    hi = v.astype(BF16)
    lo = (v - hi.astype(F32)).astype(BF16)
    return hi, lo


ROUTER_ROWS = 40


def _route(logits_t):
    sub = lax.broadcasted_iota(jnp.int32, logits_t.shape, 0)
    big = jnp.int32(LANES)
    gmask = sub < N_GROUPS
    gl = jnp.where(gmask, logits_t, NEG_INF)
    gmax = jnp.max(gl, axis=0, keepdims=True)
    g_top = jnp.min(jnp.where(gmask & (gl == gmax), sub, big), axis=0, keepdims=True)
    g_weight = 1.0 / jnp.sum(jnp.where(gmask, jnp.exp(gl - gmax), 0.0), axis=0, keepdims=True)
    lo = ROUTER_BASE + g_top * EXPERTS_PER_GROUP
    emask = (sub >= lo) & (sub < lo + EXPERTS_PER_GROUP)
    el = jnp.where(emask, logits_t, NEG_INF)
    m1 = jnp.max(el, axis=0, keepdims=True)
    i1 = jnp.min(jnp.where(emask & (el == m1), sub, big), axis=0, keepdims=True)
    emask2 = emask & (sub != i1)
    el2 = jnp.where(emask2, logits_t, NEG_INF)
    m2 = jnp.max(el2, axis=0, keepdims=True)
    i2 = jnp.min(jnp.where(emask2 & (el2 == m2), sub, big), axis=0, keepdims=True)
    e2 = jnp.exp(m2 - m1)
    w1 = g_weight / (1.0 + e2)
    w2 = g_weight * e2 / (1.0 + e2)
    return i1 - ROUTER_BASE, i2 - ROUTER_BASE, w1, w2


TOKEN_ROWS = D_MODEL // (2 * LANES)
WORD = jnp.uint32


def _to_token_tiles(ref, v, base=0):
    t, d = v.shape
    words = pltpu.pack_elementwise([v[:, :d // 2], v[:, d // 2:]], packed_dtype=BF16)
    for s in range(TOKEN_ROWS):
        ref[pl.ds(base + s, t, stride=TOKEN_ROWS), :] = words[:, s * LANES:(s + 1) * LANES]


def _token_words(ref, t, base=0):
    return jnp.concatenate([ref[pl.ds(base + s, t, stride=TOKEN_ROWS), :] for s in range(TOKEN_ROWS)], axis=-1)


def _from_token_tiles(ref, t, base=0):
    words = _token_words(ref, t, base)
    halves = [pltpu.unpack_elementwise(words, index=j, packed_dtype=BF16, unpacked_dtype=F32) for j in range(2)]
    return jnp.concatenate(halves, axis=-1)


MIX_CHUNK = 256


def _mix_kernel(na_ref, sw_ref, x_ref, bna_ref, bsw_ref, wo_ref, gpm_ref, ga_ref, gpf_ref, scf_ref, shf_ref,
                wr_ref, x1_ref, h2_ref, r_ref, rt_ref, cntc_ref, cntr_ref):
    first_step = (pl.program_id(0) == 0) & (pl.program_id(1) == 0)

    @pl.when(first_step)
    def _():
        cntc_ref[...] = jnp.zeros_like(cntc_ref)
        cntr_ref[...] = jnp.zeros_like(cntr_ref)

    t = MIX_CHUNK
    chunks = range(x_ref.shape[1] // t)
    row_slices = [slice(c * t, (c + 1) * t) for c in chunks]
    mixes, h2s, all_logits = [], [], []
    for rows in row_slices:
        def heads(ref):
            return jnp.concatenate([ref[0, j, rows, :] for j in range(PAIRS)], axis=-1).astype(F32)

        na = (_rms(heads(na_ref)) * bna_ref[...]).astype(BF16)
        sw = (_rms(heads(sw_ref)) * bsw_ref[...]).astype(BF16)
        mixes.append(jnp.dot(na, wo_ref[:NA_WIDTH, :], preferred_element_type=F32)
                     + jnp.dot(sw, wo_ref[NA_WIDTH:, :], preferred_element_type=F32))
    gate_gain = ga_ref[0] * gpm_ref[...]
    ffn_gain = gpf_ref[...] * (1.0 + scf_ref[0])
    for c, rows in zip(chunks, row_slices):
        x1 = x_ref[0, rows, :] + _rms(mixes[c]) * gate_gain
        x1_ref[0, rows, :] = x1
        h2 = _rms(x1) * ffn_gain + shf_ref[0]
        _to_token_tiles(h2_ref, h2, base=c * t * TOKEN_ROWS)
        h2s.append(h2)
    for c in chunks:
        h_hi, h_lo = _split2(h2s[c])
        both = (jnp.dot(h_hi, wr_ref[...], preferred_element_type=F32)
                + jnp.dot(h_lo, wr_ref[...], preferred_element_type=F32))
        all_logits.append(both + pltpu.roll(both, LANES - ROUTER_COLS, 1))
    for c, rows in zip(chunks, row_slices):
        e1, e2, w1, w2 = _route(jnp.transpose(all_logits[c])[:ROUTER_ROWS])
        fields = [None] * 4
        fields[R_E1], fields[R_E2], fields[R_W1], fields[R_W2] = e1.astype(F32), e2.astype(F32), w1, w2
        rt = jnp.concatenate(fields + [jnp.zeros((SUBLANES - 4, t), F32)], axis=0)
        rt_ref[:, rows] = rt
        r = jnp.transpose(jnp.concatenate([rt, jnp.zeros((LANES - SUBLANES, t), F32)], axis=0))
        r_ref[rows, :] = r
        sub = lax.broadcasted_iota(jnp.int32, (N_EXPERTS, t), 0)
        on_sub = ((sub == e1) | (sub == e2)).astype(F32)
        cntc_ref[...] += jnp.broadcast_to(jnp.sum(on_sub, axis=1, keepdims=True), cntc_ref.shape)
        lane = lax.broadcasted_iota(jnp.int32, r.shape, 1).astype(F32)
        on_lane = ((lane == r[:, R_E1:R_E1 + 1]) | (lane == r[:, R_E2:R_E2 + 1])).astype(F32)
        cntr_ref[...] += jnp.broadcast_to(jnp.sum(on_lane, axis=0, keepdims=True), cntr_ref.shape)


def _mix(na, sw, x, beta_na, beta_swa, w_out, g_post_mix, gate_a, g_pre_ffn, scale_f, shift_f, w_router3):
    b, s, d = x.shape
    tm = 1024
    nt = s // tm
    row = lambda bi, i: (bi, i, 0)
    const2 = lambda bi, i: (0, 0)
    per_b = lambda bi, i: (bi, 0, 0)
    return pl.pallas_call(
        _mix_kernel,
        out_shape=(jax.ShapeDtypeStruct((b, s, d), F32),
                   jax.ShapeDtypeStruct((b * s * TOKEN_ROWS, LANES), WORD),
                   jax.ShapeDtypeStruct((b * s, LANES), F32),
                   jax.ShapeDtypeStruct((SUBLANES, b * s), F32),
                   jax.ShapeDtypeStruct((N_EXPERTS, LANES), F32),
                   jax.ShapeDtypeStruct((SUBLANES, LANES), F32)),
        grid=(b, s // tm),
        in_specs=[pl.BlockSpec((1, PAIRS, tm, LANES), lambda bi, i: (bi, 0, i, 0)),
                  pl.BlockSpec((1, PAIRS, tm, LANES), lambda bi, i: (bi, 0, i, 0)),
                  pl.BlockSpec((1, tm, d), row),
                  pl.BlockSpec((1, NA_WIDTH), const2),
                  pl.BlockSpec((1, SWA_WIDTH), const2),
                  pl.BlockSpec((NA_WIDTH + SWA_WIDTH, d), const2),
                  pl.BlockSpec((1, d), const2),
                  pl.BlockSpec((1, 1, d), per_b),
                  pl.BlockSpec((1, d), const2),
                  pl.BlockSpec((1, 1, d), per_b),
                  pl.BlockSpec((1, 1, d), per_b),
                  pl.BlockSpec((d, LANES), const2)],
        out_specs=(pl.BlockSpec((1, tm, d), row),
                   pl.BlockSpec((tm * TOKEN_ROWS, LANES), lambda bi, i: (bi * nt + i, 0)),
                   pl.BlockSpec((tm, LANES), lambda bi, i: (bi * nt + i, 0)),
                   pl.BlockSpec((SUBLANES, tm), lambda bi, i: (0, bi * nt + i)),
                   pl.BlockSpec((N_EXPERTS, LANES), const2),
                   pl.BlockSpec((SUBLANES, LANES), const2)),
        compiler_params=_params("arbitrary", "arbitrary"),
        name="mix",
    )(na, sw, x, beta_na.reshape(1, -1), beta_swa.reshape(1, -1), w_out, g_post_mix.reshape(1, d), gate_a,
      g_pre_ffn.reshape(1, d), scale_f, shift_f, w_router3)


MOE_TILE = 1024
MOE_SUBTILE = 256
PLAN_T = 2048
I_TILE, I_EXPERT, I_LO, I_HI, I_FIRST, I_NEW, I_NEXT, I_ORDER = range(8)


def _plan_kernel(rt_ref, cntc_ref, cntr_ref, pos_ref, items_ref, start_ref, carry_ref, *, n_tiles, n_items):
    i = pl.program_id(0)
    sub = lax.broadcasted_iota(jnp.int32, (N_EXPERTS, LANES), 0)
    lane = lax.broadcasted_iota(jnp.int32, (N_EXPERTS, LANES), 1)

    @pl.when(i == 0)
    def _():
        c_col = cntc_ref[:, 0:1]
        c_row = cntr_ref[0:1, :]
        s_col = jnp.sum(jnp.where(lane < sub, c_row, 0.0), axis=1, keepdims=True)
        s_row = jnp.sum(jnp.where(sub < lane, c_col, 0.0), axis=0, keepdims=True)
        start_ref[...] = jnp.broadcast_to(s_col, start_ref.shape)
        carry_ref[...] = jnp.zeros_like(carry_ref)

        def tiles_of(s, c):
            first = jnp.floor(s * (1.0 / MOE_TILE))
            last = jnp.floor((s + c - 1.0) * (1.0 / MOE_TILE))
            return first, jnp.where(c > 0.0, last - first + 1.0, 0.0)

        f_col, n_col = tiles_of(s_col, c_col)
        _, n_row = tiles_of(s_row, c_row)
        i_col = jnp.sum(jnp.where(lane < sub, n_row, 0.0), axis=1, keepdims=True)
        total = jnp.sum(n_col, axis=0, keepdims=True)
        k = lax.broadcasted_iota(jnp.int32, (N_EXPERTS, n_items), 1).astype(F32)
        subk = lax.broadcasted_iota(jnp.int32, (N_EXPERTS, n_items), 0).astype(F32)
        ek = jnp.sum(jnp.where(i_col + n_col <= k, 1.0, 0.0), axis=0, keepdims=True)
        k0 = k[0:1]
        valid = k0 < total
        sel = subk == ek

        def pick(v):
            return jnp.sum(jnp.where(sel, v, 0.0), axis=0, keepdims=True)

        i_k, f_k, s_k, c_k = pick(i_col), pick(f_col), pick(s_col), pick(c_col)
        tile = f_k + (k0 - i_k)
        row0 = tile * MOE_TILE
        lo = jnp.maximum(s_k, row0) - row0
        hi = jnp.minimum(s_k + c_k, row0 + MOE_TILE) - row0
        present = n_col > 0.0
        last_expert = jnp.sum(jnp.where(i_col + n_col <= total - 1.0, 1.0, 0.0), axis=0, keepdims=True)
        nxt = jnp.min(jnp.where(present & (subk > ek), subk, float(N_EXPERTS)), axis=0, keepdims=True)
        order = jnp.sum(jnp.where(present & (subk < ek), 1.0, 0.0), axis=0, keepdims=True)
        rows = [jnp.where(valid, tile, n_tiles - 1.0), jnp.where(valid, ek, last_expert),
                jnp.where(valid, lo, 0.0), jnp.where(valid, hi, 0.0),
                jnp.where(valid & (lo == 0.0), 1.0, 0.0),
                jnp.where(valid & (k0 == i_k), 1.0, 0.0), jnp.where(valid, nxt, float(N_EXPERTS)),
                jnp.where(valid, order, 0.0)]
        assert len(rows) == SUBLANES
        items_ref[...] = jnp.concatenate(rows, axis=0).astype(jnp.int32)

    t = rt_ref.shape[1]
    e1 = rt_ref[R_E1:R_E1 + 1, :]
    e2 = rt_ref[R_E2:R_E2 + 1, :]
    sub_t = lax.broadcasted_iota(jnp.int32, (N_EXPERTS, t), 0).astype(F32)
    oh1 = sub_t == e1
    oh2 = sub_t == e2
    oh = (oh1 | oh2).astype(F32)
    before = (lax.broadcasted_iota(jnp.int32, (t, t), 0) < lax.broadcasted_iota(jnp.int32, (t, t), 1)).astype(BF16)
    rank = jnp.dot(oh.astype(BF16), before, preferred_element_type=F32)
    base = start_ref[:, 0:1] + carry_ref[:, 0:1] + rank
    pos1 = jnp.sum(jnp.where(oh1, base, 0.0), axis=0, keepdims=True)
    pos2 = jnp.sum(jnp.where(oh2, base, 0.0), axis=0, keepdims=True)
    carry_ref[...] += jnp.broadcast_to(jnp.sum(oh, axis=1, keepdims=True), carry_ref.shape)
    pos = jnp.concatenate([pos1, pos2] + [jnp.zeros_like(pos1)] * (SUBLANES - 2), axis=0)
    pos_ref[...] = pos.astype(jnp.int32)


def _plan(rt, cntc, cntr):
    n = rt.shape[1]
    n_tiles = 2 * n // MOE_TILE
    n_items = 2 * LANES
    assert n_tiles + N_EXPERTS <= n_items
    return pl.pallas_call(
        functools.partial(_plan_kernel, n_tiles=n_tiles, n_items=n_items),
        out_shape=(jax.ShapeDtypeStruct((SUBLANES, n), jnp.int32),
                   jax.ShapeDtypeStruct((SUBLANES, n_items), jnp.int32)),
        grid=(n // PLAN_T,),
        in_specs=[pl.BlockSpec((SUBLANES, PLAN_T), lambda i: (0, i)),
                  pl.BlockSpec((N_EXPERTS, LANES), lambda i: (0, 0)),
                  pl.BlockSpec((SUBLANES, LANES), lambda i: (0, 0))],
        out_specs=(pl.BlockSpec((SUBLANES, PLAN_T), lambda i: (0, i)),
                   pl.BlockSpec((SUBLANES, n_items), lambda i: (0, 0))),
        scratch_shapes=[pltpu.VMEM((N_EXPERTS, LANES), F32), pltpu.VMEM((N_EXPERTS, LANES), F32)],
        compiler_params=_params("arbitrary"),
        name="plan",
    )(rt, cntc, cntr)


DISPATCH_T = 2048


def _token_rows(ref, index):
    return ref.at[pl.ds(pl.multiple_of(index * TOKEN_ROWS, TOKEN_ROWS), TOKEN_ROWS), :]


def _dispatch_kernel(pos_ref, h_ref, xs_hbm, sem):
    def body(r, carry):
        src = _token_rows(h_ref, r)
        for k in range(2):
            pltpu.make_async_copy(src, _token_rows(xs_hbm, pos_ref[k, r]), sem.at[0]).start(priority=k)
        return carry

    lax.fori_loop(0, DISPATCH_T, body, 0, unroll=8)
    for k in range(2):
        pltpu.make_async_copy(h_ref, xs_hbm.at[pl.ds(0, DISPATCH_T * TOKEN_ROWS), :], sem.at[0]).wait()


def _dispatch(pos, h2):
    n = pos.shape[1]
    return pl.pallas_call(
        _dispatch_kernel,
        out_shape=jax.ShapeDtypeStruct((2 * n * TOKEN_ROWS, LANES), WORD),
        grid=(n // DISPATCH_T,),
        in_specs=[pl.BlockSpec((SUBLANES, DISPATCH_T), lambda i: (0, i), memory_space=pltpu.SMEM),
                  pl.BlockSpec((DISPATCH_T * TOKEN_ROWS, LANES), lambda i: (i, 0))],
        out_specs=pl.BlockSpec(memory_space=pl.ANY),
        scratch_shapes=[pltpu.SemaphoreType.DMA((1,))],
        compiler_params=_params("arbitrary"),
        name="dispatch",
    )(pos, h2)


def _expert_kernel(tile_ref, exp_ref, lo_ref, hi_ref, first_ref, new_ref, next_ref, order_ref,
                   xs_ref, wg_hbm, wu_hbm, wd_hbm, o_ref, wg_buf, wu_buf, wd_buf, sem):
    k = pl.program_id(0)
    lo = lo_ref[k]
    hi = hi_ref[k]
    slot = order_ref[k] % 2

    def weight_copies(expert, sl):
        return [pltpu.make_async_copy(src.at[expert], dst.at[sl], sem.at[sl])
                for src, dst in ((wg_hbm, wg_buf), (wu_hbm, wu_buf), (wd_hbm, wd_buf))]

    @pl.when(k == 0)
    def _():
        for copy in weight_copies(exp_ref[0], 0):
            copy.start()

    @pl.when(new_ref[k] == 1)
    def _():
        for copy in weight_copies(exp_ref[k], slot):
            copy.wait()

        @pl.when(next_ref[k] < N_EXPERTS)
        def _():
            for copy in weight_copies(next_ref[k], 1 - slot):
                copy.start()

    @pl.when(first_ref[k] == 1)
    def _():
        o_ref[...] = jnp.zeros_like(o_ref)

    for part in range(MOE_TILE // MOE_SUBTILE):
        first_row = part * MOE_SUBTILE
        base = first_row * TOKEN_ROWS

        @pl.when((hi > first_row) & (lo < first_row + MOE_SUBTILE))
        def _():
            x = _from_token_tiles(xs_ref, MOE_SUBTILE, base=base).astype(BF16)
            gate = jnp.dot(x, wg_buf[slot].astype(BF16), preferred_element_type=F32)
            up = jnp.dot(x, wu_buf[slot].astype(BF16), preferred_element_type=F32)
            he = (gate * jax.nn.sigmoid(gate) * up).astype(BF16)
            ye = jnp.dot(he, wd_buf[slot].astype(BF16), preferred_element_type=F32)
            row = first_row + lax.broadcasted_iota(jnp.int32, (MOE_SUBTILE, 1), 0)
            mine = (row >= lo) & (row < hi)
            _to_token_tiles(o_ref, jnp.where(mine, ye, _from_token_tiles(o_ref, MOE_SUBTILE, base=base)), base=base)


def _experts(items, xs, w_gate, w_up, w_down):
    n_rows = xs.shape[0] // TOKEN_ROWS
    n_items = n_rows // MOE_TILE + N_EXPERTS
    d = w_gate.shape[1]
    tile_map = lambda k, tile, *_: (tile[k], 0)
    grid_spec = pltpu.PrefetchScalarGridSpec(
        num_scalar_prefetch=SUBLANES,
        grid=(n_items,),
        in_specs=[pl.BlockSpec((MOE_TILE * TOKEN_ROWS, LANES), tile_map),
                  pl.BlockSpec(memory_space=pl.ANY),
                  pl.BlockSpec(memory_space=pl.ANY),
                  pl.BlockSpec(memory_space=pl.ANY)],
        out_specs=pl.BlockSpec((MOE_TILE * TOKEN_ROWS, LANES), tile_map),
        scratch_shapes=[pltpu.VMEM((2, d, EXPERT_FF), F32), pltpu.VMEM((2, d, EXPERT_FF), F32),
                        pltpu.VMEM((2, EXPERT_FF, d), F32), pltpu.SemaphoreType.DMA((2,))])
    return pl.pallas_call(
        _expert_kernel,
        out_shape=jax.ShapeDtypeStruct(xs.shape, WORD),
        grid_spec=grid_spec,
        compiler_params=_params("arbitrary"),
        name="experts",
    )(*(items[j, :n_items] for j in range(SUBLANES)), xs, w_gate, w_up, w_down)


COMBINE_T = 512


def _combine_kernel(pos_ref, posn_ref, ys_hbm, r_ref, x1_ref, gf_ref, gpost_ref, o_ref, buf, sem):
    i = pl.program_id(0)
    n = pl.num_programs(0)
    slot = i % 2
    slot_rows = COMBINE_T * TOKEN_ROWS

    def start_row(p_ref, sl, r):
        for k in range(2):
            dst = buf.at[sl, pl.ds(pl.multiple_of(k * slot_rows + r * TOKEN_ROWS, TOKEN_ROWS), TOKEN_ROWS), :]
            pltpu.make_async_copy(_token_rows(ys_hbm, p_ref[k, r]), dst, sem.at[sl]).start(priority=k)

    def wait_slot(sl):
        pltpu.make_async_copy(ys_hbm.at[pl.ds(0, 2 * slot_rows), :], buf.at[sl], sem.at[sl]).wait()

    def issue(p_ref, sl):
        def body(r, carry):
            start_row(p_ref, sl, r)
            return carry

        lax.fori_loop(0, COMBINE_T, body, 0, unroll=8)

    @pl.when(i == 0)
    def _():
        issue(pos_ref, 0)

    @pl.when(i + 1 < n)
    def _():
        issue(posn_ref, 1 - slot)

    wait_slot(slot)
    ya = _from_token_tiles(buf.at[slot], COMBINE_T)
    yb = _from_token_tiles(buf.at[slot], COMBINE_T, base=slot_rows)
    r = r_ref[...]
    y = r[:, R_W1:R_W1 + 1] * ya + r[:, R_W2:R_W2 + 1] * yb
    o_ref[0] = x1_ref[0] + gf_ref[0] * (_rms(y) * gpost_ref[...])


def _combine(pos, ys, r, x1, gate_f, g_post_ffn):
    b, s, d = x1.shape
    nt = s // COMBINE_T
    n_steps = b * nt
    row = lambda i: (i // nt, i % nt, 0)
    pos_spec = lambda f: pl.BlockSpec((SUBLANES, COMBINE_T), lambda i: (0, f(i)), memory_space=pltpu.SMEM)
    return pl.pallas_call(
        _combine_kernel,
        out_shape=jax.ShapeDtypeStruct((b, s, d), F32),
        grid=(n_steps,),
        in_specs=[pos_spec(lambda i: i),
                  pos_spec(lambda i: jnp.minimum(i + 1, n_steps - 1)),
                  pl.BlockSpec(memory_space=pl.ANY),
                  pl.BlockSpec((COMBINE_T, LANES), lambda i: (i, 0)),
                  pl.BlockSpec((1, COMBINE_T, d), row),
                  pl.BlockSpec((1, 1, d), lambda i: (i // nt, 0, 0)),
                  pl.BlockSpec((1, d), lambda i: (0, 0))],
        out_specs=pl.BlockSpec((1, COMBINE_T, d), row),
        scratch_shapes=[pltpu.VMEM((2, 2 * COMBINE_T * TOKEN_ROWS, LANES), WORD), pltpu.SemaphoreType.DMA((2,))],
        compiler_params=_params("arbitrary"),
        name="combine",
    )(pos, pos, ys, r, x1, gate_f, g_post_ffn.reshape(1, d))


def _rope_tables(s):
    half = HEAD_DIM // 2
    inv = ROPE_THETA ** (-np.arange(half, dtype=np.float64) * 2.0 / HEAD_DIM)
    ang = np.arange(s, dtype=np.float64)[:, None] * inv[None, :]
    cos = np.cos(ang)
    sin = np.sin(ang)
    cos = np.concatenate([cos, cos, cos, cos], axis=-1)
    sin_signed = np.concatenate([-sin, sin, -sin, sin], axis=-1)
    return jnp.asarray(cos, F32), jnp.asarray(sin_signed, F32)


def _router_weights(w_group_router, w_expert_router):
    d = w_group_router.shape[0]
    we = jnp.transpose(w_expert_router, (1, 0, 2)).reshape(d, N_EXPERTS)
    hi, lo = _split2(jnp.concatenate([w_group_router, we], axis=-1))
    return jnp.concatenate([hi, lo, jnp.zeros((d, LANES - 2 * ROUTER_COLS), BF16)], axis=-1)


def kernel(x, c, w_ada, b_ada, g_pre_mix, w_in, na_rpb, swa_sinks, beta_na, beta_swa, w_out, g_post_mix, g_pre_ffn,
           w_group_router, w_expert_router, w_gate, w_up, w_down, g_post_ffn):
    b, s, d = x.shape
    depth = w_ada.shape[0]
    cos, sin_signed = _rope_tables(s)
    for l in range(depth):
        mod = _adaln(c, w_ada[l], b_ada[l]).reshape(b, N_MOD, 1, d)
        shift_a, scale_a, gate_a, shift_f, scale_f, gate_f = (mod[:, k] for k in range(N_MOD))
        qkv, na_kg, na_vg = _qkv(x, g_pre_mix[l], scale_a, shift_a, w_in[l], cos, sin_signed)
        na = _na(qkv, na_kg, na_vg, na_rpb[l])
        sw = _swa(qkv, swa_sinks[l])
        x1, h2, r, rt, cntc, cntr = _mix(na, sw, x, beta_na[l], beta_swa[l], w_out[l].astype(BF16), g_post_mix[l],
                                          gate_a, g_pre_ffn[l], scale_f, shift_f,
                                          _router_weights(w_group_router[l], w_expert_router[l]))
        pos, items = _plan(rt, cntc, cntr)
        xs = _dispatch(pos, h2)
        ys = _experts(items, xs, w_gate[l], w_up[l], w_down[l])
        x = _combine(pos, ys, r, x1, gate_f, g_post_ffn[l])
    return x
```

```python
import functools

import jax
import jax.numpy as jnp
import numpy as np
from jax import lax
from jax.experimental import pallas as pl
from jax.experimental.pallas import tpu as pltpu

D_MODEL = 1024
GRID_W = 64
HEAD_DIM = 64
NA_HEADS = 8
NA_KH = 8
NA_KW = 16
SWA_HEADS = 8
SWA_KV_HEADS = 2
SWA_WINDOW = 128
SWA_BLOCK = 128
ROPE_THETA = 10000.0
NA_WIDTH = NA_HEADS * HEAD_DIM
SWA_WIDTH = SWA_HEADS * HEAD_DIM
N_GROUPS = 4
EXPERTS_PER_GROUP = 8
N_EXPERTS = N_GROUPS * EXPERTS_PER_GROUP
EXPERT_FF = 256
N_MOD = 6
EPS = 1e-6
NEG_INF = -1e30

LANES = 128
SUBLANES = 8
R_E1, R_E2, R_W1, R_W2 = range(4)
PAIRS = NA_HEADS // 2
W_NQ, W_NK, W_NV, W_SQ, W_SKV = 0, 4, 8, 12, 16
COL_NQ, COL_SQ, COL_SK, COL_SV = 0, 4, 8, 10
QKV_TILES = 12
NA_GROUPS = GRID_W // NA_KW
NA_WIN = 2 * NA_KW
NA_WIN_START = tuple(min(max(NA_KW * g - NA_KW // 2, 0), GRID_W - NA_WIN) for g in range(NA_GROUPS))
ROUTER_BASE = N_GROUPS
VMEM_LIMIT = 56 * 1024 * 1024

F32 = jnp.float32
BF16 = jnp.bfloat16
LOG2E = 1.4426950408889634


def _rms(v):
    return v * lax.rsqrt(jnp.mean(v * v, axis=-1, keepdims=True) + EPS)


def _params(*sem):
    return pltpu.CompilerParams(dimension_semantics=sem, vmem_limit_bytes=VMEM_LIMIT)


def _adaln_kernel(c_ref, w_ref, b_ref, o_ref, *, batch):
    c = c_ref[...]
    a_t = jnp.transpose(c * jax.nn.sigmoid(c))
    w = w_ref[...]
    rows = [jnp.sum(a_t[:, bi:bi + 1] * w, axis=0, keepdims=True) for bi in range(batch)]
    rows.append(jnp.zeros((c.shape[0] - batch, w.shape[1]), F32))
    o_ref[...] = jnp.concatenate(rows, axis=0) + b_ref[...]


def _adaln(c, w_ada, b_ada):
    batch, d = c.shape
    n = w_ada.shape[1]
    tn = 1024
    b = SUBLANES
    assert batch < b
    c = jnp.pad(c, ((0, b - batch), (0, 0)))
    return pl.pallas_call(
        functools.partial(_adaln_kernel, batch=batch),
        out_shape=jax.ShapeDtypeStruct((b, n), F32),
        grid=(n // tn,),
        in_specs=[pl.BlockSpec((b, d), lambda j: (0, 0)),
                  pl.BlockSpec((d, tn), lambda j: (0, j)),
                  pl.BlockSpec((1, tn), lambda j: (0, j))],
        out_specs=pl.BlockSpec((b, tn), lambda j: (0, j)),
        compiler_params=_params("arbitrary"),
        name="adaln",
    )(c, w_ada, b_ada.reshape(1, n))[:batch]


def _rope(v, cos, sin_signed, first_half):
    rot = jnp.where(first_half, pltpu.roll(v, LANES - HEAD_DIM // 2, 1), pltpu.roll(v, HEAD_DIM // 2, 1))
    return v * cos + rot * sin_signed


QKV_CHUNK = 256


def _column_windows(v):
    grid_rows = v.shape[0] // GRID_W
    return [jnp.concatenate([v[r * GRID_W + w0:r * GRID_W + w0 + NA_WIN] for r in range(grid_rows)], axis=0)
            for w0 in NA_WIN_START]


def _qkv_kernel(x_ref, g_ref, sc_ref, sh_ref, w_ref, cos_ref, sin_ref, o_ref, kg_ref, vg_ref):
    scale = HEAD_DIM ** -0.5 * LOG2E
    lane = lax.broadcasted_iota(jnp.int32, (QKV_CHUNK, LANES), 1)
    first_half = (lane % HEAD_DIM) < HEAD_DIM // 2
    upper = lane >= HEAD_DIM

    def tile(v, j):
        return v[:, j * LANES:(j + 1) * LANES]

    for c in range(x_ref.shape[1] // QKV_CHUNK):
        rows = slice(c * QKV_CHUNK, (c + 1) * QKV_CHUNK)
        h = (_rms(x_ref[0, rows, :]) * g_ref[...]) * (1.0 + sc_ref[0]) + sh_ref[0]
        h = h.astype(BF16)
        cos = cos_ref[rows, :]
        sin = sin_ref[rows, :]

        def proj(col, width):
            return jnp.dot(h, w_ref[:, col * LANES:(col + width) * LANES], preferred_element_type=F32)

        nq, nk, nv, sq = proj(W_NQ, 4), proj(W_NK, 4), proj(W_NV, 4), proj(W_SQ, 4)
        win_rows = slice(c * QKV_CHUNK // 2, (c + 1) * QKV_CHUNK // 2)
        for j in range(PAIRS):
            o_ref[0, COL_NQ + j, rows, :] = (tile(nq, j) * scale).astype(BF16)
            o_ref[0, COL_SQ + j, rows, :] = (_rope(tile(sq, j), cos, sin, first_half) * scale).astype(BF16)
            for ref, val in ((kg_ref, tile(nk, j)), (vg_ref, tile(nv, j))):
                for g, win in enumerate(_column_windows(val)):
                    ref[0, j, g, win_rows, :] = win.astype(BF16)
        skv = proj(W_SKV, 2)
        k = _rope(tile(skv, 0), cos, sin, first_half)
        v = tile(skv, 1)
        for t, col in ((k, COL_SK), (v, COL_SV)):
            swapped = pltpu.roll(t, HEAD_DIM, 1)
            o_ref[0, col, rows, :] = jnp.where(upper, swapped, t).astype(BF16)
            o_ref[0, col + 1, rows, :] = jnp.where(upper, t, swapped).astype(BF16)


def _qkv(x, g, scale_a, shift_a, w_in, cos, sin):
    b, s, d = x.shape
    tm = 1024
    n_in = w_in.shape[1]
    windows = jax.ShapeDtypeStruct((b, PAIRS, NA_GROUPS, s // 2, LANES), BF16)
    windows_spec = pl.BlockSpec((1, PAIRS, NA_GROUPS, tm // 2, LANES), lambda bi, i: (bi, 0, 0, i, 0))
    return pl.pallas_call(
        _qkv_kernel,
        out_shape=(jax.ShapeDtypeStruct((b, QKV_TILES, s, LANES), BF16), windows, windows),
        grid=(b, s // tm),
        in_specs=[pl.BlockSpec((1, tm, d), lambda bi, i: (bi, i, 0)),
                  pl.BlockSpec((1, d), lambda bi, i: (0, 0)),
                  pl.BlockSpec((1, 1, d), lambda bi, i: (bi, 0, 0)),
                  pl.BlockSpec((1, 1, d), lambda bi, i: (bi, 0, 0)),
                  pl.BlockSpec((d, n_in), lambda bi, i: (0, 0)),
                  pl.BlockSpec((tm, LANES), lambda bi, i: (i, 0)),
                  pl.BlockSpec((tm, LANES), lambda bi, i: (i, 0))],
        out_specs=(pl.BlockSpec((1, QKV_TILES, tm, LANES), lambda bi, i: (bi, 0, i, 0)), windows_spec, windows_spec),
        compiler_params=_params("arbitrary", "arbitrary"),
        name="qkv",
    )(x, g.reshape(1, d), scale_a, shift_a, w_in, cos, sin)


NA_QROWS = 8
NA_KROWS = NA_QROWS + NA_KH
NA_BLOCKS_PER_STEP = 8
NA_INTERLEAVE = 8
NA_Q = NA_QROWS * NA_KW
NA_K = NA_KROWS * NA_WIN
NA_RPB_ROWS = 2 * NA_KH - 1
NA_RPB_COLS = 2 * NA_KW - 1


def _clamp(v, lo, hi):
    return min(max(v, lo), hi)


def _na_first_key_row(block, rows, clip):
    return clip(block * NA_QROWS - NA_KH // 2, 0, rows - NA_KROWS)


def _na_group_tables():
    def geometry(g):
        cols = [NA_KW * g + cc for cc in range(NA_KW)]
        return (NA_WIN_START[g] - NA_KW * g,) + tuple(_clamp(c - NA_KW // 2, 0, GRID_W - NA_KW) - c for c in cols)

    seen, table_of_group, representatives = {}, [], []
    for g in range(NA_GROUPS):
        key = geometry(g)
        if key not in seen:
            seen[key] = len(representatives)
            representatives.append(g)
        table_of_group.append(seen[key])
    return table_of_group, representatives


def _na_block_types(rows):
    def geometry(block):
        r = block * NA_QROWS
        a = _na_first_key_row(block, rows, _clamp)
        return (a - r,) + tuple(_clamp(r + j - NA_KH // 2, 0, rows - NA_KH) - r for j in range(NA_QROWS))

    n_blocks = rows // NA_QROWS
    interior = geometry(n_blocks // 2)
    lead = next(b for b in range(n_blocks) if geometry(b) == interior)
    trail = next(b for b in range(n_blocks) if geometry(n_blocks - 1 - b) == interior)
    assert all(geometry(b) == interior for b in range(lead, n_blocks - trail))
    return lead, trail


def _na_kernel(q_ref, k_ref, v_ref, bias_ref, o_ref, *, rows):
    upper = lax.broadcasted_iota(jnp.int32, (NA_Q, LANES), 1) >= HEAD_DIM
    key_upper = lax.broadcasted_iota(jnp.int32, (NA_K, LANES), 1) >= HEAD_DIM
    lead, trail = _na_block_types(rows)
    first_trailing = rows // NA_QROWS - trail
    n_types = lead + trail + 1
    table_of_group, _ = _na_group_tables()
    for c0 in range(0, NA_BLOCKS_PER_STEP, NA_INTERLEAVE):
        blocks = range(c0, c0 + NA_INTERLEAVE)
        starts, types, pieces = {}, {}, {}
        for c in blocks:
            block = pl.program_id(2) * NA_BLOCKS_PER_STEP + c
            a = _na_first_key_row(block, rows, jnp.clip)
            types[c] = jnp.where(block < lead, block,
                                 jnp.where(block >= first_trailing, block - first_trailing + lead + 1, lead))
            starts[c] = pl.multiple_of(a * NA_WIN, NA_WIN)
            for g in range(NA_GROUPS):
                pieces[c, g] = [slice((c * NA_QROWS + rr) * GRID_W + g * NA_KW,
                                      (c * NA_QROWS + rr) * GRID_W + (g + 1) * NA_KW) for rr in range(NA_QROWS)]
        chains = [(c, g, hh) for c in blocks for g in range(NA_GROUPS) for hh in range(2)]
        scores, probs, outs = {}, {}, {}
        for c, g, hh in chains:
            q = jnp.concatenate([q_ref[0, 0, rws, :] for rws in pieces[c, g]], axis=0)
            qm = jnp.where(upper if hh else ~upper, q, jnp.zeros_like(q))
            ks = k_ref[0, 0, g, pl.ds(starts[c], NA_K), :]
            s = lax.dot_general(qm, ks, (((1,), (1,)), ((), ())), preferred_element_type=F32)
            scores[c, g, hh] = s + bias_ref[hh, table_of_group[g] * n_types + types[c]]
        for chain in chains:
            s = scores[chain]
            probs[chain] = jnp.exp2(s - jnp.max(s, axis=-1, keepdims=True)).astype(BF16)
        for c, g, hh in chains:
            vs = v_ref[0, 0, g, pl.ds(starts[c], NA_K), :]
            v1 = jnp.where(key_upper if hh else ~key_upper, vs, jnp.ones_like(vs))
            o = jnp.dot(probs[c, g, hh], v1, preferred_element_type=F32)
            outs[c, g, hh] = o / o[:, (1 - hh) * HEAD_DIM:(1 - hh) * HEAD_DIM + 1]
        for c in blocks:
            for g in range(NA_GROUPS):
                out = jnp.where(upper, outs[c, g, 1], outs[c, g, 0]).astype(BF16)
                for rr, rws in enumerate(pieces[c, g]):
                    o_ref[0, 0, rws, :] = out[rr * NA_KW:(rr + 1) * NA_KW]


def _na_bias_kernel(rpb_ref, o_ref, *, rows):
    h = pl.program_id(0)
    cc = lax.broadcasted_iota(jnp.int32, (NA_KW, LANES), 0)
    lane = lax.broadcasted_iota(jnp.int32, (NA_KW, LANES), 1)
    w = lane % NA_WIN
    key_row_in_tile = lane // NA_WIN
    rows_per_tile = LANES // NA_WIN
    neg = jnp.full((NA_KW, LANES), NEG_INF, F32)
    base = h * NA_RPB_ROWS * NA_RPB_COLS
    lead, trail = _na_block_types(rows)
    n_blocks = rows // NA_QROWS
    type_blocks = list(range(lead + 1)) + list(range(n_blocks - trail, n_blocks))
    for table, g in enumerate(_na_group_tables()[1]):
        qc = g * NA_KW + cc
        kc = NA_WIN_START[g] + w
        c0 = jnp.clip(qc - NA_KW // 2, 0, GRID_W - NA_KW)
        in_cols = (kc >= c0) & (kc < c0 + NA_KW)
        dc = kc - qc + NA_KW - 1
        by_row_offset = []
        for d in range(NA_RPB_ROWS):
            acc = neg
            for dd in range(NA_RPB_COLS):
                acc = jnp.where(dc == dd, rpb_ref[base + d * NA_RPB_COLS + dd], acc)
            by_row_offset.append(jnp.where(in_cols, acc * LOG2E, NEG_INF))
        for ty, block in enumerate(type_blocks):
            r = block * NA_QROWS
            a = _na_first_key_row(block, rows, _clamp)
            for j in range(NA_QROWS):
                r0 = _clamp(r + j - NA_KH // 2, 0, rows - NA_KH)
                for t in range(NA_K // LANES):
                    tile = neg
                    for part in range(rows_per_tile):
                        i = t * rows_per_tile + part
                        if r0 <= a + i < r0 + NA_KH:
                            tile = jnp.where(key_row_in_tile == part, by_row_offset[a + i - (r + j) + NA_KH - 1], tile)
                    o_ref[0, table * len(type_blocks) + ty, j * NA_KW:(j + 1) * NA_KW, t * LANES:(t + 1) * LANES] = tile


def _na_bias(rpb, rows):
    n_tables = len(_na_group_tables()[1]) * (sum(_na_block_types(rows)) + 1)
    return pl.pallas_call(
        functools.partial(_na_bias_kernel, rows=rows),
        out_shape=jax.ShapeDtypeStruct((NA_HEADS, n_tables, NA_Q, NA_K), F32),
        grid=(NA_HEADS,),
        in_specs=[pl.BlockSpec(memory_space=pltpu.SMEM)],
        out_specs=pl.BlockSpec((1, n_tables, NA_Q, NA_K), lambda h: (h, 0, 0, 0)),
        compiler_params=_params("arbitrary"),
        name="na_bias",
    )(rpb.astype(F32).reshape(-1))


def _na(qkv, kg, vg, rpb):
    b, _, s, _ = qkv.shape
    rows = s // GRID_W
    assert rows % (NA_QROWS * NA_BLOCKS_PER_STEP) == 0
    tq = NA_BLOCKS_PER_STEP * NA_QROWS * GRID_W
    bias = _na_bias(rpb, rows)
    windows_spec = pl.BlockSpec((1, 1) + kg.shape[2:], lambda bi, p, i: (bi, p, 0, 0, 0))
    return pl.pallas_call(
        functools.partial(_na_kernel, rows=rows),
        out_shape=jax.ShapeDtypeStruct((b, PAIRS, s, LANES), BF16),
        grid=(b, PAIRS, s // tq),
        in_specs=[pl.BlockSpec((1, 1, tq, LANES), lambda bi, p, i: (bi, COL_NQ + p, i, 0)),
                  windows_spec,
                  windows_spec,
                  pl.BlockSpec((2,) + bias.shape[1:], lambda bi, p, i: (p, 0, 0, 0))],
        out_specs=pl.BlockSpec((1, 1, tq, LANES), lambda bi, p, i: (bi, p, i, 0)),
        compiler_params=_params("arbitrary", "arbitrary", "arbitrary"),
        name="na",
    )(qkv, kg, vg, bias)


SWA_KEYS = 3 * SWA_BLOCK
SWA_BLOCKS_PER_STEP = 32
SWA_INTERLEAVE = 2
SWA_GROUP = SWA_HEADS // SWA_KV_HEADS
assert COL_SQ % (PAIRS // SWA_KV_HEADS) == 0


def _swa_masks():
    v = np.arange(SWA_KEYS // SWA_BLOCK)[:, None, None]
    q = np.arange(SWA_BLOCK)[None, :, None]
    k = np.arange(SWA_KEYS)[None, None, :]
    return np.where(np.abs(k - v * SWA_BLOCK - q) <= SWA_WINDOW, 0.0, NEG_INF).astype(np.float32)


def _swa_kernel(sink_ref, mask_ref, q_ref, k_ref, v_ref, o_ref, *, seq):
    kv = pl.program_id(1)
    rows = SWA_GROUP * SWA_BLOCK
    lane = lax.broadcasted_iota(jnp.int32, (SWA_BLOCK, LANES), 1)
    upper = lane >= HEAD_DIM
    head = lax.broadcasted_iota(jnp.int32, (rows, 1), 0) // SWA_BLOCK
    sink = jnp.zeros((rows, 1), F32)
    for g in range(SWA_GROUP):
        sink = jnp.where(head == g, sink_ref[kv * SWA_GROUP + g], sink)
    sink = sink * LOG2E
    for j0 in range(0, SWA_BLOCKS_PER_STEP, SWA_INTERLEAVE):
        blocks = range(j0, j0 + SWA_INTERLEAVE)
        starts, scores, probs = {}, {}, {}
        for j in blocks:
            n = pl.program_id(2) * SWA_BLOCKS_PER_STEP + j
            start = pl.multiple_of(jnp.clip((n - 1) * SWA_BLOCK, 0, seq - SWA_KEYS), SWA_BLOCK)
            starts[j] = start
            ks = k_ref[0, 0, pl.ds(start, SWA_KEYS), :]
            qs = []
            for g in range(SWA_GROUP):
                q = q_ref[0, g // 2, j * SWA_BLOCK:(j + 1) * SWA_BLOCK, :]
                qs.append(jnp.where(upper if g % 2 else ~upper, q, jnp.zeros_like(q)))
            s = lax.dot_general(jnp.concatenate(qs, axis=0), ks, (((1,), (1,)), ((), ())), preferred_element_type=F32)
            mask = mask_ref[(n * SWA_BLOCK - start) // SWA_BLOCK]
            scores[j] = s + jnp.concatenate([mask] * SWA_GROUP, axis=0)
        for j in blocks:
            s = scores[j]
            m = jnp.maximum(jnp.max(s, axis=-1, keepdims=True), sink)
            e = jnp.exp2(s - m)
            probs[j] = (e.astype(BF16), jnp.sum(e, axis=-1, keepdims=True) + jnp.exp2(sink - m))
        for j in blocks:
            e, l = probs[j]
            vs = v_ref[0, 0, pl.ds(starts[j], SWA_KEYS), :]
            o = jnp.dot(e, vs, preferred_element_type=F32) / l
            for pair in range(SWA_GROUP // 2):
                even = o[(2 * pair) * SWA_BLOCK:(2 * pair + 1) * SWA_BLOCK]
                odd = o[(2 * pair + 1) * SWA_BLOCK:(2 * pair + 2) * SWA_BLOCK]
                o_ref[0, pair, j * SWA_BLOCK:(j + 1) * SWA_BLOCK, :] = jnp.where(upper, odd, even).astype(BF16)


def _swa(qkv, sinks):
    b, _, s, _ = qkv.shape
    pairs_per_kv = PAIRS // SWA_KV_HEADS
    tq = SWA_BLOCKS_PER_STEP * SWA_BLOCK
    masks = _swa_masks()
    return pl.pallas_call(
        functools.partial(_swa_kernel, seq=s),
        out_shape=jax.ShapeDtypeStruct((b, PAIRS, s, LANES), BF16),
        grid=(b, SWA_KV_HEADS, s // tq),
        in_specs=[pl.BlockSpec(memory_space=pltpu.SMEM),
                  pl.BlockSpec(masks.shape, lambda bi, kv, n: (0, 0, 0)),
                  pl.BlockSpec((1, pairs_per_kv, tq, LANES),
                               lambda bi, kv, n: (bi, COL_SQ // pairs_per_kv + kv, n, 0)),
                  pl.BlockSpec((1, 1, s, LANES), lambda bi, kv, n: (bi, COL_SK + kv, 0, 0)),
                  pl.BlockSpec((1, 1, s, LANES), lambda bi, kv, n: (bi, COL_SV + kv, 0, 0))],
        out_specs=pl.BlockSpec((1, pairs_per_kv, tq, LANES), lambda bi, kv, n: (bi, kv, n, 0)),
        compiler_params=_params("arbitrary", "arbitrary", "arbitrary"),
        name="swa",
    )(sinks, jnp.asarray(masks), qkv, qkv, qkv)


ROUTER_COLS = N_GROUPS + N_EXPERTS


def _split2(v):
    hi = v.astype(BF16)
    lo = (v - hi.astype(F32)).astype(BF16)
    return hi, lo


ROUTER_ROWS = 40


def _route(logits_t):
    sub = lax.broadcasted_iota(jnp.int32, logits_t.shape, 0)
    big = jnp.int32(LANES)
    gmask = sub < N_GROUPS
    gl = jnp.where(gmask, logits_t, NEG_INF)
    gmax = jnp.max(gl, axis=0, keepdims=True)
    g_top = jnp.min(jnp.where(gmask & (gl == gmax), sub, big), axis=0, keepdims=True)
    g_weight = 1.0 / jnp.sum(jnp.where(gmask, jnp.exp(gl - gmax), 0.0), axis=0, keepdims=True)
    lo = ROUTER_BASE + g_top * EXPERTS_PER_GROUP
    emask = (sub >= lo) & (sub < lo + EXPERTS_PER_GROUP)
    el = jnp.where(emask, logits_t, NEG_INF)
    m1 = jnp.max(el, axis=0, keepdims=True)
    i1 = jnp.min(jnp.where(emask & (el == m1), sub, big), axis=0, keepdims=True)
    emask2 = emask & (sub != i1)
    el2 = jnp.where(emask2, logits_t, NEG_INF)
    m2 = jnp.max(el2, axis=0, keepdims=True)
    i2 = jnp.min(jnp.where(emask2 & (el2 == m2), sub, big), axis=0, keepdims=True)
    e2 = jnp.exp(m2 - m1)
    w1 = g_weight / (1.0 + e2)
    w2 = g_weight * e2 / (1.0 + e2)
    return i1 - ROUTER_BASE, i2 - ROUTER_BASE, w1, w2


TOKEN_ROWS = D_MODEL // (2 * LANES)
WORD = jnp.uint32


def _to_token_tiles(ref, v, base=0):
    t, d = v.shape
    words = pltpu.pack_elementwise([v[:, :d // 2], v[:, d // 2:]], packed_dtype=BF16)
    for s in range(TOKEN_ROWS):
        ref[pl.ds(base + s, t, stride=TOKEN_ROWS), :] = words[:, s * LANES:(s + 1) * LANES]


def _token_words(ref, t, base=0):
    return jnp.concatenate([ref[pl.ds(base + s, t, stride=TOKEN_ROWS), :] for s in range(TOKEN_ROWS)], axis=-1)


def _from_token_tiles(ref, t, base=0):
    words = _token_words(ref, t, base)
    halves = [pltpu.unpack_elementwise(words, index=j, packed_dtype=BF16, unpacked_dtype=F32) for j in range(2)]
    return jnp.concatenate(halves, axis=-1)


MIX_CHUNK = 256


def _mix_kernel(na_ref, sw_ref, x_ref, bna_ref, bsw_ref, wo_ref, gpm_ref, ga_ref, gpf_ref, scf_ref, shf_ref,
                wr_ref, x1_ref, h2_ref, r_ref, rt_ref, cntc_ref, cntr_ref):
    first_step = (pl.program_id(0) == 0) & (pl.program_id(1) == 0)

    @pl.when(first_step)
    def _():
        cntc_ref[...] = jnp.zeros_like(cntc_ref)
        cntr_ref[...] = jnp.zeros_like(cntr_ref)

    t = MIX_CHUNK
    chunks = range(x_ref.shape[1] // t)
    row_slices = [slice(c * t, (c + 1) * t) for c in chunks]
    mixes, h2s, all_logits = [], [], []
    for rows in row_slices:
        def heads(ref):
            return jnp.concatenate([ref[0, j, rows, :] for j in range(PAIRS)], axis=-1).astype(F32)

        na = (_rms(heads(na_ref)) * bna_ref[...]).astype(BF16)
        sw = (_rms(heads(sw_ref)) * bsw_ref[...]).astype(BF16)
        mixes.append(jnp.dot(na, wo_ref[:NA_WIDTH, :], preferred_element_type=F32)
                     + jnp.dot(sw, wo_ref[NA_WIDTH:, :], preferred_element_type=F32))
    gate_gain = ga_ref[0] * gpm_ref[...]
    ffn_gain = gpf_ref[...] * (1.0 + scf_ref[0])
    for c, rows in zip(chunks, row_slices):
        x1 = x_ref[0, rows, :] + _rms(mixes[c]) * gate_gain
        x1_ref[0, rows, :] = x1
        h2 = _rms(x1) * ffn_gain + shf_ref[0]
        _to_token_tiles(h2_ref, h2, base=c * t * TOKEN_ROWS)
        h2s.append(h2)
    for c in chunks:
        h_hi, h_lo = _split2(h2s[c])
        both = (jnp.dot(h_hi, wr_ref[...], preferred_element_type=F32)
                + jnp.dot(h_lo, wr_ref[...], preferred_element_type=F32))
        all_logits.append(both + pltpu.roll(both, LANES - ROUTER_COLS, 1))
    for c, rows in zip(chunks, row_slices):
        e1, e2, w1, w2 = _route(jnp.transpose(all_logits[c])[:ROUTER_ROWS])
        fields = [None] * 4
        fields[R_E1], fields[R_E2], fields[R_W1], fields[R_W2] = e1.astype(F32), e2.astype(F32), w1, w2
        rt = jnp.concatenate(fields + [jnp.zeros((SUBLANES - 4, t), F32)], axis=0)
        rt_ref[:, rows] = rt
        r = jnp.transpose(jnp.concatenate([rt, jnp.zeros((LANES - SUBLANES, t), F32)], axis=0))
        r_ref[rows, :] = r
        sub = lax.broadcasted_iota(jnp.int32, (N_EXPERTS, t), 0)
        on_sub = ((sub == e1) | (sub == e2)).astype(F32)
        cntc_ref[...] += jnp.broadcast_to(jnp.sum(on_sub, axis=1, keepdims=True), cntc_ref.shape)
        lane = lax.broadcasted_iota(jnp.int32, r.shape, 1).astype(F32)
        on_lane = ((lane == r[:, R_E1:R_E1 + 1]) | (lane == r[:, R_E2:R_E2 + 1])).astype(F32)
        cntr_ref[...] += jnp.broadcast_to(jnp.sum(on_lane, axis=0, keepdims=True), cntr_ref.shape)


def _mix(na, sw, x, beta_na, beta_swa, w_out, g_post_mix, gate_a, g_pre_ffn, scale_f, shift_f, w_router3):
    b, s, d = x.shape
    tm = 1024
    nt = s // tm
    row = lambda bi, i: (bi, i, 0)
    const2 = lambda bi, i: (0, 0)
    per_b = lambda bi, i: (bi, 0, 0)
    return pl.pallas_call(
        _mix_kernel,
        out_shape=(jax.ShapeDtypeStruct((b, s, d), F32),
                   jax.ShapeDtypeStruct((b * s * TOKEN_ROWS, LANES), WORD),
                   jax.ShapeDtypeStruct((b * s, LANES), F32),
                   jax.ShapeDtypeStruct((SUBLANES, b * s), F32),
                   jax.ShapeDtypeStruct((N_EXPERTS, LANES), F32),
                   jax.ShapeDtypeStruct((SUBLANES, LANES), F32)),
        grid=(b, s // tm),
        in_specs=[pl.BlockSpec((1, PAIRS, tm, LANES), lambda bi, i: (bi, 0, i, 0)),
                  pl.BlockSpec((1, PAIRS, tm, LANES), lambda bi, i: (bi, 0, i, 0)),
                  pl.BlockSpec((1, tm, d), row),
                  pl.BlockSpec((1, NA_WIDTH), const2),
                  pl.BlockSpec((1, SWA_WIDTH), const2),
                  pl.BlockSpec((NA_WIDTH + SWA_WIDTH, d), const2),
                  pl.BlockSpec((1, d), const2),
                  pl.BlockSpec((1, 1, d), per_b),
                  pl.BlockSpec((1, d), const2),
                  pl.BlockSpec((1, 1, d), per_b),
                  pl.BlockSpec((1, 1, d), per_b),
                  pl.BlockSpec((d, LANES), const2)],
        out_specs=(pl.BlockSpec((1, tm, d), row),
                   pl.BlockSpec((tm * TOKEN_ROWS, LANES), lambda bi, i: (bi * nt + i, 0)),
                   pl.BlockSpec((tm, LANES), lambda bi, i: (bi * nt + i, 0)),
                   pl.BlockSpec((SUBLANES, tm), lambda bi, i: (0, bi * nt + i)),
                   pl.BlockSpec((N_EXPERTS, LANES), const2),
                   pl.BlockSpec((SUBLANES, LANES), const2)),
        compiler_params=_params("arbitrary", "arbitrary"),
        name="mix",
    )(na, sw, x, beta_na.reshape(1, -1), beta_swa.reshape(1, -1), w_out, g_post_mix.reshape(1, d), gate_a,
      g_pre_ffn.reshape(1, d), scale_f, shift_f, w_router3)


MOE_TILE = 1024
MOE_SUBTILE = 256
PLAN_T = 2048
I_TILE, I_EXPERT, I_LO, I_HI, I_FIRST, I_NEW, I_NEXT, I_ORDER = range(8)


def _plan_kernel(rt_ref, cntc_ref, cntr_ref, pos_ref, items_ref, start_ref, carry_ref, *, n_tiles, n_items):
    i = pl.program_id(0)
    sub = lax.broadcasted_iota(jnp.int32, (N_EXPERTS, LANES), 0)
    lane = lax.broadcasted_iota(jnp.int32, (N_EXPERTS, LANES), 1)

    @pl.when(i == 0)
    def _():
        c_col = cntc_ref[:, 0:1]
        c_row = cntr_ref[0:1, :]
        s_col = jnp.sum(jnp.where(lane < sub, c_row, 0.0), axis=1, keepdims=True)
        s_row = jnp.sum(jnp.where(sub < lane, c_col, 0.0), axis=0, keepdims=True)
        start_ref[...] = jnp.broadcast_to(s_col, start_ref.shape)
        carry_ref[...] = jnp.zeros_like(carry_ref)

        def tiles_of(s, c):
            first = jnp.floor(s * (1.0 / MOE_TILE))
            last = jnp.floor((s + c - 1.0) * (1.0 / MOE_TILE))
            return first, jnp.where(c > 0.0, last - first + 1.0, 0.0)

        f_col, n_col = tiles_of(s_col, c_col)
        _, n_row = tiles_of(s_row, c_row)
        i_col = jnp.sum(jnp.where(lane < sub, n_row, 0.0), axis=1, keepdims=True)
        total = jnp.sum(n_col, axis=0, keepdims=True)
        k = lax.broadcasted_iota(jnp.int32, (N_EXPERTS, n_items), 1).astype(F32)
        subk = lax.broadcasted_iota(jnp.int32, (N_EXPERTS, n_items), 0).astype(F32)
        ek = jnp.sum(jnp.where(i_col + n_col <= k, 1.0, 0.0), axis=0, keepdims=True)
        k0 = k[0:1]
        valid = k0 < total
        sel = subk == ek

        def pick(v):
            return jnp.sum(jnp.where(sel, v, 0.0), axis=0, keepdims=True)

        i_k, f_k, s_k, c_k = pick(i_col), pick(f_col), pick(s_col), pick(c_col)
        tile = f_k + (k0 - i_k)
        row0 = tile * MOE_TILE
        lo = jnp.maximum(s_k, row0) - row0
        hi = jnp.minimum(s_k + c_k, row0 + MOE_TILE) - row0
        present = n_col > 0.0
        last_expert = jnp.sum(jnp.where(i_col + n_col <= total - 1.0, 1.0, 0.0), axis=0, keepdims=True)
        nxt = jnp.min(jnp.where(present & (subk > ek), subk, float(N_EXPERTS)), axis=0, keepdims=True)
        order = jnp.sum(jnp.where(present & (subk < ek), 1.0, 0.0), axis=0, keepdims=True)
        rows = [jnp.where(valid, tile, n_tiles - 1.0), jnp.where(valid, ek, last_expert),
                jnp.where(valid, lo, 0.0), jnp.where(valid, hi, 0.0),
                jnp.where(valid & (lo == 0.0), 1.0, 0.0),
                jnp.where(valid & (k0 == i_k), 1.0, 0.0), jnp.where(valid, nxt, float(N_EXPERTS)),
                jnp.where(valid, order, 0.0)]
        assert len(rows) == SUBLANES
        items_ref[...] = jnp.concatenate(rows, axis=0).astype(jnp.int32)

    t = rt_ref.shape[1]
    e1 = rt_ref[R_E1:R_E1 + 1, :]
    e2 = rt_ref[R_E2:R_E2 + 1, :]
    sub_t = lax.broadcasted_iota(jnp.int32, (N_EXPERTS, t), 0).astype(F32)
    oh1 = sub_t == e1
    oh2 = sub_t == e2
    oh = (oh1 | oh2).astype(F32)
    before = (lax.broadcasted_iota(jnp.int32, (t, t), 0) < lax.broadcasted_iota(jnp.int32, (t, t), 1)).astype(BF16)
    rank = jnp.dot(oh.astype(BF16), before, preferred_element_type=F32)
    base = start_ref[:, 0:1] + carry_ref[:, 0:1] + rank
    pos1 = jnp.sum(jnp.where(oh1, base, 0.0), axis=0, keepdims=True)
    pos2 = jnp.sum(jnp.where(oh2, base, 0.0), axis=0, keepdims=True)
    carry_ref[...] += jnp.broadcast_to(jnp.sum(oh, axis=1, keepdims=True), carry_ref.shape)
    pos = jnp.concatenate([pos1, pos2] + [jnp.zeros_like(pos1)] * (SUBLANES - 2), axis=0)
    pos = pos.astype(jnp.int32)
    for g in range(t // LANES):
        pos_ref[g] = pos[:, g * LANES:(g + 1) * LANES]


def _plan(rt, cntc, cntr):
    n = rt.shape[1]
    n_tiles = 2 * n // MOE_TILE
    n_items = 2 * LANES
    assert n_tiles + N_EXPERTS <= n_items
    return pl.pallas_call(
        functools.partial(_plan_kernel, n_tiles=n_tiles, n_items=n_items),
        out_shape=(jax.ShapeDtypeStruct((n // LANES, SUBLANES, LANES), jnp.int32),
                   jax.ShapeDtypeStruct((SUBLANES, n_items), jnp.int32)),
        grid=(n // PLAN_T,),
        in_specs=[pl.BlockSpec((SUBLANES, PLAN_T), lambda i: (0, i)),
                  pl.BlockSpec((N_EXPERTS, LANES), lambda i: (0, 0)),
                  pl.BlockSpec((SUBLANES, LANES), lambda i: (0, 0))],
        out_specs=(pl.BlockSpec((PLAN_T // LANES, SUBLANES, LANES), lambda i: (i, 0, 0)),
                   pl.BlockSpec((SUBLANES, n_items), lambda i: (0, 0))),
        scratch_shapes=[pltpu.VMEM((N_EXPERTS, LANES), F32), pltpu.VMEM((N_EXPERTS, LANES), F32)],
        compiler_params=_params("arbitrary"),
        name="plan",
    )(rt, cntc, cntr)


DISPATCH_T = 2048


def _token_rows(ref, index):
    return ref.at[pl.ds(pl.multiple_of(index * TOKEN_ROWS, TOKEN_ROWS), TOKEN_ROWS), :]


def _dispatch_kernel(pos_ref, h_ref, xs_hbm, sem):
    for g in range(DISPATCH_T // LANES):
        def body(j, carry):
            src = _token_rows(h_ref, g * LANES + j)
            for k in range(2):
                pltpu.make_async_copy(src, _token_rows(xs_hbm, pos_ref[g, k, j]), sem.at[0]).start(priority=k)
            return carry

        lax.fori_loop(0, LANES, body, 0, unroll=8)
    for k in range(2):
        pltpu.make_async_copy(h_ref, xs_hbm.at[pl.ds(0, DISPATCH_T * TOKEN_ROWS), :], sem.at[0]).wait()


def _dispatch(pos, h2):
    n = pos.shape[0] * LANES
    return pl.pallas_call(
        _dispatch_kernel,
        out_shape=jax.ShapeDtypeStruct((2 * n * TOKEN_ROWS, LANES), WORD),
        grid=(n // DISPATCH_T,),
        in_specs=[pl.BlockSpec((DISPATCH_T // LANES, SUBLANES, LANES), lambda i: (i, 0, 0), memory_space=pltpu.SMEM),
                  pl.BlockSpec((DISPATCH_T * TOKEN_ROWS, LANES), lambda i: (i, 0))],
        out_specs=pl.BlockSpec(memory_space=pl.ANY),
        scratch_shapes=[pltpu.SemaphoreType.DMA((1,))],
        compiler_params=_params("arbitrary"),
        name="dispatch",
    )(pos, h2)


def _expert_kernel(tile_ref, exp_ref, lo_ref, hi_ref, first_ref, new_ref, next_ref, order_ref,
                   xs_ref, wg_hbm, wu_hbm, wd_hbm, o_ref, wg_buf, wu_buf, wd_buf, sem):
    k = pl.program_id(0)
    lo = lo_ref[k]
    hi = hi_ref[k]
    slot = order_ref[k] % 2

    def weight_copies(expert, sl):
        return [pltpu.make_async_copy(src.at[expert], dst.at[sl], sem.at[sl])
                for src, dst in ((wg_hbm, wg_buf), (wu_hbm, wu_buf), (wd_hbm, wd_buf))]

    @pl.when(k == 0)
    def _():
        for copy in weight_copies(exp_ref[0], 0):
            copy.start()

    @pl.when(new_ref[k] == 1)
    def _():
        for copy in weight_copies(exp_ref[k], slot):
            copy.wait()

        @pl.when(next_ref[k] < N_EXPERTS)
        def _():
            for copy in weight_copies(next_ref[k], 1 - slot):
                copy.start()

    @pl.when(first_ref[k] == 1)
    def _():
        o_ref[...] = jnp.zeros_like(o_ref)

    for part in range(MOE_TILE // MOE_SUBTILE):
        first_row = part * MOE_SUBTILE
        base = first_row * TOKEN_ROWS

        @pl.when((hi > first_row) & (lo < first_row + MOE_SUBTILE))
        def _():
            x = _from_token_tiles(xs_ref, MOE_SUBTILE, base=base).astype(BF16)
            gate = jnp.dot(x, wg_buf[slot].astype(BF16), preferred_element_type=F32)
            up = jnp.dot(x, wu_buf[slot].astype(BF16), preferred_element_type=F32)
            he = (gate * jax.nn.sigmoid(gate) * up).astype(BF16)
            ye = jnp.dot(he, wd_buf[slot].astype(BF16), preferred_element_type=F32)
            row = first_row + lax.broadcasted_iota(jnp.int32, (MOE_SUBTILE, 1), 0)
            mine = (row >= lo) & (row < hi)
            _to_token_tiles(o_ref, jnp.where(mine, ye, _from_token_tiles(o_ref, MOE_SUBTILE, base=base)), base=base)


def _experts(items, xs, w_gate, w_up, w_down):
    n_rows = xs.shape[0] // TOKEN_ROWS
    n_items = n_rows // MOE_TILE + N_EXPERTS
    d = w_gate.shape[1]
    tile_map = lambda k, tile, *_: (tile[k], 0)
    grid_spec = pltpu.PrefetchScalarGridSpec(
        num_scalar_prefetch=SUBLANES,
        grid=(n_items,),
        in_specs=[pl.BlockSpec((MOE_TILE * TOKEN_ROWS, LANES), tile_map),
                  pl.BlockSpec(memory_space=pl.ANY),
                  pl.BlockSpec(memory_space=pl.ANY),
                  pl.BlockSpec(memory_space=pl.ANY)],
        out_specs=pl.BlockSpec((MOE_TILE * TOKEN_ROWS, LANES), tile_map),
        scratch_shapes=[pltpu.VMEM((2, d, EXPERT_FF), F32), pltpu.VMEM((2, d, EXPERT_FF), F32),
                        pltpu.VMEM((2, EXPERT_FF, d), F32), pltpu.SemaphoreType.DMA((2,))])
    return pl.pallas_call(
        _expert_kernel,
        out_shape=jax.ShapeDtypeStruct(xs.shape, WORD),
        grid_spec=grid_spec,
        compiler_params=_params("arbitrary"),
        name="experts",
    )(*(items[j, :n_items] for j in range(SUBLANES)), xs, w_gate, w_up, w_down)


COMBINE_T = 512


def _combine_kernel(pos_ref, posn_ref, ys_hbm, r_ref, x1_ref, gf_ref, gpost_ref, o_ref, buf, sem):
    i = pl.program_id(0)
    n = pl.num_programs(0)
    slot = i % 2
    slot_rows = COMBINE_T * TOKEN_ROWS

    def start_row(p_ref, sl, g, j):
        r = g * LANES + j
        for k in range(2):
            dst = buf.at[sl, pl.ds(pl.multiple_of(k * slot_rows + r * TOKEN_ROWS, TOKEN_ROWS), TOKEN_ROWS), :]
            pltpu.make_async_copy(_token_rows(ys_hbm, p_ref[g, k, j]), dst, sem.at[sl]).start(priority=k)

    def wait_slot(sl):
        pltpu.make_async_copy(ys_hbm.at[pl.ds(0, 2 * slot_rows), :], buf.at[sl], sem.at[sl]).wait()

    def issue(p_ref, sl):
        for g in range(COMBINE_T // LANES):
            def body(j, carry):
                start_row(p_ref, sl, g, j)
                return carry

            lax.fori_loop(0, LANES, body, 0, unroll=8)

    @pl.when(i == 0)
    def _():
        issue(pos_ref, 0)

    @pl.when(i + 1 < n)
    def _():
        issue(posn_ref, 1 - slot)

    wait_slot(slot)
    ya = _from_token_tiles(buf.at[slot], COMBINE_T)
    yb = _from_token_tiles(buf.at[slot], COMBINE_T, base=slot_rows)
    r = r_ref[...]
    y = r[:, R_W1:R_W1 + 1] * ya + r[:, R_W2:R_W2 + 1] * yb
    o_ref[0] = x1_ref[0] + gf_ref[0] * (_rms(y) * gpost_ref[...])


def _combine(pos, ys, r, x1, gate_f, g_post_ffn):
    b, s, d = x1.shape
    nt = s // COMBINE_T
    n_steps = b * nt
    row = lambda i: (i // nt, i % nt, 0)
    pos_spec = lambda f: pl.BlockSpec((COMBINE_T // LANES, SUBLANES, LANES), lambda i: (f(i), 0, 0),
                                      memory_space=pltpu.SMEM)
    return pl.pallas_call(
        _combine_kernel,
        out_shape=jax.ShapeDtypeStruct((b, s, d), F32),
        grid=(n_steps,),
        in_specs=[pos_spec(lambda i: i),
                  pos_spec(lambda i: jnp.minimum(i + 1, n_steps - 1)),
                  pl.BlockSpec(memory_space=pl.ANY),
                  pl.BlockSpec((COMBINE_T, LANES), lambda i: (i, 0)),
                  pl.BlockSpec((1, COMBINE_T, d), row),
                  pl.BlockSpec((1, 1, d), lambda i: (i // nt, 0, 0)),
                  pl.BlockSpec((1, d), lambda i: (0, 0))],
        out_specs=pl.BlockSpec((1, COMBINE_T, d), row),
        scratch_shapes=[pltpu.VMEM((2, 2 * COMBINE_T * TOKEN_ROWS, LANES), WORD), pltpu.SemaphoreType.DMA((2,))],
        compiler_params=_params("arbitrary"),
        name="combine",
    )(pos, pos, ys, r, x1, gate_f, g_post_ffn.reshape(1, d))


def _rope_tables(s):
    half = HEAD_DIM // 2
    inv = ROPE_THETA ** (-np.arange(half, dtype=np.float64) * 2.0 / HEAD_DIM)
    ang = np.arange(s, dtype=np.float64)[:, None] * inv[None, :]
    cos = np.cos(ang)
    sin = np.sin(ang)
    cos = np.concatenate([cos, cos, cos, cos], axis=-1)
    sin_signed = np.concatenate([-sin, sin, -sin, sin], axis=-1)
    return jnp.asarray(cos, F32), jnp.asarray(sin_signed, F32)


def _router_weights(w_group_router, w_expert_router):
    d = w_group_router.shape[0]
    we = jnp.transpose(w_expert_router, (1, 0, 2)).reshape(d, N_EXPERTS)
    hi, lo = _split2(jnp.concatenate([w_group_router, we], axis=-1))
    return jnp.concatenate([hi, lo, jnp.zeros((d, LANES - 2 * ROUTER_COLS), BF16)], axis=-1)


def kernel(x, c, w_ada, b_ada, g_pre_mix, w_in, na_rpb, swa_sinks, beta_na, beta_swa, w_out, g_post_mix, g_pre_ffn,
           w_group_router, w_expert_router, w_gate, w_up, w_down, g_post_ffn):
    b, s, d = x.shape
    depth = w_ada.shape[0]
    cos, sin_signed = _rope_tables(s)
    for l in range(depth):
        mod = _adaln(c, w_ada[l], b_ada[l]).reshape(b, N_MOD, 1, d)
        shift_a, scale_a, gate_a, shift_f, scale_f, gate_f = (mod[:, k] for k in range(N_MOD))
        qkv, na_kg, na_vg = _qkv(x, g_pre_mix[l], scale_a, shift_a, w_in[l].astype(BF16), cos, sin_signed)
        na = _na(qkv, na_kg, na_vg, na_rpb[l])
        sw = _swa(qkv, swa_sinks[l])
        x1, h2, r, rt, cntc, cntr = _mix(na, sw, x, beta_na[l], beta_swa[l], w_out[l].astype(BF16), g_post_mix[l],
                                          gate_a, g_pre_ffn[l], scale_f, shift_f,
                                          _router_weights(w_group_router[l], w_expert_router[l]))
        pos, items = _plan(rt, cntc, cntr)
        xs = _dispatch(pos, h2)
        ys = _experts(items, xs, w_gate[l], w_up[l], w_down[l])
        x = _combine(pos, ys, r, x1, gate_f, g_post_ffn[l])
    return x
```
